```python
import jax
import jax.numpy as jnp
from jax import lax
import numpy as np

D_MODEL = 1024
BATCH = 8
SEQ = 4096
DEPTH = 2

MEM_LEN = 256
EPS = 1e-6
N_EVEN = (DEPTH + 1) // 2
N_ODD = DEPTH // 2

HEAD_DIM = 64
SWA_Q_HEADS = (D_MODEL // 2) // HEAD_DIM
SWA_KV_HEADS = SWA_Q_HEADS // 4
WINDOW = 128
SWA_Q_COLS = SWA_Q_HEADS * HEAD_DIM
SWA_KV_COLS = SWA_KV_HEADS * HEAD_DIM
SWA_COLS = SWA_Q_COLS + 2 * SWA_KV_COLS

RWKV_HEADS = (D_MODEL // 2) // HEAD_DIM
RWKV_WIDTH = RWKV_HEADS * HEAD_DIM
RWKV_DECAY_RANK = 64
RWKV_A_RANK = 64
RWKV_GATE_RANK = 128
RWKV_COLS = 3 * RWKV_WIDTH + RWKV_DECAY_RANK + RWKV_A_RANK + RWKV_GATE_RANK
RWKV_LN_EPS = 64e-5
EVEN_IN_COLS = SWA_COLS + RWKV_COLS
EVEN_MIX_WIDTH = SWA_Q_COLS + RWKV_WIDTH

GLA_HEADS = 4
GLA_KEY_WIDTH = D_MODEL // 2
GLA_VAL_WIDTH = D_MODEL
GLA_DK = GLA_KEY_WIDTH // GLA_HEADS
GLA_DV = GLA_VAL_WIDTH // GLA_HEADS
GLA_GATE_RANK = 16
GLA_GATE_NORM = 16.0
GLA_CHUNK = 64
ODD_IN_COLS = 2 * GLA_KEY_WIDTH + 2 * GLA_VAL_WIDTH + GLA_GATE_RANK

XA_HEADS = 4
XA_HEAD_DIM = 128
XA_WIDTH = XA_HEADS * XA_HEAD_DIM

MOE_GROUPS = 4
MOE_EXPERTS_PER_GROUP = 8
MOE_EXPERTS = MOE_GROUPS * MOE_EXPERTS_PER_GROUP
MOE_TOPK = 2
MOE_FF = 512
MOE_BLOCK = 128

kernel_name = 'hybrid_swa_rwkv7_gla_hmoe_trunk'


def rmsnorm(x, g, eps=EPS):
    xf = x.astype(jnp.float32)
    y = xf * lax.rsqrt(jnp.mean(xf * xf, axis=-1, keepdims=True) + eps)
    return (y * g.astype(jnp.float32)).astype(x.dtype)


def swa_with_sinks(q, k, v, sinks):
    B, S, HQ, Dh = q.shape
    HKV = k.shape[2]
    G = HQ // HKV
    W = WINDOW
    nb = S // W
    qb = q.astype(jnp.float32).reshape(B, nb, W, HKV, G, Dh)

    def band(t):
        tp = jnp.pad(t.astype(jnp.float32), ((0, 0), (W, 0), (0, 0), (0, 0))).reshape(B, nb + 1, W, HKV, Dh)
        return jnp.concatenate([tp[:, :-1], tp[:, 1:]], axis=2)

    kb, vb = band(k), band(v)
    s = jnp.einsum('bnqhgd,bnkhd->bnhgqk', qb, kb) * (Dh ** -0.5)
    qpos = jnp.arange(W)[:, None] + W
    kpos = jnp.arange(2 * W)[None, :]
    rel = qpos - kpos
    gkey = jnp.arange(nb)[:, None, None] * W + kpos[None] - W
    valid = (rel >= 0)[None] & (rel < W)[None] & (gkey >= 0)
    s = jnp.where(valid[None, :, None, None], s, -jnp.inf)
    sink = sinks.astype(jnp.float32).reshape(1, 1, HKV, G, 1, 1)
    m = jnp.maximum(jnp.max(s, axis=-1, keepdims=True), sink)
    pr = jnp.exp(s - m)
    den = jnp.sum(pr, axis=-1, keepdims=True) + jnp.exp(sink - m)
    o = jnp.einsum('bnhgqk,bnkhd->bnqhgd', pr / den, vb)
    return o.reshape(B, S, HQ * Dh)


def rwkv7_scan(r, decay, k, v, a_vec, b_vec):
    B, S, H, N = r.shape

    def step(state, inp):
        r_t, w_t, k_t, v_t, a_t, b_t = inp
        sa = jnp.einsum('bhvk,bhk->bhv', state, a_t)
        state = state * w_t[:, :, None, :] + sa[..., None] * b_t[:, :, None, :] + v_t[..., None] * k_t[:, :, None, :]
        y = jnp.einsum('bhvk,bhk->bhv', state, r_t)
        return state, y

    xs = tuple(jnp.moveaxis(t, 1, 0) for t in (r, decay, k, v, a_vec, b_vec))
    _, ys = lax.scan(step, jnp.zeros((B, H, N, N), jnp.float32), xs)
    return jnp.moveaxis(ys, 0, 1)


def rwkv7_time_mix(p, mu, w0, w2, a0, a2, g2, k_k, k_a, r_k, lnx_w, lnx_b):
    B, S, _ = p.shape
    C = RWKV_WIDTH
    p = p.astype(jnp.float32)
    p_prev = jnp.pad(p, ((0, 0), (1, 0), (0, 0)))[:, :-1]
    p = p + (p_prev - p) * mu
    r = p[..., :C]
    k = p[..., C:2 * C]
    v = p[..., 2 * C:3 * C]
    o = 3 * C
    xw = p[..., o:o + RWKV_DECAY_RANK]
    o += RWKV_DECAY_RANK
    xa = p[..., o:o + RWKV_A_RANK]
    o += RWKV_A_RANK
    xg = p[..., o:]
    w = -jax.nn.softplus(-(w0 + jnp.tanh(xw) @ w2)) - 0.5
    decay = jnp.exp(-jnp.exp(w))
    a = jax.nn.sigmoid(a0 + xa @ a2)
    g = jax.nn.sigmoid(xg) @ g2
    kk = (k * k_k).reshape(B, S, RWKV_HEADS, HEAD_DIM)
    kk = kk / jnp.maximum(jnp.sqrt(jnp.sum(kk * kk, axis=-1, keepdims=True)), 1e-12)
    k = k * (1.0 + (a - 1.0) * k_a)

    def heads(t):
        return t.reshape(B, S, RWKV_HEADS, HEAD_DIM)

    r, k, v, decay, a = heads(r), heads(k), heads(v), heads(decay), heads(a)
    y = rwkv7_scan(r, decay, k, v, -kk, kk * a)
    mean = jnp.mean(y, axis=-1, keepdims=True)
    var = jnp.mean(jnp.square(y - mean), axis=-1, keepdims=True)
    y = ((y - mean) * lax.rsqrt(var + RWKV_LN_EPS)).reshape(B, S, C) * lnx_w + lnx_b
    bonus = jnp.sum(r * k * r_k, axis=-1, keepdims=True) * v
    y = y + bonus.reshape(B, S, C)
    return y * g


def even_mixer(h, w_in, sinks, mu, w0, w2, a0, a2, g2, k_k, k_a, r_k, lnx_w, lnx_b, w_out):
    B, S, _ = h.shape
    p = h @ w_in
    q = p[..., :SWA_Q_COLS].reshape(B, S, SWA_Q_HEADS, HEAD_DIM)
    k = p[..., SWA_Q_COLS:SWA_Q_COLS + SWA_KV_COLS].reshape(B, S, SWA_KV_HEADS, HEAD_DIM)
    v = p[..., SWA_Q_COLS + SWA_KV_COLS:SWA_COLS].reshape(B, S, SWA_KV_HEADS, HEAD_DIM)
    o_a = swa_with_sinks(q, k, v, sinks).astype(h.dtype)
    o_b = rwkv7_time_mix(p[..., SWA_COLS:], mu, w0, w2, a0, a2, g2, k_k, k_a, r_k, lnx_w, lnx_b).astype(h.dtype)
    return jnp.concatenate([o_a, o_b], axis=-1) @ w_out


def gla_chunked(q, k, v, gk):
    B, S, _ = q.shape
    H, C = GLA_HEADS, GLA_CHUNK
    nc = S // C

    def chunks(t, d):
        return t.reshape(B, nc, C, H, d).transpose(1, 0, 3, 2, 4)

    qc, kc, gc = chunks(q, GLA_DK), chunks(k, GLA_DK), chunks(gk, GLA_DK)
    vc = chunks(v, GLA_DV)
    causal = jnp.tril(jnp.ones((C, C), dtype=bool))

    def step(state, inp):
        qi, ki, vi, gi = inp
        b = jnp.cumsum(gi, axis=2)
        o_inter = jnp.einsum('bhcd,bhde->bhce', qi * jnp.exp(b), state)
        diff = b[:, :, :, None, :] - b[:, :, None, :, :]
        diff = jnp.where(causal[:, :, None], diff, -jnp.inf)
        att = jnp.einsum('bhid,bhjd,bhijd->bhij', qi, ki, jnp.exp(diff))
        o_intra = jnp.einsum('bhij,bhje->bhie', att, vi)
        b_last = b[:, :, -1:, :]
        state = state * jnp.exp(b_last)[:, :, 0, :, None] + jnp.einsum('bhcd,bhce->bhde', ki * jnp.exp(b_last - b), vi)
        return state, o_inter + o_intra

    _, oc = lax.scan(step, jnp.zeros((B, H, GLA_DK, GLA_DV), jnp.float32), (qc, kc, vc, gc))
    return oc.transpose(1, 0, 3, 2, 4).reshape(B, S, H, GLA_DV)


def odd_mixer(h, w_in, gate_up, gate_b, onorm, w_out):
    B, S, _ = h.shape
    KW, VW, R = GLA_KEY_WIDTH, GLA_VAL_WIDTH, GLA_GATE_RANK
    p = h @ w_in
    q = p[..., :KW].astype(jnp.float32) * (GLA_DK ** -0.5)
    k = p[..., KW:2 * KW].astype(jnp.float32)
    v = p[..., 2 * KW:2 * KW + VW].astype(jnp.float32)
    gd = p[..., 2 * KW + VW:2 * KW + VW + R]
    og = p[..., 2 * KW + VW + R:].astype(jnp.float32)
    gk = jax.nn.log_sigmoid((gd @ gate_up + gate_b).astype(jnp.float32)) / GLA_GATE_NORM
    o = gla_chunked(q, k, v, gk)
    o = rmsnorm(o, onorm).reshape(B, S, VW) * jax.nn.silu(og)
    return o.astype(h.dtype) @ w_out


def memory_kv(mem, mem_norm, wk, wv):
    B, M, _ = mem.shape
    m = rmsnorm(mem, mem_norm)
    mk = (m @ wk).reshape(B, M, XA_HEADS, XA_HEAD_DIM)
    mv = (m @ wv).reshape(B, M, XA_HEADS, XA_HEAD_DIM)
    return mk, mv


def cross_attention(h, mk, mv, wq, wo):
    B, S, _ = h.shape
    q = (h @ wq).reshape(B, S, XA_HEADS, XA_HEAD_DIM).astype(jnp.float32)
    s = jnp.einsum('bshd,bmhd->bhsm', q, mk.astype(jnp.float32)) * (XA_HEAD_DIM ** -0.5)
    pr = jax.nn.softmax(s, axis=-1)
    o = jnp.einsum('bhsm,bmhd->bshd', pr, mv.astype(jnp.float32)).reshape(B, S, XA_WIDTH)
    return o.astype(h.dtype) @ wo


def hierarchical_moe(h, w_group, b_group, w_expert, b_expert, w1, w3, w2):
    B, S, D = h.shape
    T = B * S
    hf = h.reshape(T, D)
    g_logits = (hf @ w_group).astype(jnp.float32) + b_group.astype(jnp.float32)
    g_prob = jax.nn.softmax(g_logits, axis=-1)
    _, g_top = lax.top_k(g_logits, 1)
    p_group = jnp.take_along_axis(g_prob, g_top, axis=-1)
    e_logits = ((hf @ w_expert).astype(jnp.float32) + b_expert.astype(jnp.float32)).reshape(T, MOE_GROUPS, MOE_EXPERTS_PER_GROUP)
    idx = jnp.broadcast_to(g_top[:, :, None], (T, 1, MOE_EXPERTS_PER_GROUP))
    e_in = jnp.take_along_axis(e_logits, idx, axis=1)[:, 0]
    e_prob = jax.nn.softmax(e_in, axis=-1)
    top_p, top_i = lax.top_k(e_prob, MOE_TOPK)
    gate = p_group * top_p / jnp.sum(top_p, axis=-1, keepdims=True)
    expert = g_top * MOE_EXPERTS_PER_GROUP + top_i

    A = T * MOE_TOPK
    P = A + MOE_EXPERTS * MOE_BLOCK
    NB = P // MOE_BLOCK
    e_flat = expert.reshape(A)
    order = jnp.argsort(e_flat)
    e_s = e_flat[order]
    tok_s = (order // MOE_TOPK).astype(jnp.int32)
    w_s = gate.reshape(A)[order]
    counts = jnp.bincount(e_flat, length=MOE_EXPERTS)
    starts = jnp.cumsum(counts) - counts
    padded = (counts + MOE_BLOCK - 1) // MOE_BLOCK * MOE_BLOCK
    pends = jnp.cumsum(padded)
    pstarts = pends - padded
    dest = pstarts[e_s] + jnp.arange(A) - starts[e_s]
    buf_tok = jnp.zeros((P,), jnp.int32).at[dest].set(tok_s)
    buf_w = jnp.zeros((P,), jnp.float32).at[dest].set(w_s)
    block_e = jnp.minimum(jnp.searchsorted(pends, jnp.arange(NB) * MOE_BLOCK, side='right'), MOE_EXPERTS - 1)
    xb = hf[buf_tok].reshape(NB, MOE_BLOCK, D)

    def expert_block(args):
        xe, e = args
        return (jax.nn.silu(xe @ w1[e]) * (xe @ w3[e])) @ w2[e]

    yb = lax.map(expert_block, (xb, block_e)).reshape(P, D)
    y = jnp.zeros((T, D), jnp.float32).at[buf_tok].add(yb.astype(jnp.float32) * buf_w[:, None])
    return y.astype(h.dtype).reshape(B, S, D)


def setup_inputs(seed: int = 0) -> dict:
    key = jax.random.key(seed)
    ks = iter(jax.random.split(key, 64))
    D = D_MODEL

    def nrm(shape, scale):
        return jax.random.normal(next(ks), shape, jnp.float32) * scale

    def unif(shape, lo, hi):
        return jax.random.uniform(next(ks), shape, jnp.float32, lo, hi)

    def gain(shape):
        return 1.0 + nrm(shape, 0.02)

    return {
        'x': nrm((BATCH, SEQ, D), 1.0),
        'mem': nrm((BATCH, MEM_LEN, D), 1.0),
        'norm_mix': gain((DEPTH, D)),
        'norm_xattn': gain((DEPTH, D)),
        'norm_moe': gain((DEPTH, D)),
        'norm_final': gain((D,)),
        'ev_w_in': nrm((N_EVEN, D, EVEN_IN_COLS), D ** -0.5),
        'ev_sinks': nrm((N_EVEN, SWA_Q_HEADS), 0.5),
        'ev_mu': unif((N_EVEN, RWKV_COLS), 0.0, 1.0),
        'ev_w0': unif((N_EVEN, RWKV_WIDTH), -4.0, 0.0),
        'ev_w2': nrm((N_EVEN, RWKV_DECAY_RANK, RWKV_WIDTH), 0.1),
        'ev_a0': nrm((N_EVEN, RWKV_WIDTH), 0.1),
        'ev_a2': nrm((N_EVEN, RWKV_A_RANK, RWKV_WIDTH), 0.1),
        'ev_g2': nrm((N_EVEN, RWKV_GATE_RANK, RWKV_WIDTH), RWKV_GATE_RANK ** -0.5),
        'ev_k_k': 0.85 + nrm((N_EVEN, RWKV_WIDTH), 0.02),
        'ev_k_a': gain((N_EVEN, RWKV_WIDTH)),
        'ev_r_k': nrm((N_EVEN, RWKV_HEADS, HEAD_DIM), 0.1),
        'ev_lnx_w': gain((N_EVEN, RWKV_WIDTH)),
        'ev_lnx_b': nrm((N_EVEN, RWKV_WIDTH), 0.02),
        'ev_w_out': nrm((N_EVEN, EVEN_MIX_WIDTH, D), EVEN_MIX_WIDTH ** -0.5),
        'od_w_in': nrm((N_ODD, D, ODD_IN_COLS), D ** -0.5),
        'od_gate_up': nrm((N_ODD, GLA_GATE_RANK, GLA_KEY_WIDTH), GLA_GATE_RANK ** -0.5),
        'od_gate_b': nrm((N_ODD, GLA_KEY_WIDTH), 0.1),
        'od_onorm': gain((N_ODD, GLA_DV)),
        'od_w_out': nrm((N_ODD, GLA_VAL_WIDTH, D), GLA_VAL_WIDTH ** -0.5),
        'mem_norm': gain((D,)),
        'mem_wk': nrm((D, XA_WIDTH), D ** -0.5),
        'mem_wv': nrm((D, XA_WIDTH), D ** -0.5),
        'xa_wq': nrm((DEPTH, D, XA_WIDTH), D ** -0.5),
        'xa_wo': nrm((DEPTH, XA_WIDTH, D), XA_WIDTH ** -0.5),
        'moe_w_group': nrm((DEPTH, D, MOE_GROUPS), D ** -0.5),
        'moe_b_group': nrm((DEPTH, MOE_GROUPS), 0.01),
        'moe_w_expert': nrm((DEPTH, D, MOE_EXPERTS), D ** -0.5),
        'moe_b_expert': nrm((DEPTH, MOE_EXPERTS), 0.01),
        'moe_w1': nrm((DEPTH, MOE_EXPERTS, D, MOE_FF), D ** -0.5),
        'moe_w3': nrm((DEPTH, MOE_EXPERTS, D, MOE_FF), D ** -0.5),
        'moe_w2': nrm((DEPTH, MOE_EXPERTS, MOE_FF, D), MOE_FF ** -0.5),
    }


def reference(x, mem, norm_mix, norm_xattn, norm_moe, norm_final,
              ev_w_in, ev_sinks, ev_mu, ev_w0, ev_w2, ev_a0, ev_a2, ev_g2, ev_k_k, ev_k_a, ev_r_k,
              ev_lnx_w, ev_lnx_b, ev_w_out,
              od_w_in, od_gate_up, od_gate_b, od_onorm, od_w_out,
              mem_norm, mem_wk, mem_wv, xa_wq, xa_wo,
              moe_w_group, moe_b_group, moe_w_expert, moe_b_expert, moe_w1, moe_w3, moe_w2):
    mk, mv = memory_kv(mem, mem_norm, mem_wk, mem_wv)
    for layer in range(DEPTH):
        i = layer // 2
        h = rmsnorm(x, norm_mix[layer])
        if layer % 2 == 0:
            mix = even_mixer(h, ev_w_in[i], ev_sinks[i], ev_mu[i], ev_w0[i], ev_w2[i], ev_a0[i], ev_a2[i],
                             ev_g2[i], ev_k_k[i], ev_k_a[i], ev_r_k[i], ev_lnx_w[i], ev_lnx_b[i], ev_w_out[i])
        else:
            mix = odd_mixer(h, od_w_in[i], od_gate_up[i], od_gate_b[i], od_onorm[i], od_w_out[i])
        x = x + mix
        x = x + cross_attention(rmsnorm(x, norm_xattn[layer]), mk, mv, xa_wq[layer], xa_wo[layer])
        x = x + hierarchical_moe(rmsnorm(x, norm_moe[layer]), moe_w_group[layer], moe_b_group[layer],
                                 moe_w_expert[layer], moe_b_expert[layer], moe_w1[layer], moe_w3[layer], moe_w2[layer])
    return rmsnorm(x, norm_final)
```

```python
import functools

import jax
import jax.numpy as jnp
from jax import lax
from jax.experimental import pallas as pl
from jax.experimental.pallas import tpu as pltpu

F32 = jnp.float32
BF16 = jnp.bfloat16
I32 = jnp.int32

EPS = 1e-6
HEAD_DIM = 64
SWA_WINDOW = 128
SWA_Q_HEADS = 8
SWA_GROUP = 4
RWKV_HEADS = 8
RWKV_WIDTH = 512
RWKV_LN_EPS = 64e-5
RWKV_CHUNK = 64
GLA_HEADS = 4
GLA_DK = 128
GLA_DV = 256
GLA_CHUNK = 64
GLA_GATE_NORM = 16.0
XA_HEADS = 4
XA_HEAD_DIM = 128
MOE_GROUPS = 4
MOE_EXPERTS_PER_GROUP = 8
MOE_EXPERTS = 32
MOE_BLOCK = 128
LANES = 128

VMEM_LIMIT_BYTES = 48 * 1024 * 1024


def _cparams(n_axes):
    return pltpu.CompilerParams(dimension_semantics=("arbitrary",) * n_axes,
                                vmem_limit_bytes=VMEM_LIMIT_BYTES)


def _dot(a, b):
    return jnp.dot(a.astype(BF16), b.astype(BF16), preferred_element_type=F32)


def _dot_nt(a, b):
    return lax.dot_general(a.astype(BF16), b.astype(BF16), (((1,), (1,)), ((), ())),
                           preferred_element_type=F32)


def _dot_tn(a, b):
    return lax.dot_general(a.astype(BF16), b.astype(BF16), (((0,), (0,)), ((), ())),
                           preferred_element_type=F32)


def _dot_f32(a, b):
    return jnp.dot(a, b, preferred_element_type=F32, precision=lax.Precision.HIGHEST)


def _rms(x, g):
    ms = jnp.mean(x * x, axis=-1, keepdims=True)
    return x * lax.rsqrt(ms + EPS) * g


def _sigmoid(x):
    return 1.0 / (1.0 + jnp.exp(-x))


def _softplus(x):
    return jnp.maximum(x, 0.0) + jnp.log(1.0 + jnp.exp(-jnp.abs(x)))


def _norm_matmul_kernel(x_ref, g_ref, w_ref, *o_refs, splits):
    h = _rms(x_ref[...], g_ref[...]).astype(BF16)
    off = 0
    for o_ref, n in zip(o_refs, splits):
        o_ref[...] = jnp.dot(h, w_ref[:, off:off + n], preferred_element_type=F32).astype(o_ref.dtype)
        off += n


def _norm_matmul(x, g, w, splits, out_dtypes, tm=256):
    T, D = x.shape
    N = w.shape[1]
    assert sum(splits) == N and T % tm == 0
    return pl.pallas_call(
        functools.partial(_norm_matmul_kernel, splits=tuple(splits)),
        grid=(T // tm,),
        in_specs=[pl.BlockSpec((tm, D), lambda i: (i, 0)),
                  pl.BlockSpec((1, D), lambda i: (0, 0)),
                  pl.BlockSpec((D, N), lambda i: (0, 0))],
        out_specs=[pl.BlockSpec((tm, n), lambda i: (i, 0)) for n in splits],
        out_shape=[jax.ShapeDtypeStruct((T, n), dt) for n, dt in zip(splits, out_dtypes)],
        compiler_params=_cparams(1),
        name="norm_matmul",
    )(x, g.reshape(1, D), w)


def _proj_residual_kernel(*refs, n_in):
    x_ref = refs[0]
    a_refs = refs[1:1 + n_in]
    w_refs = refs[1 + n_in:1 + 2 * n_in]
    o_ref = refs[1 + 2 * n_in]
    acc = x_ref[...]
    for a_ref, w_ref in zip(a_refs, w_refs):
        acc = acc + jnp.dot(a_ref[...], w_ref[...], preferred_element_type=F32)
    o_ref[...] = acc


def _proj_residual(x, acts, weights, tm=512):
    T, D = x.shape
    n_in = len(acts)
    in_specs = [pl.BlockSpec((tm, D), lambda i: (i, 0))]
    in_specs += [pl.BlockSpec((tm, a.shape[1]), lambda i: (i, 0)) for a in acts]
    in_specs += [pl.BlockSpec(w.shape, lambda i: (0, 0)) for w in weights]
    return pl.pallas_call(
        functools.partial(_proj_residual_kernel, n_in=n_in),
        grid=(T // tm,),
        in_specs=in_specs,
        out_specs=pl.BlockSpec((tm, D), lambda i: (i, 0)),
        out_shape=jax.ShapeDtypeStruct((T, D), F32),
        compiler_params=_cparams(1),
        name="proj_residual",
    )(x, *acts, *weights)


def _swa_kernel(sinks_ref, q_ref, kp_ref, kc_ref, vp_ref, vc_ref, o_ref):
    n = pl.program_id(1)
    W = SWA_WINDOW
    q = q_ref[...]
    k = jnp.concatenate([kp_ref[...], kc_ref[...]], axis=0)
    v = jnp.concatenate([vp_ref[...], vc_ref[...]], axis=0)
    qpos = lax.broadcasted_iota(I32, (W, 2 * W), 0) + W
    kpos = lax.broadcasted_iota(I32, (W, 2 * W), 1)
    rel = qpos - kpos
    in_window = jnp.where(rel >= 0, jnp.where(rel < W, 1, 0), 0)
    has_prev = jnp.where(n > 0, 1, 0)
    valid = (in_window * jnp.where(kpos >= W, 1, has_prev)) > 0
    outs = []
    for h in range(SWA_Q_HEADS):
        g = h // SWA_GROUP
        qh = q[:, h * HEAD_DIM:(h + 1) * HEAD_DIM]
        kh = k[:, g * HEAD_DIM:(g + 1) * HEAD_DIM]
        vh = v[:, g * HEAD_DIM:(g + 1) * HEAD_DIM]
        s = _dot_nt(qh, kh) * (HEAD_DIM ** -0.5)
        s = jnp.where(valid, s, -jnp.inf)
        sink = sinks_ref[h]
        m = jnp.maximum(jnp.max(s, axis=-1, keepdims=True), sink)
        p = jnp.exp(s - m)
        den = jnp.sum(p, axis=-1, keepdims=True) + jnp.exp(sink - m)
        outs.append(_dot(p / den, vh))
    o_ref[...] = jnp.concatenate(outs, axis=1).astype(o_ref.dtype)


def _swa(qkv, sinks, B, S):
    W = SWA_WINDOW
    qkv3 = qkv.reshape(B, S, qkv.shape[-1])
    qw = SWA_Q_HEADS * HEAD_DIM
    kw = qw // SWA_GROUP
    kcol = qw // kw
    out = pl.pallas_call(
        _swa_kernel,
        grid=(B, S // W),
        in_specs=[pl.BlockSpec(memory_space=pltpu.SMEM),
                  pl.BlockSpec((None, W, qw), lambda b, n: (b, n, 0)),
                  pl.BlockSpec((None, W, kw), lambda b, n: (b, jnp.maximum(n - 1, 0), kcol)),
                  pl.BlockSpec((None, W, kw), lambda b, n: (b, n, kcol)),
                  pl.BlockSpec((None, W, kw), lambda b, n: (b, jnp.maximum(n - 1, 0), kcol + 1)),
                  pl.BlockSpec((None, W, kw), lambda b, n: (b, n, kcol + 1))],
        out_specs=pl.BlockSpec((None, W, qw), lambda b, n: (b, n, 0)),
        out_shape=jax.ShapeDtypeStruct((B, S, qw), BF16),
        compiler_params=_cparams(2),
        name="swa",
    )(sinks, qkv3, qkv3, qkv3, qkv3, qkv3)
    return out.reshape(B * S, qw)


def _rwkv_prep_kernel(p_ref, pprev_ref, mu_ref, w0_ref, w2_ref, a0_ref, a2_ref, g2_ref, kk_ref, ka_ref,
                      r_out, lw_out, k_out, v_out, a_out, b_out, g_out):
    n = pl.program_id(1)
    C = RWKV_WIDTH
    p = p_ref[...]
    last = jnp.where(n > 0, pprev_ref[7:8, :], 0.0)
    row = lax.broadcasted_iota(I32, p.shape, 0)
    p_prev = jnp.where(row == 0, last, pltpu.roll(p, 1, axis=0))
    p = p + (p_prev - p) * mu_ref[...]
    r = p[:, :C]
    k = p[:, C:2 * C]
    v = p[:, 2 * C:3 * C]
    xw = p[:, 3 * C:3 * C + 64]
    xa = p[:, 3 * C + 64:3 * C + 128]
    xg = p[:, 3 * C + 128:]
    w = -_softplus(-(w0_ref[...] + _dot(jnp.tanh(xw), w2_ref[...]))) - 0.5
    lw = -jnp.exp(w)
    a = _sigmoid(a0_ref[...] + _dot(xa, a2_ref[...]))
    g = _dot(_sigmoid(xg), g2_ref[...])
    kk = k * kk_ref[...]
    pieces = []
    for h in range(RWKV_HEADS):
        kh = kk[:, h * HEAD_DIM:(h + 1) * HEAD_DIM]
        nrm = jnp.sqrt(jnp.sum(kh * kh, axis=-1, keepdims=True))
        pieces.append(kh / jnp.maximum(nrm, 1e-12))
    kk = jnp.concatenate(pieces, axis=1)
    r_out[...] = r
    lw_out[...] = lw
    k_out[...] = k * (1.0 + (a - 1.0) * ka_ref[...])
    v_out[...] = v
    a_out[...] = -kk
    b_out[...] = kk * a
    g_out[...] = g


def _rwkv_prep(p, mu, w0, w2, a0, a2, g2, k_k, k_a, B, S, tt=256):
    C = RWKV_WIDTH
    PW = p.shape[-1]
    p3 = p.reshape(B, S, PW)
    row = lambda t: t.reshape(1, -1)
    full = lambda arr: pl.BlockSpec(arr.shape, lambda b, n: (0,) * arr.ndim)
    params = [row(mu), row(w0), w2.astype(BF16), row(a0), a2.astype(BF16), g2.astype(BF16), row(k_k), row(k_a)]
    outs = pl.pallas_call(
        _rwkv_prep_kernel,
        grid=(B, S // tt),
        in_specs=[pl.BlockSpec((None, tt, PW), lambda b, n: (b, n, 0)),
                  pl.BlockSpec((None, 8, PW), lambda b, n: (b, jnp.maximum(n * (tt // 8) - 1, 0), 0))]
                 + [full(t) for t in params],
        out_specs=[pl.BlockSpec((None, tt, C), lambda b, n: (b, n, 0))] * 7,
        out_shape=[jax.ShapeDtypeStruct((B, S, C), F32)] * 7,
        compiler_params=_cparams(2),
        name="rwkv_prep",
    )(p3, p3, *params)
    return outs


def _rwkv_scan_kernel(r_ref, lw_ref, k_ref, v_ref, a_ref, b_ref, g_ref, rk_ref, lnw_ref, lnb_ref,
                      o_ref, s_ref):
    c = pl.program_id(1)

    @pl.when(c == 0)
    def _():
        s_ref[...] = jnp.zeros_like(s_ref)

    C = RWKV_CHUNK
    row = lax.broadcasted_iota(I32, (C, C), 0)
    col = lax.broadcasted_iota(I32, (C, C), 1)
    lower = row >= col
    strict = row > col
    eye = jnp.where(row == col, 1.0, 0.0).astype(F32)

    lw = lw_ref[...]
    cum = _dot_f32(jnp.where(lower, 1.0, 0.0).astype(F32), lw)
    cum_last = cum[C - 1:C, :]
    r = r_ref[...]
    k = k_ref[...]
    v = v_ref[...]
    a = a_ref[...]
    b = b_ref[...]
    g = g_ref[...]
    e_neg = jnp.exp(-cum)
    e_rem = jnp.exp(cum_last - cum)
    r_t = r * jnp.exp(cum)
    a_t = a * jnp.exp(cum - lw)
    b_t = b * e_neg
    k_t = k * e_neg
    b_d = b * e_rem
    k_d = k * e_rem
    e_last = jnp.exp(cum_last)
    rkk = r * k * rk_ref[...]
    lnw = lnw_ref[...]
    lnb = lnb_ref[...]

    outs = []
    for h in range(RWKV_HEADS):
        sl = slice(h * HEAD_DIM, (h + 1) * HEAD_DIM)
        s0 = s_ref[h]
        ar = jnp.concatenate([a_t[:, sl], r_t[:, sl]], axis=0)
        mb = _dot_nt(ar, b_t[:, sl])
        mk = _dot_nt(ar, k_t[:, sl])
        l_ab = jnp.where(strict, mb[:C], 0.0)
        l_ak = jnp.where(strict, mk[:C], 0.0)
        m_rb = jnp.where(lower, mb[C:], 0.0)
        m_rk = jnp.where(lower, mk[C:], 0.0)
        inv = eye + l_ab
        pw = l_ab
        for _ in range(5):
            pw = _dot_f32(pw, pw)
            inv = inv + _dot_f32(inv, pw)
        ars = _dot_nt(ar, s0)
        vh = v[:, sl]
        u = _dot_f32(inv, ars[:C] + _dot(l_ak, vh))
        y = ars[C:] + _dot(m_rb, u) + _dot(m_rk, vh)
        s_ref[h] = s0 * e_last[:, sl] + _dot_tn(u, b_d[:, sl]) + _dot_tn(vh, k_d[:, sl])
        mean = jnp.mean(y, axis=-1, keepdims=True)
        yc = y - mean
        var = jnp.mean(yc * yc, axis=-1, keepdims=True)
        yn = yc * lax.rsqrt(var + RWKV_LN_EPS) * lnw[:, sl] + lnb[:, sl]
        bonus = jnp.sum(rkk[:, sl], axis=-1, keepdims=True) * vh
        outs.append((yn + bonus) * g[:, sl])
    o_ref[...] = jnp.concatenate(outs, axis=1).astype(o_ref.dtype)


def _rwkv_scan(r, lw, k, v, a, b, g, r_k, lnx_w, lnx_b):
    B, S, W = r.shape
    C = RWKV_CHUNK
    seq = pl.BlockSpec((None, C, W), lambda bb, c: (bb, c, 0))
    par = pl.BlockSpec((1, W), lambda bb, c: (0, 0))
    out = pl.pallas_call(
        _rwkv_scan_kernel,
        grid=(B, S // C),
        in_specs=[seq] * 7 + [par] * 3,
        out_specs=seq,
        out_shape=jax.ShapeDtypeStruct((B, S, W), BF16),
        scratch_shapes=[pltpu.VMEM((RWKV_HEADS, HEAD_DIM, HEAD_DIM), F32)],
        compiler_params=_cparams(2),
        name="rwkv_scan",
    )(r, lw, k, v, a, b, g, r_k.reshape(1, W), lnx_w.reshape(1, W), lnx_b.reshape(1, W))
    return out.reshape(B * S, W)


def _gla_kernel(q_ref, k_ref, v_ref, og_ref, gd_ref, gup_ref, gb_ref, on_ref, o_ref, s_ref):
    c = pl.program_id(1)

    @pl.when(c == 0)
    def _():
        s_ref[...] = jnp.zeros_like(s_ref)

    C = GLA_CHUNK
    row = lax.broadcasted_iota(I32, (C, C), 0)
    col = lax.broadcasted_iota(I32, (C, C), 1)
    lower = row >= col
    z = _dot(gd_ref[...], gup_ref[...]) + gb_ref[...]
    gk = -_softplus(-z) / GLA_GATE_NORM
    cum = _dot_f32(jnp.where(lower, 1.0, 0.0).astype(F32), gk)
    cum_last = cum[C - 1:C, :]
    q = q_ref[...] * (GLA_DK ** -0.5)
    k = k_ref[...]
    qe = q * jnp.exp(cum)
    ke = k * jnp.exp(-cum)
    kd = k * jnp.exp(cum_last - cum)
    e_last = jnp.exp(cum_last)
    v = v_ref[...]
    og = og_ref[...]
    onorm = on_ref[...]
    outs = []
    for h in range(GLA_HEADS):
        ks = slice(h * GLA_DK, (h + 1) * GLA_DK)
        vs = slice(h * GLA_DV, (h + 1) * GLA_DV)
        st = s_ref[h]
        vh = v[:, vs]
        att = jnp.where(lower, _dot_nt(qe[:, ks], ke[:, ks]), 0.0)
        o = _dot_nt(qe[:, ks], st) + _dot(att, vh)
        s_ref[h] = st * e_last[:, ks] + _dot_tn(vh, kd[:, ks])
        o = _rms(o, onorm)
        gate = og[:, vs]
        outs.append(o * (gate * _sigmoid(gate)))
    o_ref[...] = jnp.concatenate(outs, axis=1).astype(o_ref.dtype)


def _gla(qkvo, gd, gate_up_pad, gate_b, onorm, B, S):
    C = GLA_CHUNK
    KW = GLA_HEADS * GLA_DK
    VW = GLA_HEADS * GLA_DV
    x3 = qkvo.reshape(B, S, qkvo.shape[-1])
    gd3 = gd.reshape(B, S, LANES)
    kb = KW // KW
    out = pl.pallas_call(
        _gla_kernel,
        grid=(B, S // C),
        in_specs=[pl.BlockSpec((None, C, KW), lambda b, c: (b, c, 0)),
                  pl.BlockSpec((None, C, KW), lambda b, c: (b, c, kb)),
                  pl.BlockSpec((None, C, VW), lambda b, c: (b, c, 1)),
                  pl.BlockSpec((None, C, VW), lambda b, c: (b, c, 2)),
                  pl.BlockSpec((None, C, LANES), lambda b, c: (b, c, 0)),
                  pl.BlockSpec((LANES, KW), lambda b, c: (0, 0)),
                  pl.BlockSpec((1, KW), lambda b, c: (0, 0)),
                  pl.BlockSpec((1, GLA_DV), lambda b, c: (0, 0))],
        out_specs=pl.BlockSpec((None, C, VW), lambda b, c: (b, c, 0)),
        out_shape=jax.ShapeDtypeStruct((B, S, VW), BF16),
        scratch_shapes=[pltpu.VMEM((GLA_HEADS, GLA_DV, GLA_DK), F32)],
        compiler_params=_cparams(2),
        name="gla",
    )(x3, x3, x3, x3, gd3, gate_up_pad, gate_b.reshape(1, KW), onorm.reshape(1, GLA_DV))
    return out.reshape(B * S, VW)


def _xattn_kernel(x_ref, g_ref, wq_ref, mk_ref, mv_ref, wo_ref, o_ref):
    x = x_ref[...]
    h = _rms(x, g_ref[...])
    q = _dot(h, wq_ref[...])
    mk = mk_ref[...]
    mv = mv_ref[...]
    outs = []
    for hd in range(XA_HEADS):
        sl = slice(hd * XA_HEAD_DIM, (hd + 1) * XA_HEAD_DIM)
        s = _dot_nt(q[:, sl], mk[:, sl]) * (XA_HEAD_DIM ** -0.5)
        m = jnp.max(s, axis=-1, keepdims=True)
        p = jnp.exp(s - m)
        p = p / jnp.sum(p, axis=-1, keepdims=True)
        outs.append(_dot(p, mv[:, sl]))
    o = jnp.concatenate(outs, axis=1)
    o_ref[...] = x + _dot(o, wo_ref[...])


def _xattn(x, g, wq, mk, mv, wo, B, S, tq=256):
    D = x.shape[-1]
    M = mk.shape[0] // B
    XW = mk.shape[-1]
    x3 = x.reshape(B, S, D)
    out = pl.pallas_call(
        _xattn_kernel,
        grid=(B, S // tq),
        in_specs=[pl.BlockSpec((None, tq, D), lambda b, n: (b, n, 0)),
                  pl.BlockSpec((1, D), lambda b, n: (0, 0)),
                  pl.BlockSpec((D, XW), lambda b, n: (0, 0)),
                  pl.BlockSpec((None, M, XW), lambda b, n: (b, 0, 0)),
                  pl.BlockSpec((None, M, XW), lambda b, n: (b, 0, 0)),
                  pl.BlockSpec((XW, D), lambda b, n: (0, 0))],
        out_specs=pl.BlockSpec((None, tq, D), lambda b, n: (b, n, 0)),
        out_shape=jax.ShapeDtypeStruct((B, S, D), F32),
        compiler_params=_cparams(2),
        name="xattn",
    )(x3, g.reshape(1, D), wq, mk.reshape(B, M, XW), mv.reshape(B, M, XW), wo)
    return out.reshape(B * S, D)


def _router_kernel(x_ref, g_ref, w_ref, b_ref, eid_ref, gate_ref):
    h = _rms(x_ref[...], g_ref[...])
    logits = _dot_f32(h, w_ref[...]) + b_ref[...]
    lane = lax.broadcasted_iota(I32, logits.shape, 1)
    big = jnp.int32(LANES)
    neg = -jnp.inf
    gl = jnp.where(lane < MOE_GROUPS, logits, neg)
    gmax = jnp.max(gl, axis=-1, keepdims=True)
    g_top = jnp.min(jnp.where(gl == gmax, lane, big), axis=-1, keepdims=True)
    p_group = 1.0 / jnp.sum(jnp.exp(gl - gmax), axis=-1, keepdims=True)
    lo = MOE_GROUPS + MOE_EXPERTS_PER_GROUP * g_top
    in_group = jnp.where(lane >= lo, jnp.where(lane < lo + MOE_EXPERTS_PER_GROUP, 1, 0), 0) > 0
    el = jnp.where(in_group, logits, neg)
    emax = jnp.max(el, axis=-1, keepdims=True)
    ee = jnp.exp(el - emax)
    prob = ee / jnp.sum(ee, axis=-1, keepdims=True)
    prob = jnp.where(in_group, prob, -1.0)
    p1 = jnp.max(prob, axis=-1, keepdims=True)
    i1 = jnp.min(jnp.where(prob == p1, lane, big), axis=-1, keepdims=True)
    rest = jnp.where(lane == i1, -1.0, prob)
    p2 = jnp.max(rest, axis=-1, keepdims=True)
    i2 = jnp.min(jnp.where(rest == p2, lane, big), axis=-1, keepdims=True)
    tot = p1 + p2
    eid_ref[...] = jnp.where(lane == 0, i1 - MOE_GROUPS, jnp.where(lane == 1, i2 - MOE_GROUPS, 0))
    gate_ref[...] = jnp.where(lane == 0, p_group * p1 / tot, jnp.where(lane == 1, p_group * p2 / tot, 0.0))


def _router(x, g, w_router, b_router, tm=256):
    T, D = x.shape
    return pl.pallas_call(
        _router_kernel,
        grid=(T // tm,),
        in_specs=[pl.BlockSpec((tm, D), lambda i: (i, 0)),
                  pl.BlockSpec((1, D), lambda i: (0, 0)),
                  pl.BlockSpec((D, LANES), lambda i: (0, 0)),
                  pl.BlockSpec((1, LANES), lambda i: (0, 0))],
        out_specs=[pl.BlockSpec((tm, LANES), lambda i: (i, 0))] * 2,
        out_shape=[jax.ShapeDtypeStruct((T, LANES), I32), jax.ShapeDtypeStruct((T, LANES), F32)],
        compiler_params=_cparams(1),
        name="router",
    )(x, g.reshape(1, D), w_router, b_router)


def _gather_rows(src_hbm, idx_ref, dst_ref, sem, n_rows):
    def issue(r, carry):
        pltpu.make_async_copy(src_hbm.at[pl.ds(idx_ref[0, r], 1), :],
                              dst_ref.at[pl.ds(r, 1), :], sem).start()
        return carry
    lax.fori_loop(0, n_rows, issue, 0, unroll=8)


def _wait_rows(src_hbm, dst_ref, sem, n_rows):
    pltpu.make_async_copy(src_hbm.at[pl.ds(0, n_rows), :], dst_ref, sem).wait()


def _moe_expert_kernel(be_ref, tok_ref, tokn_ref, x_hbm, g_ref, w1_ref, w3_ref, w2_ref, o_ref,
                       xbuf, sems, w1b, w3b, w2b):
    i = pl.program_id(0)
    nb = pl.num_programs(0)
    slot = lax.rem(i, 2)

    @pl.when(i == 0)
    def _():
        _gather_rows(x_hbm, tok_ref, xbuf.at[0], sems.at[0], MOE_BLOCK)

    @pl.when(i + 1 < nb)
    def _():
        _gather_rows(x_hbm, tokn_ref, xbuf.at[1 - slot], sems.at[1 - slot], MOE_BLOCK)

    changed = jnp.logical_or(i == 0, be_ref[i] != be_ref[jnp.maximum(i - 1, 0)])

    @pl.when(changed)
    def _():
        w1b[...] = w1_ref[...].astype(BF16)
        w3b[...] = w3_ref[...].astype(BF16)
        w2b[...] = w2_ref[...].astype(BF16)

    _wait_rows(x_hbm, xbuf.at[slot], sems.at[slot], MOE_BLOCK)
    xe = _rms(xbuf[slot], g_ref[...]).astype(BF16)
    h1 = jnp.dot(xe, w1b[...], preferred_element_type=F32)
    h3 = jnp.dot(xe, w3b[...], preferred_element_type=F32)
    act = (h1 * _sigmoid(h1) * h3).astype(BF16)
    o_ref[...] = jnp.dot(act, w2b[...], preferred_element_type=F32)


def _moe_experts(x, g, block_e, buf_tok, w1, w3, w2):
    T, D = x.shape
    FF = w1.shape[-1]
    P = buf_tok.shape[0]
    NB = P // MOE_BLOCK
    tok3 = buf_tok.reshape(NB, 1, MOE_BLOCK)
    grid_spec = pltpu.PrefetchScalarGridSpec(
        num_scalar_prefetch=1,
        grid=(NB,),
        in_specs=[pl.BlockSpec((None, 1, MOE_BLOCK), lambda i, be: (i, 0, 0), memory_space=pltpu.SMEM),
                  pl.BlockSpec((None, 1, MOE_BLOCK), lambda i, be: (jnp.minimum(i + 1, NB - 1), 0, 0),
                               memory_space=pltpu.SMEM),
                  pl.BlockSpec(memory_space=pl.ANY),
                  pl.BlockSpec((1, D), lambda i, be: (0, 0)),
                  pl.BlockSpec((None, D, FF), lambda i, be: (be[i], 0, 0)),
                  pl.BlockSpec((None, D, FF), lambda i, be: (be[i], 0, 0)),
                  pl.BlockSpec((None, FF, D), lambda i, be: (be[i], 0, 0))],
        out_specs=pl.BlockSpec((MOE_BLOCK, D), lambda i, be: (i, 0)),
        scratch_shapes=[pltpu.VMEM((2, MOE_BLOCK, D), F32),
                        pltpu.SemaphoreType.DMA((2,)),
                        pltpu.VMEM((D, FF), BF16),
                        pltpu.VMEM((D, FF), BF16),
                        pltpu.VMEM((FF, D), BF16)],
    )
    return pl.pallas_call(
        _moe_expert_kernel,
        grid_spec=grid_spec,
        out_shape=jax.ShapeDtypeStruct((P, D), F32),
        compiler_params=_cparams(1),
        name="moe_experts",
    )(block_e, tok3, tok3, x, g.reshape(1, D), w1, w3, w2)


def _moe_combine_kernel(pos_ref, posn_ref, x_ref, gate_ref, yb_hbm, gf_ref, o_ref, ybuf, sems, *, tc, final_norm):
    i = pl.program_id(0)
    nb = pl.num_programs(0)
    slot = lax.rem(i, 2)

    @pl.when(i == 0)
    def _():
        _gather_rows(yb_hbm, pos_ref, ybuf.at[0], sems.at[0], 2 * tc)

    @pl.when(i + 1 < nb)
    def _():
        _gather_rows(yb_hbm, posn_ref, ybuf.at[1 - slot], sems.at[1 - slot], 2 * tc)

    _wait_rows(yb_hbm, ybuf.at[slot], sems.at[slot], 2 * tc)
    gate = gate_ref[...]
    y0 = ybuf[slot, 0:tc, :]
    y1 = ybuf[slot, tc:2 * tc, :]
    out = x_ref[...] + (y0 * gate[:, 0:1] + y1 * gate[:, 1:2])
    if final_norm:
        out = _rms(out, gf_ref[...])
    o_ref[...] = out


def _moe_combine(x, gates, pos, yb, g_final, final_norm, tc=128):
    T, D = x.shape
    NT = T // tc
    pos3 = pos.reshape(NT, tc, 2).transpose(0, 2, 1).reshape(NT, 1, 2 * tc)
    return pl.pallas_call(
        functools.partial(_moe_combine_kernel, tc=tc, final_norm=final_norm),
        grid=(NT,),
        in_specs=[pl.BlockSpec((None, 1, 2 * tc), lambda i: (i, 0, 0), memory_space=pltpu.SMEM),
                  pl.BlockSpec((None, 1, 2 * tc), lambda i: (jnp.minimum(i + 1, NT - 1), 0, 0),
                               memory_space=pltpu.SMEM),
                  pl.BlockSpec((tc, D), lambda i: (i, 0)),
                  pl.BlockSpec((tc, LANES), lambda i: (i, 0)),
                  pl.BlockSpec(memory_space=pl.ANY),
                  pl.BlockSpec((1, D), lambda i: (0, 0))],
        out_specs=pl.BlockSpec((tc, D), lambda i: (i, 0)),
        out_shape=jax.ShapeDtypeStruct((T, D), F32),
        scratch_shapes=[pltpu.VMEM((2, 2 * tc, D), F32), pltpu.SemaphoreType.DMA((2,))],
        compiler_params=_cparams(1),
        name="moe_combine",
    )(pos3, pos3, x, gates, yb, g_final.reshape(1, D))


def _moe_slots(eid, T):
    A = 2 * T
    P = A + MOE_EXPERTS * MOE_BLOCK
    NB = P // MOE_BLOCK
    e_flat = eid[:, :2].reshape(A)
    onehot = (e_flat[:, None] == jnp.arange(MOE_EXPERTS, dtype=I32)[None, :]).astype(I32)
    csum = jnp.cumsum(onehot, axis=0)
    counts = csum[-1]
    rank = jnp.sum(csum * onehot, axis=1) - 1
    padded = (counts + MOE_BLOCK - 1) // MOE_BLOCK * MOE_BLOCK
    pends = jnp.cumsum(padded)
    pstarts = pends - padded
    dest = (pstarts[e_flat] + rank).astype(I32)
    buf_tok = jnp.zeros((P,), I32).at[dest].set(jnp.arange(A, dtype=I32) // 2)
    block_e = jnp.minimum(jnp.searchsorted(pends, jnp.arange(NB, dtype=I32) * MOE_BLOCK, side='right'),
                          MOE_EXPERTS - 1).astype(I32)
    return dest.reshape(T, 2), buf_tok, block_e


def _moe_layer(x, g, w_group, b_group, w_expert, b_expert, w1, w3, w2, g_final, final_norm):
    T, D = x.shape
    n_log = MOE_GROUPS + MOE_EXPERTS
    w_router = jnp.zeros((D, LANES), F32).at[:, :MOE_GROUPS].set(w_group).at[:, MOE_GROUPS:n_log].set(w_expert)
    b_router = jnp.zeros((1, LANES), F32).at[0, :MOE_GROUPS].set(b_group).at[0, MOE_GROUPS:n_log].set(b_expert)
    eid, gates = _router(x, g, w_router, b_router)
    pos, buf_tok, block_e = _moe_slots(eid, T)
    yb = _moe_experts(x, g, block_e, buf_tok, w1, w3, w2)
    return _moe_combine(x, gates, pos, yb, g_final, final_norm)


def kernel(x, mem, norm_mix, norm_xattn, norm_moe, norm_final, ev_w_in, ev_sinks, ev_mu, ev_w0, ev_w2, ev_a0, ev_a2, ev_g2, ev_k_k, ev_k_a, ev_r_k, ev_lnx_w, ev_lnx_b, ev_w_out, od_w_in, od_gate_up, od_gate_b, od_onorm, od_w_out, mem_norm, mem_wk, mem_wv, xa_wq, xa_wo, moe_w_group, moe_b_group, moe_w_expert, moe_b_expert, moe_w1, moe_w3, moe_w2):
    B, S, D = x.shape
    M = mem.shape[1]
    T = B * S
    depth = norm_mix.shape[0]
    xf = x.reshape(T, D)

    XW = XA_HEADS * XA_HEAD_DIM
    w_kv = jnp.concatenate([mem_wk, mem_wv], axis=1).astype(BF16)
    mk, mv = _norm_matmul(mem.reshape(B * M, D), mem_norm, w_kv, (XW, XW), (BF16, BF16))

    for layer in range(depth):
        i = layer // 2
        if layer % 2 == 0:
            swa_cols = SWA_Q_HEADS * HEAD_DIM + 2 * (SWA_Q_HEADS // SWA_GROUP) * HEAD_DIM
            rw_cols = ev_w_in.shape[-1] - swa_cols
            qkv, p_rw = _norm_matmul(xf, norm_mix[layer], ev_w_in[i].astype(BF16),
                                     (swa_cols, rw_cols), (F32, F32))
            o_a = _swa(qkv, ev_sinks[i], B, S)
            r, lw, k, v, a, b, g = _rwkv_prep(p_rw, ev_mu[i], ev_w0[i], ev_w2[i], ev_a0[i], ev_a2[i],
                                              ev_g2[i], ev_k_k[i], ev_k_a[i], B, S)
            o_b = _rwkv_scan(r, lw, k, v, a, b, g, ev_r_k[i].reshape(-1), ev_lnx_w[i], ev_lnx_b[i])
            w_out = ev_w_out[i].astype(BF16)
            qw = o_a.shape[-1]
            xf = _proj_residual(xf, [o_a, o_b], [w_out[:qw], w_out[qw:]])
        else:
            KW = GLA_HEADS * GLA_DK
            VW = GLA_HEADS * GLA_DV
            R = od_gate_up.shape[1]
            w = od_w_in[i]
            w_re = jnp.concatenate([w[:, :2 * KW + VW], w[:, 2 * KW + VW + R:],
                                    w[:, 2 * KW + VW:2 * KW + VW + R],
                                    jnp.zeros((D, LANES - R), F32)], axis=1).astype(BF16)
            qkvo, gd = _norm_matmul(xf, norm_mix[layer], w_re, (2 * KW + 2 * VW, LANES), (F32, F32))
            gup = jnp.zeros((LANES, KW), F32).at[:R].set(od_gate_up[i]).astype(BF16)
            o = _gla(qkvo, gd, gup, od_gate_b[i], od_onorm[i], B, S)
            xf = _proj_residual(xf, [o], [od_w_out[i].astype(BF16)])
        xf = _xattn(xf, norm_xattn[layer], xa_wq[layer].astype(BF16), mk, mv, xa_wo[layer].astype(BF16), B, S)
        xf = _moe_layer(xf, norm_moe[layer], moe_w_group[layer], moe_b_group[layer], moe_w_expert[layer],
                        moe_b_expert[layer], moe_w1[layer], moe_w3[layer], moe_w2[layer],
                        norm_final, layer == depth - 1)
    return xf.reshape(B, S, D)
```

```python
import functools

import jax
import jax.numpy as jnp
from jax import lax
from jax.experimental import pallas as pl
from jax.experimental.pallas import tpu as pltpu

F32 = jnp.float32
BF16 = jnp.bfloat16
I32 = jnp.int32

EPS = 1e-6
HEAD_DIM = 64
SWA_WINDOW = 128
SWA_Q_HEADS = 8
SWA_GROUP = 4
RWKV_HEADS = 8
RWKV_WIDTH = 512
RWKV_LN_EPS = 64e-5
RWKV_CHUNK = 64
GLA_HEADS = 4
GLA_DK = 128
GLA_DV = 256
GLA_CHUNK = 64
GLA_GATE_NORM = 16.0
XA_HEADS = 4
XA_HEAD_DIM = 128
MOE_GROUPS = 4
MOE_EXPERTS_PER_GROUP = 8
MOE_EXPERTS = 32
MOE_BLOCK = 128
LANES = 128

VMEM_LIMIT_BYTES = 48 * 1024 * 1024


def _cparams(n_axes):
    return pltpu.CompilerParams(dimension_semantics=("arbitrary",) * n_axes,
                                vmem_limit_bytes=VMEM_LIMIT_BYTES)


def _dot(a, b):
    return jnp.dot(a.astype(BF16), b.astype(BF16), preferred_element_type=F32)


def _dot_nt(a, b):
    return lax.dot_general(a.astype(BF16), b.astype(BF16), (((1,), (1,)), ((), ())),
                           preferred_element_type=F32)


def _dot_tn(a, b):
    return lax.dot_general(a.astype(BF16), b.astype(BF16), (((0,), (0,)), ((), ())),
                           preferred_element_type=F32)


def _dot_f32(a, b):
    return jnp.dot(a, b, preferred_element_type=F32, precision=lax.Precision.HIGHEST)


def _rms(x, g):
    ms = jnp.mean(x * x, axis=-1, keepdims=True)
    return x * lax.rsqrt(ms + EPS) * g


def _sigmoid(x):
    return 1.0 / (1.0 + jnp.exp(-x))


def _softplus(x):
    return jnp.maximum(x, 0.0) + jnp.log(1.0 + jnp.exp(-jnp.abs(x)))


def _norm_matmul_kernel(x_ref, g_ref, w_ref, *o_refs, splits):
    h = _rms(x_ref[...], g_ref[...]).astype(BF16)
    off = 0
    for o_ref, n in zip(o_refs, splits):
        o_ref[...] = jnp.dot(h, w_ref[:, off:off + n], preferred_element_type=F32).astype(o_ref.dtype)
        off += n


def _norm_matmul(x, g, w, splits, out_dtypes, tm=256):
    T, D = x.shape
    N = w.shape[1]
    assert sum(splits) == N and T % tm == 0
    return pl.pallas_call(
        functools.partial(_norm_matmul_kernel, splits=tuple(splits)),
        grid=(T // tm,),
        in_specs=[pl.BlockSpec((tm, D), lambda i: (i, 0)),
                  pl.BlockSpec((1, D), lambda i: (0, 0)),
                  pl.BlockSpec((D, N), lambda i: (0, 0))],
        out_specs=[pl.BlockSpec((tm, n), lambda i: (i, 0)) for n in splits],
        out_shape=[jax.ShapeDtypeStruct((T, n), dt) for n, dt in zip(splits, out_dtypes)],
        compiler_params=_cparams(1),
        name="norm_matmul",
    )(x, g.reshape(1, D), w)


def _proj_residual_kernel(*refs, n_in):
    x_ref = refs[0]
    a_refs = refs[1:1 + n_in]
    w_refs = refs[1 + n_in:1 + 2 * n_in]
    o_ref = refs[1 + 2 * n_in]
    acc = x_ref[...]
    for a_ref, w_ref in zip(a_refs, w_refs):
        acc = acc + jnp.dot(a_ref[...], w_ref[...], preferred_element_type=F32)
    o_ref[...] = acc


def _proj_residual(x, acts, weights, tm=512):
    T, D = x.shape
    n_in = len(acts)
    in_specs = [pl.BlockSpec((tm, D), lambda i: (i, 0))]
    in_specs += [pl.BlockSpec((tm, a.shape[1]), lambda i: (i, 0)) for a in acts]
    in_specs += [pl.BlockSpec(w.shape, lambda i: (0, 0)) for w in weights]
    return pl.pallas_call(
        functools.partial(_proj_residual_kernel, n_in=n_in),
        grid=(T // tm,),
        in_specs=in_specs,
        out_specs=pl.BlockSpec((tm, D), lambda i: (i, 0)),
        out_shape=jax.ShapeDtypeStruct((T, D), F32),
        compiler_params=_cparams(1),
        name="proj_residual",
    )(x, *acts, *weights)


def _swa_kernel(sinks_ref, q_ref, kp_ref, kc_ref, vp_ref, vc_ref, o_ref):
    n = pl.program_id(1)
    W = SWA_WINDOW
    q = q_ref[...]
    k = jnp.concatenate([kp_ref[...], kc_ref[...]], axis=0)
    v = jnp.concatenate([vp_ref[...], vc_ref[...]], axis=0)
    qpos = lax.broadcasted_iota(I32, (W, 2 * W), 0) + W
    kpos = lax.broadcasted_iota(I32, (W, 2 * W), 1)
    rel = qpos - kpos
    in_window = jnp.where(rel >= 0, jnp.where(rel < W, 1, 0), 0)
    has_prev = jnp.where(n > 0, 1, 0)
    valid = (in_window * jnp.where(kpos >= W, 1, has_prev)) > 0
    n_groups = SWA_Q_HEADS // SWA_GROUP
    qb = q.astype(BF16)
    kb = k.astype(BF16)
    vb = v.astype(BF16)
    scores = []
    for g in range(n_groups):
        qg = jnp.concatenate([qb[:, h * HEAD_DIM:(h + 1) * HEAD_DIM]
                              for h in range(g * SWA_GROUP, (g + 1) * SWA_GROUP)], axis=0)
        scores.append(_dot_nt(qg, kb[:, g * HEAD_DIM:(g + 1) * HEAD_DIM]))
    probs = []
    for g in range(n_groups):
        pieces = []
        for j in range(SWA_GROUP):
            s = jnp.where(valid, scores[g][j * W:(j + 1) * W] * (HEAD_DIM ** -0.5), -jnp.inf)
            sink = sinks_ref[g * SWA_GROUP + j]
            m = jnp.maximum(jnp.max(s, axis=-1, keepdims=True), sink)
            p = jnp.exp(s - m)
            den = jnp.sum(p, axis=-1, keepdims=True) + jnp.exp(sink - m)
            pieces.append((p / den).astype(BF16))
        probs.append(jnp.concatenate(pieces, axis=0))
    outs = []
    for g in range(n_groups):
        og = _dot(probs[g], vb[:, g * HEAD_DIM:(g + 1) * HEAD_DIM])
        outs += [og[j * W:(j + 1) * W] for j in range(SWA_GROUP)]
    o_ref[...] = jnp.concatenate(outs, axis=1).astype(o_ref.dtype)


def _swa(qkv, sinks, B, S):
    W = SWA_WINDOW
    qkv3 = qkv.reshape(B, S, qkv.shape[-1])
    qw = SWA_Q_HEADS * HEAD_DIM
    kw = qw // SWA_GROUP
    kcol = qw // kw
    out = pl.pallas_call(
        _swa_kernel,
        grid=(B, S // W),
        in_specs=[pl.BlockSpec(memory_space=pltpu.SMEM),
                  pl.BlockSpec((None, W, qw), lambda b, n: (b, n, 0)),
                  pl.BlockSpec((None, W, kw), lambda b, n: (b, jnp.maximum(n - 1, 0), kcol)),
                  pl.BlockSpec((None, W, kw), lambda b, n: (b, n, kcol)),
                  pl.BlockSpec((None, W, kw), lambda b, n: (b, jnp.maximum(n - 1, 0), kcol + 1)),
                  pl.BlockSpec((None, W, kw), lambda b, n: (b, n, kcol + 1))],
        out_specs=pl.BlockSpec((None, W, qw), lambda b, n: (b, n, 0)),
        out_shape=jax.ShapeDtypeStruct((B, S, qw), BF16),
        compiler_params=_cparams(2),
        name="swa",
    )(sinks, qkv3, qkv3, qkv3, qkv3, qkv3)
    return out.reshape(B * S, qw)


def _rwkv_prep_kernel(p_ref, pprev_ref, mu_ref, w0_ref, w2_ref, a0_ref, a2_ref, g2_ref, kk_ref, ka_ref,
                      r_out, lw_out, k_out, v_out, a_out, b_out, g_out):
    n = pl.program_id(1)
    C = RWKV_WIDTH
    p = p_ref[...]
    last = jnp.where(n > 0, pprev_ref[7:8, :], 0.0)
    row = lax.broadcasted_iota(I32, p.shape, 0)
    p_prev = jnp.where(row == 0, last, pltpu.roll(p, 1, axis=0))
    p = p + (p_prev - p) * mu_ref[...]
    r = p[:, :C]
    k = p[:, C:2 * C]
    v = p[:, 2 * C:3 * C]
    xw = p[:, 3 * C:3 * C + 64]
    xa = p[:, 3 * C + 64:3 * C + 128]
    xg = p[:, 3 * C + 128:]
    w = -_softplus(-(w0_ref[...] + _dot(jnp.tanh(xw), w2_ref[...]))) - 0.5
    lw = -jnp.exp(w)
    a = _sigmoid(a0_ref[...] + _dot(xa, a2_ref[...]))
    g = _dot(_sigmoid(xg), g2_ref[...])
    kk = k * kk_ref[...]
    pieces = []
    for h in range(RWKV_HEADS):
        kh = kk[:, h * HEAD_DIM:(h + 1) * HEAD_DIM]
        nrm = jnp.sqrt(jnp.sum(kh * kh, axis=-1, keepdims=True))
        pieces.append(kh / jnp.maximum(nrm, 1e-12))
    kk = jnp.concatenate(pieces, axis=1)
    r_out[...] = r
    lw_out[...] = lw
    k_out[...] = k * (1.0 + (a - 1.0) * ka_ref[...])
    v_out[...] = v
    a_out[...] = -kk
    b_out[...] = kk * a
    g_out[...] = g


def _rwkv_prep(p, mu, w0, w2, a0, a2, g2, k_k, k_a, B, S, tt=256):
    C = RWKV_WIDTH
    PW = p.shape[-1]
    p3 = p.reshape(B, S, PW)
    row = lambda t: t.reshape(1, -1)
    full = lambda arr: pl.BlockSpec(arr.shape, lambda b, n: (0,) * arr.ndim)
    params = [row(mu), row(w0), w2.astype(BF16), row(a0), a2.astype(BF16), g2.astype(BF16), row(k_k), row(k_a)]
    outs = pl.pallas_call(
        _rwkv_prep_kernel,
        grid=(B, S // tt),
        in_specs=[pl.BlockSpec((None, tt, PW), lambda b, n: (b, n, 0)),
                  pl.BlockSpec((None, 8, PW), lambda b, n: (b, jnp.maximum(n * (tt // 8) - 1, 0), 0))]
                 + [full(t) for t in params],
        out_specs=[pl.BlockSpec((None, tt, C), lambda b, n: (b, n, 0))] * 7,
        out_shape=[jax.ShapeDtypeStruct((B, S, C), F32)] * 7,
        compiler_params=_cparams(2),
        name="rwkv_prep",
    )(p3, p3, *params)
    return outs


def _rwkv_scan_kernel(r_ref, lw_ref, k_ref, v_ref, a_ref, b_ref, g_ref, rk_ref, lnw_ref, lnb_ref,
                      o_ref, s_ref):
    c = pl.program_id(1)

    @pl.when(c == 0)
    def _():
        s_ref[...] = jnp.zeros_like(s_ref)

    C = RWKV_CHUNK
    row = lax.broadcasted_iota(I32, (C, C), 0)
    col = lax.broadcasted_iota(I32, (C, C), 1)
    lower = row >= col
    strict = row > col

    lw = lw_ref[...]
    cum = _dot_f32(jnp.where(lower, 1.0, 0.0).astype(F32), lw)
    cum_last = cum[C - 1:C, :]
    r = r_ref[...]
    k = k_ref[...]
    v = v_ref[...]
    a = a_ref[...]
    b = b_ref[...]
    g = g_ref[...]
    e_neg = jnp.exp(-cum)
    e_rem = jnp.exp(cum_last - cum)
    r_t = (r * jnp.exp(cum)).astype(BF16)
    a_t = (a * jnp.exp(cum - lw)).astype(BF16)
    b_t = (b * e_neg).astype(BF16)
    k_t = (k * e_neg).astype(BF16)
    b_d = (b * e_rem).astype(BF16)
    k_d = (k * e_rem).astype(BF16)
    v_b = v.astype(BF16)
    e_last = jnp.exp(cum_last)
    rkk = r * k * rk_ref[...]
    lnw = lnw_ref[...]
    lnb = lnb_ref[...]
    row2 = lax.broadcasted_iota(I32, (C, 2 * C), 0)
    col2 = lax.broadcasted_iota(I32, (C, 2 * C), 1)
    colm = jnp.where(col2 >= C, col2 - C, col2)
    lower2 = row2 >= colm
    strict_k = jnp.where(col2 >= C, jnp.where(row2 > colm, 1, 0), 0) > 0

    heads = range(RWKV_HEADS)
    sls = [slice(h * HEAD_DIM, (h + 1) * HEAD_DIM) for h in heads]
    s0s = [s_ref[h] for h in heads]
    ars_l = [jnp.concatenate([a_t[:, sl], r_t[:, sl]], axis=0) for sl in sls]
    bks = [jnp.concatenate([b_t[:, sl], k_t[:, sl]], axis=0) for sl in sls]
    ms = [_dot_nt(ars_l[h], bks[h]) for h in heads]
    arss = [_dot_nt(ars_l[h], s0s[h]) for h in heads]
    vhs = [v_b[:, sl] for sl in sls]
    xs = [arss[h][:C] + _dot(jnp.where(strict_k, ms[h][:C], 0.0), jnp.concatenate([vhs[h], vhs[h]], axis=0))
          for h in heads]
    pws = [jnp.where(strict, ms[h][:C, :C], 0.0).astype(BF16) for h in heads]
    xs = [xs[h] + _dot(pws[h], xs[h]) for h in heads]
    for _ in range(5):
        pws = [_dot(pws[h], pws[h]).astype(BF16) for h in heads]
        xs = [xs[h] + _dot(pws[h], xs[h]) for h in heads]
    uvs = [jnp.concatenate([xs[h].astype(BF16), vhs[h]], axis=0) for h in heads]
    ys = [arss[h][C:] + _dot(jnp.where(lower2, ms[h][C:], 0.0), uvs[h]) for h in heads]
    for h in heads:
        bkd = jnp.concatenate([b_d[:, sls[h]], k_d[:, sls[h]]], axis=0)
        s_ref[h] = s0s[h] * e_last[:, sls[h]] + _dot_tn(uvs[h], bkd)

    outs = []
    for h in heads:
        sl = sls[h]
        y = ys[h]
        mean = jnp.mean(y, axis=-1, keepdims=True)
        yc = y - mean
        var = jnp.mean(yc * yc, axis=-1, keepdims=True)
        yn = yc * lax.rsqrt(var + RWKV_LN_EPS) * lnw[:, sl] + lnb[:, sl]
        bonus = jnp.sum(rkk[:, sl], axis=-1, keepdims=True) * v[:, sl]
        outs.append((yn + bonus) * g[:, sl])
    o_ref[...] = jnp.concatenate(outs, axis=1).astype(o_ref.dtype)


def _rwkv_scan(r, lw, k, v, a, b, g, r_k, lnx_w, lnx_b):
    B, S, W = r.shape
    C = RWKV_CHUNK
    seq = pl.BlockSpec((None, C, W), lambda bb, c: (bb, c, 0))
    par = pl.BlockSpec((1, W), lambda bb, c: (0, 0))
    out = pl.pallas_call(
        _rwkv_scan_kernel,
        grid=(B, S // C),
        in_specs=[seq] * 7 + [par] * 3,
        out_specs=seq,
        out_shape=jax.ShapeDtypeStruct((B, S, W), BF16),
        scratch_shapes=[pltpu.VMEM((RWKV_HEADS, HEAD_DIM, HEAD_DIM), F32)],
        compiler_params=_cparams(2),
        name="rwkv_scan",
    )(r, lw, k, v, a, b, g, r_k.reshape(1, W), lnx_w.reshape(1, W), lnx_b.reshape(1, W))
    return out.reshape(B * S, W)


def _gla_kernel(q_ref, k_ref, v_ref, og_ref, gd_ref, gup_ref, gb_ref, on_ref, o_ref, s_ref):
    c = pl.program_id(1)

    @pl.when(c == 0)
    def _():
        s_ref[...] = jnp.zeros_like(s_ref)

    C = GLA_CHUNK
    row = lax.broadcasted_iota(I32, (C, C), 0)
    col = lax.broadcasted_iota(I32, (C, C), 1)
    lower = row >= col
    z = _dot(gd_ref[...], gup_ref[...]) + gb_ref[...]
    gk = -_softplus(-z) / GLA_GATE_NORM
    cum = _dot_f32(jnp.where(lower, 1.0, 0.0).astype(F32), gk)
    cum_last = cum[C - 1:C, :]
    q = q_ref[...] * (GLA_DK ** -0.5)
    k = k_ref[...]
    qe = (q * jnp.exp(cum)).astype(BF16)
    ke = (k * jnp.exp(-cum)).astype(BF16)
    kd = (k * jnp.exp(cum_last - cum)).astype(BF16)
    e_last = jnp.exp(cum_last)
    v = v_ref[...].astype(BF16)
    og = og_ref[...]
    onorm = on_ref[...]
    heads = range(GLA_HEADS)
    kss = [slice(h * GLA_DK, (h + 1) * GLA_DK) for h in heads]
    vss = [slice(h * GLA_DV, (h + 1) * GLA_DV) for h in heads]
    sts = [s_ref[h] for h in heads]
    atts = [jnp.where(lower, _dot_nt(qe[:, kss[h]], ke[:, kss[h]]), 0.0) for h in heads]
    inters = [_dot_nt(qe[:, kss[h]], sts[h]) for h in heads]
    os_ = [inters[h] + _dot(atts[h], v[:, vss[h]]) for h in heads]
    for h in heads:
        s_ref[h] = sts[h] * e_last[:, kss[h]] + _dot_tn(v[:, vss[h]], kd[:, kss[h]])
    outs = []
    for h in heads:
        gate = og[:, vss[h]]
        outs.append(_rms(os_[h], onorm) * (gate * _sigmoid(gate)))
    o_ref[...] = jnp.concatenate(outs, axis=1).astype(o_ref.dtype)


def _gla(qkvo, gd, gate_up_pad, gate_b, onorm, B, S):
    C = GLA_CHUNK
    KW = GLA_HEADS * GLA_DK
    VW = GLA_HEADS * GLA_DV
    x3 = qkvo.reshape(B, S, qkvo.shape[-1])
    gd3 = gd.reshape(B, S, LANES)
    kb = KW // KW
    out = pl.pallas_call(
        _gla_kernel,
        grid=(B, S // C),
        in_specs=[pl.BlockSpec((None, C, KW), lambda b, c: (b, c, 0)),
                  pl.BlockSpec((None, C, KW), lambda b, c: (b, c, kb)),
                  pl.BlockSpec((None, C, VW), lambda b, c: (b, c, 1)),
                  pl.BlockSpec((None, C, VW), lambda b, c: (b, c, 2)),
                  pl.BlockSpec((None, C, LANES), lambda b, c: (b, c, 0)),
                  pl.BlockSpec((LANES, KW), lambda b, c: (0, 0)),
                  pl.BlockSpec((1, KW), lambda b, c: (0, 0)),
                  pl.BlockSpec((1, GLA_DV), lambda b, c: (0, 0))],
        out_specs=pl.BlockSpec((None, C, VW), lambda b, c: (b, c, 0)),
        out_shape=jax.ShapeDtypeStruct((B, S, VW), BF16),
        scratch_shapes=[pltpu.VMEM((GLA_HEADS, GLA_DV, GLA_DK), F32)],
        compiler_params=_cparams(2),
        name="gla",
    )(x3, x3, x3, x3, gd3, gate_up_pad, gate_b.reshape(1, KW), onorm.reshape(1, GLA_DV))
    return out.reshape(B * S, VW)


def _xattn_kernel(x_ref, g_ref, wq_ref, mk_ref, mv_ref, wo_ref, o_ref):
    x = x_ref[...]
    h = _rms(x, g_ref[...])
    q = _dot(h, wq_ref[...]).astype(BF16)
    mk = mk_ref[...]
    mv = mv_ref[...]
    sls = [slice(hd * XA_HEAD_DIM, (hd + 1) * XA_HEAD_DIM) for hd in range(XA_HEADS)]
    scores = [_dot_nt(q[:, sl], mk[:, sl]) for sl in sls]
    probs = []
    for s in scores:
        s = s * (XA_HEAD_DIM ** -0.5)
        p = jnp.exp(s - jnp.max(s, axis=-1, keepdims=True))
        probs.append((p / jnp.sum(p, axis=-1, keepdims=True)).astype(BF16))
    o = jnp.concatenate([_dot(p, mv[:, sl]) for p, sl in zip(probs, sls)], axis=1)
    o_ref[...] = x + _dot(o, wo_ref[...])


def _xattn(x, g, wq, mk, mv, wo, B, S, tq=256):
    D = x.shape[-1]
    M = mk.shape[0] // B
    XW = mk.shape[-1]
    x3 = x.reshape(B, S, D)
    out = pl.pallas_call(
        _xattn_kernel,
        grid=(B, S // tq),
        in_specs=[pl.BlockSpec((None, tq, D), lambda b, n: (b, n, 0)),
                  pl.BlockSpec((1, D), lambda b, n: (0, 0)),
                  pl.BlockSpec((D, XW), lambda b, n: (0, 0)),
                  pl.BlockSpec((None, M, XW), lambda b, n: (b, 0, 0)),
                  pl.BlockSpec((None, M, XW), lambda b, n: (b, 0, 0)),
                  pl.BlockSpec((XW, D), lambda b, n: (0, 0))],
        out_specs=pl.BlockSpec((None, tq, D), lambda b, n: (b, n, 0)),
        out_shape=jax.ShapeDtypeStruct((B, S, D), F32),
        compiler_params=_cparams(2),
        name="xattn",
    )(x3, g.reshape(1, D), wq, mk.reshape(B, M, XW), mv.reshape(B, M, XW), wo)
    return out.reshape(B * S, D)


def _router_kernel(x_ref, g_ref, w_ref, b_ref, eid_ref, gate_ref):
    h = _rms(x_ref[...], g_ref[...])
    logits = _dot_f32(h, w_ref[...]) + b_ref[...]
    lane = lax.broadcasted_iota(I32, logits.shape, 1)
    big = jnp.int32(LANES)
    neg = -jnp.inf
    gl = jnp.where(lane < MOE_GROUPS, logits, neg)
    gmax = jnp.max(gl, axis=-1, keepdims=True)
    g_top = jnp.min(jnp.where(gl == gmax, lane, big), axis=-1, keepdims=True)
    p_group = 1.0 / jnp.sum(jnp.exp(gl - gmax), axis=-1, keepdims=True)
    lo = MOE_GROUPS + MOE_EXPERTS_PER_GROUP * g_top
    in_group = jnp.where(lane >= lo, jnp.where(lane < lo + MOE_EXPERTS_PER_GROUP, 1, 0), 0) > 0
    el = jnp.where(in_group, logits, neg)
    emax = jnp.max(el, axis=-1, keepdims=True)
    ee = jnp.exp(el - emax)
    prob = ee / jnp.sum(ee, axis=-1, keepdims=True)
    prob = jnp.where(in_group, prob, -1.0)
    p1 = jnp.max(prob, axis=-1, keepdims=True)
    i1 = jnp.min(jnp.where(prob == p1, lane, big), axis=-1, keepdims=True)
    rest = jnp.where(lane == i1, -1.0, prob)
    p2 = jnp.max(rest, axis=-1, keepdims=True)
    i2 = jnp.min(jnp.where(rest == p2, lane, big), axis=-1, keepdims=True)
    tot = p1 + p2
    eid_ref[...] = jnp.where(lane == 0, i1 - MOE_GROUPS, jnp.where(lane == 1, i2 - MOE_GROUPS, 0))
    gate_ref[...] = jnp.where(lane == 0, p_group * p1 / tot, jnp.where(lane == 1, p_group * p2 / tot, 0.0))


def _router(x, g, w_router, b_router, tm=256):
    T, D = x.shape
    return pl.pallas_call(
        _router_kernel,
        grid=(T // tm,),
        in_specs=[pl.BlockSpec((tm, D), lambda i: (i, 0)),
                  pl.BlockSpec((1, D), lambda i: (0, 0)),
                  pl.BlockSpec((D, LANES), lambda i: (0, 0)),
                  pl.BlockSpec((1, LANES), lambda i: (0, 0))],
        out_specs=[pl.BlockSpec((tm, LANES), lambda i: (i, 0))] * 2,
        out_shape=[jax.ShapeDtypeStruct((T, LANES), I32), jax.ShapeDtypeStruct((T, LANES), F32)],
        compiler_params=_cparams(1),
        name="router",
    )(x, g.reshape(1, D), w_router, b_router)


def _gather_rows(src_hbm, idx_ref, dst_ref, sem, n_rows):
    def issue(r, carry):
        pltpu.make_async_copy(src_hbm.at[pl.ds(idx_ref[0, r], 1), :],
                              dst_ref.at[pl.ds(r, 1), :], sem).start()
        return carry
    lax.fori_loop(0, n_rows, issue, 0, unroll=8)


def _wait_rows(src_hbm, dst_ref, sem, n_rows):
    pltpu.make_async_copy(src_hbm.at[pl.ds(0, n_rows), :], dst_ref, sem).wait()


def _moe_expert_kernel(be_ref, tok_ref, tokn_ref, x_hbm, g_ref, w1_ref, w3_ref, w2_ref, o_ref,
                       xbuf, sems, w1b, w3b, w2b):
    i = pl.program_id(0)
    nb = pl.num_programs(0)
    slot = lax.rem(i, 2)

    @pl.when(i == 0)
    def _():
        _gather_rows(x_hbm, tok_ref, xbuf.at[0], sems.at[0], MOE_BLOCK)

    @pl.when(i + 1 < nb)
    def _():
        _gather_rows(x_hbm, tokn_ref, xbuf.at[1 - slot], sems.at[1 - slot], MOE_BLOCK)

    changed = jnp.logical_or(i == 0, be_ref[i] != be_ref[jnp.maximum(i - 1, 0)])

    @pl.when(changed)
    def _():
        w1b[...] = w1_ref[...].astype(BF16)
        w3b[...] = w3_ref[...].astype(BF16)
        w2b[...] = w2_ref[...].astype(BF16)

    _wait_rows(x_hbm, xbuf.at[slot], sems.at[slot], MOE_BLOCK)
    xe = _rms(xbuf[slot], g_ref[...]).astype(BF16)
    h1 = jnp.dot(xe, w1b[...], preferred_element_type=F32)
    h3 = jnp.dot(xe, w3b[...], preferred_element_type=F32)
    act = (h1 * _sigmoid(h1) * h3).astype(BF16)
    o_ref[...] = jnp.dot(act, w2b[...], preferred_element_type=F32)


def _moe_experts(x, g, block_e, buf_tok, w1, w3, w2):
    T, D = x.shape
    FF = w1.shape[-1]
    P = buf_tok.shape[0]
    NB = P // MOE_BLOCK
    tok3 = buf_tok.reshape(NB, 1, MOE_BLOCK)
    grid_spec = pltpu.PrefetchScalarGridSpec(
        num_scalar_prefetch=1,
        grid=(NB,),
        in_specs=[pl.BlockSpec((None, 1, MOE_BLOCK), lambda i, be: (i, 0, 0), memory_space=pltpu.SMEM),
                  pl.BlockSpec((None, 1, MOE_BLOCK), lambda i, be: (jnp.minimum(i + 1, NB - 1), 0, 0),
                               memory_space=pltpu.SMEM),
                  pl.BlockSpec(memory_space=pl.ANY),
                  pl.BlockSpec((1, D), lambda i, be: (0, 0)),
                  pl.BlockSpec((None, D, FF), lambda i, be: (be[i], 0, 0)),
                  pl.BlockSpec((None, D, FF), lambda i, be: (be[i], 0, 0)),
                  pl.BlockSpec((None, FF, D), lambda i, be: (be[i], 0, 0))],
        out_specs=pl.BlockSpec((MOE_BLOCK, D), lambda i, be: (i, 0)),
        scratch_shapes=[pltpu.VMEM((2, MOE_BLOCK, D), F32),
                        pltpu.SemaphoreType.DMA((2,)),
                        pltpu.VMEM((D, FF), BF16),
                        pltpu.VMEM((D, FF), BF16),
                        pltpu.VMEM((FF, D), BF16)],
    )
    return pl.pallas_call(
        _moe_expert_kernel,
        grid_spec=grid_spec,
        out_shape=jax.ShapeDtypeStruct((P, D), F32),
        compiler_params=_cparams(1),
        name="moe_experts",
    )(block_e, tok3, tok3, x, g.reshape(1, D), w1, w3, w2)


def _moe_combine_kernel(pos_ref, posn_ref, x_ref, gate_ref, yb_hbm, gf_ref, o_ref, ybuf, sems, *, tc, final_norm):
    i = pl.program_id(0)
    nb = pl.num_programs(0)
    slot = lax.rem(i, 2)

    @pl.when(i == 0)
    def _():
        _gather_rows(yb_hbm, pos_ref, ybuf.at[0], sems.at[0], 2 * tc)

    @pl.when(i + 1 < nb)
    def _():
        _gather_rows(yb_hbm, posn_ref, ybuf.at[1 - slot], sems.at[1 - slot], 2 * tc)

    _wait_rows(yb_hbm, ybuf.at[slot], sems.at[slot], 2 * tc)
    gate = gate_ref[...]
    y0 = ybuf[slot, 0:tc, :]
    y1 = ybuf[slot, tc:2 * tc, :]
    out = x_ref[...] + (y0 * gate[:, 0:1] + y1 * gate[:, 1:2])
    if final_norm:
        out = _rms(out, gf_ref[...])
    o_ref[...] = out


def _moe_combine(x, gates, pos, yb, g_final, final_norm, tc=128):
    T, D = x.shape
    NT = T // tc
    pos3 = pos.reshape(NT, tc, 2).transpose(0, 2, 1).reshape(NT, 1, 2 * tc)
    return pl.pallas_call(
        functools.partial(_moe_combine_kernel, tc=tc, final_norm=final_norm),
        grid=(NT,),
        in_specs=[pl.BlockSpec((None, 1, 2 * tc), lambda i: (i, 0, 0), memory_space=pltpu.SMEM),
                  pl.BlockSpec((None, 1, 2 * tc), lambda i: (jnp.minimum(i + 1, NT - 1), 0, 0),
                               memory_space=pltpu.SMEM),
                  pl.BlockSpec((tc, D), lambda i: (i, 0)),
                  pl.BlockSpec((tc, LANES), lambda i: (i, 0)),
                  pl.BlockSpec(memory_space=pl.ANY),
                  pl.BlockSpec((1, D), lambda i: (0, 0))],
        out_specs=pl.BlockSpec((tc, D), lambda i: (i, 0)),
        out_shape=jax.ShapeDtypeStruct((T, D), F32),
        scratch_shapes=[pltpu.VMEM((2, 2 * tc, D), F32), pltpu.SemaphoreType.DMA((2,))],
        compiler_params=_cparams(1),
        name="moe_combine",
    )(pos3, pos3, x, gates, yb, g_final.reshape(1, D))


def _moe_slots(eid, T):
    A = 2 * T
    P = A + MOE_EXPERTS * MOE_BLOCK
    NB = P // MOE_BLOCK
    e_flat = eid[:, :2].reshape(A)
    onehot = (e_flat[:, None] == jnp.arange(MOE_EXPERTS, dtype=I32)[None, :]).astype(I32)
    csum = jnp.cumsum(onehot, axis=0)
    counts = csum[-1]
    rank = jnp.sum(csum * onehot, axis=1) - 1
    padded = (counts + MOE_BLOCK - 1) // MOE_BLOCK * MOE_BLOCK
    pends = jnp.cumsum(padded)
    pstarts = pends - padded
    dest = (pstarts[e_flat] + rank).astype(I32)
    buf_tok = jnp.zeros((P,), I32).at[dest].set(jnp.arange(A, dtype=I32) // 2)
    block_e = jnp.minimum(jnp.searchsorted(pends, jnp.arange(NB, dtype=I32) * MOE_BLOCK, side='right'),
                          MOE_EXPERTS - 1).astype(I32)
    return dest.reshape(T, 2), buf_tok, block_e


def _moe_layer(x, g, w_group, b_group, w_expert, b_expert, w1, w3, w2, g_final, final_norm):
    T, D = x.shape
    n_log = MOE_GROUPS + MOE_EXPERTS
    w_router = jnp.zeros((D, LANES), F32).at[:, :MOE_GROUPS].set(w_group).at[:, MOE_GROUPS:n_log].set(w_expert)
    b_router = jnp.zeros((1, LANES), F32).at[0, :MOE_GROUPS].set(b_group).at[0, MOE_GROUPS:n_log].set(b_expert)
    eid, gates = _router(x, g, w_router, b_router)
    pos, buf_tok, block_e = _moe_slots(eid, T)
    yb = _moe_experts(x, g, block_e, buf_tok, w1, w3, w2)
    return _moe_combine(x, gates, pos, yb, g_final, final_norm)


def kernel(x, mem, norm_mix, norm_xattn, norm_moe, norm_final, ev_w_in, ev_sinks, ev_mu, ev_w0, ev_w2, ev_a0, ev_a2, ev_g2, ev_k_k, ev_k_a, ev_r_k, ev_lnx_w, ev_lnx_b, ev_w_out, od_w_in, od_gate_up, od_gate_b, od_onorm, od_w_out, mem_norm, mem_wk, mem_wv, xa_wq, xa_wo, moe_w_group, moe_b_group, moe_w_expert, moe_b_expert, moe_w1, moe_w3, moe_w2):
    B, S, D = x.shape
    M = mem.shape[1]
    T = B * S
    depth = norm_mix.shape[0]
    xf = x.reshape(T, D)

    XW = XA_HEADS * XA_HEAD_DIM
    w_kv = jnp.concatenate([mem_wk, mem_wv], axis=1).astype(BF16)
    mk, mv = _norm_matmul(mem.reshape(B * M, D), mem_norm, w_kv, (XW, XW), (BF16, BF16))

    for layer in range(depth):
        i = layer // 2
        if layer % 2 == 0:
            swa_cols = SWA_Q_HEADS * HEAD_DIM + 2 * (SWA_Q_HEADS // SWA_GROUP) * HEAD_DIM
            rw_cols = ev_w_in.shape[-1] - swa_cols
            qkv, p_rw = _norm_matmul(xf, norm_mix[layer], ev_w_in[i].astype(BF16),
                                     (swa_cols, rw_cols), (F32, F32))
            o_a = _swa(qkv, ev_sinks[i], B, S)
            r, lw, k, v, a, b, g = _rwkv_prep(p_rw, ev_mu[i], ev_w0[i], ev_w2[i], ev_a0[i], ev_a2[i],
                                              ev_g2[i], ev_k_k[i], ev_k_a[i], B, S)
            o_b = _rwkv_scan(r, lw, k, v, a, b, g, ev_r_k[i].reshape(-1), ev_lnx_w[i], ev_lnx_b[i])
            w_out = ev_w_out[i].astype(BF16)
            qw = o_a.shape[-1]
            xf = _proj_residual(xf, [o_a, o_b], [w_out[:qw], w_out[qw:]])
        else:
            KW = GLA_HEADS * GLA_DK
            VW = GLA_HEADS * GLA_DV
            R = od_gate_up.shape[1]
            w = od_w_in[i]
            w_re = jnp.concatenate([w[:, :2 * KW + VW], w[:, 2 * KW + VW + R:],
                                    w[:, 2 * KW + VW:2 * KW + VW + R],
                                    jnp.zeros((D, LANES - R), F32)], axis=1).astype(BF16)
            qkvo, gd = _norm_matmul(xf, norm_mix[layer], w_re, (2 * KW + 2 * VW, LANES), (F32, F32))
            gup = jnp.zeros((LANES, KW), F32).at[:R].set(od_gate_up[i]).astype(BF16)
            o = _gla(qkvo, gd, gup, od_gate_b[i], od_onorm[i], B, S)
            xf = _proj_residual(xf, [o], [od_w_out[i].astype(BF16)])
        xf = _xattn(xf, norm_xattn[layer], xa_wq[layer].astype(BF16), mk, mv, xa_wo[layer].astype(BF16), B, S)
        xf = _moe_layer(xf, norm_moe[layer], moe_w_group[layer], moe_b_group[layer], moe_w_expert[layer],
                        moe_b_expert[layer], moe_w1[layer], moe_w3[layer], moe_w2[layer],
                        norm_final, layer == depth - 1)
    return xf.reshape(B, S, D)
```

```python
import functools

import jax
import jax.numpy as jnp
from jax import lax
from jax.experimental import pallas as pl
from jax.experimental.pallas import tpu as pltpu

F32 = jnp.float32
BF16 = jnp.bfloat16
I32 = jnp.int32

EPS = 1e-6
HEAD_DIM = 64
SWA_WINDOW = 128
SWA_Q_HEADS = 8
SWA_GROUP = 4
RWKV_HEADS = 8
RWKV_WIDTH = 512
RWKV_LN_EPS = 64e-5
RWKV_CHUNK = 64
GLA_HEADS = 4
GLA_DK = 128
GLA_DV = 256
GLA_CHUNK = 64
GLA_GATE_NORM = 16.0
XA_HEADS = 4
XA_HEAD_DIM = 128
MOE_GROUPS = 4
MOE_EXPERTS_PER_GROUP = 8
MOE_EXPERTS = 32
MOE_BLOCK = 128
LANES = 128

VMEM_LIMIT_BYTES = 48 * 1024 * 1024


def _cparams(n_axes):
    return pltpu.CompilerParams(dimension_semantics=("arbitrary",) * n_axes,
                                vmem_limit_bytes=VMEM_LIMIT_BYTES)


def _dot(a, b):
    return jnp.dot(a.astype(BF16), b.astype(BF16), preferred_element_type=F32)


def _dot_nt(a, b):
    return lax.dot_general(a.astype(BF16), b.astype(BF16), (((1,), (1,)), ((), ())),
                           preferred_element_type=F32)


def _dot_tn(a, b):
    return lax.dot_general(a.astype(BF16), b.astype(BF16), (((0,), (0,)), ((), ())),
                           preferred_element_type=F32)


def _dot_f32(a, b):
    return jnp.dot(a, b, preferred_element_type=F32, precision=lax.Precision.HIGHEST)


def _rms(x, g):
    ms = jnp.mean(x * x, axis=-1, keepdims=True)
    return x * lax.rsqrt(ms + EPS) * g


def _sigmoid(x):
    return 1.0 / (1.0 + jnp.exp(-x))


def _softplus(x):
    return jnp.maximum(x, 0.0) + jnp.log(1.0 + jnp.exp(-jnp.abs(x)))


def _norm_matmul_kernel(x_ref, g_ref, w_ref, *o_refs, splits):
    h = _rms(x_ref[...], g_ref[...]).astype(BF16)
    off = 0
    for o_ref, n in zip(o_refs, splits):
        o_ref[...] = jnp.dot(h, w_ref[:, off:off + n], preferred_element_type=F32).astype(o_ref.dtype)
        off += n


def _norm_matmul(x, g, w, splits, out_dtypes, tm=256):
    T, D = x.shape
    N = w.shape[1]
    assert sum(splits) == N and T % tm == 0
    return pl.pallas_call(
        functools.partial(_norm_matmul_kernel, splits=tuple(splits)),
        grid=(T // tm,),
        in_specs=[pl.BlockSpec((tm, D), lambda i: (i, 0)),
                  pl.BlockSpec((1, D), lambda i: (0, 0)),
                  pl.BlockSpec((D, N), lambda i: (0, 0))],
        out_specs=[pl.BlockSpec((tm, n), lambda i: (i, 0)) for n in splits],
        out_shape=[jax.ShapeDtypeStruct((T, n), dt) for n, dt in zip(splits, out_dtypes)],
        compiler_params=_cparams(1),
        name="norm_matmul",
    )(x, g.reshape(1, D), w)


def _proj_residual_kernel(*refs, n_in):
    x_ref = refs[0]
    a_refs = refs[1:1 + n_in]
    w_refs = refs[1 + n_in:1 + 2 * n_in]
    o_ref = refs[1 + 2 * n_in]
    acc = x_ref[...]
    for a_ref, w_ref in zip(a_refs, w_refs):
        acc = acc + jnp.dot(a_ref[...], w_ref[...], preferred_element_type=F32)
    o_ref[...] = acc


def _proj_residual(x, acts, weights, tm=512):
    T, D = x.shape
    n_in = len(acts)
    in_specs = [pl.BlockSpec((tm, D), lambda i: (i, 0))]
    in_specs += [pl.BlockSpec((tm, a.shape[1]), lambda i: (i, 0)) for a in acts]
    in_specs += [pl.BlockSpec(w.shape, lambda i: (0, 0)) for w in weights]
    return pl.pallas_call(
        functools.partial(_proj_residual_kernel, n_in=n_in),
        grid=(T // tm,),
        in_specs=in_specs,
        out_specs=pl.BlockSpec((tm, D), lambda i: (i, 0)),
        out_shape=jax.ShapeDtypeStruct((T, D), F32),
        compiler_params=_cparams(1),
        name="proj_residual",
    )(x, *acts, *weights)


def _swa_kernel(sinks_ref, q_ref, kp_ref, kc_ref, vp_ref, vc_ref, o_ref):
    n = pl.program_id(1)
    W = SWA_WINDOW
    q = q_ref[...]
    k = jnp.concatenate([kp_ref[...], kc_ref[...]], axis=0)
    v = jnp.concatenate([vp_ref[...], vc_ref[...]], axis=0)
    qpos = lax.broadcasted_iota(I32, (W, 2 * W), 0) + W
    kpos = lax.broadcasted_iota(I32, (W, 2 * W), 1)
    rel = qpos - kpos
    in_window = jnp.where(rel >= 0, jnp.where(rel < W, 1, 0), 0)
    has_prev = jnp.where(n > 0, 1, 0)
    valid = (in_window * jnp.where(kpos >= W, 1, has_prev)) > 0
    n_groups = SWA_Q_HEADS // SWA_GROUP
    qb = q.astype(BF16)
    kb = k.astype(BF16)
    vb = v.astype(BF16)
    scores = []
    for g in range(n_groups):
        qg = jnp.concatenate([qb[:, h * HEAD_DIM:(h + 1) * HEAD_DIM]
                              for h in range(g * SWA_GROUP, (g + 1) * SWA_GROUP)], axis=0)
        scores.append(_dot_nt(qg, kb[:, g * HEAD_DIM:(g + 1) * HEAD_DIM]))
    probs = []
    for g in range(n_groups):
        pieces = []
        for j in range(SWA_GROUP):
            s = jnp.where(valid, scores[g][j * W:(j + 1) * W] * (HEAD_DIM ** -0.5), -jnp.inf)
            sink = sinks_ref[g * SWA_GROUP + j]
            m = jnp.maximum(jnp.max(s, axis=-1, keepdims=True), sink)
            p = jnp.exp(s - m)
            den = jnp.sum(p, axis=-1, keepdims=True) + jnp.exp(sink - m)
            pieces.append((p / den).astype(BF16))
        probs.append(jnp.concatenate(pieces, axis=0))
    outs = []
    for g in range(n_groups):
        og = _dot(probs[g], vb[:, g * HEAD_DIM:(g + 1) * HEAD_DIM])
        outs += [og[j * W:(j + 1) * W] for j in range(SWA_GROUP)]
    o_ref[...] = jnp.concatenate(outs, axis=1).astype(o_ref.dtype)


def _swa(qkv, sinks, B, S):
    W = SWA_WINDOW
    qkv3 = qkv.reshape(B, S, qkv.shape[-1])
    qw = SWA_Q_HEADS * HEAD_DIM
    kw = qw // SWA_GROUP
    kcol = qw // kw
    out = pl.pallas_call(
        _swa_kernel,
        grid=(B, S // W),
        in_specs=[pl.BlockSpec(memory_space=pltpu.SMEM),
                  pl.BlockSpec((None, W, qw), lambda b, n: (b, n, 0)),
                  pl.BlockSpec((None, W, kw), lambda b, n: (b, jnp.maximum(n - 1, 0), kcol)),
                  pl.BlockSpec((None, W, kw), lambda b, n: (b, n, kcol)),
                  pl.BlockSpec((None, W, kw), lambda b, n: (b, jnp.maximum(n - 1, 0), kcol + 1)),
                  pl.BlockSpec((None, W, kw), lambda b, n: (b, n, kcol + 1))],
        out_specs=pl.BlockSpec((None, W, qw), lambda b, n: (b, n, 0)),
        out_shape=jax.ShapeDtypeStruct((B, S, qw), BF16),
        compiler_params=_cparams(2),
        name="swa",
    )(sinks, qkv3, qkv3, qkv3, qkv3, qkv3)
    return out.reshape(B * S, qw)


def _rwkv_prep_kernel(p_ref, pprev_ref, mu_ref, w0_ref, w2_ref, a0_ref, a2_ref, g2_ref, kk_ref, ka_ref,
                      r_out, lw_out, k_out, v_out, a_out, b_out, g_out):
    n = pl.program_id(1)
    C = RWKV_WIDTH
    p = p_ref[...]
    last = jnp.where(n > 0, pprev_ref[7:8, :], 0.0)
    row = lax.broadcasted_iota(I32, p.shape, 0)
    p_prev = jnp.where(row == 0, last, pltpu.roll(p, 1, axis=0))
    p = p + (p_prev - p) * mu_ref[...]
    r = p[:, :C]
    k = p[:, C:2 * C]
    v = p[:, 2 * C:3 * C]
    xw = p[:, 3 * C:3 * C + 64]
    xa = p[:, 3 * C + 64:3 * C + 128]
    xg = p[:, 3 * C + 128:]
    w = -_softplus(-(w0_ref[...] + _dot(jnp.tanh(xw), w2_ref[...]))) - 0.5
    lw = -jnp.exp(w)
    a = _sigmoid(a0_ref[...] + _dot(xa, a2_ref[...]))
    g = _dot(_sigmoid(xg), g2_ref[...])
    kk = k * kk_ref[...]
    pieces = []
    for h in range(RWKV_HEADS):
        kh = kk[:, h * HEAD_DIM:(h + 1) * HEAD_DIM]
        nrm = jnp.sqrt(jnp.sum(kh * kh, axis=-1, keepdims=True))
        pieces.append(kh / jnp.maximum(nrm, 1e-12))
    kk = jnp.concatenate(pieces, axis=1)
    r_out[...] = r
    lw_out[...] = lw
    k_out[...] = k * (1.0 + (a - 1.0) * ka_ref[...])
    v_out[...] = v
    a_out[...] = -kk
    b_out[...] = kk * a
    g_out[...] = g


def _rwkv_prep(p, mu, w0, w2, a0, a2, g2, k_k, k_a, B, S, tt=256):
    C = RWKV_WIDTH
    PW = p.shape[-1]
    p3 = p.reshape(B, S, PW)
    row = lambda t: t.reshape(1, -1)
    full = lambda arr: pl.BlockSpec(arr.shape, lambda b, n: (0,) * arr.ndim)
    params = [row(mu), row(w0), w2.astype(BF16), row(a0), a2.astype(BF16), g2.astype(BF16), row(k_k), row(k_a)]
    outs = pl.pallas_call(
        _rwkv_prep_kernel,
        grid=(B, S // tt),
        in_specs=[pl.BlockSpec((None, tt, PW), lambda b, n: (b, n, 0)),
                  pl.BlockSpec((None, 8, PW), lambda b, n: (b, jnp.maximum(n * (tt // 8) - 1, 0), 0))]
                 + [full(t) for t in params],
        out_specs=[pl.BlockSpec((None, tt, C), lambda b, n: (b, n, 0))] * 7,
        out_shape=[jax.ShapeDtypeStruct((B, S, C), F32)] * 7,
        compiler_params=_cparams(2),
        name="rwkv_prep",
    )(p3, p3, *params)
    return outs


def _rwkv_scan_kernel(r_ref, lw_ref, k_ref, v_ref, a_ref, b_ref, g_ref, rk_ref, lnw_ref, lnb_ref,
                      o_ref, s_ref):
    c = pl.program_id(1)

    @pl.when(c == 0)
    def _():
        s_ref[...] = jnp.zeros_like(s_ref)

    C = RWKV_CHUNK
    row = lax.broadcasted_iota(I32, (C, C), 0)
    col = lax.broadcasted_iota(I32, (C, C), 1)
    lower = row >= col
    strict = row > col

    lw = lw_ref[...]
    cum = _dot_f32(jnp.where(lower, 1.0, 0.0).astype(F32), lw)
    cum_last = cum[C - 1:C, :]
    r = r_ref[...]
    k = k_ref[...]
    v = v_ref[...]
    a = a_ref[...]
    b = b_ref[...]
    g = g_ref[...]
    e_neg = jnp.exp(-cum)
    e_rem = jnp.exp(cum_last - cum)
    r_t = (r * jnp.exp(cum)).astype(BF16)
    a_t = (a * jnp.exp(cum - lw)).astype(BF16)
    b_t = (b * e_neg).astype(BF16)
    k_t = (k * e_neg).astype(BF16)
    b_d = (b * e_rem).astype(BF16)
    k_d = (k * e_rem).astype(BF16)
    v_b = v.astype(BF16)
    e_last = jnp.exp(cum_last)
    rkk = r * k * rk_ref[...]
    lnw = lnw_ref[...]
    lnb = lnb_ref[...]
    row2 = lax.broadcasted_iota(I32, (C, 2 * C), 0)
    col2 = lax.broadcasted_iota(I32, (C, 2 * C), 1)
    colm = jnp.where(col2 >= C, col2 - C, col2)
    lower2 = row2 >= colm
    strict_k = jnp.where(col2 >= C, jnp.where(row2 > colm, 1, 0), 0) > 0

    heads = range(RWKV_HEADS)
    sls = [slice(h * HEAD_DIM, (h + 1) * HEAD_DIM) for h in heads]
    s0s = [s_ref[h] for h in heads]
    ars_l = [jnp.concatenate([a_t[:, sl], r_t[:, sl]], axis=0) for sl in sls]
    bks = [jnp.concatenate([b_t[:, sl], k_t[:, sl]], axis=0) for sl in sls]
    ms = [_dot_nt(ars_l[h], bks[h]) for h in heads]
    arss = [_dot_nt(ars_l[h], s0s[h]) for h in heads]
    vhs = [v_b[:, sl] for sl in sls]
    xs = [arss[h][:C] + _dot(jnp.where(strict_k, ms[h][:C], 0.0), jnp.concatenate([vhs[h], vhs[h]], axis=0))
          for h in heads]
    pws = [jnp.where(strict, ms[h][:C, :C], 0.0).astype(BF16) for h in heads]
    xs = [xs[h] + _dot(pws[h], xs[h]) for h in heads]
    for _ in range(5):
        pws = [_dot(pws[h], pws[h]).astype(BF16) for h in heads]
        xs = [xs[h] + _dot(pws[h], xs[h]) for h in heads]
    uvs = [jnp.concatenate([xs[h].astype(BF16), vhs[h]], axis=0) for h in heads]
    ys = [arss[h][C:] + _dot(jnp.where(lower2, ms[h][C:], 0.0), uvs[h]) for h in heads]
    for h in heads:
        bkd = jnp.concatenate([b_d[:, sls[h]], k_d[:, sls[h]]], axis=0)
        s_ref[h] = s0s[h] * e_last[:, sls[h]] + _dot_tn(uvs[h], bkd)

    outs = []
    for h in heads:
        sl = sls[h]
        y = ys[h]
        mean = jnp.mean(y, axis=-1, keepdims=True)
        yc = y - mean
        var = jnp.mean(yc * yc, axis=-1, keepdims=True)
        yn = yc * lax.rsqrt(var + RWKV_LN_EPS) * lnw[:, sl] + lnb[:, sl]
        bonus = jnp.sum(rkk[:, sl], axis=-1, keepdims=True) * v[:, sl]
        outs.append((yn + bonus) * g[:, sl])
    o_ref[...] = jnp.concatenate(outs, axis=1).astype(o_ref.dtype)


def _rwkv_scan(r, lw, k, v, a, b, g, r_k, lnx_w, lnx_b):
    B, S, W = r.shape
    C = RWKV_CHUNK
    seq = pl.BlockSpec((None, C, W), lambda bb, c: (bb, c, 0))
    par = pl.BlockSpec((1, W), lambda bb, c: (0, 0))
    out = pl.pallas_call(
        _rwkv_scan_kernel,
        grid=(B, S // C),
        in_specs=[seq] * 7 + [par] * 3,
        out_specs=seq,
        out_shape=jax.ShapeDtypeStruct((B, S, W), BF16),
        scratch_shapes=[pltpu.VMEM((RWKV_HEADS, HEAD_DIM, HEAD_DIM), F32)],
        compiler_params=_cparams(2),
        name="rwkv_scan",
    )(r, lw, k, v, a, b, g, r_k.reshape(1, W), lnx_w.reshape(1, W), lnx_b.reshape(1, W))
    return out.reshape(B * S, W)


def _gla_kernel(q_ref, k_ref, v_ref, og_ref, gd_ref, gup_ref, gb_ref, on_ref, o_ref, s_ref):
    c = pl.program_id(1)

    @pl.when(c == 0)
    def _():
        s_ref[...] = jnp.zeros_like(s_ref)

    C = GLA_CHUNK
    row = lax.broadcasted_iota(I32, (C, C), 0)
    col = lax.broadcasted_iota(I32, (C, C), 1)
    lower = row >= col
    z = _dot(gd_ref[...], gup_ref[...]) + gb_ref[...]
    gk = -_softplus(-z) / GLA_GATE_NORM
    cum = _dot_f32(jnp.where(lower, 1.0, 0.0).astype(F32), gk)
    cum_last = cum[C - 1:C, :]
    q = q_ref[...] * (GLA_DK ** -0.5)
    k = k_ref[...]
    qe = (q * jnp.exp(cum)).astype(BF16)
    ke = (k * jnp.exp(-cum)).astype(BF16)
    kd = (k * jnp.exp(cum_last - cum)).astype(BF16)
    e_last = jnp.exp(cum_last)
    v = v_ref[...].astype(BF16)
    og = og_ref[...]
    onorm = on_ref[...]
    heads = range(GLA_HEADS)
    kss = [slice(h * GLA_DK, (h + 1) * GLA_DK) for h in heads]
    vss = [slice(h * GLA_DV, (h + 1) * GLA_DV) for h in heads]
    sts = [s_ref[h] for h in heads]
    atts = [jnp.where(lower, _dot_nt(qe[:, kss[h]], ke[:, kss[h]]), 0.0) for h in heads]
    inters = [_dot_nt(qe[:, kss[h]], sts[h]) for h in heads]
    os_ = [inters[h] + _dot(atts[h], v[:, vss[h]]) for h in heads]
    for h in heads:
        s_ref[h] = sts[h] * e_last[:, kss[h]] + _dot_tn(v[:, vss[h]], kd[:, kss[h]])
    outs = []
    for h in heads:
        gate = og[:, vss[h]]
        outs.append(_rms(os_[h], onorm) * (gate * _sigmoid(gate)))
    o_ref[...] = jnp.concatenate(outs, axis=1).astype(o_ref.dtype)


def _gla(qkvo, gd, gate_up_pad, gate_b, onorm, B, S):
    C = GLA_CHUNK
    KW = GLA_HEADS * GLA_DK
    VW = GLA_HEADS * GLA_DV
    x3 = qkvo.reshape(B, S, qkvo.shape[-1])
    gd3 = gd.reshape(B, S, LANES)
    kb = KW // KW
    out = pl.pallas_call(
        _gla_kernel,
        grid=(B, S // C),
        in_specs=[pl.BlockSpec((None, C, KW), lambda b, c: (b, c, 0)),
                  pl.BlockSpec((None, C, KW), lambda b, c: (b, c, kb)),
                  pl.BlockSpec((None, C, VW), lambda b, c: (b, c, 1)),
                  pl.BlockSpec((None, C, VW), lambda b, c: (b, c, 2)),
                  pl.BlockSpec((None, C, LANES), lambda b, c: (b, c, 0)),
                  pl.BlockSpec((LANES, KW), lambda b, c: (0, 0)),
                  pl.BlockSpec((1, KW), lambda b, c: (0, 0)),
                  pl.BlockSpec((1, GLA_DV), lambda b, c: (0, 0))],
        out_specs=pl.BlockSpec((None, C, VW), lambda b, c: (b, c, 0)),
        out_shape=jax.ShapeDtypeStruct((B, S, VW), BF16),
        scratch_shapes=[pltpu.VMEM((GLA_HEADS, GLA_DV, GLA_DK), F32)],
        compiler_params=_cparams(2),
        name="gla",
    )(x3, x3, x3, x3, gd3, gate_up_pad, gate_b.reshape(1, KW), onorm.reshape(1, GLA_DV))
    return out.reshape(B * S, VW)


def _xattn_kernel(x_ref, g_ref, wq_ref, mk_ref, mv_ref, wo_ref, o_ref):
    x = x_ref[...]
    h = _rms(x, g_ref[...])
    q = _dot(h, wq_ref[...]).astype(BF16)
    mk = mk_ref[...]
    mv = mv_ref[...]
    sls = [slice(hd * XA_HEAD_DIM, (hd + 1) * XA_HEAD_DIM) for hd in range(XA_HEADS)]
    scores = [_dot_nt(q[:, sl], mk[:, sl]) for sl in sls]
    probs = []
    for s in scores:
        s = s * (XA_HEAD_DIM ** -0.5)
        p = jnp.exp(s - jnp.max(s, axis=-1, keepdims=True))
        probs.append((p / jnp.sum(p, axis=-1, keepdims=True)).astype(BF16))
    o = jnp.concatenate([_dot(p, mv[:, sl]) for p, sl in zip(probs, sls)], axis=1)
    o_ref[...] = x + _dot(o, wo_ref[...])


def _xattn(x, g, wq, mk, mv, wo, B, S, tq=256):
    D = x.shape[-1]
    M = mk.shape[0] // B
    XW = mk.shape[-1]
    x3 = x.reshape(B, S, D)
    out = pl.pallas_call(
        _xattn_kernel,
        grid=(B, S // tq),
        in_specs=[pl.BlockSpec((None, tq, D), lambda b, n: (b, n, 0)),
                  pl.BlockSpec((1, D), lambda b, n: (0, 0)),
                  pl.BlockSpec((D, XW), lambda b, n: (0, 0)),
                  pl.BlockSpec((None, M, XW), lambda b, n: (b, 0, 0)),
                  pl.BlockSpec((None, M, XW), lambda b, n: (b, 0, 0)),
                  pl.BlockSpec((XW, D), lambda b, n: (0, 0))],
        out_specs=pl.BlockSpec((None, tq, D), lambda b, n: (b, n, 0)),
        out_shape=jax.ShapeDtypeStruct((B, S, D), F32),
        compiler_params=_cparams(2),
        name="xattn",
    )(x3, g.reshape(1, D), wq, mk.reshape(B, M, XW), mv.reshape(B, M, XW), wo)
    return out.reshape(B * S, D)


def _router_kernel(x_ref, g_ref, w_ref, b_ref, info_ref, cnt_ref, carry_ref):
    i = pl.program_id(0)

    @pl.when(i == 0)
    def _():
        carry_ref[...] = jnp.zeros_like(carry_ref)

    h = _rms(x_ref[...], g_ref[...])
    logits = _dot_f32(h, w_ref[...]) + b_ref[...]
    tm = logits.shape[0]
    lane = lax.broadcasted_iota(I32, logits.shape, 1)
    big = jnp.int32(LANES)
    neg = -jnp.inf
    gl = jnp.where(lane < MOE_GROUPS, logits, neg)
    gmax = jnp.max(gl, axis=-1, keepdims=True)
    g_top = jnp.min(jnp.where(gl == gmax, lane, big), axis=-1, keepdims=True)
    p_group = 1.0 / jnp.sum(jnp.exp(gl - gmax), axis=-1, keepdims=True)
    lo = MOE_GROUPS + MOE_EXPERTS_PER_GROUP * g_top
    in_group = jnp.where(lane >= lo, jnp.where(lane < lo + MOE_EXPERTS_PER_GROUP, 1, 0), 0) > 0
    el = jnp.where(in_group, logits, neg)
    emax = jnp.max(el, axis=-1, keepdims=True)
    ee = jnp.exp(el - emax)
    prob = ee / jnp.sum(ee, axis=-1, keepdims=True)
    prob = jnp.where(in_group, prob, -1.0)
    p1 = jnp.max(prob, axis=-1, keepdims=True)
    i1 = jnp.min(jnp.where(prob == p1, lane, big), axis=-1, keepdims=True)
    rest = jnp.where(lane == i1, -1.0, prob)
    p2 = jnp.max(rest, axis=-1, keepdims=True)
    i2 = jnp.min(jnp.where(rest == p2, lane, big), axis=-1, keepdims=True)
    tot = p1 + p2
    e1 = i1 - MOE_GROUPS
    e2 = i2 - MOE_GROUPS
    oh1 = jnp.where(lane == e1, 1.0, 0.0)
    oh2 = jnp.where(lane == e2, 1.0, 0.0)
    row = lax.broadcasted_iota(I32, (tm, tm), 0)
    col = lax.broadcasted_iota(I32, (tm, tm), 1)
    before = jnp.where(row > col, 1.0, 0.0)
    pre = _dot(before, jnp.concatenate([oh1, oh2], axis=1))
    tot1 = jnp.sum(oh1, axis=0, keepdims=True)
    tot2 = jnp.sum(oh2, axis=0, keepdims=True)
    carry = carry_ref[...]
    r1 = jnp.sum(oh1 * (carry + pre[:, :LANES]), axis=-1, keepdims=True)
    r2 = jnp.sum(oh2 * (carry + tot1 + pre[:, LANES:]), axis=-1, keepdims=True)
    carry = carry + tot1 + tot2
    carry_ref[...] = carry
    cnt_ref[...] = carry
    g1 = p_group * p1 / tot
    g2 = p_group * p2 / tot
    vals = [e1.astype(F32), e2.astype(F32), r1, r2, g1, g2]
    info = jnp.zeros_like(logits)
    for j, val in enumerate(vals):
        info = jnp.where(lane == j, val, info)
    info_ref[...] = info


def _router(x, g, w_router, b_router, tm=256):
    T, D = x.shape
    return pl.pallas_call(
        _router_kernel,
        grid=(T // tm,),
        in_specs=[pl.BlockSpec((tm, D), lambda i: (i, 0)),
                  pl.BlockSpec((1, D), lambda i: (0, 0)),
                  pl.BlockSpec((D, LANES), lambda i: (0, 0)),
                  pl.BlockSpec((1, LANES), lambda i: (0, 0))],
        out_specs=[pl.BlockSpec((tm, LANES), lambda i: (i, 0)),
                   pl.BlockSpec((1, LANES), lambda i: (0, 0))],
        out_shape=[jax.ShapeDtypeStruct((T, LANES), F32), jax.ShapeDtypeStruct((1, LANES), F32)],
        scratch_shapes=[pltpu.VMEM((1, LANES), F32)],
        compiler_params=_cparams(1),
        name="router",
    )(x, g.reshape(1, D), w_router, b_router)


def _row_bytes_wait(hbm, buf, sem):
    pltpu.make_async_copy(buf, hbm.at[pl.ds(0, buf.shape[0]), :], sem).wait()


def _moe_dispatch_kernel(pends_ref, cnt_ref, dest_ref, x_ref, g_ref, hs_hbm, hbuf, zbuf, sems, zsem, *, td):
    i = pl.program_id(0)
    nt = pl.num_programs(0)
    slot = lax.rem(i, 2)

    @pl.when(i == 0)
    def _():
        zbuf[...] = jnp.zeros_like(zbuf)
        for e in range(MOE_EXPERTS):
            @pl.when(cnt_ref[e] > 0)
            def _():
                start = pl.multiple_of(pends_ref[e] - MOE_BLOCK, MOE_BLOCK)
                pltpu.make_async_copy(zbuf, hs_hbm.at[pl.ds(start, MOE_BLOCK), :], zsem).start()
        for e in range(MOE_EXPERTS):
            @pl.when(cnt_ref[e] > 0)
            def _():
                pltpu.make_async_copy(zbuf, hs_hbm.at[pl.ds(0, MOE_BLOCK), :], zsem).wait()

        first_unused = pends_ref[MOE_EXPERTS - 1] // MOE_BLOCK
        n_blocks = hs_hbm.shape[0] // MOE_BLOCK

        def zero_start(blk, carry):
            start = pl.multiple_of(blk * MOE_BLOCK, MOE_BLOCK)
            pltpu.make_async_copy(zbuf, hs_hbm.at[pl.ds(start, MOE_BLOCK), :], zsem).start()
            return carry

        def zero_wait(blk, carry):
            pltpu.make_async_copy(zbuf, hs_hbm.at[pl.ds(0, MOE_BLOCK), :], zsem).wait()
            return carry

        lax.fori_loop(first_unused, n_blocks, zero_start, 0)
        lax.fori_loop(first_unused, n_blocks, zero_wait, 0)

    hb = hbuf.at[slot]
    hb[...] = _rms(x_ref[...], g_ref[...])
    for j in range(td):
        for c in range(2):
            pltpu.make_async_copy(hb.at[pl.ds(j, 1), :],
                                  hs_hbm.at[pl.ds(dest_ref[0, c * td + j], 1), :],
                                  sems.at[slot]).start(priority=c)

    @pl.when(i > 0)
    def _():
        other = hbuf.at[1 - slot]
        _row_bytes_wait(hs_hbm, other, sems.at[1 - slot])
        _row_bytes_wait(hs_hbm, other, sems.at[1 - slot])

    @pl.when(i == nt - 1)
    def _():
        _row_bytes_wait(hs_hbm, hb, sems.at[slot])
        _row_bytes_wait(hs_hbm, hb, sems.at[slot])


def _moe_dispatch(x, g, pends, counts, dest3, P, td):
    T, D = x.shape
    grid_spec = pltpu.PrefetchScalarGridSpec(
        num_scalar_prefetch=2,
        grid=(T // td,),
        in_specs=[pl.BlockSpec((None, 1, 2 * td), lambda i, pe, cn: (i, 0, 0), memory_space=pltpu.SMEM),
                  pl.BlockSpec((td, D), lambda i, pe, cn: (i, 0)),
                  pl.BlockSpec((1, D), lambda i, pe, cn: (0, 0))],
        out_specs=pl.BlockSpec(memory_space=pl.ANY),
        scratch_shapes=[pltpu.VMEM((2, td, D), F32),
                        pltpu.VMEM((MOE_BLOCK, D), F32),
                        pltpu.SemaphoreType.DMA((2,)),
                        pltpu.SemaphoreType.DMA(())],
    )
    return pl.pallas_call(
        functools.partial(_moe_dispatch_kernel, td=td),
        grid_spec=grid_spec,
        out_shape=jax.ShapeDtypeStruct((P, D), F32),
        compiler_params=_cparams(1),
        name="moe_dispatch",
    )(pends, counts, dest3, x, g.reshape(1, D))


def _moe_expert_kernel(be_ref, nu_ref, hs_ref, w1_ref, w3_ref, w2_ref, o_ref, w1b, w3b, w2b):
    i = pl.program_id(0)
    used = i < nu_ref[0]
    changed = jnp.logical_or(i == 0, be_ref[i] != be_ref[jnp.maximum(i - 1, 0)])

    @pl.when(jnp.logical_and(used, changed))
    def _():
        w1b[...] = w1_ref[...].astype(BF16)
        w3b[...] = w3_ref[...].astype(BF16)
        w2b[...] = w2_ref[...].astype(BF16)

    @pl.when(used)
    def _():
        xe = hs_ref[...].astype(BF16)
        h1 = jnp.dot(xe, w1b[...], preferred_element_type=F32)
        h3 = jnp.dot(xe, w3b[...], preferred_element_type=F32)
        act = (h1 * _sigmoid(h1) * h3).astype(BF16)
        o_ref[...] = jnp.dot(act, w2b[...], preferred_element_type=F32)

    @pl.when(jnp.logical_not(used))
    def _():
        o_ref[...] = jnp.zeros_like(o_ref)


def _moe_experts(hs, block_e, n_used, w1, w3, w2):
    P, D = hs.shape
    FF = w1.shape[-1]
    NB = P // MOE_BLOCK
    last = lambda i, nu: jnp.minimum(i, nu[0] - 1)
    grid_spec = pltpu.PrefetchScalarGridSpec(
        num_scalar_prefetch=2,
        grid=(NB,),
        in_specs=[pl.BlockSpec((MOE_BLOCK, D), lambda i, be, nu: (last(i, nu), 0)),
                  pl.BlockSpec((None, D, FF), lambda i, be, nu: (be[last(i, nu)], 0, 0)),
                  pl.BlockSpec((None, D, FF), lambda i, be, nu: (be[last(i, nu)], 0, 0)),
                  pl.BlockSpec((None, FF, D), lambda i, be, nu: (be[last(i, nu)], 0, 0))],
        out_specs=pl.BlockSpec((MOE_BLOCK, D), lambda i, be, nu: (i, 0)),
        scratch_shapes=[pltpu.VMEM((D, FF), BF16),
                        pltpu.VMEM((D, FF), BF16),
                        pltpu.VMEM((FF, D), BF16)],
    )
    return pl.pallas_call(
        _moe_expert_kernel,
        grid_spec=grid_spec,
        out_shape=jax.ShapeDtypeStruct((P, D), F32),
        compiler_params=_cparams(1),
        name="moe_experts",
    )(block_e, n_used, hs, w1, w3, w2)


def _gather_rows(src_hbm, idx_ref, dst_ref, sem, n_rows):
    for r in range(n_rows):
        pltpu.make_async_copy(src_hbm.at[pl.ds(idx_ref[0, r], 1), :],
                              dst_ref.at[pl.ds(r, 1), :], sem).start(priority=r % 2)


def _moe_combine_kernel(pos_ref, posn_ref, x_ref, info_ref, yb_hbm, gf_ref, o_ref, ybuf, sems, *, tc, final_norm):
    i = pl.program_id(0)
    nb = pl.num_programs(0)
    slot = lax.rem(i, 2)

    @pl.when(i == 0)
    def _():
        def issue(r, carry):
            pltpu.make_async_copy(yb_hbm.at[pl.ds(pos_ref[0, r], 1), :],
                                  ybuf.at[0, pl.ds(r, 1), :], sems.at[0]).start()
            return carry
        lax.fori_loop(0, 2 * tc, issue, 0)

    @pl.when(i + 1 < nb)
    def _():
        _gather_rows(yb_hbm, posn_ref, ybuf.at[1 - slot], sems.at[1 - slot], 2 * tc)

    pltpu.make_async_copy(yb_hbm.at[pl.ds(0, 2 * tc), :], ybuf.at[slot], sems.at[slot]).wait()
    info = info_ref[...]
    y0 = ybuf[slot, 0:tc, :]
    y1 = ybuf[slot, tc:2 * tc, :]
    out = x_ref[...] + (y0 * info[:, 4:5] + y1 * info[:, 5:6])
    if final_norm:
        out = _rms(out, gf_ref[...])
    o_ref[...] = out


def _moe_combine(x, info, dest3, yb, g_final, final_norm, tc):
    T, D = x.shape
    NT = T // tc
    return pl.pallas_call(
        functools.partial(_moe_combine_kernel, tc=tc, final_norm=final_norm),
        grid=(NT,),
        in_specs=[pl.BlockSpec((None, 1, 2 * tc), lambda i: (i, 0, 0), memory_space=pltpu.SMEM),
                  pl.BlockSpec((None, 1, 2 * tc), lambda i: (jnp.minimum(i + 1, NT - 1), 0, 0),
                               memory_space=pltpu.SMEM),
                  pl.BlockSpec((tc, D), lambda i: (i, 0)),
                  pl.BlockSpec((tc, LANES), lambda i: (i, 0)),
                  pl.BlockSpec(memory_space=pl.ANY),
                  pl.BlockSpec((1, D), lambda i: (0, 0))],
        out_specs=pl.BlockSpec((tc, D), lambda i: (i, 0)),
        out_shape=jax.ShapeDtypeStruct((T, D), F32),
        scratch_shapes=[pltpu.VMEM((2, 2 * tc, D), F32), pltpu.SemaphoreType.DMA((2,))],
        compiler_params=_cparams(1),
        name="moe_combine",
    )(dest3, dest3, x, info, yb, g_final.reshape(1, D))


MOE_TILE = 128


def _moe_layer(x, g, w_group, b_group, w_expert, b_expert, w1, w3, w2, g_final, final_norm):
    T, D = x.shape
    n_log = MOE_GROUPS + MOE_EXPERTS
    w_router = jnp.zeros((D, LANES), F32).at[:, :MOE_GROUPS].set(w_group).at[:, MOE_GROUPS:n_log].set(w_expert)
    b_router = jnp.zeros((1, LANES), F32).at[0, :MOE_GROUPS].set(b_group).at[0, MOE_GROUPS:n_log].set(b_expert)
    info, cnt = _router(x, g, w_router, b_router)
    P = 2 * T + MOE_EXPERTS * MOE_BLOCK
    NB = P // MOE_BLOCK
    counts = cnt[0, :MOE_EXPERTS].astype(I32)
    padded = (counts + MOE_BLOCK - 1) // MOE_BLOCK * MOE_BLOCK
    pends = jnp.cumsum(padded).astype(I32)
    pstarts = pends - padded
    block_e = jnp.minimum(jnp.searchsorted(pends, jnp.arange(NB, dtype=I32) * MOE_BLOCK, side='right'),
                          MOE_EXPERTS - 1).astype(I32)
    n_used = (pends[-1:] // MOE_BLOCK).astype(I32)
    eid = info[:, 0:2].astype(I32)
    dest = pstarts[eid] + info[:, 2:4].astype(I32)
    NT = T // MOE_TILE
    dest3 = dest.reshape(NT, MOE_TILE, 2).transpose(0, 2, 1).reshape(NT, 1, 2 * MOE_TILE)
    hs = _moe_dispatch(x, g, pends, counts, dest3, P, MOE_TILE)
    yb = _moe_experts(hs, block_e, n_used, w1, w3, w2)
    return _moe_combine(x, info, dest3, yb, g_final, final_norm, MOE_TILE)


def kernel(x, mem, norm_mix, norm_xattn, norm_moe, norm_final, ev_w_in, ev_sinks, ev_mu, ev_w0, ev_w2, ev_a0, ev_a2, ev_g2, ev_k_k, ev_k_a, ev_r_k, ev_lnx_w, ev_lnx_b, ev_w_out, od_w_in, od_gate_up, od_gate_b, od_onorm, od_w_out, mem_norm, mem_wk, mem_wv, xa_wq, xa_wo, moe_w_group, moe_b_group, moe_w_expert, moe_b_expert, moe_w1, moe_w3, moe_w2):
    B, S, D = x.shape
    M = mem.shape[1]
    T = B * S
    depth = norm_mix.shape[0]
    xf = x.reshape(T, D)

    XW = XA_HEADS * XA_HEAD_DIM
    w_kv = jnp.concatenate([mem_wk, mem_wv], axis=1).astype(BF16)
    mk, mv = _norm_matmul(mem.reshape(B * M, D), mem_norm, w_kv, (XW, XW), (BF16, BF16))

    for layer in range(depth):
        i = layer // 2
        if layer % 2 == 0:
            swa_cols = SWA_Q_HEADS * HEAD_DIM + 2 * (SWA_Q_HEADS // SWA_GROUP) * HEAD_DIM
            rw_cols = ev_w_in.shape[-1] - swa_cols
            qkv, p_rw = _norm_matmul(xf, norm_mix[layer], ev_w_in[i].astype(BF16),
                                     (swa_cols, rw_cols), (F32, F32))
            o_a = _swa(qkv, ev_sinks[i], B, S)
            r, lw, k, v, a, b, g = _rwkv_prep(p_rw, ev_mu[i], ev_w0[i], ev_w2[i], ev_a0[i], ev_a2[i],
                                              ev_g2[i], ev_k_k[i], ev_k_a[i], B, S)
            o_b = _rwkv_scan(r, lw, k, v, a, b, g, ev_r_k[i].reshape(-1), ev_lnx_w[i], ev_lnx_b[i])
            w_out = ev_w_out[i].astype(BF16)
            qw = o_a.shape[-1]
            xf = _proj_residual(xf, [o_a, o_b], [w_out[:qw], w_out[qw:]])
        else:
            KW = GLA_HEADS * GLA_DK
            VW = GLA_HEADS * GLA_DV
            R = od_gate_up.shape[1]
            w = od_w_in[i]
            w_re = jnp.concatenate([w[:, :2 * KW + VW], w[:, 2 * KW + VW + R:],
                                    w[:, 2 * KW + VW:2 * KW + VW + R],
                                    jnp.zeros((D, LANES - R), F32)], axis=1).astype(BF16)
            qkvo, gd = _norm_matmul(xf, norm_mix[layer], w_re, (2 * KW + 2 * VW, LANES), (F32, F32))
            gup = jnp.zeros((LANES, KW), F32).at[:R].set(od_gate_up[i]).astype(BF16)
            o = _gla(qkvo, gd, gup, od_gate_b[i], od_onorm[i], B, S)
            xf = _proj_residual(xf, [o], [od_w_out[i].astype(BF16)])
        xf = _xattn(xf, norm_xattn[layer], xa_wq[layer].astype(BF16), mk, mv, xa_wo[layer].astype(BF16), B, S)
        xf = _moe_layer(xf, norm_moe[layer], moe_w_group[layer], moe_b_group[layer], moe_w_expert[layer],
                        moe_b_expert[layer], moe_w1[layer], moe_w3[layer], moe_w2[layer],
                        norm_final, layer == depth - 1)
    return xf.reshape(B, S, D)
```

```python
import functools

import jax
import jax.numpy as jnp
from jax import lax
from jax.experimental import pallas as pl
from jax.experimental.pallas import tpu as pltpu

F32 = jnp.float32
BF16 = jnp.bfloat16
I32 = jnp.int32

EPS = 1e-6
HEAD_DIM = 64
SWA_WINDOW = 128
SWA_Q_HEADS = 8
SWA_GROUP = 4
RWKV_HEADS = 8
RWKV_WIDTH = 512
RWKV_LN_EPS = 64e-5
RWKV_CHUNK = 64
GLA_HEADS = 4
GLA_DK = 128
GLA_DV = 256
GLA_CHUNK = 64
GLA_GATE_NORM = 16.0
XA_HEADS = 4
XA_HEAD_DIM = 128
MOE_GROUPS = 4
MOE_EXPERTS_PER_GROUP = 8
MOE_EXPERTS = 32
MOE_BLOCK = 512
LANES = 128

VMEM_LIMIT_BYTES = 48 * 1024 * 1024


def _cparams(n_axes):
    return pltpu.CompilerParams(dimension_semantics=("arbitrary",) * n_axes,
                                vmem_limit_bytes=VMEM_LIMIT_BYTES)


def _dot(a, b):
    return jnp.dot(a.astype(BF16), b.astype(BF16), preferred_element_type=F32)


def _dot_nt(a, b):
    return lax.dot_general(a.astype(BF16), b.astype(BF16), (((1,), (1,)), ((), ())),
                           preferred_element_type=F32)


def _dot_tn(a, b):
    return lax.dot_general(a.astype(BF16), b.astype(BF16), (((0,), (0,)), ((), ())),
                           preferred_element_type=F32)


def _dot_f32(a, b):
    return jnp.dot(a, b, preferred_element_type=F32, precision=lax.Precision.HIGHEST)


def _rms(x, g):
    ms = jnp.mean(x * x, axis=-1, keepdims=True)
    return x * lax.rsqrt(ms + EPS) * g


def _sigmoid(x):
    return 1.0 / (1.0 + jnp.exp(-x))


def _softplus(x):
    return jnp.maximum(x, 0.0) + jnp.log(1.0 + jnp.exp(-jnp.abs(x)))


def _norm_matmul_kernel(x_ref, g_ref, w_ref, *o_refs, splits):
    h = _rms(x_ref[...], g_ref[...]).astype(BF16)
    off = 0
    for o_ref, n in zip(o_refs, splits):
        o_ref[...] = jnp.dot(h, w_ref[:, off:off + n], preferred_element_type=F32).astype(o_ref.dtype)
        off += n


def _norm_matmul(x, g, w, splits, out_dtypes, tm=256):
    T, D = x.shape
    N = w.shape[1]
    assert sum(splits) == N and T % tm == 0
    return pl.pallas_call(
        functools.partial(_norm_matmul_kernel, splits=tuple(splits)),
        grid=(T // tm,),
        in_specs=[pl.BlockSpec((tm, D), lambda i: (i, 0)),
                  pl.BlockSpec((1, D), lambda i: (0, 0)),
                  pl.BlockSpec((D, N), lambda i: (0, 0))],
        out_specs=[pl.BlockSpec((tm, n), lambda i: (i, 0)) for n in splits],
        out_shape=[jax.ShapeDtypeStruct((T, n), dt) for n, dt in zip(splits, out_dtypes)],
        compiler_params=_cparams(1),
        name="norm_matmul",
    )(x, g.reshape(1, D), w)


def _proj_residual_kernel(*refs, n_in):
    x_ref = refs[0]
    a_refs = refs[1:1 + n_in]
    w_refs = refs[1 + n_in:1 + 2 * n_in]
    o_ref = refs[1 + 2 * n_in]
    acc = x_ref[...]
    for a_ref, w_ref in zip(a_refs, w_refs):
        acc = acc + jnp.dot(a_ref[...], w_ref[...], preferred_element_type=F32)
    o_ref[...] = acc


def _proj_residual(x, acts, weights, tm=512):
    T, D = x.shape
    n_in = len(acts)
    in_specs = [pl.BlockSpec((tm, D), lambda i: (i, 0))]
    in_specs += [pl.BlockSpec((tm, a.shape[1]), lambda i: (i, 0)) for a in acts]
    in_specs += [pl.BlockSpec(w.shape, lambda i: (0, 0)) for w in weights]
    return pl.pallas_call(
        functools.partial(_proj_residual_kernel, n_in=n_in),
        grid=(T // tm,),
        in_specs=in_specs,
        out_specs=pl.BlockSpec((tm, D), lambda i: (i, 0)),
        out_shape=jax.ShapeDtypeStruct((T, D), F32),
        compiler_params=_cparams(1),
        name="proj_residual",
    )(x, *acts, *weights)


def _swa_kernel(sinks_ref, q_ref, kp_ref, kc_ref, vp_ref, vc_ref, o_ref):
    n = pl.program_id(1)
    W = SWA_WINDOW
    q = q_ref[...]
    k = jnp.concatenate([kp_ref[...], kc_ref[...]], axis=0)
    v = jnp.concatenate([vp_ref[...], vc_ref[...]], axis=0)
    qpos = lax.broadcasted_iota(I32, (W, 2 * W), 0) + W
    kpos = lax.broadcasted_iota(I32, (W, 2 * W), 1)
    rel = qpos - kpos
    in_window = jnp.where(rel >= 0, jnp.where(rel < W, 1, 0), 0)
    has_prev = jnp.where(n > 0, 1, 0)
    valid = (in_window * jnp.where(kpos >= W, 1, has_prev)) > 0
    n_groups = SWA_Q_HEADS // SWA_GROUP
    qb = q.astype(BF16)
    kb = k.astype(BF16)
    vb = v.astype(BF16)
    scores = []
    for g in range(n_groups):
        qg = jnp.concatenate([qb[:, h * HEAD_DIM:(h + 1) * HEAD_DIM]
                              for h in range(g * SWA_GROUP, (g + 1) * SWA_GROUP)], axis=0)
        scores.append(_dot_nt(qg, kb[:, g * HEAD_DIM:(g + 1) * HEAD_DIM]))
    probs = []
    for g in range(n_groups):
        pieces = []
        for j in range(SWA_GROUP):
            s = jnp.where(valid, scores[g][j * W:(j + 1) * W] * (HEAD_DIM ** -0.5), -jnp.inf)
            sink = sinks_ref[g * SWA_GROUP + j]
            m = jnp.maximum(jnp.max(s, axis=-1, keepdims=True), sink)
            p = jnp.exp(s - m)
            den = jnp.sum(p, axis=-1, keepdims=True) + jnp.exp(sink - m)
            pieces.append((p / den).astype(BF16))
        probs.append(jnp.concatenate(pieces, axis=0))
    outs = []
    for g in range(n_groups):
        og = _dot(probs[g], vb[:, g * HEAD_DIM:(g + 1) * HEAD_DIM])
        outs += [og[j * W:(j + 1) * W] for j in range(SWA_GROUP)]
    o_ref[...] = jnp.concatenate(outs, axis=1).astype(o_ref.dtype)


def _swa(qkv, sinks, B, S):
    W = SWA_WINDOW
    qkv3 = qkv.reshape(B, S, qkv.shape[-1])
    qw = SWA_Q_HEADS * HEAD_DIM
    kw = qw // SWA_GROUP
    kcol = qw // kw
    out = pl.pallas_call(
        _swa_kernel,
        grid=(B, S // W),
        in_specs=[pl.BlockSpec(memory_space=pltpu.SMEM),
                  pl.BlockSpec((None, W, qw), lambda b, n: (b, n, 0)),
                  pl.BlockSpec((None, W, kw), lambda b, n: (b, jnp.maximum(n - 1, 0), kcol)),
                  pl.BlockSpec((None, W, kw), lambda b, n: (b, n, kcol)),
                  pl.BlockSpec((None, W, kw), lambda b, n: (b, jnp.maximum(n - 1, 0), kcol + 1)),
                  pl.BlockSpec((None, W, kw), lambda b, n: (b, n, kcol + 1))],
        out_specs=pl.BlockSpec((None, W, qw), lambda b, n: (b, n, 0)),
        out_shape=jax.ShapeDtypeStruct((B, S, qw), BF16),
        compiler_params=_cparams(2),
        name="swa",
    )(sinks, qkv3, qkv3, qkv3, qkv3, qkv3)
    return out.reshape(B * S, qw)


def _rwkv_prep_kernel(p_ref, pprev_ref, mu_ref, w0_ref, w2_ref, a0_ref, a2_ref, g2_ref, kk_ref, ka_ref,
                      r_out, lw_out, k_out, v_out, a_out, b_out, g_out):
    n = pl.program_id(1)
    C = RWKV_WIDTH
    p = p_ref[...]
    last = jnp.where(n > 0, pprev_ref[7:8, :], 0.0)
    row = lax.broadcasted_iota(I32, p.shape, 0)
    p_prev = jnp.where(row == 0, last, pltpu.roll(p, 1, axis=0))
    p = p + (p_prev - p) * mu_ref[...]
    r = p[:, :C]
    k = p[:, C:2 * C]
    v = p[:, 2 * C:3 * C]
    xw = p[:, 3 * C:3 * C + 64]
    xa = p[:, 3 * C + 64:3 * C + 128]
    xg = p[:, 3 * C + 128:]
    w = -_softplus(-(w0_ref[...] + _dot(jnp.tanh(xw), w2_ref[...]))) - 0.5
    lw = -jnp.exp(w)
    a = _sigmoid(a0_ref[...] + _dot(xa, a2_ref[...]))
    g = _dot(_sigmoid(xg), g2_ref[...])
    kk = k * kk_ref[...]
    pieces = []
    for h in range(RWKV_HEADS):
        kh = kk[:, h * HEAD_DIM:(h + 1) * HEAD_DIM]
        nrm = jnp.sqrt(jnp.sum(kh * kh, axis=-1, keepdims=True))
        pieces.append(kh / jnp.maximum(nrm, 1e-12))
    kk = jnp.concatenate(pieces, axis=1)
    r_out[...] = r
    lw_out[...] = lw
    k_out[...] = k * (1.0 + (a - 1.0) * ka_ref[...])
    v_out[...] = v
    a_out[...] = -kk
    b_out[...] = kk * a
    g_out[...] = g


def _rwkv_prep(p, mu, w0, w2, a0, a2, g2, k_k, k_a, B, S, tt=256):
    C = RWKV_WIDTH
    PW = p.shape[-1]
    p3 = p.reshape(B, S, PW)
    row = lambda t: t.reshape(1, -1)
    full = lambda arr: pl.BlockSpec(arr.shape, lambda b, n: (0,) * arr.ndim)
    params = [row(mu), row(w0), w2.astype(BF16), row(a0), a2.astype(BF16), g2.astype(BF16), row(k_k), row(k_a)]
    outs = pl.pallas_call(
        _rwkv_prep_kernel,
        grid=(B, S // tt),
        in_specs=[pl.BlockSpec((None, tt, PW), lambda b, n: (b, n, 0)),
                  pl.BlockSpec((None, 8, PW), lambda b, n: (b, jnp.maximum(n * (tt // 8) - 1, 0), 0))]
                 + [full(t) for t in params],
        out_specs=[pl.BlockSpec((None, tt, C), lambda b, n: (b, n, 0))] * 7,
        out_shape=[jax.ShapeDtypeStruct((B, S, C), F32)] * 7,
        compiler_params=_cparams(2),
        name="rwkv_prep",
    )(p3, p3, *params)
    return outs


def _rwkv_scan_kernel(r_ref, lw_ref, k_ref, v_ref, a_ref, b_ref, g_ref, rk_ref, lnw_ref, lnb_ref,
                      o_ref, s_ref):
    c = pl.program_id(1)

    @pl.when(c == 0)
    def _():
        s_ref[...] = jnp.zeros_like(s_ref)

    C = RWKV_CHUNK
    row = lax.broadcasted_iota(I32, (C, C), 0)
    col = lax.broadcasted_iota(I32, (C, C), 1)
    lower = row >= col
    strict = row > col

    lw = lw_ref[...]
    cum = _dot_f32(jnp.where(lower, 1.0, 0.0).astype(F32), lw)
    cum_last = cum[C - 1:C, :]
    r = r_ref[...]
    k = k_ref[...]
    v = v_ref[...]
    a = a_ref[...]
    b = b_ref[...]
    g = g_ref[...]
    e_neg = jnp.exp(-cum)
    e_rem = jnp.exp(cum_last - cum)
    r_t = (r * jnp.exp(cum)).astype(BF16)
    a_t = (a * jnp.exp(cum - lw)).astype(BF16)
    b_t = (b * e_neg).astype(BF16)
    k_t = (k * e_neg).astype(BF16)
    b_d = (b * e_rem).astype(BF16)
    k_d = (k * e_rem).astype(BF16)
    v_b = v.astype(BF16)
    e_last = jnp.exp(cum_last)
    rkk = r * k * rk_ref[...]
    lnw = lnw_ref[...]
    lnb = lnb_ref[...]
    row2 = lax.broadcasted_iota(I32, (C, 2 * C), 0)
    col2 = lax.broadcasted_iota(I32, (C, 2 * C), 1)
    colm = jnp.where(col2 >= C, col2 - C, col2)
    lower2 = row2 >= colm
    strict_k = jnp.where(col2 >= C, jnp.where(row2 > colm, 1, 0), 0) > 0

    heads = range(RWKV_HEADS)
    sls = [slice(h * HEAD_DIM, (h + 1) * HEAD_DIM) for h in heads]
    s0s = [s_ref[h] for h in heads]
    ars_l = [jnp.concatenate([a_t[:, sl], r_t[:, sl]], axis=0) for sl in sls]
    bks = [jnp.concatenate([b_t[:, sl], k_t[:, sl]], axis=0) for sl in sls]
    ms = [_dot_nt(ars_l[h], bks[h]) for h in heads]
    arss = [_dot_nt(ars_l[h], s0s[h]) for h in heads]
    vhs = [v_b[:, sl] for sl in sls]
    xs = [arss[h][:C] + _dot(jnp.where(strict_k, ms[h][:C], 0.0), jnp.concatenate([vhs[h], vhs[h]], axis=0))
          for h in heads]
    pws = [jnp.where(strict, ms[h][:C, :C], 0.0).astype(BF16) for h in heads]
    xs = [xs[h] + _dot(pws[h], xs[h]) for h in heads]
    for _ in range(5):
        pws = [_dot(pws[h], pws[h]).astype(BF16) for h in heads]
        xs = [xs[h] + _dot(pws[h], xs[h]) for h in heads]
    uvs = [jnp.concatenate([xs[h].astype(BF16), vhs[h]], axis=0) for h in heads]
    ys = [arss[h][C:] + _dot(jnp.where(lower2, ms[h][C:], 0.0), uvs[h]) for h in heads]
    for h in heads:
        bkd = jnp.concatenate([b_d[:, sls[h]], k_d[:, sls[h]]], axis=0)
        s_ref[h] = s0s[h] * e_last[:, sls[h]] + _dot_tn(uvs[h], bkd)

    outs = []
    for h in heads:
        sl = sls[h]
        y = ys[h]
        mean = jnp.mean(y, axis=-1, keepdims=True)
        yc = y - mean
        var = jnp.mean(yc * yc, axis=-1, keepdims=True)
        yn = yc * lax.rsqrt(var + RWKV_LN_EPS) * lnw[:, sl] + lnb[:, sl]
        bonus = jnp.sum(rkk[:, sl], axis=-1, keepdims=True) * v[:, sl]
        outs.append((yn + bonus) * g[:, sl])
    o_ref[...] = jnp.concatenate(outs, axis=1).astype(o_ref.dtype)


def _rwkv_scan(r, lw, k, v, a, b, g, r_k, lnx_w, lnx_b):
    B, S, W = r.shape
    C = RWKV_CHUNK
    seq = pl.BlockSpec((None, C, W), lambda bb, c: (bb, c, 0))
    par = pl.BlockSpec((1, W), lambda bb, c: (0, 0))
    out = pl.pallas_call(
        _rwkv_scan_kernel,
        grid=(B, S // C),
        in_specs=[seq] * 7 + [par] * 3,
        out_specs=seq,
        out_shape=jax.ShapeDtypeStruct((B, S, W), BF16),
        scratch_shapes=[pltpu.VMEM((RWKV_HEADS, HEAD_DIM, HEAD_DIM), F32)],
        compiler_params=_cparams(2),
        name="rwkv_scan",
    )(r, lw, k, v, a, b, g, r_k.reshape(1, W), lnx_w.reshape(1, W), lnx_b.reshape(1, W))
    return out.reshape(B * S, W)


def _gla_kernel(q_ref, k_ref, v_ref, og_ref, gd_ref, gup_ref, gb_ref, on_ref, o_ref, s_ref):
    c = pl.program_id(1)

    @pl.when(c == 0)
    def _():
        s_ref[...] = jnp.zeros_like(s_ref)

    C = GLA_CHUNK
    row = lax.broadcasted_iota(I32, (C, C), 0)
    col = lax.broadcasted_iota(I32, (C, C), 1)
    lower = row >= col
    z = _dot(gd_ref[...], gup_ref[...]) + gb_ref[...]
    gk = -_softplus(-z) / GLA_GATE_NORM
    cum = _dot_f32(jnp.where(lower, 1.0, 0.0).astype(F32), gk)
    cum_last = cum[C - 1:C, :]
    q = q_ref[...] * (GLA_DK ** -0.5)
    k = k_ref[...]
    qe = (q * jnp.exp(cum)).astype(BF16)
    ke = (k * jnp.exp(-cum)).astype(BF16)
    kd = (k * jnp.exp(cum_last - cum)).astype(BF16)
    e_last = jnp.exp(cum_last)
    v = v_ref[...].astype(BF16)
    og = og_ref[...]
    onorm = on_ref[...]
    heads = range(GLA_HEADS)
    kss = [slice(h * GLA_DK, (h + 1) * GLA_DK) for h in heads]
    vss = [slice(h * GLA_DV, (h + 1) * GLA_DV) for h in heads]
    sts = [s_ref[h] for h in heads]
    atts = [jnp.where(lower, _dot_nt(qe[:, kss[h]], ke[:, kss[h]]), 0.0) for h in heads]
    inters = [_dot_nt(qe[:, kss[h]], sts[h]) for h in heads]
    os_ = [inters[h] + _dot(atts[h], v[:, vss[h]]) for h in heads]
    for h in heads:
        s_ref[h] = sts[h] * e_last[:, kss[h]] + _dot_tn(v[:, vss[h]], kd[:, kss[h]])
    outs = []
    for h in heads:
        gate = og[:, vss[h]]
        outs.append(_rms(os_[h], onorm) * (gate * _sigmoid(gate)))
    o_ref[...] = jnp.concatenate(outs, axis=1).astype(o_ref.dtype)


def _gla(qkvo, gd, gate_up_pad, gate_b, onorm, B, S):
    C = GLA_CHUNK
    KW = GLA_HEADS * GLA_DK
    VW = GLA_HEADS * GLA_DV
    x3 = qkvo.reshape(B, S, qkvo.shape[-1])
    gd3 = gd.reshape(B, S, LANES)
    kb = KW // KW
    out = pl.pallas_call(
        _gla_kernel,
        grid=(B, S // C),
        in_specs=[pl.BlockSpec((None, C, KW), lambda b, c: (b, c, 0)),
                  pl.BlockSpec((None, C, KW), lambda b, c: (b, c, kb)),
                  pl.BlockSpec((None, C, VW), lambda b, c: (b, c, 1)),
                  pl.BlockSpec((None, C, VW), lambda b, c: (b, c, 2)),
                  pl.BlockSpec((None, C, LANES), lambda b, c: (b, c, 0)),
                  pl.BlockSpec((LANES, KW), lambda b, c: (0, 0)),
                  pl.BlockSpec((1, KW), lambda b, c: (0, 0)),
                  pl.BlockSpec((1, GLA_DV), lambda b, c: (0, 0))],
        out_specs=pl.BlockSpec((None, C, VW), lambda b, c: (b, c, 0)),
        out_shape=jax.ShapeDtypeStruct((B, S, VW), BF16),
        scratch_shapes=[pltpu.VMEM((GLA_HEADS, GLA_DV, GLA_DK), F32)],
        compiler_params=_cparams(2),
        name="gla",
    )(x3, x3, x3, x3, gd3, gate_up_pad, gate_b.reshape(1, KW), onorm.reshape(1, GLA_DV))
    return out.reshape(B * S, VW)


def _xattn_kernel(x_ref, g_ref, wq_ref, mk_ref, mv_ref, wo_ref, o_ref):
    x = x_ref[...]
    h = _rms(x, g_ref[...])
    q = _dot(h, wq_ref[...]).astype(BF16)
    mk = mk_ref[...]
    mv = mv_ref[...]
    sls = [slice(hd * XA_HEAD_DIM, (hd + 1) * XA_HEAD_DIM) for hd in range(XA_HEADS)]
    scores = [_dot_nt(q[:, sl], mk[:, sl]) for sl in sls]
    probs = []
    for s in scores:
        s = s * (XA_HEAD_DIM ** -0.5)
        p = jnp.exp(s - jnp.max(s, axis=-1, keepdims=True))
        probs.append((p / jnp.sum(p, axis=-1, keepdims=True)).astype(BF16))
    o = jnp.concatenate([_dot(p, mv[:, sl]) for p, sl in zip(probs, sls)], axis=1)
    o_ref[...] = x + _dot(o, wo_ref[...])


def _xattn(x, g, wq, mk, mv, wo, B, S, tq=256):
    D = x.shape[-1]
    M = mk.shape[0] // B
    XW = mk.shape[-1]
    x3 = x.reshape(B, S, D)
    out = pl.pallas_call(
        _xattn_kernel,
        grid=(B, S // tq),
        in_specs=[pl.BlockSpec((None, tq, D), lambda b, n: (b, n, 0)),
                  pl.BlockSpec((1, D), lambda b, n: (0, 0)),
                  pl.BlockSpec((D, XW), lambda b, n: (0, 0)),
                  pl.BlockSpec((None, M, XW), lambda b, n: (b, 0, 0)),
                  pl.BlockSpec((None, M, XW), lambda b, n: (b, 0, 0)),
                  pl.BlockSpec((XW, D), lambda b, n: (0, 0))],
        out_specs=pl.BlockSpec((None, tq, D), lambda b, n: (b, n, 0)),
        out_shape=jax.ShapeDtypeStruct((B, S, D), F32),
        compiler_params=_cparams(2),
        name="xattn",
    )(x3, g.reshape(1, D), wq, mk.reshape(B, M, XW), mv.reshape(B, M, XW), wo)
    return out.reshape(B * S, D)


def _router_kernel(x_ref, g_ref, w_ref, b_ref, info_ref, cnt_ref, carry_ref):
    i = pl.program_id(0)

    @pl.when(i == 0)
    def _():
        carry_ref[...] = jnp.zeros_like(carry_ref)

    h = _rms(x_ref[...], g_ref[...])
    logits = _dot(h, w_ref[...]) + b_ref[...]
    tm = logits.shape[0]
    lane = lax.broadcasted_iota(I32, logits.shape, 1)
    big = jnp.int32(LANES)
    neg = -jnp.inf
    gl = jnp.where(lane < MOE_GROUPS, logits, neg)
    gmax = jnp.max(gl, axis=-1, keepdims=True)
    g_top = jnp.min(jnp.where(gl == gmax, lane, big), axis=-1, keepdims=True)
    p_group = 1.0 / jnp.sum(jnp.exp(gl - gmax), axis=-1, keepdims=True)
    lo = MOE_GROUPS + MOE_EXPERTS_PER_GROUP * g_top
    in_group = jnp.where(lane >= lo, jnp.where(lane < lo + MOE_EXPERTS_PER_GROUP, 1, 0), 0) > 0
    el = jnp.where(in_group, logits, neg)
    emax = jnp.max(el, axis=-1, keepdims=True)
    ee = jnp.exp(el - emax)
    prob = ee / jnp.sum(ee, axis=-1, keepdims=True)
    prob = jnp.where(in_group, prob, -1.0)
    p1 = jnp.max(prob, axis=-1, keepdims=True)
    i1 = jnp.min(jnp.where(prob == p1, lane, big), axis=-1, keepdims=True)
    rest = jnp.where(lane == i1, -1.0, prob)
    p2 = jnp.max(rest, axis=-1, keepdims=True)
    i2 = jnp.min(jnp.where(rest == p2, lane, big), axis=-1, keepdims=True)
    tot = p1 + p2
    e1 = i1 - MOE_GROUPS
    e2 = i2 - MOE_GROUPS
    oh1 = jnp.where(lane == e1, 1.0, 0.0)
    oh2 = jnp.where(lane == e2, 1.0, 0.0)
    row = lax.broadcasted_iota(I32, (tm, tm), 0)
    col = lax.broadcasted_iota(I32, (tm, tm), 1)
    before = jnp.where(row > col, 1.0, 0.0)
    pre = _dot(before, jnp.concatenate([oh1, oh2], axis=1))
    tot1 = jnp.sum(oh1, axis=0, keepdims=True)
    tot2 = jnp.sum(oh2, axis=0, keepdims=True)
    carry = carry_ref[...]
    r1 = jnp.sum(oh1 * (carry + pre[:, :LANES]), axis=-1, keepdims=True)
    r2 = jnp.sum(oh2 * (carry + tot1 + pre[:, LANES:]), axis=-1, keepdims=True)
    carry = carry + tot1 + tot2
    carry_ref[...] = carry
    cnt_ref[...] = carry
    g1 = p_group * p1 / tot
    g2 = p_group * p2 / tot
    vals = [e1.astype(F32), e2.astype(F32), r1, r2, g1, g2]
    info = jnp.zeros_like(logits)
    for j, val in enumerate(vals):
        info = jnp.where(lane == j, val, info)
    info_ref[...] = info


def _router(x, g, w_router, b_router, tm=256):
    T, D = x.shape
    return pl.pallas_call(
        _router_kernel,
        grid=(T // tm,),
        in_specs=[pl.BlockSpec((tm, D), lambda i: (i, 0)),
                  pl.BlockSpec((1, D), lambda i: (0, 0)),
                  pl.BlockSpec((D, LANES), lambda i: (0, 0)),
                  pl.BlockSpec((1, LANES), lambda i: (0, 0))],
        out_specs=[pl.BlockSpec((tm, LANES), lambda i: (i, 0)),
                   pl.BlockSpec((1, LANES), lambda i: (0, 0))],
        out_shape=[jax.ShapeDtypeStruct((T, LANES), F32), jax.ShapeDtypeStruct((1, LANES), F32)],
        scratch_shapes=[pltpu.VMEM((1, LANES), F32)],
        compiler_params=_cparams(1),
        name="router",
    )(x, g.reshape(1, D), w_router, b_router)


def _row_bytes_wait(hbm, buf, sem):
    pltpu.make_async_copy(buf, hbm.at[pl.ds(0, buf.shape[0]), :], sem).wait()


def _moe_dispatch_kernel(pends_ref, cnt_ref, dest_ref, x_ref, g_ref, hs_hbm, hbuf, zbuf, sems, zsem, *, td):
    i = pl.program_id(0)
    nt = pl.num_programs(0)
    slot = lax.rem(i, 2)

    @pl.when(i == 0)
    def _():
        zbuf[...] = jnp.zeros_like(zbuf)
        for e in range(MOE_EXPERTS):
            @pl.when(cnt_ref[e] > 0)
            def _():
                start = pl.multiple_of(pends_ref[e] - MOE_BLOCK, MOE_BLOCK)
                pltpu.make_async_copy(zbuf, hs_hbm.at[pl.ds(start, MOE_BLOCK), :], zsem).start()
        for e in range(MOE_EXPERTS):
            @pl.when(cnt_ref[e] > 0)
            def _():
                pltpu.make_async_copy(zbuf, hs_hbm.at[pl.ds(0, MOE_BLOCK), :], zsem).wait()

        first_unused = pends_ref[MOE_EXPERTS - 1] // MOE_BLOCK
        n_blocks = hs_hbm.shape[0] // MOE_BLOCK

        def zero_start(blk, carry):
            start = pl.multiple_of(blk * MOE_BLOCK, MOE_BLOCK)
            pltpu.make_async_copy(zbuf, hs_hbm.at[pl.ds(start, MOE_BLOCK), :], zsem).start()
            return carry

        def zero_wait(blk, carry):
            pltpu.make_async_copy(zbuf, hs_hbm.at[pl.ds(0, MOE_BLOCK), :], zsem).wait()
            return carry

        lax.fori_loop(first_unused, n_blocks, zero_start, 0)
        lax.fori_loop(first_unused, n_blocks, zero_wait, 0)

    hb = hbuf.at[slot]
    hb[...] = _rms(x_ref[...], g_ref[...])
    for j in range(td):
        for c in range(2):
            pltpu.make_async_copy(hb.at[pl.ds(j, 1), :],
                                  hs_hbm.at[pl.ds(dest_ref[0, c * td + j], 1), :],
                                  sems.at[slot]).start(priority=c)

    @pl.when(i > 0)
    def _():
        other = hbuf.at[1 - slot]
        _row_bytes_wait(hs_hbm, other, sems.at[1 - slot])
        _row_bytes_wait(hs_hbm, other, sems.at[1 - slot])

    @pl.when(i == nt - 1)
    def _():
        _row_bytes_wait(hs_hbm, hb, sems.at[slot])
        _row_bytes_wait(hs_hbm, hb, sems.at[slot])


def _moe_dispatch(x, g, pends, counts, dest3, P, td):
    T, D = x.shape
    grid_spec = pltpu.PrefetchScalarGridSpec(
        num_scalar_prefetch=2,
        grid=(T // td,),
        in_specs=[pl.BlockSpec((None, 1, 2 * td), lambda i, pe, cn: (i, 0, 0), memory_space=pltpu.SMEM),
                  pl.BlockSpec((td, D), lambda i, pe, cn: (i, 0)),
                  pl.BlockSpec((1, D), lambda i, pe, cn: (0, 0))],
        out_specs=pl.BlockSpec(memory_space=pl.ANY),
        scratch_shapes=[pltpu.VMEM((2, td, D), F32),
                        pltpu.VMEM((MOE_BLOCK, D), F32),
                        pltpu.SemaphoreType.DMA((2,)),
                        pltpu.SemaphoreType.DMA(())],
    )
    return pl.pallas_call(
        functools.partial(_moe_dispatch_kernel, td=td),
        grid_spec=grid_spec,
        out_shape=jax.ShapeDtypeStruct((P, D), F32),
        compiler_params=_cparams(1),
        name="moe_dispatch",
    )(pends, counts, dest3, x, g.reshape(1, D))


def _moe_expert_kernel(be_ref, nu_ref, hs_ref, w1_ref, w3_ref, w2_ref, o_ref, w1b, w3b, w2b):
    i = pl.program_id(0)
    used = i < nu_ref[0]
    changed = jnp.logical_or(i == 0, be_ref[i] != be_ref[jnp.maximum(i - 1, 0)])

    @pl.when(jnp.logical_and(used, changed))
    def _():
        w1b[...] = w1_ref[...].astype(BF16)
        w3b[...] = w3_ref[...].astype(BF16)
        w2b[...] = w2_ref[...].astype(BF16)

    @pl.when(used)
    def _():
        xe = hs_ref[...].astype(BF16)
        h1 = jnp.dot(xe, w1b[...], preferred_element_type=F32)
        h3 = jnp.dot(xe, w3b[...], preferred_element_type=F32)
        act = (h1 * _sigmoid(h1) * h3).astype(BF16)
        o_ref[...] = jnp.dot(act, w2b[...], preferred_element_type=F32)

    @pl.when(jnp.logical_not(used))
    def _():
        o_ref[...] = jnp.zeros_like(o_ref)


def _moe_experts(hs, block_e, n_used, w1, w3, w2, layer):
    P, D = hs.shape
    FF = w1.shape[-1]
    NB = P // MOE_BLOCK
    last = lambda i, nu: jnp.minimum(i, nu[0] - 1)
    grid_spec = pltpu.PrefetchScalarGridSpec(
        num_scalar_prefetch=2,
        grid=(NB,),
        in_specs=[pl.BlockSpec((MOE_BLOCK, D), lambda i, be, nu: (last(i, nu), 0)),
                  pl.BlockSpec((None, None, D, FF), lambda i, be, nu: (layer, be[last(i, nu)], 0, 0)),
                  pl.BlockSpec((None, None, D, FF), lambda i, be, nu: (layer, be[last(i, nu)], 0, 0)),
                  pl.BlockSpec((None, None, FF, D), lambda i, be, nu: (layer, be[last(i, nu)], 0, 0))],
        out_specs=pl.BlockSpec((MOE_BLOCK, D), lambda i, be, nu: (i, 0)),
        scratch_shapes=[pltpu.VMEM((D, FF), BF16),
                        pltpu.VMEM((D, FF), BF16),
                        pltpu.VMEM((FF, D), BF16)],
    )
    return pl.pallas_call(
        _moe_expert_kernel,
        grid_spec=grid_spec,
        out_shape=jax.ShapeDtypeStruct((P, D), F32),
        compiler_params=_cparams(1),
        name="moe_experts",
    )(block_e, n_used, hs, w1, w3, w2)


def _gather_rows(src_hbm, idx_ref, dst_ref, sem, n_rows):
    for r in range(n_rows):
        pltpu.make_async_copy(src_hbm.at[pl.ds(idx_ref[0, r], 1), :],
                              dst_ref.at[pl.ds(r, 1), :], sem).start(priority=r % 2)


def _moe_combine_kernel(pos_ref, posn_ref, x_ref, info_ref, yb_hbm, gf_ref, o_ref, ybuf, sems, *, tc, final_norm):
    i = pl.program_id(0)
    nb = pl.num_programs(0)
    slot = lax.rem(i, 2)

    @pl.when(i == 0)
    def _():
        def issue(r, carry):
            pltpu.make_async_copy(yb_hbm.at[pl.ds(pos_ref[0, r], 1), :],
                                  ybuf.at[0, pl.ds(r, 1), :], sems.at[0]).start()
            return carry
        lax.fori_loop(0, 2 * tc, issue, 0)

    @pl.when(i + 1 < nb)
    def _():
        _gather_rows(yb_hbm, posn_ref, ybuf.at[1 - slot], sems.at[1 - slot], 2 * tc)

    pltpu.make_async_copy(yb_hbm.at[pl.ds(0, 2 * tc), :], ybuf.at[slot], sems.at[slot]).wait()
    info = info_ref[...]
    y0 = ybuf[slot, 0:tc, :]
    y1 = ybuf[slot, tc:2 * tc, :]
    out = x_ref[...] + (y0 * info[:, 4:5] + y1 * info[:, 5:6])
    if final_norm:
        out = _rms(out, gf_ref[...])
    o_ref[...] = out


def _moe_combine(x, info, dest3, yb, g_final, final_norm, tc):
    T, D = x.shape
    NT = T // tc
    return pl.pallas_call(
        functools.partial(_moe_combine_kernel, tc=tc, final_norm=final_norm),
        grid=(NT,),
        in_specs=[pl.BlockSpec((None, 1, 2 * tc), lambda i: (i, 0, 0), memory_space=pltpu.SMEM),
                  pl.BlockSpec((None, 1, 2 * tc), lambda i: (jnp.minimum(i + 1, NT - 1), 0, 0),
                               memory_space=pltpu.SMEM),
                  pl.BlockSpec((tc, D), lambda i: (i, 0)),
                  pl.BlockSpec((tc, LANES), lambda i: (i, 0)),
                  pl.BlockSpec(memory_space=pl.ANY),
                  pl.BlockSpec((1, D), lambda i: (0, 0))],
        out_specs=pl.BlockSpec((tc, D), lambda i: (i, 0)),
        out_shape=jax.ShapeDtypeStruct((T, D), F32),
        scratch_shapes=[pltpu.VMEM((2, 2 * tc, D), F32), pltpu.SemaphoreType.DMA((2,))],
        compiler_params=_cparams(1),
        name="moe_combine",
    )(dest3, dest3, x, info, yb, g_final.reshape(1, D))


MOE_TILE = 128


def _moe_layer(x, g, w_group, b_group, w_expert, b_expert, w1, w3, w2, layer, g_final, final_norm):
    T, D = x.shape
    n_log = MOE_GROUPS + MOE_EXPERTS
    w_router = jnp.zeros((D, LANES), F32).at[:, :MOE_GROUPS].set(w_group).at[:, MOE_GROUPS:n_log].set(w_expert)
    b_router = jnp.zeros((1, LANES), F32).at[0, :MOE_GROUPS].set(b_group).at[0, MOE_GROUPS:n_log].set(b_expert)
    info, cnt = _router(x, g, w_router.astype(BF16), b_router)
    P = 2 * T + MOE_EXPERTS * MOE_BLOCK
    NB = P // MOE_BLOCK
    counts = cnt[0, :MOE_EXPERTS].astype(I32)
    padded = (counts + MOE_BLOCK - 1) // MOE_BLOCK * MOE_BLOCK
    pends = jnp.cumsum(padded).astype(I32)
    pstarts = pends - padded
    block_start = jnp.arange(NB, dtype=I32) * MOE_BLOCK
    block_e = jnp.minimum(jnp.sum((pends[None, :] <= block_start[:, None]).astype(I32), axis=1),
                          MOE_EXPERTS - 1).astype(I32)
    n_used = (pends[-1:] // MOE_BLOCK).astype(I32)
    eid = info[:, 0:2].astype(I32)
    dest = pstarts[eid] + info[:, 2:4].astype(I32)
    NT = T // MOE_TILE
    dest3 = dest.reshape(NT, MOE_TILE, 2).transpose(0, 2, 1).reshape(NT, 1, 2 * MOE_TILE)
    hs = _moe_dispatch(x, g, pends, counts, dest3, P, MOE_TILE)
    yb = _moe_experts(hs, block_e, n_used, w1, w3, w2, layer)
    return _moe_combine(x, info, dest3, yb, g_final, final_norm, MOE_TILE)


def kernel(x, mem, norm_mix, norm_xattn, norm_moe, norm_final, ev_w_in, ev_sinks, ev_mu, ev_w0, ev_w2, ev_a0, ev_a2, ev_g2, ev_k_k, ev_k_a, ev_r_k, ev_lnx_w, ev_lnx_b, ev_w_out, od_w_in, od_gate_up, od_gate_b, od_onorm, od_w_out, mem_norm, mem_wk, mem_wv, xa_wq, xa_wo, moe_w_group, moe_b_group, moe_w_expert, moe_b_expert, moe_w1, moe_w3, moe_w2):
    B, S, D = x.shape
    M = mem.shape[1]
    T = B * S
    depth = norm_mix.shape[0]
    xf = x.reshape(T, D)

    XW = XA_HEADS * XA_HEAD_DIM
    w_kv = jnp.concatenate([mem_wk, mem_wv], axis=1).astype(BF16)
    mk, mv = _norm_matmul(mem.reshape(B * M, D), mem_norm, w_kv, (XW, XW), (BF16, BF16))

    for layer in range(depth):
        i = layer // 2
        if layer % 2 == 0:
            swa_cols = SWA_Q_HEADS * HEAD_DIM + 2 * (SWA_Q_HEADS // SWA_GROUP) * HEAD_DIM
            rw_cols = ev_w_in.shape[-1] - swa_cols
            qkv, p_rw = _norm_matmul(xf, norm_mix[layer], ev_w_in[i].astype(BF16),
                                     (swa_cols, rw_cols), (F32, F32))
            o_a = _swa(qkv, ev_sinks[i], B, S)
            r, lw, k, v, a, b, g = _rwkv_prep(p_rw, ev_mu[i], ev_w0[i], ev_w2[i], ev_a0[i], ev_a2[i],
                                              ev_g2[i], ev_k_k[i], ev_k_a[i], B, S)
            o_b = _rwkv_scan(r, lw, k, v, a, b, g, ev_r_k[i].reshape(-1), ev_lnx_w[i], ev_lnx_b[i])
            w_out = ev_w_out[i].astype(BF16)
            qw = o_a.shape[-1]
            xf = _proj_residual(xf, [o_a, o_b], [w_out[:qw], w_out[qw:]])
        else:
            KW = GLA_HEADS * GLA_DK
            VW = GLA_HEADS * GLA_DV
            R = od_gate_up.shape[1]
            w = od_w_in[i]
            w_re = jnp.concatenate([w[:, :2 * KW + VW], w[:, 2 * KW + VW + R:],
                                    w[:, 2 * KW + VW:2 * KW + VW + R],
                                    jnp.zeros((D, LANES - R), F32)], axis=1).astype(BF16)
            qkvo, gd = _norm_matmul(xf, norm_mix[layer], w_re, (2 * KW + 2 * VW, LANES), (F32, F32))
            gup = jnp.zeros((LANES, KW), F32).at[:R].set(od_gate_up[i]).astype(BF16)
            o = _gla(qkvo, gd, gup, od_gate_b[i], od_onorm[i], B, S)
            xf = _proj_residual(xf, [o], [od_w_out[i].astype(BF16)])
        xf = _xattn(xf, norm_xattn[layer], xa_wq[layer].astype(BF16), mk, mv, xa_wo[layer].astype(BF16), B, S)
        xf = _moe_layer(xf, norm_moe[layer], moe_w_group[layer], moe_b_group[layer], moe_w_expert[layer],
                        moe_b_expert[layer], moe_w1, moe_w3, moe_w2, layer,
                        norm_final, layer == depth - 1)
    return xf.reshape(B, S, D)
```

```python
import functools

import jax
import jax.numpy as jnp
from jax import lax
from jax.experimental import pallas as pl
from jax.experimental.pallas import tpu as pltpu

F32 = jnp.float32
BF16 = jnp.bfloat16
I32 = jnp.int32

EPS = 1e-6
HEAD_DIM = 64
SWA_WINDOW = 128
SWA_Q_HEADS = 8
SWA_GROUP = 4
RWKV_HEADS = 8
RWKV_WIDTH = 512
RWKV_LN_EPS = 64e-5
RWKV_CHUNK = 64
GLA_HEADS = 4
GLA_DK = 128
GLA_DV = 256
GLA_CHUNK = 64
GLA_GATE_NORM = 16.0
XA_HEADS = 4
XA_HEAD_DIM = 128
MOE_GROUPS = 4
MOE_EXPERTS_PER_GROUP = 8
MOE_EXPERTS = 32
MOE_BLOCK = 512
LANES = 128
SUBLANES = 8
ROW_TILE = SUBLANES * LANES

VMEM_LIMIT_BYTES = 48 * 1024 * 1024


def _cparams(n_axes):
    return pltpu.CompilerParams(dimension_semantics=("arbitrary",) * n_axes,
                                vmem_limit_bytes=VMEM_LIMIT_BYTES)


def _dot(a, b):
    return jnp.dot(a.astype(BF16), b.astype(BF16), preferred_element_type=F32)


def _dot_nt(a, b):
    return lax.dot_general(a.astype(BF16), b.astype(BF16), (((1,), (1,)), ((), ())),
                           preferred_element_type=F32)


def _dot_tn(a, b):
    return lax.dot_general(a.astype(BF16), b.astype(BF16), (((0,), (0,)), ((), ())),
                           preferred_element_type=F32)


def _dot_f32(a, b):
    return jnp.dot(a, b, preferred_element_type=F32, precision=lax.Precision.HIGHEST)


def _rms(x, g):
    ms = jnp.mean(x * x, axis=-1, keepdims=True)
    return x * lax.rsqrt(ms + EPS) * g


def _sigmoid(x):
    return 1.0 / (1.0 + jnp.exp(-x))


def _softplus(x):
    return jnp.maximum(x, 0.0) + jnp.log(1.0 + jnp.exp(-jnp.abs(x)))


def _norm_matmul_kernel(x_ref, g_ref, w_ref, *o_refs, splits):
    h = _rms(x_ref[...], g_ref[...]).astype(BF16)
    off = 0
    for o_ref, n in zip(o_refs, splits):
        o_ref[...] = jnp.dot(h, w_ref[:, off:off + n], preferred_element_type=F32).astype(o_ref.dtype)
        off += n


def _norm_matmul(x, g, w, splits, out_dtypes, tm=256):
    T, D = x.shape
    N = w.shape[1]
    assert sum(splits) == N and T % tm == 0
    return pl.pallas_call(
        functools.partial(_norm_matmul_kernel, splits=tuple(splits)),
        grid=(T // tm,),
        in_specs=[pl.BlockSpec((tm, D), lambda i: (i, 0)),
                  pl.BlockSpec((1, D), lambda i: (0, 0)),
                  pl.BlockSpec((D, N), lambda i: (0, 0))],
        out_specs=[pl.BlockSpec((tm, n), lambda i: (i, 0)) for n in splits],
        out_shape=[jax.ShapeDtypeStruct((T, n), dt) for n, dt in zip(splits, out_dtypes)],
        compiler_params=_cparams(1),
        name="norm_matmul",
    )(x, g.reshape(1, D), w)


def _proj_residual_kernel(*refs, n_in):
    x_ref = refs[0]
    a_refs = refs[1:1 + n_in]
    w_refs = refs[1 + n_in:1 + 2 * n_in]
    o_ref = refs[1 + 2 * n_in]
    acc = x_ref[...]
    for a_ref, w_ref in zip(a_refs, w_refs):
        acc = acc + jnp.dot(a_ref[...], w_ref[...], preferred_element_type=F32)
    o_ref[...] = acc


def _proj_residual(x, acts, weights, tm=512):
    T, D = x.shape
    n_in = len(acts)
    in_specs = [pl.BlockSpec((tm, D), lambda i: (i, 0))]
    in_specs += [pl.BlockSpec((tm, a.shape[1]), lambda i: (i, 0)) for a in acts]
    in_specs += [pl.BlockSpec(w.shape, lambda i: (0, 0)) for w in weights]
    return pl.pallas_call(
        functools.partial(_proj_residual_kernel, n_in=n_in),
        grid=(T // tm,),
        in_specs=in_specs,
        out_specs=pl.BlockSpec((tm, D), lambda i: (i, 0)),
        out_shape=jax.ShapeDtypeStruct((T, D), F32),
        compiler_params=_cparams(1),
        name="proj_residual",
    )(x, *acts, *weights)


def _swa_kernel(sinks_ref, q_ref, kp_ref, kc_ref, vp_ref, vc_ref, o_ref):
    n = pl.program_id(1)
    W = SWA_WINDOW
    q = q_ref[...]
    k = jnp.concatenate([kp_ref[...], kc_ref[...]], axis=0)
    v = jnp.concatenate([vp_ref[...], vc_ref[...]], axis=0)
    qpos = lax.broadcasted_iota(I32, (W, 2 * W), 0) + W
    kpos = lax.broadcasted_iota(I32, (W, 2 * W), 1)
    rel = qpos - kpos
    in_window = jnp.where(rel >= 0, jnp.where(rel < W, 1, 0), 0)
    has_prev = jnp.where(n > 0, 1, 0)
    valid = (in_window * jnp.where(kpos >= W, 1, has_prev)) > 0
    n_groups = SWA_Q_HEADS // SWA_GROUP
    qb = q.astype(BF16)
    kb = k.astype(BF16)
    vb = v.astype(BF16)
    scores = []
    for g in range(n_groups):
        qg = jnp.concatenate([qb[:, h * HEAD_DIM:(h + 1) * HEAD_DIM]
                              for h in range(g * SWA_GROUP, (g + 1) * SWA_GROUP)], axis=0)
        scores.append(_dot_nt(qg, kb[:, g * HEAD_DIM:(g + 1) * HEAD_DIM]))
    probs = []
    for g in range(n_groups):
        pieces = []
        for j in range(SWA_GROUP):
            s = jnp.where(valid, scores[g][j * W:(j + 1) * W] * (HEAD_DIM ** -0.5), -jnp.inf)
            sink = sinks_ref[g * SWA_GROUP + j]
            m = jnp.maximum(jnp.max(s, axis=-1, keepdims=True), sink)
            p = jnp.exp(s - m)
            den = jnp.sum(p, axis=-1, keepdims=True) + jnp.exp(sink - m)
            pieces.append((p / den).astype(BF16))
        probs.append(jnp.concatenate(pieces, axis=0))
    outs = []
    for g in range(n_groups):
        og = _dot(probs[g], vb[:, g * HEAD_DIM:(g + 1) * HEAD_DIM])
        outs += [og[j * W:(j + 1) * W] for j in range(SWA_GROUP)]
    o_ref[...] = jnp.concatenate(outs, axis=1).astype(o_ref.dtype)


def _swa(qkv, sinks, B, S):
    W = SWA_WINDOW
    qkv3 = qkv.reshape(B, S, qkv.shape[-1])
    qw = SWA_Q_HEADS * HEAD_DIM
    kw = qw // SWA_GROUP
    kcol = qw // kw
    out = pl.pallas_call(
        _swa_kernel,
        grid=(B, S // W),
        in_specs=[pl.BlockSpec(memory_space=pltpu.SMEM),
                  pl.BlockSpec((None, W, qw), lambda b, n: (b, n, 0)),
                  pl.BlockSpec((None, W, kw), lambda b, n: (b, jnp.maximum(n - 1, 0), kcol)),
                  pl.BlockSpec((None, W, kw), lambda b, n: (b, n, kcol)),
                  pl.BlockSpec((None, W, kw), lambda b, n: (b, jnp.maximum(n - 1, 0), kcol + 1)),
                  pl.BlockSpec((None, W, kw), lambda b, n: (b, n, kcol + 1))],
        out_specs=pl.BlockSpec((None, W, qw), lambda b, n: (b, n, 0)),
        out_shape=jax.ShapeDtypeStruct((B, S, qw), BF16),
        compiler_params=_cparams(2),
        name="swa",
    )(sinks, qkv3, qkv3, qkv3, qkv3, qkv3)
    return out.reshape(B * S, qw)


def _rwkv_prep_kernel(p_ref, pprev_ref, mu_ref, w0_ref, w2_ref, a0_ref, a2_ref, g2_ref, kk_ref, ka_ref,
                      r_out, lw_out, k_out, v_out, a_out, b_out, g_out):
    n = pl.program_id(1)
    C = RWKV_WIDTH
    p = p_ref[...]
    last = jnp.where(n > 0, pprev_ref[7:8, :], 0.0)
    row = lax.broadcasted_iota(I32, p.shape, 0)
    p_prev = jnp.where(row == 0, last, pltpu.roll(p, 1, axis=0))
    p = p + (p_prev - p) * mu_ref[...]
    r = p[:, :C]
    k = p[:, C:2 * C]
    v = p[:, 2 * C:3 * C]
    xw = p[:, 3 * C:3 * C + 64]
    xa = p[:, 3 * C + 64:3 * C + 128]
    xg = p[:, 3 * C + 128:]
    w = -_softplus(-(w0_ref[...] + _dot(jnp.tanh(xw), w2_ref[...]))) - 0.5
    lw = -jnp.exp(w)
    a = _sigmoid(a0_ref[...] + _dot(xa, a2_ref[...]))
    g = _dot(_sigmoid(xg), g2_ref[...])
    kk = k * kk_ref[...]
    pieces = []
    for h in range(RWKV_HEADS):
        kh = kk[:, h * HEAD_DIM:(h + 1) * HEAD_DIM]
        nrm = jnp.sqrt(jnp.sum(kh * kh, axis=-1, keepdims=True))
        pieces.append(kh / jnp.maximum(nrm, 1e-12))
    kk = jnp.concatenate(pieces, axis=1)
    r_out[...] = r
    lw_out[...] = lw
    k_out[...] = k * (1.0 + (a - 1.0) * ka_ref[...])
    v_out[...] = v
    a_out[...] = -kk
    b_out[...] = kk * a
    g_out[...] = g


def _rwkv_prep(p, mu, w0, w2, a0, a2, g2, k_k, k_a, B, S, tt=256):
    C = RWKV_WIDTH
    PW = p.shape[-1]
    p3 = p.reshape(B, S, PW)
    row = lambda t: t.reshape(1, -1)
    full = lambda arr: pl.BlockSpec(arr.shape, lambda b, n: (0,) * arr.ndim)
    params = [row(mu), row(w0), w2.astype(BF16), row(a0), a2.astype(BF16), g2.astype(BF16), row(k_k), row(k_a)]
    outs = pl.pallas_call(
        _rwkv_prep_kernel,
        grid=(B, S // tt),
        in_specs=[pl.BlockSpec((None, tt, PW), lambda b, n: (b, n, 0)),
                  pl.BlockSpec((None, 8, PW), lambda b, n: (b, jnp.maximum(n * (tt // 8) - 1, 0), 0))]
                 + [full(t) for t in params],
        out_specs=[pl.BlockSpec((None, tt, C), lambda b, n: (b, n, 0))] * 7,
        out_shape=[jax.ShapeDtypeStruct((B, S, C), F32)] * 7,
        compiler_params=_cparams(2),
        name="rwkv_prep",
    )(p3, p3, *params)
    return outs


def _rwkv_scan_kernel(r_ref, lw_ref, k_ref, v_ref, a_ref, b_ref, g_ref, rk_ref, lnw_ref, lnb_ref,
                      o_ref, s_ref):
    c = pl.program_id(1)

    @pl.when(c == 0)
    def _():
        s_ref[...] = jnp.zeros_like(s_ref)

    C = RWKV_CHUNK
    row = lax.broadcasted_iota(I32, (C, C), 0)
    col = lax.broadcasted_iota(I32, (C, C), 1)
    lower = row >= col
    strict = row > col

    lw = lw_ref[...]
    cum = _dot_f32(jnp.where(lower, 1.0, 0.0).astype(F32), lw)
    cum_last = cum[C - 1:C, :]
    r = r_ref[...]
    k = k_ref[...]
    v = v_ref[...]
    a = a_ref[...]
    b = b_ref[...]
    g = g_ref[...]
    e_neg = jnp.exp(-cum)
    e_rem = jnp.exp(cum_last - cum)
    r_t = (r * jnp.exp(cum)).astype(BF16)
    a_t = (a * jnp.exp(cum - lw)).astype(BF16)
    b_t = (b * e_neg).astype(BF16)
    k_t = (k * e_neg).astype(BF16)
    b_d = (b * e_rem).astype(BF16)
    k_d = (k * e_rem).astype(BF16)
    v_b = v.astype(BF16)
    e_last = jnp.exp(cum_last)
    rkk = r * k * rk_ref[...]
    lnw = lnw_ref[...]
    lnb = lnb_ref[...]
    row2 = lax.broadcasted_iota(I32, (C, 2 * C), 0)
    col2 = lax.broadcasted_iota(I32, (C, 2 * C), 1)
    colm = jnp.where(col2 >= C, col2 - C, col2)
    lower2 = row2 >= colm
    strict_k = jnp.where(col2 >= C, jnp.where(row2 > colm, 1, 0), 0) > 0

    heads = range(RWKV_HEADS)
    sls = [slice(h * HEAD_DIM, (h + 1) * HEAD_DIM) for h in heads]
    s0s = [s_ref[h] for h in heads]
    ars_l = [jnp.concatenate([a_t[:, sl], r_t[:, sl]], axis=0) for sl in sls]
    bks = [jnp.concatenate([b_t[:, sl], k_t[:, sl]], axis=0) for sl in sls]
    ms = [_dot_nt(ars_l[h], bks[h]) for h in heads]
    arss = [_dot_nt(ars_l[h], s0s[h]) for h in heads]
    vhs = [v_b[:, sl] for sl in sls]
    xs = [arss[h][:C] + _dot(jnp.where(strict_k, ms[h][:C], 0.0), jnp.concatenate([vhs[h], vhs[h]], axis=0))
          for h in heads]
    pws = [jnp.where(strict, ms[h][:C, :C], 0.0).astype(BF16) for h in heads]
    xs = [xs[h] + _dot(pws[h], xs[h]) for h in heads]
    for _ in range(5):
        pws = [_dot(pws[h], pws[h]).astype(BF16) for h in heads]
        xs = [xs[h] + _dot(pws[h], xs[h]) for h in heads]
    uvs = [jnp.concatenate([xs[h].astype(BF16), vhs[h]], axis=0) for h in heads]
    ys = [arss[h][C:] + _dot(jnp.where(lower2, ms[h][C:], 0.0), uvs[h]) for h in heads]
    for h in heads:
        bkd = jnp.concatenate([b_d[:, sls[h]], k_d[:, sls[h]]], axis=0)
        s_ref[h] = s0s[h] * e_last[:, sls[h]] + _dot_tn(uvs[h], bkd)

    outs = []
    for h in heads:
        sl = sls[h]
        y = ys[h]
        mean = jnp.mean(y, axis=-1, keepdims=True)
        yc = y - mean
        var = jnp.mean(yc * yc, axis=-1, keepdims=True)
        yn = yc * lax.rsqrt(var + RWKV_LN_EPS) * lnw[:, sl] + lnb[:, sl]
        bonus = jnp.sum(rkk[:, sl], axis=-1, keepdims=True) * v[:, sl]
        outs.append((yn + bonus) * g[:, sl])
    o_ref[...] = jnp.concatenate(outs, axis=1).astype(o_ref.dtype)


def _rwkv_scan(r, lw, k, v, a, b, g, r_k, lnx_w, lnx_b):
    B, S, W = r.shape
    C = RWKV_CHUNK
    seq = pl.BlockSpec((None, C, W), lambda bb, c: (bb, c, 0))
    par = pl.BlockSpec((1, W), lambda bb, c: (0, 0))
    out = pl.pallas_call(
        _rwkv_scan_kernel,
        grid=(B, S // C),
        in_specs=[seq] * 7 + [par] * 3,
        out_specs=seq,
        out_shape=jax.ShapeDtypeStruct((B, S, W), BF16),
        scratch_shapes=[pltpu.VMEM((RWKV_HEADS, HEAD_DIM, HEAD_DIM), F32)],
        compiler_params=_cparams(2),
        name="rwkv_scan",
    )(r, lw, k, v, a, b, g, r_k.reshape(1, W), lnx_w.reshape(1, W), lnx_b.reshape(1, W))
    return out.reshape(B * S, W)


def _gla_kernel(q_ref, k_ref, v_ref, og_ref, gd_ref, gup_ref, gb_ref, on_ref, o_ref, s_ref):
    c = pl.program_id(1)

    @pl.when(c == 0)
    def _():
        s_ref[...] = jnp.zeros_like(s_ref)

    C = GLA_CHUNK
    row = lax.broadcasted_iota(I32, (C, C), 0)
    col = lax.broadcasted_iota(I32, (C, C), 1)
    lower = row >= col
    z = _dot(gd_ref[...], gup_ref[...]) + gb_ref[...]
    gk = -_softplus(-z) / GLA_GATE_NORM
    cum = _dot_f32(jnp.where(lower, 1.0, 0.0).astype(F32), gk)
    cum_last = cum[C - 1:C, :]
    q = q_ref[...] * (GLA_DK ** -0.5)
    k = k_ref[...]
    qe = (q * jnp.exp(cum)).astype(BF16)
    ke = (k * jnp.exp(-cum)).astype(BF16)
    kd = (k * jnp.exp(cum_last - cum)).astype(BF16)
    e_last = jnp.exp(cum_last)
    v = v_ref[...].astype(BF16)
    og = og_ref[...]
    onorm = on_ref[...]
    heads = range(GLA_HEADS)
    kss = [slice(h * GLA_DK, (h + 1) * GLA_DK) for h in heads]
    vss = [slice(h * GLA_DV, (h + 1) * GLA_DV) for h in heads]
    sts = [s_ref[h] for h in heads]
    atts = [jnp.where(lower, _dot_nt(qe[:, kss[h]], ke[:, kss[h]]), 0.0) for h in heads]
    inters = [_dot_nt(qe[:, kss[h]], sts[h]) for h in heads]
    os_ = [inters[h] + _dot(atts[h], v[:, vss[h]]) for h in heads]
    for h in heads:
        s_ref[h] = sts[h] * e_last[:, kss[h]] + _dot_tn(v[:, vss[h]], kd[:, kss[h]])
    outs = []
    for h in heads:
        gate = og[:, vss[h]]
        outs.append(_rms(os_[h], onorm) * (gate * _sigmoid(gate)))
    o_ref[...] = jnp.concatenate(outs, axis=1).astype(o_ref.dtype)


def _gla(qkvo, gd, gate_up_pad, gate_b, onorm, B, S):
    C = GLA_CHUNK
    KW = GLA_HEADS * GLA_DK
    VW = GLA_HEADS * GLA_DV
    x3 = qkvo.reshape(B, S, qkvo.shape[-1])
    gd3 = gd.reshape(B, S, LANES)
    kb = KW // KW
    out = pl.pallas_call(
        _gla_kernel,
        grid=(B, S // C),
        in_specs=[pl.BlockSpec((None, C, KW), lambda b, c: (b, c, 0)),
                  pl.BlockSpec((None, C, KW), lambda b, c: (b, c, kb)),
                  pl.BlockSpec((None, C, VW), lambda b, c: (b, c, 1)),
                  pl.BlockSpec((None, C, VW), lambda b, c: (b, c, 2)),
                  pl.BlockSpec((None, C, LANES), lambda b, c: (b, c, 0)),
                  pl.BlockSpec((LANES, KW), lambda b, c: (0, 0)),
                  pl.BlockSpec((1, KW), lambda b, c: (0, 0)),
                  pl.BlockSpec((1, GLA_DV), lambda b, c: (0, 0))],
        out_specs=pl.BlockSpec((None, C, VW), lambda b, c: (b, c, 0)),
        out_shape=jax.ShapeDtypeStruct((B, S, VW), BF16),
        scratch_shapes=[pltpu.VMEM((GLA_HEADS, GLA_DV, GLA_DK), F32)],
        compiler_params=_cparams(2),
        name="gla",
    )(x3, x3, x3, x3, gd3, gate_up_pad, gate_b.reshape(1, KW), onorm.reshape(1, GLA_DV))
    return out.reshape(B * S, VW)


def _xattn_kernel(x_ref, g_ref, wq_ref, mk_ref, mv_ref, wo_ref, o_ref):
    x = x_ref[...]
    h = _rms(x, g_ref[...])
    q = _dot(h, wq_ref[...]).astype(BF16)
    mk = mk_ref[...]
    mv = mv_ref[...]
    sls = [slice(hd * XA_HEAD_DIM, (hd + 1) * XA_HEAD_DIM) for hd in range(XA_HEADS)]
    scores = [_dot_nt(q[:, sl], mk[:, sl]) for sl in sls]
    probs = []
    for s in scores:
        s = s * (XA_HEAD_DIM ** -0.5)
        p = jnp.exp(s - jnp.max(s, axis=-1, keepdims=True))
        probs.append((p / jnp.sum(p, axis=-1, keepdims=True)).astype(BF16))
    o = jnp.concatenate([_dot(p, mv[:, sl]) for p, sl in zip(probs, sls)], axis=1)
    o_ref[...] = x + _dot(o, wo_ref[...])


def _xattn(x, g, wq, mk, mv, wo, B, S, tq=256):
    D = x.shape[-1]
    M = mk.shape[0] // B
    XW = mk.shape[-1]
    x3 = x.reshape(B, S, D)
    out = pl.pallas_call(
        _xattn_kernel,
        grid=(B, S // tq),
        in_specs=[pl.BlockSpec((None, tq, D), lambda b, n: (b, n, 0)),
                  pl.BlockSpec((1, D), lambda b, n: (0, 0)),
                  pl.BlockSpec((D, XW), lambda b, n: (0, 0)),
                  pl.BlockSpec((None, M, XW), lambda b, n: (b, 0, 0)),
                  pl.BlockSpec((None, M, XW), lambda b, n: (b, 0, 0)),
                  pl.BlockSpec((XW, D), lambda b, n: (0, 0))],
        out_specs=pl.BlockSpec((None, tq, D), lambda b, n: (b, n, 0)),
        out_shape=jax.ShapeDtypeStruct((B, S, D), F32),
        compiler_params=_cparams(2),
        name="xattn",
    )(x3, g.reshape(1, D), wq, mk.reshape(B, M, XW), mv.reshape(B, M, XW), wo)
    return out.reshape(B * S, D)


def _router_kernel(x_ref, g_ref, w_ref, b_ref, info_ref, cnt_ref, carry_ref):
    i = pl.program_id(0)

    @pl.when(i == 0)
    def _():
        carry_ref[...] = jnp.zeros_like(carry_ref)

    h = _rms(x_ref[...], g_ref[...])
    logits = _dot(h, w_ref[...]) + b_ref[...]
    tm = logits.shape[0]
    lane = lax.broadcasted_iota(I32, logits.shape, 1)
    big = jnp.int32(LANES)
    neg = -jnp.inf
    gl = jnp.where(lane < MOE_GROUPS, logits, neg)
    gmax = jnp.max(gl, axis=-1, keepdims=True)
    g_top = jnp.min(jnp.where(gl == gmax, lane, big), axis=-1, keepdims=True)
    p_group = 1.0 / jnp.sum(jnp.exp(gl - gmax), axis=-1, keepdims=True)
    lo = MOE_GROUPS + MOE_EXPERTS_PER_GROUP * g_top
    in_group = jnp.where(lane >= lo, jnp.where(lane < lo + MOE_EXPERTS_PER_GROUP, 1, 0), 0) > 0
    el = jnp.where(in_group, logits, neg)
    emax = jnp.max(el, axis=-1, keepdims=True)
    ee = jnp.exp(el - emax)
    prob = ee / jnp.sum(ee, axis=-1, keepdims=True)
    prob = jnp.where(in_group, prob, -1.0)
    p1 = jnp.max(prob, axis=-1, keepdims=True)
    i1 = jnp.min(jnp.where(prob == p1, lane, big), axis=-1, keepdims=True)
    rest = jnp.where(lane == i1, -1.0, prob)
    p2 = jnp.max(rest, axis=-1, keepdims=True)
    i2 = jnp.min(jnp.where(rest == p2, lane, big), axis=-1, keepdims=True)
    tot = p1 + p2
    e1 = i1 - MOE_GROUPS
    e2 = i2 - MOE_GROUPS
    oh1 = jnp.where(lane == e1, 1.0, 0.0)
    oh2 = jnp.where(lane == e2, 1.0, 0.0)
    row = lax.broadcasted_iota(I32, (tm, tm), 0)
    col = lax.broadcasted_iota(I32, (tm, tm), 1)
    before = jnp.where(row > col, 1.0, 0.0)
    pre = _dot(before, jnp.concatenate([oh1, oh2], axis=1))
    tot1 = jnp.sum(oh1, axis=0, keepdims=True)
    tot2 = jnp.sum(oh2, axis=0, keepdims=True)
    carry = carry_ref[...]
    r1 = jnp.sum(oh1 * (carry + pre[:, :LANES]), axis=-1, keepdims=True)
    r2 = jnp.sum(oh2 * (carry + tot1 + pre[:, LANES:]), axis=-1, keepdims=True)
    carry = carry + tot1 + tot2
    carry_ref[...] = carry
    cnt_ref[...] = carry
    g1 = p_group * p1 / tot
    g2 = p_group * p2 / tot
    vals = [e1.astype(F32), e2.astype(F32), r1, r2, g1, g2]
    info = jnp.zeros_like(logits)
    for j, val in enumerate(vals):
        info = jnp.where(lane == j, val, info)
    info_ref[...] = info


def _router(x, g, w_router, b_router, tm=256):
    T, D = x.shape
    return pl.pallas_call(
        _router_kernel,
        grid=(T // tm,),
        in_specs=[pl.BlockSpec((tm, D), lambda i: (i, 0)),
                  pl.BlockSpec((1, D), lambda i: (0, 0)),
                  pl.BlockSpec((D, LANES), lambda i: (0, 0)),
                  pl.BlockSpec((1, LANES), lambda i: (0, 0))],
        out_specs=[pl.BlockSpec((tm, LANES), lambda i: (i, 0)),
                   pl.BlockSpec((1, LANES), lambda i: (0, 0))],
        out_shape=[jax.ShapeDtypeStruct((T, LANES), F32), jax.ShapeDtypeStruct((1, LANES), F32)],
        scratch_shapes=[pltpu.VMEM((1, LANES), F32)],
        compiler_params=_cparams(1),
        name="router",
    )(x, g.reshape(1, D), w_router, b_router)


def _row_bytes_wait(hbm, buf, sem):
    pltpu.make_async_copy(buf, hbm.at[pl.ds(0, buf.shape[0]), :], sem).wait()


def _to_row_tiles(ref, val):
    n = val.shape[0]
    for c in range(SUBLANES):
        ref[pl.ds(c, n, stride=SUBLANES), :] = val[:, c * LANES:(c + 1) * LANES]


def _from_row_tiles(ref):
    n = ref.shape[0] // SUBLANES
    return jnp.concatenate([ref[pl.ds(c, n, stride=SUBLANES), :] for c in range(SUBLANES)], axis=1)


def _moe_dispatch_kernel(pends_ref, cnt_ref, dest_ref, x_ref, g_ref, hs_hbm, hbuf, zbuf, sems, zsem, *, td):
    i = pl.program_id(0)
    nt = pl.num_programs(0)
    slot = lax.rem(i, 2)

    @pl.when(i == 0)
    def _():
        zbuf[...] = jnp.zeros_like(zbuf)
        for e in range(MOE_EXPERTS):
            @pl.when(cnt_ref[e] > 0)
            def _():
                start = pl.multiple_of((pends_ref[e] - MOE_BLOCK) * SUBLANES, MOE_BLOCK)
                pltpu.make_async_copy(zbuf, hs_hbm.at[pl.ds(start, MOE_BLOCK * SUBLANES), :], zsem).start()
        for e in range(MOE_EXPERTS):
            @pl.when(cnt_ref[e] > 0)
            def _():
                pltpu.make_async_copy(zbuf, hs_hbm.at[pl.ds(0, MOE_BLOCK * SUBLANES), :], zsem).wait()

        first_unused = pends_ref[MOE_EXPERTS - 1] // MOE_BLOCK
        n_blocks = hs_hbm.shape[0] // (MOE_BLOCK * SUBLANES)

        def zero_start(blk, carry):
            start = pl.multiple_of(blk * (MOE_BLOCK * SUBLANES), MOE_BLOCK)
            pltpu.make_async_copy(zbuf, hs_hbm.at[pl.ds(start, MOE_BLOCK * SUBLANES), :], zsem).start()
            return carry

        def zero_wait(blk, carry):
            pltpu.make_async_copy(zbuf, hs_hbm.at[pl.ds(0, MOE_BLOCK * SUBLANES), :], zsem).wait()
            return carry

        lax.fori_loop(first_unused, n_blocks, zero_start, 0)
        lax.fori_loop(first_unused, n_blocks, zero_wait, 0)

    hb = hbuf.at[slot]
    _to_row_tiles(hb, _rms(x_ref[...], g_ref[...]))
    for j in range(td):
        for c in range(2):
            row = pl.multiple_of(dest_ref[0, c * td + j] * SUBLANES, SUBLANES)
            pltpu.make_async_copy(hb.at[pl.ds(j * SUBLANES, SUBLANES), :],
                                  hs_hbm.at[pl.ds(row, SUBLANES), :],
                                  sems.at[slot]).start(priority=c)

    @pl.when(i > 0)
    def _():
        other = hbuf.at[1 - slot]
        _row_bytes_wait(hs_hbm, other, sems.at[1 - slot])
        _row_bytes_wait(hs_hbm, other, sems.at[1 - slot])

    @pl.when(i == nt - 1)
    def _():
        _row_bytes_wait(hs_hbm, hb, sems.at[slot])
        _row_bytes_wait(hs_hbm, hb, sems.at[slot])


def _moe_dispatch(x, g, pends, counts, dest3, P, td):
    T, D = x.shape
    assert D == ROW_TILE
    grid_spec = pltpu.PrefetchScalarGridSpec(
        num_scalar_prefetch=2,
        grid=(T // td,),
        in_specs=[pl.BlockSpec((None, 1, 2 * td), lambda i, pe, cn: (i, 0, 0), memory_space=pltpu.SMEM),
                  pl.BlockSpec((td, D), lambda i, pe, cn: (i, 0)),
                  pl.BlockSpec((1, D), lambda i, pe, cn: (0, 0))],
        out_specs=pl.BlockSpec(memory_space=pl.ANY),
        scratch_shapes=[pltpu.VMEM((2, td * SUBLANES, LANES), F32),
                        pltpu.VMEM((MOE_BLOCK * SUBLANES, LANES), F32),
                        pltpu.SemaphoreType.DMA((2,)),
                        pltpu.SemaphoreType.DMA(())],
    )
    return pl.pallas_call(
        functools.partial(_moe_dispatch_kernel, td=td),
        grid_spec=grid_spec,
        out_shape=jax.ShapeDtypeStruct((P * SUBLANES, LANES), F32),
        compiler_params=_cparams(1),
        name="moe_dispatch",
    )(pends, counts, dest3, x, g.reshape(1, D))


def _moe_expert_kernel(be_ref, nu_ref, hs_ref, w1_ref, w3_ref, w2_ref, o_ref, w1b, w3b, w2b):
    i = pl.program_id(0)
    used = i < nu_ref[0]
    changed = jnp.logical_or(i == 0, be_ref[i] != be_ref[jnp.maximum(i - 1, 0)])

    @pl.when(jnp.logical_and(used, changed))
    def _():
        w1b[...] = w1_ref[...].astype(BF16)
        w3b[...] = w3_ref[...].astype(BF16)
        w2b[...] = w2_ref[...].astype(BF16)

    @pl.when(used)
    def _():
        xe = _from_row_tiles(hs_ref).astype(BF16)
        h1 = jnp.dot(xe, w1b[...], preferred_element_type=F32)
        h3 = jnp.dot(xe, w3b[...], preferred_element_type=F32)
        act = (h1 * _sigmoid(h1) * h3).astype(BF16)
        _to_row_tiles(o_ref, jnp.dot(act, w2b[...], preferred_element_type=F32))

    @pl.when(jnp.logical_not(used))
    def _():
        o_ref[...] = jnp.zeros_like(o_ref)


def _moe_experts(hs, block_e, n_used, w1, w3, w2, layer):
    P = hs.shape[0] // SUBLANES
    D = ROW_TILE
    FF = w1.shape[-1]
    NB = P // MOE_BLOCK
    last = lambda i, nu: jnp.minimum(i, nu[0] - 1)
    grid_spec = pltpu.PrefetchScalarGridSpec(
        num_scalar_prefetch=2,
        grid=(NB,),
        in_specs=[pl.BlockSpec((MOE_BLOCK * SUBLANES, LANES), lambda i, be, nu: (last(i, nu), 0)),
                  pl.BlockSpec((None, None, D, FF), lambda i, be, nu: (layer, be[last(i, nu)], 0, 0)),
                  pl.BlockSpec((None, None, D, FF), lambda i, be, nu: (layer, be[last(i, nu)], 0, 0)),
                  pl.BlockSpec((None, None, FF, D), lambda i, be, nu: (layer, be[last(i, nu)], 0, 0))],
        out_specs=pl.BlockSpec((MOE_BLOCK * SUBLANES, LANES), lambda i, be, nu: (i, 0)),
        scratch_shapes=[pltpu.VMEM((D, FF), BF16),
                        pltpu.VMEM((D, FF), BF16),
                        pltpu.VMEM((FF, D), BF16)],
    )
    return pl.pallas_call(
        _moe_expert_kernel,
        grid_spec=grid_spec,
        out_shape=jax.ShapeDtypeStruct((P * SUBLANES, LANES), F32),
        compiler_params=_cparams(1),
        name="moe_experts",
    )(block_e, n_used, hs, w1, w3, w2)


def _gather_rows(src_hbm, idx_ref, dst_ref, sem, n_rows):
    for r in range(n_rows):
        row = pl.multiple_of(idx_ref[0, r] * SUBLANES, SUBLANES)
        pltpu.make_async_copy(src_hbm.at[pl.ds(row, SUBLANES), :],
                              dst_ref.at[pl.ds(r * SUBLANES, SUBLANES), :], sem).start(priority=r % 2)


def _moe_combine_kernel(pos_ref, posn_ref, x_ref, info_ref, yb_hbm, gf_ref, o_ref, ybuf, sems, *, tc, final_norm):
    i = pl.program_id(0)
    nb = pl.num_programs(0)
    slot = lax.rem(i, 2)

    @pl.when(i == 0)
    def _():
        def issue(r, carry):
            src = pl.multiple_of(pos_ref[0, r] * SUBLANES, SUBLANES)
            dst = pl.multiple_of(r * SUBLANES, SUBLANES)
            pltpu.make_async_copy(yb_hbm.at[pl.ds(src, SUBLANES), :],
                                  ybuf.at[0, pl.ds(dst, SUBLANES), :], sems.at[0]).start()
            return carry
        lax.fori_loop(0, 2 * tc, issue, 0)

    @pl.when(i + 1 < nb)
    def _():
        _gather_rows(yb_hbm, posn_ref, ybuf.at[1 - slot], sems.at[1 - slot], 2 * tc)

    pltpu.make_async_copy(yb_hbm.at[pl.ds(0, 2 * tc * SUBLANES), :], ybuf.at[slot], sems.at[slot]).wait()
    info = info_ref[...]
    yb = ybuf.at[slot]
    y0 = _from_row_tiles(yb.at[pl.ds(0, tc * SUBLANES), :])
    y1 = _from_row_tiles(yb.at[pl.ds(tc * SUBLANES, tc * SUBLANES), :])
    out = x_ref[...] + (y0 * info[:, 4:5] + y1 * info[:, 5:6])
    if final_norm:
        out = _rms(out, gf_ref[...])
    o_ref[...] = out


def _moe_combine(x, info, dest3, yb, g_final, final_norm, tc):
    T, D = x.shape
    NT = T // tc
    return pl.pallas_call(
        functools.partial(_moe_combine_kernel, tc=tc, final_norm=final_norm),
        grid=(NT,),
        in_specs=[pl.BlockSpec((None, 1, 2 * tc), lambda i: (i, 0, 0), memory_space=pltpu.SMEM),
                  pl.BlockSpec((None, 1, 2 * tc), lambda i: (jnp.minimum(i + 1, NT - 1), 0, 0),
                               memory_space=pltpu.SMEM),
                  pl.BlockSpec((tc, D), lambda i: (i, 0)),
                  pl.BlockSpec((tc, LANES), lambda i: (i, 0)),
                  pl.BlockSpec(memory_space=pl.ANY),
                  pl.BlockSpec((1, D), lambda i: (0, 0))],
        out_specs=pl.BlockSpec((tc, D), lambda i: (i, 0)),
        out_shape=jax.ShapeDtypeStruct((T, D), F32),
        scratch_shapes=[pltpu.VMEM((2, 2 * tc * SUBLANES, LANES), F32), pltpu.SemaphoreType.DMA((2,))],
        compiler_params=_cparams(1),
        name="moe_combine",
    )(dest3, dest3, x, info, yb, g_final.reshape(1, D))


MOE_TILE = 128


def _moe_layer(x, g, w_group, b_group, w_expert, b_expert, w1, w3, w2, layer, g_final, final_norm):
    T, D = x.shape
    n_log = MOE_GROUPS + MOE_EXPERTS
    w_router = jnp.zeros((D, LANES), F32).at[:, :MOE_GROUPS].set(w_group).at[:, MOE_GROUPS:n_log].set(w_expert)
    b_router = jnp.zeros((1, LANES), F32).at[0, :MOE_GROUPS].set(b_group).at[0, MOE_GROUPS:n_log].set(b_expert)
    info, cnt = _router(x, g, w_router.astype(BF16), b_router)
    P = 2 * T + MOE_EXPERTS * MOE_BLOCK
    NB = P // MOE_BLOCK
    counts = cnt[0, :MOE_EXPERTS].astype(I32)
    padded = (counts + MOE_BLOCK - 1) // MOE_BLOCK * MOE_BLOCK
    pends = jnp.cumsum(padded).astype(I32)
    pstarts = pends - padded
    block_start = jnp.arange(NB, dtype=I32) * MOE_BLOCK
    block_e = jnp.minimum(jnp.sum((pends[None, :] <= block_start[:, None]).astype(I32), axis=1),
                          MOE_EXPERTS - 1).astype(I32)
    n_used = (pends[-1:] // MOE_BLOCK).astype(I32)
    eid = info[:, 0:2].astype(I32)
    dest = pstarts[eid] + info[:, 2:4].astype(I32)
    NT = T // MOE_TILE
    dest3 = dest.reshape(NT, MOE_TILE, 2).transpose(0, 2, 1).reshape(NT, 1, 2 * MOE_TILE)
    hs = _moe_dispatch(x, g, pends, counts, dest3, P, MOE_TILE)
    yb = _moe_experts(hs, block_e, n_used, w1, w3, w2, layer)
    return _moe_combine(x, info, dest3, yb, g_final, final_norm, MOE_TILE)


def kernel(x, mem, norm_mix, norm_xattn, norm_moe, norm_final, ev_w_in, ev_sinks, ev_mu, ev_w0, ev_w2, ev_a0, ev_a2, ev_g2, ev_k_k, ev_k_a, ev_r_k, ev_lnx_w, ev_lnx_b, ev_w_out, od_w_in, od_gate_up, od_gate_b, od_onorm, od_w_out, mem_norm, mem_wk, mem_wv, xa_wq, xa_wo, moe_w_group, moe_b_group, moe_w_expert, moe_b_expert, moe_w1, moe_w3, moe_w2):
    B, S, D = x.shape
    M = mem.shape[1]
    T = B * S
    depth = norm_mix.shape[0]
    xf = x.reshape(T, D)

    XW = XA_HEADS * XA_HEAD_DIM
    w_kv = jnp.concatenate([mem_wk, mem_wv], axis=1).astype(BF16)
    mk, mv = _norm_matmul(mem.reshape(B * M, D), mem_norm, w_kv, (XW, XW), (BF16, BF16))

    for layer in range(depth):
        i = layer // 2
        if layer % 2 == 0:
            swa_cols = SWA_Q_HEADS * HEAD_DIM + 2 * (SWA_Q_HEADS // SWA_GROUP) * HEAD_DIM
            rw_cols = ev_w_in.shape[-1] - swa_cols
            qkv, p_rw = _norm_matmul(xf, norm_mix[layer], ev_w_in[i].astype(BF16),
                                     (swa_cols, rw_cols), (F32, F32))
            o_a = _swa(qkv, ev_sinks[i], B, S)
            r, lw, k, v, a, b, g = _rwkv_prep(p_rw, ev_mu[i], ev_w0[i], ev_w2[i], ev_a0[i], ev_a2[i],
                                              ev_g2[i], ev_k_k[i], ev_k_a[i], B, S)
            o_b = _rwkv_scan(r, lw, k, v, a, b, g, ev_r_k[i].reshape(-1), ev_lnx_w[i], ev_lnx_b[i])
            w_out = ev_w_out[i].astype(BF16)
            qw = o_a.shape[-1]
            xf = _proj_residual(xf, [o_a, o_b], [w_out[:qw], w_out[qw:]])
        else:
            KW = GLA_HEADS * GLA_DK
            VW = GLA_HEADS * GLA_DV
            R = od_gate_up.shape[1]
            w = od_w_in[i]
            w_re = jnp.concatenate([w[:, :2 * KW + VW], w[:, 2 * KW + VW + R:],
                                    w[:, 2 * KW + VW:2 * KW + VW + R],
                                    jnp.zeros((D, LANES - R), F32)], axis=1).astype(BF16)
            qkvo, gd = _norm_matmul(xf, norm_mix[layer], w_re, (2 * KW + 2 * VW, LANES), (F32, F32))
            gup = jnp.zeros((LANES, KW), F32).at[:R].set(od_gate_up[i]).astype(BF16)
            o = _gla(qkvo, gd, gup, od_gate_b[i], od_onorm[i], B, S)
            xf = _proj_residual(xf, [o], [od_w_out[i].astype(BF16)])
        xf = _xattn(xf, norm_xattn[layer], xa_wq[layer].astype(BF16), mk, mv, xa_wo[layer].astype(BF16), B, S)
        xf = _moe_layer(xf, norm_moe[layer], moe_w_group[layer], moe_b_group[layer], moe_w_expert[layer],
                        moe_b_expert[layer], moe_w1, moe_w3, moe_w2, layer,
                        norm_final, layer == depth - 1)
    return xf.reshape(B, S, D)
```

```python
import functools

import jax
import jax.numpy as jnp
from jax import lax
from jax.experimental import pallas as pl
from jax.experimental.pallas import tpu as pltpu

F32 = jnp.float32
BF16 = jnp.bfloat16
I32 = jnp.int32

EPS = 1e-6
HEAD_DIM = 64
SWA_WINDOW = 128
SWA_Q_HEADS = 8
SWA_GROUP = 4
RWKV_HEADS = 8
RWKV_WIDTH = 512
RWKV_LN_EPS = 64e-5
RWKV_CHUNK = 64
GLA_HEADS = 4
GLA_DK = 128
GLA_DV = 256
GLA_CHUNK = 64
GLA_GATE_NORM = 16.0
XA_HEADS = 4
XA_HEAD_DIM = 128
MOE_GROUPS = 4
MOE_EXPERTS_PER_GROUP = 8
MOE_EXPERTS = 32
MOE_BLOCK = 512
LANES = 128
SUBLANES = 8
ROW_TILE = SUBLANES * LANES

VMEM_LIMIT_BYTES = 48 * 1024 * 1024


def _cparams(n_axes):
    return pltpu.CompilerParams(dimension_semantics=("arbitrary",) * n_axes,
                                vmem_limit_bytes=VMEM_LIMIT_BYTES)


def _dot(a, b):
    return jnp.dot(a.astype(BF16), b.astype(BF16), preferred_element_type=F32)


def _dot_nt(a, b):
    return lax.dot_general(a.astype(BF16), b.astype(BF16), (((1,), (1,)), ((), ())),
                           preferred_element_type=F32)


def _dot_tn(a, b):
    return lax.dot_general(a.astype(BF16), b.astype(BF16), (((0,), (0,)), ((), ())),
                           preferred_element_type=F32)


def _dot_f32(a, b):
    return jnp.dot(a, b, preferred_element_type=F32, precision=lax.Precision.HIGHEST)


def _rms(x, g):
    ms = jnp.mean(x * x, axis=-1, keepdims=True)
    return x * lax.rsqrt(ms + EPS) * g


def _sigmoid(x):
    return 1.0 / (1.0 + jnp.exp(-x))


def _softplus(x):
    return jnp.maximum(x, 0.0) + jnp.log(1.0 + jnp.exp(-jnp.abs(x)))


def _norm_matmul_kernel(x_ref, g_ref, w_ref, *o_refs, splits):
    h = _rms(x_ref[...], g_ref[...]).astype(BF16)
    off = 0
    for o_ref, n in zip(o_refs, splits):
        o_ref[...] = jnp.dot(h, w_ref[:, off:off + n], preferred_element_type=F32).astype(o_ref.dtype)
        off += n


def _norm_matmul(x, g, w, splits, out_dtypes, tm=256):
    T, D = x.shape
    N = w.shape[1]
    assert sum(splits) == N and T % tm == 0
    return pl.pallas_call(
        functools.partial(_norm_matmul_kernel, splits=tuple(splits)),
        grid=(T // tm,),
        in_specs=[pl.BlockSpec((tm, D), lambda i: (i, 0)),
                  pl.BlockSpec((1, D), lambda i: (0, 0)),
                  pl.BlockSpec((D, N), lambda i: (0, 0))],
        out_specs=[pl.BlockSpec((tm, n), lambda i: (i, 0)) for n in splits],
        out_shape=[jax.ShapeDtypeStruct((T, n), dt) for n, dt in zip(splits, out_dtypes)],
        compiler_params=_cparams(1),
        name="norm_matmul",
    )(x, g.reshape(1, D), w)


def _proj_residual_kernel(*refs, n_in):
    x_ref = refs[0]
    a_refs = refs[1:1 + n_in]
    w_refs = refs[1 + n_in:1 + 2 * n_in]
    o_ref = refs[1 + 2 * n_in]
    acc = x_ref[...]
    for a_ref, w_ref in zip(a_refs, w_refs):
        acc = acc + jnp.dot(a_ref[...], w_ref[...], preferred_element_type=F32)
    o_ref[...] = acc


def _proj_residual(x, acts, weights, tm=512):
    T, D = x.shape
    n_in = len(acts)
    in_specs = [pl.BlockSpec((tm, D), lambda i: (i, 0))]
    in_specs += [pl.BlockSpec((tm, a.shape[1]), lambda i: (i, 0)) for a in acts]
    in_specs += [pl.BlockSpec(w.shape, lambda i: (0, 0)) for w in weights]
    return pl.pallas_call(
        functools.partial(_proj_residual_kernel, n_in=n_in),
        grid=(T // tm,),
        in_specs=in_specs,
        out_specs=pl.BlockSpec((tm, D), lambda i: (i, 0)),
        out_shape=jax.ShapeDtypeStruct((T, D), F32),
        compiler_params=_cparams(1),
        name="proj_residual",
    )(x, *acts, *weights)


def _swa_kernel(sinks_ref, q_ref, kp_ref, kc_ref, vp_ref, vc_ref, o_ref):
    n = pl.program_id(1)
    W = SWA_WINDOW
    q = q_ref[...]
    k = jnp.concatenate([kp_ref[...], kc_ref[...]], axis=0)
    v = jnp.concatenate([vp_ref[...], vc_ref[...]], axis=0)
    qpos = lax.broadcasted_iota(I32, (W, 2 * W), 0) + W
    kpos = lax.broadcasted_iota(I32, (W, 2 * W), 1)
    rel = qpos - kpos
    in_window = jnp.where(rel >= 0, jnp.where(rel < W, 1, 0), 0)
    has_prev = jnp.where(n > 0, 1, 0)
    valid = (in_window * jnp.where(kpos >= W, 1, has_prev)) > 0
    n_groups = SWA_Q_HEADS // SWA_GROUP
    qb = q.astype(BF16)
    kb = k.astype(BF16)
    vb = v.astype(BF16)
    scores = []
    for g in range(n_groups):
        qg = jnp.concatenate([qb[:, h * HEAD_DIM:(h + 1) * HEAD_DIM]
                              for h in range(g * SWA_GROUP, (g + 1) * SWA_GROUP)], axis=0)
        scores.append(_dot_nt(qg, kb[:, g * HEAD_DIM:(g + 1) * HEAD_DIM]))
    probs = []
    for g in range(n_groups):
        pieces = []
        for j in range(SWA_GROUP):
            s = jnp.where(valid, scores[g][j * W:(j + 1) * W] * (HEAD_DIM ** -0.5), -jnp.inf)
            sink = sinks_ref[g * SWA_GROUP + j]
            m = jnp.maximum(jnp.max(s, axis=-1, keepdims=True), sink)
            p = jnp.exp(s - m)
            den = jnp.sum(p, axis=-1, keepdims=True) + jnp.exp(sink - m)
            pieces.append((p / den).astype(BF16))
        probs.append(jnp.concatenate(pieces, axis=0))
    outs = []
    for g in range(n_groups):
        og = _dot(probs[g], vb[:, g * HEAD_DIM:(g + 1) * HEAD_DIM])
        outs += [og[j * W:(j + 1) * W] for j in range(SWA_GROUP)]
    o_ref[...] = jnp.concatenate(outs, axis=1).astype(o_ref.dtype)


def _swa(qkv, sinks, B, S):
    W = SWA_WINDOW
    qkv3 = qkv.reshape(B, S, qkv.shape[-1])
    qw = SWA_Q_HEADS * HEAD_DIM
    kw = qw // SWA_GROUP
    kcol = qw // kw
    out = pl.pallas_call(
        _swa_kernel,
        grid=(B, S // W),
        in_specs=[pl.BlockSpec(memory_space=pltpu.SMEM),
                  pl.BlockSpec((None, W, qw), lambda b, n: (b, n, 0)),
                  pl.BlockSpec((None, W, kw), lambda b, n: (b, jnp.maximum(n - 1, 0), kcol)),
                  pl.BlockSpec((None, W, kw), lambda b, n: (b, n, kcol)),
                  pl.BlockSpec((None, W, kw), lambda b, n: (b, jnp.maximum(n - 1, 0), kcol + 1)),
                  pl.BlockSpec((None, W, kw), lambda b, n: (b, n, kcol + 1))],
        out_specs=pl.BlockSpec((None, W, qw), lambda b, n: (b, n, 0)),
        out_shape=jax.ShapeDtypeStruct((B, S, qw), BF16),
        compiler_params=_cparams(2),
        name="swa",
    )(sinks, qkv3, qkv3, qkv3, qkv3, qkv3)
    return out.reshape(B * S, qw)


def _rwkv_prep_kernel(p_ref, pprev_ref, mu_ref, w0_ref, w2_ref, a0_ref, a2_ref, g2_ref, kk_ref, ka_ref,
                      r_out, lw_out, k_out, v_out, a_out, b_out, g_out):
    n = pl.program_id(1)
    C = RWKV_WIDTH
    p = p_ref[...]
    last = jnp.where(n > 0, pprev_ref[7:8, :], 0.0)
    row = lax.broadcasted_iota(I32, p.shape, 0)
    p_prev = jnp.where(row == 0, last, pltpu.roll(p, 1, axis=0))
    p = p + (p_prev - p) * mu_ref[...]
    r = p[:, :C]
    k = p[:, C:2 * C]
    v = p[:, 2 * C:3 * C]
    xw = p[:, 3 * C:3 * C + 64]
    xa = p[:, 3 * C + 64:3 * C + 128]
    xg = p[:, 3 * C + 128:]
    w = -_softplus(-(w0_ref[...] + _dot(jnp.tanh(xw), w2_ref[...]))) - 0.5
    lw = -jnp.exp(w)
    a = _sigmoid(a0_ref[...] + _dot(xa, a2_ref[...]))
    g = _dot(_sigmoid(xg), g2_ref[...])
    kk = k * kk_ref[...]
    pieces = []
    for h in range(RWKV_HEADS):
        kh = kk[:, h * HEAD_DIM:(h + 1) * HEAD_DIM]
        nrm = jnp.sqrt(jnp.sum(kh * kh, axis=-1, keepdims=True))
        pieces.append(kh / jnp.maximum(nrm, 1e-12))
    kk = jnp.concatenate(pieces, axis=1)
    r_out[...] = r
    lw_out[...] = lw
    k_out[...] = k * (1.0 + (a - 1.0) * ka_ref[...])
    v_out[...] = v
    a_out[...] = -kk
    b_out[...] = kk * a
    g_out[...] = g


def _rwkv_prep(p, mu, w0, w2, a0, a2, g2, k_k, k_a, B, S, tt=256):
    C = RWKV_WIDTH
    PW = p.shape[-1]
    p3 = p.reshape(B, S, PW)
    row = lambda t: t.reshape(1, -1)
    full = lambda arr: pl.BlockSpec(arr.shape, lambda b, n: (0,) * arr.ndim)
    params = [row(mu), row(w0), w2.astype(BF16), row(a0), a2.astype(BF16), g2.astype(BF16), row(k_k), row(k_a)]
    outs = pl.pallas_call(
        _rwkv_prep_kernel,
        grid=(B, S // tt),
        in_specs=[pl.BlockSpec((None, tt, PW), lambda b, n: (b, n, 0)),
                  pl.BlockSpec((None, 8, PW), lambda b, n: (b, jnp.maximum(n * (tt // 8) - 1, 0), 0))]
                 + [full(t) for t in params],
        out_specs=[pl.BlockSpec((None, tt, C), lambda b, n: (b, n, 0))] * 7,
        out_shape=[jax.ShapeDtypeStruct((B, S, C), F32)] * 7,
        compiler_params=_cparams(2),
        name="rwkv_prep",
    )(p3, p3, *params)
    return outs


def _pair_blockdiag(x):
    lane = lax.broadcasted_iota(I32, x.shape, 1)
    zero = jnp.zeros_like(x)
    return jnp.concatenate([jnp.where(lane < HEAD_DIM, x, zero), jnp.where(lane >= HEAD_DIM, x, zero)], axis=0)


def _rwkv_scan_kernel(r_ref, lw_ref, k_ref, v_ref, a_ref, b_ref, g_ref, rk_ref, lnw_ref, lnb_ref,
                      o_ref, s_ref):
    c = pl.program_id(1)

    @pl.when(c == 0)
    def _():
        s_ref[...] = jnp.zeros_like(s_ref)

    C = RWKV_CHUNK
    NB = r_ref.shape[0]
    NP = RWKV_HEADS // 2
    PW = 2 * HEAD_DIM
    row = lax.broadcasted_iota(I32, (C, C), 0)
    col = lax.broadcasted_iota(I32, (C, C), 1)
    tri = jnp.where(row >= col, 1.0, 0.0).astype(F32)
    rowp = lax.broadcasted_iota(I32, (C, PW), 0)
    colp = lax.broadcasted_iota(I32, (C, PW), 1)
    colp = jnp.where(colp >= HEAD_DIM, colp - HEAD_DIM, colp)
    lower_p = rowp >= colp
    strict_p = rowp > colp
    rows = lax.broadcasted_iota(I32, (PW, PW), 0)
    cols = lax.broadcasted_iota(I32, (PW, PW), 1)
    same_head = jnp.where(rows >= HEAD_DIM, 1, 0) == jnp.where(cols >= HEAD_DIM, 1, 0)
    first = lax.broadcasted_iota(I32, (C, PW), 1) < HEAD_DIM

    streams = [(bi, p) for bi in range(NB) for p in range(NP)]
    pre = []
    for bi in range(NB):
        lw = lw_ref[bi]
        cum = _dot_f32(tri, lw)
        cum_last = cum[C - 1:C, :]
        r = r_ref[bi]
        k = k_ref[bi]
        v = v_ref[bi]
        a = a_ref[bi]
        b = b_ref[bi]
        e_neg = jnp.exp(-cum)
        e_rem = jnp.exp(cum_last - cum)
        pre.append(dict(
            r_t=(r * jnp.exp(cum)).astype(BF16), a_t=(a * jnp.exp(cum - lw)).astype(BF16),
            b_t=(b * e_neg).astype(BF16), k_t=(k * e_neg).astype(BF16),
            b_d=(b * e_rem).astype(BF16), k_d=(k * e_rem).astype(BF16),
            v_b=v.astype(BF16), v=v, e_last=jnp.exp(cum_last), rkk=r * k * rk_ref[...], g=g_ref[bi]))

    def lanes(p):
        return slice(p * PW, (p + 1) * PW)

    ar = [jnp.concatenate([pre[bi]['a_t'][:, lanes(p)], pre[bi]['r_t'][:, lanes(p)]], axis=0) for bi, p in streams]
    s0 = [s_ref[bi, p] for bi, p in streams]
    m_b = [_dot_nt(ar[i], _pair_blockdiag(pre[bi]['b_t'][:, lanes(p)])) for i, (bi, p) in enumerate(streams)]
    m_k = [_dot_nt(ar[i], _pair_blockdiag(pre[bi]['k_t'][:, lanes(p)])) for i, (bi, p) in enumerate(streams)]
    ars = [_dot_nt(ar[i], s0[i]) for i in range(len(streams))]
    v_p = [pre[bi]['v_b'][:, lanes(p)] for bi, p in streams]
    v_bd = [_pair_blockdiag(vp) for vp in v_p]
    x = [ars[i][:C] + _dot(jnp.where(strict_p, m_k[i][:C], 0.0), v_bd[i]) for i in range(len(streams))]
    pw = [jnp.where(strict_p, m_b[i][:C], 0.0).astype(BF16) for i in range(len(streams))]
    x = [x[i] + _dot(pw[i], _pair_blockdiag(x[i].astype(BF16))) for i in range(len(streams))]
    for _ in range(5):
        pw = [_dot(pw[i], _pair_blockdiag(pw[i])).astype(BF16) for i in range(len(streams))]
        x = [x[i] + _dot(pw[i], _pair_blockdiag(x[i].astype(BF16))) for i in range(len(streams))]
    u_b = [xi.astype(BF16) for xi in x]
    y = [ars[i][C:]
         + _dot(jnp.concatenate([jnp.where(lower_p, m_b[i][C:], 0.0), jnp.where(lower_p, m_k[i][C:], 0.0)], axis=1),
                jnp.concatenate([_pair_blockdiag(u_b[i]), v_bd[i]], axis=0))
         for i in range(len(streams))]
    for i, (bi, p) in enumerate(streams):
        upd = _dot_tn(jnp.concatenate([u_b[i], v_p[i]], axis=0),
                      jnp.concatenate([pre[bi]['b_d'][:, lanes(p)], pre[bi]['k_d'][:, lanes(p)]], axis=0))
        s_ref[bi, p] = s0[i] * pre[bi]['e_last'][:, lanes(p)] + jnp.where(same_head, upd, 0.0)

    lnw = lnw_ref[...]
    lnb = lnb_ref[...]

    def head_sum(t):
        s1 = jnp.sum(jnp.where(first, t, 0.0), axis=-1, keepdims=True)
        s2 = jnp.sum(jnp.where(first, 0.0, t), axis=-1, keepdims=True)
        return jnp.where(first, s1, s2)

    for bi in range(NB):
        outs = []
        for p in range(NP):
            yi = y[bi * NP + p]
            mean = head_sum(yi) * (1.0 / HEAD_DIM)
            yc = yi - mean
            var = head_sum(yc * yc) * (1.0 / HEAD_DIM)
            yn = yc * lax.rsqrt(var + RWKV_LN_EPS) * lnw[:, lanes(p)] + lnb[:, lanes(p)]
            bonus = head_sum(pre[bi]['rkk'][:, lanes(p)]) * pre[bi]['v'][:, lanes(p)]
            outs.append((yn + bonus) * pre[bi]['g'][:, lanes(p)])
        o_ref[bi] = jnp.concatenate(outs, axis=1).astype(o_ref.dtype)


RWKV_BATCH_ROWS = 4


def _rwkv_scan(r, lw, k, v, a, b, g, r_k, lnx_w, lnx_b):
    B, S, W = r.shape
    C = RWKV_CHUNK
    nb = RWKV_BATCH_ROWS if B % RWKV_BATCH_ROWS == 0 else 1
    seq = pl.BlockSpec((nb, C, W), lambda bb, c: (bb, c, 0))
    par = pl.BlockSpec((1, W), lambda bb, c: (0, 0))
    out = pl.pallas_call(
        _rwkv_scan_kernel,
        grid=(B // nb, S // C),
        in_specs=[seq] * 7 + [par] * 3,
        out_specs=seq,
        out_shape=jax.ShapeDtypeStruct((B, S, W), BF16),
        scratch_shapes=[pltpu.VMEM((nb, RWKV_HEADS // 2, 2 * HEAD_DIM, 2 * HEAD_DIM), F32)],
        compiler_params=_cparams(2),
        name="rwkv_scan",
    )(r, lw, k, v, a, b, g, r_k.reshape(1, W), lnx_w.reshape(1, W), lnx_b.reshape(1, W))
    return out.reshape(B * S, W)


def _gla_kernel(q_ref, k_ref, v_ref, og_ref, gd_ref, gup_ref, gb_ref, on_ref, o_ref, s_ref):
    c = pl.program_id(1)

    @pl.when(c == 0)
    def _():
        s_ref[...] = jnp.zeros_like(s_ref)

    C = GLA_CHUNK
    NB = q_ref.shape[0]
    row = lax.broadcasted_iota(I32, (C, C), 0)
    col = lax.broadcasted_iota(I32, (C, C), 1)
    lower = row >= col
    tri = jnp.where(lower, 1.0, 0.0).astype(F32)
    onorm = on_ref[...]
    zs = [_dot(gd_ref[bi], gup_ref[...]) + gb_ref[...] for bi in range(NB)]
    cums = [_dot_f32(tri, -_softplus(-z) / GLA_GATE_NORM) for z in zs]
    qe, ke, kd, e_last, v = [], [], [], [], []
    for bi in range(NB):
        cum = cums[bi]
        cum_last = cum[C - 1:C, :]
        k = k_ref[bi]
        qe.append((q_ref[bi] * (GLA_DK ** -0.5) * jnp.exp(cum)).astype(BF16))
        ke.append((k * jnp.exp(-cum)).astype(BF16))
        kd.append((k * jnp.exp(cum_last - cum)).astype(BF16))
        e_last.append(jnp.exp(cum_last))
        v.append(v_ref[bi].astype(BF16))
    streams = [(bi, h) for bi in range(NB) for h in range(GLA_HEADS)]
    ks = lambda h: slice(h * GLA_DK, (h + 1) * GLA_DK)
    vs = lambda h: slice(h * GLA_DV, (h + 1) * GLA_DV)
    sts = [s_ref[bi, h] for bi, h in streams]
    atts = [jnp.where(lower, _dot_nt(qe[bi][:, ks(h)], ke[bi][:, ks(h)]), 0.0) for bi, h in streams]
    inters = [_dot_nt(qe[bi][:, ks(h)], sts[i]) for i, (bi, h) in enumerate(streams)]
    os_ = [inters[i] + _dot(atts[i], v[bi][:, vs(h)]) for i, (bi, h) in enumerate(streams)]
    for i, (bi, h) in enumerate(streams):
        s_ref[bi, h] = sts[i] * e_last[bi][:, ks(h)] + _dot_tn(v[bi][:, vs(h)], kd[bi][:, ks(h)])
    for bi in range(NB):
        og = og_ref[bi]
        outs = []
        for h in range(GLA_HEADS):
            gate = og[:, vs(h)]
            outs.append(_rms(os_[bi * GLA_HEADS + h], onorm) * (gate * _sigmoid(gate)))
        o_ref[bi] = jnp.concatenate(outs, axis=1).astype(o_ref.dtype)


GLA_BATCH_ROWS = 4


def _gla(qkvo, gd, gate_up_pad, gate_b, onorm, B, S):
    C = GLA_CHUNK
    KW = GLA_HEADS * GLA_DK
    VW = GLA_HEADS * GLA_DV
    x3 = qkvo.reshape(B, S, qkvo.shape[-1])
    gd3 = gd.reshape(B, S, LANES)
    nb = GLA_BATCH_ROWS if B % GLA_BATCH_ROWS == 0 else 1
    out = pl.pallas_call(
        _gla_kernel,
        grid=(B // nb, S // C),
        in_specs=[pl.BlockSpec((nb, C, KW), lambda b, c: (b, c, 0)),
                  pl.BlockSpec((nb, C, KW), lambda b, c: (b, c, 1)),
                  pl.BlockSpec((nb, C, VW), lambda b, c: (b, c, 1)),
                  pl.BlockSpec((nb, C, VW), lambda b, c: (b, c, 2)),
                  pl.BlockSpec((nb, C, LANES), lambda b, c: (b, c, 0)),
                  pl.BlockSpec((LANES, KW), lambda b, c: (0, 0)),
                  pl.BlockSpec((1, KW), lambda b, c: (0, 0)),
                  pl.BlockSpec((1, GLA_DV), lambda b, c: (0, 0))],
        out_specs=pl.BlockSpec((nb, C, VW), lambda b, c: (b, c, 0)),
        out_shape=jax.ShapeDtypeStruct((B, S, VW), BF16),
        scratch_shapes=[pltpu.VMEM((nb, GLA_HEADS, GLA_DV, GLA_DK), F32)],
        compiler_params=_cparams(2),
        name="gla",
    )(x3, x3, x3, x3, gd3, gate_up_pad, gate_b.reshape(1, KW), onorm.reshape(1, GLA_DV))
    return out.reshape(B * S, VW)


def _xattn_kernel(x_ref, g_ref, wq_ref, mk_ref, mv_ref, wo_ref, o_ref):
    x = x_ref[...]
    h = _rms(x, g_ref[...])
    q = _dot(h, wq_ref[...]).astype(BF16)
    mk = mk_ref[...]
    mv = mv_ref[...]
    sls = [slice(hd * XA_HEAD_DIM, (hd + 1) * XA_HEAD_DIM) for hd in range(XA_HEADS)]
    scores = [_dot_nt(q[:, sl], mk[:, sl]) for sl in sls]
    probs = []
    for s in scores:
        s = s * (XA_HEAD_DIM ** -0.5)
        p = jnp.exp(s - jnp.max(s, axis=-1, keepdims=True))
        probs.append((p / jnp.sum(p, axis=-1, keepdims=True)).astype(BF16))
    o = jnp.concatenate([_dot(p, mv[:, sl]) for p, sl in zip(probs, sls)], axis=1)
    o_ref[...] = x + _dot(o, wo_ref[...])


def _xattn(x, g, wq, mk, mv, wo, B, S, tq=256):
    D = x.shape[-1]
    M = mk.shape[0] // B
    XW = mk.shape[-1]
    x3 = x.reshape(B, S, D)
    out = pl.pallas_call(
        _xattn_kernel,
        grid=(B, S // tq),
        in_specs=[pl.BlockSpec((None, tq, D), lambda b, n: (b, n, 0)),
                  pl.BlockSpec((1, D), lambda b, n: (0, 0)),
                  pl.BlockSpec((D, XW), lambda b, n: (0, 0)),
                  pl.BlockSpec((None, M, XW), lambda b, n: (b, 0, 0)),
                  pl.BlockSpec((None, M, XW), lambda b, n: (b, 0, 0)),
                  pl.BlockSpec((XW, D), lambda b, n: (0, 0))],
        out_specs=pl.BlockSpec((None, tq, D), lambda b, n: (b, n, 0)),
        out_shape=jax.ShapeDtypeStruct((B, S, D), F32),
        compiler_params=_cparams(2),
        name="xattn",
    )(x3, g.reshape(1, D), wq, mk.reshape(B, M, XW), mv.reshape(B, M, XW), wo)
    return out.reshape(B * S, D)


def _router_kernel(x_ref, g_ref, w_ref, b_ref, info_ref, cnt_ref, carry_ref):
    i = pl.program_id(0)

    @pl.when(i == 0)
    def _():
        carry_ref[...] = jnp.zeros_like(carry_ref)

    h = _rms(x_ref[...], g_ref[...])
    logits = _dot(h, w_ref[...]) + b_ref[...]
    tm = logits.shape[0]
    lane = lax.broadcasted_iota(I32, logits.shape, 1)
    big = jnp.int32(LANES)
    neg = -jnp.inf
    gl = jnp.where(lane < MOE_GROUPS, logits, neg)
    gmax = jnp.max(gl, axis=-1, keepdims=True)
    g_top = jnp.min(jnp.where(gl == gmax, lane, big), axis=-1, keepdims=True)
    p_group = 1.0 / jnp.sum(jnp.exp(gl - gmax), axis=-1, keepdims=True)
    lo = MOE_GROUPS + MOE_EXPERTS_PER_GROUP * g_top
    in_group = jnp.where(lane >= lo, jnp.where(lane < lo + MOE_EXPERTS_PER_GROUP, 1, 0), 0) > 0
    el = jnp.where(in_group, logits, neg)
    emax = jnp.max(el, axis=-1, keepdims=True)
    ee = jnp.exp(el - emax)
    prob = ee / jnp.sum(ee, axis=-1, keepdims=True)
    prob = jnp.where(in_group, prob, -1.0)
    p1 = jnp.max(prob, axis=-1, keepdims=True)
    i1 = jnp.min(jnp.where(prob == p1, lane, big), axis=-1, keepdims=True)
    rest = jnp.where(lane == i1, -1.0, prob)
    p2 = jnp.max(rest, axis=-1, keepdims=True)
    i2 = jnp.min(jnp.where(rest == p2, lane, big), axis=-1, keepdims=True)
    tot = p1 + p2
    e1 = i1 - MOE_GROUPS
    e2 = i2 - MOE_GROUPS
    oh1 = jnp.where(lane == e1, 1.0, 0.0)
    oh2 = jnp.where(lane == e2, 1.0, 0.0)
    row = lax.broadcasted_iota(I32, (tm, tm), 0)
    col = lax.broadcasted_iota(I32, (tm, tm), 1)
    before = jnp.where(row > col, 1.0, 0.0)
    pre = _dot(before, jnp.concatenate([oh1, oh2], axis=1))
    tot1 = jnp.sum(oh1, axis=0, keepdims=True)
    tot2 = jnp.sum(oh2, axis=0, keepdims=True)
    carry = carry_ref[...]
    r1 = jnp.sum(oh1 * (carry + pre[:, :LANES]), axis=-1, keepdims=True)
    r2 = jnp.sum(oh2 * (carry + tot1 + pre[:, LANES:]), axis=-1, keepdims=True)
    carry = carry + tot1 + tot2
    carry_ref[...] = carry
    cnt_ref[...] = carry
    g1 = p_group * p1 / tot
    g2 = p_group * p2 / tot
    vals = [e1.astype(F32), e2.astype(F32), r1, r2, g1, g2]
    info = jnp.zeros_like(logits)
    for j, val in enumerate(vals):
        info = jnp.where(lane == j, val, info)
    info_ref[...] = info


def _router(x, g, w_router, b_router, tm=256):
    T, D = x.shape
    return pl.pallas_call(
        _router_kernel,
        grid=(T // tm,),
        in_specs=[pl.BlockSpec((tm, D), lambda i: (i, 0)),
                  pl.BlockSpec((1, D), lambda i: (0, 0)),
                  pl.BlockSpec((D, LANES), lambda i: (0, 0)),
                  pl.BlockSpec((1, LANES), lambda i: (0, 0))],
        out_specs=[pl.BlockSpec((tm, LANES), lambda i: (i, 0)),
                   pl.BlockSpec((1, LANES), lambda i: (0, 0))],
        out_shape=[jax.ShapeDtypeStruct((T, LANES), F32), jax.ShapeDtypeStruct((1, LANES), F32)],
        scratch_shapes=[pltpu.VMEM((1, LANES), F32)],
        compiler_params=_cparams(1),
        name="router",
    )(x, g.reshape(1, D), w_router, b_router)


def _row_bytes_wait(hbm, buf, sem):
    pltpu.make_async_copy(buf, hbm.at[pl.ds(0, buf.shape[0]), :], sem).wait()


def _to_row_tiles(ref, val):
    n = val.shape[0]
    for c in range(SUBLANES):
        ref[pl.ds(c, n, stride=SUBLANES), :] = val[:, c * LANES:(c + 1) * LANES]


def _from_row_tiles(ref):
    n = ref.shape[0] // SUBLANES
    return jnp.concatenate([ref[pl.ds(c, n, stride=SUBLANES), :] for c in range(SUBLANES)], axis=1)


def _moe_dispatch_kernel(pends_ref, cnt_ref, dest_ref, x_ref, g_ref, hs_hbm, hbuf, zbuf, sems, zsem, *, td):
    i = pl.program_id(0)
    nt = pl.num_programs(0)
    slot = lax.rem(i, 2)

    @pl.when(i == 0)
    def _():
        zbuf[...] = jnp.zeros_like(zbuf)
        for e in range(MOE_EXPERTS):
            @pl.when(cnt_ref[e] > 0)
            def _():
                start = pl.multiple_of((pends_ref[e] - MOE_BLOCK) * SUBLANES, MOE_BLOCK)
                pltpu.make_async_copy(zbuf, hs_hbm.at[pl.ds(start, MOE_BLOCK * SUBLANES), :], zsem).start()
        for e in range(MOE_EXPERTS):
            @pl.when(cnt_ref[e] > 0)
            def _():
                pltpu.make_async_copy(zbuf, hs_hbm.at[pl.ds(0, MOE_BLOCK * SUBLANES), :], zsem).wait()

        first_unused = pends_ref[MOE_EXPERTS - 1] // MOE_BLOCK
        n_blocks = hs_hbm.shape[0] // (MOE_BLOCK * SUBLANES)

        def zero_start(blk, carry):
            start = pl.multiple_of(blk * (MOE_BLOCK * SUBLANES), MOE_BLOCK)
            pltpu.make_async_copy(zbuf, hs_hbm.at[pl.ds(start, MOE_BLOCK * SUBLANES), :], zsem).start()
            return carry

        def zero_wait(blk, carry):
            pltpu.make_async_copy(zbuf, hs_hbm.at[pl.ds(0, MOE_BLOCK * SUBLANES), :], zsem).wait()
            return carry

        lax.fori_loop(first_unused, n_blocks, zero_start, 0)
        lax.fori_loop(first_unused, n_blocks, zero_wait, 0)

    hb = hbuf.at[slot]
    _to_row_tiles(hb, _rms(x_ref[...], g_ref[...]))
    for j in range(td):
        for c in range(2):
            row = pl.multiple_of(dest_ref[0, c * td + j] * SUBLANES, SUBLANES)
            pltpu.make_async_copy(hb.at[pl.ds(j * SUBLANES, SUBLANES), :],
                                  hs_hbm.at[pl.ds(row, SUBLANES), :],
                                  sems.at[slot]).start(priority=c)

    @pl.when(i > 0)
    def _():
        other = hbuf.at[1 - slot]
        _row_bytes_wait(hs_hbm, other, sems.at[1 - slot])
        _row_bytes_wait(hs_hbm, other, sems.at[1 - slot])

    @pl.when(i == nt - 1)
    def _():
        _row_bytes_wait(hs_hbm, hb, sems.at[slot])
        _row_bytes_wait(hs_hbm, hb, sems.at[slot])


def _moe_dispatch(x, g, pends, counts, dest3, P, td):
    T, D = x.shape
    assert D == ROW_TILE
    grid_spec = pltpu.PrefetchScalarGridSpec(
        num_scalar_prefetch=2,
        grid=(T // td,),
        in_specs=[pl.BlockSpec((None, 1, 2 * td), lambda i, pe, cn: (i, 0, 0), memory_space=pltpu.SMEM),
                  pl.BlockSpec((td, D), lambda i, pe, cn: (i, 0)),
                  pl.BlockSpec((1, D), lambda i, pe, cn: (0, 0))],
        out_specs=pl.BlockSpec(memory_space=pl.ANY),
        scratch_shapes=[pltpu.VMEM((2, td * SUBLANES, LANES), F32),
                        pltpu.VMEM((MOE_BLOCK * SUBLANES, LANES), F32),
                        pltpu.SemaphoreType.DMA((2,)),
                        pltpu.SemaphoreType.DMA(())],
    )
    return pl.pallas_call(
        functools.partial(_moe_dispatch_kernel, td=td),
        grid_spec=grid_spec,
        out_shape=jax.ShapeDtypeStruct((P * SUBLANES, LANES), F32),
        compiler_params=_cparams(1),
        name="moe_dispatch",
    )(pends, counts, dest3, x, g.reshape(1, D))


def _moe_expert_kernel(be_ref, nu_ref, hs_ref, w1_ref, w3_ref, w2_ref, o_ref, w1b, w3b, w2b):
    i = pl.program_id(0)
    used = i < nu_ref[0]
    changed = jnp.logical_or(i == 0, be_ref[i] != be_ref[jnp.maximum(i - 1, 0)])

    @pl.when(jnp.logical_and(used, changed))
    def _():
        w1b[...] = w1_ref[...].astype(BF16)
        w3b[...] = w3_ref[...].astype(BF16)
        w2b[...] = w2_ref[...].astype(BF16)

    @pl.when(used)
    def _():
        xe = _from_row_tiles(hs_ref).astype(BF16)
        h1 = jnp.dot(xe, w1b[...], preferred_element_type=F32)
        h3 = jnp.dot(xe, w3b[...], preferred_element_type=F32)
        act = (h1 * _sigmoid(h1) * h3).astype(BF16)
        _to_row_tiles(o_ref, jnp.dot(act, w2b[...], preferred_element_type=F32))

    @pl.when(jnp.logical_not(used))
    def _():
        o_ref[...] = jnp.zeros_like(o_ref)


def _moe_experts(hs, block_e, n_used, w1, w3, w2, layer):
    P = hs.shape[0] // SUBLANES
    D = ROW_TILE
    FF = w1.shape[-1]
    NB = P // MOE_BLOCK
    last = lambda i, nu: jnp.minimum(i, nu[0] - 1)
    grid_spec = pltpu.PrefetchScalarGridSpec(
        num_scalar_prefetch=2,
        grid=(NB,),
        in_specs=[pl.BlockSpec((MOE_BLOCK * SUBLANES, LANES), lambda i, be, nu: (last(i, nu), 0)),
                  pl.BlockSpec((None, None, D, FF), lambda i, be, nu: (layer, be[last(i, nu)], 0, 0)),
                  pl.BlockSpec((None, None, D, FF), lambda i, be, nu: (layer, be[last(i, nu)], 0, 0)),
                  pl.BlockSpec((None, None, FF, D), lambda i, be, nu: (layer, be[last(i, nu)], 0, 0))],
        out_specs=pl.BlockSpec((MOE_BLOCK * SUBLANES, LANES), lambda i, be, nu: (i, 0)),
        scratch_shapes=[pltpu.VMEM((D, FF), BF16),
                        pltpu.VMEM((D, FF), BF16),
                        pltpu.VMEM((FF, D), BF16)],
    )
    return pl.pallas_call(
        _moe_expert_kernel,
        grid_spec=grid_spec,
        out_shape=jax.ShapeDtypeStruct((P * SUBLANES, LANES), F32),
        compiler_params=_cparams(1),
        name="moe_experts",
    )(block_e, n_used, hs, w1, w3, w2)


def _gather_rows(src_hbm, idx_ref, dst_ref, sem, n_rows):
    for r in range(n_rows):
        row = pl.multiple_of(idx_ref[0, r] * SUBLANES, SUBLANES)
        pltpu.make_async_copy(src_hbm.at[pl.ds(row, SUBLANES), :],
                              dst_ref.at[pl.ds(r * SUBLANES, SUBLANES), :], sem).start(priority=r % 2)


def _moe_combine_kernel(pos_ref, posn_ref, x_ref, info_ref, yb_hbm, gf_ref, o_ref, ybuf, sems, *, tc, final_norm):
    i = pl.program_id(0)
    nb = pl.num_programs(0)
    slot = lax.rem(i, 2)

    @pl.when(i == 0)
    def _():
        def issue(r, carry):
            src = pl.multiple_of(pos_ref[0, r] * SUBLANES, SUBLANES)
            dst = pl.multiple_of(r * SUBLANES, SUBLANES)
            pltpu.make_async_copy(yb_hbm.at[pl.ds(src, SUBLANES), :],
                                  ybuf.at[0, pl.ds(dst, SUBLANES), :], sems.at[0]).start()
            return carry
        lax.fori_loop(0, 2 * tc, issue, 0)

    @pl.when(i + 1 < nb)
    def _():
        _gather_rows(yb_hbm, posn_ref, ybuf.at[1 - slot], sems.at[1 - slot], 2 * tc)

    pltpu.make_async_copy(yb_hbm.at[pl.ds(0, 2 * tc * SUBLANES), :], ybuf.at[slot], sems.at[slot]).wait()
    info = info_ref[...]
    yb = ybuf.at[slot]
    y0 = _from_row_tiles(yb.at[pl.ds(0, tc * SUBLANES), :])
    y1 = _from_row_tiles(yb.at[pl.ds(tc * SUBLANES, tc * SUBLANES), :])
    out = x_ref[...] + (y0 * info[:, 4:5] + y1 * info[:, 5:6])
    if final_norm:
        out = _rms(out, gf_ref[...])
    o_ref[...] = out


def _moe_combine(x, info, dest3, yb, g_final, final_norm, tc):
    T, D = x.shape
    NT = T // tc
    return pl.pallas_call(
        functools.partial(_moe_combine_kernel, tc=tc, final_norm=final_norm),
        grid=(NT,),
        in_specs=[pl.BlockSpec((None, 1, 2 * tc), lambda i: (i, 0, 0), memory_space=pltpu.SMEM),
                  pl.BlockSpec((None, 1, 2 * tc), lambda i: (jnp.minimum(i + 1, NT - 1), 0, 0),
                               memory_space=pltpu.SMEM),
                  pl.BlockSpec((tc, D), lambda i: (i, 0)),
                  pl.BlockSpec((tc, LANES), lambda i: (i, 0)),
                  pl.BlockSpec(memory_space=pl.ANY),
                  pl.BlockSpec((1, D), lambda i: (0, 0))],
        out_specs=pl.BlockSpec((tc, D), lambda i: (i, 0)),
        out_shape=jax.ShapeDtypeStruct((T, D), F32),
        scratch_shapes=[pltpu.VMEM((2, 2 * tc * SUBLANES, LANES), F32), pltpu.SemaphoreType.DMA((2,))],
        compiler_params=_cparams(1),
        name="moe_combine",
    )(dest3, dest3, x, info, yb, g_final.reshape(1, D))


MOE_TILE = 128


def _moe_layer(x, g, w_group, b_group, w_expert, b_expert, w1, w3, w2, layer, g_final, final_norm):
    T, D = x.shape
    n_log = MOE_GROUPS + MOE_EXPERTS
    w_router = jnp.zeros((D, LANES), F32).at[:, :MOE_GROUPS].set(w_group).at[:, MOE_GROUPS:n_log].set(w_expert)
    b_router = jnp.zeros((1, LANES), F32).at[0, :MOE_GROUPS].set(b_group).at[0, MOE_GROUPS:n_log].set(b_expert)
    info, cnt = _router(x, g, w_router.astype(BF16), b_router)
    P = 2 * T + MOE_EXPERTS * MOE_BLOCK
    NB = P // MOE_BLOCK
    counts = cnt[0, :MOE_EXPERTS].astype(I32)
    padded = (counts + MOE_BLOCK - 1) // MOE_BLOCK * MOE_BLOCK
    pends = jnp.cumsum(padded).astype(I32)
    pstarts = pends - padded
    block_start = jnp.arange(NB, dtype=I32) * MOE_BLOCK
    block_e = jnp.minimum(jnp.sum((pends[None, :] <= block_start[:, None]).astype(I32), axis=1),
                          MOE_EXPERTS - 1).astype(I32)
    n_used = (pends[-1:] // MOE_BLOCK).astype(I32)
    eid = info[:, 0:2].astype(I32)
    dest = pstarts[eid] + info[:, 2:4].astype(I32)
    NT = T // MOE_TILE
    dest3 = dest.reshape(NT, MOE_TILE, 2).transpose(0, 2, 1).reshape(NT, 1, 2 * MOE_TILE)
    hs = _moe_dispatch(x, g, pends, counts, dest3, P, MOE_TILE)
    yb = _moe_experts(hs, block_e, n_used, w1, w3, w2, layer)
    return _moe_combine(x, info, dest3, yb, g_final, final_norm, MOE_TILE)


def kernel(x, mem, norm_mix, norm_xattn, norm_moe, norm_final, ev_w_in, ev_sinks, ev_mu, ev_w0, ev_w2, ev_a0, ev_a2, ev_g2, ev_k_k, ev_k_a, ev_r_k, ev_lnx_w, ev_lnx_b, ev_w_out, od_w_in, od_gate_up, od_gate_b, od_onorm, od_w_out, mem_norm, mem_wk, mem_wv, xa_wq, xa_wo, moe_w_group, moe_b_group, moe_w_expert, moe_b_expert, moe_w1, moe_w3, moe_w2):
    B, S, D = x.shape
    M = mem.shape[1]
    T = B * S
    depth = norm_mix.shape[0]
    xf = x.reshape(T, D)

    XW = XA_HEADS * XA_HEAD_DIM
    w_kv = jnp.concatenate([mem_wk, mem_wv], axis=1).astype(BF16)
    mk, mv = _norm_matmul(mem.reshape(B * M, D), mem_norm, w_kv, (XW, XW), (BF16, BF16))

    for layer in range(depth):
        i = layer // 2
        if layer % 2 == 0:
            swa_cols = SWA_Q_HEADS * HEAD_DIM + 2 * (SWA_Q_HEADS // SWA_GROUP) * HEAD_DIM
            rw_cols = ev_w_in.shape[-1] - swa_cols
            qkv, p_rw = _norm_matmul(xf, norm_mix[layer], ev_w_in[i].astype(BF16),
                                     (swa_cols, rw_cols), (F32, F32))
            o_a = _swa(qkv, ev_sinks[i], B, S)
            r, lw, k, v, a, b, g = _rwkv_prep(p_rw, ev_mu[i], ev_w0[i], ev_w2[i], ev_a0[i], ev_a2[i],
                                              ev_g2[i], ev_k_k[i], ev_k_a[i], B, S)
            o_b = _rwkv_scan(r, lw, k, v, a, b, g, ev_r_k[i].reshape(-1), ev_lnx_w[i], ev_lnx_b[i])
            w_out = ev_w_out[i].astype(BF16)
            qw = o_a.shape[-1]
            xf = _proj_residual(xf, [o_a, o_b], [w_out[:qw], w_out[qw:]])
        else:
            KW = GLA_HEADS * GLA_DK
            VW = GLA_HEADS * GLA_DV
            R = od_gate_up.shape[1]
            w = od_w_in[i]
            w_re = jnp.concatenate([w[:, :2 * KW + VW], w[:, 2 * KW + VW + R:],
                                    w[:, 2 * KW + VW:2 * KW + VW + R],
                                    jnp.zeros((D, LANES - R), F32)], axis=1).astype(BF16)
            qkvo, gd = _norm_matmul(xf, norm_mix[layer], w_re, (2 * KW + 2 * VW, LANES), (F32, F32))
            gup = jnp.zeros((LANES, KW), F32).at[:R].set(od_gate_up[i]).astype(BF16)
            o = _gla(qkvo, gd, gup, od_gate_b[i], od_onorm[i], B, S)
            xf = _proj_residual(xf, [o], [od_w_out[i].astype(BF16)])
        xf = _xattn(xf, norm_xattn[layer], xa_wq[layer].astype(BF16), mk, mv, xa_wo[layer].astype(BF16), B, S)
        xf = _moe_layer(xf, norm_moe[layer], moe_w_group[layer], moe_b_group[layer], moe_w_expert[layer],
                        moe_b_expert[layer], moe_w1, moe_w3, moe_w2, layer,
                        norm_final, layer == depth - 1)
    return xf.reshape(B, S, D)
```

```python
import functools

import jax
import jax.numpy as jnp
from jax import lax
from jax.experimental import pallas as pl
from jax.experimental.pallas import tpu as pltpu

F32 = jnp.float32
BF16 = jnp.bfloat16
I32 = jnp.int32

EPS = 1e-6
HEAD_DIM = 64
SWA_WINDOW = 128
SWA_Q_HEADS = 8
SWA_GROUP = 4
RWKV_HEADS = 8
RWKV_WIDTH = 512
RWKV_LN_EPS = 64e-5
RWKV_CHUNK = 64
GLA_HEADS = 4
GLA_DK = 128
GLA_DV = 256
GLA_CHUNK = 64
GLA_GATE_NORM = 16.0
XA_HEADS = 4
XA_HEAD_DIM = 128
MOE_GROUPS = 4
MOE_EXPERTS_PER_GROUP = 8
MOE_EXPERTS = 32
MOE_BLOCK = 512
LANES = 128
SUBLANES = 8
ROW_TILE = SUBLANES * LANES

VMEM_LIMIT_BYTES = 48 * 1024 * 1024


def _cparams(n_axes):
    return pltpu.CompilerParams(dimension_semantics=("arbitrary",) * n_axes,
                                vmem_limit_bytes=VMEM_LIMIT_BYTES)


def _dot(a, b):
    return jnp.dot(a.astype(BF16), b.astype(BF16), preferred_element_type=F32)


def _dot_nt(a, b):
    return lax.dot_general(a.astype(BF16), b.astype(BF16), (((1,), (1,)), ((), ())),
                           preferred_element_type=F32)


def _dot_tn(a, b):
    return lax.dot_general(a.astype(BF16), b.astype(BF16), (((0,), (0,)), ((), ())),
                           preferred_element_type=F32)


def _dot_f32(a, b):
    return jnp.dot(a, b, preferred_element_type=F32, precision=lax.Precision.HIGHEST)


def _rms(x, g):
    ms = jnp.mean(x * x, axis=-1, keepdims=True)
    return x * lax.rsqrt(ms + EPS) * g


def _sigmoid(x):
    return 1.0 / (1.0 + jnp.exp(-x))


def _softplus(x):
    return jnp.maximum(x, 0.0) + jnp.log(1.0 + jnp.exp(-jnp.abs(x)))


def _norm_matmul_kernel(x_ref, g_ref, w_ref, *o_refs, splits):
    h = _rms(x_ref[...], g_ref[...]).astype(BF16)
    off = 0
    for o_ref, n in zip(o_refs, splits):
        o_ref[...] = jnp.dot(h, w_ref[:, off:off + n], preferred_element_type=F32).astype(o_ref.dtype)
        off += n


def _norm_matmul(x, g, w, splits, out_dtypes, tm=256):
    T, D = x.shape
    N = w.shape[1]
    assert sum(splits) == N and T % tm == 0
    return pl.pallas_call(
        functools.partial(_norm_matmul_kernel, splits=tuple(splits)),
        grid=(T // tm,),
        in_specs=[pl.BlockSpec((tm, D), lambda i: (i, 0)),
                  pl.BlockSpec((1, D), lambda i: (0, 0)),
                  pl.BlockSpec((D, N), lambda i: (0, 0))],
        out_specs=[pl.BlockSpec((tm, n), lambda i: (i, 0)) for n in splits],
        out_shape=[jax.ShapeDtypeStruct((T, n), dt) for n, dt in zip(splits, out_dtypes)],
        compiler_params=_cparams(1),
        name="norm_matmul",
    )(x, g.reshape(1, D), w)


def _swa_kernel(sinks_ref, q_ref, kp_ref, kc_ref, vp_ref, vc_ref, o_ref):
    n = pl.program_id(1)
    W = SWA_WINDOW
    q = q_ref[...]
    k = jnp.concatenate([kp_ref[...], kc_ref[...]], axis=0)
    v = jnp.concatenate([vp_ref[...], vc_ref[...]], axis=0)
    qpos = lax.broadcasted_iota(I32, (W, 2 * W), 0) + W
    kpos = lax.broadcasted_iota(I32, (W, 2 * W), 1)
    rel = qpos - kpos
    in_window = jnp.where(rel >= 0, jnp.where(rel < W, 1, 0), 0)
    has_prev = jnp.where(n > 0, 1, 0)
    valid = (in_window * jnp.where(kpos >= W, 1, has_prev)) > 0
    n_groups = SWA_Q_HEADS // SWA_GROUP
    qb = q.astype(BF16)
    kb = k.astype(BF16)
    vb = v.astype(BF16)
    scores = []
    for g in range(n_groups):
        qg = jnp.concatenate([qb[:, h * HEAD_DIM:(h + 1) * HEAD_DIM]
                              for h in range(g * SWA_GROUP, (g + 1) * SWA_GROUP)], axis=0)
        scores.append(_dot_nt(qg, kb[:, g * HEAD_DIM:(g + 1) * HEAD_DIM]))
    probs = []
    for g in range(n_groups):
        pieces = []
        for j in range(SWA_GROUP):
            s = jnp.where(valid, scores[g][j * W:(j + 1) * W] * (HEAD_DIM ** -0.5), -jnp.inf)
            sink = sinks_ref[g * SWA_GROUP + j]
            m = jnp.maximum(jnp.max(s, axis=-1, keepdims=True), sink)
            p = jnp.exp(s - m)
            den = jnp.sum(p, axis=-1, keepdims=True) + jnp.exp(sink - m)
            pieces.append((p / den).astype(BF16))
        probs.append(jnp.concatenate(pieces, axis=0))
    outs = []
    for g in range(n_groups):
        og = _dot(probs[g], vb[:, g * HEAD_DIM:(g + 1) * HEAD_DIM])
        outs += [og[j * W:(j + 1) * W] for j in range(SWA_GROUP)]
    o_ref[...] = jnp.concatenate(outs, axis=1).astype(o_ref.dtype)


def _swa(qkv, sinks, B, S):
    W = SWA_WINDOW
    qkv3 = qkv.reshape(B, S, qkv.shape[-1])
    qw = SWA_Q_HEADS * HEAD_DIM
    kw = qw // SWA_GROUP
    kcol = qw // kw
    out = pl.pallas_call(
        _swa_kernel,
        grid=(B, S // W),
        in_specs=[pl.BlockSpec(memory_space=pltpu.SMEM),
                  pl.BlockSpec((None, W, qw), lambda b, n: (b, n, 0)),
                  pl.BlockSpec((None, W, kw), lambda b, n: (b, jnp.maximum(n - 1, 0), kcol)),
                  pl.BlockSpec((None, W, kw), lambda b, n: (b, n, kcol)),
                  pl.BlockSpec((None, W, kw), lambda b, n: (b, jnp.maximum(n - 1, 0), kcol + 1)),
                  pl.BlockSpec((None, W, kw), lambda b, n: (b, n, kcol + 1))],
        out_specs=pl.BlockSpec((None, W, qw), lambda b, n: (b, n, 0)),
        out_shape=jax.ShapeDtypeStruct((B, S, qw), BF16),
        compiler_params=_cparams(2),
        name="swa",
    )(sinks, qkv3, qkv3, qkv3, qkv3, qkv3)
    return out.reshape(B * S, qw)


def _rwkv_prep_kernel(p_ref, pprev_ref, mu_ref, w0_ref, w2_ref, a0_ref, a2_ref, g2_ref, kk_ref, ka_ref,
                      r_out, lw_out, k_out, v_out, a_out, b_out, g_out):
    n = pl.program_id(1)
    C = RWKV_WIDTH
    p = p_ref[...]
    last = jnp.where(n > 0, pprev_ref[7:8, :], 0.0)
    row = lax.broadcasted_iota(I32, p.shape, 0)
    p_prev = jnp.where(row == 0, last, pltpu.roll(p, 1, axis=0))
    p = p + (p_prev - p) * mu_ref[...]
    r = p[:, :C]
    k = p[:, C:2 * C]
    v = p[:, 2 * C:3 * C]
    xw = p[:, 3 * C:3 * C + 64]
    xa = p[:, 3 * C + 64:3 * C + 128]
    xg = p[:, 3 * C + 128:]
    w = -_softplus(-(w0_ref[...] + _dot(jnp.tanh(xw), w2_ref[...]))) - 0.5
    lw = -jnp.exp(w)
    a = _sigmoid(a0_ref[...] + _dot(xa, a2_ref[...]))
    g = _dot(_sigmoid(xg), g2_ref[...])
    kk = k * kk_ref[...]
    pieces = []
    for h in range(RWKV_HEADS):
        kh = kk[:, h * HEAD_DIM:(h + 1) * HEAD_DIM]
        nrm = jnp.sqrt(jnp.sum(kh * kh, axis=-1, keepdims=True))
        pieces.append(kh / jnp.maximum(nrm, 1e-12))
    kk = jnp.concatenate(pieces, axis=1)
    r_out[...] = r
    lw_out[...] = lw
    k_out[...] = k * (1.0 + (a - 1.0) * ka_ref[...])
    v_out[...] = v
    a_out[...] = -kk
    b_out[...] = kk * a
    g_out[...] = g


def _rwkv_prep(p, mu, w0, w2, a0, a2, g2, k_k, k_a, B, S, tt=256):
    C = RWKV_WIDTH
    PW = p.shape[-1]
    p3 = p.reshape(B, S, PW)
    row = lambda t: t.reshape(1, -1)
    full = lambda arr: pl.BlockSpec(arr.shape, lambda b, n: (0,) * arr.ndim)
    params = [row(mu), row(w0), w2.astype(BF16), row(a0), a2.astype(BF16), g2.astype(BF16), row(k_k), row(k_a)]
    outs = pl.pallas_call(
        _rwkv_prep_kernel,
        grid=(B, S // tt),
        in_specs=[pl.BlockSpec((None, tt, PW), lambda b, n: (b, n, 0)),
                  pl.BlockSpec((None, 8, PW), lambda b, n: (b, jnp.maximum(n * (tt // 8) - 1, 0), 0))]
                 + [full(t) for t in params],
        out_specs=[pl.BlockSpec((None, tt, C), lambda b, n: (b, n, 0))] * 7,
        out_shape=[jax.ShapeDtypeStruct((B, S, C), F32)] * 7,
        compiler_params=_cparams(2),
        name="rwkv_prep",
    )(p3, p3, *params)
    return outs


def _pair_blockdiag(x):
    lane = lax.broadcasted_iota(I32, x.shape, 1)
    zero = jnp.zeros_like(x)
    return jnp.concatenate([jnp.where(lane < HEAD_DIM, x, zero), jnp.where(lane >= HEAD_DIM, x, zero)], axis=0)


def _rwkv_scan_kernel(r_ref, lw_ref, k_ref, v_ref, a_ref, b_ref, g_ref, rk_ref, lnw_ref, lnb_ref,
                      o_ref, s_ref):
    c = pl.program_id(1)

    @pl.when(c == 0)
    def _():
        s_ref[...] = jnp.zeros_like(s_ref)

    C = RWKV_CHUNK
    NB = r_ref.shape[0]
    NP = RWKV_HEADS // 2
    PW = 2 * HEAD_DIM
    row = lax.broadcasted_iota(I32, (C, C), 0)
    col = lax.broadcasted_iota(I32, (C, C), 1)
    tri = jnp.where(row >= col, 1.0, 0.0).astype(F32)
    rowp = lax.broadcasted_iota(I32, (C, PW), 0)
    colp = lax.broadcasted_iota(I32, (C, PW), 1)
    colp = jnp.where(colp >= HEAD_DIM, colp - HEAD_DIM, colp)
    lower_p = rowp >= colp
    strict_p = rowp > colp
    rows = lax.broadcasted_iota(I32, (PW, PW), 0)
    cols = lax.broadcasted_iota(I32, (PW, PW), 1)
    same_head = jnp.where(rows >= HEAD_DIM, 1, 0) == jnp.where(cols >= HEAD_DIM, 1, 0)
    first = lax.broadcasted_iota(I32, (C, PW), 1) < HEAD_DIM

    streams = [(bi, p) for bi in range(NB) for p in range(NP)]
    pre = []
    for bi in range(NB):
        lw = lw_ref[bi]
        cum = _dot_f32(tri, lw)
        cum_last = cum[C - 1:C, :]
        r = r_ref[bi]
        k = k_ref[bi]
        v = v_ref[bi]
        a = a_ref[bi]
        b = b_ref[bi]
        e_neg = jnp.exp(-cum)
        e_rem = jnp.exp(cum_last - cum)
        pre.append(dict(
            r_t=(r * jnp.exp(cum)).astype(BF16), a_t=(a * jnp.exp(cum - lw)).astype(BF16),
            b_t=(b * e_neg).astype(BF16), k_t=(k * e_neg).astype(BF16),
            b_d=(b * e_rem).astype(BF16), k_d=(k * e_rem).astype(BF16),
            v_b=v.astype(BF16), v=v, e_last=jnp.exp(cum_last), rkk=r * k * rk_ref[...], g=g_ref[bi]))

    def lanes(p):
        return slice(p * PW, (p + 1) * PW)

    ar = [jnp.concatenate([pre[bi]['a_t'][:, lanes(p)], pre[bi]['r_t'][:, lanes(p)]], axis=0) for bi, p in streams]
    s0 = [s_ref[bi, p] for bi, p in streams]
    m_b = [_dot_nt(ar[i], _pair_blockdiag(pre[bi]['b_t'][:, lanes(p)])) for i, (bi, p) in enumerate(streams)]
    m_k = [_dot_nt(ar[i], _pair_blockdiag(pre[bi]['k_t'][:, lanes(p)])) for i, (bi, p) in enumerate(streams)]
    ars = [_dot_nt(ar[i], s0[i]) for i in range(len(streams))]
    v_p = [pre[bi]['v_b'][:, lanes(p)] for bi, p in streams]
    v_bd = [_pair_blockdiag(vp) for vp in v_p]
    x = [ars[i][:C] + _dot(jnp.where(strict_p, m_k[i][:C], 0.0), v_bd[i]) for i in range(len(streams))]
    pw = [jnp.where(strict_p, m_b[i][:C], 0.0).astype(BF16) for i in range(len(streams))]
    x = [x[i] + _dot(pw[i], _pair_blockdiag(x[i].astype(BF16))) for i in range(len(streams))]
    for _ in range(5):
        pw = [_dot(pw[i], _pair_blockdiag(pw[i])).astype(BF16) for i in range(len(streams))]
        x = [x[i] + _dot(pw[i], _pair_blockdiag(x[i].astype(BF16))) for i in range(len(streams))]
    u_b = [xi.astype(BF16) for xi in x]
    y = [ars[i][C:]
         + _dot(jnp.concatenate([jnp.where(lower_p, m_b[i][C:], 0.0), jnp.where(lower_p, m_k[i][C:], 0.0)], axis=1),
                jnp.concatenate([_pair_blockdiag(u_b[i]), v_bd[i]], axis=0))
         for i in range(len(streams))]
    for i, (bi, p) in enumerate(streams):
        upd = _dot_tn(jnp.concatenate([u_b[i], v_p[i]], axis=0),
                      jnp.concatenate([pre[bi]['b_d'][:, lanes(p)], pre[bi]['k_d'][:, lanes(p)]], axis=0))
        s_ref[bi, p] = s0[i] * pre[bi]['e_last'][:, lanes(p)] + jnp.where(same_head, upd, 0.0)

    lnw = lnw_ref[...]
    lnb = lnb_ref[...]

    def head_sum(t):
        s1 = jnp.sum(jnp.where(first, t, 0.0), axis=-1, keepdims=True)
        s2 = jnp.sum(jnp.where(first, 0.0, t), axis=-1, keepdims=True)
        return jnp.where(first, s1, s2)

    for bi in range(NB):
        outs = []
        for p in range(NP):
            yi = y[bi * NP + p]
            mean = head_sum(yi) * (1.0 / HEAD_DIM)
            yc = yi - mean
            var = head_sum(yc * yc) * (1.0 / HEAD_DIM)
            yn = yc * lax.rsqrt(var + RWKV_LN_EPS) * lnw[:, lanes(p)] + lnb[:, lanes(p)]
            bonus = head_sum(pre[bi]['rkk'][:, lanes(p)]) * pre[bi]['v'][:, lanes(p)]
            outs.append((yn + bonus) * pre[bi]['g'][:, lanes(p)])
        o_ref[bi] = jnp.concatenate(outs, axis=1).astype(o_ref.dtype)


RWKV_BATCH_ROWS = 4


def _rwkv_scan(r, lw, k, v, a, b, g, r_k, lnx_w, lnx_b):
    B, S, W = r.shape
    C = RWKV_CHUNK
    nb = RWKV_BATCH_ROWS if B % RWKV_BATCH_ROWS == 0 else 1
    seq = pl.BlockSpec((nb, C, W), lambda bb, c: (bb, c, 0))
    par = pl.BlockSpec((1, W), lambda bb, c: (0, 0))
    out = pl.pallas_call(
        _rwkv_scan_kernel,
        grid=(B // nb, S // C),
        in_specs=[seq] * 7 + [par] * 3,
        out_specs=seq,
        out_shape=jax.ShapeDtypeStruct((B, S, W), BF16),
        scratch_shapes=[pltpu.VMEM((nb, RWKV_HEADS // 2, 2 * HEAD_DIM, 2 * HEAD_DIM), F32)],
        compiler_params=_cparams(2),
        name="rwkv_scan",
    )(r, lw, k, v, a, b, g, r_k.reshape(1, W), lnx_w.reshape(1, W), lnx_b.reshape(1, W))
    return out.reshape(B * S, W)


def _gla_kernel(q_ref, k_ref, v_ref, og_ref, gd_ref, gup_ref, gb_ref, on_ref, o_ref, s_ref):
    c = pl.program_id(1)

    @pl.when(c == 0)
    def _():
        s_ref[...] = jnp.zeros_like(s_ref)

    C = GLA_CHUNK
    NB = q_ref.shape[0]
    row = lax.broadcasted_iota(I32, (C, C), 0)
    col = lax.broadcasted_iota(I32, (C, C), 1)
    lower = row >= col
    tri = jnp.where(lower, 1.0, 0.0).astype(F32)
    onorm = on_ref[...]
    zs = [_dot(gd_ref[bi], gup_ref[...]) + gb_ref[...] for bi in range(NB)]
    cums = [_dot_f32(tri, -_softplus(-z) / GLA_GATE_NORM) for z in zs]
    qe, ke, kd, e_last, v = [], [], [], [], []
    for bi in range(NB):
        cum = cums[bi]
        cum_last = cum[C - 1:C, :]
        k = k_ref[bi]
        qe.append((q_ref[bi] * (GLA_DK ** -0.5) * jnp.exp(cum)).astype(BF16))
        ke.append((k * jnp.exp(-cum)).astype(BF16))
        kd.append((k * jnp.exp(cum_last - cum)).astype(BF16))
        e_last.append(jnp.exp(cum_last))
        v.append(v_ref[bi].astype(BF16))
    streams = [(bi, h) for bi in range(NB) for h in range(GLA_HEADS)]
    ks = lambda h: slice(h * GLA_DK, (h + 1) * GLA_DK)
    vs = lambda h: slice(h * GLA_DV, (h + 1) * GLA_DV)
    sts = [s_ref[bi, h] for bi, h in streams]
    atts = [jnp.where(lower, _dot_nt(qe[bi][:, ks(h)], ke[bi][:, ks(h)]), 0.0) for bi, h in streams]
    inters = [_dot_nt(qe[bi][:, ks(h)], sts[i]) for i, (bi, h) in enumerate(streams)]
    os_ = [inters[i] + _dot(atts[i], v[bi][:, vs(h)]) for i, (bi, h) in enumerate(streams)]
    for i, (bi, h) in enumerate(streams):
        s_ref[bi, h] = sts[i] * e_last[bi][:, ks(h)] + _dot_tn(v[bi][:, vs(h)], kd[bi][:, ks(h)])
    for bi in range(NB):
        og = og_ref[bi]
        outs = []
        for h in range(GLA_HEADS):
            gate = og[:, vs(h)]
            outs.append(_rms(os_[bi * GLA_HEADS + h], onorm) * (gate * _sigmoid(gate)))
        o_ref[bi] = jnp.concatenate(outs, axis=1).astype(o_ref.dtype)


GLA_BATCH_ROWS = 4


def _gla(qkvo, gd, gate_up_pad, gate_b, onorm, B, S):
    C = GLA_CHUNK
    KW = GLA_HEADS * GLA_DK
    VW = GLA_HEADS * GLA_DV
    x3 = qkvo.reshape(B, S, qkvo.shape[-1])
    gd3 = gd.reshape(B, S, LANES)
    nb = GLA_BATCH_ROWS if B % GLA_BATCH_ROWS == 0 else 1
    out = pl.pallas_call(
        _gla_kernel,
        grid=(B // nb, S // C),
        in_specs=[pl.BlockSpec((nb, C, KW), lambda b, c: (b, c, 0)),
                  pl.BlockSpec((nb, C, KW), lambda b, c: (b, c, 1)),
                  pl.BlockSpec((nb, C, VW), lambda b, c: (b, c, 1)),
                  pl.BlockSpec((nb, C, VW), lambda b, c: (b, c, 2)),
                  pl.BlockSpec((nb, C, LANES), lambda b, c: (b, c, 0)),
                  pl.BlockSpec((LANES, KW), lambda b, c: (0, 0)),
                  pl.BlockSpec((1, KW), lambda b, c: (0, 0)),
                  pl.BlockSpec((1, GLA_DV), lambda b, c: (0, 0))],
        out_specs=pl.BlockSpec((nb, C, VW), lambda b, c: (b, c, 0)),
        out_shape=jax.ShapeDtypeStruct((B, S, VW), BF16),
        scratch_shapes=[pltpu.VMEM((nb, GLA_HEADS, GLA_DV, GLA_DK), F32)],
        compiler_params=_cparams(2),
        name="gla",
    )(x3, x3, x3, x3, gd3, gate_up_pad, gate_b.reshape(1, KW), onorm.reshape(1, GLA_DV))
    return out.reshape(B * S, VW)


def _xattn_kernel(*refs, n_in):
    x_ref = refs[0]
    a_refs = refs[1:1 + n_in]
    w_refs = refs[1 + n_in:1 + 2 * n_in]
    g_ref, wq_ref, mk_ref, mv_ref, wo_ref, o_ref = refs[1 + 2 * n_in:]
    x = x_ref[...]
    for a_ref, w_ref in zip(a_refs, w_refs):
        x = x + jnp.dot(a_ref[...], w_ref[...], preferred_element_type=F32)
    h = _rms(x, g_ref[...])
    q = _dot(h, wq_ref[...]).astype(BF16)
    mk = mk_ref[...]
    mv = mv_ref[...]
    sls = [slice(hd * XA_HEAD_DIM, (hd + 1) * XA_HEAD_DIM) for hd in range(XA_HEADS)]
    scores = [_dot_nt(q[:, sl], mk[:, sl]) for sl in sls]
    probs = []
    for s in scores:
        s = s * (XA_HEAD_DIM ** -0.5)
        p = jnp.exp(s - jnp.max(s, axis=-1, keepdims=True))
        probs.append((p / jnp.sum(p, axis=-1, keepdims=True)).astype(BF16))
    o = jnp.concatenate([_dot(p, mv[:, sl]) for p, sl in zip(probs, sls)], axis=1)
    o_ref[...] = x + _dot(o, wo_ref[...])


def _mix_proj_xattn(x, acts, weights, g, wq, mk, mv, wo, B, S, tq=256):
    D = x.shape[-1]
    M = mk.shape[0] // B
    XW = mk.shape[-1]
    n_in = len(acts)
    seq3 = lambda a: a.reshape(B, S, a.shape[-1])
    row_spec = lambda a: pl.BlockSpec((None, tq, a.shape[-1]), lambda b, n: (b, n, 0))
    const = lambda a: pl.BlockSpec(a.shape, lambda b, n: (0,) * a.ndim)
    out = pl.pallas_call(
        functools.partial(_xattn_kernel, n_in=n_in),
        grid=(B, S // tq),
        in_specs=[row_spec(x)] + [row_spec(a) for a in acts] + [const(w) for w in weights]
                 + [pl.BlockSpec((1, D), lambda b, n: (0, 0)),
                    pl.BlockSpec((D, XW), lambda b, n: (0, 0)),
                    pl.BlockSpec((None, M, XW), lambda b, n: (b, 0, 0)),
                    pl.BlockSpec((None, M, XW), lambda b, n: (b, 0, 0)),
                    pl.BlockSpec((XW, D), lambda b, n: (0, 0))],
        out_specs=pl.BlockSpec((None, tq, D), lambda b, n: (b, n, 0)),
        out_shape=jax.ShapeDtypeStruct((B, S, D), F32),
        compiler_params=_cparams(2),
        name="xattn",
    )(seq3(x), *[seq3(a) for a in acts], *weights, g.reshape(1, D), wq,
      mk.reshape(B, M, XW), mv.reshape(B, M, XW), wo)
    return out.reshape(B * S, D)


ROUTER_ROWS = 40


def _router_kernel(x_ref, g_ref, wt_ref, bt_ref, info_ref, slot_ref, cnt_ref, carry_ref):
    i = pl.program_id(0)

    @pl.when(i == 0)
    def _():
        carry_ref[...] = jnp.zeros_like(carry_ref)

    h = _rms(x_ref[...], g_ref[...])
    tm = h.shape[0]
    logits = (_dot_nt(wt_ref[...], h) + bt_ref[...])[:ROUTER_ROWS]
    row = lax.broadcasted_iota(I32, logits.shape, 0)
    big = jnp.int32(LANES)
    neg = -jnp.inf
    gl = jnp.where(row < MOE_GROUPS, logits, neg)
    gmax = jnp.max(gl, axis=0, keepdims=True)
    g_top = jnp.min(jnp.where(gl == gmax, row, big), axis=0, keepdims=True)
    p_group = 1.0 / jnp.sum(jnp.exp(gl - gmax), axis=0, keepdims=True)
    lo = MOE_GROUPS + MOE_EXPERTS_PER_GROUP * g_top
    in_group = jnp.where(row >= lo, jnp.where(row < lo + MOE_EXPERTS_PER_GROUP, 1, 0), 0) > 0
    el = jnp.where(in_group, logits, neg)
    emax = jnp.max(el, axis=0, keepdims=True)
    ee = jnp.exp(el - emax)
    prob = ee / jnp.sum(ee, axis=0, keepdims=True)
    prob = jnp.where(in_group, prob, -1.0)
    p1 = jnp.max(prob, axis=0, keepdims=True)
    i1 = jnp.min(jnp.where(prob == p1, row, big), axis=0, keepdims=True)
    rest = jnp.where(row == i1, -1.0, prob)
    p2 = jnp.max(rest, axis=0, keepdims=True)
    i2 = jnp.min(jnp.where(rest == p2, row, big), axis=0, keepdims=True)
    tot = p1 + p2
    g1 = p_group * p1 / tot
    g2 = p_group * p2 / tot
    oh = jnp.concatenate([jnp.where(row == i1, 1.0, 0.0), jnp.where(row == i2, 1.0, 0.0)], axis=0)
    tr = lax.broadcasted_iota(I32, (tm, tm), 0)
    tc = lax.broadcasted_iota(I32, (tm, tm), 1)
    pre = _dot(oh, jnp.where(tr < tc, 1.0, 0.0))
    tots = _dot(oh, jnp.ones((tm, LANES), F32))
    reps = tm // LANES
    carry = carry_ref[...]
    base1 = jnp.concatenate([carry] * reps, axis=1)
    base2 = jnp.concatenate([carry + tots[:ROUTER_ROWS]] * reps, axis=1)
    r1 = jnp.sum(oh[:ROUTER_ROWS] * (base1 + pre[:ROUTER_ROWS]), axis=0, keepdims=True)
    r2 = jnp.sum(oh[ROUTER_ROWS:] * (base2 + pre[ROUTER_ROWS:]), axis=0, keepdims=True)
    carry = carry + tots[:ROUTER_ROWS] + tots[ROUTER_ROWS:]
    carry_ref[...] = carry
    cnt_ref[...] = carry
    e1 = (i1 - MOE_GROUPS).astype(F32)
    e2 = (i2 - MOE_GROUPS).astype(F32)
    slot_rows = [e1, e2, r1, r2, g1, g2]
    rows8 = lax.broadcasted_iota(I32, (SUBLANES, tm), 0)
    slot = jnp.zeros((SUBLANES, tm), F32)
    for j, val in enumerate(slot_rows):
        slot = jnp.where(rows8 == j, val, slot)
    slot_ref[...] = slot
    wide = jnp.concatenate([slot, jnp.zeros((LANES - SUBLANES, tm), F32)], axis=0)
    info_ref[...] = jnp.transpose(wide)


def _router(x, g, wt_router, bt_router, tm=256):
    T, D = x.shape
    NT = T // tm
    return pl.pallas_call(
        _router_kernel,
        grid=(NT,),
        in_specs=[pl.BlockSpec((tm, D), lambda i: (i, 0)),
                  pl.BlockSpec((1, D), lambda i: (0, 0)),
                  pl.BlockSpec((LANES, D), lambda i: (0, 0)),
                  pl.BlockSpec((LANES, tm), lambda i: (0, 0))],
        out_specs=[pl.BlockSpec((tm, LANES), lambda i: (i, 0)),
                   pl.BlockSpec((None, SUBLANES, tm), lambda i: (i, 0, 0)),
                   pl.BlockSpec((ROUTER_ROWS, LANES), lambda i: (0, 0))],
        out_shape=[jax.ShapeDtypeStruct((T, LANES), F32),
                   jax.ShapeDtypeStruct((NT, SUBLANES, tm), F32),
                   jax.ShapeDtypeStruct((ROUTER_ROWS, LANES), F32)],
        scratch_shapes=[pltpu.VMEM((ROUTER_ROWS, LANES), F32)],
        compiler_params=_cparams(1),
        name="router",
    )(x, g.reshape(1, D), wt_router, bt_router)


def _row_bytes_wait(hbm, buf, sem):
    pltpu.make_async_copy(buf, hbm.at[pl.ds(0, buf.shape[0]), :], sem).wait()


def _to_row_tiles(ref, val):
    n = val.shape[0]
    for c in range(SUBLANES):
        ref[pl.ds(c, n, stride=SUBLANES), :] = val[:, c * LANES:(c + 1) * LANES]


def _from_row_tiles(ref):
    n = ref.shape[0] // SUBLANES
    return jnp.concatenate([ref[pl.ds(c, n, stride=SUBLANES), :] for c in range(SUBLANES)], axis=1)


def _moe_dispatch_kernel(pends_ref, cnt_ref, dest_ref, x_ref, g_ref, hs_hbm, hbuf, zbuf, sems, zsem, *, td):
    i = pl.program_id(0)
    nt = pl.num_programs(0)
    slot = lax.rem(i, 2)

    @pl.when(i == 0)
    def _():
        zbuf[...] = jnp.zeros_like(zbuf)
        for e in range(MOE_EXPERTS):
            @pl.when(cnt_ref[e] > 0)
            def _():
                start = pl.multiple_of((pends_ref[e] - MOE_BLOCK) * SUBLANES, MOE_BLOCK)
                pltpu.make_async_copy(zbuf, hs_hbm.at[pl.ds(start, MOE_BLOCK * SUBLANES), :], zsem).start()
        for e in range(MOE_EXPERTS):
            @pl.when(cnt_ref[e] > 0)
            def _():
                pltpu.make_async_copy(zbuf, hs_hbm.at[pl.ds(0, MOE_BLOCK * SUBLANES), :], zsem).wait()

        first_unused = pends_ref[MOE_EXPERTS - 1] // MOE_BLOCK
        n_blocks = hs_hbm.shape[0] // (MOE_BLOCK * SUBLANES)

        def zero_start(blk, carry):
            start = pl.multiple_of(blk * (MOE_BLOCK * SUBLANES), MOE_BLOCK)
            pltpu.make_async_copy(zbuf, hs_hbm.at[pl.ds(start, MOE_BLOCK * SUBLANES), :], zsem).start()
            return carry

        def zero_wait(blk, carry):
            pltpu.make_async_copy(zbuf, hs_hbm.at[pl.ds(0, MOE_BLOCK * SUBLANES), :], zsem).wait()
            return carry

        lax.fori_loop(first_unused, n_blocks, zero_start, 0)
        lax.fori_loop(first_unused, n_blocks, zero_wait, 0)

    hb = hbuf.at[slot]
    _to_row_tiles(hb, _rms(x_ref[...], g_ref[...]))
    for j in range(td):
        for c in range(2):
            row = pl.multiple_of(dest_ref[0, c * td + j] * SUBLANES, SUBLANES)
            pltpu.make_async_copy(hb.at[pl.ds(j * SUBLANES, SUBLANES), :],
                                  hs_hbm.at[pl.ds(row, SUBLANES), :],
                                  sems.at[slot]).start(priority=c)

    @pl.when(i > 0)
    def _():
        other = hbuf.at[1 - slot]
        _row_bytes_wait(hs_hbm, other, sems.at[1 - slot])
        _row_bytes_wait(hs_hbm, other, sems.at[1 - slot])

    @pl.when(i == nt - 1)
    def _():
        _row_bytes_wait(hs_hbm, hb, sems.at[slot])
        _row_bytes_wait(hs_hbm, hb, sems.at[slot])


def _moe_dispatch(x, g, pends, counts, dest3, P, td):
    T, D = x.shape
    assert D == ROW_TILE
    grid_spec = pltpu.PrefetchScalarGridSpec(
        num_scalar_prefetch=2,
        grid=(T // td,),
        in_specs=[pl.BlockSpec((None, 1, 2 * td), lambda i, pe, cn: (i, 0, 0), memory_space=pltpu.SMEM),
                  pl.BlockSpec((td, D), lambda i, pe, cn: (i, 0)),
                  pl.BlockSpec((1, D), lambda i, pe, cn: (0, 0))],
        out_specs=pl.BlockSpec(memory_space=pl.ANY),
        scratch_shapes=[pltpu.VMEM((2, td * SUBLANES, LANES), F32),
                        pltpu.VMEM((MOE_BLOCK * SUBLANES, LANES), F32),
                        pltpu.SemaphoreType.DMA((2,)),
                        pltpu.SemaphoreType.DMA(())],
    )
    return pl.pallas_call(
        functools.partial(_moe_dispatch_kernel, td=td),
        grid_spec=grid_spec,
        out_shape=jax.ShapeDtypeStruct((P * SUBLANES, LANES), F32),
        compiler_params=_cparams(1),
        name="moe_dispatch",
    )(pends, counts, dest3, x, g.reshape(1, D))


def _moe_expert_kernel(be_ref, nu_ref, hs_ref, w1_ref, w3_ref, w2_ref, o_ref, w1b, w3b, w2b):
    i = pl.program_id(0)
    used = i < nu_ref[0]
    changed = jnp.logical_or(i == 0, be_ref[i] != be_ref[jnp.maximum(i - 1, 0)])

    @pl.when(jnp.logical_and(used, changed))
    def _():
        w1b[...] = w1_ref[...].astype(BF16)
        w3b[...] = w3_ref[...].astype(BF16)
        w2b[...] = w2_ref[...].astype(BF16)

    @pl.when(used)
    def _():
        xe = _from_row_tiles(hs_ref).astype(BF16)
        h1 = jnp.dot(xe, w1b[...], preferred_element_type=F32)
        h3 = jnp.dot(xe, w3b[...], preferred_element_type=F32)
        act = (h1 * _sigmoid(h1) * h3).astype(BF16)
        _to_row_tiles(o_ref, jnp.dot(act, w2b[...], preferred_element_type=F32))

    @pl.when(jnp.logical_not(used))
    def _():
        o_ref[...] = jnp.zeros_like(o_ref)


def _moe_experts(hs, block_e, n_used, w1, w3, w2, layer):
    P = hs.shape[0] // SUBLANES
    D = ROW_TILE
    FF = w1.shape[-1]
    NB = P // MOE_BLOCK
    last = lambda i, nu: jnp.minimum(i, nu[0] - 1)
    grid_spec = pltpu.PrefetchScalarGridSpec(
        num_scalar_prefetch=2,
        grid=(NB,),
        in_specs=[pl.BlockSpec((MOE_BLOCK * SUBLANES, LANES), lambda i, be, nu: (last(i, nu), 0)),
                  pl.BlockSpec((None, None, D, FF), lambda i, be, nu: (layer, be[last(i, nu)], 0, 0)),
                  pl.BlockSpec((None, None, D, FF), lambda i, be, nu: (layer, be[last(i, nu)], 0, 0)),
                  pl.BlockSpec((None, None, FF, D), lambda i, be, nu: (layer, be[last(i, nu)], 0, 0))],
        out_specs=pl.BlockSpec((MOE_BLOCK * SUBLANES, LANES), lambda i, be, nu: (i, 0)),
        scratch_shapes=[pltpu.VMEM((D, FF), BF16),
                        pltpu.VMEM((D, FF), BF16),
                        pltpu.VMEM((FF, D), BF16)],
    )
    return pl.pallas_call(
        _moe_expert_kernel,
        grid_spec=grid_spec,
        out_shape=jax.ShapeDtypeStruct((P * SUBLANES, LANES), F32),
        compiler_params=_cparams(1),
        name="moe_experts",
    )(block_e, n_used, hs, w1, w3, w2)


def _gather_rows(src_hbm, idx_ref, dst_ref, sem, n_rows):
    for r in range(n_rows):
        row = pl.multiple_of(idx_ref[0, r] * SUBLANES, SUBLANES)
        pltpu.make_async_copy(src_hbm.at[pl.ds(row, SUBLANES), :],
                              dst_ref.at[pl.ds(r * SUBLANES, SUBLANES), :], sem).start(priority=r % 2)


def _moe_combine_kernel(pos_ref, posn_ref, x_ref, info_ref, yb_hbm, gf_ref, o_ref, ybuf, sems, *, tc, final_norm):
    i = pl.program_id(0)
    nb = pl.num_programs(0)
    slot = lax.rem(i, 2)

    @pl.when(i == 0)
    def _():
        def issue(r, carry):
            src = pl.multiple_of(pos_ref[0, r] * SUBLANES, SUBLANES)
            dst = pl.multiple_of(r * SUBLANES, SUBLANES)
            pltpu.make_async_copy(yb_hbm.at[pl.ds(src, SUBLANES), :],
                                  ybuf.at[0, pl.ds(dst, SUBLANES), :], sems.at[0]).start()
            return carry
        lax.fori_loop(0, 2 * tc, issue, 0)

    @pl.when(i + 1 < nb)
    def _():
        _gather_rows(yb_hbm, posn_ref, ybuf.at[1 - slot], sems.at[1 - slot], 2 * tc)

    pltpu.make_async_copy(yb_hbm.at[pl.ds(0, 2 * tc * SUBLANES), :], ybuf.at[slot], sems.at[slot]).wait()
    info = info_ref[...]
    yb = ybuf.at[slot]
    y0 = _from_row_tiles(yb.at[pl.ds(0, tc * SUBLANES), :])
    y1 = _from_row_tiles(yb.at[pl.ds(tc * SUBLANES, tc * SUBLANES), :])
    out = x_ref[...] + (y0 * info[:, 4:5] + y1 * info[:, 5:6])
    if final_norm:
        out = _rms(out, gf_ref[...])
    o_ref[...] = out


def _moe_combine(x, info, dest3, yb, g_final, final_norm, tc):
    T, D = x.shape
    NT = T // tc
    return pl.pallas_call(
        functools.partial(_moe_combine_kernel, tc=tc, final_norm=final_norm),
        grid=(NT,),
        in_specs=[pl.BlockSpec((None, 1, 2 * tc), lambda i: (i, 0, 0), memory_space=pltpu.SMEM),
                  pl.BlockSpec((None, 1, 2 * tc), lambda i: (jnp.minimum(i + 1, NT - 1), 0, 0),
                               memory_space=pltpu.SMEM),
                  pl.BlockSpec((tc, D), lambda i: (i, 0)),
                  pl.BlockSpec((tc, LANES), lambda i: (i, 0)),
                  pl.BlockSpec(memory_space=pl.ANY),
                  pl.BlockSpec((1, D), lambda i: (0, 0))],
        out_specs=pl.BlockSpec((tc, D), lambda i: (i, 0)),
        out_shape=jax.ShapeDtypeStruct((T, D), F32),
        scratch_shapes=[pltpu.VMEM((2, 2 * tc * SUBLANES, LANES), F32), pltpu.SemaphoreType.DMA((2,))],
        compiler_params=_cparams(1),
        name="moe_combine",
    )(dest3, dest3, x, info, yb, g_final.reshape(1, D))


MOE_TILE = 128


def _moe_layer(x, g, w_group, b_group, w_expert, b_expert, w1, w3, w2, layer, g_final, final_norm):
    T, D = x.shape
    n_log = MOE_GROUPS + MOE_EXPERTS
    tm = 2 * MOE_TILE
    wt_router = jnp.zeros((LANES, D), F32).at[:MOE_GROUPS].set(w_group.T).at[MOE_GROUPS:n_log].set(w_expert.T)
    bt_router = jnp.zeros((LANES,), F32).at[:MOE_GROUPS].set(b_group).at[MOE_GROUPS:n_log].set(b_expert)
    info, slot, cnt = _router(x, g, wt_router.astype(BF16), jnp.broadcast_to(bt_router[:, None], (LANES, tm)), tm)
    P = 2 * T + MOE_EXPERTS * MOE_BLOCK
    NB = P // MOE_BLOCK
    counts = cnt[MOE_GROUPS:n_log, 0].astype(I32)
    padded = (counts + MOE_BLOCK - 1) // MOE_BLOCK * MOE_BLOCK
    pends = jnp.cumsum(padded).astype(I32)
    pstarts = pends - padded
    block_start = jnp.arange(NB, dtype=I32) * MOE_BLOCK
    block_e = jnp.minimum(jnp.sum((pends[None, :] <= block_start[:, None]).astype(I32), axis=1),
                          MOE_EXPERTS - 1).astype(I32)
    n_used = (pends[-1:] // MOE_BLOCK).astype(I32)
    NT = T // MOE_TILE
    eid = slot[:, 0:2, :].astype(I32)
    dest = pstarts[eid] + slot[:, 2:4, :].astype(I32)
    dest3 = jnp.concatenate([dest[:, 0, :].reshape(NT, 1, MOE_TILE), dest[:, 1, :].reshape(NT, 1, MOE_TILE)], axis=2)
    hs = _moe_dispatch(x, g, pends, counts, dest3, P, MOE_TILE)
    yb = _moe_experts(hs, block_e, n_used, w1, w3, w2, layer)
    return _moe_combine(x, info, dest3, yb, g_final, final_norm, MOE_TILE)


def kernel(x, mem, norm_mix, norm_xattn, norm_moe, norm_final, ev_w_in, ev_sinks, ev_mu, ev_w0, ev_w2, ev_a0, ev_a2, ev_g2, ev_k_k, ev_k_a, ev_r_k, ev_lnx_w, ev_lnx_b, ev_w_out, od_w_in, od_gate_up, od_gate_b, od_onorm, od_w_out, mem_norm, mem_wk, mem_wv, xa_wq, xa_wo, moe_w_group, moe_b_group, moe_w_expert, moe_b_expert, moe_w1, moe_w3, moe_w2):
    B, S, D = x.shape
    M = mem.shape[1]
    T = B * S
    depth = norm_mix.shape[0]
    xf = x.reshape(T, D)

    XW = XA_HEADS * XA_HEAD_DIM
    w_kv = jnp.concatenate([mem_wk, mem_wv], axis=1).astype(BF16)
    mk, mv = _norm_matmul(mem.reshape(B * M, D), mem_norm, w_kv, (XW, XW), (BF16, BF16))

    for layer in range(depth):
        i = layer // 2
        if layer % 2 == 0:
            swa_cols = SWA_Q_HEADS * HEAD_DIM + 2 * (SWA_Q_HEADS // SWA_GROUP) * HEAD_DIM
            rw_cols = ev_w_in.shape[-1] - swa_cols
            qkv, p_rw = _norm_matmul(xf, norm_mix[layer], ev_w_in[i].astype(BF16),
                                     (swa_cols, rw_cols), (F32, F32))
            o_a = _swa(qkv, ev_sinks[i], B, S)
            r, lw, k, v, a, b, g = _rwkv_prep(p_rw, ev_mu[i], ev_w0[i], ev_w2[i], ev_a0[i], ev_a2[i],
                                              ev_g2[i], ev_k_k[i], ev_k_a[i], B, S)
            o_b = _rwkv_scan(r, lw, k, v, a, b, g, ev_r_k[i].reshape(-1), ev_lnx_w[i], ev_lnx_b[i])
            w_out = ev_w_out[i].astype(BF16)
            qw = o_a.shape[-1]
            mix_acts, mix_ws = [o_a, o_b], [w_out[:qw], w_out[qw:]]
        else:
            KW = GLA_HEADS * GLA_DK
            VW = GLA_HEADS * GLA_DV
            R = od_gate_up.shape[1]
            w = od_w_in[i]
            w_re = jnp.concatenate([w[:, :2 * KW + VW], w[:, 2 * KW + VW + R:],
                                    w[:, 2 * KW + VW:2 * KW + VW + R],
                                    jnp.zeros((D, LANES - R), F32)], axis=1).astype(BF16)
            qkvo, gd = _norm_matmul(xf, norm_mix[layer], w_re, (2 * KW + 2 * VW, LANES), (F32, F32))
            gup = jnp.zeros((LANES, KW), F32).at[:R].set(od_gate_up[i]).astype(BF16)
            o = _gla(qkvo, gd, gup, od_gate_b[i], od_onorm[i], B, S)
            mix_acts, mix_ws = [o], [od_w_out[i].astype(BF16)]
        xf = _mix_proj_xattn(xf, mix_acts, mix_ws, norm_xattn[layer], xa_wq[layer].astype(BF16), mk, mv,
                             xa_wo[layer].astype(BF16), B, S)
        xf = _moe_layer(xf, norm_moe[layer], moe_w_group[layer], moe_b_group[layer], moe_w_expert[layer],
                        moe_b_expert[layer], moe_w1, moe_w3, moe_w2, layer,
                        norm_final, layer == depth - 1)
    return xf.reshape(B, S, D)
```

```python
import functools

import jax
import jax.numpy as jnp
from jax import lax
from jax.experimental import pallas as pl
from jax.experimental.pallas import tpu as pltpu

F32 = jnp.float32
BF16 = jnp.bfloat16
I32 = jnp.int32

EPS = 1e-6
HEAD_DIM = 64
SWA_WINDOW = 128
SWA_Q_HEADS = 8
SWA_GROUP = 4
RWKV_HEADS = 8
RWKV_WIDTH = 512
RWKV_LN_EPS = 64e-5
RWKV_CHUNK = 64
GLA_HEADS = 4
GLA_DK = 128
GLA_DV = 256
GLA_CHUNK = 64
GLA_GATE_NORM = 16.0
XA_HEADS = 4
XA_HEAD_DIM = 128
MOE_GROUPS = 4
MOE_EXPERTS_PER_GROUP = 8
MOE_EXPERTS = 32
MOE_BLOCK = 512
LANES = 128
SUBLANES = 8
ROW_TILE = SUBLANES * LANES

VMEM_LIMIT_BYTES = 48 * 1024 * 1024


def _cparams(n_axes):
    return pltpu.CompilerParams(dimension_semantics=("arbitrary",) * n_axes,
                                vmem_limit_bytes=VMEM_LIMIT_BYTES)


def _dot(a, b):
    return jnp.dot(a.astype(BF16), b.astype(BF16), preferred_element_type=F32)


def _dot_nt(a, b):
    return lax.dot_general(a.astype(BF16), b.astype(BF16), (((1,), (1,)), ((), ())),
                           preferred_element_type=F32)


def _dot_tn(a, b):
    return lax.dot_general(a.astype(BF16), b.astype(BF16), (((0,), (0,)), ((), ())),
                           preferred_element_type=F32)


def _dot_f32(a, b):
    return jnp.dot(a, b, preferred_element_type=F32, precision=lax.Precision.HIGHEST)


def _rms(x, g):
    ms = jnp.mean(x * x, axis=-1, keepdims=True)
    return x * lax.rsqrt(ms + EPS) * g


def _sigmoid(x):
    return 1.0 / (1.0 + jnp.exp(-x))


def _softplus(x):
    return jnp.maximum(x, 0.0) + jnp.log(1.0 + jnp.exp(-jnp.abs(x)))


def _norm_matmul_kernel(x_ref, g_ref, w_ref, *o_refs, splits):
    h = _rms(x_ref[...], g_ref[...]).astype(BF16)
    off = 0
    for o_ref, n in zip(o_refs, splits):
        o_ref[...] = jnp.dot(h, w_ref[:, off:off + n], preferred_element_type=F32).astype(o_ref.dtype)
        off += n


def _norm_matmul(x, g, w, splits, out_dtypes, tm=256):
    T, D = x.shape
    N = w.shape[1]
    assert sum(splits) == N and T % tm == 0
    return pl.pallas_call(
        functools.partial(_norm_matmul_kernel, splits=tuple(splits)),
        grid=(T // tm,),
        in_specs=[pl.BlockSpec((tm, D), lambda i: (i, 0)),
                  pl.BlockSpec((1, D), lambda i: (0, 0)),
                  pl.BlockSpec((D, N), lambda i: (0, 0))],
        out_specs=[pl.BlockSpec((tm, n), lambda i: (i, 0)) for n in splits],
        out_shape=[jax.ShapeDtypeStruct((T, n), dt) for n, dt in zip(splits, out_dtypes)],
        compiler_params=_cparams(1),
        name="norm_matmul",
    )(x, g.reshape(1, D), w)


def _swa_kernel(sinks_ref, q_ref, kp_ref, kc_ref, vp_ref, vc_ref, o_ref):
    n = pl.program_id(1)
    W = SWA_WINDOW
    q = q_ref[...]
    k = jnp.concatenate([kp_ref[...], kc_ref[...]], axis=0)
    v = jnp.concatenate([vp_ref[...], vc_ref[...]], axis=0)
    qpos = lax.broadcasted_iota(I32, (W, 2 * W), 0) + W
    kpos = lax.broadcasted_iota(I32, (W, 2 * W), 1)
    rel = qpos - kpos
    in_window = jnp.where(rel >= 0, jnp.where(rel < W, 1, 0), 0)
    has_prev = jnp.where(n > 0, 1, 0)
    valid = (in_window * jnp.where(kpos >= W, 1, has_prev)) > 0
    n_groups = SWA_Q_HEADS // SWA_GROUP
    qb = q.astype(BF16)
    kb = k.astype(BF16)
    vb = v.astype(BF16)
    scores = []
    for g in range(n_groups):
        qg = jnp.concatenate([qb[:, h * HEAD_DIM:(h + 1) * HEAD_DIM]
                              for h in range(g * SWA_GROUP, (g + 1) * SWA_GROUP)], axis=0)
        scores.append(_dot_nt(qg, kb[:, g * HEAD_DIM:(g + 1) * HEAD_DIM]))
    probs = []
    for g in range(n_groups):
        pieces = []
        for j in range(SWA_GROUP):
            s = jnp.where(valid, scores[g][j * W:(j + 1) * W] * (HEAD_DIM ** -0.5), -jnp.inf)
            sink = sinks_ref[g * SWA_GROUP + j]
            m = jnp.maximum(jnp.max(s, axis=-1, keepdims=True), sink)
            p = jnp.exp(s - m)
            den = jnp.sum(p, axis=-1, keepdims=True) + jnp.exp(sink - m)
            pieces.append((p / den).astype(BF16))
        probs.append(jnp.concatenate(pieces, axis=0))
    outs = []
    for g in range(n_groups):
        og = _dot(probs[g], vb[:, g * HEAD_DIM:(g + 1) * HEAD_DIM])
        outs += [og[j * W:(j + 1) * W] for j in range(SWA_GROUP)]
    o_ref[...] = jnp.concatenate(outs, axis=1).astype(o_ref.dtype)


def _swa(qkv, sinks, B, S):
    W = SWA_WINDOW
    qkv3 = qkv.reshape(B, S, qkv.shape[-1])
    qw = SWA_Q_HEADS * HEAD_DIM
    kw = qw // SWA_GROUP
    kcol = qw // kw
    out = pl.pallas_call(
        _swa_kernel,
        grid=(B, S // W),
        in_specs=[pl.BlockSpec(memory_space=pltpu.SMEM),
                  pl.BlockSpec((None, W, qw), lambda b, n: (b, n, 0)),
                  pl.BlockSpec((None, W, kw), lambda b, n: (b, jnp.maximum(n - 1, 0), kcol)),
                  pl.BlockSpec((None, W, kw), lambda b, n: (b, n, kcol)),
                  pl.BlockSpec((None, W, kw), lambda b, n: (b, jnp.maximum(n - 1, 0), kcol + 1)),
                  pl.BlockSpec((None, W, kw), lambda b, n: (b, n, kcol + 1))],
        out_specs=pl.BlockSpec((None, W, qw), lambda b, n: (b, n, 0)),
        out_shape=jax.ShapeDtypeStruct((B, S, qw), BF16),
        compiler_params=_cparams(2),
        name="swa",
    )(sinks, qkv3, qkv3, qkv3, qkv3, qkv3)
    return out.reshape(B * S, qw)


def _rwkv_prep_kernel(p_ref, pprev_ref, mu_ref, w0_ref, w2_ref, a0_ref, a2_ref, g2_ref, kk_ref, ka_ref,
                      r_out, lw_out, k_out, v_out, a_out, b_out, g_out):
    n = pl.program_id(1)
    C = RWKV_WIDTH
    p = p_ref[...]
    last = jnp.where(n > 0, pprev_ref[7:8, :], 0.0)
    row = lax.broadcasted_iota(I32, p.shape, 0)
    p_prev = jnp.where(row == 0, last, pltpu.roll(p, 1, axis=0))
    p = p + (p_prev - p) * mu_ref[...]
    r = p[:, :C]
    k = p[:, C:2 * C]
    v = p[:, 2 * C:3 * C]
    xw = p[:, 3 * C:3 * C + 64]
    xa = p[:, 3 * C + 64:3 * C + 128]
    xg = p[:, 3 * C + 128:]
    w = -_softplus(-(w0_ref[...] + _dot(jnp.tanh(xw), w2_ref[...]))) - 0.5
    lw = -jnp.exp(w)
    a = _sigmoid(a0_ref[...] + _dot(xa, a2_ref[...]))
    g = _dot(_sigmoid(xg), g2_ref[...])
    kk = k * kk_ref[...]
    pieces = []
    for h in range(RWKV_HEADS):
        kh = kk[:, h * HEAD_DIM:(h + 1) * HEAD_DIM]
        nrm = jnp.sqrt(jnp.sum(kh * kh, axis=-1, keepdims=True))
        pieces.append(kh / jnp.maximum(nrm, 1e-12))
    kk = jnp.concatenate(pieces, axis=1)
    r_out[...] = r
    lw_out[...] = lw
    k_out[...] = k * (1.0 + (a - 1.0) * ka_ref[...])
    v_out[...] = v
    a_out[...] = -kk
    b_out[...] = kk * a
    g_out[...] = g


def _rwkv_prep(p, mu, w0, w2, a0, a2, g2, k_k, k_a, B, S, tt=256):
    C = RWKV_WIDTH
    PW = p.shape[-1]
    p3 = p.reshape(B, S, PW)
    row = lambda t: t.reshape(1, -1)
    full = lambda arr: pl.BlockSpec(arr.shape, lambda b, n: (0,) * arr.ndim)
    params = [row(mu), row(w0), w2.astype(BF16), row(a0), a2.astype(BF16), g2.astype(BF16), row(k_k), row(k_a)]
    outs = pl.pallas_call(
        _rwkv_prep_kernel,
        grid=(B, S // tt),
        in_specs=[pl.BlockSpec((None, tt, PW), lambda b, n: (b, n, 0)),
                  pl.BlockSpec((None, 8, PW), lambda b, n: (b, jnp.maximum(n * (tt // 8) - 1, 0), 0))]
                 + [full(t) for t in params],
        out_specs=[pl.BlockSpec((None, tt, C), lambda b, n: (b, n, 0))] * 7,
        out_shape=[jax.ShapeDtypeStruct((B, S, C), F32)] * 7,
        compiler_params=_cparams(2),
        name="rwkv_prep",
    )(p3, p3, *params)
    return outs


def _pair_blockdiag(x):
    lane = lax.broadcasted_iota(I32, x.shape, 1)
    zero = jnp.zeros_like(x)
    return jnp.concatenate([jnp.where(lane < HEAD_DIM, x, zero), jnp.where(lane >= HEAD_DIM, x, zero)], axis=0)


def _rwkv_scan_kernel(r_ref, lw_ref, k_ref, v_ref, a_ref, b_ref, g_ref, rk_ref, lnw_ref, lnb_ref,
                      o_ref, s_ref):
    c = pl.program_id(1)

    @pl.when(c == 0)
    def _():
        s_ref[...] = jnp.zeros_like(s_ref)

    C = RWKV_CHUNK
    NB = r_ref.shape[0]
    NP = RWKV_HEADS // 2
    PW = 2 * HEAD_DIM
    row = lax.broadcasted_iota(I32, (C, C), 0)
    col = lax.broadcasted_iota(I32, (C, C), 1)
    tri = jnp.where(row >= col, 1.0, 0.0).astype(F32)
    rowp = lax.broadcasted_iota(I32, (C, PW), 0)
    colp = lax.broadcasted_iota(I32, (C, PW), 1)
    colp = jnp.where(colp >= HEAD_DIM, colp - HEAD_DIM, colp)
    lower_p = rowp >= colp
    strict_p = rowp > colp
    rows = lax.broadcasted_iota(I32, (PW, PW), 0)
    cols = lax.broadcasted_iota(I32, (PW, PW), 1)
    same_head = jnp.where(rows >= HEAD_DIM, 1, 0) == jnp.where(cols >= HEAD_DIM, 1, 0)
    first = lax.broadcasted_iota(I32, (C, PW), 1) < HEAD_DIM

    streams = [(bi, p) for bi in range(NB) for p in range(NP)]
    pre = []
    for bi in range(NB):
        lw = lw_ref[bi]
        cum = _dot_f32(tri, lw)
        cum_last = cum[C - 1:C, :]
        r = r_ref[bi]
        k = k_ref[bi]
        v = v_ref[bi]
        a = a_ref[bi]
        b = b_ref[bi]
        e_neg = jnp.exp(-cum)
        e_rem = jnp.exp(cum_last - cum)
        pre.append(dict(
            r_t=(r * jnp.exp(cum)).astype(BF16), a_t=(a * jnp.exp(cum - lw)).astype(BF16),
            b_t=(b * e_neg).astype(BF16), k_t=(k * e_neg).astype(BF16),
            b_d=(b * e_rem).astype(BF16), k_d=(k * e_rem).astype(BF16),
            v_b=v.astype(BF16), v=v, e_last=jnp.exp(cum_last), rkk=r * k * rk_ref[...], g=g_ref[bi]))

    def lanes(p):
        return slice(p * PW, (p + 1) * PW)

    ar = [jnp.concatenate([pre[bi]['a_t'][:, lanes(p)], pre[bi]['r_t'][:, lanes(p)]], axis=0) for bi, p in streams]
    s0 = [s_ref[bi, p] for bi, p in streams]
    m_b = [_dot_nt(ar[i], _pair_blockdiag(pre[bi]['b_t'][:, lanes(p)])) for i, (bi, p) in enumerate(streams)]
    m_k = [_dot_nt(ar[i], _pair_blockdiag(pre[bi]['k_t'][:, lanes(p)])) for i, (bi, p) in enumerate(streams)]
    ars = [_dot_nt(ar[i], s0[i]) for i in range(len(streams))]
    v_p = [pre[bi]['v_b'][:, lanes(p)] for bi, p in streams]
    v_bd = [_pair_blockdiag(vp) for vp in v_p]
    x = [ars[i][:C] + _dot(jnp.where(strict_p, m_k[i][:C], 0.0), v_bd[i]) for i in range(len(streams))]
    pw = [jnp.where(strict_p, m_b[i][:C], 0.0).astype(BF16) for i in range(len(streams))]
    x = [x[i] + _dot(pw[i], _pair_blockdiag(x[i].astype(BF16))) for i in range(len(streams))]
    for _ in range(5):
        pw = [_dot(pw[i], _pair_blockdiag(pw[i])).astype(BF16) for i in range(len(streams))]
        x = [x[i] + _dot(pw[i], _pair_blockdiag(x[i].astype(BF16))) for i in range(len(streams))]
    u_b = [xi.astype(BF16) for xi in x]
    y = [ars[i][C:]
         + _dot(jnp.concatenate([jnp.where(lower_p, m_b[i][C:], 0.0), jnp.where(lower_p, m_k[i][C:], 0.0)], axis=1),
                jnp.concatenate([_pair_blockdiag(u_b[i]), v_bd[i]], axis=0))
         for i in range(len(streams))]
    for i, (bi, p) in enumerate(streams):
        upd = _dot_tn(jnp.concatenate([u_b[i], v_p[i]], axis=0),
                      jnp.concatenate([pre[bi]['b_d'][:, lanes(p)], pre[bi]['k_d'][:, lanes(p)]], axis=0))
        s_ref[bi, p] = s0[i] * pre[bi]['e_last'][:, lanes(p)] + jnp.where(same_head, upd, 0.0)

    lnw = lnw_ref[...]
    lnb = lnb_ref[...]

    def head_sum(t):
        s1 = jnp.sum(jnp.where(first, t, 0.0), axis=-1, keepdims=True)
        s2 = jnp.sum(jnp.where(first, 0.0, t), axis=-1, keepdims=True)
        return jnp.where(first, s1, s2)

    for bi in range(NB):
        outs = []
        for p in range(NP):
            yi = y[bi * NP + p]
            mean = head_sum(yi) * (1.0 / HEAD_DIM)
            yc = yi - mean
            var = head_sum(yc * yc) * (1.0 / HEAD_DIM)
            yn = yc * lax.rsqrt(var + RWKV_LN_EPS) * lnw[:, lanes(p)] + lnb[:, lanes(p)]
            bonus = head_sum(pre[bi]['rkk'][:, lanes(p)]) * pre[bi]['v'][:, lanes(p)]
            outs.append((yn + bonus) * pre[bi]['g'][:, lanes(p)])
        o_ref[bi] = jnp.concatenate(outs, axis=1).astype(o_ref.dtype)


RWKV_BATCH_ROWS = 4


def _rwkv_scan(r, lw, k, v, a, b, g, r_k, lnx_w, lnx_b):
    B, S, W = r.shape
    C = RWKV_CHUNK
    nb = RWKV_BATCH_ROWS if B % RWKV_BATCH_ROWS == 0 else 1
    seq = pl.BlockSpec((nb, C, W), lambda bb, c: (bb, c, 0))
    par = pl.BlockSpec((1, W), lambda bb, c: (0, 0))
    out = pl.pallas_call(
        _rwkv_scan_kernel,
        grid=(B // nb, S // C),
        in_specs=[seq] * 7 + [par] * 3,
        out_specs=seq,
        out_shape=jax.ShapeDtypeStruct((B, S, W), BF16),
        scratch_shapes=[pltpu.VMEM((nb, RWKV_HEADS // 2, 2 * HEAD_DIM, 2 * HEAD_DIM), F32)],
        compiler_params=_cparams(2),
        name="rwkv_scan",
    )(r, lw, k, v, a, b, g, r_k.reshape(1, W), lnx_w.reshape(1, W), lnx_b.reshape(1, W))
    return out.reshape(B * S, W)


def _gla_kernel(q_ref, k_ref, v_ref, og_ref, gd_ref, gup_ref, gb_ref, on_ref, o_ref, s_ref):
    c = pl.program_id(1)

    @pl.when(c == 0)
    def _():
        s_ref[...] = jnp.zeros_like(s_ref)

    C = GLA_CHUNK
    NB = q_ref.shape[0]
    row = lax.broadcasted_iota(I32, (C, C), 0)
    col = lax.broadcasted_iota(I32, (C, C), 1)
    lower = row >= col
    tri = jnp.where(lower, 1.0, 0.0).astype(F32)
    onorm = on_ref[...]
    zs = [_dot(gd_ref[bi], gup_ref[...]) + gb_ref[...] for bi in range(NB)]
    cums = [_dot_f32(tri, -_softplus(-z) / GLA_GATE_NORM) for z in zs]
    qe, ke, kd, e_last, v = [], [], [], [], []
    for bi in range(NB):
        cum = cums[bi]
        cum_last = cum[C - 1:C, :]
        k = k_ref[bi]
        qe.append((q_ref[bi] * (GLA_DK ** -0.5) * jnp.exp(cum)).astype(BF16))
        ke.append((k * jnp.exp(-cum)).astype(BF16))
        kd.append((k * jnp.exp(cum_last - cum)).astype(BF16))
        e_last.append(jnp.exp(cum_last))
        v.append(v_ref[bi].astype(BF16))
    streams = [(bi, h) for bi in range(NB) for h in range(GLA_HEADS)]
    ks = lambda h: slice(h * GLA_DK, (h + 1) * GLA_DK)
    vs = lambda h: slice(h * GLA_DV, (h + 1) * GLA_DV)
    sts = [s_ref[bi, h] for bi, h in streams]
    atts = [jnp.where(lower, _dot_nt(qe[bi][:, ks(h)], ke[bi][:, ks(h)]), 0.0) for bi, h in streams]
    inters = [_dot_nt(qe[bi][:, ks(h)], sts[i]) for i, (bi, h) in enumerate(streams)]
    os_ = [inters[i] + _dot(atts[i], v[bi][:, vs(h)]) for i, (bi, h) in enumerate(streams)]
    for i, (bi, h) in enumerate(streams):
        s_ref[bi, h] = sts[i] * e_last[bi][:, ks(h)] + _dot_tn(v[bi][:, vs(h)], kd[bi][:, ks(h)])
    for bi in range(NB):
        og = og_ref[bi]
        outs = []
        for h in range(GLA_HEADS):
            gate = og[:, vs(h)]
            outs.append(_rms(os_[bi * GLA_HEADS + h], onorm) * (gate * _sigmoid(gate)))
        o_ref[bi] = jnp.concatenate(outs, axis=1).astype(o_ref.dtype)


GLA_BATCH_ROWS = 4


def _gla(qkvo, gd, gate_up_pad, gate_b, onorm, B, S):
    C = GLA_CHUNK
    KW = GLA_HEADS * GLA_DK
    VW = GLA_HEADS * GLA_DV
    x3 = qkvo.reshape(B, S, qkvo.shape[-1])
    gd3 = gd.reshape(B, S, LANES)
    nb = GLA_BATCH_ROWS if B % GLA_BATCH_ROWS == 0 else 1
    out = pl.pallas_call(
        _gla_kernel,
        grid=(B // nb, S // C),
        in_specs=[pl.BlockSpec((nb, C, KW), lambda b, c: (b, c, 0)),
                  pl.BlockSpec((nb, C, KW), lambda b, c: (b, c, 1)),
                  pl.BlockSpec((nb, C, VW), lambda b, c: (b, c, 1)),
                  pl.BlockSpec((nb, C, VW), lambda b, c: (b, c, 2)),
                  pl.BlockSpec((nb, C, LANES), lambda b, c: (b, c, 0)),
                  pl.BlockSpec((LANES, KW), lambda b, c: (0, 0)),
                  pl.BlockSpec((1, KW), lambda b, c: (0, 0)),
                  pl.BlockSpec((1, GLA_DV), lambda b, c: (0, 0))],
        out_specs=pl.BlockSpec((nb, C, VW), lambda b, c: (b, c, 0)),
        out_shape=jax.ShapeDtypeStruct((B, S, VW), BF16),
        scratch_shapes=[pltpu.VMEM((nb, GLA_HEADS, GLA_DV, GLA_DK), F32)],
        compiler_params=_cparams(2),
        name="gla",
    )(x3, x3, x3, x3, gd3, gate_up_pad, gate_b.reshape(1, KW), onorm.reshape(1, GLA_DV))
    return out.reshape(B * S, VW)


def _xattn_kernel(*refs, n_in):
    x_ref = refs[0]
    a_refs = refs[1:1 + n_in]
    w_refs = refs[1 + n_in:1 + 2 * n_in]
    g_ref, wq_ref, mk_ref, mv_ref, wo_ref, o_ref = refs[1 + 2 * n_in:]
    x = x_ref[...]
    for a_ref, w_ref in zip(a_refs, w_refs):
        x = x + jnp.dot(a_ref[...], w_ref[...], preferred_element_type=F32)
    h = _rms(x, g_ref[...])
    q = _dot(h, wq_ref[...]).astype(BF16)
    mk = mk_ref[...]
    mv = mv_ref[...]
    sls = [slice(hd * XA_HEAD_DIM, (hd + 1) * XA_HEAD_DIM) for hd in range(XA_HEADS)]
    scores = [_dot_nt(q[:, sl], mk[:, sl]) for sl in sls]
    probs = []
    for s in scores:
        s = s * (XA_HEAD_DIM ** -0.5)
        p = jnp.exp(s - jnp.max(s, axis=-1, keepdims=True))
        probs.append((p / jnp.sum(p, axis=-1, keepdims=True)).astype(BF16))
    o = jnp.concatenate([_dot(p, mv[:, sl]) for p, sl in zip(probs, sls)], axis=1)
    o_ref[...] = x + _dot(o, wo_ref[...])


def _mix_proj_xattn(x, acts, weights, g, wq, mk, mv, wo, B, S, tq=256):
    D = x.shape[-1]
    M = mk.shape[0] // B
    XW = mk.shape[-1]
    n_in = len(acts)
    seq3 = lambda a: a.reshape(B, S, a.shape[-1])
    row_spec = lambda a: pl.BlockSpec((None, tq, a.shape[-1]), lambda b, n: (b, n, 0))
    const = lambda a: pl.BlockSpec(a.shape, lambda b, n: (0,) * a.ndim)
    out = pl.pallas_call(
        functools.partial(_xattn_kernel, n_in=n_in),
        grid=(B, S // tq),
        in_specs=[row_spec(x)] + [row_spec(a) for a in acts] + [const(w) for w in weights]
                 + [pl.BlockSpec((1, D), lambda b, n: (0, 0)),
                    pl.BlockSpec((D, XW), lambda b, n: (0, 0)),
                    pl.BlockSpec((None, M, XW), lambda b, n: (b, 0, 0)),
                    pl.BlockSpec((None, M, XW), lambda b, n: (b, 0, 0)),
                    pl.BlockSpec((XW, D), lambda b, n: (0, 0))],
        out_specs=pl.BlockSpec((None, tq, D), lambda b, n: (b, n, 0)),
        out_shape=jax.ShapeDtypeStruct((B, S, D), F32),
        compiler_params=_cparams(2),
        name="xattn",
    )(seq3(x), *[seq3(a) for a in acts], *weights, g.reshape(1, D), wq,
      mk.reshape(B, M, XW), mv.reshape(B, M, XW), wo)
    return out.reshape(B * S, D)


ROUTER_ROWS = 40


def _router_kernel(x_ref, g_ref, wt_ref, bt_ref, info_ref, slot_ref, cnt_ref, carry_ref):
    i = pl.program_id(0)

    @pl.when(i == 0)
    def _():
        carry_ref[...] = jnp.zeros_like(carry_ref)

    h = _rms(x_ref[...], g_ref[...])
    tm = h.shape[0]
    logits = (_dot_nt(wt_ref[...], h) + bt_ref[...])[:ROUTER_ROWS]
    row = lax.broadcasted_iota(I32, logits.shape, 0)
    big = jnp.int32(LANES)
    neg = -jnp.inf
    gl = jnp.where(row < MOE_GROUPS, logits, neg)
    gmax = jnp.max(gl, axis=0, keepdims=True)
    g_top = jnp.min(jnp.where(gl == gmax, row, big), axis=0, keepdims=True)
    p_group = 1.0 / jnp.sum(jnp.exp(gl - gmax), axis=0, keepdims=True)
    lo = MOE_GROUPS + MOE_EXPERTS_PER_GROUP * g_top
    in_group = jnp.where(row >= lo, jnp.where(row < lo + MOE_EXPERTS_PER_GROUP, 1, 0), 0) > 0
    el = jnp.where(in_group, logits, neg)
    emax = jnp.max(el, axis=0, keepdims=True)
    ee = jnp.exp(el - emax)
    prob = ee / jnp.sum(ee, axis=0, keepdims=True)
    prob = jnp.where(in_group, prob, -1.0)
    p1 = jnp.max(prob, axis=0, keepdims=True)
    i1 = jnp.min(jnp.where(prob == p1, row, big), axis=0, keepdims=True)
    rest = jnp.where(row == i1, -1.0, prob)
    p2 = jnp.max(rest, axis=0, keepdims=True)
    i2 = jnp.min(jnp.where(rest == p2, row, big), axis=0, keepdims=True)
    tot = p1 + p2
    g1 = p_group * p1 / tot
    g2 = p_group * p2 / tot
    oh = jnp.concatenate([jnp.where(row == i1, 1.0, 0.0), jnp.where(row == i2, 1.0, 0.0)], axis=0)
    tr = lax.broadcasted_iota(I32, (tm, tm), 0)
    tc = lax.broadcasted_iota(I32, (tm, tm), 1)
    pre = _dot(oh, jnp.where(tr < tc, 1.0, 0.0))
    tots = _dot(oh, jnp.ones((tm, LANES), F32))
    reps = tm // LANES
    carry = carry_ref[...]
    base1 = jnp.concatenate([carry] * reps, axis=1)
    base2 = jnp.concatenate([carry + tots[:ROUTER_ROWS]] * reps, axis=1)
    r1 = jnp.sum(oh[:ROUTER_ROWS] * (base1 + pre[:ROUTER_ROWS]), axis=0, keepdims=True)
    r2 = jnp.sum(oh[ROUTER_ROWS:] * (base2 + pre[ROUTER_ROWS:]), axis=0, keepdims=True)
    carry = carry + tots[:ROUTER_ROWS] + tots[ROUTER_ROWS:]
    carry_ref[...] = carry
    cnt_ref[...] = carry
    e1 = (i1 - MOE_GROUPS).astype(F32)
    e2 = (i2 - MOE_GROUPS).astype(F32)
    slot_rows = [e1, e2, r1, r2, g1, g2]
    rows8 = lax.broadcasted_iota(I32, (SUBLANES, tm), 0)
    slot = jnp.zeros((SUBLANES, tm), F32)
    for j, val in enumerate(slot_rows):
        slot = jnp.where(rows8 == j, val, slot)
    slot_ref[...] = slot
    wide = jnp.concatenate([slot, jnp.zeros((LANES - SUBLANES, tm), F32)], axis=0)
    info_ref[...] = jnp.transpose(wide)


def _router(x, g, wt_router, bt_router, tm=256):
    T, D = x.shape
    NT = T // tm
    return pl.pallas_call(
        _router_kernel,
        grid=(NT,),
        in_specs=[pl.BlockSpec((tm, D), lambda i: (i, 0)),
                  pl.BlockSpec((1, D), lambda i: (0, 0)),
                  pl.BlockSpec((LANES, D), lambda i: (0, 0)),
                  pl.BlockSpec((LANES, tm), lambda i: (0, 0))],
        out_specs=[pl.BlockSpec((tm, LANES), lambda i: (i, 0)),
                   pl.BlockSpec((None, SUBLANES, tm), lambda i: (i, 0, 0)),
                   pl.BlockSpec((ROUTER_ROWS, LANES), lambda i: (0, 0))],
        out_shape=[jax.ShapeDtypeStruct((T, LANES), F32),
                   jax.ShapeDtypeStruct((NT, SUBLANES, tm), F32),
                   jax.ShapeDtypeStruct((ROUTER_ROWS, LANES), F32)],
        scratch_shapes=[pltpu.VMEM((ROUTER_ROWS, LANES), F32)],
        compiler_params=_cparams(1),
        name="router",
    )(x, g.reshape(1, D), wt_router, bt_router)


def _row_bytes_wait(hbm, buf, sem):
    pltpu.make_async_copy(buf, hbm.at[pl.ds(0, buf.shape[0]), :], sem).wait()


def _to_row_tiles(ref, val):
    n = val.shape[0]
    for c in range(SUBLANES):
        ref[pl.ds(c, n, stride=SUBLANES), :] = val[:, c * LANES:(c + 1) * LANES]


def _from_row_tiles(ref):
    n = ref.shape[0] // SUBLANES
    return jnp.concatenate([ref[pl.ds(c, n, stride=SUBLANES), :] for c in range(SUBLANES)], axis=1)


def _moe_dispatch_kernel(pends_ref, cnt_ref, dest_ref, x_ref, g_ref, hs_hbm, hbuf, zbuf, sems, zsem, *, td):
    i = pl.program_id(0)
    nt = pl.num_programs(0)
    slot = lax.rem(i, 2)

    @pl.when(i == 0)
    def _():
        zbuf[...] = jnp.zeros_like(zbuf)
        for e in range(MOE_EXPERTS):
            @pl.when(cnt_ref[e] > 0)
            def _():
                start = pl.multiple_of((pends_ref[e] - MOE_BLOCK) * SUBLANES, MOE_BLOCK)
                pltpu.make_async_copy(zbuf, hs_hbm.at[pl.ds(start, MOE_BLOCK * SUBLANES), :], zsem).start()
        for e in range(MOE_EXPERTS):
            @pl.when(cnt_ref[e] > 0)
            def _():
                pltpu.make_async_copy(zbuf, hs_hbm.at[pl.ds(0, MOE_BLOCK * SUBLANES), :], zsem).wait()

        first_unused = pends_ref[MOE_EXPERTS - 1] // MOE_BLOCK
        n_blocks = hs_hbm.shape[0] // (MOE_BLOCK * SUBLANES)

        def zero_start(blk, carry):
            start = pl.multiple_of(blk * (MOE_BLOCK * SUBLANES), MOE_BLOCK)
            pltpu.make_async_copy(zbuf, hs_hbm.at[pl.ds(start, MOE_BLOCK * SUBLANES), :], zsem).start()
            return carry

        def zero_wait(blk, carry):
            pltpu.make_async_copy(zbuf, hs_hbm.at[pl.ds(0, MOE_BLOCK * SUBLANES), :], zsem).wait()
            return carry

        lax.fori_loop(first_unused, n_blocks, zero_start, 0)
        lax.fori_loop(first_unused, n_blocks, zero_wait, 0)

    hb = hbuf.at[slot]
    _to_row_tiles(hb, _rms(x_ref[...], g_ref[...]))
    for j in range(td):
        for c in range(2):
            row = pl.multiple_of(dest_ref[0, c * td + j] * SUBLANES, SUBLANES)
            pltpu.make_async_copy(hb.at[pl.ds(j * SUBLANES, SUBLANES), :],
                                  hs_hbm.at[pl.ds(row, SUBLANES), :],
                                  sems.at[slot]).start(priority=c)

    @pl.when(i > 0)
    def _():
        other = hbuf.at[1 - slot]
        _row_bytes_wait(hs_hbm, other, sems.at[1 - slot])
        _row_bytes_wait(hs_hbm, other, sems.at[1 - slot])

    @pl.when(i == nt - 1)
    def _():
        _row_bytes_wait(hs_hbm, hb, sems.at[slot])
        _row_bytes_wait(hs_hbm, hb, sems.at[slot])


def _moe_dispatch(x, g, pends, counts, dest3, P, td):
    T, D = x.shape
    assert D == ROW_TILE
    grid_spec = pltpu.PrefetchScalarGridSpec(
        num_scalar_prefetch=2,
        grid=(T // td,),
        in_specs=[pl.BlockSpec((None, 1, 2 * td), lambda i, pe, cn: (i, 0, 0), memory_space=pltpu.SMEM),
                  pl.BlockSpec((td, D), lambda i, pe, cn: (i, 0)),
                  pl.BlockSpec((1, D), lambda i, pe, cn: (0, 0))],
        out_specs=pl.BlockSpec(memory_space=pl.ANY),
        scratch_shapes=[pltpu.VMEM((2, td * SUBLANES, LANES), F32),
                        pltpu.VMEM((MOE_BLOCK * SUBLANES, LANES), F32),
                        pltpu.SemaphoreType.DMA((2,)),
                        pltpu.SemaphoreType.DMA(())],
    )
    return pl.pallas_call(
        functools.partial(_moe_dispatch_kernel, td=td),
        grid_spec=grid_spec,
        out_shape=jax.ShapeDtypeStruct((P * SUBLANES, LANES), F32),
        compiler_params=_cparams(1),
        name="moe_dispatch",
    )(pends, counts, dest3, x, g.reshape(1, D))


def _moe_expert_kernel(be_ref, nu_ref, hs_ref, w1_ref, w3_ref, w2_ref, o_ref, w1b, w3b, w2b):
    i = pl.program_id(0)
    used = i < nu_ref[0]
    changed = jnp.logical_or(i == 0, be_ref[i] != be_ref[jnp.maximum(i - 1, 0)])

    @pl.when(jnp.logical_and(used, changed))
    def _():
        w1b[...] = w1_ref[...].astype(BF16)
        w3b[...] = w3_ref[...].astype(BF16)
        w2b[...] = w2_ref[...].astype(BF16)

    @pl.when(used)
    def _():
        xe = _from_row_tiles(hs_ref).astype(BF16)
        ff = w1b.shape[1]
        halves = [slice(0, ff // 2), slice(ff // 2, ff)]
        ups = [(jnp.dot(xe, w1b[:, sl], preferred_element_type=F32),
                jnp.dot(xe, w3b[:, sl], preferred_element_type=F32)) for sl in halves]
        act = [(a * _sigmoid(a) * b).astype(BF16) for a, b in ups]
        y = sum(jnp.dot(a, w2b[sl, :], preferred_element_type=F32) for a, sl in zip(act, halves))
        _to_row_tiles(o_ref, y)

    @pl.when(jnp.logical_not(used))
    def _():
        o_ref[...] = jnp.zeros_like(o_ref)


def _moe_experts(hs, block_e, n_used, w1, w3, w2, layer):
    P = hs.shape[0] // SUBLANES
    D = ROW_TILE
    FF = w1.shape[-1]
    NB = P // MOE_BLOCK
    last = lambda i, nu: jnp.minimum(i, nu[0] - 1)
    grid_spec = pltpu.PrefetchScalarGridSpec(
        num_scalar_prefetch=2,
        grid=(NB,),
        in_specs=[pl.BlockSpec((MOE_BLOCK * SUBLANES, LANES), lambda i, be, nu: (last(i, nu), 0)),
                  pl.BlockSpec((None, None, D, FF), lambda i, be, nu: (layer, be[last(i, nu)], 0, 0)),
                  pl.BlockSpec((None, None, D, FF), lambda i, be, nu: (layer, be[last(i, nu)], 0, 0)),
                  pl.BlockSpec((None, None, FF, D), lambda i, be, nu: (layer, be[last(i, nu)], 0, 0))],
        out_specs=pl.BlockSpec((MOE_BLOCK * SUBLANES, LANES), lambda i, be, nu: (i, 0)),
        scratch_shapes=[pltpu.VMEM((D, FF), BF16),
                        pltpu.VMEM((D, FF), BF16),
                        pltpu.VMEM((FF, D), BF16)],
    )
    return pl.pallas_call(
        _moe_expert_kernel,
        grid_spec=grid_spec,
        out_shape=jax.ShapeDtypeStruct((P * SUBLANES, LANES), F32),
        compiler_params=_cparams(1),
        name="moe_experts",
    )(block_e, n_used, hs, w1, w3, w2)


def _gather_rows(src_hbm, idx_ref, dst_ref, sem, n_rows):
    for r in range(n_rows):
        row = pl.multiple_of(idx_ref[0, r] * SUBLANES, SUBLANES)
        pltpu.make_async_copy(src_hbm.at[pl.ds(row, SUBLANES), :],
                              dst_ref.at[pl.ds(r * SUBLANES, SUBLANES), :], sem).start(priority=r % 2)


def _moe_combine_kernel(pos_ref, posn_ref, x_ref, info_ref, yb_hbm, gf_ref, o_ref, ybuf, sems, *, tc, final_norm):
    i = pl.program_id(0)
    nb = pl.num_programs(0)
    slot = lax.rem(i, 2)

    @pl.when(i == 0)
    def _():
        def issue(r, carry):
            src = pl.multiple_of(pos_ref[0, r] * SUBLANES, SUBLANES)
            dst = pl.multiple_of(r * SUBLANES, SUBLANES)
            pltpu.make_async_copy(yb_hbm.at[pl.ds(src, SUBLANES), :],
                                  ybuf.at[0, pl.ds(dst, SUBLANES), :], sems.at[0]).start()
            return carry
        lax.fori_loop(0, 2 * tc, issue, 0)

    @pl.when(i + 1 < nb)
    def _():
        _gather_rows(yb_hbm, posn_ref, ybuf.at[1 - slot], sems.at[1 - slot], 2 * tc)

    pltpu.make_async_copy(yb_hbm.at[pl.ds(0, 2 * tc * SUBLANES), :], ybuf.at[slot], sems.at[slot]).wait()
    info = info_ref[...]
    yb = ybuf.at[slot]
    y0 = _from_row_tiles(yb.at[pl.ds(0, tc * SUBLANES), :])
    y1 = _from_row_tiles(yb.at[pl.ds(tc * SUBLANES, tc * SUBLANES), :])
    out = x_ref[...] + (y0 * info[:, 4:5] + y1 * info[:, 5:6])
    if final_norm:
        out = _rms(out, gf_ref[...])
    o_ref[...] = out


def _moe_combine(x, info, dest3, yb, g_final, final_norm, tc):
    T, D = x.shape
    NT = T // tc
    return pl.pallas_call(
        functools.partial(_moe_combine_kernel, tc=tc, final_norm=final_norm),
        grid=(NT,),
        in_specs=[pl.BlockSpec((None, 1, 2 * tc), lambda i: (i, 0, 0), memory_space=pltpu.SMEM),
                  pl.BlockSpec((None, 1, 2 * tc), lambda i: (jnp.minimum(i + 1, NT - 1), 0, 0),
                               memory_space=pltpu.SMEM),
                  pl.BlockSpec((tc, D), lambda i: (i, 0)),
                  pl.BlockSpec((tc, LANES), lambda i: (i, 0)),
                  pl.BlockSpec(memory_space=pl.ANY),
                  pl.BlockSpec((1, D), lambda i: (0, 0))],
        out_specs=pl.BlockSpec((tc, D), lambda i: (i, 0)),
        out_shape=jax.ShapeDtypeStruct((T, D), F32),
        scratch_shapes=[pltpu.VMEM((2, 2 * tc * SUBLANES, LANES), F32), pltpu.SemaphoreType.DMA((2,))],
        compiler_params=_cparams(1),
        name="moe_combine",
    )(dest3, dest3, x, info, yb, g_final.reshape(1, D))


MOE_TILE = 128


def _moe_layer(x, g, w_group, b_group, w_expert, b_expert, w1, w3, w2, layer, g_final, final_norm):
    T, D = x.shape
    n_log = MOE_GROUPS + MOE_EXPERTS
    tm = 2 * MOE_TILE
    wt_router = jnp.zeros((LANES, D), F32).at[:MOE_GROUPS].set(w_group.T).at[MOE_GROUPS:n_log].set(w_expert.T)
    bt_router = jnp.zeros((LANES,), F32).at[:MOE_GROUPS].set(b_group).at[MOE_GROUPS:n_log].set(b_expert)
    info, slot, cnt = _router(x, g, wt_router.astype(BF16), jnp.broadcast_to(bt_router[:, None], (LANES, tm)), tm)
    P = 2 * T + MOE_EXPERTS * MOE_BLOCK
    NB = P // MOE_BLOCK
    counts = cnt[MOE_GROUPS:n_log, 0].astype(I32)
    padded = (counts + MOE_BLOCK - 1) // MOE_BLOCK * MOE_BLOCK
    pends = jnp.cumsum(padded).astype(I32)
    pstarts = pends - padded
    block_start = jnp.arange(NB, dtype=I32) * MOE_BLOCK
    block_e = jnp.minimum(jnp.sum((pends[None, :] <= block_start[:, None]).astype(I32), axis=1),
                          MOE_EXPERTS - 1).astype(I32)
    n_used = (pends[-1:] // MOE_BLOCK).astype(I32)
    NT = T // MOE_TILE
    eid = slot[:, 0:2, :].astype(I32)
    expert_ids = jnp.arange(MOE_EXPERTS, dtype=I32)
    seg_start = jnp.sum(jnp.where(eid[..., None] == expert_ids, pstarts, 0), axis=-1)
    dest = seg_start + slot[:, 2:4, :].astype(I32)
    dest3 = jnp.concatenate([dest[:, 0, :].reshape(NT, 1, MOE_TILE), dest[:, 1, :].reshape(NT, 1, MOE_TILE)], axis=2)
    hs = _moe_dispatch(x, g, pends, counts, dest3, P, MOE_TILE)
    yb = _moe_experts(hs, block_e, n_used, w1, w3, w2, layer)
    return _moe_combine(x, info, dest3, yb, g_final, final_norm, MOE_TILE)


def kernel(x, mem, norm_mix, norm_xattn, norm_moe, norm_final, ev_w_in, ev_sinks, ev_mu, ev_w0, ev_w2, ev_a0, ev_a2, ev_g2, ev_k_k, ev_k_a, ev_r_k, ev_lnx_w, ev_lnx_b, ev_w_out, od_w_in, od_gate_up, od_gate_b, od_onorm, od_w_out, mem_norm, mem_wk, mem_wv, xa_wq, xa_wo, moe_w_group, moe_b_group, moe_w_expert, moe_b_expert, moe_w1, moe_w3, moe_w2):
    B, S, D = x.shape
    M = mem.shape[1]
    T = B * S
    depth = norm_mix.shape[0]
    xf = x.reshape(T, D)

    XW = XA_HEADS * XA_HEAD_DIM
    w_kv = jnp.concatenate([mem_wk, mem_wv], axis=1).astype(BF16)
    mk, mv = _norm_matmul(mem.reshape(B * M, D), mem_norm, w_kv, (XW, XW), (BF16, BF16))

    for layer in range(depth):
        i = layer // 2
        if layer % 2 == 0:
            swa_cols = SWA_Q_HEADS * HEAD_DIM + 2 * (SWA_Q_HEADS // SWA_GROUP) * HEAD_DIM
            rw_cols = ev_w_in.shape[-1] - swa_cols
            qkv, p_rw = _norm_matmul(xf, norm_mix[layer], ev_w_in[i].astype(BF16),
                                     (swa_cols, rw_cols), (F32, F32))
            o_a = _swa(qkv, ev_sinks[i], B, S)
            r, lw, k, v, a, b, g = _rwkv_prep(p_rw, ev_mu[i], ev_w0[i], ev_w2[i], ev_a0[i], ev_a2[i],
                                              ev_g2[i], ev_k_k[i], ev_k_a[i], B, S)
            o_b = _rwkv_scan(r, lw, k, v, a, b, g, ev_r_k[i].reshape(-1), ev_lnx_w[i], ev_lnx_b[i])
            w_out = ev_w_out[i].astype(BF16)
            qw = o_a.shape[-1]
            mix_acts, mix_ws = [o_a, o_b], [w_out[:qw], w_out[qw:]]
        else:
            KW = GLA_HEADS * GLA_DK
            VW = GLA_HEADS * GLA_DV
            R = od_gate_up.shape[1]
            w = od_w_in[i]
            w_re = jnp.concatenate([w[:, :2 * KW + VW], w[:, 2 * KW + VW + R:],
                                    w[:, 2 * KW + VW:2 * KW + VW + R],
                                    jnp.zeros((D, LANES - R), F32)], axis=1).astype(BF16)
            qkvo, gd = _norm_matmul(xf, norm_mix[layer], w_re, (2 * KW + 2 * VW, LANES), (F32, F32))
            gup = jnp.zeros((LANES, KW), F32).at[:R].set(od_gate_up[i]).astype(BF16)
            o = _gla(qkvo, gd, gup, od_gate_b[i], od_onorm[i], B, S)
            mix_acts, mix_ws = [o], [od_w_out[i].astype(BF16)]
        xf = _mix_proj_xattn(xf, mix_acts, mix_ws, norm_xattn[layer], xa_wq[layer].astype(BF16), mk, mv,
                             xa_wo[layer].astype(BF16), B, S)
        xf = _moe_layer(xf, norm_moe[layer], moe_w_group[layer], moe_b_group[layer], moe_w_expert[layer],
                        moe_b_expert[layer], moe_w1, moe_w3, moe_w2, layer,
                        norm_final, layer == depth - 1)
    return xf.reshape(B, S, D)
```

```python
import functools

import jax
import jax.numpy as jnp
from jax import lax
from jax.experimental import pallas as pl
from jax.experimental.pallas import tpu as pltpu

F32 = jnp.float32
BF16 = jnp.bfloat16
I32 = jnp.int32

EPS = 1e-6
HEAD_DIM = 64
SWA_WINDOW = 128
SWA_Q_HEADS = 8
SWA_GROUP = 4
RWKV_HEADS = 8
RWKV_WIDTH = 512
RWKV_LN_EPS = 64e-5
RWKV_CHUNK = 64
GLA_HEADS = 4
GLA_DK = 128
GLA_DV = 256
GLA_CHUNK = 64
GLA_GATE_NORM = 16.0
XA_HEADS = 4
XA_HEAD_DIM = 128
MOE_GROUPS = 4
MOE_EXPERTS_PER_GROUP = 8
MOE_EXPERTS = 32
MOE_BLOCK = 512
LANES = 128
SUBLANES = 8
ROW_TILE = SUBLANES * LANES

VMEM_LIMIT_BYTES = 48 * 1024 * 1024


def _cparams(n_axes):
    return pltpu.CompilerParams(dimension_semantics=("arbitrary",) * n_axes,
                                vmem_limit_bytes=VMEM_LIMIT_BYTES)


def _dot(a, b):
    return jnp.dot(a.astype(BF16), b.astype(BF16), preferred_element_type=F32)


def _dot_nt(a, b):
    return lax.dot_general(a.astype(BF16), b.astype(BF16), (((1,), (1,)), ((), ())),
                           preferred_element_type=F32)


def _dot_tn(a, b):
    return lax.dot_general(a.astype(BF16), b.astype(BF16), (((0,), (0,)), ((), ())),
                           preferred_element_type=F32)


def _dot_f32(a, b):
    return jnp.dot(a, b, preferred_element_type=F32, precision=lax.Precision.HIGHEST)


def _rms(x, g):
    ms = jnp.mean(x * x, axis=-1, keepdims=True)
    return x * lax.rsqrt(ms + EPS) * g


def _sigmoid(x):
    return 1.0 / (1.0 + jnp.exp(-x))


def _softplus(x):
    return jnp.maximum(x, 0.0) + jnp.log(1.0 + jnp.exp(-jnp.abs(x)))


def _norm_matmul_kernel(x_ref, g_ref, w_ref, *o_refs, splits):
    h = _rms(x_ref[...], g_ref[...]).astype(BF16)
    off = 0
    for o_ref, n in zip(o_refs, splits):
        o_ref[...] = jnp.dot(h, w_ref[:, off:off + n], preferred_element_type=F32).astype(o_ref.dtype)
        off += n


def _norm_matmul(x, g, w, splits, out_dtypes, tm=256):
    T, D = x.shape
    N = w.shape[1]
    assert sum(splits) == N and T % tm == 0
    return pl.pallas_call(
        functools.partial(_norm_matmul_kernel, splits=tuple(splits)),
        grid=(T // tm,),
        in_specs=[pl.BlockSpec((tm, D), lambda i: (i, 0)),
                  pl.BlockSpec((1, D), lambda i: (0, 0)),
                  pl.BlockSpec((D, N), lambda i: (0, 0))],
        out_specs=[pl.BlockSpec((tm, n), lambda i: (i, 0)) for n in splits],
        out_shape=[jax.ShapeDtypeStruct((T, n), dt) for n, dt in zip(splits, out_dtypes)],
        compiler_params=_cparams(1),
        name="norm_matmul",
    )(x, g.reshape(1, D), w)


def _swa_kernel(sinks_ref, q_ref, kp_ref, kc_ref, vp_ref, vc_ref, o_ref):
    n = pl.program_id(1)
    W = SWA_WINDOW
    q = q_ref[...]
    k = jnp.concatenate([kp_ref[...], kc_ref[...]], axis=0)
    v = jnp.concatenate([vp_ref[...], vc_ref[...]], axis=0)
    qpos = lax.broadcasted_iota(I32, (W, 2 * W), 0) + W
    kpos = lax.broadcasted_iota(I32, (W, 2 * W), 1)
    rel = qpos - kpos
    in_window = jnp.where(rel >= 0, jnp.where(rel < W, 1, 0), 0)
    has_prev = jnp.where(n > 0, 1, 0)
    valid = (in_window * jnp.where(kpos >= W, 1, has_prev)) > 0
    n_groups = SWA_Q_HEADS // SWA_GROUP
    qb = q.astype(BF16)
    kb = k.astype(BF16)
    vb = v.astype(BF16)
    scores = []
    for g in range(n_groups):
        qg = jnp.concatenate([qb[:, h * HEAD_DIM:(h + 1) * HEAD_DIM]
                              for h in range(g * SWA_GROUP, (g + 1) * SWA_GROUP)], axis=0)
        scores.append(_dot_nt(qg, kb[:, g * HEAD_DIM:(g + 1) * HEAD_DIM]))
    probs = []
    for g in range(n_groups):
        pieces = []
        for j in range(SWA_GROUP):
            s = jnp.where(valid, scores[g][j * W:(j + 1) * W] * (HEAD_DIM ** -0.5), -jnp.inf)
            sink = sinks_ref[g * SWA_GROUP + j]
            m = jnp.maximum(jnp.max(s, axis=-1, keepdims=True), sink)
            p = jnp.exp(s - m)
            den = jnp.sum(p, axis=-1, keepdims=True) + jnp.exp(sink - m)
            pieces.append((p / den).astype(BF16))
        probs.append(jnp.concatenate(pieces, axis=0))
    outs = []
    for g in range(n_groups):
        og = _dot(probs[g], vb[:, g * HEAD_DIM:(g + 1) * HEAD_DIM])
        outs += [og[j * W:(j + 1) * W] for j in range(SWA_GROUP)]
    o_ref[...] = jnp.concatenate(outs, axis=1).astype(o_ref.dtype)


def _swa(qkv, sinks, B, S):
    W = SWA_WINDOW
    qkv3 = qkv.reshape(B, S, qkv.shape[-1])
    qw = SWA_Q_HEADS * HEAD_DIM
    kw = qw // SWA_GROUP
    kcol = qw // kw
    out = pl.pallas_call(
        _swa_kernel,
        grid=(B, S // W),
        in_specs=[pl.BlockSpec(memory_space=pltpu.SMEM),
                  pl.BlockSpec((None, W, qw), lambda b, n: (b, n, 0)),
                  pl.BlockSpec((None, W, kw), lambda b, n: (b, jnp.maximum(n - 1, 0), kcol)),
                  pl.BlockSpec((None, W, kw), lambda b, n: (b, n, kcol)),
                  pl.BlockSpec((None, W, kw), lambda b, n: (b, jnp.maximum(n - 1, 0), kcol + 1)),
                  pl.BlockSpec((None, W, kw), lambda b, n: (b, n, kcol + 1))],
        out_specs=pl.BlockSpec((None, W, qw), lambda b, n: (b, n, 0)),
        out_shape=jax.ShapeDtypeStruct((B, S, qw), BF16),
        compiler_params=_cparams(2),
        name="swa",
    )(sinks, qkv3, qkv3, qkv3, qkv3, qkv3)
    return out.reshape(B * S, qw)


def _rwkv_prep_kernel(p_ref, pprev_ref, mu_ref, w0_ref, w2_ref, a0_ref, a2_ref, g2_ref, kk_ref, ka_ref,
                      r_out, lw_out, k_out, v_out, a_out, b_out, g_out):
    n = pl.program_id(1)
    C = RWKV_WIDTH
    p = p_ref[...]
    last = jnp.where(n > 0, pprev_ref[7:8, :], 0.0)
    row = lax.broadcasted_iota(I32, p.shape, 0)
    p_prev = jnp.where(row == 0, last, pltpu.roll(p, 1, axis=0))
    p = p + (p_prev - p) * mu_ref[...]
    r = p[:, :C]
    k = p[:, C:2 * C]
    v = p[:, 2 * C:3 * C]
    xw = p[:, 3 * C:3 * C + 64]
    xa = p[:, 3 * C + 64:3 * C + 128]
    xg = p[:, 3 * C + 128:]
    w = -_softplus(-(w0_ref[...] + _dot(jnp.tanh(xw), w2_ref[...]))) - 0.5
    lw = -jnp.exp(w)
    a = _sigmoid(a0_ref[...] + _dot(xa, a2_ref[...]))
    g = _dot(_sigmoid(xg), g2_ref[...])
    kk = k * kk_ref[...]
    pieces = []
    for h in range(RWKV_HEADS):
        kh = kk[:, h * HEAD_DIM:(h + 1) * HEAD_DIM]
        nrm = jnp.sqrt(jnp.sum(kh * kh, axis=-1, keepdims=True))
        pieces.append(kh / jnp.maximum(nrm, 1e-12))
    kk = jnp.concatenate(pieces, axis=1)
    r_out[...] = r
    lw_out[...] = lw
    k_out[...] = k * (1.0 + (a - 1.0) * ka_ref[...])
    v_out[...] = v
    a_out[...] = -kk
    b_out[...] = kk * a
    g_out[...] = g


def _rwkv_prep(p, mu, w0, w2, a0, a2, g2, k_k, k_a, B, S, tt=256):
    C = RWKV_WIDTH
    PW = p.shape[-1]
    p3 = p.reshape(B, S, PW)
    row = lambda t: t.reshape(1, -1)
    full = lambda arr: pl.BlockSpec(arr.shape, lambda b, n: (0,) * arr.ndim)
    params = [row(mu), row(w0), w2.astype(BF16), row(a0), a2.astype(BF16), g2.astype(BF16), row(k_k), row(k_a)]
    outs = pl.pallas_call(
        _rwkv_prep_kernel,
        grid=(B, S // tt),
        in_specs=[pl.BlockSpec((None, tt, PW), lambda b, n: (b, n, 0)),
                  pl.BlockSpec((None, 8, PW), lambda b, n: (b, jnp.maximum(n * (tt // 8) - 1, 0), 0))]
                 + [full(t) for t in params],
        out_specs=[pl.BlockSpec((None, tt, C), lambda b, n: (b, n, 0))] * 7,
        out_shape=[jax.ShapeDtypeStruct((B, S, C), F32)] * 7,
        compiler_params=_cparams(2),
        name="rwkv_prep",
    )(p3, p3, *params)
    return outs


def _pair_blockdiag(x):
    lane = lax.broadcasted_iota(I32, x.shape, 1)
    zero = jnp.zeros_like(x)
    return jnp.concatenate([jnp.where(lane < HEAD_DIM, x, zero), jnp.where(lane >= HEAD_DIM, x, zero)], axis=0)


def _rwkv_scan_kernel(r_ref, lw_ref, k_ref, v_ref, a_ref, b_ref, g_ref, rk_ref, lnw_ref, lnb_ref,
                      o_ref, s_ref):
    c = pl.program_id(1)

    @pl.when(c == 0)
    def _():
        s_ref[...] = jnp.zeros_like(s_ref)

    C = RWKV_CHUNK
    NB = r_ref.shape[0]
    NP = RWKV_HEADS // 2
    PW = 2 * HEAD_DIM
    row = lax.broadcasted_iota(I32, (C, C), 0)
    col = lax.broadcasted_iota(I32, (C, C), 1)
    tri = jnp.where(row >= col, 1.0, 0.0).astype(F32)
    rowp = lax.broadcasted_iota(I32, (C, PW), 0)
    colp = lax.broadcasted_iota(I32, (C, PW), 1)
    colp = jnp.where(colp >= HEAD_DIM, colp - HEAD_DIM, colp)
    lower_p = rowp >= colp
    strict_p = rowp > colp
    rows = lax.broadcasted_iota(I32, (PW, PW), 0)
    cols = lax.broadcasted_iota(I32, (PW, PW), 1)
    same_head = jnp.where(rows >= HEAD_DIM, 1, 0) == jnp.where(cols >= HEAD_DIM, 1, 0)
    first = lax.broadcasted_iota(I32, (C, PW), 1) < HEAD_DIM

    streams = [(bi, p) for bi in range(NB) for p in range(NP)]
    pre = []
    for bi in range(NB):
        lw = lw_ref[bi]
        cum = _dot_f32(tri, lw)
        cum_last = cum[C - 1:C, :]
        r = r_ref[bi]
        k = k_ref[bi]
        v = v_ref[bi]
        a = a_ref[bi]
        b = b_ref[bi]
        e_neg = jnp.exp(-cum)
        e_rem = jnp.exp(cum_last - cum)
        pre.append(dict(
            r_t=(r * jnp.exp(cum)).astype(BF16), a_t=(a * jnp.exp(cum - lw)).astype(BF16),
            b_t=(b * e_neg).astype(BF16), k_t=(k * e_neg).astype(BF16),
            b_d=(b * e_rem).astype(BF16), k_d=(k * e_rem).astype(BF16),
            v_b=v.astype(BF16), v=v, e_last=jnp.exp(cum_last), rkk=r * k * rk_ref[...], g=g_ref[bi]))

    def lanes(p):
        return slice(p * PW, (p + 1) * PW)

    ar = [jnp.concatenate([pre[bi]['a_t'][:, lanes(p)], pre[bi]['r_t'][:, lanes(p)]], axis=0) for bi, p in streams]
    s0 = [s_ref[bi, p] for bi, p in streams]
    m_b = [_dot_nt(ar[i], _pair_blockdiag(pre[bi]['b_t'][:, lanes(p)])) for i, (bi, p) in enumerate(streams)]
    m_k = [_dot_nt(ar[i], _pair_blockdiag(pre[bi]['k_t'][:, lanes(p)])) for i, (bi, p) in enumerate(streams)]
    ars = [_dot_nt(ar[i], s0[i]) for i in range(len(streams))]
    v_p = [pre[bi]['v_b'][:, lanes(p)] for bi, p in streams]
    v_bd = [_pair_blockdiag(vp) for vp in v_p]
    x = [ars[i][:C] + _dot(jnp.where(strict_p, m_k[i][:C], 0.0), v_bd[i]) for i in range(len(streams))]
    pw = [jnp.where(strict_p, m_b[i][:C], 0.0).astype(BF16) for i in range(len(streams))]
    x = [x[i] + _dot(pw[i], _pair_blockdiag(x[i].astype(BF16))) for i in range(len(streams))]
    for _ in range(5):
        pw = [_dot(pw[i], _pair_blockdiag(pw[i])).astype(BF16) for i in range(len(streams))]
        x = [x[i] + _dot(pw[i], _pair_blockdiag(x[i].astype(BF16))) for i in range(len(streams))]
    u_b = [xi.astype(BF16) for xi in x]
    y = [ars[i][C:]
         + _dot(jnp.concatenate([jnp.where(lower_p, m_b[i][C:], 0.0), jnp.where(lower_p, m_k[i][C:], 0.0)], axis=1),
                jnp.concatenate([_pair_blockdiag(u_b[i]), v_bd[i]], axis=0))
         for i in range(len(streams))]
    for i, (bi, p) in enumerate(streams):
        upd = _dot_tn(jnp.concatenate([u_b[i], v_p[i]], axis=0),
                      jnp.concatenate([pre[bi]['b_d'][:, lanes(p)], pre[bi]['k_d'][:, lanes(p)]], axis=0))
        s_ref[bi, p] = s0[i] * pre[bi]['e_last'][:, lanes(p)] + jnp.where(same_head, upd, 0.0)

    lnw = lnw_ref[...]
    lnb = lnb_ref[...]

    def head_sum(t):
        s1 = jnp.sum(jnp.where(first, t, 0.0), axis=-1, keepdims=True)
        s2 = jnp.sum(jnp.where(first, 0.0, t), axis=-1, keepdims=True)
        return jnp.where(first, s1, s2)

    for bi in range(NB):
        outs = []
        for p in range(NP):
            yi = y[bi * NP + p]
            mean = head_sum(yi) * (1.0 / HEAD_DIM)
            yc = yi - mean
            var = head_sum(yc * yc) * (1.0 / HEAD_DIM)
            yn = yc * lax.rsqrt(var + RWKV_LN_EPS) * lnw[:, lanes(p)] + lnb[:, lanes(p)]
            bonus = head_sum(pre[bi]['rkk'][:, lanes(p)]) * pre[bi]['v'][:, lanes(p)]
            outs.append((yn + bonus) * pre[bi]['g'][:, lanes(p)])
        o_ref[bi] = jnp.concatenate(outs, axis=1).astype(o_ref.dtype)


RWKV_BATCH_ROWS = 4


def _rwkv_scan(r, lw, k, v, a, b, g, r_k, lnx_w, lnx_b):
    B, S, W = r.shape
    C = RWKV_CHUNK
    nb = RWKV_BATCH_ROWS if B % RWKV_BATCH_ROWS == 0 else 1
    seq = pl.BlockSpec((nb, C, W), lambda bb, c: (bb, c, 0))
    par = pl.BlockSpec((1, W), lambda bb, c: (0, 0))
    out = pl.pallas_call(
        _rwkv_scan_kernel,
        grid=(B // nb, S // C),
        in_specs=[seq] * 7 + [par] * 3,
        out_specs=seq,
        out_shape=jax.ShapeDtypeStruct((B, S, W), BF16),
        scratch_shapes=[pltpu.VMEM((nb, RWKV_HEADS // 2, 2 * HEAD_DIM, 2 * HEAD_DIM), F32)],
        compiler_params=_cparams(2),
        name="rwkv_scan",
    )(r, lw, k, v, a, b, g, r_k.reshape(1, W), lnx_w.reshape(1, W), lnx_b.reshape(1, W))
    return out.reshape(B * S, W)


def _gla_kernel(q_ref, k_ref, v_ref, og_ref, gd_ref, gup_ref, gb_ref, on_ref, o_ref, s_ref):
    c = pl.program_id(1)

    @pl.when(c == 0)
    def _():
        s_ref[...] = jnp.zeros_like(s_ref)

    C = GLA_CHUNK
    NB = q_ref.shape[0]
    row = lax.broadcasted_iota(I32, (C, C), 0)
    col = lax.broadcasted_iota(I32, (C, C), 1)
    lower = row >= col
    tri = jnp.where(lower, 1.0, 0.0).astype(F32)
    onorm = on_ref[...]
    zs = [_dot(gd_ref[bi], gup_ref[...]) + gb_ref[...] for bi in range(NB)]
    cums = [_dot_f32(tri, -_softplus(-z) / GLA_GATE_NORM) for z in zs]
    qe, ke, kd, e_last, v = [], [], [], [], []
    for bi in range(NB):
        cum = cums[bi]
        cum_last = cum[C - 1:C, :]
        k = k_ref[bi]
        qe.append((q_ref[bi] * (GLA_DK ** -0.5) * jnp.exp(cum)).astype(BF16))
        ke.append((k * jnp.exp(-cum)).astype(BF16))
        kd.append((k * jnp.exp(cum_last - cum)).astype(BF16))
        e_last.append(jnp.exp(cum_last))
        v.append(v_ref[bi].astype(BF16))
    streams = [(bi, h) for bi in range(NB) for h in range(GLA_HEADS)]
    ks = lambda h: slice(h * GLA_DK, (h + 1) * GLA_DK)
    vs = lambda h: slice(h * GLA_DV, (h + 1) * GLA_DV)
    sts = [s_ref[bi, h] for bi, h in streams]
    atts = [jnp.where(lower, _dot_nt(qe[bi][:, ks(h)], ke[bi][:, ks(h)]), 0.0) for bi, h in streams]
    inters = [_dot_nt(qe[bi][:, ks(h)], sts[i]) for i, (bi, h) in enumerate(streams)]
    os_ = [inters[i] + _dot(atts[i], v[bi][:, vs(h)]) for i, (bi, h) in enumerate(streams)]
    for i, (bi, h) in enumerate(streams):
        s_ref[bi, h] = sts[i] * e_last[bi][:, ks(h)] + _dot_tn(v[bi][:, vs(h)], kd[bi][:, ks(h)])
    for bi in range(NB):
        og = og_ref[bi]
        outs = []
        for h in range(GLA_HEADS):
            gate = og[:, vs(h)]
            outs.append(_rms(os_[bi * GLA_HEADS + h], onorm) * (gate * _sigmoid(gate)))
        o_ref[bi] = jnp.concatenate(outs, axis=1).astype(o_ref.dtype)


GLA_BATCH_ROWS = 4


def _gla(qkvo, gd, gate_up_pad, gate_b, onorm, B, S):
    C = GLA_CHUNK
    KW = GLA_HEADS * GLA_DK
    VW = GLA_HEADS * GLA_DV
    x3 = qkvo.reshape(B, S, qkvo.shape[-1])
    gd3 = gd.reshape(B, S, LANES)
    nb = GLA_BATCH_ROWS if B % GLA_BATCH_ROWS == 0 else 1
    out = pl.pallas_call(
        _gla_kernel,
        grid=(B // nb, S // C),
        in_specs=[pl.BlockSpec((nb, C, KW), lambda b, c: (b, c, 0)),
                  pl.BlockSpec((nb, C, KW), lambda b, c: (b, c, 1)),
                  pl.BlockSpec((nb, C, VW), lambda b, c: (b, c, 1)),
                  pl.BlockSpec((nb, C, VW), lambda b, c: (b, c, 2)),
                  pl.BlockSpec((nb, C, LANES), lambda b, c: (b, c, 0)),
                  pl.BlockSpec((LANES, KW), lambda b, c: (0, 0)),
                  pl.BlockSpec((1, KW), lambda b, c: (0, 0)),
                  pl.BlockSpec((1, GLA_DV), lambda b, c: (0, 0))],
        out_specs=pl.BlockSpec((nb, C, VW), lambda b, c: (b, c, 0)),
        out_shape=jax.ShapeDtypeStruct((B, S, VW), BF16),
        scratch_shapes=[pltpu.VMEM((nb, GLA_HEADS, GLA_DV, GLA_DK), F32)],
        compiler_params=_cparams(2),
        name="gla",
    )(x3, x3, x3, x3, gd3, gate_up_pad, gate_b.reshape(1, KW), onorm.reshape(1, GLA_DV))
    return out.reshape(B * S, VW)


def _xattn_kernel(*refs, n_in):
    x_ref = refs[0]
    a_refs = refs[1:1 + n_in]
    w_refs = refs[1 + n_in:1 + 2 * n_in]
    g_ref, wq_ref, mk_ref, mv_ref, wo_ref, o_ref = refs[1 + 2 * n_in:]
    x = x_ref[...]
    for a_ref, w_ref in zip(a_refs, w_refs):
        x = x + jnp.dot(a_ref[...], w_ref[...], preferred_element_type=F32)
    h = _rms(x, g_ref[...])
    q = _dot(h, wq_ref[...]).astype(BF16)
    mk = mk_ref[...]
    mv = mv_ref[...]
    sls = [slice(hd * XA_HEAD_DIM, (hd + 1) * XA_HEAD_DIM) for hd in range(XA_HEADS)]
    scores = [_dot_nt(q[:, sl], mk[:, sl]) for sl in sls]
    probs = []
    for s in scores:
        s = s * (XA_HEAD_DIM ** -0.5)
        p = jnp.exp(s - jnp.max(s, axis=-1, keepdims=True))
        probs.append((p / jnp.sum(p, axis=-1, keepdims=True)).astype(BF16))
    o = jnp.concatenate([_dot(p, mv[:, sl]) for p, sl in zip(probs, sls)], axis=1)
    o_ref[...] = x + _dot(o, wo_ref[...])


def _mix_proj_xattn(x, acts, weights, g, wq, mk, mv, wo, B, S, tq=256):
    D = x.shape[-1]
    M = mk.shape[0] // B
    XW = mk.shape[-1]
    n_in = len(acts)
    seq3 = lambda a: a.reshape(B, S, a.shape[-1])
    row_spec = lambda a: pl.BlockSpec((None, tq, a.shape[-1]), lambda b, n: (b, n, 0))
    const = lambda a: pl.BlockSpec(a.shape, lambda b, n: (0,) * a.ndim)
    out = pl.pallas_call(
        functools.partial(_xattn_kernel, n_in=n_in),
        grid=(B, S // tq),
        in_specs=[row_spec(x)] + [row_spec(a) for a in acts] + [const(w) for w in weights]
                 + [pl.BlockSpec((1, D), lambda b, n: (0, 0)),
                    pl.BlockSpec((D, XW), lambda b, n: (0, 0)),
                    pl.BlockSpec((None, M, XW), lambda b, n: (b, 0, 0)),
                    pl.BlockSpec((None, M, XW), lambda b, n: (b, 0, 0)),
                    pl.BlockSpec((XW, D), lambda b, n: (0, 0))],
        out_specs=pl.BlockSpec((None, tq, D), lambda b, n: (b, n, 0)),
        out_shape=jax.ShapeDtypeStruct((B, S, D), F32),
        compiler_params=_cparams(2),
        name="xattn",
    )(seq3(x), *[seq3(a) for a in acts], *weights, g.reshape(1, D), wq,
      mk.reshape(B, M, XW), mv.reshape(B, M, XW), wo)
    return out.reshape(B * S, D)


ROUTER_ROWS = 40


def _router_kernel(x_ref, g_ref, wt_ref, bt_ref, info_ref, slot_ref, cnt_ref, carry_ref):
    i = pl.program_id(0)

    @pl.when(i == 0)
    def _():
        carry_ref[...] = jnp.zeros_like(carry_ref)

    h = _rms(x_ref[...], g_ref[...])
    tm = h.shape[0]
    logits = (_dot_nt(wt_ref[...], h) + bt_ref[...])[:ROUTER_ROWS]
    row = lax.broadcasted_iota(I32, logits.shape, 0)
    big = jnp.int32(LANES)
    neg = -jnp.inf
    gl = jnp.where(row < MOE_GROUPS, logits, neg)
    gmax = jnp.max(gl, axis=0, keepdims=True)
    g_top = jnp.min(jnp.where(gl == gmax, row, big), axis=0, keepdims=True)
    p_group = 1.0 / jnp.sum(jnp.exp(gl - gmax), axis=0, keepdims=True)
    lo = MOE_GROUPS + MOE_EXPERTS_PER_GROUP * g_top
    in_group = jnp.where(row >= lo, jnp.where(row < lo + MOE_EXPERTS_PER_GROUP, 1, 0), 0) > 0
    el = jnp.where(in_group, logits, neg)
    emax = jnp.max(el, axis=0, keepdims=True)
    ee = jnp.exp(el - emax)
    prob = ee / jnp.sum(ee, axis=0, keepdims=True)
    prob = jnp.where(in_group, prob, -1.0)
    p1 = jnp.max(prob, axis=0, keepdims=True)
    i1 = jnp.min(jnp.where(prob == p1, row, big), axis=0, keepdims=True)
    rest = jnp.where(row == i1, -1.0, prob)
    p2 = jnp.max(rest, axis=0, keepdims=True)
    i2 = jnp.min(jnp.where(rest == p2, row, big), axis=0, keepdims=True)
    tot = p1 + p2
    g1 = p_group * p1 / tot
    g2 = p_group * p2 / tot
    oh = jnp.concatenate([jnp.where(row == i1, 1.0, 0.0), jnp.where(row == i2, 1.0, 0.0)], axis=0)
    tr = lax.broadcasted_iota(I32, (tm, tm), 0)
    tc = lax.broadcasted_iota(I32, (tm, tm), 1)
    pre = _dot(oh, jnp.where(tr < tc, 1.0, 0.0))
    tots = _dot(oh, jnp.ones((tm, LANES), F32))
    reps = tm // LANES
    carry = carry_ref[...]
    base1 = jnp.concatenate([carry] * reps, axis=1)
    base2 = jnp.concatenate([carry + tots[:ROUTER_ROWS]] * reps, axis=1)
    r1 = jnp.sum(oh[:ROUTER_ROWS] * (base1 + pre[:ROUTER_ROWS]), axis=0, keepdims=True)
    r2 = jnp.sum(oh[ROUTER_ROWS:] * (base2 + pre[ROUTER_ROWS:]), axis=0, keepdims=True)
    carry = carry + tots[:ROUTER_ROWS] + tots[ROUTER_ROWS:]
    carry_ref[...] = carry
    cnt_ref[...] = carry
    e1 = (i1 - MOE_GROUPS).astype(F32)
    e2 = (i2 - MOE_GROUPS).astype(F32)
    slot_rows = [e1, e2, r1, r2, g1, g2]
    rows8 = lax.broadcasted_iota(I32, (SUBLANES, tm), 0)
    slot = jnp.zeros((SUBLANES, tm), F32)
    for j, val in enumerate(slot_rows):
        slot = jnp.where(rows8 == j, val, slot)
    slot_ref[...] = slot
    wide = jnp.concatenate([slot, jnp.zeros((LANES - SUBLANES, tm), F32)], axis=0)
    info_ref[...] = jnp.transpose(wide)


def _router(x, g, wt_router, bt_router, tm=256):
    T, D = x.shape
    NT = T // tm
    return pl.pallas_call(
        _router_kernel,
        grid=(NT,),
        in_specs=[pl.BlockSpec((tm, D), lambda i: (i, 0)),
                  pl.BlockSpec((1, D), lambda i: (0, 0)),
                  pl.BlockSpec((LANES, D), lambda i: (0, 0)),
                  pl.BlockSpec((LANES, tm), lambda i: (0, 0))],
        out_specs=[pl.BlockSpec((tm, LANES), lambda i: (i, 0)),
                   pl.BlockSpec((None, SUBLANES, tm), lambda i: (i, 0, 0)),
                   pl.BlockSpec((ROUTER_ROWS, LANES), lambda i: (0, 0))],
        out_shape=[jax.ShapeDtypeStruct((T, LANES), F32),
                   jax.ShapeDtypeStruct((NT, SUBLANES, tm), F32),
                   jax.ShapeDtypeStruct((ROUTER_ROWS, LANES), F32)],
        scratch_shapes=[pltpu.VMEM((ROUTER_ROWS, LANES), F32)],
        compiler_params=_cparams(1),
        name="router",
    )(x, g.reshape(1, D), wt_router, bt_router)


def _row_bytes_wait(hbm, buf, sem):
    pltpu.make_async_copy(buf, hbm.at[pl.ds(0, buf.shape[0]), :], sem).wait()


def _to_row_tiles(ref, val):
    n = val.shape[0]
    for c in range(SUBLANES):
        ref[pl.ds(c, n, stride=SUBLANES), :] = val[:, c * LANES:(c + 1) * LANES]


def _from_row_tiles(ref):
    n = ref.shape[0] // SUBLANES
    return jnp.concatenate([ref[pl.ds(c, n, stride=SUBLANES), :] for c in range(SUBLANES)], axis=1)


def _moe_dispatch_kernel(pends_ref, cnt_ref, dest_ref, x_ref, g_ref, hs_hbm, hbuf, zbuf, sems, zsem, *, td):
    i = pl.program_id(0)
    nt = pl.num_programs(0)
    slot = lax.rem(i, 2)

    @pl.when(i == 0)
    def _():
        zbuf[...] = jnp.zeros_like(zbuf)
        for e in range(MOE_EXPERTS):
            @pl.when(cnt_ref[e] > 0)
            def _():
                start = pl.multiple_of((pends_ref[e] - MOE_BLOCK) * SUBLANES, MOE_BLOCK)
                pltpu.make_async_copy(zbuf, hs_hbm.at[pl.ds(start, MOE_BLOCK * SUBLANES), :], zsem).start()
        for e in range(MOE_EXPERTS):
            @pl.when(cnt_ref[e] > 0)
            def _():
                pltpu.make_async_copy(zbuf, hs_hbm.at[pl.ds(0, MOE_BLOCK * SUBLANES), :], zsem).wait()

        first_unused = pends_ref[MOE_EXPERTS - 1] // MOE_BLOCK
        n_blocks = hs_hbm.shape[0] // (MOE_BLOCK * SUBLANES)

        def zero_start(blk, carry):
            start = pl.multiple_of(blk * (MOE_BLOCK * SUBLANES), MOE_BLOCK)
            pltpu.make_async_copy(zbuf, hs_hbm.at[pl.ds(start, MOE_BLOCK * SUBLANES), :], zsem).start()
            return carry

        def zero_wait(blk, carry):
            pltpu.make_async_copy(zbuf, hs_hbm.at[pl.ds(0, MOE_BLOCK * SUBLANES), :], zsem).wait()
            return carry

        lax.fori_loop(first_unused, n_blocks, zero_start, 0)
        lax.fori_loop(first_unused, n_blocks, zero_wait, 0)

    hb = hbuf.at[slot]
    _to_row_tiles(hb, _rms(x_ref[...], g_ref[...]))
    for j in range(td):
        for c in range(2):
            row = pl.multiple_of(dest_ref[0, c * td + j] * SUBLANES, SUBLANES)
            pltpu.make_async_copy(hb.at[pl.ds(j * SUBLANES, SUBLANES), :],
                                  hs_hbm.at[pl.ds(row, SUBLANES), :],
                                  sems.at[slot]).start(priority=c)

    @pl.when(i > 0)
    def _():
        other = hbuf.at[1 - slot]
        _row_bytes_wait(hs_hbm, other, sems.at[1 - slot])
        _row_bytes_wait(hs_hbm, other, sems.at[1 - slot])

    @pl.when(i == nt - 1)
    def _():
        _row_bytes_wait(hs_hbm, hb, sems.at[slot])
        _row_bytes_wait(hs_hbm, hb, sems.at[slot])


def _moe_dispatch(x, g, pends, counts, dest3, P, td):
    T, D = x.shape
    assert D == ROW_TILE
    grid_spec = pltpu.PrefetchScalarGridSpec(
        num_scalar_prefetch=2,
        grid=(T // td,),
        in_specs=[pl.BlockSpec((None, 1, 2 * td), lambda i, pe, cn: (i, 0, 0), memory_space=pltpu.SMEM),
                  pl.BlockSpec((td, D), lambda i, pe, cn: (i, 0)),
                  pl.BlockSpec((1, D), lambda i, pe, cn: (0, 0))],
        out_specs=pl.BlockSpec(memory_space=pl.ANY),
        scratch_shapes=[pltpu.VMEM((2, td * SUBLANES, LANES), F32),
                        pltpu.VMEM((MOE_BLOCK * SUBLANES, LANES), F32),
                        pltpu.SemaphoreType.DMA((2,)),
                        pltpu.SemaphoreType.DMA(())],
    )
    return pl.pallas_call(
        functools.partial(_moe_dispatch_kernel, td=td),
        grid_spec=grid_spec,
        out_shape=jax.ShapeDtypeStruct((P * SUBLANES, LANES), F32),
        compiler_params=_cparams(1),
        name="moe_dispatch",
    )(pends, counts, dest3, x, g.reshape(1, D))


def _moe_expert_kernel(be_ref, nu_ref, hs_ref, w1_ref, w3_ref, w2_ref, o_ref, w1b, w3b, w2b):
    i = pl.program_id(0)
    used = i < nu_ref[0]
    changed = jnp.logical_or(i == 0, be_ref[i] != be_ref[jnp.maximum(i - 1, 0)])

    @pl.when(jnp.logical_and(used, changed))
    def _():
        w1b[...] = w1_ref[...].astype(BF16)
        w3b[...] = w3_ref[...].astype(BF16)
        w2b[...] = w2_ref[...].astype(BF16)

    @pl.when(used)
    def _():
        xe = _from_row_tiles(hs_ref).astype(BF16)
        ff = w1b.shape[1]
        halves = [slice(0, ff // 2), slice(ff // 2, ff)]
        ups = [(jnp.dot(xe, w1b[:, sl], preferred_element_type=F32),
                jnp.dot(xe, w3b[:, sl], preferred_element_type=F32)) for sl in halves]
        act = [(a * _sigmoid(a) * b).astype(BF16) for a, b in ups]
        y = sum(jnp.dot(a, w2b[sl, :], preferred_element_type=F32) for a, sl in zip(act, halves))
        _to_row_tiles(o_ref, y)

    @pl.when(jnp.logical_not(used))
    def _():
        o_ref[...] = jnp.zeros_like(o_ref)


def _moe_experts(hs, block_e, n_used, w1, w3, w2, layer):
    P = hs.shape[0] // SUBLANES
    D = ROW_TILE
    FF = w1.shape[-1]
    NB = P // MOE_BLOCK
    last = lambda i, nu: jnp.minimum(i, nu[0] - 1)
    grid_spec = pltpu.PrefetchScalarGridSpec(
        num_scalar_prefetch=2,
        grid=(NB,),
        in_specs=[pl.BlockSpec((MOE_BLOCK * SUBLANES, LANES), lambda i, be, nu: (last(i, nu), 0)),
                  pl.BlockSpec((None, None, D, FF), lambda i, be, nu: (layer, be[last(i, nu)], 0, 0)),
                  pl.BlockSpec((None, None, D, FF), lambda i, be, nu: (layer, be[last(i, nu)], 0, 0)),
                  pl.BlockSpec((None, None, FF, D), lambda i, be, nu: (layer, be[last(i, nu)], 0, 0))],
        out_specs=pl.BlockSpec((MOE_BLOCK * SUBLANES, LANES), lambda i, be, nu: (i, 0)),
        scratch_shapes=[pltpu.VMEM((D, FF), BF16),
                        pltpu.VMEM((D, FF), BF16),
                        pltpu.VMEM((FF, D), BF16)],
    )
    return pl.pallas_call(
        _moe_expert_kernel,
        grid_spec=grid_spec,
        out_shape=jax.ShapeDtypeStruct((P * SUBLANES, LANES), F32),
        compiler_params=_cparams(1),
        name="moe_experts",
    )(block_e, n_used, hs, w1, w3, w2)


def _gather_rows(src_hbm, idx_ref, dst_ref, sem, n_rows):
    for r in range(n_rows):
        row = pl.multiple_of(idx_ref[0, r] * SUBLANES, SUBLANES)
        pltpu.make_async_copy(src_hbm.at[pl.ds(row, SUBLANES), :],
                              dst_ref.at[pl.ds(r * SUBLANES, SUBLANES), :], sem).start(priority=r % 2)


def _moe_combine_kernel(pos_ref, posn_ref, x_ref, info_ref, yb_hbm, g_ref, *rest, tc, final_norm, splits):
    if splits:
        w_ref, o_ref = rest[0], rest[1]
        p_refs = rest[2:2 + len(splits)]
        ybuf, sems = rest[2 + len(splits):]
    else:
        o_ref, ybuf, sems = rest
    i = pl.program_id(0)
    nb = pl.num_programs(0)
    slot = lax.rem(i, 2)

    @pl.when(i == 0)
    def _():
        def issue(r, carry):
            src = pl.multiple_of(pos_ref[0, r] * SUBLANES, SUBLANES)
            dst = pl.multiple_of(r * SUBLANES, SUBLANES)
            pltpu.make_async_copy(yb_hbm.at[pl.ds(src, SUBLANES), :],
                                  ybuf.at[0, pl.ds(dst, SUBLANES), :], sems.at[0]).start()
            return carry
        lax.fori_loop(0, 2 * tc, issue, 0)

    @pl.when(i + 1 < nb)
    def _():
        _gather_rows(yb_hbm, posn_ref, ybuf.at[1 - slot], sems.at[1 - slot], 2 * tc)

    pltpu.make_async_copy(yb_hbm.at[pl.ds(0, 2 * tc * SUBLANES), :], ybuf.at[slot], sems.at[slot]).wait()
    info = info_ref[...]
    yb = ybuf.at[slot]
    y0 = _from_row_tiles(yb.at[pl.ds(0, tc * SUBLANES), :])
    y1 = _from_row_tiles(yb.at[pl.ds(tc * SUBLANES, tc * SUBLANES), :])
    out = x_ref[...] + (y0 * info[:, 4:5] + y1 * info[:, 5:6])
    if final_norm:
        out = _rms(out, g_ref[...])
    o_ref[...] = out
    if splits:
        h = _rms(out, g_ref[...]).astype(BF16)
        off = 0
        for p_ref, n in zip(p_refs, splits):
            p_ref[...] = jnp.dot(h, w_ref[:, off:off + n], preferred_element_type=F32).astype(p_ref.dtype)
            off += n


def _moe_combine(x, info, dest3, yb, g, final_norm, tc, next_proj=None):
    T, D = x.shape
    NT = T // tc
    in_specs = [pl.BlockSpec((None, 1, 2 * tc), lambda i: (i, 0, 0), memory_space=pltpu.SMEM),
                pl.BlockSpec((None, 1, 2 * tc), lambda i: (jnp.minimum(i + 1, NT - 1), 0, 0),
                             memory_space=pltpu.SMEM),
                pl.BlockSpec((tc, D), lambda i: (i, 0)),
                pl.BlockSpec((tc, LANES), lambda i: (i, 0)),
                pl.BlockSpec(memory_space=pl.ANY),
                pl.BlockSpec((1, D), lambda i: (0, 0))]
    out_specs = [pl.BlockSpec((tc, D), lambda i: (i, 0))]
    out_shape = [jax.ShapeDtypeStruct((T, D), F32)]
    args = [dest3, dest3, x, info, yb, g.reshape(1, D)]
    splits = ()
    if next_proj is not None:
        w, splits, out_dtypes = next_proj
        assert not final_norm and sum(splits) == w.shape[1]
        in_specs.append(pl.BlockSpec(w.shape, lambda i: (0, 0)))
        args.append(w)
        out_specs += [pl.BlockSpec((tc, n), lambda i: (i, 0)) for n in splits]
        out_shape += [jax.ShapeDtypeStruct((T, n), dt) for n, dt in zip(splits, out_dtypes)]
    outs = pl.pallas_call(
        functools.partial(_moe_combine_kernel, tc=tc, final_norm=final_norm, splits=tuple(splits)),
        grid=(NT,),
        in_specs=in_specs,
        out_specs=out_specs,
        out_shape=out_shape,
        scratch_shapes=[pltpu.VMEM((2, 2 * tc * SUBLANES, LANES), F32), pltpu.SemaphoreType.DMA((2,))],
        compiler_params=_cparams(1),
        name="moe_combine",
    )(*args)
    return outs[0], tuple(outs[1:])


MOE_TILE = 128
MOE_COMBINE_TILE = 256


def _tile_slots(dest, tile):
    n_tiles = dest.shape[0] * dest.shape[2] // tile
    return jnp.concatenate([dest[:, 0, :].reshape(n_tiles, 1, tile), dest[:, 1, :].reshape(n_tiles, 1, tile)], axis=2)


def _moe_layer(x, g, w_group, b_group, w_expert, b_expert, w1, w3, w2, layer, g_out, final_norm, next_proj):
    T, D = x.shape
    n_log = MOE_GROUPS + MOE_EXPERTS
    tm = 2 * MOE_TILE
    wt_router = jnp.zeros((LANES, D), F32).at[:MOE_GROUPS].set(w_group.T).at[MOE_GROUPS:n_log].set(w_expert.T)
    bt_router = jnp.zeros((LANES,), F32).at[:MOE_GROUPS].set(b_group).at[MOE_GROUPS:n_log].set(b_expert)
    info, slot, cnt = _router(x, g, wt_router.astype(BF16), jnp.broadcast_to(bt_router[:, None], (LANES, tm)), tm)
    P = 2 * T + MOE_EXPERTS * MOE_BLOCK
    NB = P // MOE_BLOCK
    counts = cnt[MOE_GROUPS:n_log, 0].astype(I32)
    padded = (counts + MOE_BLOCK - 1) // MOE_BLOCK * MOE_BLOCK
    pends = jnp.cumsum(padded).astype(I32)
    pstarts = pends - padded
    block_start = jnp.arange(NB, dtype=I32) * MOE_BLOCK
    block_e = jnp.minimum(jnp.sum((pends[None, :] <= block_start[:, None]).astype(I32), axis=1),
                          MOE_EXPERTS - 1).astype(I32)
    n_used = (pends[-1:] // MOE_BLOCK).astype(I32)
    eid = slot[:, 0:2, :].astype(I32)
    expert_ids = jnp.arange(MOE_EXPERTS, dtype=I32)
    seg_start = jnp.sum(jnp.where(eid[..., None] == expert_ids, pstarts, 0), axis=-1)
    dest = seg_start + slot[:, 2:4, :].astype(I32)
    hs = _moe_dispatch(x, g, pends, counts, _tile_slots(dest, MOE_TILE), P, MOE_TILE)
    yb = _moe_experts(hs, block_e, n_used, w1, w3, w2, layer)
    return _moe_combine(x, info, _tile_slots(dest, MOE_COMBINE_TILE), yb, g_out, final_norm, MOE_COMBINE_TILE,
                        next_proj)


def kernel(x, mem, norm_mix, norm_xattn, norm_moe, norm_final, ev_w_in, ev_sinks, ev_mu, ev_w0, ev_w2, ev_a0, ev_a2, ev_g2, ev_k_k, ev_k_a, ev_r_k, ev_lnx_w, ev_lnx_b, ev_w_out, od_w_in, od_gate_up, od_gate_b, od_onorm, od_w_out, mem_norm, mem_wk, mem_wv, xa_wq, xa_wo, moe_w_group, moe_b_group, moe_w_expert, moe_b_expert, moe_w1, moe_w3, moe_w2):
    B, S, D = x.shape
    M = mem.shape[1]
    T = B * S
    depth = norm_mix.shape[0]
    xf = x.reshape(T, D)

    XW = XA_HEADS * XA_HEAD_DIM
    w_kv = jnp.concatenate([mem_wk, mem_wv], axis=1).astype(BF16)
    mk, mv = _norm_matmul(mem.reshape(B * M, D), mem_norm, w_kv, (XW, XW), (BF16, BF16))

    KW = GLA_HEADS * GLA_DK
    VW = GLA_HEADS * GLA_DV
    swa_cols = SWA_Q_HEADS * HEAD_DIM + 2 * (SWA_Q_HEADS // SWA_GROUP) * HEAD_DIM

    def in_proj(layer):
        i = layer // 2
        if layer % 2 == 0:
            return ev_w_in[i].astype(BF16), (swa_cols, ev_w_in.shape[-1] - swa_cols), (F32, F32)
        R = od_gate_up.shape[1]
        w = od_w_in[i]
        w_re = jnp.concatenate([w[:, :2 * KW + VW], w[:, 2 * KW + VW + R:],
                                w[:, 2 * KW + VW:2 * KW + VW + R],
                                jnp.zeros((D, LANES - R), F32)], axis=1).astype(BF16)
        return w_re, (2 * KW + 2 * VW, LANES), (F32, F32)

    w0, splits0, dts0 = in_proj(0)
    proj = _norm_matmul(xf, norm_mix[0], w0, splits0, dts0)
    for layer in range(depth):
        i = layer // 2
        if layer % 2 == 0:
            qkv, p_rw = proj
            o_a = _swa(qkv, ev_sinks[i], B, S)
            r, lw, k, v, a, b, g = _rwkv_prep(p_rw, ev_mu[i], ev_w0[i], ev_w2[i], ev_a0[i], ev_a2[i],
                                              ev_g2[i], ev_k_k[i], ev_k_a[i], B, S)
            o_b = _rwkv_scan(r, lw, k, v, a, b, g, ev_r_k[i].reshape(-1), ev_lnx_w[i], ev_lnx_b[i])
            w_out = ev_w_out[i].astype(BF16)
            qw = o_a.shape[-1]
            mix_acts, mix_ws = [o_a, o_b], [w_out[:qw], w_out[qw:]]
        else:
            qkvo, gd = proj
            R = od_gate_up.shape[1]
            gup = jnp.zeros((LANES, KW), F32).at[:R].set(od_gate_up[i]).astype(BF16)
            o = _gla(qkvo, gd, gup, od_gate_b[i], od_onorm[i], B, S)
            mix_acts, mix_ws = [o], [od_w_out[i].astype(BF16)]
        xf = _mix_proj_xattn(xf, mix_acts, mix_ws, norm_xattn[layer], xa_wq[layer].astype(BF16), mk, mv,
                             xa_wo[layer].astype(BF16), B, S)
        last = layer == depth - 1
        g_out = norm_final if last else norm_mix[layer + 1]
        xf, proj = _moe_layer(xf, norm_moe[layer], moe_w_group[layer], moe_b_group[layer], moe_w_expert[layer],
                              moe_b_expert[layer], moe_w1, moe_w3, moe_w2, layer,
                              g_out, last, None if last else in_proj(layer + 1))
    return xf.reshape(B, S, D)
```

```python
import functools

import jax
import jax.numpy as jnp
from jax import lax
from jax.experimental import pallas as pl
from jax.experimental.pallas import tpu as pltpu

F32 = jnp.float32
BF16 = jnp.bfloat16
I32 = jnp.int32

EPS = 1e-6
HEAD_DIM = 64
SWA_WINDOW = 128
SWA_Q_HEADS = 8
SWA_GROUP = 4
RWKV_HEADS = 8
RWKV_WIDTH = 512
RWKV_LN_EPS = 64e-5
RWKV_CHUNK = 64
GLA_HEADS = 4
GLA_DK = 128
GLA_DV = 256
GLA_CHUNK = 64
GLA_GATE_NORM = 16.0
XA_HEADS = 4
XA_HEAD_DIM = 128
MOE_GROUPS = 4
MOE_EXPERTS_PER_GROUP = 8
MOE_EXPERTS = 32
MOE_BLOCK = 512
LANES = 128
SUBLANES = 8
ROW_TILE = SUBLANES * LANES

VMEM_LIMIT_BYTES = 48 * 1024 * 1024


def _cparams(n_axes):
    return pltpu.CompilerParams(dimension_semantics=("arbitrary",) * n_axes,
                                vmem_limit_bytes=VMEM_LIMIT_BYTES)


def _dot(a, b):
    return jnp.dot(a.astype(BF16), b.astype(BF16), preferred_element_type=F32)


def _dot_nt(a, b):
    return lax.dot_general(a.astype(BF16), b.astype(BF16), (((1,), (1,)), ((), ())),
                           preferred_element_type=F32)


def _dot_tn(a, b):
    return lax.dot_general(a.astype(BF16), b.astype(BF16), (((0,), (0,)), ((), ())),
                           preferred_element_type=F32)


def _dot_f32(a, b):
    return jnp.dot(a, b, preferred_element_type=F32, precision=lax.Precision.HIGHEST)


def _rms(x, g):
    ms = jnp.mean(x * x, axis=-1, keepdims=True)
    return x * lax.rsqrt(ms + EPS) * g


def _sigmoid(x):
    return 1.0 / (1.0 + jnp.exp(-x))


def _softplus(x):
    return jnp.maximum(x, 0.0) + jnp.log(1.0 + jnp.exp(-jnp.abs(x)))


def _norm_matmul_kernel(x_ref, g_ref, w_ref, *o_refs, splits):
    h = _rms(x_ref[...], g_ref[...]).astype(BF16)
    off = 0
    for o_ref, n in zip(o_refs, splits):
        o_ref[...] = jnp.dot(h, w_ref[:, off:off + n], preferred_element_type=F32).astype(o_ref.dtype)
        off += n


def _norm_matmul(x, g, w, splits, out_dtypes, tm=256):
    T, D = x.shape
    N = w.shape[1]
    assert sum(splits) == N and T % tm == 0
    return pl.pallas_call(
        functools.partial(_norm_matmul_kernel, splits=tuple(splits)),
        grid=(T // tm,),
        in_specs=[pl.BlockSpec((tm, D), lambda i: (i, 0)),
                  pl.BlockSpec((1, D), lambda i: (0, 0)),
                  pl.BlockSpec((D, N), lambda i: (0, 0))],
        out_specs=[pl.BlockSpec((tm, n), lambda i: (i, 0)) for n in splits],
        out_shape=[jax.ShapeDtypeStruct((T, n), dt) for n, dt in zip(splits, out_dtypes)],
        compiler_params=_cparams(1),
        name="norm_matmul",
    )(x, g.reshape(1, D), w)


def _swa_kernel(sinks_ref, q_ref, kp_ref, kc_ref, vp_ref, vc_ref, o_ref):
    n = pl.program_id(1)
    W = SWA_WINDOW
    NB = q_ref.shape[0]
    qpos = lax.broadcasted_iota(I32, (W, 2 * W), 0) + W
    kpos = lax.broadcasted_iota(I32, (W, 2 * W), 1)
    rel = qpos - kpos
    in_window = jnp.where(rel >= 0, jnp.where(rel < W, 1, 0), 0)
    has_prev = jnp.where(n > 0, 1, 0)
    valid = (in_window * jnp.where(kpos >= W, 1, has_prev)) > 0
    n_groups = SWA_Q_HEADS // SWA_GROUP
    streams = [(bi, g) for bi in range(NB) for g in range(n_groups)]
    qb = [q_ref[bi].astype(BF16) for bi in range(NB)]
    kb = [jnp.concatenate([kp_ref[bi], kc_ref[bi]], axis=0).astype(BF16) for bi in range(NB)]
    vb = [jnp.concatenate([vp_ref[bi], vc_ref[bi]], axis=0).astype(BF16) for bi in range(NB)]
    gs = lambda g: slice(g * HEAD_DIM, (g + 1) * HEAD_DIM)
    scores = []
    for bi, g in streams:
        qg = jnp.concatenate([qb[bi][:, h * HEAD_DIM:(h + 1) * HEAD_DIM]
                              for h in range(g * SWA_GROUP, (g + 1) * SWA_GROUP)], axis=0)
        scores.append(_dot_nt(qg, kb[bi][:, gs(g)]))
    probs = []
    for i, (bi, g) in enumerate(streams):
        pieces = []
        for j in range(SWA_GROUP):
            s = jnp.where(valid, scores[i][j * W:(j + 1) * W] * (HEAD_DIM ** -0.5), -jnp.inf)
            sink = sinks_ref[g * SWA_GROUP + j]
            m = jnp.maximum(jnp.max(s, axis=-1, keepdims=True), sink)
            p = jnp.exp(s - m)
            den = jnp.sum(p, axis=-1, keepdims=True) + jnp.exp(sink - m)
            pieces.append((p / den).astype(BF16))
        probs.append(jnp.concatenate(pieces, axis=0))
    ogs = [_dot(probs[i], vb[bi][:, gs(g)]) for i, (bi, g) in enumerate(streams)]
    for bi in range(NB):
        outs = []
        for g in range(n_groups):
            og = ogs[bi * n_groups + g]
            outs += [og[j * W:(j + 1) * W] for j in range(SWA_GROUP)]
        o_ref[bi] = jnp.concatenate(outs, axis=1).astype(o_ref.dtype)


SWA_BATCH_ROWS = 2


def _swa(qkv, sinks, B, S):
    W = SWA_WINDOW
    qkv3 = qkv.reshape(B, S, qkv.shape[-1])
    qw = SWA_Q_HEADS * HEAD_DIM
    kw = qw // SWA_GROUP
    kcol = qw // kw
    nb = SWA_BATCH_ROWS if B % SWA_BATCH_ROWS == 0 else 1
    out = pl.pallas_call(
        _swa_kernel,
        grid=(B // nb, S // W),
        in_specs=[pl.BlockSpec(memory_space=pltpu.SMEM),
                  pl.BlockSpec((nb, W, qw), lambda b, n: (b, n, 0)),
                  pl.BlockSpec((nb, W, kw), lambda b, n: (b, jnp.maximum(n - 1, 0), kcol)),
                  pl.BlockSpec((nb, W, kw), lambda b, n: (b, n, kcol)),
                  pl.BlockSpec((nb, W, kw), lambda b, n: (b, jnp.maximum(n - 1, 0), kcol + 1)),
                  pl.BlockSpec((nb, W, kw), lambda b, n: (b, n, kcol + 1))],
        out_specs=pl.BlockSpec((nb, W, qw), lambda b, n: (b, n, 0)),
        out_shape=jax.ShapeDtypeStruct((B, S, qw), BF16),
        compiler_params=_cparams(2),
        name="swa",
    )(sinks, qkv3, qkv3, qkv3, qkv3, qkv3)
    return out.reshape(B * S, qw)


def _rwkv_prep_kernel(p_ref, pprev_ref, mu_ref, w0_ref, w2_ref, a0_ref, a2_ref, g2_ref, kk_ref, ka_ref,
                      r_out, lw_out, k_out, v_out, a_out, b_out, g_out):
    n = pl.program_id(1)
    C = RWKV_WIDTH
    p = p_ref[...]
    last = jnp.where(n > 0, pprev_ref[7:8, :], 0.0)
    row = lax.broadcasted_iota(I32, p.shape, 0)
    p_prev = jnp.where(row == 0, last, pltpu.roll(p, 1, axis=0))
    p = p + (p_prev - p) * mu_ref[...]
    r = p[:, :C]
    k = p[:, C:2 * C]
    v = p[:, 2 * C:3 * C]
    xw = p[:, 3 * C:3 * C + 64]
    xa = p[:, 3 * C + 64:3 * C + 128]
    xg = p[:, 3 * C + 128:]
    w = -_softplus(-(w0_ref[...] + _dot(jnp.tanh(xw), w2_ref[...]))) - 0.5
    lw = -jnp.exp(w)
    a = _sigmoid(a0_ref[...] + _dot(xa, a2_ref[...]))
    g = _dot(_sigmoid(xg), g2_ref[...])
    kk = k * kk_ref[...]
    pieces = []
    for h in range(RWKV_HEADS):
        kh = kk[:, h * HEAD_DIM:(h + 1) * HEAD_DIM]
        nrm = jnp.sqrt(jnp.sum(kh * kh, axis=-1, keepdims=True))
        pieces.append(kh / jnp.maximum(nrm, 1e-12))
    kk = jnp.concatenate(pieces, axis=1)
    r_out[...] = r
    lw_out[...] = lw
    k_out[...] = k * (1.0 + (a - 1.0) * ka_ref[...])
    v_out[...] = v
    a_out[...] = -kk
    b_out[...] = kk * a
    g_out[...] = g


def _rwkv_prep(p, mu, w0, w2, a0, a2, g2, k_k, k_a, B, S, tt=256):
    C = RWKV_WIDTH
    PW = p.shape[-1]
    p3 = p.reshape(B, S, PW)
    row = lambda t: t.reshape(1, -1)
    full = lambda arr: pl.BlockSpec(arr.shape, lambda b, n: (0,) * arr.ndim)
    params = [row(mu), row(w0), w2.astype(BF16), row(a0), a2.astype(BF16), g2.astype(BF16), row(k_k), row(k_a)]
    outs = pl.pallas_call(
        _rwkv_prep_kernel,
        grid=(B, S // tt),
        in_specs=[pl.BlockSpec((None, tt, PW), lambda b, n: (b, n, 0)),
                  pl.BlockSpec((None, 8, PW), lambda b, n: (b, jnp.maximum(n * (tt // 8) - 1, 0), 0))]
                 + [full(t) for t in params],
        out_specs=[pl.BlockSpec((None, tt, C), lambda b, n: (b, n, 0))] * 7,
        out_shape=[jax.ShapeDtypeStruct((B, S, C), F32)] * 7,
        compiler_params=_cparams(2),
        name="rwkv_prep",
    )(p3, p3, *params)
    return outs


def _pair_blockdiag(x):
    lane = lax.broadcasted_iota(I32, x.shape, 1)
    zero = jnp.zeros_like(x)
    return jnp.concatenate([jnp.where(lane < HEAD_DIM, x, zero), jnp.where(lane >= HEAD_DIM, x, zero)], axis=0)


def _rwkv_scan_kernel(r_ref, lw_ref, k_ref, v_ref, a_ref, b_ref, g_ref, rk_ref, lnw_ref, lnb_ref,
                      o_ref, s_ref):
    c = pl.program_id(1)

    @pl.when(c == 0)
    def _():
        s_ref[...] = jnp.zeros_like(s_ref)

    C = RWKV_CHUNK
    NB = r_ref.shape[0]
    NP = RWKV_HEADS // 2
    PW = 2 * HEAD_DIM
    row = lax.broadcasted_iota(I32, (C, C), 0)
    col = lax.broadcasted_iota(I32, (C, C), 1)
    tri = jnp.where(row >= col, 1.0, 0.0).astype(F32)
    rowp = lax.broadcasted_iota(I32, (C, PW), 0)
    colp = lax.broadcasted_iota(I32, (C, PW), 1)
    colp = jnp.where(colp >= HEAD_DIM, colp - HEAD_DIM, colp)
    lower_p = rowp >= colp
    strict_p = rowp > colp
    rows = lax.broadcasted_iota(I32, (PW, PW), 0)
    cols = lax.broadcasted_iota(I32, (PW, PW), 1)
    same_head = jnp.where(rows >= HEAD_DIM, 1, 0) == jnp.where(cols >= HEAD_DIM, 1, 0)
    first = lax.broadcasted_iota(I32, (C, PW), 1) < HEAD_DIM

    streams = [(bi, p) for bi in range(NB) for p in range(NP)]
    pre = []
    for bi in range(NB):
        lw = lw_ref[bi]
        cum = _dot_f32(tri, lw)
        cum_last = cum[C - 1:C, :]
        r = r_ref[bi]
        k = k_ref[bi]
        v = v_ref[bi]
        a = a_ref[bi]
        b = b_ref[bi]
        e_neg = jnp.exp(-cum)
        e_rem = jnp.exp(cum_last - cum)
        pre.append(dict(
            r_t=(r * jnp.exp(cum)).astype(BF16), a_t=(a * jnp.exp(cum - lw)).astype(BF16),
            b_t=(b * e_neg).astype(BF16), k_t=(k * e_neg).astype(BF16),
            b_d=(b * e_rem).astype(BF16), k_d=(k * e_rem).astype(BF16),
            v_b=v.astype(BF16), v=v, e_last=jnp.exp(cum_last), rkk=r * k * rk_ref[...], g=g_ref[bi]))

    def lanes(p):
        return slice(p * PW, (p + 1) * PW)

    ar = [jnp.concatenate([pre[bi]['a_t'][:, lanes(p)], pre[bi]['r_t'][:, lanes(p)]], axis=0) for bi, p in streams]
    s0 = [s_ref[bi, p] for bi, p in streams]
    m_b = [_dot_nt(ar[i], _pair_blockdiag(pre[bi]['b_t'][:, lanes(p)])) for i, (bi, p) in enumerate(streams)]
    m_k = [_dot_nt(ar[i], _pair_blockdiag(pre[bi]['k_t'][:, lanes(p)])) for i, (bi, p) in enumerate(streams)]
    ars = [_dot_nt(ar[i], s0[i]) for i in range(len(streams))]
    v_p = [pre[bi]['v_b'][:, lanes(p)] for bi, p in streams]
    v_bd = [_pair_blockdiag(vp) for vp in v_p]
    x = [ars[i][:C] + _dot(jnp.where(strict_p, m_k[i][:C], 0.0), v_bd[i]) for i in range(len(streams))]
    pw = [jnp.where(strict_p, m_b[i][:C], 0.0).astype(BF16) for i in range(len(streams))]
    x = [x[i] + _dot(pw[i], _pair_blockdiag(x[i].astype(BF16))) for i in range(len(streams))]
    for _ in range(5):
        pw = [_dot(pw[i], _pair_blockdiag(pw[i])).astype(BF16) for i in range(len(streams))]
        x = [x[i] + _dot(pw[i], _pair_blockdiag(x[i].astype(BF16))) for i in range(len(streams))]
    u_b = [xi.astype(BF16) for xi in x]
    y = [ars[i][C:]
         + _dot(jnp.concatenate([jnp.where(lower_p, m_b[i][C:], 0.0), jnp.where(lower_p, m_k[i][C:], 0.0)], axis=1),
                jnp.concatenate([_pair_blockdiag(u_b[i]), v_bd[i]], axis=0))
         for i in range(len(streams))]
    for i, (bi, p) in enumerate(streams):
        upd = _dot_tn(jnp.concatenate([u_b[i], v_p[i]], axis=0),
                      jnp.concatenate([pre[bi]['b_d'][:, lanes(p)], pre[bi]['k_d'][:, lanes(p)]], axis=0))
        s_ref[bi, p] = s0[i] * pre[bi]['e_last'][:, lanes(p)] + jnp.where(same_head, upd, 0.0)

    lnw = lnw_ref[...]
    lnb = lnb_ref[...]

    def head_sum(t):
        s1 = jnp.sum(jnp.where(first, t, 0.0), axis=-1, keepdims=True)
        s2 = jnp.sum(jnp.where(first, 0.0, t), axis=-1, keepdims=True)
        return jnp.where(first, s1, s2)

    for bi in range(NB):
        outs = []
        for p in range(NP):
            yi = y[bi * NP + p]
            mean = head_sum(yi) * (1.0 / HEAD_DIM)
            yc = yi - mean
            var = head_sum(yc * yc) * (1.0 / HEAD_DIM)
            yn = yc * lax.rsqrt(var + RWKV_LN_EPS) * lnw[:, lanes(p)] + lnb[:, lanes(p)]
            bonus = head_sum(pre[bi]['rkk'][:, lanes(p)]) * pre[bi]['v'][:, lanes(p)]
            outs.append((yn + bonus) * pre[bi]['g'][:, lanes(p)])
        o_ref[bi] = jnp.concatenate(outs, axis=1).astype(o_ref.dtype)


RWKV_BATCH_ROWS = 4


def _rwkv_scan(r, lw, k, v, a, b, g, r_k, lnx_w, lnx_b):
    B, S, W = r.shape
    C = RWKV_CHUNK
    nb = RWKV_BATCH_ROWS if B % RWKV_BATCH_ROWS == 0 else 1
    seq = pl.BlockSpec((nb, C, W), lambda bb, c: (bb, c, 0))
    par = pl.BlockSpec((1, W), lambda bb, c: (0, 0))
    out = pl.pallas_call(
        _rwkv_scan_kernel,
        grid=(B // nb, S // C),
        in_specs=[seq] * 7 + [par] * 3,
        out_specs=seq,
        out_shape=jax.ShapeDtypeStruct((B, S, W), BF16),
        scratch_shapes=[pltpu.VMEM((nb, RWKV_HEADS // 2, 2 * HEAD_DIM, 2 * HEAD_DIM), F32)],
        compiler_params=_cparams(2),
        name="rwkv_scan",
    )(r, lw, k, v, a, b, g, r_k.reshape(1, W), lnx_w.reshape(1, W), lnx_b.reshape(1, W))
    return out.reshape(B * S, W)


def _gla_kernel(q_ref, k_ref, v_ref, og_ref, gd_ref, gup_ref, gb_ref, on_ref, o_ref, s_ref):
    c = pl.program_id(1)

    @pl.when(c == 0)
    def _():
        s_ref[...] = jnp.zeros_like(s_ref)

    C = GLA_CHUNK
    NB = q_ref.shape[0]
    row = lax.broadcasted_iota(I32, (C, C), 0)
    col = lax.broadcasted_iota(I32, (C, C), 1)
    lower = row >= col
    tri = jnp.where(lower, 1.0, 0.0).astype(F32)
    onorm = on_ref[...]
    zs = [_dot(gd_ref[bi], gup_ref[...]) + gb_ref[...] for bi in range(NB)]
    cums = [_dot_f32(tri, -_softplus(-z) / GLA_GATE_NORM) for z in zs]
    qe, ke, kd, e_last, v = [], [], [], [], []
    for bi in range(NB):
        cum = cums[bi]
        cum_last = cum[C - 1:C, :]
        k = k_ref[bi]
        qe.append((q_ref[bi] * (GLA_DK ** -0.5) * jnp.exp(cum)).astype(BF16))
        ke.append((k * jnp.exp(-cum)).astype(BF16))
        kd.append((k * jnp.exp(cum_last - cum)).astype(BF16))
        e_last.append(jnp.exp(cum_last))
        v.append(v_ref[bi].astype(BF16))
    streams = [(bi, h) for bi in range(NB) for h in range(GLA_HEADS)]
    ks = lambda h: slice(h * GLA_DK, (h + 1) * GLA_DK)
    vs = lambda h: slice(h * GLA_DV, (h + 1) * GLA_DV)
    sts = [s_ref[bi, h] for bi, h in streams]
    atts = [jnp.where(lower, _dot_nt(qe[bi][:, ks(h)], ke[bi][:, ks(h)]), 0.0) for bi, h in streams]
    inters = [_dot_nt(qe[bi][:, ks(h)], sts[i]) for i, (bi, h) in enumerate(streams)]
    os_ = [inters[i] + _dot(atts[i], v[bi][:, vs(h)]) for i, (bi, h) in enumerate(streams)]
    for i, (bi, h) in enumerate(streams):
        s_ref[bi, h] = sts[i] * e_last[bi][:, ks(h)] + _dot_tn(v[bi][:, vs(h)], kd[bi][:, ks(h)])
    for bi in range(NB):
        og = og_ref[bi]
        outs = []
        for h in range(GLA_HEADS):
            gate = og[:, vs(h)]
            outs.append(_rms(os_[bi * GLA_HEADS + h], onorm) * (gate * _sigmoid(gate)))
        o_ref[bi] = jnp.concatenate(outs, axis=1).astype(o_ref.dtype)


GLA_BATCH_ROWS = 4


def _gla(qkvo, gd, gate_up_pad, gate_b, onorm, B, S):
    C = GLA_CHUNK
    KW = GLA_HEADS * GLA_DK
    VW = GLA_HEADS * GLA_DV
    x3 = qkvo.reshape(B, S, qkvo.shape[-1])
    gd3 = gd.reshape(B, S, LANES)
    nb = GLA_BATCH_ROWS if B % GLA_BATCH_ROWS == 0 else 1
    out = pl.pallas_call(
        _gla_kernel,
        grid=(B // nb, S // C),
        in_specs=[pl.BlockSpec((nb, C, KW), lambda b, c: (b, c, 0)),
                  pl.BlockSpec((nb, C, KW), lambda b, c: (b, c, 1)),
                  pl.BlockSpec((nb, C, VW), lambda b, c: (b, c, 1)),
                  pl.BlockSpec((nb, C, VW), lambda b, c: (b, c, 2)),
                  pl.BlockSpec((nb, C, LANES), lambda b, c: (b, c, 0)),
                  pl.BlockSpec((LANES, KW), lambda b, c: (0, 0)),
                  pl.BlockSpec((1, KW), lambda b, c: (0, 0)),
                  pl.BlockSpec((1, GLA_DV), lambda b, c: (0, 0))],
        out_specs=pl.BlockSpec((nb, C, VW), lambda b, c: (b, c, 0)),
        out_shape=jax.ShapeDtypeStruct((B, S, VW), BF16),
        scratch_shapes=[pltpu.VMEM((nb, GLA_HEADS, GLA_DV, GLA_DK), F32)],
        compiler_params=_cparams(2),
        name="gla",
    )(x3, x3, x3, x3, gd3, gate_up_pad, gate_b.reshape(1, KW), onorm.reshape(1, GLA_DV))
    return out.reshape(B * S, VW)


def _xattn_kernel(*refs, n_in):
    x_ref = refs[0]
    a_refs = refs[1:1 + n_in]
    w_refs = refs[1 + n_in:1 + 2 * n_in]
    g_ref, wq_ref, mk_ref, mv_ref, wo_ref, o_ref = refs[1 + 2 * n_in:]
    tq = x_ref.shape[0]
    subs = [slice(r, r + XA_SUB_ROWS) for r in range(0, tq, XA_SUB_ROWS)]
    xs = [x_ref[sub, :] for sub in subs]
    for a_ref, w_ref in zip(a_refs, w_refs):
        xs = [x + jnp.dot(a_ref[sub, :], w_ref[...], preferred_element_type=F32) for x, sub in zip(xs, subs)]
    qs = [_dot(_rms(x, g_ref[...]), wq_ref[...]).astype(BF16) for x in xs]
    mk = mk_ref[...]
    mv = mv_ref[...]
    sls = [slice(hd * XA_HEAD_DIM, (hd + 1) * XA_HEAD_DIM) for hd in range(XA_HEADS)]
    scores = [[_dot_nt(q[:, sl], mk[:, sl]) for sl in sls] for q in qs]
    probs = []
    for sc in scores:
        ps = []
        for s in sc:
            s = s * (XA_HEAD_DIM ** -0.5)
            p = jnp.exp(s - jnp.max(s, axis=-1, keepdims=True))
            ps.append((p / jnp.sum(p, axis=-1, keepdims=True)).astype(BF16))
        probs.append(ps)
    os_ = [jnp.concatenate([_dot(p, mv[:, sl]) for p, sl in zip(ps, sls)], axis=1) for ps in probs]
    for x, o, sub in zip(xs, os_, subs):
        o_ref[sub, :] = x + _dot(o, wo_ref[...])


XA_SUB_ROWS = 256


def _mix_proj_xattn(x, acts, weights, g, wq, mk, mv, wo, B, S, tq=512):
    D = x.shape[-1]
    M = mk.shape[0] // B
    XW = mk.shape[-1]
    n_in = len(acts)
    seq3 = lambda a: a.reshape(B, S, a.shape[-1])
    row_spec = lambda a: pl.BlockSpec((None, tq, a.shape[-1]), lambda b, n: (b, n, 0))
    const = lambda a: pl.BlockSpec(a.shape, lambda b, n: (0,) * a.ndim)
    out = pl.pallas_call(
        functools.partial(_xattn_kernel, n_in=n_in),
        grid=(B, S // tq),
        in_specs=[row_spec(x)] + [row_spec(a) for a in acts] + [const(w) for w in weights]
                 + [pl.BlockSpec((1, D), lambda b, n: (0, 0)),
                    pl.BlockSpec((D, XW), lambda b, n: (0, 0)),
                    pl.BlockSpec((None, M, XW), lambda b, n: (b, 0, 0)),
                    pl.BlockSpec((None, M, XW), lambda b, n: (b, 0, 0)),
                    pl.BlockSpec((XW, D), lambda b, n: (0, 0))],
        out_specs=pl.BlockSpec((None, tq, D), lambda b, n: (b, n, 0)),
        out_shape=jax.ShapeDtypeStruct((B, S, D), F32),
        compiler_params=_cparams(2),
        name="xattn",
    )(seq3(x), *[seq3(a) for a in acts], *weights, g.reshape(1, D), wq,
      mk.reshape(B, M, XW), mv.reshape(B, M, XW), wo)
    return out.reshape(B * S, D)


ROUTER_ROWS = 40


def _router_kernel(x_ref, g_ref, wt_ref, bt_ref, info_ref, slot_ref, cnt_ref, carry_ref):
    i = pl.program_id(0)

    @pl.when(i == 0)
    def _():
        carry_ref[...] = jnp.zeros_like(carry_ref)

    h = _rms(x_ref[...], g_ref[...])
    tm = h.shape[0]
    logits = (_dot_nt(wt_ref[...], h) + bt_ref[...])[:ROUTER_ROWS]
    row = lax.broadcasted_iota(I32, logits.shape, 0)
    big = jnp.int32(LANES)
    neg = -jnp.inf
    gl = jnp.where(row < MOE_GROUPS, logits, neg)
    gmax = jnp.max(gl, axis=0, keepdims=True)
    g_top = jnp.min(jnp.where(gl == gmax, row, big), axis=0, keepdims=True)
    p_group = 1.0 / jnp.sum(jnp.exp(gl - gmax), axis=0, keepdims=True)
    lo = MOE_GROUPS + MOE_EXPERTS_PER_GROUP * g_top
    in_group = jnp.where(row >= lo, jnp.where(row < lo + MOE_EXPERTS_PER_GROUP, 1, 0), 0) > 0
    el = jnp.where(in_group, logits, neg)
    emax = jnp.max(el, axis=0, keepdims=True)
    ee = jnp.exp(el - emax)
    prob = ee / jnp.sum(ee, axis=0, keepdims=True)
    prob = jnp.where(in_group, prob, -1.0)
    p1 = jnp.max(prob, axis=0, keepdims=True)
    i1 = jnp.min(jnp.where(prob == p1, row, big), axis=0, keepdims=True)
    rest = jnp.where(row == i1, -1.0, prob)
    p2 = jnp.max(rest, axis=0, keepdims=True)
    i2 = jnp.min(jnp.where(rest == p2, row, big), axis=0, keepdims=True)
    tot = p1 + p2
    g1 = p_group * p1 / tot
    g2 = p_group * p2 / tot
    oh = jnp.concatenate([jnp.where(row == i1, 1.0, 0.0), jnp.where(row == i2, 1.0, 0.0)], axis=0)
    tr = lax.broadcasted_iota(I32, (tm, tm), 0)
    tc = lax.broadcasted_iota(I32, (tm, tm), 1)
    pre = _dot(oh, jnp.where(tr < tc, 1.0, 0.0))
    tots = _dot(oh, jnp.ones((tm, LANES), F32))
    reps = tm // LANES
    carry = carry_ref[...]
    base1 = jnp.concatenate([carry] * reps, axis=1)
    base2 = jnp.concatenate([carry + tots[:ROUTER_ROWS]] * reps, axis=1)
    r1 = jnp.sum(oh[:ROUTER_ROWS] * (base1 + pre[:ROUTER_ROWS]), axis=0, keepdims=True)
    r2 = jnp.sum(oh[ROUTER_ROWS:] * (base2 + pre[ROUTER_ROWS:]), axis=0, keepdims=True)
    carry = carry + tots[:ROUTER_ROWS] + tots[ROUTER_ROWS:]
    carry_ref[...] = carry
    cnt_ref[...] = carry
    e1 = (i1 - MOE_GROUPS).astype(F32)
    e2 = (i2 - MOE_GROUPS).astype(F32)
    slot_rows = [e1, e2, r1, r2, g1, g2]
    rows8 = lax.broadcasted_iota(I32, (SUBLANES, tm), 0)
    slot = jnp.zeros((SUBLANES, tm), F32)
    for j, val in enumerate(slot_rows):
        slot = jnp.where(rows8 == j, val, slot)
    slot_ref[...] = slot
    wide = jnp.concatenate([slot, jnp.zeros((LANES - SUBLANES, tm), F32)], axis=0)
    info_ref[...] = jnp.transpose(wide)


def _router(x, g, wt_router, bt_router, tm=256):
    T, D = x.shape
    NT = T // tm
    return pl.pallas_call(
        _router_kernel,
        grid=(NT,),
        in_specs=[pl.BlockSpec((tm, D), lambda i: (i, 0)),
                  pl.BlockSpec((1, D), lambda i: (0, 0)),
                  pl.BlockSpec((LANES, D), lambda i: (0, 0)),
                  pl.BlockSpec((LANES, tm), lambda i: (0, 0))],
        out_specs=[pl.BlockSpec((tm, LANES), lambda i: (i, 0)),
                   pl.BlockSpec((None, SUBLANES, tm), lambda i: (i, 0, 0)),
                   pl.BlockSpec((ROUTER_ROWS, LANES), lambda i: (0, 0))],
        out_shape=[jax.ShapeDtypeStruct((T, LANES), F32),
                   jax.ShapeDtypeStruct((NT, SUBLANES, tm), F32),
                   jax.ShapeDtypeStruct((ROUTER_ROWS, LANES), F32)],
        scratch_shapes=[pltpu.VMEM((ROUTER_ROWS, LANES), F32)],
        compiler_params=_cparams(1),
        name="router",
    )(x, g.reshape(1, D), wt_router, bt_router)


def _row_bytes_wait(hbm, buf, sem):
    pltpu.make_async_copy(buf, hbm.at[pl.ds(0, buf.shape[0]), :], sem).wait()


def _to_row_tiles(ref, val):
    n = val.shape[0]
    for c in range(SUBLANES):
        ref[pl.ds(c, n, stride=SUBLANES), :] = val[:, c * LANES:(c + 1) * LANES]


def _from_row_tiles(ref):
    n = ref.shape[0] // SUBLANES
    return jnp.concatenate([ref[pl.ds(c, n, stride=SUBLANES), :] for c in range(SUBLANES)], axis=1)


def _moe_dispatch_kernel(pends_ref, cnt_ref, dest_ref, x_ref, g_ref, hs_hbm, hbuf, zbuf, sems, zsem, *, td):
    i = pl.program_id(0)
    nt = pl.num_programs(0)
    slot = lax.rem(i, 2)

    @pl.when(i == 0)
    def _():
        zbuf[...] = jnp.zeros_like(zbuf)
        for e in range(MOE_EXPERTS):
            @pl.when(cnt_ref[e] > 0)
            def _():
                start = pl.multiple_of((pends_ref[e] - MOE_BLOCK) * SUBLANES, MOE_BLOCK)
                pltpu.make_async_copy(zbuf, hs_hbm.at[pl.ds(start, MOE_BLOCK * SUBLANES), :], zsem).start()
        for e in range(MOE_EXPERTS):
            @pl.when(cnt_ref[e] > 0)
            def _():
                pltpu.make_async_copy(zbuf, hs_hbm.at[pl.ds(0, MOE_BLOCK * SUBLANES), :], zsem).wait()

        first_unused = pends_ref[MOE_EXPERTS - 1] // MOE_BLOCK
        n_blocks = hs_hbm.shape[0] // (MOE_BLOCK * SUBLANES)

        def zero_start(blk, carry):
            start = pl.multiple_of(blk * (MOE_BLOCK * SUBLANES), MOE_BLOCK)
            pltpu.make_async_copy(zbuf, hs_hbm.at[pl.ds(start, MOE_BLOCK * SUBLANES), :], zsem).start()
            return carry

        def zero_wait(blk, carry):
            pltpu.make_async_copy(zbuf, hs_hbm.at[pl.ds(0, MOE_BLOCK * SUBLANES), :], zsem).wait()
            return carry

        lax.fori_loop(first_unused, n_blocks, zero_start, 0)
        lax.fori_loop(first_unused, n_blocks, zero_wait, 0)

    hb = hbuf.at[slot]
    _to_row_tiles(hb, _rms(x_ref[...], g_ref[...]))
    for j in range(td):
        for c in range(2):
            row = pl.multiple_of(dest_ref[0, c * td + j] * SUBLANES, SUBLANES)
            pltpu.make_async_copy(hb.at[pl.ds(j * SUBLANES, SUBLANES), :],
                                  hs_hbm.at[pl.ds(row, SUBLANES), :],
                                  sems.at[slot]).start(priority=c)

    @pl.when(i > 0)
    def _():
        other = hbuf.at[1 - slot]
        _row_bytes_wait(hs_hbm, other, sems.at[1 - slot])
        _row_bytes_wait(hs_hbm, other, sems.at[1 - slot])

    @pl.when(i == nt - 1)
    def _():
        _row_bytes_wait(hs_hbm, hb, sems.at[slot])
        _row_bytes_wait(hs_hbm, hb, sems.at[slot])


def _moe_dispatch(x, g, pends, counts, dest3, P, td):
    T, D = x.shape
    assert D == ROW_TILE
    grid_spec = pltpu.PrefetchScalarGridSpec(
        num_scalar_prefetch=2,
        grid=(T // td,),
        in_specs=[pl.BlockSpec((None, 1, 2 * td), lambda i, pe, cn: (i, 0, 0), memory_space=pltpu.SMEM),
                  pl.BlockSpec((td, D), lambda i, pe, cn: (i, 0)),
                  pl.BlockSpec((1, D), lambda i, pe, cn: (0, 0))],
        out_specs=pl.BlockSpec(memory_space=pl.ANY),
        scratch_shapes=[pltpu.VMEM((2, td * SUBLANES, LANES), F32),
                        pltpu.VMEM((MOE_BLOCK * SUBLANES, LANES), F32),
                        pltpu.SemaphoreType.DMA((2,)),
                        pltpu.SemaphoreType.DMA(())],
    )
    return pl.pallas_call(
        functools.partial(_moe_dispatch_kernel, td=td),
        grid_spec=grid_spec,
        out_shape=jax.ShapeDtypeStruct((P * SUBLANES, LANES), F32),
        compiler_params=_cparams(1),
        name="moe_dispatch",
    )(pends, counts, dest3, x, g.reshape(1, D))


def _moe_expert_kernel(be_ref, nu_ref, hs_ref, w1_ref, w3_ref, w2_ref, o_ref, w1b, w3b, w2b):
    i = pl.program_id(0)
    used = i < nu_ref[0]
    changed = jnp.logical_or(i == 0, be_ref[i] != be_ref[jnp.maximum(i - 1, 0)])

    @pl.when(jnp.logical_and(used, changed))
    def _():
        w1b[...] = w1_ref[...].astype(BF16)
        w3b[...] = w3_ref[...].astype(BF16)
        w2b[...] = w2_ref[...].astype(BF16)

    @pl.when(used)
    def _():
        xe = _from_row_tiles(hs_ref).astype(BF16)
        ff = w1b.shape[1]
        halves = [slice(0, ff // 2), slice(ff // 2, ff)]
        ups = [(jnp.dot(xe, w1b[:, sl], preferred_element_type=F32),
                jnp.dot(xe, w3b[:, sl], preferred_element_type=F32)) for sl in halves]
        act = [(a * _sigmoid(a) * b).astype(BF16) for a, b in ups]
        y = sum(jnp.dot(a, w2b[sl, :], preferred_element_type=F32) for a, sl in zip(act, halves))
        _to_row_tiles(o_ref, y)

    @pl.when(jnp.logical_not(used))
    def _():
        o_ref[...] = jnp.zeros_like(o_ref)


def _moe_experts(hs, block_e, n_used, w1, w3, w2, layer):
    P = hs.shape[0] // SUBLANES
    D = ROW_TILE
    FF = w1.shape[-1]
    NB = P // MOE_BLOCK
    last = lambda i, nu: jnp.minimum(i, nu[0] - 1)
    grid_spec = pltpu.PrefetchScalarGridSpec(
        num_scalar_prefetch=2,
        grid=(NB,),
        in_specs=[pl.BlockSpec((MOE_BLOCK * SUBLANES, LANES), lambda i, be, nu: (last(i, nu), 0)),
                  pl.BlockSpec((None, None, D, FF), lambda i, be, nu: (layer, be[last(i, nu)], 0, 0)),
                  pl.BlockSpec((None, None, D, FF), lambda i, be, nu: (layer, be[last(i, nu)], 0, 0)),
                  pl.BlockSpec((None, None, FF, D), lambda i, be, nu: (layer, be[last(i, nu)], 0, 0))],
        out_specs=pl.BlockSpec((MOE_BLOCK * SUBLANES, LANES), lambda i, be, nu: (i, 0)),
        scratch_shapes=[pltpu.VMEM((D, FF), BF16),
                        pltpu.VMEM((D, FF), BF16),
                        pltpu.VMEM((FF, D), BF16)],
    )
    return pl.pallas_call(
        _moe_expert_kernel,
        grid_spec=grid_spec,
        out_shape=jax.ShapeDtypeStruct((P * SUBLANES, LANES), F32),
        compiler_params=_cparams(1),
        name="moe_experts",
    )(block_e, n_used, hs, w1, w3, w2)


def _gather_rows(src_hbm, idx_ref, dst_ref, sem, n_rows):
    for r in range(n_rows):
        row = pl.multiple_of(idx_ref[0, r] * SUBLANES, SUBLANES)
        pltpu.make_async_copy(src_hbm.at[pl.ds(row, SUBLANES), :],
                              dst_ref.at[pl.ds(r * SUBLANES, SUBLANES), :], sem).start(priority=r % 2)


def _moe_combine_kernel(pos_ref, posn_ref, x_ref, info_ref, yb_hbm, g_ref, *rest, tc, final_norm, splits):
    if splits:
        w_ref, o_ref = rest[0], rest[1]
        p_refs = rest[2:2 + len(splits)]
        ybuf, sems = rest[2 + len(splits):]
    else:
        o_ref, ybuf, sems = rest
    i = pl.program_id(0)
    nb = pl.num_programs(0)
    slot = lax.rem(i, 2)

    @pl.when(i == 0)
    def _():
        def issue(r, carry):
            src = pl.multiple_of(pos_ref[0, r] * SUBLANES, SUBLANES)
            dst = pl.multiple_of(r * SUBLANES, SUBLANES)
            pltpu.make_async_copy(yb_hbm.at[pl.ds(src, SUBLANES), :],
                                  ybuf.at[0, pl.ds(dst, SUBLANES), :], sems.at[0]).start()
            return carry
        lax.fori_loop(0, 2 * tc, issue, 0)

    @pl.when(i + 1 < nb)
    def _():
        _gather_rows(yb_hbm, posn_ref, ybuf.at[1 - slot], sems.at[1 - slot], 2 * tc)

    pltpu.make_async_copy(yb_hbm.at[pl.ds(0, 2 * tc * SUBLANES), :], ybuf.at[slot], sems.at[slot]).wait()
    info = info_ref[...]
    yb = ybuf.at[slot]
    y0 = _from_row_tiles(yb.at[pl.ds(0, tc * SUBLANES), :])
    y1 = _from_row_tiles(yb.at[pl.ds(tc * SUBLANES, tc * SUBLANES), :])
    out = x_ref[...] + (y0 * info[:, 4:5] + y1 * info[:, 5:6])
    if final_norm:
        out = _rms(out, g_ref[...])
    o_ref[...] = out
    if splits:
        h = _rms(out, g_ref[...]).astype(BF16)
        off = 0
        for p_ref, n in zip(p_refs, splits):
            p_ref[...] = jnp.dot(h, w_ref[:, off:off + n], preferred_element_type=F32).astype(p_ref.dtype)
            off += n


def _moe_combine(x, info, dest3, yb, g, final_norm, tc, next_proj=None):
    T, D = x.shape
    NT = T // tc
    in_specs = [pl.BlockSpec((None, 1, 2 * tc), lambda i: (i, 0, 0), memory_space=pltpu.SMEM),
                pl.BlockSpec((None, 1, 2 * tc), lambda i: (jnp.minimum(i + 1, NT - 1), 0, 0),
                             memory_space=pltpu.SMEM),
                pl.BlockSpec((tc, D), lambda i: (i, 0)),
                pl.BlockSpec((tc, LANES), lambda i: (i, 0)),
                pl.BlockSpec(memory_space=pl.ANY),
                pl.BlockSpec((1, D), lambda i: (0, 0))]
    out_specs = [pl.BlockSpec((tc, D), lambda i: (i, 0))]
    out_shape = [jax.ShapeDtypeStruct((T, D), F32)]
    args = [dest3, dest3, x, info, yb, g.reshape(1, D)]
    splits = ()
    if next_proj is not None:
        w, splits, out_dtypes = next_proj
        assert not final_norm and sum(splits) == w.shape[1]
        in_specs.append(pl.BlockSpec(w.shape, lambda i: (0, 0)))
        args.append(w)
        out_specs += [pl.BlockSpec((tc, n), lambda i: (i, 0)) for n in splits]
        out_shape += [jax.ShapeDtypeStruct((T, n), dt) for n, dt in zip(splits, out_dtypes)]
    outs = pl.pallas_call(
        functools.partial(_moe_combine_kernel, tc=tc, final_norm=final_norm, splits=tuple(splits)),
        grid=(NT,),
        in_specs=in_specs,
        out_specs=out_specs,
        out_shape=out_shape,
        scratch_shapes=[pltpu.VMEM((2, 2 * tc * SUBLANES, LANES), F32), pltpu.SemaphoreType.DMA((2,))],
        compiler_params=_cparams(1),
        name="moe_combine",
    )(*args)
    return outs[0], tuple(outs[1:])


MOE_TILE = 256
MOE_COMBINE_TILE = 256
ROUTER_TILE = 256


def _tile_slots(dest, tile):
    n_tiles = dest.shape[0] * dest.shape[2] // tile
    return jnp.concatenate([dest[:, 0, :].reshape(n_tiles, 1, tile), dest[:, 1, :].reshape(n_tiles, 1, tile)], axis=2)


def _moe_layer(x, g, w_group, b_group, w_expert, b_expert, w1, w3, w2, layer, g_out, final_norm, next_proj):
    T, D = x.shape
    n_log = MOE_GROUPS + MOE_EXPERTS
    tm = ROUTER_TILE
    wt_router = jnp.zeros((LANES, D), F32).at[:MOE_GROUPS].set(w_group.T).at[MOE_GROUPS:n_log].set(w_expert.T)
    bt_router = jnp.zeros((LANES,), F32).at[:MOE_GROUPS].set(b_group).at[MOE_GROUPS:n_log].set(b_expert)
    info, slot, cnt = _router(x, g, wt_router.astype(BF16), jnp.broadcast_to(bt_router[:, None], (LANES, tm)), tm)
    P = 2 * T + MOE_EXPERTS * MOE_BLOCK
    NB = P // MOE_BLOCK
    counts = cnt[MOE_GROUPS:n_log, 0].astype(I32)
    padded = (counts + MOE_BLOCK - 1) // MOE_BLOCK * MOE_BLOCK
    pends = jnp.cumsum(padded).astype(I32)
    pstarts = pends - padded
    block_start = jnp.arange(NB, dtype=I32) * MOE_BLOCK
    block_e = jnp.minimum(jnp.sum((pends[None, :] <= block_start[:, None]).astype(I32), axis=1),
                          MOE_EXPERTS - 1).astype(I32)
    n_used = (pends[-1:] // MOE_BLOCK).astype(I32)
    eid = slot[:, 0:2, :].astype(I32)
    expert_ids = jnp.arange(MOE_EXPERTS, dtype=I32)
    seg_start = jnp.sum(jnp.where(eid[..., None] == expert_ids, pstarts, 0), axis=-1)
    dest = seg_start + slot[:, 2:4, :].astype(I32)
    hs = _moe_dispatch(x, g, pends, counts, _tile_slots(dest, MOE_TILE), P, MOE_TILE)
    yb = _moe_experts(hs, block_e, n_used, w1, w3, w2, layer)
    return _moe_combine(x, info, _tile_slots(dest, MOE_COMBINE_TILE), yb, g_out, final_norm, MOE_COMBINE_TILE,
                        next_proj)


def kernel(x, mem, norm_mix, norm_xattn, norm_moe, norm_final, ev_w_in, ev_sinks, ev_mu, ev_w0, ev_w2, ev_a0, ev_a2, ev_g2, ev_k_k, ev_k_a, ev_r_k, ev_lnx_w, ev_lnx_b, ev_w_out, od_w_in, od_gate_up, od_gate_b, od_onorm, od_w_out, mem_norm, mem_wk, mem_wv, xa_wq, xa_wo, moe_w_group, moe_b_group, moe_w_expert, moe_b_expert, moe_w1, moe_w3, moe_w2):
    B, S, D = x.shape
    M = mem.shape[1]
    T = B * S
    depth = norm_mix.shape[0]
    xf = x.reshape(T, D)

    XW = XA_HEADS * XA_HEAD_DIM
    w_kv = jnp.concatenate([mem_wk, mem_wv], axis=1).astype(BF16)
    mk, mv = _norm_matmul(mem.reshape(B * M, D), mem_norm, w_kv, (XW, XW), (BF16, BF16))

    KW = GLA_HEADS * GLA_DK
    VW = GLA_HEADS * GLA_DV
    swa_cols = SWA_Q_HEADS * HEAD_DIM + 2 * (SWA_Q_HEADS // SWA_GROUP) * HEAD_DIM

    def in_proj(layer):
        i = layer // 2
        if layer % 2 == 0:
            return ev_w_in[i].astype(BF16), (swa_cols, ev_w_in.shape[-1] - swa_cols), (F32, F32)
        R = od_gate_up.shape[1]
        w = od_w_in[i]
        w_re = jnp.concatenate([w[:, :2 * KW + VW], w[:, 2 * KW + VW + R:],
                                w[:, 2 * KW + VW:2 * KW + VW + R],
                                jnp.zeros((D, LANES - R), F32)], axis=1).astype(BF16)
        return w_re, (2 * KW + 2 * VW, LANES), (F32, F32)

    w0, splits0, dts0 = in_proj(0)
    proj = _norm_matmul(xf, norm_mix[0], w0, splits0, dts0)
    for layer in range(depth):
        i = layer // 2
        if layer % 2 == 0:
            qkv, p_rw = proj
            o_a = _swa(qkv, ev_sinks[i], B, S)
            r, lw, k, v, a, b, g = _rwkv_prep(p_rw, ev_mu[i], ev_w0[i], ev_w2[i], ev_a0[i], ev_a2[i],
                                              ev_g2[i], ev_k_k[i], ev_k_a[i], B, S)
            o_b = _rwkv_scan(r, lw, k, v, a, b, g, ev_r_k[i].reshape(-1), ev_lnx_w[i], ev_lnx_b[i])
            w_out = ev_w_out[i].astype(BF16)
            qw = o_a.shape[-1]
            mix_acts, mix_ws = [o_a, o_b], [w_out[:qw], w_out[qw:]]
        else:
            qkvo, gd = proj
            R = od_gate_up.shape[1]
            gup = jnp.zeros((LANES, KW), F32).at[:R].set(od_gate_up[i]).astype(BF16)
            o = _gla(qkvo, gd, gup, od_gate_b[i], od_onorm[i], B, S)
            mix_acts, mix_ws = [o], [od_w_out[i].astype(BF16)]
        xf = _mix_proj_xattn(xf, mix_acts, mix_ws, norm_xattn[layer], xa_wq[layer].astype(BF16), mk, mv,
                             xa_wo[layer].astype(BF16), B, S)
        last = layer == depth - 1
        g_out = norm_final if last else norm_mix[layer + 1]
        xf, proj = _moe_layer(xf, norm_moe[layer], moe_w_group[layer], moe_b_group[layer], moe_w_expert[layer],
                              moe_b_expert[layer], moe_w1, moe_w3, moe_w2, layer,
                              g_out, last, None if last else in_proj(layer + 1))
    return xf.reshape(B, S, D)
```

```python
import functools

import jax
import jax.numpy as jnp
from jax import lax
from jax.experimental import pallas as pl
from jax.experimental.pallas import tpu as pltpu

F32 = jnp.float32
BF16 = jnp.bfloat16
I32 = jnp.int32

EPS = 1e-6
HEAD_DIM = 64
SWA_WINDOW = 128
SWA_Q_HEADS = 8
SWA_GROUP = 4
RWKV_HEADS = 8
RWKV_WIDTH = 512
RWKV_LN_EPS = 64e-5
RWKV_CHUNK = 64
GLA_HEADS = 4
GLA_DK = 128
GLA_DV = 256
GLA_CHUNK = 64
GLA_GATE_NORM = 16.0
XA_HEADS = 4
XA_HEAD_DIM = 128
MOE_GROUPS = 4
MOE_EXPERTS_PER_GROUP = 8
MOE_EXPERTS = 32
MOE_BLOCK = 512
LANES = 128
SUBLANES = 8
ROW_TILE = SUBLANES * LANES

VMEM_LIMIT_BYTES = 48 * 1024 * 1024


def _cparams(n_axes):
    return pltpu.CompilerParams(dimension_semantics=("arbitrary",) * n_axes,
                                vmem_limit_bytes=VMEM_LIMIT_BYTES)


def _dot(a, b):
    return jnp.dot(a.astype(BF16), b.astype(BF16), preferred_element_type=F32)


def _dot_nt(a, b):
    return lax.dot_general(a.astype(BF16), b.astype(BF16), (((1,), (1,)), ((), ())),
                           preferred_element_type=F32)


def _dot_tn(a, b):
    return lax.dot_general(a.astype(BF16), b.astype(BF16), (((0,), (0,)), ((), ())),
                           preferred_element_type=F32)


def _dot_f32(a, b):
    return jnp.dot(a, b, preferred_element_type=F32, precision=lax.Precision.HIGHEST)


def _rms(x, g):
    ms = jnp.mean(x * x, axis=-1, keepdims=True)
    return x * lax.rsqrt(ms + EPS) * g


def _sigmoid(x):
    return 1.0 / (1.0 + jnp.exp(-x))


def _softplus(x):
    return jnp.maximum(x, 0.0) + jnp.log(1.0 + jnp.exp(-jnp.abs(x)))


def _norm_matmul_kernel(x_ref, g_ref, w_ref, *o_refs, splits):
    h = _rms(x_ref[...], g_ref[...]).astype(BF16)
    off = 0
    for o_ref, n in zip(o_refs, splits):
        o_ref[...] = jnp.dot(h, w_ref[:, off:off + n], preferred_element_type=F32).astype(o_ref.dtype)
        off += n


def _norm_matmul(x, g, w, splits, out_dtypes, tm=256):
    T, D = x.shape
    N = w.shape[1]
    assert sum(splits) == N and T % tm == 0
    return pl.pallas_call(
        functools.partial(_norm_matmul_kernel, splits=tuple(splits)),
        grid=(T // tm,),
        in_specs=[pl.BlockSpec((tm, D), lambda i: (i, 0)),
                  pl.BlockSpec((1, D), lambda i: (0, 0)),
                  pl.BlockSpec((D, N), lambda i: (0, 0))],
        out_specs=[pl.BlockSpec((tm, n), lambda i: (i, 0)) for n in splits],
        out_shape=[jax.ShapeDtypeStruct((T, n), dt) for n, dt in zip(splits, out_dtypes)],
        compiler_params=_cparams(1),
        name="norm_matmul",
    )(x, g.reshape(1, D), w)


def _swa_kernel(sinks_ref, q_ref, kp_ref, kc_ref, vp_ref, vc_ref, o_ref):
    n = pl.program_id(1)
    W = SWA_WINDOW
    NB = q_ref.shape[0]
    qpos = lax.broadcasted_iota(I32, (W, 2 * W), 0) + W
    kpos = lax.broadcasted_iota(I32, (W, 2 * W), 1)
    rel = qpos - kpos
    in_window = jnp.where(rel >= 0, jnp.where(rel < W, 1, 0), 0)
    has_prev = jnp.where(n > 0, 1, 0)
    valid = (in_window * jnp.where(kpos >= W, 1, has_prev)) > 0
    n_groups = SWA_Q_HEADS // SWA_GROUP
    streams = [(bi, g) for bi in range(NB) for g in range(n_groups)]
    qb = [q_ref[bi].astype(BF16) for bi in range(NB)]
    kb = [jnp.concatenate([kp_ref[bi], kc_ref[bi]], axis=0).astype(BF16) for bi in range(NB)]
    vb = [jnp.concatenate([vp_ref[bi], vc_ref[bi]], axis=0).astype(BF16) for bi in range(NB)]
    gs = lambda g: slice(g * HEAD_DIM, (g + 1) * HEAD_DIM)
    scores = []
    for bi, g in streams:
        qg = jnp.concatenate([qb[bi][:, h * HEAD_DIM:(h + 1) * HEAD_DIM]
                              for h in range(g * SWA_GROUP, (g + 1) * SWA_GROUP)], axis=0)
        scores.append(_dot_nt(qg, kb[bi][:, gs(g)]))
    probs = []
    for i, (bi, g) in enumerate(streams):
        pieces = []
        for j in range(SWA_GROUP):
            s = jnp.where(valid, scores[i][j * W:(j + 1) * W] * (HEAD_DIM ** -0.5), -jnp.inf)
            sink = sinks_ref[g * SWA_GROUP + j]
            m = jnp.maximum(jnp.max(s, axis=-1, keepdims=True), sink)
            p = jnp.exp(s - m)
            den = jnp.sum(p, axis=-1, keepdims=True) + jnp.exp(sink - m)
            pieces.append((p / den).astype(BF16))
        probs.append(jnp.concatenate(pieces, axis=0))
    ogs = [_dot(probs[i], vb[bi][:, gs(g)]) for i, (bi, g) in enumerate(streams)]
    for bi in range(NB):
        outs = []
        for g in range(n_groups):
            og = ogs[bi * n_groups + g]
            outs += [og[j * W:(j + 1) * W] for j in range(SWA_GROUP)]
        o_ref[bi] = jnp.concatenate(outs, axis=1).astype(o_ref.dtype)


SWA_BATCH_ROWS = 2


def _swa(qkv, sinks, B, S):
    W = SWA_WINDOW
    qkv3 = qkv.reshape(B, S, qkv.shape[-1])
    qw = SWA_Q_HEADS * HEAD_DIM
    kw = qw // SWA_GROUP
    kcol = qw // kw
    nb = SWA_BATCH_ROWS if B % SWA_BATCH_ROWS == 0 else 1
    out = pl.pallas_call(
        _swa_kernel,
        grid=(B // nb, S // W),
        in_specs=[pl.BlockSpec(memory_space=pltpu.SMEM),
                  pl.BlockSpec((nb, W, qw), lambda b, n: (b, n, 0)),
                  pl.BlockSpec((nb, W, kw), lambda b, n: (b, jnp.maximum(n - 1, 0), kcol)),
                  pl.BlockSpec((nb, W, kw), lambda b, n: (b, n, kcol)),
                  pl.BlockSpec((nb, W, kw), lambda b, n: (b, jnp.maximum(n - 1, 0), kcol + 1)),
                  pl.BlockSpec((nb, W, kw), lambda b, n: (b, n, kcol + 1))],
        out_specs=pl.BlockSpec((nb, W, qw), lambda b, n: (b, n, 0)),
        out_shape=jax.ShapeDtypeStruct((B, S, qw), BF16),
        compiler_params=_cparams(2),
        name="swa",
    )(sinks, qkv3, qkv3, qkv3, qkv3, qkv3)
    return out.reshape(B * S, qw)


def _rwkv_prep_math(p, last, mu_ref, w0_ref, w2_ref, a0_ref, a2_ref, g2_ref, kk_ref, ka_ref, outs):
    r_out, lw_out, k_out, v_out, a_out, b_out, g_out = outs
    C = RWKV_WIDTH
    row = lax.broadcasted_iota(I32, p.shape, 0)
    p_prev = jnp.where(row == 0, last, pltpu.roll(p, 1, axis=0))
    p = p + (p_prev - p) * mu_ref[...]
    r = p[:, :C]
    k = p[:, C:2 * C]
    v = p[:, 2 * C:3 * C]
    xw = p[:, 3 * C:3 * C + 64]
    xa = p[:, 3 * C + 64:3 * C + 128]
    xg = p[:, 3 * C + 128:]
    w = -_softplus(-(w0_ref[...] + _dot(jnp.tanh(xw), w2_ref[...]))) - 0.5
    lw = -jnp.exp(w)
    a = _sigmoid(a0_ref[...] + _dot(xa, a2_ref[...]))
    g = _dot(_sigmoid(xg), g2_ref[...])
    kk = k * kk_ref[...]
    pieces = []
    for h in range(RWKV_HEADS):
        kh = kk[:, h * HEAD_DIM:(h + 1) * HEAD_DIM]
        nrm = jnp.sqrt(jnp.sum(kh * kh, axis=-1, keepdims=True))
        pieces.append(kh / jnp.maximum(nrm, 1e-12))
    kk = jnp.concatenate(pieces, axis=1)
    r_out[...] = r
    lw_out[...] = lw
    k_out[...] = k * (1.0 + (a - 1.0) * ka_ref[...])
    v_out[...] = v
    a_out[...] = -kk
    b_out[...] = kk * a
    g_out[...] = g


def _rwkv_prep_kernel(p_ref, pprev_ref, *refs):
    n = pl.program_id(1)
    last = jnp.where(n > 0, pprev_ref[7:8, :], 0.0)
    _rwkv_prep_math(p_ref[...], last, *refs[:8], refs[8:])


def _rwkv_params(mu, w0, w2, a0, a2, g2, k_k, k_a):
    row = lambda t: t.reshape(1, -1)
    return [row(mu), row(w0), w2.astype(BF16), row(a0), a2.astype(BF16), g2.astype(BF16), row(k_k), row(k_a)]


def _rwkv_prep(p, mu, w0, w2, a0, a2, g2, k_k, k_a, B, S, tt=256):
    C = RWKV_WIDTH
    PW = p.shape[-1]
    p3 = p.reshape(B, S, PW)
    full = lambda arr: pl.BlockSpec(arr.shape, lambda b, n: (0,) * arr.ndim)
    params = _rwkv_params(mu, w0, w2, a0, a2, g2, k_k, k_a)
    outs = pl.pallas_call(
        _rwkv_prep_kernel,
        grid=(B, S // tt),
        in_specs=[pl.BlockSpec((None, tt, PW), lambda b, n: (b, n, 0)),
                  pl.BlockSpec((None, 8, PW), lambda b, n: (b, jnp.maximum(n * (tt // 8) - 1, 0), 0))]
                 + [full(t) for t in params],
        out_specs=[pl.BlockSpec((None, tt, C), lambda b, n: (b, n, 0))] * 7,
        out_shape=[jax.ShapeDtypeStruct((B, S, C), F32)] * 7,
        compiler_params=_cparams(2),
        name="rwkv_prep",
    )(p3, p3, *params)
    return outs


def _in_proj_rwkv_kernel(x_ref, g_ref, w_ref, *refs, swa_cols, tiles_per_seq):
    params = refs[:8]
    qkv_out = refs[8]
    outs = refs[9:16]
    p_buf, last_buf = refs[16:]
    i = pl.program_id(0)

    @pl.when(i == 0)
    def _():
        p_buf[...] = jnp.zeros_like(p_buf)
        last_buf[...] = jnp.zeros_like(last_buf)

    j = i - 1
    p_prev_tile = p_buf[lax.rem(i + 1, 2)]
    tm = p_prev_tile.shape[0]
    last = jnp.where(lax.rem(j, tiles_per_seq) == 0, 0.0, last_buf[...])
    _rwkv_prep_math(p_prev_tile, last, *params, outs)
    last_buf[...] = p_prev_tile[tm - 1:tm, :]
    h = _rms(x_ref[...], g_ref[...]).astype(BF16)
    qkv_out[...] = jnp.dot(h, w_ref[:, :swa_cols], preferred_element_type=F32)
    p_buf[lax.rem(i, 2)] = jnp.dot(h, w_ref[:, swa_cols:], preferred_element_type=F32)


def _in_proj_rwkv(x, g, w, swa_cols, mu, w0, w2, a0, a2, g2, k_k, k_a, S, tm=256):
    T, D = x.shape
    N = w.shape[1]
    C = RWKV_WIDTH
    NT = T // tm
    params = _rwkv_params(mu, w0, w2, a0, a2, g2, k_k, k_a)
    const = lambda arr: pl.BlockSpec(arr.shape, lambda i: (0,) * arr.ndim)
    cur = lambda i: (jnp.minimum(i, NT - 1), 0)
    prev = lambda i: (jnp.maximum(i - 1, 0), 0)
    outs = pl.pallas_call(
        functools.partial(_in_proj_rwkv_kernel, swa_cols=swa_cols, tiles_per_seq=S // tm),
        grid=(NT + 1,),
        in_specs=[pl.BlockSpec((tm, D), cur), pl.BlockSpec((1, D), lambda i: (0, 0)), const(w)]
                 + [const(t) for t in params],
        out_specs=[pl.BlockSpec((tm, swa_cols), cur)] + [pl.BlockSpec((tm, C), prev)] * 7,
        out_shape=[jax.ShapeDtypeStruct((T, swa_cols), F32)] + [jax.ShapeDtypeStruct((T, C), F32)] * 7,
        scratch_shapes=[pltpu.VMEM((2, tm, N - swa_cols), F32), pltpu.VMEM((1, N - swa_cols), F32)],
        compiler_params=_cparams(1),
        name="in_proj_rwkv",
    )(x, g.reshape(1, D), w, *params)
    return outs[0], outs[1:]


def _pair_blockdiag(x):
    lane = lax.broadcasted_iota(I32, x.shape, 1)
    zero = jnp.zeros_like(x)
    return jnp.concatenate([jnp.where(lane < HEAD_DIM, x, zero), jnp.where(lane >= HEAD_DIM, x, zero)], axis=0)


def _rwkv_scan_kernel(r_ref, lw_ref, k_ref, v_ref, a_ref, b_ref, g_ref, rk_ref, lnw_ref, lnb_ref,
                      o_ref, s_ref):
    c = pl.program_id(1)

    @pl.when(c == 0)
    def _():
        s_ref[...] = jnp.zeros_like(s_ref)

    C = RWKV_CHUNK
    NB = r_ref.shape[0]
    NP = RWKV_HEADS // 2
    PW = 2 * HEAD_DIM
    row = lax.broadcasted_iota(I32, (C, C), 0)
    col = lax.broadcasted_iota(I32, (C, C), 1)
    tri = jnp.where(row >= col, 1.0, 0.0).astype(F32)
    rowp = lax.broadcasted_iota(I32, (C, PW), 0)
    colp = lax.broadcasted_iota(I32, (C, PW), 1)
    colp = jnp.where(colp >= HEAD_DIM, colp - HEAD_DIM, colp)
    lower_p = rowp >= colp
    strict_p = rowp > colp
    rows = lax.broadcasted_iota(I32, (PW, PW), 0)
    cols = lax.broadcasted_iota(I32, (PW, PW), 1)
    same_head = jnp.where(rows >= HEAD_DIM, 1, 0) == jnp.where(cols >= HEAD_DIM, 1, 0)
    first = lax.broadcasted_iota(I32, (C, PW), 1) < HEAD_DIM

    streams = [(bi, p) for bi in range(NB) for p in range(NP)]
    pre = []
    for bi in range(NB):
        lw = lw_ref[bi]
        cum = _dot_f32(tri, lw)
        cum_last = cum[C - 1:C, :]
        r = r_ref[bi]
        k = k_ref[bi]
        v = v_ref[bi]
        a = a_ref[bi]
        b = b_ref[bi]
        e_neg = jnp.exp(-cum)
        e_rem = jnp.exp(cum_last - cum)
        pre.append(dict(
            r_t=(r * jnp.exp(cum)).astype(BF16), a_t=(a * jnp.exp(cum - lw)).astype(BF16),
            b_t=(b * e_neg).astype(BF16), k_t=(k * e_neg).astype(BF16),
            b_d=(b * e_rem).astype(BF16), k_d=(k * e_rem).astype(BF16),
            v_b=v.astype(BF16), v=v, e_last=jnp.exp(cum_last), rkk=r * k * rk_ref[...], g=g_ref[bi]))

    def lanes(p):
        return slice(p * PW, (p + 1) * PW)

    ar = [jnp.concatenate([pre[bi]['a_t'][:, lanes(p)], pre[bi]['r_t'][:, lanes(p)]], axis=0) for bi, p in streams]
    s0 = [s_ref[bi, p] for bi, p in streams]
    big = [_dot_nt(ar[i], jnp.concatenate([_pair_blockdiag(pre[bi]['b_t'][:, lanes(p)]),
                                           _pair_blockdiag(pre[bi]['k_t'][:, lanes(p)]),
                                           s0[i].astype(BF16)], axis=0))
           for i, (bi, p) in enumerate(streams)]
    m_b = [t[:, :PW] for t in big]
    m_k = [t[:, PW:2 * PW] for t in big]
    ars = [t[:, 2 * PW:] for t in big]
    v_p = [pre[bi]['v_b'][:, lanes(p)] for bi, p in streams]
    v_bd = [_pair_blockdiag(vp) for vp in v_p]
    x = [ars[i][:C] + _dot(jnp.where(strict_p, m_k[i][:C], 0.0), v_bd[i]) for i in range(len(streams))]
    pw = [jnp.where(strict_p, m_b[i][:C], 0.0).astype(BF16) for i in range(len(streams))]
    n_stages = 6
    for stage in range(n_stages):
        if stage < n_stages - 1:
            prod = [_dot(pw[i], jnp.concatenate([_pair_blockdiag(x[i].astype(BF16)), _pair_blockdiag(pw[i])], axis=1))
                    for i in range(len(streams))]
            x = [x[i] + prod[i][:, :PW] for i in range(len(streams))]
            pw = [prod[i][:, PW:].astype(BF16) for i in range(len(streams))]
        else:
            x = [x[i] + _dot(pw[i], _pair_blockdiag(x[i].astype(BF16))) for i in range(len(streams))]
    u_b = [xi.astype(BF16) for xi in x]
    y = [ars[i][C:]
         + _dot(jnp.concatenate([jnp.where(lower_p, m_b[i][C:], 0.0), jnp.where(lower_p, m_k[i][C:], 0.0)], axis=1),
                jnp.concatenate([_pair_blockdiag(u_b[i]), v_bd[i]], axis=0))
         for i in range(len(streams))]
    for i, (bi, p) in enumerate(streams):
        upd = _dot_tn(jnp.concatenate([u_b[i], v_p[i]], axis=0),
                      jnp.concatenate([pre[bi]['b_d'][:, lanes(p)], pre[bi]['k_d'][:, lanes(p)]], axis=0))
        s_ref[bi, p] = s0[i] * pre[bi]['e_last'][:, lanes(p)] + jnp.where(same_head, upd, 0.0)

    lnw = lnw_ref[...]
    lnb = lnb_ref[...]

    def head_sum(t):
        s1 = jnp.sum(jnp.where(first, t, 0.0), axis=-1, keepdims=True)
        s2 = jnp.sum(jnp.where(first, 0.0, t), axis=-1, keepdims=True)
        return jnp.where(first, s1, s2)

    for bi in range(NB):
        outs = []
        for p in range(NP):
            yi = y[bi * NP + p]
            mean = head_sum(yi) * (1.0 / HEAD_DIM)
            yc = yi - mean
            var = head_sum(yc * yc) * (1.0 / HEAD_DIM)
            yn = yc * lax.rsqrt(var + RWKV_LN_EPS) * lnw[:, lanes(p)] + lnb[:, lanes(p)]
            bonus = head_sum(pre[bi]['rkk'][:, lanes(p)]) * pre[bi]['v'][:, lanes(p)]
            outs.append((yn + bonus) * pre[bi]['g'][:, lanes(p)])
        o_ref[bi] = jnp.concatenate(outs, axis=1).astype(o_ref.dtype)


RWKV_BATCH_ROWS = 4


def _rwkv_scan(r, lw, k, v, a, b, g, r_k, lnx_w, lnx_b):
    B, S, W = r.shape
    C = RWKV_CHUNK
    nb = RWKV_BATCH_ROWS if B % RWKV_BATCH_ROWS == 0 else 1
    seq = pl.BlockSpec((nb, C, W), lambda bb, c: (bb, c, 0))
    par = pl.BlockSpec((1, W), lambda bb, c: (0, 0))
    out = pl.pallas_call(
        _rwkv_scan_kernel,
        grid=(B // nb, S // C),
        in_specs=[seq] * 7 + [par] * 3,
        out_specs=seq,
        out_shape=jax.ShapeDtypeStruct((B, S, W), BF16),
        scratch_shapes=[pltpu.VMEM((nb, RWKV_HEADS // 2, 2 * HEAD_DIM, 2 * HEAD_DIM), F32)],
        compiler_params=_cparams(2),
        name="rwkv_scan",
    )(r, lw, k, v, a, b, g, r_k.reshape(1, W), lnx_w.reshape(1, W), lnx_b.reshape(1, W))
    return out.reshape(B * S, W)


def _gla_kernel(q_ref, k_ref, v_ref, og_ref, gd_ref, gup_ref, gb_ref, on_ref, o_ref, s_ref):
    c = pl.program_id(1)

    @pl.when(c == 0)
    def _():
        s_ref[...] = jnp.zeros_like(s_ref)

    C = GLA_CHUNK
    NB = q_ref.shape[0]
    row = lax.broadcasted_iota(I32, (C, C), 0)
    col = lax.broadcasted_iota(I32, (C, C), 1)
    lower = row >= col
    tri = jnp.where(lower, 1.0, 0.0).astype(F32)
    onorm = on_ref[...]
    zs = [_dot(gd_ref[bi], gup_ref[...]) + gb_ref[...] for bi in range(NB)]
    cums = [_dot_f32(tri, -_softplus(-z) / GLA_GATE_NORM) for z in zs]
    qe, ke, kd, e_last, v = [], [], [], [], []
    for bi in range(NB):
        cum = cums[bi]
        cum_last = cum[C - 1:C, :]
        k = k_ref[bi]
        qe.append((q_ref[bi] * (GLA_DK ** -0.5) * jnp.exp(cum)).astype(BF16))
        ke.append((k * jnp.exp(-cum)).astype(BF16))
        kd.append((k * jnp.exp(cum_last - cum)).astype(BF16))
        e_last.append(jnp.exp(cum_last))
        v.append(v_ref[bi].astype(BF16))
    streams = [(bi, h) for bi in range(NB) for h in range(GLA_HEADS)]
    ks = lambda h: slice(h * GLA_DK, (h + 1) * GLA_DK)
    vs = lambda h: slice(h * GLA_DV, (h + 1) * GLA_DV)
    sts = [s_ref[bi, h] for bi, h in streams]
    atts = [jnp.where(lower, _dot_nt(qe[bi][:, ks(h)], ke[bi][:, ks(h)]), 0.0) for bi, h in streams]
    inters = [_dot_nt(qe[bi][:, ks(h)], sts[i]) for i, (bi, h) in enumerate(streams)]
    os_ = [inters[i] + _dot(atts[i], v[bi][:, vs(h)]) for i, (bi, h) in enumerate(streams)]
    for i, (bi, h) in enumerate(streams):
        s_ref[bi, h] = sts[i] * e_last[bi][:, ks(h)] + _dot_tn(v[bi][:, vs(h)], kd[bi][:, ks(h)])
    for bi in range(NB):
        og = og_ref[bi]
        outs = []
        for h in range(GLA_HEADS):
            gate = og[:, vs(h)]
            outs.append(_rms(os_[bi * GLA_HEADS + h], onorm) * (gate * _sigmoid(gate)))
        o_ref[bi] = jnp.concatenate(outs, axis=1).astype(o_ref.dtype)


GLA_BATCH_ROWS = 4


def _gla(qkvo, gd, gate_up_pad, gate_b, onorm, B, S):
    C = GLA_CHUNK
    KW = GLA_HEADS * GLA_DK
    VW = GLA_HEADS * GLA_DV
    x3 = qkvo.reshape(B, S, qkvo.shape[-1])
    gd3 = gd.reshape(B, S, LANES)
    nb = GLA_BATCH_ROWS if B % GLA_BATCH_ROWS == 0 else 1
    out = pl.pallas_call(
        _gla_kernel,
        grid=(B // nb, S // C),
        in_specs=[pl.BlockSpec((nb, C, KW), lambda b, c: (b, c, 0)),
                  pl.BlockSpec((nb, C, KW), lambda b, c: (b, c, 1)),
                  pl.BlockSpec((nb, C, VW), lambda b, c: (b, c, 1)),
                  pl.BlockSpec((nb, C, VW), lambda b, c: (b, c, 2)),
                  pl.BlockSpec((nb, C, LANES), lambda b, c: (b, c, 0)),
                  pl.BlockSpec((LANES, KW), lambda b, c: (0, 0)),
                  pl.BlockSpec((1, KW), lambda b, c: (0, 0)),
                  pl.BlockSpec((1, GLA_DV), lambda b, c: (0, 0))],
        out_specs=pl.BlockSpec((nb, C, VW), lambda b, c: (b, c, 0)),
        out_shape=jax.ShapeDtypeStruct((B, S, VW), BF16),
        scratch_shapes=[pltpu.VMEM((nb, GLA_HEADS, GLA_DV, GLA_DK), F32)],
        compiler_params=_cparams(2),
        name="gla",
    )(x3, x3, x3, x3, gd3, gate_up_pad, gate_b.reshape(1, KW), onorm.reshape(1, GLA_DV))
    return out.reshape(B * S, VW)


def _xattn_kernel(*refs, n_in):
    x_ref = refs[0]
    a_refs = refs[1:1 + n_in]
    w_refs = refs[1 + n_in:1 + 2 * n_in]
    g_ref, wq_ref, mk_ref, mv_ref, wo_ref, o_ref = refs[1 + 2 * n_in:]
    tq = x_ref.shape[0]
    subs = [slice(r, r + XA_SUB_ROWS) for r in range(0, tq, XA_SUB_ROWS)]
    xs = [x_ref[sub, :] for sub in subs]
    for a_ref, w_ref in zip(a_refs, w_refs):
        xs = [x + jnp.dot(a_ref[sub, :], w_ref[...], preferred_element_type=F32) for x, sub in zip(xs, subs)]
    qs = [_dot(_rms(x, g_ref[...]), wq_ref[...]).astype(BF16) for x in xs]
    mk = mk_ref[...]
    mv = mv_ref[...]
    sls = [slice(hd * XA_HEAD_DIM, (hd + 1) * XA_HEAD_DIM) for hd in range(XA_HEADS)]
    scores = [[_dot_nt(q[:, sl], mk[:, sl]) for sl in sls] for q in qs]
    probs = []
    for sc in scores:
        ps = []
        for s in sc:
            s = s * (XA_HEAD_DIM ** -0.5)
            p = jnp.exp(s - jnp.max(s, axis=-1, keepdims=True))
            ps.append((p / jnp.sum(p, axis=-1, keepdims=True)).astype(BF16))
        probs.append(ps)
    os_ = [jnp.concatenate([_dot(p, mv[:, sl]) for p, sl in zip(ps, sls)], axis=1) for ps in probs]
    for x, o, sub in zip(xs, os_, subs):
        o_ref[sub, :] = x + _dot(o, wo_ref[...])


XA_SUB_ROWS = 256


def _mix_proj_xattn(x, acts, weights, g, wq, mk, mv, wo, B, S, tq=512):
    D = x.shape[-1]
    M = mk.shape[0] // B
    XW = mk.shape[-1]
    n_in = len(acts)
    seq3 = lambda a: a.reshape(B, S, a.shape[-1])
    row_spec = lambda a: pl.BlockSpec((None, tq, a.shape[-1]), lambda b, n: (b, n, 0))
    const = lambda a: pl.BlockSpec(a.shape, lambda b, n: (0,) * a.ndim)
    out = pl.pallas_call(
        functools.partial(_xattn_kernel, n_in=n_in),
        grid=(B, S // tq),
        in_specs=[row_spec(x)] + [row_spec(a) for a in acts] + [const(w) for w in weights]
                 + [pl.BlockSpec((1, D), lambda b, n: (0, 0)),
                    pl.BlockSpec((D, XW), lambda b, n: (0, 0)),
                    pl.BlockSpec((None, M, XW), lambda b, n: (b, 0, 0)),
                    pl.BlockSpec((None, M, XW), lambda b, n: (b, 0, 0)),
                    pl.BlockSpec((XW, D), lambda b, n: (0, 0))],
        out_specs=pl.BlockSpec((None, tq, D), lambda b, n: (b, n, 0)),
        out_shape=jax.ShapeDtypeStruct((B, S, D), F32),
        compiler_params=_cparams(2),
        name="xattn",
    )(seq3(x), *[seq3(a) for a in acts], *weights, g.reshape(1, D), wq,
      mk.reshape(B, M, XW), mv.reshape(B, M, XW), wo)
    return out.reshape(B * S, D)


ROUTER_ROWS = 40


def _router_kernel(x_ref, g_ref, wt_ref, bt_ref, info_ref, slot_ref, cnt_ref, carry_ref):
    i = pl.program_id(0)

    @pl.when(i == 0)
    def _():
        carry_ref[...] = jnp.zeros_like(carry_ref)

    h = _rms(x_ref[...], g_ref[...])
    tm = h.shape[0]
    logits = (_dot_nt(wt_ref[...], h) + bt_ref[...])[:ROUTER_ROWS]
    row = lax.broadcasted_iota(I32, logits.shape, 0)
    big = jnp.int32(LANES)
    neg = -jnp.inf
    gl = jnp.where(row < MOE_GROUPS, logits, neg)
    gmax = jnp.max(gl, axis=0, keepdims=True)
    g_top = jnp.min(jnp.where(gl == gmax, row, big), axis=0, keepdims=True)
    p_group = 1.0 / jnp.sum(jnp.exp(gl - gmax), axis=0, keepdims=True)
    lo = MOE_GROUPS + MOE_EXPERTS_PER_GROUP * g_top
    in_group = jnp.where(row >= lo, jnp.where(row < lo + MOE_EXPERTS_PER_GROUP, 1, 0), 0) > 0
    el = jnp.where(in_group, logits, neg)
    emax = jnp.max(el, axis=0, keepdims=True)
    ee = jnp.exp(el - emax)
    prob = ee / jnp.sum(ee, axis=0, keepdims=True)
    prob = jnp.where(in_group, prob, -1.0)
    p1 = jnp.max(prob, axis=0, keepdims=True)
    i1 = jnp.min(jnp.where(prob == p1, row, big), axis=0, keepdims=True)
    rest = jnp.where(row == i1, -1.0, prob)
    p2 = jnp.max(rest, axis=0, keepdims=True)
    i2 = jnp.min(jnp.where(rest == p2, row, big), axis=0, keepdims=True)
    tot = p1 + p2
    g1 = p_group * p1 / tot
    g2 = p_group * p2 / tot
    oh = jnp.concatenate([jnp.where(row == i1, 1.0, 0.0), jnp.where(row == i2, 1.0, 0.0)], axis=0)
    tr = lax.broadcasted_iota(I32, (tm, tm), 0)
    tc = lax.broadcasted_iota(I32, (tm, tm), 1)
    pre = _dot(oh, jnp.where(tr < tc, 1.0, 0.0))
    tots = _dot(oh, jnp.ones((tm, LANES), F32))
    reps = tm // LANES
    carry = carry_ref[...]
    base1 = jnp.concatenate([carry] * reps, axis=1)
    base2 = jnp.concatenate([carry + tots[:ROUTER_ROWS]] * reps, axis=1)
    r1 = jnp.sum(oh[:ROUTER_ROWS] * (base1 + pre[:ROUTER_ROWS]), axis=0, keepdims=True)
    r2 = jnp.sum(oh[ROUTER_ROWS:] * (base2 + pre[ROUTER_ROWS:]), axis=0, keepdims=True)
    carry = carry + tots[:ROUTER_ROWS] + tots[ROUTER_ROWS:]
    carry_ref[...] = carry
    cnt_ref[...] = carry
    e1 = (i1 - MOE_GROUPS).astype(F32)
    e2 = (i2 - MOE_GROUPS).astype(F32)
    slot_rows = [e1, e2, r1, r2, g1, g2]
    rows8 = lax.broadcasted_iota(I32, (SUBLANES, tm), 0)
    slot = jnp.zeros((SUBLANES, tm), F32)
    for j, val in enumerate(slot_rows):
        slot = jnp.where(rows8 == j, val, slot)
    slot_ref[...] = slot
    wide = jnp.concatenate([slot, jnp.zeros((LANES - SUBLANES, tm), F32)], axis=0)
    info_ref[...] = jnp.transpose(wide)


def _router(x, g, wt_router, bt_router, tm=256):
    T, D = x.shape
    NT = T // tm
    return pl.pallas_call(
        _router_kernel,
        grid=(NT,),
        in_specs=[pl.BlockSpec((tm, D), lambda i: (i, 0)),
                  pl.BlockSpec((1, D), lambda i: (0, 0)),
                  pl.BlockSpec((LANES, D), lambda i: (0, 0)),
                  pl.BlockSpec((LANES, tm), lambda i: (0, 0))],
        out_specs=[pl.BlockSpec((tm, LANES), lambda i: (i, 0)),
                   pl.BlockSpec((None, SUBLANES, tm), lambda i: (i, 0, 0)),
                   pl.BlockSpec((ROUTER_ROWS, LANES), lambda i: (0, 0))],
        out_shape=[jax.ShapeDtypeStruct((T, LANES), F32),
                   jax.ShapeDtypeStruct((NT, SUBLANES, tm), F32),
                   jax.ShapeDtypeStruct((ROUTER_ROWS, LANES), F32)],
        scratch_shapes=[pltpu.VMEM((ROUTER_ROWS, LANES), F32)],
        compiler_params=_cparams(1),
        name="router",
    )(x, g.reshape(1, D), wt_router, bt_router)


def _row_bytes_wait(hbm, buf, sem):
    pltpu.make_async_copy(buf, hbm.at[pl.ds(0, buf.shape[0]), :], sem).wait()


def _to_row_tiles(ref, val):
    n = val.shape[0]
    for c in range(SUBLANES):
        ref[pl.ds(c, n, stride=SUBLANES), :] = val[:, c * LANES:(c + 1) * LANES]


def _from_row_tiles(ref):
    n = ref.shape[0] // SUBLANES
    return jnp.concatenate([ref[pl.ds(c, n, stride=SUBLANES), :] for c in range(SUBLANES)], axis=1)


def _moe_dispatch_kernel(pends_ref, cnt_ref, dest_ref, x_ref, g_ref, hs_hbm, hbuf, zbuf, sems, zsem, *, td):
    i = pl.program_id(0)
    nt = pl.num_programs(0)
    slot = lax.rem(i, 2)

    @pl.when(i == 0)
    def _():
        zbuf[...] = jnp.zeros_like(zbuf)
        for e in range(MOE_EXPERTS):
            @pl.when(cnt_ref[e] > 0)
            def _():
                start = pl.multiple_of((pends_ref[e] - MOE_BLOCK) * SUBLANES, MOE_BLOCK)
                pltpu.make_async_copy(zbuf, hs_hbm.at[pl.ds(start, MOE_BLOCK * SUBLANES), :], zsem).start()
        for e in range(MOE_EXPERTS):
            @pl.when(cnt_ref[e] > 0)
            def _():
                pltpu.make_async_copy(zbuf, hs_hbm.at[pl.ds(0, MOE_BLOCK * SUBLANES), :], zsem).wait()

        first_unused = pends_ref[MOE_EXPERTS - 1] // MOE_BLOCK
        n_blocks = hs_hbm.shape[0] // (MOE_BLOCK * SUBLANES)

        def zero_start(blk, carry):
            start = pl.multiple_of(blk * (MOE_BLOCK * SUBLANES), MOE_BLOCK)
            pltpu.make_async_copy(zbuf, hs_hbm.at[pl.ds(start, MOE_BLOCK * SUBLANES), :], zsem).start()
            return carry

        def zero_wait(blk, carry):
            pltpu.make_async_copy(zbuf, hs_hbm.at[pl.ds(0, MOE_BLOCK * SUBLANES), :], zsem).wait()
            return carry

        lax.fori_loop(first_unused, n_blocks, zero_start, 0)
        lax.fori_loop(first_unused, n_blocks, zero_wait, 0)

    hb = hbuf.at[slot]
    _to_row_tiles(hb, _rms(x_ref[...], g_ref[...]))
    for j in range(td):
        for c in range(2):
            row = pl.multiple_of(dest_ref[0, c * td + j] * SUBLANES, SUBLANES)
            pltpu.make_async_copy(hb.at[pl.ds(j * SUBLANES, SUBLANES), :],
                                  hs_hbm.at[pl.ds(row, SUBLANES), :],
                                  sems.at[slot]).start(priority=c)

    @pl.when(i > 0)
    def _():
        other = hbuf.at[1 - slot]
        _row_bytes_wait(hs_hbm, other, sems.at[1 - slot])
        _row_bytes_wait(hs_hbm, other, sems.at[1 - slot])

    @pl.when(i == nt - 1)
    def _():
        _row_bytes_wait(hs_hbm, hb, sems.at[slot])
        _row_bytes_wait(hs_hbm, hb, sems.at[slot])


def _moe_dispatch(x, g, pends, counts, dest3, P, td):
    T, D = x.shape
    assert D == ROW_TILE
    grid_spec = pltpu.PrefetchScalarGridSpec(
        num_scalar_prefetch=2,
        grid=(T // td,),
        in_specs=[pl.BlockSpec((None, 1, 2 * td), lambda i, pe, cn: (i, 0, 0), memory_space=pltpu.SMEM),
                  pl.BlockSpec((td, D), lambda i, pe, cn: (i, 0)),
                  pl.BlockSpec((1, D), lambda i, pe, cn: (0, 0))],
        out_specs=pl.BlockSpec(memory_space=pl.ANY),
        scratch_shapes=[pltpu.VMEM((2, td * SUBLANES, LANES), F32),
                        pltpu.VMEM((MOE_BLOCK * SUBLANES, LANES), F32),
                        pltpu.SemaphoreType.DMA((2,)),
                        pltpu.SemaphoreType.DMA(())],
    )
    return pl.pallas_call(
        functools.partial(_moe_dispatch_kernel, td=td),
        grid_spec=grid_spec,
        out_shape=jax.ShapeDtypeStruct((P * SUBLANES, LANES), F32),
        compiler_params=_cparams(1),
        name="moe_dispatch",
    )(pends, counts, dest3, x, g.reshape(1, D))


def _moe_expert_kernel(be_ref, nu_ref, hs_ref, w1_ref, w3_ref, w2_ref, o_ref, w1b, w3b, w2b):
    i = pl.program_id(0)
    used = i < nu_ref[0]
    changed = jnp.logical_or(i == 0, be_ref[i] != be_ref[jnp.maximum(i - 1, 0)])

    @pl.when(jnp.logical_and(used, changed))
    def _():
        w1b[...] = w1_ref[...].astype(BF16)
        w3b[...] = w3_ref[...].astype(BF16)
        w2b[...] = w2_ref[...].astype(BF16)

    @pl.when(used)
    def _():
        xe = _from_row_tiles(hs_ref).astype(BF16)
        ff = w1b.shape[1]
        halves = [slice(0, ff // 2), slice(ff // 2, ff)]
        ups = [(jnp.dot(xe, w1b[:, sl], preferred_element_type=F32),
                jnp.dot(xe, w3b[:, sl], preferred_element_type=F32)) for sl in halves]
        act = [(a * _sigmoid(a) * b).astype(BF16) for a, b in ups]
        y = sum(jnp.dot(a, w2b[sl, :], preferred_element_type=F32) for a, sl in zip(act, halves))
        _to_row_tiles(o_ref, y)

    @pl.when(jnp.logical_not(used))
    def _():
        o_ref[...] = jnp.zeros_like(o_ref)


def _moe_experts(hs, block_e, n_used, w1, w3, w2, layer):
    P = hs.shape[0] // SUBLANES
    D = ROW_TILE
    FF = w1.shape[-1]
    NB = P // MOE_BLOCK
    last = lambda i, nu: jnp.minimum(i, nu[0] - 1)
    grid_spec = pltpu.PrefetchScalarGridSpec(
        num_scalar_prefetch=2,
        grid=(NB,),
        in_specs=[pl.BlockSpec((MOE_BLOCK * SUBLANES, LANES), lambda i, be, nu: (last(i, nu), 0)),
                  pl.BlockSpec((None, None, D, FF), lambda i, be, nu: (layer, be[last(i, nu)], 0, 0)),
                  pl.BlockSpec((None, None, D, FF), lambda i, be, nu: (layer, be[last(i, nu)], 0, 0)),
                  pl.BlockSpec((None, None, FF, D), lambda i, be, nu: (layer, be[last(i, nu)], 0, 0))],
        out_specs=pl.BlockSpec((MOE_BLOCK * SUBLANES, LANES), lambda i, be, nu: (i, 0)),
        scratch_shapes=[pltpu.VMEM((D, FF), BF16),
                        pltpu.VMEM((D, FF), BF16),
                        pltpu.VMEM((FF, D), BF16)],
    )
    return pl.pallas_call(
        _moe_expert_kernel,
        grid_spec=grid_spec,
        out_shape=jax.ShapeDtypeStruct((P * SUBLANES, LANES), F32),
        compiler_params=_cparams(1),
        name="moe_experts",
    )(block_e, n_used, hs, w1, w3, w2)


def _gather_rows(src_hbm, idx_ref, dst_ref, sem, n_rows):
    for r in range(n_rows):
        row = pl.multiple_of(idx_ref[0, r] * SUBLANES, SUBLANES)
        pltpu.make_async_copy(src_hbm.at[pl.ds(row, SUBLANES), :],
                              dst_ref.at[pl.ds(r * SUBLANES, SUBLANES), :], sem).start(priority=r % 2)


def _moe_combine_kernel(pos_ref, posn_ref, x_ref, info_ref, yb_hbm, g_ref, *rest, tc, final_norm, splits):
    if splits:
        w_ref, o_ref = rest[0], rest[1]
        p_refs = rest[2:2 + len(splits)]
        ybuf, sems = rest[2 + len(splits):]
    else:
        o_ref, ybuf, sems = rest
    i = pl.program_id(0)
    nb = pl.num_programs(0)
    slot = lax.rem(i, 2)

    @pl.when(i == 0)
    def _():
        def issue(r, carry):
            src = pl.multiple_of(pos_ref[0, r] * SUBLANES, SUBLANES)
            dst = pl.multiple_of(r * SUBLANES, SUBLANES)
            pltpu.make_async_copy(yb_hbm.at[pl.ds(src, SUBLANES), :],
                                  ybuf.at[0, pl.ds(dst, SUBLANES), :], sems.at[0]).start()
            return carry
        lax.fori_loop(0, 2 * tc, issue, 0)

    @pl.when(i + 1 < nb)
    def _():
        _gather_rows(yb_hbm, posn_ref, ybuf.at[1 - slot], sems.at[1 - slot], 2 * tc)

    pltpu.make_async_copy(yb_hbm.at[pl.ds(0, 2 * tc * SUBLANES), :], ybuf.at[slot], sems.at[slot]).wait()
    info = info_ref[...]
    yb = ybuf.at[slot]
    y0 = _from_row_tiles(yb.at[pl.ds(0, tc * SUBLANES), :])
    y1 = _from_row_tiles(yb.at[pl.ds(tc * SUBLANES, tc * SUBLANES), :])
    out = x_ref[...] + (y0 * info[:, 4:5] + y1 * info[:, 5:6])
    if final_norm:
        out = _rms(out, g_ref[...])
    o_ref[...] = out
    if splits:
        h = _rms(out, g_ref[...]).astype(BF16)
        off = 0
        for p_ref, n in zip(p_refs, splits):
            p_ref[...] = jnp.dot(h, w_ref[:, off:off + n], preferred_element_type=F32).astype(p_ref.dtype)
            off += n


def _moe_combine(x, info, dest3, yb, g, final_norm, tc, next_proj=None):
    T, D = x.shape
    NT = T // tc
    in_specs = [pl.BlockSpec((None, 1, 2 * tc), lambda i: (i, 0, 0), memory_space=pltpu.SMEM),
                pl.BlockSpec((None, 1, 2 * tc), lambda i: (jnp.minimum(i + 1, NT - 1), 0, 0),
                             memory_space=pltpu.SMEM),
                pl.BlockSpec((tc, D), lambda i: (i, 0)),
                pl.BlockSpec((tc, LANES), lambda i: (i, 0)),
                pl.BlockSpec(memory_space=pl.ANY),
                pl.BlockSpec((1, D), lambda i: (0, 0))]
    out_specs = [pl.BlockSpec((tc, D), lambda i: (i, 0))]
    out_shape = [jax.ShapeDtypeStruct((T, D), F32)]
    args = [dest3, dest3, x, info, yb, g.reshape(1, D)]
    splits = ()
    if next_proj is not None:
        w, splits, out_dtypes = next_proj
        assert not final_norm and sum(splits) == w.shape[1]
        in_specs.append(pl.BlockSpec(w.shape, lambda i: (0, 0)))
        args.append(w)
        out_specs += [pl.BlockSpec((tc, n), lambda i: (i, 0)) for n in splits]
        out_shape += [jax.ShapeDtypeStruct((T, n), dt) for n, dt in zip(splits, out_dtypes)]
    outs = pl.pallas_call(
        functools.partial(_moe_combine_kernel, tc=tc, final_norm=final_norm, splits=tuple(splits)),
        grid=(NT,),
        in_specs=in_specs,
        out_specs=out_specs,
        out_shape=out_shape,
        scratch_shapes=[pltpu.VMEM((2, 2 * tc * SUBLANES, LANES), F32), pltpu.SemaphoreType.DMA((2,))],
        compiler_params=_cparams(1),
        name="moe_combine",
    )(*args)
    return outs[0], tuple(outs[1:])


MOE_TILE = 256
MOE_COMBINE_TILE = 256
ROUTER_TILE = 256


def _tile_slots(dest, tile):
    n_tiles = dest.shape[0] * dest.shape[2] // tile
    return jnp.concatenate([dest[:, 0, :].reshape(n_tiles, 1, tile), dest[:, 1, :].reshape(n_tiles, 1, tile)], axis=2)


def _moe_layer(x, g, w_group, b_group, w_expert, b_expert, w1, w3, w2, layer, g_out, final_norm, next_proj):
    T, D = x.shape
    n_log = MOE_GROUPS + MOE_EXPERTS
    tm = ROUTER_TILE
    wt_router = jnp.zeros((LANES, D), F32).at[:MOE_GROUPS].set(w_group.T).at[MOE_GROUPS:n_log].set(w_expert.T)
    bt_router = jnp.zeros((LANES,), F32).at[:MOE_GROUPS].set(b_group).at[MOE_GROUPS:n_log].set(b_expert)
    info, slot, cnt = _router(x, g, wt_router.astype(BF16), jnp.broadcast_to(bt_router[:, None], (LANES, tm)), tm)
    P = 2 * T + MOE_EXPERTS * MOE_BLOCK
    NB = P // MOE_BLOCK
    counts = cnt[MOE_GROUPS:n_log, 0].astype(I32)
    padded = (counts + MOE_BLOCK - 1) // MOE_BLOCK * MOE_BLOCK
    pends = jnp.cumsum(padded).astype(I32)
    pstarts = pends - padded
    block_start = jnp.arange(NB, dtype=I32) * MOE_BLOCK
    block_e = jnp.minimum(jnp.sum((pends[None, :] <= block_start[:, None]).astype(I32), axis=1),
                          MOE_EXPERTS - 1).astype(I32)
    n_used = (pends[-1:] // MOE_BLOCK).astype(I32)
    eid = slot[:, 0:2, :].astype(I32)
    expert_ids = jnp.arange(MOE_EXPERTS, dtype=I32)
    seg_start = jnp.sum(jnp.where(eid[..., None] == expert_ids, pstarts, 0), axis=-1)
    dest = seg_start + slot[:, 2:4, :].astype(I32)
    hs = _moe_dispatch(x, g, pends, counts, _tile_slots(dest, MOE_TILE), P, MOE_TILE)
    yb = _moe_experts(hs, block_e, n_used, w1, w3, w2, layer)
    return _moe_combine(x, info, _tile_slots(dest, MOE_COMBINE_TILE), yb, g_out, final_norm, MOE_COMBINE_TILE,
                        next_proj)


def kernel(x, mem, norm_mix, norm_xattn, norm_moe, norm_final, ev_w_in, ev_sinks, ev_mu, ev_w0, ev_w2, ev_a0, ev_a2, ev_g2, ev_k_k, ev_k_a, ev_r_k, ev_lnx_w, ev_lnx_b, ev_w_out, od_w_in, od_gate_up, od_gate_b, od_onorm, od_w_out, mem_norm, mem_wk, mem_wv, xa_wq, xa_wo, moe_w_group, moe_b_group, moe_w_expert, moe_b_expert, moe_w1, moe_w3, moe_w2):
    B, S, D = x.shape
    M = mem.shape[1]
    T = B * S
    depth = norm_mix.shape[0]
    xf = x.reshape(T, D)

    XW = XA_HEADS * XA_HEAD_DIM
    w_kv = jnp.concatenate([mem_wk, mem_wv], axis=1).astype(BF16)
    mk, mv = _norm_matmul(mem.reshape(B * M, D), mem_norm, w_kv, (XW, XW), (BF16, BF16))

    KW = GLA_HEADS * GLA_DK
    VW = GLA_HEADS * GLA_DV
    swa_cols = SWA_Q_HEADS * HEAD_DIM + 2 * (SWA_Q_HEADS // SWA_GROUP) * HEAD_DIM

    def in_proj(layer):
        i = layer // 2
        if layer % 2 == 0:
            return ev_w_in[i].astype(BF16), (swa_cols, ev_w_in.shape[-1] - swa_cols), (F32, F32)
        R = od_gate_up.shape[1]
        w = od_w_in[i]
        w_re = jnp.concatenate([w[:, :2 * KW + VW], w[:, 2 * KW + VW + R:],
                                w[:, 2 * KW + VW:2 * KW + VW + R],
                                jnp.zeros((D, LANES - R), F32)], axis=1).astype(BF16)
        return w_re, (2 * KW + 2 * VW, LANES), (F32, F32)

    w_first, _, _ = in_proj(0)
    qkv0, rw0 = _in_proj_rwkv(xf, norm_mix[0], w_first, swa_cols, ev_mu[0], ev_w0[0], ev_w2[0], ev_a0[0], ev_a2[0],
                              ev_g2[0], ev_k_k[0], ev_k_a[0], S)
    proj = None
    for layer in range(depth):
        i = layer // 2
        if layer % 2 == 0:
            if layer == 0:
                qkv, rw = qkv0, [t.reshape(B, S, RWKV_WIDTH) for t in rw0]
            else:
                qkv, p_rw = proj
                rw = _rwkv_prep(p_rw, ev_mu[i], ev_w0[i], ev_w2[i], ev_a0[i], ev_a2[i],
                                ev_g2[i], ev_k_k[i], ev_k_a[i], B, S)
            o_a = _swa(qkv, ev_sinks[i], B, S)
            o_b = _rwkv_scan(*rw, ev_r_k[i].reshape(-1), ev_lnx_w[i], ev_lnx_b[i])
            w_out = ev_w_out[i].astype(BF16)
            qw = o_a.shape[-1]
            mix_acts, mix_ws = [o_a, o_b], [w_out[:qw], w_out[qw:]]
        else:
            qkvo, gd = proj
            R = od_gate_up.shape[1]
            gup = jnp.zeros((LANES, KW), F32).at[:R].set(od_gate_up[i]).astype(BF16)
            o = _gla(qkvo, gd, gup, od_gate_b[i], od_onorm[i], B, S)
            mix_acts, mix_ws = [o], [od_w_out[i].astype(BF16)]
        xf = _mix_proj_xattn(xf, mix_acts, mix_ws, norm_xattn[layer], xa_wq[layer].astype(BF16), mk, mv,
                             xa_wo[layer].astype(BF16), B, S)
        last = layer == depth - 1
        g_out = norm_final if last else norm_mix[layer + 1]
        xf, proj = _moe_layer(xf, norm_moe[layer], moe_w_group[layer], moe_b_group[layer], moe_w_expert[layer],
                              moe_b_expert[layer], moe_w1, moe_w3, moe_w2, layer,
                              g_out, last, None if last else in_proj(layer + 1))
    return xf.reshape(B, S, D)
```

```python
import functools

import jax
import jax.numpy as jnp
from jax import lax
from jax.experimental import pallas as pl
from jax.experimental.pallas import tpu as pltpu

F32 = jnp.float32
BF16 = jnp.bfloat16
I32 = jnp.int32

EPS = 1e-6
HEAD_DIM = 64
SWA_WINDOW = 128
SWA_Q_HEADS = 8
SWA_GROUP = 4
RWKV_HEADS = 8
RWKV_WIDTH = 512
RWKV_LN_EPS = 64e-5
RWKV_CHUNK = 64
GLA_HEADS = 4
GLA_DK = 128
GLA_DV = 256
GLA_CHUNK = 64
GLA_GATE_NORM = 16.0
XA_HEADS = 4
XA_HEAD_DIM = 128
MOE_GROUPS = 4
MOE_EXPERTS_PER_GROUP = 8
MOE_EXPERTS = 32
MOE_BLOCK = 512
LANES = 128
SUBLANES = 8
ROW_TILE = SUBLANES * LANES

VMEM_LIMIT_BYTES = 48 * 1024 * 1024


def _cparams(n_axes):
    return pltpu.CompilerParams(dimension_semantics=("arbitrary",) * n_axes,
                                vmem_limit_bytes=VMEM_LIMIT_BYTES)


def _dot(a, b):
    return jnp.dot(a.astype(BF16), b.astype(BF16), preferred_element_type=F32)


def _dot_nt(a, b):
    return lax.dot_general(a.astype(BF16), b.astype(BF16), (((1,), (1,)), ((), ())),
                           preferred_element_type=F32)


def _dot_tn(a, b):
    return lax.dot_general(a.astype(BF16), b.astype(BF16), (((0,), (0,)), ((), ())),
                           preferred_element_type=F32)


def _dot_f32(a, b):
    return jnp.dot(a, b, preferred_element_type=F32, precision=lax.Precision.HIGHEST)


def _rms(x, g):
    ms = jnp.mean(x * x, axis=-1, keepdims=True)
    return x * lax.rsqrt(ms + EPS) * g


def _sigmoid(x):
    return 1.0 / (1.0 + jnp.exp(-x))


def _softplus(x):
    return jnp.maximum(x, 0.0) + jnp.log(1.0 + jnp.exp(-jnp.abs(x)))


def _norm_matmul_kernel(x_ref, g_ref, w_ref, *o_refs, splits):
    h = _rms(x_ref[...], g_ref[...]).astype(BF16)
    off = 0
    for o_ref, n in zip(o_refs, splits):
        o_ref[...] = jnp.dot(h, w_ref[:, off:off + n], preferred_element_type=F32).astype(o_ref.dtype)
        off += n


def _norm_matmul(x, g, w, splits, out_dtypes, tm=256):
    T, D = x.shape
    N = w.shape[1]
    assert sum(splits) == N and T % tm == 0
    return pl.pallas_call(
        functools.partial(_norm_matmul_kernel, splits=tuple(splits)),
        grid=(T // tm,),
        in_specs=[pl.BlockSpec((tm, D), lambda i: (i, 0)),
                  pl.BlockSpec((1, D), lambda i: (0, 0)),
                  pl.BlockSpec((D, N), lambda i: (0, 0))],
        out_specs=[pl.BlockSpec((tm, n), lambda i: (i, 0)) for n in splits],
        out_shape=[jax.ShapeDtypeStruct((T, n), dt) for n, dt in zip(splits, out_dtypes)],
        compiler_params=_cparams(1),
        name="norm_matmul",
    )(x, g.reshape(1, D), w)


def _swa_kernel(sinks_ref, q_ref, kp_ref, kc_ref, vp_ref, vc_ref, o_ref):
    n = pl.program_id(1)
    W = SWA_WINDOW
    NB = q_ref.shape[0]
    qpos = lax.broadcasted_iota(I32, (W, 2 * W), 0) + W
    kpos = lax.broadcasted_iota(I32, (W, 2 * W), 1)
    rel = qpos - kpos
    in_window = jnp.where(rel >= 0, jnp.where(rel < W, 1, 0), 0)
    has_prev = jnp.where(n > 0, 1, 0)
    valid = (in_window * jnp.where(kpos >= W, 1, has_prev)) > 0
    n_groups = SWA_Q_HEADS // SWA_GROUP
    streams = [(bi, g) for bi in range(NB) for g in range(n_groups)]
    qb = [q_ref[bi].astype(BF16) for bi in range(NB)]
    kb = [jnp.concatenate([kp_ref[bi], kc_ref[bi]], axis=0).astype(BF16) for bi in range(NB)]
    vb = [jnp.concatenate([vp_ref[bi], vc_ref[bi]], axis=0).astype(BF16) for bi in range(NB)]
    gs = lambda g: slice(g * HEAD_DIM, (g + 1) * HEAD_DIM)
    scores = []
    for bi, g in streams:
        qg = jnp.concatenate([qb[bi][:, h * HEAD_DIM:(h + 1) * HEAD_DIM]
                              for h in range(g * SWA_GROUP, (g + 1) * SWA_GROUP)], axis=0)
        scores.append(_dot_nt(qg, kb[bi][:, gs(g)]))
    probs = []
    for i, (bi, g) in enumerate(streams):
        pieces = []
        for j in range(SWA_GROUP):
            s = jnp.where(valid, scores[i][j * W:(j + 1) * W] * (HEAD_DIM ** -0.5), -jnp.inf)
            sink = sinks_ref[g * SWA_GROUP + j]
            m = jnp.maximum(jnp.max(s, axis=-1, keepdims=True), sink)
            p = jnp.exp(s - m)
            den = jnp.sum(p, axis=-1, keepdims=True) + jnp.exp(sink - m)
            pieces.append((p / den).astype(BF16))
        probs.append(jnp.concatenate(pieces, axis=0))
    ogs = [_dot(probs[i], vb[bi][:, gs(g)]) for i, (bi, g) in enumerate(streams)]
    for bi in range(NB):
        outs = []
        for g in range(n_groups):
            og = ogs[bi * n_groups + g]
            outs += [og[j * W:(j + 1) * W] for j in range(SWA_GROUP)]
        o_ref[bi] = jnp.concatenate(outs, axis=1).astype(o_ref.dtype)


SWA_BATCH_ROWS = 2


def _swa(qkv, sinks, B, S):
    W = SWA_WINDOW
    qkv3 = qkv.reshape(B, S, qkv.shape[-1])
    qw = SWA_Q_HEADS * HEAD_DIM
    kw = qw // SWA_GROUP
    kcol = qw // kw
    nb = SWA_BATCH_ROWS if B % SWA_BATCH_ROWS == 0 else 1
    out = pl.pallas_call(
        _swa_kernel,
        grid=(B // nb, S // W),
        in_specs=[pl.BlockSpec(memory_space=pltpu.SMEM),
                  pl.BlockSpec((nb, W, qw), lambda b, n: (b, n, 0)),
                  pl.BlockSpec((nb, W, kw), lambda b, n: (b, jnp.maximum(n - 1, 0), kcol)),
                  pl.BlockSpec((nb, W, kw), lambda b, n: (b, n, kcol)),
                  pl.BlockSpec((nb, W, kw), lambda b, n: (b, jnp.maximum(n - 1, 0), kcol + 1)),
                  pl.BlockSpec((nb, W, kw), lambda b, n: (b, n, kcol + 1))],
        out_specs=pl.BlockSpec((nb, W, qw), lambda b, n: (b, n, 0)),
        out_shape=jax.ShapeDtypeStruct((B, S, qw), BF16),
        compiler_params=_cparams(2),
        name="swa",
    )(sinks, qkv3, qkv3, qkv3, qkv3, qkv3)
    return out.reshape(B * S, qw)


def _rwkv_prep_math(p, last, mu_ref, w0_ref, w2_ref, a0_ref, a2_ref, g2_ref, kk_ref, ka_ref, outs):
    r_out, lw_out, k_out, v_out, a_out, b_out, g_out = outs
    C = RWKV_WIDTH
    row = lax.broadcasted_iota(I32, p.shape, 0)
    p_prev = jnp.where(row == 0, last, pltpu.roll(p, 1, axis=0))
    p = p + (p_prev - p) * mu_ref[...]
    r = p[:, :C]
    k = p[:, C:2 * C]
    v = p[:, 2 * C:3 * C]
    xw = p[:, 3 * C:3 * C + 64]
    xa = p[:, 3 * C + 64:3 * C + 128]
    xg = p[:, 3 * C + 128:]
    w = -_softplus(-(w0_ref[...] + _dot(jnp.tanh(xw), w2_ref[...]))) - 0.5
    lw = -jnp.exp(w)
    a = _sigmoid(a0_ref[...] + _dot(xa, a2_ref[...]))
    g = _dot(_sigmoid(xg), g2_ref[...])
    kk = k * kk_ref[...]
    pieces = []
    for h in range(RWKV_HEADS):
        kh = kk[:, h * HEAD_DIM:(h + 1) * HEAD_DIM]
        nrm = jnp.sqrt(jnp.sum(kh * kh, axis=-1, keepdims=True))
        pieces.append(kh / jnp.maximum(nrm, 1e-12))
    kk = jnp.concatenate(pieces, axis=1)
    r_out[...] = r
    lw_out[...] = lw
    k_out[...] = k * (1.0 + (a - 1.0) * ka_ref[...])
    v_out[...] = v
    a_out[...] = -kk
    b_out[...] = kk * a
    g_out[...] = g


def _rwkv_prep_kernel(p_ref, pprev_ref, *refs):
    n = pl.program_id(1)
    last = jnp.where(n > 0, pprev_ref[7:8, :], 0.0)
    _rwkv_prep_math(p_ref[...], last, *refs[:8], refs[8:])


def _rwkv_params(mu, w0, w2, a0, a2, g2, k_k, k_a):
    row = lambda t: t.reshape(1, -1)
    return [row(mu), row(w0), w2.astype(BF16), row(a0), a2.astype(BF16), g2.astype(BF16), row(k_k), row(k_a)]


def _rwkv_prep(p, mu, w0, w2, a0, a2, g2, k_k, k_a, B, S, tt=256):
    C = RWKV_WIDTH
    PW = p.shape[-1]
    p3 = p.reshape(B, S, PW)
    full = lambda arr: pl.BlockSpec(arr.shape, lambda b, n: (0,) * arr.ndim)
    params = _rwkv_params(mu, w0, w2, a0, a2, g2, k_k, k_a)
    outs = pl.pallas_call(
        _rwkv_prep_kernel,
        grid=(B, S // tt),
        in_specs=[pl.BlockSpec((None, tt, PW), lambda b, n: (b, n, 0)),
                  pl.BlockSpec((None, 8, PW), lambda b, n: (b, jnp.maximum(n * (tt // 8) - 1, 0), 0))]
                 + [full(t) for t in params],
        out_specs=[pl.BlockSpec((None, tt, C), lambda b, n: (b, n, 0))] * 7,
        out_shape=[jax.ShapeDtypeStruct((B, S, C), F32)] * 7,
        compiler_params=_cparams(2),
        name="rwkv_prep",
    )(p3, p3, *params)
    return outs


def _in_proj_rwkv_kernel(x_ref, g_ref, w_ref, *refs, swa_cols, tiles_per_seq):
    params = refs[:8]
    qkv_out = refs[8]
    outs = refs[9:16]
    p_buf, last_buf = refs[16:]
    i = pl.program_id(0)

    @pl.when(i == 0)
    def _():
        p_buf[...] = jnp.zeros_like(p_buf)
        last_buf[...] = jnp.zeros_like(last_buf)

    j = i - 1
    p_prev_tile = p_buf[lax.rem(i + 1, 2)]
    tm = p_prev_tile.shape[0]
    last = jnp.where(lax.rem(j, tiles_per_seq) == 0, 0.0, last_buf[...])
    _rwkv_prep_math(p_prev_tile, last, *params, outs)
    last_buf[...] = p_prev_tile[tm - 1:tm, :]
    h = _rms(x_ref[...], g_ref[...]).astype(BF16)
    qkv_out[...] = jnp.dot(h, w_ref[:, :swa_cols], preferred_element_type=F32)
    p_buf[lax.rem(i, 2)] = jnp.dot(h, w_ref[:, swa_cols:], preferred_element_type=F32)


def _in_proj_rwkv(x, g, w, swa_cols, mu, w0, w2, a0, a2, g2, k_k, k_a, S, tm=256):
    T, D = x.shape
    N = w.shape[1]
    C = RWKV_WIDTH
    NT = T // tm
    params = _rwkv_params(mu, w0, w2, a0, a2, g2, k_k, k_a)
    const = lambda arr: pl.BlockSpec(arr.shape, lambda i: (0,) * arr.ndim)
    cur = lambda i: (jnp.minimum(i, NT - 1), 0)
    prev = lambda i: (jnp.maximum(i - 1, 0), 0)
    outs = pl.pallas_call(
        functools.partial(_in_proj_rwkv_kernel, swa_cols=swa_cols, tiles_per_seq=S // tm),
        grid=(NT + 1,),
        in_specs=[pl.BlockSpec((tm, D), cur), pl.BlockSpec((1, D), lambda i: (0, 0)), const(w)]
                 + [const(t) for t in params],
        out_specs=[pl.BlockSpec((tm, swa_cols), cur)] + [pl.BlockSpec((tm, C), prev)] * 7,
        out_shape=[jax.ShapeDtypeStruct((T, swa_cols), F32)] + [jax.ShapeDtypeStruct((T, C), F32)] * 7,
        scratch_shapes=[pltpu.VMEM((2, tm, N - swa_cols), F32), pltpu.VMEM((1, N - swa_cols), F32)],
        compiler_params=_cparams(1),
        name="in_proj_rwkv",
    )(x, g.reshape(1, D), w, *params)
    return outs[0], outs[1:]


def _pair_blockdiag(x):
    lane = lax.broadcasted_iota(I32, x.shape, 1)
    zero = jnp.zeros_like(x)
    return jnp.concatenate([jnp.where(lane < HEAD_DIM, x, zero), jnp.where(lane >= HEAD_DIM, x, zero)], axis=0)


def _rwkv_scan_kernel(r_ref, lw_ref, k_ref, v_ref, a_ref, b_ref, g_ref, rk_ref, lnw_ref, lnb_ref,
                      o_ref, s_ref):
    c = pl.program_id(1)

    @pl.when(c == 0)
    def _():
        s_ref[...] = jnp.zeros_like(s_ref)

    C = RWKV_CHUNK
    NB = r_ref.shape[0]
    NP = RWKV_HEADS // 2
    PW = 2 * HEAD_DIM
    row = lax.broadcasted_iota(I32, (C, C), 0)
    col = lax.broadcasted_iota(I32, (C, C), 1)
    tri = jnp.where(row >= col, 1.0, 0.0).astype(F32)
    rowp = lax.broadcasted_iota(I32, (C, PW), 0)
    colp = lax.broadcasted_iota(I32, (C, PW), 1)
    colp = jnp.where(colp >= HEAD_DIM, colp - HEAD_DIM, colp)
    lower_p = rowp >= colp
    strict_p = rowp > colp
    rows = lax.broadcasted_iota(I32, (PW, PW), 0)
    cols = lax.broadcasted_iota(I32, (PW, PW), 1)
    same_head = jnp.where(rows >= HEAD_DIM, 1, 0) == jnp.where(cols >= HEAD_DIM, 1, 0)
    first = lax.broadcasted_iota(I32, (C, PW), 1) < HEAD_DIM

    streams = [(bi, p) for bi in range(NB) for p in range(NP)]
    pre = []
    for bi in range(NB):
        lw = lw_ref[bi]
        cum = _dot_f32(tri, lw)
        cum_last = cum[C - 1:C, :]
        r = r_ref[bi]
        k = k_ref[bi]
        v = v_ref[bi]
        a = a_ref[bi]
        b = b_ref[bi]
        e_neg = jnp.exp(-cum)
        e_rem = jnp.exp(cum_last - cum)
        pre.append(dict(
            r_t=(r * jnp.exp(cum)).astype(BF16), a_t=(a * jnp.exp(cum - lw)).astype(BF16),
            b_t=(b * e_neg).astype(BF16), k_t=(k * e_neg).astype(BF16),
            b_d=(b * e_rem).astype(BF16), k_d=(k * e_rem).astype(BF16),
            v_b=v.astype(BF16), v=v, e_last=jnp.exp(cum_last), rkk=r * k * rk_ref[...], g=g_ref[bi]))

    def lanes(p):
        return slice(p * PW, (p + 1) * PW)

    ar = [jnp.concatenate([pre[bi]['a_t'][:, lanes(p)], pre[bi]['r_t'][:, lanes(p)]], axis=0) for bi, p in streams]
    s0 = [s_ref[bi, p] for bi, p in streams]
    big = [_dot_nt(ar[i], jnp.concatenate([_pair_blockdiag(pre[bi]['b_t'][:, lanes(p)]),
                                           _pair_blockdiag(pre[bi]['k_t'][:, lanes(p)]),
                                           s0[i].astype(BF16)], axis=0))
           for i, (bi, p) in enumerate(streams)]
    m_b = [t[:, :PW] for t in big]
    m_k = [t[:, PW:2 * PW] for t in big]
    ars = [t[:, 2 * PW:] for t in big]
    v_p = [pre[bi]['v_b'][:, lanes(p)] for bi, p in streams]
    v_bd = [_pair_blockdiag(vp) for vp in v_p]
    x = [ars[i][:C] + _dot(jnp.where(strict_p, m_k[i][:C], 0.0), v_bd[i]) for i in range(len(streams))]
    pw = [jnp.where(strict_p, m_b[i][:C], 0.0).astype(BF16) for i in range(len(streams))]
    n_stages = 6
    for stage in range(n_stages):
        if stage < n_stages - 1:
            prod = [_dot(pw[i], jnp.concatenate([_pair_blockdiag(x[i].astype(BF16)), _pair_blockdiag(pw[i])], axis=1))
                    for i in range(len(streams))]
            x = [x[i] + prod[i][:, :PW] for i in range(len(streams))]
            pw = [prod[i][:, PW:].astype(BF16) for i in range(len(streams))]
        else:
            x = [x[i] + _dot(pw[i], _pair_blockdiag(x[i].astype(BF16))) for i in range(len(streams))]
    u_b = [xi.astype(BF16) for xi in x]
    y = [ars[i][C:]
         + _dot(jnp.concatenate([jnp.where(lower_p, m_b[i][C:], 0.0), jnp.where(lower_p, m_k[i][C:], 0.0)], axis=1),
                jnp.concatenate([_pair_blockdiag(u_b[i]), v_bd[i]], axis=0))
         for i in range(len(streams))]
    for i, (bi, p) in enumerate(streams):
        upd = _dot_tn(jnp.concatenate([u_b[i], v_p[i]], axis=0),
                      jnp.concatenate([pre[bi]['b_d'][:, lanes(p)], pre[bi]['k_d'][:, lanes(p)]], axis=0))
        s_ref[bi, p] = s0[i] * pre[bi]['e_last'][:, lanes(p)] + jnp.where(same_head, upd, 0.0)

    lnw = lnw_ref[...]
    lnb = lnb_ref[...]

    def head_sum(t):
        s1 = jnp.sum(jnp.where(first, t, 0.0), axis=-1, keepdims=True)
        s2 = jnp.sum(jnp.where(first, 0.0, t), axis=-1, keepdims=True)
        return jnp.where(first, s1, s2)

    for bi in range(NB):
        outs = []
        for p in range(NP):
            yi = y[bi * NP + p]
            mean = head_sum(yi) * (1.0 / HEAD_DIM)
            yc = yi - mean
            var = head_sum(yc * yc) * (1.0 / HEAD_DIM)
            yn = yc * lax.rsqrt(var + RWKV_LN_EPS) * lnw[:, lanes(p)] + lnb[:, lanes(p)]
            bonus = head_sum(pre[bi]['rkk'][:, lanes(p)]) * pre[bi]['v'][:, lanes(p)]
            outs.append((yn + bonus) * pre[bi]['g'][:, lanes(p)])
        o_ref[bi] = jnp.concatenate(outs, axis=1).astype(o_ref.dtype)


RWKV_BATCH_ROWS = 4


def _rwkv_scan(r, lw, k, v, a, b, g, r_k, lnx_w, lnx_b):
    B, S, W = r.shape
    C = RWKV_CHUNK
    nb = RWKV_BATCH_ROWS if B % RWKV_BATCH_ROWS == 0 else 1
    seq = pl.BlockSpec((nb, C, W), lambda bb, c: (bb, c, 0))
    par = pl.BlockSpec((1, W), lambda bb, c: (0, 0))
    out = pl.pallas_call(
        _rwkv_scan_kernel,
        grid=(B // nb, S // C),
        in_specs=[seq] * 7 + [par] * 3,
        out_specs=seq,
        out_shape=jax.ShapeDtypeStruct((B, S, W), BF16),
        scratch_shapes=[pltpu.VMEM((nb, RWKV_HEADS // 2, 2 * HEAD_DIM, 2 * HEAD_DIM), F32)],
        compiler_params=_cparams(2),
        name="rwkv_scan",
    )(r, lw, k, v, a, b, g, r_k.reshape(1, W), lnx_w.reshape(1, W), lnx_b.reshape(1, W))
    return out.reshape(B * S, W)


def _gla_kernel(q_ref, k_ref, v_ref, og_ref, gd_ref, gup_ref, gb_ref, on_ref, o_ref, s_ref):
    c = pl.program_id(1)

    @pl.when(c == 0)
    def _():
        s_ref[...] = jnp.zeros_like(s_ref)

    C = GLA_CHUNK
    NB = q_ref.shape[0]
    row = lax.broadcasted_iota(I32, (C, C), 0)
    col = lax.broadcasted_iota(I32, (C, C), 1)
    lower = row >= col
    tri = jnp.where(lower, 1.0, 0.0).astype(F32)
    onorm = on_ref[...]
    zs = [_dot(gd_ref[bi], gup_ref[...]) + gb_ref[...] for bi in range(NB)]
    cums = [_dot_f32(tri, -_softplus(-z) / GLA_GATE_NORM) for z in zs]
    qe, ke, kd, e_last, v = [], [], [], [], []
    for bi in range(NB):
        cum = cums[bi]
        cum_last = cum[C - 1:C, :]
        k = k_ref[bi]
        qe.append((q_ref[bi] * (GLA_DK ** -0.5) * jnp.exp(cum)).astype(BF16))
        ke.append((k * jnp.exp(-cum)).astype(BF16))
        kd.append((k * jnp.exp(cum_last - cum)).astype(BF16))
        e_last.append(jnp.exp(cum_last))
        v.append(v_ref[bi].astype(BF16))
    streams = [(bi, h) for bi in range(NB) for h in range(GLA_HEADS)]
    ks = lambda h: slice(h * GLA_DK, (h + 1) * GLA_DK)
    vs = lambda h: slice(h * GLA_DV, (h + 1) * GLA_DV)
    sts = [s_ref[bi, h] for bi, h in streams]
    atts = [jnp.where(lower, _dot_nt(qe[bi][:, ks(h)], ke[bi][:, ks(h)]), 0.0) for bi, h in streams]
    inters = [_dot_nt(qe[bi][:, ks(h)], sts[i]) for i, (bi, h) in enumerate(streams)]
    os_ = [inters[i] + _dot(atts[i], v[bi][:, vs(h)]) for i, (bi, h) in enumerate(streams)]
    for i, (bi, h) in enumerate(streams):
        s_ref[bi, h] = sts[i] * e_last[bi][:, ks(h)] + _dot_tn(v[bi][:, vs(h)], kd[bi][:, ks(h)])
    for bi in range(NB):
        og = og_ref[bi]
        outs = []
        for h in range(GLA_HEADS):
            gate = og[:, vs(h)]
            outs.append(_rms(os_[bi * GLA_HEADS + h], onorm) * (gate * _sigmoid(gate)))
        o_ref[bi] = jnp.concatenate(outs, axis=1).astype(o_ref.dtype)


GLA_BATCH_ROWS = 4


def _gla(qkvo, gd, gate_up_pad, gate_b, onorm, B, S):
    C = GLA_CHUNK
    KW = GLA_HEADS * GLA_DK
    VW = GLA_HEADS * GLA_DV
    x3 = qkvo.reshape(B, S, qkvo.shape[-1])
    gd3 = gd.reshape(B, S, LANES)
    nb = GLA_BATCH_ROWS if B % GLA_BATCH_ROWS == 0 else 1
    out = pl.pallas_call(
        _gla_kernel,
        grid=(B // nb, S // C),
        in_specs=[pl.BlockSpec((nb, C, KW), lambda b, c: (b, c, 0)),
                  pl.BlockSpec((nb, C, KW), lambda b, c: (b, c, 1)),
                  pl.BlockSpec((nb, C, VW), lambda b, c: (b, c, 1)),
                  pl.BlockSpec((nb, C, VW), lambda b, c: (b, c, 2)),
                  pl.BlockSpec((nb, C, LANES), lambda b, c: (b, c, 0)),
                  pl.BlockSpec((LANES, KW), lambda b, c: (0, 0)),
                  pl.BlockSpec((1, KW), lambda b, c: (0, 0)),
                  pl.BlockSpec((1, GLA_DV), lambda b, c: (0, 0))],
        out_specs=pl.BlockSpec((nb, C, VW), lambda b, c: (b, c, 0)),
        out_shape=jax.ShapeDtypeStruct((B, S, VW), BF16),
        scratch_shapes=[pltpu.VMEM((nb, GLA_HEADS, GLA_DV, GLA_DK), F32)],
        compiler_params=_cparams(2),
        name="gla",
    )(x3, x3, x3, x3, gd3, gate_up_pad, gate_b.reshape(1, KW), onorm.reshape(1, GLA_DV))
    return out.reshape(B * S, VW)


def _xattn_kernel(*refs, n_in):
    x_ref = refs[0]
    a_refs = refs[1:1 + n_in]
    w_refs = refs[1 + n_in:1 + 2 * n_in]
    g_ref, wq_ref, mk_ref, mv_ref, wo_ref, gm_ref, wr_ref, br_ref, o_ref, lg_ref = refs[1 + 2 * n_in:]
    tq = x_ref.shape[0]
    subs = [slice(r, r + XA_SUB_ROWS) for r in range(0, tq, XA_SUB_ROWS)]
    xs = [x_ref[sub, :] for sub in subs]
    for a_ref, w_ref in zip(a_refs, w_refs):
        xs = [x + jnp.dot(a_ref[sub, :], w_ref[...], preferred_element_type=F32) for x, sub in zip(xs, subs)]
    qs = [_dot(_rms(x, g_ref[...]), wq_ref[...]).astype(BF16) for x in xs]
    mk = mk_ref[...]
    mv = mv_ref[...]
    sls = [slice(hd * XA_HEAD_DIM, (hd + 1) * XA_HEAD_DIM) for hd in range(XA_HEADS)]
    scores = [[_dot_nt(q[:, sl], mk[:, sl]) for sl in sls] for q in qs]
    probs = []
    for sc in scores:
        ps = []
        for s in sc:
            s = s * (XA_HEAD_DIM ** -0.5)
            p = jnp.exp(s - jnp.max(s, axis=-1, keepdims=True))
            ps.append((p / jnp.sum(p, axis=-1, keepdims=True)).astype(BF16))
        probs.append(ps)
    os_ = [jnp.concatenate([_dot(p, mv[:, sl]) for p, sl in zip(ps, sls)], axis=1) for ps in probs]
    outs = [x + _dot(o, wo_ref[...]) for x, o in zip(xs, os_)]
    for out, sub in zip(outs, subs):
        o_ref[sub, :] = out
    for out, sub in zip(outs, subs):
        lg_ref[:, sub] = _dot_nt(wr_ref[...], _rms(out, gm_ref[...])) + br_ref[...]


XA_SUB_ROWS = 256


def _mix_proj_xattn(x, acts, weights, g, wq, mk, mv, wo, g_moe, wt_router, bt_router, B, S, tq=512):
    D = x.shape[-1]
    M = mk.shape[0] // B
    XW = mk.shape[-1]
    n_in = len(acts)
    seq3 = lambda a: a.reshape(B, S, a.shape[-1])
    row_spec = lambda a: pl.BlockSpec((None, tq, a.shape[-1]), lambda b, n: (b, n, 0))
    const = lambda a: pl.BlockSpec(a.shape, lambda b, n: (0,) * a.ndim)
    out, logits = pl.pallas_call(
        functools.partial(_xattn_kernel, n_in=n_in),
        grid=(B, S // tq),
        in_specs=[row_spec(x)] + [row_spec(a) for a in acts] + [const(w) for w in weights]
                 + [pl.BlockSpec((1, D), lambda b, n: (0, 0)),
                    pl.BlockSpec((D, XW), lambda b, n: (0, 0)),
                    pl.BlockSpec((None, M, XW), lambda b, n: (b, 0, 0)),
                    pl.BlockSpec((None, M, XW), lambda b, n: (b, 0, 0)),
                    pl.BlockSpec((XW, D), lambda b, n: (0, 0)),
                    pl.BlockSpec((1, D), lambda b, n: (0, 0)),
                    pl.BlockSpec((LANES, D), lambda b, n: (0, 0)),
                    pl.BlockSpec((LANES, XA_SUB_ROWS), lambda b, n: (0, 0))],
        out_specs=[pl.BlockSpec((None, tq, D), lambda b, n: (b, n, 0)),
                   pl.BlockSpec((None, LANES, tq), lambda b, n: (b * (S // tq) + n, 0, 0))],
        out_shape=[jax.ShapeDtypeStruct((B, S, D), F32),
                   jax.ShapeDtypeStruct((B * S // tq, LANES, tq), F32)],
        compiler_params=_cparams(2),
        name="xattn",
    )(seq3(x), *[seq3(a) for a in acts], *weights, g.reshape(1, D), wq,
      mk.reshape(B, M, XW), mv.reshape(B, M, XW), wo, g_moe.reshape(1, D), wt_router, bt_router)
    return out.reshape(B * S, D), logits


ROUTER_ROWS = 40


def _router_kernel(lg_ref, info_ref, slot_ref, cnt_ref, carry_ref):
    i = pl.program_id(0)

    @pl.when(i == 0)
    def _():
        carry_ref[...] = jnp.zeros_like(carry_ref)

    logits = lg_ref[:ROUTER_ROWS, :]
    tm = logits.shape[1]
    row = lax.broadcasted_iota(I32, logits.shape, 0)
    big = jnp.int32(LANES)
    neg = -jnp.inf
    gl = jnp.where(row < MOE_GROUPS, logits, neg)
    gmax = jnp.max(gl, axis=0, keepdims=True)
    g_top = jnp.min(jnp.where(gl == gmax, row, big), axis=0, keepdims=True)
    p_group = 1.0 / jnp.sum(jnp.exp(gl - gmax), axis=0, keepdims=True)
    lo = MOE_GROUPS + MOE_EXPERTS_PER_GROUP * g_top
    in_group = jnp.where(row >= lo, jnp.where(row < lo + MOE_EXPERTS_PER_GROUP, 1, 0), 0) > 0
    el = jnp.where(in_group, logits, neg)
    emax = jnp.max(el, axis=0, keepdims=True)
    ee = jnp.exp(el - emax)
    prob = ee / jnp.sum(ee, axis=0, keepdims=True)
    prob = jnp.where(in_group, prob, -1.0)
    p1 = jnp.max(prob, axis=0, keepdims=True)
    i1 = jnp.min(jnp.where(prob == p1, row, big), axis=0, keepdims=True)
    rest = jnp.where(row == i1, -1.0, prob)
    p2 = jnp.max(rest, axis=0, keepdims=True)
    i2 = jnp.min(jnp.where(rest == p2, row, big), axis=0, keepdims=True)
    tot = p1 + p2
    g1 = p_group * p1 / tot
    g2 = p_group * p2 / tot
    oh = jnp.concatenate([jnp.where(row == i1, 1.0, 0.0), jnp.where(row == i2, 1.0, 0.0)], axis=0)
    tr = lax.broadcasted_iota(I32, (tm, tm), 0)
    tc = lax.broadcasted_iota(I32, (tm, tm), 1)
    pre = _dot(oh, jnp.where(tr < tc, 1.0, 0.0))
    tots = _dot(oh, jnp.ones((tm, LANES), F32))
    reps = tm // LANES
    carry = carry_ref[...]
    base1 = jnp.concatenate([carry] * reps, axis=1)
    base2 = jnp.concatenate([carry + tots[:ROUTER_ROWS]] * reps, axis=1)
    r1 = jnp.sum(oh[:ROUTER_ROWS] * (base1 + pre[:ROUTER_ROWS]), axis=0, keepdims=True)
    r2 = jnp.sum(oh[ROUTER_ROWS:] * (base2 + pre[ROUTER_ROWS:]), axis=0, keepdims=True)
    carry = carry + tots[:ROUTER_ROWS] + tots[ROUTER_ROWS:]
    carry_ref[...] = carry
    cnt_ref[...] = carry
    e1 = (i1 - MOE_GROUPS).astype(F32)
    e2 = (i2 - MOE_GROUPS).astype(F32)
    slot_rows = [e1, e2, r1, r2, g1, g2]
    rows8 = lax.broadcasted_iota(I32, (SUBLANES, tm), 0)
    slot = jnp.zeros((SUBLANES, tm), F32)
    for j, val in enumerate(slot_rows):
        slot = jnp.where(rows8 == j, val, slot)
    slot_ref[...] = slot
    wide = jnp.concatenate([slot, jnp.zeros((LANES - SUBLANES, tm), F32)], axis=0)
    info_ref[...] = jnp.transpose(wide)


def _router(logits, tm=256):
    n_row_tiles, _, tq = logits.shape
    per = tq // tm
    T = n_row_tiles * tq
    NT = T // tm
    return pl.pallas_call(
        _router_kernel,
        grid=(NT,),
        in_specs=[pl.BlockSpec((None, LANES, tm), lambda i: (i // per, 0, i % per))],
        out_specs=[pl.BlockSpec((tm, LANES), lambda i: (i, 0)),
                   pl.BlockSpec((None, SUBLANES, tm), lambda i: (i, 0, 0)),
                   pl.BlockSpec((ROUTER_ROWS, LANES), lambda i: (0, 0))],
        out_shape=[jax.ShapeDtypeStruct((T, LANES), F32),
                   jax.ShapeDtypeStruct((NT, SUBLANES, tm), F32),
                   jax.ShapeDtypeStruct((ROUTER_ROWS, LANES), F32)],
        scratch_shapes=[pltpu.VMEM((ROUTER_ROWS, LANES), F32)],
        compiler_params=_cparams(1),
        name="router",
    )(logits)


def _row_bytes_wait(hbm, buf, sem):
    pltpu.make_async_copy(buf, hbm.at[pl.ds(0, buf.shape[0]), :], sem).wait()


def _to_row_tiles(ref, val):
    n = val.shape[0]
    for c in range(SUBLANES):
        ref[pl.ds(c, n, stride=SUBLANES), :] = val[:, c * LANES:(c + 1) * LANES]


def _from_row_tiles(ref):
    n = ref.shape[0] // SUBLANES
    return jnp.concatenate([ref[pl.ds(c, n, stride=SUBLANES), :] for c in range(SUBLANES)], axis=1)


def _moe_dispatch_kernel(pends_ref, cnt_ref, dest_ref, x_ref, g_ref, hs_hbm, hbuf, zbuf, sems, zsem, *, td):
    i = pl.program_id(0)
    nt = pl.num_programs(0)
    slot = lax.rem(i, 2)

    @pl.when(i == 0)
    def _():
        zbuf[...] = jnp.zeros_like(zbuf)
        for e in range(MOE_EXPERTS):
            @pl.when(cnt_ref[e] > 0)
            def _():
                start = pl.multiple_of((pends_ref[e] - MOE_BLOCK) * SUBLANES, MOE_BLOCK)
                pltpu.make_async_copy(zbuf, hs_hbm.at[pl.ds(start, MOE_BLOCK * SUBLANES), :], zsem).start()
        for e in range(MOE_EXPERTS):
            @pl.when(cnt_ref[e] > 0)
            def _():
                pltpu.make_async_copy(zbuf, hs_hbm.at[pl.ds(0, MOE_BLOCK * SUBLANES), :], zsem).wait()

        first_unused = pends_ref[MOE_EXPERTS - 1] // MOE_BLOCK
        n_blocks = hs_hbm.shape[0] // (MOE_BLOCK * SUBLANES)

        def zero_start(blk, carry):
            start = pl.multiple_of(blk * (MOE_BLOCK * SUBLANES), MOE_BLOCK)
            pltpu.make_async_copy(zbuf, hs_hbm.at[pl.ds(start, MOE_BLOCK * SUBLANES), :], zsem).start()
            return carry

        def zero_wait(blk, carry):
            pltpu.make_async_copy(zbuf, hs_hbm.at[pl.ds(0, MOE_BLOCK * SUBLANES), :], zsem).wait()
            return carry

        lax.fori_loop(first_unused, n_blocks, zero_start, 0)
        lax.fori_loop(first_unused, n_blocks, zero_wait, 0)

    hb = hbuf.at[slot]
    _to_row_tiles(hb, _rms(x_ref[...], g_ref[...]))
    for j in range(td):
        for c in range(2):
            row = pl.multiple_of(dest_ref[0, c * td + j] * SUBLANES, SUBLANES)
            pltpu.make_async_copy(hb.at[pl.ds(j * SUBLANES, SUBLANES), :],
                                  hs_hbm.at[pl.ds(row, SUBLANES), :],
                                  sems.at[slot]).start(priority=c)

    @pl.when(i > 0)
    def _():
        other = hbuf.at[1 - slot]
        _row_bytes_wait(hs_hbm, other, sems.at[1 - slot])
        _row_bytes_wait(hs_hbm, other, sems.at[1 - slot])

    @pl.when(i == nt - 1)
    def _():
        _row_bytes_wait(hs_hbm, hb, sems.at[slot])
        _row_bytes_wait(hs_hbm, hb, sems.at[slot])


def _moe_dispatch(x, g, pends, counts, dest3, P, td):
    T, D = x.shape
    assert D == ROW_TILE
    grid_spec = pltpu.PrefetchScalarGridSpec(
        num_scalar_prefetch=2,
        grid=(T // td,),
        in_specs=[pl.BlockSpec((None, 1, 2 * td), lambda i, pe, cn: (i, 0, 0), memory_space=pltpu.SMEM),
                  pl.BlockSpec((td, D), lambda i, pe, cn: (i, 0)),
                  pl.BlockSpec((1, D), lambda i, pe, cn: (0, 0))],
        out_specs=pl.BlockSpec(memory_space=pl.ANY),
        scratch_shapes=[pltpu.VMEM((2, td * SUBLANES, LANES), F32),
                        pltpu.VMEM((MOE_BLOCK * SUBLANES, LANES), F32),
                        pltpu.SemaphoreType.DMA((2,)),
                        pltpu.SemaphoreType.DMA(())],
    )
    return pl.pallas_call(
        functools.partial(_moe_dispatch_kernel, td=td),
        grid_spec=grid_spec,
        out_shape=jax.ShapeDtypeStruct((P * SUBLANES, LANES), F32),
        compiler_params=_cparams(1),
        name="moe_dispatch",
    )(pends, counts, dest3, x, g.reshape(1, D))


def _moe_expert_kernel(be_ref, nu_ref, hs_ref, w1_ref, w3_ref, w2_ref, o_ref, w1b, w3b, w2b):
    i = pl.program_id(0)
    used = i < nu_ref[0]
    changed = jnp.logical_or(i == 0, be_ref[i] != be_ref[jnp.maximum(i - 1, 0)])

    @pl.when(jnp.logical_and(used, changed))
    def _():
        w1b[...] = w1_ref[...].astype(BF16)
        w3b[...] = w3_ref[...].astype(BF16)
        w2b[...] = w2_ref[...].astype(BF16)

    @pl.when(used)
    def _():
        xe = _from_row_tiles(hs_ref).astype(BF16)
        ff = w1b.shape[1]
        halves = [slice(0, ff // 2), slice(ff // 2, ff)]
        ups = [(jnp.dot(xe, w1b[:, sl], preferred_element_type=F32),
                jnp.dot(xe, w3b[:, sl], preferred_element_type=F32)) for sl in halves]
        act = [(a * _sigmoid(a) * b).astype(BF16) for a, b in ups]
        y = sum(jnp.dot(a, w2b[sl, :], preferred_element_type=F32) for a, sl in zip(act, halves))
        _to_row_tiles(o_ref, y)

    @pl.when(jnp.logical_not(used))
    def _():
        o_ref[...] = jnp.zeros_like(o_ref)


def _moe_experts(hs, block_e, n_used, w1, w3, w2, layer):
    P = hs.shape[0] // SUBLANES
    D = ROW_TILE
    FF = w1.shape[-1]
    NB = P // MOE_BLOCK
    last = lambda i, nu: jnp.minimum(i, nu[0] - 1)
    grid_spec = pltpu.PrefetchScalarGridSpec(
        num_scalar_prefetch=2,
        grid=(NB,),
        in_specs=[pl.BlockSpec((MOE_BLOCK * SUBLANES, LANES), lambda i, be, nu: (last(i, nu), 0)),
                  pl.BlockSpec((None, None, D, FF), lambda i, be, nu: (layer, be[last(i, nu)], 0, 0)),
                  pl.BlockSpec((None, None, D, FF), lambda i, be, nu: (layer, be[last(i, nu)], 0, 0)),
                  pl.BlockSpec((None, None, FF, D), lambda i, be, nu: (layer, be[last(i, nu)], 0, 0))],
        out_specs=pl.BlockSpec((MOE_BLOCK * SUBLANES, LANES), lambda i, be, nu: (i, 0)),
        scratch_shapes=[pltpu.VMEM((D, FF), BF16),
                        pltpu.VMEM((D, FF), BF16),
                        pltpu.VMEM((FF, D), BF16)],
    )
    return pl.pallas_call(
        _moe_expert_kernel,
        grid_spec=grid_spec,
        out_shape=jax.ShapeDtypeStruct((P * SUBLANES, LANES), F32),
        compiler_params=_cparams(1),
        name="moe_experts",
    )(block_e, n_used, hs, w1, w3, w2)


def _gather_rows(src_hbm, idx_ref, dst_ref, sem, n_rows):
    for r in range(n_rows):
        row = pl.multiple_of(idx_ref[0, r] * SUBLANES, SUBLANES)
        pltpu.make_async_copy(src_hbm.at[pl.ds(row, SUBLANES), :],
                              dst_ref.at[pl.ds(r * SUBLANES, SUBLANES), :], sem).start(priority=r % 2)


def _moe_combine_kernel(pos_ref, posn_ref, x_ref, info_ref, yb_hbm, g_ref, *rest, tc, final_norm, splits):
    if splits:
        w_ref, o_ref = rest[0], rest[1]
        p_refs = rest[2:2 + len(splits)]
        ybuf, sems = rest[2 + len(splits):]
    else:
        o_ref, ybuf, sems = rest
    i = pl.program_id(0)
    nb = pl.num_programs(0)
    slot = lax.rem(i, 2)

    @pl.when(i == 0)
    def _():
        def issue(r, carry):
            src = pl.multiple_of(pos_ref[0, r] * SUBLANES, SUBLANES)
            dst = pl.multiple_of(r * SUBLANES, SUBLANES)
            pltpu.make_async_copy(yb_hbm.at[pl.ds(src, SUBLANES), :],
                                  ybuf.at[0, pl.ds(dst, SUBLANES), :], sems.at[0]).start()
            return carry
        lax.fori_loop(0, 2 * tc, issue, 0)

    @pl.when(i + 1 < nb)
    def _():
        _gather_rows(yb_hbm, posn_ref, ybuf.at[1 - slot], sems.at[1 - slot], 2 * tc)

    pltpu.make_async_copy(yb_hbm.at[pl.ds(0, 2 * tc * SUBLANES), :], ybuf.at[slot], sems.at[slot]).wait()
    info = info_ref[...]
    yb = ybuf.at[slot]
    y0 = _from_row_tiles(yb.at[pl.ds(0, tc * SUBLANES), :])
    y1 = _from_row_tiles(yb.at[pl.ds(tc * SUBLANES, tc * SUBLANES), :])
    out = x_ref[...] + (y0 * info[:, 4:5] + y1 * info[:, 5:6])
    if final_norm:
        out = _rms(out, g_ref[...])
    o_ref[...] = out
    if splits:
        h = _rms(out, g_ref[...]).astype(BF16)
        off = 0
        for p_ref, n in zip(p_refs, splits):
            p_ref[...] = jnp.dot(h, w_ref[:, off:off + n], preferred_element_type=F32).astype(p_ref.dtype)
            off += n


def _moe_combine(x, info, dest3, yb, g, final_norm, tc, next_proj=None):
    T, D = x.shape
    NT = T // tc
    in_specs = [pl.BlockSpec((None, 1, 2 * tc), lambda i: (i, 0, 0), memory_space=pltpu.SMEM),
                pl.BlockSpec((None, 1, 2 * tc), lambda i: (jnp.minimum(i + 1, NT - 1), 0, 0),
                             memory_space=pltpu.SMEM),
                pl.BlockSpec((tc, D), lambda i: (i, 0)),
                pl.BlockSpec((tc, LANES), lambda i: (i, 0)),
                pl.BlockSpec(memory_space=pl.ANY),
                pl.BlockSpec((1, D), lambda i: (0, 0))]
    out_specs = [pl.BlockSpec((tc, D), lambda i: (i, 0))]
    out_shape = [jax.ShapeDtypeStruct((T, D), F32)]
    args = [dest3, dest3, x, info, yb, g.reshape(1, D)]
    splits = ()
    if next_proj is not None:
        w, splits, out_dtypes = next_proj
        assert not final_norm and sum(splits) == w.shape[1]
        in_specs.append(pl.BlockSpec(w.shape, lambda i: (0, 0)))
        args.append(w)
        out_specs += [pl.BlockSpec((tc, n), lambda i: (i, 0)) for n in splits]
        out_shape += [jax.ShapeDtypeStruct((T, n), dt) for n, dt in zip(splits, out_dtypes)]
    outs = pl.pallas_call(
        functools.partial(_moe_combine_kernel, tc=tc, final_norm=final_norm, splits=tuple(splits)),
        grid=(NT,),
        in_specs=in_specs,
        out_specs=out_specs,
        out_shape=out_shape,
        scratch_shapes=[pltpu.VMEM((2, 2 * tc * SUBLANES, LANES), F32), pltpu.SemaphoreType.DMA((2,))],
        compiler_params=_cparams(1),
        name="moe_combine",
    )(*args)
    return outs[0], tuple(outs[1:])


MOE_TILE = 256
MOE_COMBINE_TILE = 256
ROUTER_TILE = 256


def _tile_slots(dest, tile):
    n_tiles = dest.shape[0] * dest.shape[2] // tile
    return jnp.concatenate([dest[:, 0, :].reshape(n_tiles, 1, tile), dest[:, 1, :].reshape(n_tiles, 1, tile)], axis=2)


def _router_params(w_group, b_group, w_expert, b_expert, lanes_out):
    D = w_group.shape[0]
    n_log = MOE_GROUPS + MOE_EXPERTS
    wt = jnp.zeros((LANES, D), F32).at[:MOE_GROUPS].set(w_group.T).at[MOE_GROUPS:n_log].set(w_expert.T)
    bt = jnp.zeros((LANES,), F32).at[:MOE_GROUPS].set(b_group).at[MOE_GROUPS:n_log].set(b_expert)
    return wt.astype(BF16), jnp.broadcast_to(bt[:, None], (LANES, lanes_out))


def _moe_layer(x, g, logits, w1, w3, w2, layer, g_out, final_norm, next_proj):
    T, D = x.shape
    n_log = MOE_GROUPS + MOE_EXPERTS
    info, slot, cnt = _router(logits, ROUTER_TILE)
    P = 2 * T + MOE_EXPERTS * MOE_BLOCK
    NB = P // MOE_BLOCK
    counts = cnt[MOE_GROUPS:n_log, 0].astype(I32)
    padded = (counts + MOE_BLOCK - 1) // MOE_BLOCK * MOE_BLOCK
    pends = jnp.cumsum(padded).astype(I32)
    pstarts = pends - padded
    block_start = jnp.arange(NB, dtype=I32) * MOE_BLOCK
    block_e = jnp.minimum(jnp.sum((pends[None, :] <= block_start[:, None]).astype(I32), axis=1),
                          MOE_EXPERTS - 1).astype(I32)
    n_used = (pends[-1:] // MOE_BLOCK).astype(I32)
    eid = slot[:, 0:2, :].astype(I32)
    expert_ids = jnp.arange(MOE_EXPERTS, dtype=I32)
    seg_start = jnp.sum(jnp.where(eid[..., None] == expert_ids, pstarts, 0), axis=-1)
    dest = seg_start + slot[:, 2:4, :].astype(I32)
    hs = _moe_dispatch(x, g, pends, counts, _tile_slots(dest, MOE_TILE), P, MOE_TILE)
    yb = _moe_experts(hs, block_e, n_used, w1, w3, w2, layer)
    return _moe_combine(x, info, _tile_slots(dest, MOE_COMBINE_TILE), yb, g_out, final_norm, MOE_COMBINE_TILE,
                        next_proj)


def kernel(x, mem, norm_mix, norm_xattn, norm_moe, norm_final, ev_w_in, ev_sinks, ev_mu, ev_w0, ev_w2, ev_a0, ev_a2, ev_g2, ev_k_k, ev_k_a, ev_r_k, ev_lnx_w, ev_lnx_b, ev_w_out, od_w_in, od_gate_up, od_gate_b, od_onorm, od_w_out, mem_norm, mem_wk, mem_wv, xa_wq, xa_wo, moe_w_group, moe_b_group, moe_w_expert, moe_b_expert, moe_w1, moe_w3, moe_w2):
    B, S, D = x.shape
    M = mem.shape[1]
    T = B * S
    depth = norm_mix.shape[0]
    xf = x.reshape(T, D)

    XW = XA_HEADS * XA_HEAD_DIM
    w_kv = jnp.concatenate([mem_wk, mem_wv], axis=1).astype(BF16)
    mk, mv = _norm_matmul(mem.reshape(B * M, D), mem_norm, w_kv, (XW, XW), (BF16, BF16))

    KW = GLA_HEADS * GLA_DK
    VW = GLA_HEADS * GLA_DV
    swa_cols = SWA_Q_HEADS * HEAD_DIM + 2 * (SWA_Q_HEADS // SWA_GROUP) * HEAD_DIM

    def in_proj(layer):
        i = layer // 2
        if layer % 2 == 0:
            return ev_w_in[i].astype(BF16), (swa_cols, ev_w_in.shape[-1] - swa_cols), (F32, F32)
        R = od_gate_up.shape[1]
        w = od_w_in[i]
        w_re = jnp.concatenate([w[:, :2 * KW + VW], w[:, 2 * KW + VW + R:],
                                w[:, 2 * KW + VW:2 * KW + VW + R],
                                jnp.zeros((D, LANES - R), F32)], axis=1).astype(BF16)
        return w_re, (2 * KW + 2 * VW, LANES), (F32, F32)

    w_first, _, _ = in_proj(0)
    qkv0, rw0 = _in_proj_rwkv(xf, norm_mix[0], w_first, swa_cols, ev_mu[0], ev_w0[0], ev_w2[0], ev_a0[0], ev_a2[0],
                              ev_g2[0], ev_k_k[0], ev_k_a[0], S)
    proj = None
    for layer in range(depth):
        i = layer // 2
        if layer % 2 == 0:
            if layer == 0:
                qkv, rw = qkv0, [t.reshape(B, S, RWKV_WIDTH) for t in rw0]
            else:
                qkv, p_rw = proj
                rw = _rwkv_prep(p_rw, ev_mu[i], ev_w0[i], ev_w2[i], ev_a0[i], ev_a2[i],
                                ev_g2[i], ev_k_k[i], ev_k_a[i], B, S)
            o_a = _swa(qkv, ev_sinks[i], B, S)
            o_b = _rwkv_scan(*rw, ev_r_k[i].reshape(-1), ev_lnx_w[i], ev_lnx_b[i])
            w_out = ev_w_out[i].astype(BF16)
            qw = o_a.shape[-1]
            mix_acts, mix_ws = [o_a, o_b], [w_out[:qw], w_out[qw:]]
        else:
            qkvo, gd = proj
            R = od_gate_up.shape[1]
            gup = jnp.zeros((LANES, KW), F32).at[:R].set(od_gate_up[i]).astype(BF16)
            o = _gla(qkvo, gd, gup, od_gate_b[i], od_onorm[i], B, S)
            mix_acts, mix_ws = [o], [od_w_out[i].astype(BF16)]
        wt_router, bt_router = _router_params(moe_w_group[layer], moe_b_group[layer], moe_w_expert[layer],
                                              moe_b_expert[layer], XA_SUB_ROWS)
        xf, logits = _mix_proj_xattn(xf, mix_acts, mix_ws, norm_xattn[layer], xa_wq[layer].astype(BF16), mk, mv,
                                     xa_wo[layer].astype(BF16), norm_moe[layer], wt_router, bt_router, B, S)
        last = layer == depth - 1
        g_out = norm_final if last else norm_mix[layer + 1]
        xf, proj = _moe_layer(xf, norm_moe[layer], logits, moe_w1, moe_w3, moe_w2, layer,
                              g_out, last, None if last else in_proj(layer + 1))
    return xf.reshape(B, S, D)
```

```python
import functools

import jax
import jax.numpy as jnp
from jax import lax
from jax.experimental import pallas as pl
from jax.experimental.pallas import tpu as pltpu

F32 = jnp.float32
BF16 = jnp.bfloat16
I32 = jnp.int32

EPS = 1e-6
HEAD_DIM = 64
SWA_WINDOW = 128
SWA_Q_HEADS = 8
SWA_GROUP = 4
RWKV_HEADS = 8
RWKV_WIDTH = 512
RWKV_LN_EPS = 64e-5
RWKV_CHUNK = 64
GLA_HEADS = 4
GLA_DK = 128
GLA_DV = 256
GLA_CHUNK = 64
GLA_GATE_NORM = 16.0
XA_HEADS = 4
XA_HEAD_DIM = 128
MOE_GROUPS = 4
MOE_EXPERTS_PER_GROUP = 8
MOE_EXPERTS = 32
MOE_BLOCK = 512
LANES = 128
SUBLANES = 8
ROW_TILE = SUBLANES * LANES

VMEM_LIMIT_BYTES = 48 * 1024 * 1024


def _cparams(n_axes):
    return pltpu.CompilerParams(dimension_semantics=("arbitrary",) * n_axes,
                                vmem_limit_bytes=VMEM_LIMIT_BYTES)


def _dot(a, b):
    return jnp.dot(a.astype(BF16), b.astype(BF16), preferred_element_type=F32)


def _dot_nt(a, b):
    return lax.dot_general(a.astype(BF16), b.astype(BF16), (((1,), (1,)), ((), ())),
                           preferred_element_type=F32)


def _dot_tn(a, b):
    return lax.dot_general(a.astype(BF16), b.astype(BF16), (((0,), (0,)), ((), ())),
                           preferred_element_type=F32)


def _dot_f32(a, b):
    return jnp.dot(a, b, preferred_element_type=F32, precision=lax.Precision.HIGHEST)


def _rms(x, g):
    ms = jnp.mean(x * x, axis=-1, keepdims=True)
    return x * lax.rsqrt(ms + EPS) * g


def _sigmoid(x):
    return 1.0 / (1.0 + jnp.exp(-x))


def _softplus(x):
    return jnp.maximum(x, 0.0) + jnp.log(1.0 + jnp.exp(-jnp.abs(x)))


def _norm_matmul_kernel(x_ref, g_ref, w_ref, *o_refs, splits):
    h = _rms(x_ref[...], g_ref[...]).astype(BF16)
    off = 0
    for o_ref, n in zip(o_refs, splits):
        o_ref[...] = jnp.dot(h, w_ref[:, off:off + n], preferred_element_type=F32).astype(o_ref.dtype)
        off += n


def _norm_matmul(x, g, w, splits, out_dtypes, tm=256):
    T, D = x.shape
    N = w.shape[1]
    assert sum(splits) == N and T % tm == 0
    return pl.pallas_call(
        functools.partial(_norm_matmul_kernel, splits=tuple(splits)),
        grid=(T // tm,),
        in_specs=[pl.BlockSpec((tm, D), lambda i: (i, 0)),
                  pl.BlockSpec((1, D), lambda i: (0, 0)),
                  pl.BlockSpec((D, N), lambda i: (0, 0))],
        out_specs=[pl.BlockSpec((tm, n), lambda i: (i, 0)) for n in splits],
        out_shape=[jax.ShapeDtypeStruct((T, n), dt) for n, dt in zip(splits, out_dtypes)],
        compiler_params=_cparams(1),
        name="norm_matmul",
    )(x, g.reshape(1, D), w)


def _swa_kernel(sinks_ref, q_ref, kp_ref, kc_ref, vp_ref, vc_ref, o_ref):
    n = pl.program_id(1)
    W = SWA_WINDOW
    NB = q_ref.shape[0]
    qpos = lax.broadcasted_iota(I32, (W, 2 * W), 0) + W
    kpos = lax.broadcasted_iota(I32, (W, 2 * W), 1)
    rel = qpos - kpos
    in_window = jnp.where(rel >= 0, jnp.where(rel < W, 1, 0), 0)
    has_prev = jnp.where(n > 0, 1, 0)
    valid = (in_window * jnp.where(kpos >= W, 1, has_prev)) > 0
    n_groups = SWA_Q_HEADS // SWA_GROUP
    streams = [(bi, g) for bi in range(NB) for g in range(n_groups)]
    qb = [q_ref[bi].astype(BF16) for bi in range(NB)]
    kb = [jnp.concatenate([kp_ref[bi], kc_ref[bi]], axis=0).astype(BF16) for bi in range(NB)]
    vb = [jnp.concatenate([vp_ref[bi], vc_ref[bi]], axis=0).astype(BF16) for bi in range(NB)]
    gs = lambda g: slice(g * HEAD_DIM, (g + 1) * HEAD_DIM)
    scores = []
    for bi, g in streams:
        qg = jnp.concatenate([qb[bi][:, h * HEAD_DIM:(h + 1) * HEAD_DIM]
                              for h in range(g * SWA_GROUP, (g + 1) * SWA_GROUP)], axis=0)
        scores.append(_dot_nt(qg, kb[bi][:, gs(g)]))
    probs = []
    for i, (bi, g) in enumerate(streams):
        pieces = []
        for j in range(SWA_GROUP):
            s = jnp.where(valid, scores[i][j * W:(j + 1) * W] * (HEAD_DIM ** -0.5), -jnp.inf)
            sink = sinks_ref[g * SWA_GROUP + j]
            m = jnp.maximum(jnp.max(s, axis=-1, keepdims=True), sink)
            p = jnp.exp(s - m)
            den = jnp.sum(p, axis=-1, keepdims=True) + jnp.exp(sink - m)
            pieces.append((p / den).astype(BF16))
        probs.append(jnp.concatenate(pieces, axis=0))
    ogs = [_dot(probs[i], vb[bi][:, gs(g)]) for i, (bi, g) in enumerate(streams)]
    for bi in range(NB):
        outs = []
        for g in range(n_groups):
            og = ogs[bi * n_groups + g]
            outs += [og[j * W:(j + 1) * W] for j in range(SWA_GROUP)]
        o_ref[bi] = jnp.concatenate(outs, axis=1).astype(o_ref.dtype)


SWA_BATCH_ROWS = 2


def _swa(qkv, sinks, B, S):
    W = SWA_WINDOW
    qkv3 = qkv.reshape(B, S, qkv.shape[-1])
    qw = SWA_Q_HEADS * HEAD_DIM
    kw = qw // SWA_GROUP
    kcol = qw // kw
    nb = SWA_BATCH_ROWS if B % SWA_BATCH_ROWS == 0 else 1
    out = pl.pallas_call(
        _swa_kernel,
        grid=(B // nb, S // W),
        in_specs=[pl.BlockSpec(memory_space=pltpu.SMEM),
                  pl.BlockSpec((nb, W, qw), lambda b, n: (b, n, 0)),
                  pl.BlockSpec((nb, W, kw), lambda b, n: (b, jnp.maximum(n - 1, 0), kcol)),
                  pl.BlockSpec((nb, W, kw), lambda b, n: (b, n, kcol)),
                  pl.BlockSpec((nb, W, kw), lambda b, n: (b, jnp.maximum(n - 1, 0), kcol + 1)),
                  pl.BlockSpec((nb, W, kw), lambda b, n: (b, n, kcol + 1))],
        out_specs=pl.BlockSpec((nb, W, qw), lambda b, n: (b, n, 0)),
        out_shape=jax.ShapeDtypeStruct((B, S, qw), BF16),
        compiler_params=_cparams(2),
        name="swa",
    )(sinks, qkv3, qkv3, qkv3, qkv3, qkv3)
    return out.reshape(B * S, qw)


def _rwkv_prep_math(p, last, mu_ref, w0_ref, w2_ref, a0_ref, a2_ref, g2_ref, kk_ref, ka_ref, outs):
    r_out, lw_out, k_out, v_out, a_out, b_out, g_out = outs
    C = RWKV_WIDTH
    row = lax.broadcasted_iota(I32, p.shape, 0)
    p_prev = jnp.where(row == 0, last, pltpu.roll(p, 1, axis=0))
    p = p + (p_prev - p) * mu_ref[...]
    r = p[:, :C]
    k = p[:, C:2 * C]
    v = p[:, 2 * C:3 * C]
    xw = p[:, 3 * C:3 * C + 64]
    xa = p[:, 3 * C + 64:3 * C + 128]
    xg = p[:, 3 * C + 128:]
    w = -_softplus(-(w0_ref[...] + _dot(jnp.tanh(xw), w2_ref[...]))) - 0.5
    lw = -jnp.exp(w)
    a = _sigmoid(a0_ref[...] + _dot(xa, a2_ref[...]))
    g = _dot(_sigmoid(xg), g2_ref[...])
    kk = k * kk_ref[...]
    pieces = []
    for h in range(RWKV_HEADS):
        kh = kk[:, h * HEAD_DIM:(h + 1) * HEAD_DIM]
        nrm = jnp.sqrt(jnp.sum(kh * kh, axis=-1, keepdims=True))
        pieces.append(kh / jnp.maximum(nrm, 1e-12))
    kk = jnp.concatenate(pieces, axis=1)
    r_out[...] = r
    lw_out[...] = lw
    k_out[...] = k * (1.0 + (a - 1.0) * ka_ref[...])
    v_out[...] = v
    a_out[...] = -kk
    b_out[...] = kk * a
    g_out[...] = g


def _rwkv_prep_kernel(p_ref, pprev_ref, *refs):
    n = pl.program_id(1)
    last = jnp.where(n > 0, pprev_ref[7:8, :], 0.0)
    _rwkv_prep_math(p_ref[...], last, *refs[:8], refs[8:])


def _rwkv_params(mu, w0, w2, a0, a2, g2, k_k, k_a):
    row = lambda t: t.reshape(1, -1)
    return [row(mu), row(w0), w2.astype(BF16), row(a0), a2.astype(BF16), g2.astype(BF16), row(k_k), row(k_a)]


def _rwkv_prep(p, mu, w0, w2, a0, a2, g2, k_k, k_a, B, S, tt=256):
    C = RWKV_WIDTH
    PW = p.shape[-1]
    p3 = p.reshape(B, S, PW)
    full = lambda arr: pl.BlockSpec(arr.shape, lambda b, n: (0,) * arr.ndim)
    params = _rwkv_params(mu, w0, w2, a0, a2, g2, k_k, k_a)
    outs = pl.pallas_call(
        _rwkv_prep_kernel,
        grid=(B, S // tt),
        in_specs=[pl.BlockSpec((None, tt, PW), lambda b, n: (b, n, 0)),
                  pl.BlockSpec((None, 8, PW), lambda b, n: (b, jnp.maximum(n * (tt // 8) - 1, 0), 0))]
                 + [full(t) for t in params],
        out_specs=[pl.BlockSpec((None, tt, C), lambda b, n: (b, n, 0))] * 7,
        out_shape=[jax.ShapeDtypeStruct((B, S, C), F32)] * 7,
        compiler_params=_cparams(2),
        name="rwkv_prep",
    )(p3, p3, *params)
    return outs


def _in_proj_rwkv_kernel(x_ref, g_ref, w_ref, *refs, swa_cols, tiles_per_seq):
    params = refs[:8]
    qkv_out = refs[8]
    outs = refs[9:16]
    p_buf, last_buf = refs[16:]
    i = pl.program_id(0)

    @pl.when(i == 0)
    def _():
        p_buf[...] = jnp.zeros_like(p_buf)
        last_buf[...] = jnp.zeros_like(last_buf)

    j = i - 1
    p_prev_tile = p_buf[lax.rem(i + 1, 2)]
    tm = p_prev_tile.shape[0]
    last = jnp.where(lax.rem(j, tiles_per_seq) == 0, 0.0, last_buf[...])
    _rwkv_prep_math(p_prev_tile, last, *params, outs)
    last_buf[...] = p_prev_tile[tm - 1:tm, :]
    h = _rms(x_ref[...], g_ref[...]).astype(BF16)
    qkv_out[...] = jnp.dot(h, w_ref[:, :swa_cols], preferred_element_type=F32)
    p_buf[lax.rem(i, 2)] = jnp.dot(h, w_ref[:, swa_cols:], preferred_element_type=F32)


def _in_proj_rwkv(x, g, w, swa_cols, mu, w0, w2, a0, a2, g2, k_k, k_a, S, tm=256):
    T, D = x.shape
    N = w.shape[1]
    C = RWKV_WIDTH
    NT = T // tm
    params = _rwkv_params(mu, w0, w2, a0, a2, g2, k_k, k_a)
    const = lambda arr: pl.BlockSpec(arr.shape, lambda i: (0,) * arr.ndim)
    cur = lambda i: (jnp.minimum(i, NT - 1), 0)
    prev = lambda i: (jnp.maximum(i - 1, 0), 0)
    outs = pl.pallas_call(
        functools.partial(_in_proj_rwkv_kernel, swa_cols=swa_cols, tiles_per_seq=S // tm),
        grid=(NT + 1,),
        in_specs=[pl.BlockSpec((tm, D), cur), pl.BlockSpec((1, D), lambda i: (0, 0)), const(w)]
                 + [const(t) for t in params],
        out_specs=[pl.BlockSpec((tm, swa_cols), cur)] + [pl.BlockSpec((tm, C), prev)] * 7,
        out_shape=[jax.ShapeDtypeStruct((T, swa_cols), F32)] + [jax.ShapeDtypeStruct((T, C), F32)] * 7,
        scratch_shapes=[pltpu.VMEM((2, tm, N - swa_cols), F32), pltpu.VMEM((1, N - swa_cols), F32)],
        compiler_params=_cparams(1),
        name="in_proj_rwkv",
    )(x, g.reshape(1, D), w, *params)
    return outs[0], outs[1:]


def _pair_blockdiag(x):
    lane = lax.broadcasted_iota(I32, x.shape, 1)
    zero = jnp.zeros_like(x)
    return jnp.concatenate([jnp.where(lane < HEAD_DIM, x, zero), jnp.where(lane >= HEAD_DIM, x, zero)], axis=0)


def _rwkv_scan_kernel(r_ref, lw_ref, k_ref, v_ref, a_ref, b_ref, g_ref, rk_ref, lnw_ref, lnb_ref,
                      o_ref, s_ref):
    c = pl.program_id(1)

    @pl.when(c == 0)
    def _():
        s_ref[...] = jnp.zeros_like(s_ref)

    C = RWKV_CHUNK
    NB = r_ref.shape[0]
    NP = RWKV_HEADS // 2
    PW = 2 * HEAD_DIM
    row = lax.broadcasted_iota(I32, (C, C), 0)
    col = lax.broadcasted_iota(I32, (C, C), 1)
    tri = jnp.where(row >= col, 1.0, 0.0).astype(F32)
    rowp = lax.broadcasted_iota(I32, (C, PW), 0)
    colp = lax.broadcasted_iota(I32, (C, PW), 1)
    colp = jnp.where(colp >= HEAD_DIM, colp - HEAD_DIM, colp)
    lower_p = rowp >= colp
    strict_p = rowp > colp
    rows = lax.broadcasted_iota(I32, (PW, PW), 0)
    cols = lax.broadcasted_iota(I32, (PW, PW), 1)
    same_head = jnp.where(rows >= HEAD_DIM, 1, 0) == jnp.where(cols >= HEAD_DIM, 1, 0)
    first = lax.broadcasted_iota(I32, (C, PW), 1) < HEAD_DIM

    streams = [(bi, p) for bi in range(NB) for p in range(NP)]
    pre = []
    for bi in range(NB):
        lw = lw_ref[bi]
        cum = _dot_f32(tri, lw)
        cum_last = cum[C - 1:C, :]
        r = r_ref[bi]
        k = k_ref[bi]
        v = v_ref[bi]
        a = a_ref[bi]
        b = b_ref[bi]
        e_neg = jnp.exp(-cum)
        e_rem = jnp.exp(cum_last - cum)
        pre.append(dict(
            r_t=(r * jnp.exp(cum)).astype(BF16), a_t=(a * jnp.exp(cum - lw)).astype(BF16),
            b_t=(b * e_neg).astype(BF16), k_t=(k * e_neg).astype(BF16),
            b_d=(b * e_rem).astype(BF16), k_d=(k * e_rem).astype(BF16),
            v_b=v.astype(BF16), v=v, e_last=jnp.exp(cum_last), rkk=r * k * rk_ref[...], g=g_ref[bi]))

    def lanes(p):
        return slice(p * PW, (p + 1) * PW)

    ar = [jnp.concatenate([pre[bi]['a_t'][:, lanes(p)], pre[bi]['r_t'][:, lanes(p)]], axis=0) for bi, p in streams]
    s0 = [s_ref[bi, p] for bi, p in streams]
    big = [_dot_nt(ar[i], jnp.concatenate([_pair_blockdiag(pre[bi]['b_t'][:, lanes(p)]),
                                           _pair_blockdiag(pre[bi]['k_t'][:, lanes(p)]),
                                           s0[i].astype(BF16)], axis=0))
           for i, (bi, p) in enumerate(streams)]
    m_b = [t[:, :PW] for t in big]
    m_k = [t[:, PW:2 * PW] for t in big]
    ars = [t[:, 2 * PW:] for t in big]
    v_p = [pre[bi]['v_b'][:, lanes(p)] for bi, p in streams]
    v_bd = [_pair_blockdiag(vp) for vp in v_p]
    x = [ars[i][:C] + _dot(jnp.where(strict_p, m_k[i][:C], 0.0), v_bd[i]) for i in range(len(streams))]
    pw = [jnp.where(strict_p, m_b[i][:C], 0.0).astype(BF16) for i in range(len(streams))]
    n_stages = 6
    for stage in range(n_stages):
        if stage < n_stages - 1:
            prod = [_dot(pw[i], jnp.concatenate([_pair_blockdiag(x[i].astype(BF16)), _pair_blockdiag(pw[i])], axis=1))
                    for i in range(len(streams))]
            x = [x[i] + prod[i][:, :PW] for i in range(len(streams))]
            pw = [prod[i][:, PW:].astype(BF16) for i in range(len(streams))]
        else:
            x = [x[i] + _dot(pw[i], _pair_blockdiag(x[i].astype(BF16))) for i in range(len(streams))]
    u_b = [xi.astype(BF16) for xi in x]
    y = [ars[i][C:]
         + _dot(jnp.concatenate([jnp.where(lower_p, m_b[i][C:], 0.0), jnp.where(lower_p, m_k[i][C:], 0.0)], axis=1),
                jnp.concatenate([_pair_blockdiag(u_b[i]), v_bd[i]], axis=0))
         for i in range(len(streams))]
    for i, (bi, p) in enumerate(streams):
        upd = _dot_tn(jnp.concatenate([u_b[i], v_p[i]], axis=0),
                      jnp.concatenate([pre[bi]['b_d'][:, lanes(p)], pre[bi]['k_d'][:, lanes(p)]], axis=0))
        s_ref[bi, p] = s0[i] * pre[bi]['e_last'][:, lanes(p)] + jnp.where(same_head, upd, 0.0)

    lnw = lnw_ref[...]
    lnb = lnb_ref[...]

    def head_sum(t):
        s1 = jnp.sum(jnp.where(first, t, 0.0), axis=-1, keepdims=True)
        s2 = jnp.sum(jnp.where(first, 0.0, t), axis=-1, keepdims=True)
        return jnp.where(first, s1, s2)

    for bi in range(NB):
        outs = []
        for p in range(NP):
            yi = y[bi * NP + p]
            mean = head_sum(yi) * (1.0 / HEAD_DIM)
            yc = yi - mean
            var = head_sum(yc * yc) * (1.0 / HEAD_DIM)
            yn = yc * lax.rsqrt(var + RWKV_LN_EPS) * lnw[:, lanes(p)] + lnb[:, lanes(p)]
            bonus = head_sum(pre[bi]['rkk'][:, lanes(p)]) * pre[bi]['v'][:, lanes(p)]
            outs.append((yn + bonus) * pre[bi]['g'][:, lanes(p)])
        o_ref[bi] = jnp.concatenate(outs, axis=1).astype(o_ref.dtype)


RWKV_BATCH_ROWS = 4


def _rwkv_scan(r, lw, k, v, a, b, g, r_k, lnx_w, lnx_b):
    B, S, W = r.shape
    C = RWKV_CHUNK
    nb = RWKV_BATCH_ROWS if B % RWKV_BATCH_ROWS == 0 else 1
    seq = pl.BlockSpec((nb, C, W), lambda bb, c: (bb, c, 0))
    par = pl.BlockSpec((1, W), lambda bb, c: (0, 0))
    out = pl.pallas_call(
        _rwkv_scan_kernel,
        grid=(B // nb, S // C),
        in_specs=[seq] * 7 + [par] * 3,
        out_specs=seq,
        out_shape=jax.ShapeDtypeStruct((B, S, W), BF16),
        scratch_shapes=[pltpu.VMEM((nb, RWKV_HEADS // 2, 2 * HEAD_DIM, 2 * HEAD_DIM), F32)],
        compiler_params=_cparams(2),
        name="rwkv_scan",
    )(r, lw, k, v, a, b, g, r_k.reshape(1, W), lnx_w.reshape(1, W), lnx_b.reshape(1, W))
    return out.reshape(B * S, W)


def _gla_kernel(q_ref, k_ref, v_ref, og_ref, gd_ref, gup_ref, gb_ref, on_ref, o_ref, s_ref):
    c = pl.program_id(1)

    @pl.when(c == 0)
    def _():
        s_ref[...] = jnp.zeros_like(s_ref)

    C = GLA_CHUNK
    NB = q_ref.shape[0]
    row = lax.broadcasted_iota(I32, (C, C), 0)
    col = lax.broadcasted_iota(I32, (C, C), 1)
    lower = row >= col
    tri = jnp.where(lower, 1.0, 0.0).astype(F32)
    onorm = on_ref[...]
    zs = [_dot(gd_ref[bi], gup_ref[...]) + gb_ref[...] for bi in range(NB)]
    cums = [_dot_f32(tri, -_softplus(-z) / GLA_GATE_NORM) for z in zs]
    qe, ke, kd, e_last, v = [], [], [], [], []
    for bi in range(NB):
        cum = cums[bi]
        cum_last = cum[C - 1:C, :]
        k = k_ref[bi]
        qe.append((q_ref[bi] * (GLA_DK ** -0.5) * jnp.exp(cum)).astype(BF16))
        ke.append((k * jnp.exp(-cum)).astype(BF16))
        kd.append((k * jnp.exp(cum_last - cum)).astype(BF16))
        e_last.append(jnp.exp(cum_last))
        v.append(v_ref[bi].astype(BF16))
    streams = [(bi, h) for bi in range(NB) for h in range(GLA_HEADS)]
    ks = lambda h: slice(h * GLA_DK, (h + 1) * GLA_DK)
    vs = lambda h: slice(h * GLA_DV, (h + 1) * GLA_DV)
    sts = [s_ref[bi, h] for bi, h in streams]
    atts = [jnp.where(lower, _dot_nt(qe[bi][:, ks(h)], ke[bi][:, ks(h)]), 0.0) for bi, h in streams]
    inters = [_dot_nt(qe[bi][:, ks(h)], sts[i]) for i, (bi, h) in enumerate(streams)]
    os_ = [inters[i] + _dot(atts[i], v[bi][:, vs(h)]) for i, (bi, h) in enumerate(streams)]
    for i, (bi, h) in enumerate(streams):
        s_ref[bi, h] = sts[i] * e_last[bi][:, ks(h)] + _dot_tn(v[bi][:, vs(h)], kd[bi][:, ks(h)])
    for bi in range(NB):
        og = og_ref[bi]
        outs = []
        for h in range(GLA_HEADS):
            gate = og[:, vs(h)]
            outs.append(_rms(os_[bi * GLA_HEADS + h], onorm) * (gate * _sigmoid(gate)))
        o_ref[bi] = jnp.concatenate(outs, axis=1).astype(o_ref.dtype)


GLA_BATCH_ROWS = 4


def _gla(qkvo, gd, gate_up_pad, gate_b, onorm, B, S):
    C = GLA_CHUNK
    KW = GLA_HEADS * GLA_DK
    VW = GLA_HEADS * GLA_DV
    x3 = qkvo.reshape(B, S, qkvo.shape[-1])
    gd3 = gd.reshape(B, S, LANES)
    nb = GLA_BATCH_ROWS if B % GLA_BATCH_ROWS == 0 else 1
    out = pl.pallas_call(
        _gla_kernel,
        grid=(B // nb, S // C),
        in_specs=[pl.BlockSpec((nb, C, KW), lambda b, c: (b, c, 0)),
                  pl.BlockSpec((nb, C, KW), lambda b, c: (b, c, 1)),
                  pl.BlockSpec((nb, C, VW), lambda b, c: (b, c, 1)),
                  pl.BlockSpec((nb, C, VW), lambda b, c: (b, c, 2)),
                  pl.BlockSpec((nb, C, LANES), lambda b, c: (b, c, 0)),
                  pl.BlockSpec((LANES, KW), lambda b, c: (0, 0)),
                  pl.BlockSpec((1, KW), lambda b, c: (0, 0)),
                  pl.BlockSpec((1, GLA_DV), lambda b, c: (0, 0))],
        out_specs=pl.BlockSpec((nb, C, VW), lambda b, c: (b, c, 0)),
        out_shape=jax.ShapeDtypeStruct((B, S, VW), BF16),
        scratch_shapes=[pltpu.VMEM((nb, GLA_HEADS, GLA_DV, GLA_DK), F32)],
        compiler_params=_cparams(2),
        name="gla",
    )(x3, x3, x3, x3, gd3, gate_up_pad, gate_b.reshape(1, KW), onorm.reshape(1, GLA_DV))
    return out.reshape(B * S, VW)


def _xattn_kernel(*refs, n_in):
    x_ref = refs[0]
    a_refs = refs[1:1 + n_in]
    w_refs = refs[1 + n_in:1 + 2 * n_in]
    g_ref, wq_ref, mk_ref, mv_ref, wo_ref, gm_ref, wr_ref, br_ref, o_ref, lg_ref = refs[1 + 2 * n_in:]
    tq = x_ref.shape[0]
    subs = [slice(r, r + XA_SUB_ROWS) for r in range(0, tq, XA_SUB_ROWS)]
    xs = [x_ref[sub, :] for sub in subs]
    for a_ref, w_ref in zip(a_refs, w_refs):
        xs = [x + jnp.dot(a_ref[sub, :], w_ref[...], preferred_element_type=F32) for x, sub in zip(xs, subs)]
    qs = [_dot(_rms(x, g_ref[...]), wq_ref[...]).astype(BF16) for x in xs]
    mk = mk_ref[...]
    mv = mv_ref[...]
    sls = [slice(hd * XA_HEAD_DIM, (hd + 1) * XA_HEAD_DIM) for hd in range(XA_HEADS)]
    scores = [[_dot_nt(q[:, sl], mk[:, sl]) for sl in sls] for q in qs]
    probs = []
    for sc in scores:
        ps = []
        for s in sc:
            s = s * (XA_HEAD_DIM ** -0.5)
            p = jnp.exp(s - jnp.max(s, axis=-1, keepdims=True))
            ps.append((p / jnp.sum(p, axis=-1, keepdims=True)).astype(BF16))
        probs.append(ps)
    os_ = [jnp.concatenate([_dot(p, mv[:, sl]) for p, sl in zip(ps, sls)], axis=1) for ps in probs]
    outs = [x + _dot(o, wo_ref[...]) for x, o in zip(xs, os_)]
    for out, sub in zip(outs, subs):
        o_ref[sub, :] = out
    for out, sub in zip(outs, subs):
        lg_ref[:, sub] = _dot_nt(wr_ref[...], _rms(out, gm_ref[...])) + br_ref[...]


XA_SUB_ROWS = 256


def _mix_proj_xattn(x, acts, weights, g, wq, mk, mv, wo, g_moe, wt_router, bt_router, B, S, tq=512):
    D = x.shape[-1]
    M = mk.shape[0] // B
    XW = mk.shape[-1]
    n_in = len(acts)
    seq3 = lambda a: a.reshape(B, S, a.shape[-1])
    row_spec = lambda a: pl.BlockSpec((None, tq, a.shape[-1]), lambda b, n: (b, n, 0))
    const = lambda a: pl.BlockSpec(a.shape, lambda b, n: (0,) * a.ndim)
    out, logits = pl.pallas_call(
        functools.partial(_xattn_kernel, n_in=n_in),
        grid=(B, S // tq),
        in_specs=[row_spec(x)] + [row_spec(a) for a in acts] + [const(w) for w in weights]
                 + [pl.BlockSpec((1, D), lambda b, n: (0, 0)),
                    pl.BlockSpec((D, XW), lambda b, n: (0, 0)),
                    pl.BlockSpec((None, M, XW), lambda b, n: (b, 0, 0)),
                    pl.BlockSpec((None, M, XW), lambda b, n: (b, 0, 0)),
                    pl.BlockSpec((XW, D), lambda b, n: (0, 0)),
                    pl.BlockSpec((1, D), lambda b, n: (0, 0)),
                    pl.BlockSpec((LANES, D), lambda b, n: (0, 0)),
                    pl.BlockSpec((LANES, XA_SUB_ROWS), lambda b, n: (0, 0))],
        out_specs=[pl.BlockSpec((None, tq, D), lambda b, n: (b, n, 0)),
                   pl.BlockSpec((None, LANES, tq), lambda b, n: (b * (S // tq) + n, 0, 0))],
        out_shape=[jax.ShapeDtypeStruct((B, S, D), F32),
                   jax.ShapeDtypeStruct((B * S // tq, LANES, tq), F32)],
        compiler_params=_cparams(2),
        name="xattn",
    )(seq3(x), *[seq3(a) for a in acts], *weights, g.reshape(1, D), wq,
      mk.reshape(B, M, XW), mv.reshape(B, M, XW), wo, g_moe.reshape(1, D), wt_router, bt_router)
    return out.reshape(B * S, D), logits


ROUTER_ROWS = 40


def _router_kernel(lg_ref, info_ref, slot_ref, cnt_ref, carry_ref):
    i = pl.program_id(0)

    @pl.when(i == 0)
    def _():
        carry_ref[...] = jnp.zeros_like(carry_ref)

    logits = lg_ref[:ROUTER_ROWS, :]
    tm = logits.shape[1]
    row = lax.broadcasted_iota(I32, logits.shape, 0)
    big = jnp.int32(LANES)
    neg = -jnp.inf
    gl = jnp.where(row < MOE_GROUPS, logits, neg)
    gmax = jnp.max(gl, axis=0, keepdims=True)
    g_top = jnp.min(jnp.where(gl == gmax, row, big), axis=0, keepdims=True)
    p_group = 1.0 / jnp.sum(jnp.exp(gl - gmax), axis=0, keepdims=True)
    lo = MOE_GROUPS + MOE_EXPERTS_PER_GROUP * g_top
    in_group = jnp.where(row >= lo, jnp.where(row < lo + MOE_EXPERTS_PER_GROUP, 1, 0), 0) > 0
    el = jnp.where(in_group, logits, neg)
    emax = jnp.max(el, axis=0, keepdims=True)
    ee = jnp.exp(el - emax)
    prob = ee / jnp.sum(ee, axis=0, keepdims=True)
    prob = jnp.where(in_group, prob, -1.0)
    p1 = jnp.max(prob, axis=0, keepdims=True)
    i1 = jnp.min(jnp.where(prob == p1, row, big), axis=0, keepdims=True)
    rest = jnp.where(row == i1, -1.0, prob)
    p2 = jnp.max(rest, axis=0, keepdims=True)
    i2 = jnp.min(jnp.where(rest == p2, row, big), axis=0, keepdims=True)
    tot = p1 + p2
    g1 = p_group * p1 / tot
    g2 = p_group * p2 / tot
    oh = jnp.concatenate([jnp.where(row == i1, 1.0, 0.0), jnp.where(row == i2, 1.0, 0.0)], axis=0)
    tr = lax.broadcasted_iota(I32, (tm, tm), 0)
    tc = lax.broadcasted_iota(I32, (tm, tm), 1)
    pre = _dot(oh, jnp.where(tr < tc, 1.0, 0.0))
    tots = _dot(oh, jnp.ones((tm, LANES), F32))
    reps = tm // LANES
    carry = carry_ref[...]
    base1 = jnp.concatenate([carry] * reps, axis=1)
    base2 = jnp.concatenate([carry + tots[:ROUTER_ROWS]] * reps, axis=1)
    r1 = jnp.sum(oh[:ROUTER_ROWS] * (base1 + pre[:ROUTER_ROWS]), axis=0, keepdims=True)
    r2 = jnp.sum(oh[ROUTER_ROWS:] * (base2 + pre[ROUTER_ROWS:]), axis=0, keepdims=True)
    carry = carry + tots[:ROUTER_ROWS] + tots[ROUTER_ROWS:]
    carry_ref[...] = carry
    cnt_ref[...] = carry
    e1 = (i1 - MOE_GROUPS).astype(F32)
    e2 = (i2 - MOE_GROUPS).astype(F32)
    slot_rows = [e1, e2, r1, r2, g1, g2]
    rows8 = lax.broadcasted_iota(I32, (SUBLANES, tm), 0)
    slot = jnp.zeros((SUBLANES, tm), F32)
    for j, val in enumerate(slot_rows):
        slot = jnp.where(rows8 == j, val, slot)
    slot_ref[...] = slot
    wide = jnp.concatenate([slot, jnp.zeros((LANES - SUBLANES, tm), F32)], axis=0)
    info_ref[...] = jnp.transpose(wide)


def _router(logits, tm=256):
    n_row_tiles, _, tq = logits.shape
    per = tq // tm
    T = n_row_tiles * tq
    NT = T // tm
    return pl.pallas_call(
        _router_kernel,
        grid=(NT,),
        in_specs=[pl.BlockSpec((None, LANES, tm), lambda i: (i // per, 0, i % per))],
        out_specs=[pl.BlockSpec((tm, LANES), lambda i: (i, 0)),
                   pl.BlockSpec((None, SUBLANES, tm), lambda i: (i, 0, 0)),
                   pl.BlockSpec((ROUTER_ROWS, LANES), lambda i: (0, 0))],
        out_shape=[jax.ShapeDtypeStruct((T, LANES), F32),
                   jax.ShapeDtypeStruct((NT, SUBLANES, tm), F32),
                   jax.ShapeDtypeStruct((ROUTER_ROWS, LANES), F32)],
        scratch_shapes=[pltpu.VMEM((ROUTER_ROWS, LANES), F32)],
        compiler_params=_cparams(1),
        name="router",
    )(logits)


def _row_bytes_wait(hbm, buf, sem):
    pltpu.make_async_copy(buf, hbm.at[pl.ds(0, buf.shape[0]), :], sem).wait()


def _to_row_tiles(ref, val):
    n = val.shape[0]
    for c in range(SUBLANES):
        ref[pl.ds(c, n, stride=SUBLANES), :] = val[:, c * LANES:(c + 1) * LANES]


def _from_row_tiles(ref):
    n = ref.shape[0] // SUBLANES
    return jnp.concatenate([ref[pl.ds(c, n, stride=SUBLANES), :] for c in range(SUBLANES)], axis=1)


def _moe_dispatch_kernel(pends_ref, cnt_ref, dest_ref, x_ref, g_ref, hs_hbm, hbuf, zbuf, sems, zsem, *, td):
    i = pl.program_id(0)
    nt = pl.num_programs(0)
    slot = lax.rem(i, 2)

    @pl.when(i == 0)
    def _():
        zbuf[...] = jnp.zeros_like(zbuf)
        for e in range(MOE_EXPERTS):
            @pl.when(cnt_ref[e] > 0)
            def _():
                start = pl.multiple_of((pends_ref[e] - MOE_BLOCK) * SUBLANES, MOE_BLOCK)
                pltpu.make_async_copy(zbuf, hs_hbm.at[pl.ds(start, MOE_BLOCK * SUBLANES), :], zsem).start()
        for e in range(MOE_EXPERTS):
            @pl.when(cnt_ref[e] > 0)
            def _():
                pltpu.make_async_copy(zbuf, hs_hbm.at[pl.ds(0, MOE_BLOCK * SUBLANES), :], zsem).wait()

        first_unused = pends_ref[MOE_EXPERTS - 1] // MOE_BLOCK
        n_blocks = hs_hbm.shape[0] // (MOE_BLOCK * SUBLANES)

        def zero_start(blk, carry):
            start = pl.multiple_of(blk * (MOE_BLOCK * SUBLANES), MOE_BLOCK)
            pltpu.make_async_copy(zbuf, hs_hbm.at[pl.ds(start, MOE_BLOCK * SUBLANES), :], zsem).start()
            return carry

        def zero_wait(blk, carry):
            pltpu.make_async_copy(zbuf, hs_hbm.at[pl.ds(0, MOE_BLOCK * SUBLANES), :], zsem).wait()
            return carry

        lax.fori_loop(first_unused, n_blocks, zero_start, 0)
        lax.fori_loop(first_unused, n_blocks, zero_wait, 0)

    hb = hbuf.at[slot]
    _to_row_tiles(hb, _rms(x_ref[...], g_ref[...]))
    for j in range(td):
        for c in range(2):
            row = pl.multiple_of(dest_ref[0, c * td + j] * SUBLANES, SUBLANES)
            pltpu.make_async_copy(hb.at[pl.ds(j * SUBLANES, SUBLANES), :],
                                  hs_hbm.at[pl.ds(row, SUBLANES), :],
                                  sems.at[slot]).start(priority=c)

    @pl.when(i > 0)
    def _():
        other = hbuf.at[1 - slot]
        _row_bytes_wait(hs_hbm, other, sems.at[1 - slot])
        _row_bytes_wait(hs_hbm, other, sems.at[1 - slot])

    @pl.when(i == nt - 1)
    def _():
        _row_bytes_wait(hs_hbm, hb, sems.at[slot])
        _row_bytes_wait(hs_hbm, hb, sems.at[slot])


def _moe_dispatch(x, g, pends, counts, dest3, P, td):
    T, D = x.shape
    assert D == ROW_TILE
    grid_spec = pltpu.PrefetchScalarGridSpec(
        num_scalar_prefetch=2,
        grid=(T // td,),
        in_specs=[pl.BlockSpec((None, 1, 2 * td), lambda i, pe, cn: (i, 0, 0), memory_space=pltpu.SMEM),
                  pl.BlockSpec((td, D), lambda i, pe, cn: (i, 0)),
                  pl.BlockSpec((1, D), lambda i, pe, cn: (0, 0))],
        out_specs=pl.BlockSpec(memory_space=pl.ANY),
        scratch_shapes=[pltpu.VMEM((2, td * SUBLANES, LANES), F32),
                        pltpu.VMEM((MOE_BLOCK * SUBLANES, LANES), F32),
                        pltpu.SemaphoreType.DMA((2,)),
                        pltpu.SemaphoreType.DMA(())],
    )
    return pl.pallas_call(
        functools.partial(_moe_dispatch_kernel, td=td),
        grid_spec=grid_spec,
        out_shape=jax.ShapeDtypeStruct((P * SUBLANES, LANES), F32),
        compiler_params=_cparams(1),
        name="moe_dispatch",
    )(pends, counts, dest3, x, g.reshape(1, D))


def _moe_expert_kernel(be_ref, nu_ref, hs_ref, w1_ref, w3_ref, w2_ref, o_ref, w1b, w3b, w2b):
    i = pl.program_id(0)
    used = i < nu_ref[0]
    changed = jnp.logical_or(i == 0, be_ref[i] != be_ref[jnp.maximum(i - 1, 0)])

    @pl.when(jnp.logical_and(used, changed))
    def _():
        w1b[...] = w1_ref[...].astype(BF16)
        w3b[...] = w3_ref[...].astype(BF16)
        w2b[...] = w2_ref[...].astype(BF16)

    @pl.when(used)
    def _():
        xe = _from_row_tiles(hs_ref).astype(BF16)
        ff = w1b.shape[1]
        halves = [slice(0, ff // 2), slice(ff // 2, ff)]
        ups = [(jnp.dot(xe, w1b[:, sl], preferred_element_type=F32),
                jnp.dot(xe, w3b[:, sl], preferred_element_type=F32)) for sl in halves]
        act = [(a * _sigmoid(a) * b).astype(BF16) for a, b in ups]
        y = sum(jnp.dot(a, w2b[sl, :], preferred_element_type=F32) for a, sl in zip(act, halves))
        _to_row_tiles(o_ref, y)

    @pl.when(jnp.logical_not(used))
    def _():
        o_ref[...] = jnp.zeros_like(o_ref)


def _moe_experts(hs, block_e, n_used, w1, w3, w2, layer):
    P = hs.shape[0] // SUBLANES
    D = ROW_TILE
    FF = w1.shape[-1]
    NB = P // MOE_BLOCK
    last = lambda i, nu: jnp.minimum(i, nu[0] - 1)
    grid_spec = pltpu.PrefetchScalarGridSpec(
        num_scalar_prefetch=2,
        grid=(NB,),
        in_specs=[pl.BlockSpec((MOE_BLOCK * SUBLANES, LANES), lambda i, be, nu: (last(i, nu), 0)),
                  pl.BlockSpec((None, None, D, FF), lambda i, be, nu: (layer, be[last(i, nu)], 0, 0)),
                  pl.BlockSpec((None, None, D, FF), lambda i, be, nu: (layer, be[last(i, nu)], 0, 0)),
                  pl.BlockSpec((None, None, FF, D), lambda i, be, nu: (layer, be[last(i, nu)], 0, 0))],
        out_specs=pl.BlockSpec((MOE_BLOCK * SUBLANES, LANES), lambda i, be, nu: (i, 0)),
        scratch_shapes=[pltpu.VMEM((D, FF), BF16),
                        pltpu.VMEM((D, FF), BF16),
                        pltpu.VMEM((FF, D), BF16)],
    )
    return pl.pallas_call(
        _moe_expert_kernel,
        grid_spec=grid_spec,
        out_shape=jax.ShapeDtypeStruct((P * SUBLANES, LANES), F32),
        compiler_params=_cparams(1),
        name="moe_experts",
    )(block_e, n_used, hs, w1, w3, w2)


def _gather_rows(src_hbm, idx_ref, dst_ref, sem, first, last):
    for r in range(first, last):
        row = pl.multiple_of(idx_ref[0, r] * SUBLANES, SUBLANES)
        pltpu.make_async_copy(src_hbm.at[pl.ds(row, SUBLANES), :],
                              dst_ref.at[pl.ds(r * SUBLANES, SUBLANES), :], sem).start(priority=r % 2)


def _moe_combine_kernel(pos_ref, posn_ref, x_ref, info_ref, yb_hbm, g_ref, *rest, tc, final_norm, splits):
    if splits:
        w_ref, o_ref = rest[0], rest[1]
        p_refs = rest[2:2 + len(splits)]
        ybuf, sems = rest[2 + len(splits):]
    else:
        o_ref, ybuf, sems = rest
    i = pl.program_id(0)
    nb = pl.num_programs(0)
    slot = lax.rem(i, 2)

    @pl.when(i == 0)
    def _():
        def issue(r, carry):
            src = pl.multiple_of(pos_ref[0, r] * SUBLANES, SUBLANES)
            dst = pl.multiple_of(r * SUBLANES, SUBLANES)
            pltpu.make_async_copy(yb_hbm.at[pl.ds(src, SUBLANES), :],
                                  ybuf.at[0, pl.ds(dst, SUBLANES), :], sems.at[0]).start()
            return carry
        lax.fori_loop(0, 2 * tc, issue, 0)

    def wait_tile(s_):
        pltpu.make_async_copy(yb_hbm.at[pl.ds(0, 2 * tc * SUBLANES), :], ybuf.at[s_], sems.at[s_]).wait()

    if not splits:
        @pl.when(i + 1 < nb)
        def _():
            _gather_rows(yb_hbm, posn_ref, ybuf.at[1 - slot], sems.at[1 - slot], 0, 2 * tc)

    wait_tile(slot)
    info = info_ref[...]
    yb = ybuf.at[slot]
    y0 = _from_row_tiles(yb.at[pl.ds(0, tc * SUBLANES), :])
    y1 = _from_row_tiles(yb.at[pl.ds(tc * SUBLANES, tc * SUBLANES), :])
    out = x_ref[...] + (y0 * info[:, 4:5] + y1 * info[:, 5:6])
    if final_norm:
        out = _rms(out, g_ref[...])
    o_ref[...] = out
    if splits:
        h = _rms(out, g_ref[...]).astype(BF16)
        chunks = []
        for p_ref, n in zip(p_refs, splits):
            col0 = sum(c[2] for c in chunks)
            for c0 in range(0, n, PROJ_CHUNK):
                chunks.append((p_ref, c0, min(PROJ_CHUNK, n - c0), col0))
        per = -(-2 * tc // len(chunks))
        off = 0
        for ci, (p_ref, c0, width, _) in enumerate(chunks):
            val = jnp.dot(h, w_ref[:, off:off + width], preferred_element_type=F32)
            _gather_rows(yb_hbm, posn_ref, ybuf.at[1 - slot], sems.at[1 - slot],
                         min(ci * per, 2 * tc), min((ci + 1) * per, 2 * tc))
            p_ref[:, c0:c0 + width] = val.astype(p_ref.dtype)
            off += width

        @pl.when(i == nb - 1)
        def _():
            wait_tile(1 - slot)


def _moe_combine(x, info, dest3, yb, g, final_norm, tc, next_proj=None):
    T, D = x.shape
    NT = T // tc
    in_specs = [pl.BlockSpec((None, 1, 2 * tc), lambda i: (i, 0, 0), memory_space=pltpu.SMEM),
                pl.BlockSpec((None, 1, 2 * tc), lambda i: (jnp.minimum(i + 1, NT - 1), 0, 0),
                             memory_space=pltpu.SMEM),
                pl.BlockSpec((tc, D), lambda i: (i, 0)),
                pl.BlockSpec((tc, LANES), lambda i: (i, 0)),
                pl.BlockSpec(memory_space=pl.ANY),
                pl.BlockSpec((1, D), lambda i: (0, 0))]
    out_specs = [pl.BlockSpec((tc, D), lambda i: (i, 0))]
    out_shape = [jax.ShapeDtypeStruct((T, D), F32)]
    args = [dest3, dest3, x, info, yb, g.reshape(1, D)]
    splits = ()
    if next_proj is not None:
        w, splits, out_dtypes = next_proj
        assert not final_norm and sum(splits) == w.shape[1]
        in_specs.append(pl.BlockSpec(w.shape, lambda i: (0, 0)))
        args.append(w)
        out_specs += [pl.BlockSpec((tc, n), lambda i: (i, 0)) for n in splits]
        out_shape += [jax.ShapeDtypeStruct((T, n), dt) for n, dt in zip(splits, out_dtypes)]
    outs = pl.pallas_call(
        functools.partial(_moe_combine_kernel, tc=tc, final_norm=final_norm, splits=tuple(splits)),
        grid=(NT,),
        in_specs=in_specs,
        out_specs=out_specs,
        out_shape=out_shape,
        scratch_shapes=[pltpu.VMEM((2, 2 * tc * SUBLANES, LANES), F32), pltpu.SemaphoreType.DMA((2,))],
        compiler_params=_cparams(1),
        name="moe_combine",
    )(*args)
    return outs[0], tuple(outs[1:])


MOE_TILE = 256
MOE_COMBINE_TILE = 256
PROJ_CHUNK = 256
ROUTER_TILE = 256


def _tile_slots(dest, tile):
    n_tiles = dest.shape[0] * dest.shape[2] // tile
    return jnp.concatenate([dest[:, 0, :].reshape(n_tiles, 1, tile), dest[:, 1, :].reshape(n_tiles, 1, tile)], axis=2)


def _router_params(w_group, b_group, w_expert, b_expert, lanes_out):
    D = w_group.shape[0]
    n_log = MOE_GROUPS + MOE_EXPERTS
    wt = jnp.zeros((LANES, D), F32).at[:MOE_GROUPS].set(w_group.T).at[MOE_GROUPS:n_log].set(w_expert.T)
    bt = jnp.zeros((LANES,), F32).at[:MOE_GROUPS].set(b_group).at[MOE_GROUPS:n_log].set(b_expert)
    return wt.astype(BF16), jnp.broadcast_to(bt[:, None], (LANES, lanes_out))


def _moe_layer(x, g, logits, w1, w3, w2, layer, g_out, final_norm, next_proj):
    T, D = x.shape
    n_log = MOE_GROUPS + MOE_EXPERTS
    info, slot, cnt = _router(logits, ROUTER_TILE)
    P = 2 * T + MOE_EXPERTS * MOE_BLOCK
    NB = P // MOE_BLOCK
    counts = cnt[MOE_GROUPS:n_log, 0].astype(I32)
    padded = (counts + MOE_BLOCK - 1) // MOE_BLOCK * MOE_BLOCK
    pends = jnp.cumsum(padded).astype(I32)
    pstarts = pends - padded
    block_start = jnp.arange(NB, dtype=I32) * MOE_BLOCK
    block_e = jnp.minimum(jnp.sum((pends[None, :] <= block_start[:, None]).astype(I32), axis=1),
                          MOE_EXPERTS - 1).astype(I32)
    n_used = (pends[-1:] // MOE_BLOCK).astype(I32)
    eid = slot[:, 0:2, :].astype(I32)
    expert_ids = jnp.arange(MOE_EXPERTS, dtype=I32)
    seg_start = jnp.sum(jnp.where(eid[..., None] == expert_ids, pstarts, 0), axis=-1)
    dest = seg_start + slot[:, 2:4, :].astype(I32)
    hs = _moe_dispatch(x, g, pends, counts, _tile_slots(dest, MOE_TILE), P, MOE_TILE)
    yb = _moe_experts(hs, block_e, n_used, w1, w3, w2, layer)
    return _moe_combine(x, info, _tile_slots(dest, MOE_COMBINE_TILE), yb, g_out, final_norm, MOE_COMBINE_TILE,
                        next_proj)


def kernel(x, mem, norm_mix, norm_xattn, norm_moe, norm_final, ev_w_in, ev_sinks, ev_mu, ev_w0, ev_w2, ev_a0, ev_a2, ev_g2, ev_k_k, ev_k_a, ev_r_k, ev_lnx_w, ev_lnx_b, ev_w_out, od_w_in, od_gate_up, od_gate_b, od_onorm, od_w_out, mem_norm, mem_wk, mem_wv, xa_wq, xa_wo, moe_w_group, moe_b_group, moe_w_expert, moe_b_expert, moe_w1, moe_w3, moe_w2):
    B, S, D = x.shape
    M = mem.shape[1]
    T = B * S
    depth = norm_mix.shape[0]
    xf = x.reshape(T, D)

    XW = XA_HEADS * XA_HEAD_DIM
    w_kv = jnp.concatenate([mem_wk, mem_wv], axis=1).astype(BF16)
    mk, mv = _norm_matmul(mem.reshape(B * M, D), mem_norm, w_kv, (XW, XW), (BF16, BF16))

    KW = GLA_HEADS * GLA_DK
    VW = GLA_HEADS * GLA_DV
    swa_cols = SWA_Q_HEADS * HEAD_DIM + 2 * (SWA_Q_HEADS // SWA_GROUP) * HEAD_DIM

    def in_proj(layer):
        i = layer // 2
        if layer % 2 == 0:
            return ev_w_in[i].astype(BF16), (swa_cols, ev_w_in.shape[-1] - swa_cols), (F32, F32)
        R = od_gate_up.shape[1]
        w = od_w_in[i]
        w_re = jnp.concatenate([w[:, :2 * KW + VW], w[:, 2 * KW + VW + R:],
                                w[:, 2 * KW + VW:2 * KW + VW + R],
                                jnp.zeros((D, LANES - R), F32)], axis=1).astype(BF16)
        return w_re, (2 * KW + 2 * VW, LANES), (F32, F32)

    w_first, _, _ = in_proj(0)
    qkv0, rw0 = _in_proj_rwkv(xf, norm_mix[0], w_first, swa_cols, ev_mu[0], ev_w0[0], ev_w2[0], ev_a0[0], ev_a2[0],
                              ev_g2[0], ev_k_k[0], ev_k_a[0], S)
    proj = None
    for layer in range(depth):
        i = layer // 2
        if layer % 2 == 0:
            if layer == 0:
                qkv, rw = qkv0, [t.reshape(B, S, RWKV_WIDTH) for t in rw0]
            else:
                qkv, p_rw = proj
                rw = _rwkv_prep(p_rw, ev_mu[i], ev_w0[i], ev_w2[i], ev_a0[i], ev_a2[i],
                                ev_g2[i], ev_k_k[i], ev_k_a[i], B, S)
            o_a = _swa(qkv, ev_sinks[i], B, S)
            o_b = _rwkv_scan(*rw, ev_r_k[i].reshape(-1), ev_lnx_w[i], ev_lnx_b[i])
            w_out = ev_w_out[i].astype(BF16)
            qw = o_a.shape[-1]
            mix_acts, mix_ws = [o_a, o_b], [w_out[:qw], w_out[qw:]]
        else:
            qkvo, gd = proj
            R = od_gate_up.shape[1]
            gup = jnp.zeros((LANES, KW), F32).at[:R].set(od_gate_up[i]).astype(BF16)
            o = _gla(qkvo, gd, gup, od_gate_b[i], od_onorm[i], B, S)
            mix_acts, mix_ws = [o], [od_w_out[i].astype(BF16)]
        wt_router, bt_router = _router_params(moe_w_group[layer], moe_b_group[layer], moe_w_expert[layer],
                                              moe_b_expert[layer], XA_SUB_ROWS)
        xf, logits = _mix_proj_xattn(xf, mix_acts, mix_ws, norm_xattn[layer], xa_wq[layer].astype(BF16), mk, mv,
                                     xa_wo[layer].astype(BF16), norm_moe[layer], wt_router, bt_router, B, S)
        last = layer == depth - 1
        g_out = norm_final if last else norm_mix[layer + 1]
        xf, proj = _moe_layer(xf, norm_moe[layer], logits, moe_w1, moe_w3, moe_w2, layer,
                              g_out, last, None if last else in_proj(layer + 1))
    return xf.reshape(B, S, D)
```

```python
import functools

import jax
import jax.numpy as jnp
from jax import lax
from jax.experimental import pallas as pl
from jax.experimental.pallas import tpu as pltpu

F32 = jnp.float32
BF16 = jnp.bfloat16
I32 = jnp.int32

EPS = 1e-6
HEAD_DIM = 64
SWA_WINDOW = 128
SWA_Q_HEADS = 8
SWA_GROUP = 4
RWKV_HEADS = 8
RWKV_WIDTH = 512
RWKV_LN_EPS = 64e-5
RWKV_CHUNK = 64
GLA_HEADS = 4
GLA_DK = 128
GLA_DV = 256
GLA_CHUNK = 64
GLA_GATE_NORM = 16.0
XA_HEADS = 4
XA_HEAD_DIM = 128
MOE_GROUPS = 4
MOE_EXPERTS_PER_GROUP = 8
MOE_EXPERTS = 32
MOE_BLOCK = 512
LANES = 128
SUBLANES = 8
ROW_TILE = SUBLANES * LANES

VMEM_LIMIT_BYTES = 48 * 1024 * 1024


def _cparams(n_axes):
    return pltpu.CompilerParams(dimension_semantics=("arbitrary",) * n_axes,
                                vmem_limit_bytes=VMEM_LIMIT_BYTES)


def _dot(a, b):
    return jnp.dot(a.astype(BF16), b.astype(BF16), preferred_element_type=F32)


def _dot_nt(a, b):
    return lax.dot_general(a.astype(BF16), b.astype(BF16), (((1,), (1,)), ((), ())),
                           preferred_element_type=F32)


def _dot_tn(a, b):
    return lax.dot_general(a.astype(BF16), b.astype(BF16), (((0,), (0,)), ((), ())),
                           preferred_element_type=F32)


def _dot_f32(a, b):
    return jnp.dot(a, b, preferred_element_type=F32, precision=lax.Precision.HIGHEST)


def _rms(x, g):
    ms = jnp.mean(x * x, axis=-1, keepdims=True)
    return x * lax.rsqrt(ms + EPS) * g


def _sigmoid(x):
    return 1.0 / (1.0 + jnp.exp(-x))


def _softplus(x):
    return jnp.maximum(x, 0.0) + jnp.log(1.0 + jnp.exp(-jnp.abs(x)))


def _norm_matmul_kernel(x_ref, g_ref, w_ref, *o_refs, splits):
    h = _rms(x_ref[...], g_ref[...]).astype(BF16)
    off = 0
    for o_ref, n in zip(o_refs, splits):
        o_ref[...] = jnp.dot(h, w_ref[:, off:off + n], preferred_element_type=F32).astype(o_ref.dtype)
        off += n


def _norm_matmul(x, g, w, splits, out_dtypes, tm=256):
    T, D = x.shape
    N = w.shape[1]
    assert sum(splits) == N and T % tm == 0
    return pl.pallas_call(
        functools.partial(_norm_matmul_kernel, splits=tuple(splits)),
        grid=(T // tm,),
        in_specs=[pl.BlockSpec((tm, D), lambda i: (i, 0)),
                  pl.BlockSpec((1, D), lambda i: (0, 0)),
                  pl.BlockSpec((D, N), lambda i: (0, 0))],
        out_specs=[pl.BlockSpec((tm, n), lambda i: (i, 0)) for n in splits],
        out_shape=[jax.ShapeDtypeStruct((T, n), dt) for n, dt in zip(splits, out_dtypes)],
        compiler_params=_cparams(1),
        name="norm_matmul",
    )(x, g.reshape(1, D), w)


def _swa_kernel(sinks_ref, q_ref, kp_ref, kc_ref, vp_ref, vc_ref, o_ref):
    n = pl.program_id(1)
    W = SWA_WINDOW
    NB = q_ref.shape[0]
    qpos = lax.broadcasted_iota(I32, (W, 2 * W), 0) + W
    kpos = lax.broadcasted_iota(I32, (W, 2 * W), 1)
    rel = qpos - kpos
    in_window = jnp.where(rel >= 0, jnp.where(rel < W, 1, 0), 0)
    has_prev = jnp.where(n > 0, 1, 0)
    valid = (in_window * jnp.where(kpos >= W, 1, has_prev)) > 0
    n_groups = SWA_Q_HEADS // SWA_GROUP
    streams = [(bi, g) for bi in range(NB) for g in range(n_groups)]
    qb = [q_ref[bi].astype(BF16) for bi in range(NB)]
    kb = [jnp.concatenate([kp_ref[bi], kc_ref[bi]], axis=0).astype(BF16) for bi in range(NB)]
    vb = [jnp.concatenate([vp_ref[bi], vc_ref[bi]], axis=0).astype(BF16) for bi in range(NB)]
    gs = lambda g: slice(g * HEAD_DIM, (g + 1) * HEAD_DIM)
    scores = []
    for bi, g in streams:
        qg = jnp.concatenate([qb[bi][:, h * HEAD_DIM:(h + 1) * HEAD_DIM]
                              for h in range(g * SWA_GROUP, (g + 1) * SWA_GROUP)], axis=0)
        scores.append(_dot_nt(qg, kb[bi][:, gs(g)]))
    probs = []
    for i, (bi, g) in enumerate(streams):
        pieces = []
        for j in range(SWA_GROUP):
            s = jnp.where(valid, scores[i][j * W:(j + 1) * W] * (HEAD_DIM ** -0.5), -jnp.inf)
            sink = sinks_ref[g * SWA_GROUP + j]
            m = jnp.maximum(jnp.max(s, axis=-1, keepdims=True), sink)
            p = jnp.exp(s - m)
            den = jnp.sum(p, axis=-1, keepdims=True) + jnp.exp(sink - m)
            pieces.append((p / den).astype(BF16))
        probs.append(jnp.concatenate(pieces, axis=0))
    ogs = [_dot(probs[i], vb[bi][:, gs(g)]) for i, (bi, g) in enumerate(streams)]
    for bi in range(NB):
        outs = []
        for g in range(n_groups):
            og = ogs[bi * n_groups + g]
            outs += [og[j * W:(j + 1) * W] for j in range(SWA_GROUP)]
        o_ref[bi] = jnp.concatenate(outs, axis=1).astype(o_ref.dtype)


SWA_BATCH_ROWS = 2


def _swa(qkv, sinks, B, S):
    W = SWA_WINDOW
    qkv3 = qkv.reshape(B, S, qkv.shape[-1])
    qw = SWA_Q_HEADS * HEAD_DIM
    kw = qw // SWA_GROUP
    kcol = qw // kw
    nb = SWA_BATCH_ROWS if B % SWA_BATCH_ROWS == 0 else 1
    out = pl.pallas_call(
        _swa_kernel,
        grid=(B // nb, S // W),
        in_specs=[pl.BlockSpec(memory_space=pltpu.SMEM),
                  pl.BlockSpec((nb, W, qw), lambda b, n: (b, n, 0)),
                  pl.BlockSpec((nb, W, kw), lambda b, n: (b, jnp.maximum(n - 1, 0), kcol)),
                  pl.BlockSpec((nb, W, kw), lambda b, n: (b, n, kcol)),
                  pl.BlockSpec((nb, W, kw), lambda b, n: (b, jnp.maximum(n - 1, 0), kcol + 1)),
                  pl.BlockSpec((nb, W, kw), lambda b, n: (b, n, kcol + 1))],
        out_specs=pl.BlockSpec((nb, W, qw), lambda b, n: (b, n, 0)),
        out_shape=jax.ShapeDtypeStruct((B, S, qw), BF16),
        compiler_params=_cparams(2),
        name="swa",
    )(sinks, qkv3, qkv3, qkv3, qkv3, qkv3)
    return out.reshape(B * S, qw)


def _rwkv_prep_math(p, last, mu_ref, w0_ref, w2_ref, a0_ref, a2_ref, g2_ref, kk_ref, ka_ref, outs):
    r_out, lw_out, k_out, v_out, a_out, b_out, g_out = outs
    C = RWKV_WIDTH
    row = lax.broadcasted_iota(I32, p.shape, 0)
    p_prev = jnp.where(row == 0, last, pltpu.roll(p, 1, axis=0))
    p = p + (p_prev - p) * mu_ref[...]
    r = p[:, :C]
    k = p[:, C:2 * C]
    v = p[:, 2 * C:3 * C]
    xw = p[:, 3 * C:3 * C + 64]
    xa = p[:, 3 * C + 64:3 * C + 128]
    xg = p[:, 3 * C + 128:]
    w = -_softplus(-(w0_ref[...] + _dot(jnp.tanh(xw), w2_ref[...]))) - 0.5
    lw = -jnp.exp(w)
    a = _sigmoid(a0_ref[...] + _dot(xa, a2_ref[...]))
    g = _dot(_sigmoid(xg), g2_ref[...])
    kk = k * kk_ref[...]
    pieces = []
    for h in range(RWKV_HEADS):
        kh = kk[:, h * HEAD_DIM:(h + 1) * HEAD_DIM]
        nrm = jnp.sqrt(jnp.sum(kh * kh, axis=-1, keepdims=True))
        pieces.append(kh / jnp.maximum(nrm, 1e-12))
    kk = jnp.concatenate(pieces, axis=1)
    r_out[...] = r
    lw_out[...] = lw
    k_out[...] = k * (1.0 + (a - 1.0) * ka_ref[...])
    v_out[...] = v
    a_out[...] = -kk
    b_out[...] = kk * a
    g_out[...] = g


def _rwkv_prep_kernel(p_ref, pprev_ref, *refs):
    n = pl.program_id(1)
    last = jnp.where(n > 0, pprev_ref[7:8, :], 0.0)
    _rwkv_prep_math(p_ref[...], last, *refs[:8], refs[8:])


def _rwkv_params(mu, w0, w2, a0, a2, g2, k_k, k_a):
    row = lambda t: t.reshape(1, -1)
    return [row(mu), row(w0), w2.astype(BF16), row(a0), a2.astype(BF16), g2.astype(BF16), row(k_k), row(k_a)]


def _rwkv_prep(p, mu, w0, w2, a0, a2, g2, k_k, k_a, B, S, tt=256):
    C = RWKV_WIDTH
    PW = p.shape[-1]
    p3 = p.reshape(B, S, PW)
    full = lambda arr: pl.BlockSpec(arr.shape, lambda b, n: (0,) * arr.ndim)
    params = _rwkv_params(mu, w0, w2, a0, a2, g2, k_k, k_a)
    outs = pl.pallas_call(
        _rwkv_prep_kernel,
        grid=(B, S // tt),
        in_specs=[pl.BlockSpec((None, tt, PW), lambda b, n: (b, n, 0)),
                  pl.BlockSpec((None, 8, PW), lambda b, n: (b, jnp.maximum(n * (tt // 8) - 1, 0), 0))]
                 + [full(t) for t in params],
        out_specs=[pl.BlockSpec((None, tt, C), lambda b, n: (b, n, 0))] * 7,
        out_shape=[jax.ShapeDtypeStruct((B, S, C), F32)] * 7,
        compiler_params=_cparams(2),
        name="rwkv_prep",
    )(p3, p3, *params)
    return outs


def _in_proj_rwkv_kernel(x_ref, g_ref, w_ref, *refs, swa_cols, tiles_per_seq):
    params = refs[:8]
    qkv_out = refs[8]
    outs = refs[9:16]
    p_buf, last_buf = refs[16:]
    i = pl.program_id(0)

    @pl.when(i == 0)
    def _():
        p_buf[...] = jnp.zeros_like(p_buf)
        last_buf[...] = jnp.zeros_like(last_buf)

    j = i - 1
    p_prev_tile = p_buf[lax.rem(i + 1, 2)]
    tm = p_prev_tile.shape[0]
    last = jnp.where(lax.rem(j, tiles_per_seq) == 0, 0.0, last_buf[...])
    _rwkv_prep_math(p_prev_tile, last, *params, outs)
    last_buf[...] = p_prev_tile[tm - 1:tm, :]
    h = _rms(x_ref[...], g_ref[...]).astype(BF16)
    qkv_out[...] = jnp.dot(h, w_ref[:, :swa_cols], preferred_element_type=F32)
    p_buf[lax.rem(i, 2)] = jnp.dot(h, w_ref[:, swa_cols:], preferred_element_type=F32)


def _in_proj_rwkv(x, g, w, swa_cols, mu, w0, w2, a0, a2, g2, k_k, k_a, S, tm=256):
    T, D = x.shape
    N = w.shape[1]
    C = RWKV_WIDTH
    NT = T // tm
    params = _rwkv_params(mu, w0, w2, a0, a2, g2, k_k, k_a)
    const = lambda arr: pl.BlockSpec(arr.shape, lambda i: (0,) * arr.ndim)
    cur = lambda i: (jnp.minimum(i, NT - 1), 0)
    prev = lambda i: (jnp.maximum(i - 1, 0), 0)
    outs = pl.pallas_call(
        functools.partial(_in_proj_rwkv_kernel, swa_cols=swa_cols, tiles_per_seq=S // tm),
        grid=(NT + 1,),
        in_specs=[pl.BlockSpec((tm, D), cur), pl.BlockSpec((1, D), lambda i: (0, 0)), const(w)]
                 + [const(t) for t in params],
        out_specs=[pl.BlockSpec((tm, swa_cols), cur)] + [pl.BlockSpec((tm, C), prev)] * 7,
        out_shape=[jax.ShapeDtypeStruct((T, swa_cols), F32)] + [jax.ShapeDtypeStruct((T, C), F32)] * 7,
        scratch_shapes=[pltpu.VMEM((2, tm, N - swa_cols), F32), pltpu.VMEM((1, N - swa_cols), F32)],
        compiler_params=_cparams(1),
        name="in_proj_rwkv",
    )(x, g.reshape(1, D), w, *params)
    return outs[0], outs[1:]


def _pair_blockdiag(x):
    lane = lax.broadcasted_iota(I32, x.shape, 1)
    zero = jnp.zeros_like(x)
    return jnp.concatenate([jnp.where(lane < HEAD_DIM, x, zero), jnp.where(lane >= HEAD_DIM, x, zero)], axis=0)


def _rwkv_scan_kernel(r_ref, lw_ref, k_ref, v_ref, a_ref, b_ref, g_ref, rk_ref, lnw_ref, lnb_ref,
                      o_ref, s_ref):
    c = pl.program_id(1)

    @pl.when(c == 0)
    def _():
        s_ref[...] = jnp.zeros_like(s_ref)

    C = RWKV_CHUNK
    NB = r_ref.shape[0]
    NP = RWKV_HEADS // 2
    PW = 2 * HEAD_DIM
    row = lax.broadcasted_iota(I32, (C, C), 0)
    col = lax.broadcasted_iota(I32, (C, C), 1)
    tri = jnp.where(row >= col, 1.0, 0.0).astype(F32)
    rowp = lax.broadcasted_iota(I32, (C, PW), 0)
    colp = lax.broadcasted_iota(I32, (C, PW), 1)
    colp = jnp.where(colp >= HEAD_DIM, colp - HEAD_DIM, colp)
    lower_p = rowp >= colp
    strict_p = rowp > colp
    rows = lax.broadcasted_iota(I32, (PW, PW), 0)
    cols = lax.broadcasted_iota(I32, (PW, PW), 1)
    same_head = jnp.where(rows >= HEAD_DIM, 1, 0) == jnp.where(cols >= HEAD_DIM, 1, 0)
    first = lax.broadcasted_iota(I32, (C, PW), 1) < HEAD_DIM

    streams = [(bi, p) for bi in range(NB) for p in range(NP)]
    pre = []
    for bi in range(NB):
        lw = lw_ref[bi]
        cum = _dot_f32(tri, lw)
        cum_last = cum[C - 1:C, :]
        r = r_ref[bi]
        k = k_ref[bi]
        v = v_ref[bi]
        a = a_ref[bi]
        b = b_ref[bi]
        e_neg = jnp.exp(-cum)
        e_rem = jnp.exp(cum_last - cum)
        pre.append(dict(
            r_t=(r * jnp.exp(cum)).astype(BF16), a_t=(a * jnp.exp(cum - lw)).astype(BF16),
            b_t=(b * e_neg).astype(BF16), k_t=(k * e_neg).astype(BF16),
            b_d=(b * e_rem).astype(BF16), k_d=(k * e_rem).astype(BF16),
            v_b=v.astype(BF16), v=v, e_last=jnp.exp(cum_last), rkk=r * k * rk_ref[...], g=g_ref[bi]))

    def lanes(p):
        return slice(p * PW, (p + 1) * PW)

    ar = [jnp.concatenate([pre[bi]['a_t'][:, lanes(p)], pre[bi]['r_t'][:, lanes(p)]], axis=0) for bi, p in streams]
    s0 = [s_ref[bi, p] for bi, p in streams]
    big = [_dot_nt(ar[i], jnp.concatenate([_pair_blockdiag(pre[bi]['b_t'][:, lanes(p)]),
                                           _pair_blockdiag(pre[bi]['k_t'][:, lanes(p)]),
                                           s0[i].astype(BF16)], axis=0))
           for i, (bi, p) in enumerate(streams)]
    m_b = [t[:, :PW] for t in big]
    m_k = [t[:, PW:2 * PW] for t in big]
    ars = [t[:, 2 * PW:] for t in big]
    v_p = [pre[bi]['v_b'][:, lanes(p)] for bi, p in streams]
    v_bd = [_pair_blockdiag(vp) for vp in v_p]
    x = [ars[i][:C] + _dot(jnp.where(strict_p, m_k[i][:C], 0.0), v_bd[i]) for i in range(len(streams))]
    pw = [jnp.where(strict_p, m_b[i][:C], 0.0).astype(BF16) for i in range(len(streams))]
    n_stages = 6
    for stage in range(n_stages):
        if stage < n_stages - 1:
            prod = [_dot(pw[i], jnp.concatenate([_pair_blockdiag(x[i].astype(BF16)), _pair_blockdiag(pw[i])], axis=1))
                    for i in range(len(streams))]
            x = [x[i] + prod[i][:, :PW] for i in range(len(streams))]
            pw = [prod[i][:, PW:].astype(BF16) for i in range(len(streams))]
        else:
            x = [x[i] + _dot(pw[i], _pair_blockdiag(x[i].astype(BF16))) for i in range(len(streams))]
    u_b = [xi.astype(BF16) for xi in x]
    y = [ars[i][C:]
         + _dot(jnp.concatenate([jnp.where(lower_p, m_b[i][C:], 0.0), jnp.where(lower_p, m_k[i][C:], 0.0)], axis=1),
                jnp.concatenate([_pair_blockdiag(u_b[i]), v_bd[i]], axis=0))
         for i in range(len(streams))]
    for i, (bi, p) in enumerate(streams):
        upd = _dot_tn(jnp.concatenate([u_b[i], v_p[i]], axis=0),
                      jnp.concatenate([pre[bi]['b_d'][:, lanes(p)], pre[bi]['k_d'][:, lanes(p)]], axis=0))
        s_ref[bi, p] = s0[i] * pre[bi]['e_last'][:, lanes(p)] + jnp.where(same_head, upd, 0.0)

    lnw = lnw_ref[...]
    lnb = lnb_ref[...]

    def head_sum(t):
        s1 = jnp.sum(jnp.where(first, t, 0.0), axis=-1, keepdims=True)
        s2 = jnp.sum(jnp.where(first, 0.0, t), axis=-1, keepdims=True)
        return jnp.where(first, s1, s2)

    for bi in range(NB):
        outs = []
        for p in range(NP):
            yi = y[bi * NP + p]
            mean = head_sum(yi) * (1.0 / HEAD_DIM)
            yc = yi - mean
            var = head_sum(yc * yc) * (1.0 / HEAD_DIM)
            yn = yc * lax.rsqrt(var + RWKV_LN_EPS) * lnw[:, lanes(p)] + lnb[:, lanes(p)]
            bonus = head_sum(pre[bi]['rkk'][:, lanes(p)]) * pre[bi]['v'][:, lanes(p)]
            outs.append((yn + bonus) * pre[bi]['g'][:, lanes(p)])
        o_ref[bi] = jnp.concatenate(outs, axis=1).astype(o_ref.dtype)


RWKV_BATCH_ROWS = 4


def _rwkv_scan(r, lw, k, v, a, b, g, r_k, lnx_w, lnx_b):
    B, S, W = r.shape
    C = RWKV_CHUNK
    nb = RWKV_BATCH_ROWS if B % RWKV_BATCH_ROWS == 0 else 1
    seq = pl.BlockSpec((nb, C, W), lambda bb, c: (bb, c, 0))
    par = pl.BlockSpec((1, W), lambda bb, c: (0, 0))
    out = pl.pallas_call(
        _rwkv_scan_kernel,
        grid=(B // nb, S // C),
        in_specs=[seq] * 7 + [par] * 3,
        out_specs=seq,
        out_shape=jax.ShapeDtypeStruct((B, S, W), BF16),
        scratch_shapes=[pltpu.VMEM((nb, RWKV_HEADS // 2, 2 * HEAD_DIM, 2 * HEAD_DIM), F32)],
        compiler_params=_cparams(2),
        name="rwkv_scan",
    )(r, lw, k, v, a, b, g, r_k.reshape(1, W), lnx_w.reshape(1, W), lnx_b.reshape(1, W))
    return out.reshape(B * S, W)


def _gla_kernel(q_ref, k_ref, v_ref, og_ref, gd_ref, gup_ref, gb_ref, on_ref, o_ref, s_ref):
    c = pl.program_id(1)

    @pl.when(c == 0)
    def _():
        s_ref[...] = jnp.zeros_like(s_ref)

    C = GLA_CHUNK
    NB = q_ref.shape[0]
    row = lax.broadcasted_iota(I32, (C, C), 0)
    col = lax.broadcasted_iota(I32, (C, C), 1)
    lower = row >= col
    tri = jnp.where(lower, 1.0, 0.0).astype(F32)
    onorm = on_ref[...]
    zs = [_dot(gd_ref[bi], gup_ref[...]) + gb_ref[...] for bi in range(NB)]
    cums = [_dot_f32(tri, -_softplus(-z) / GLA_GATE_NORM) for z in zs]
    qe, ke, kd, e_last, v = [], [], [], [], []
    for bi in range(NB):
        cum = cums[bi]
        cum_last = cum[C - 1:C, :]
        k = k_ref[bi]
        qe.append((q_ref[bi] * (GLA_DK ** -0.5) * jnp.exp(cum)).astype(BF16))
        ke.append((k * jnp.exp(-cum)).astype(BF16))
        kd.append((k * jnp.exp(cum_last - cum)).astype(BF16))
        e_last.append(jnp.exp(cum_last))
        v.append(v_ref[bi].astype(BF16))
    streams = [(bi, h) for bi in range(NB) for h in range(GLA_HEADS)]
    ks = lambda h: slice(h * GLA_DK, (h + 1) * GLA_DK)
    vs = lambda h: slice(h * GLA_DV, (h + 1) * GLA_DV)
    sts = [s_ref[bi, h] for bi, h in streams]
    atts = [jnp.where(lower, _dot_nt(qe[bi][:, ks(h)], ke[bi][:, ks(h)]), 0.0) for bi, h in streams]
    inters = [_dot_nt(qe[bi][:, ks(h)], sts[i]) for i, (bi, h) in enumerate(streams)]
    os_ = [inters[i] + _dot(atts[i], v[bi][:, vs(h)]) for i, (bi, h) in enumerate(streams)]
    for i, (bi, h) in enumerate(streams):
        s_ref[bi, h] = sts[i] * e_last[bi][:, ks(h)] + _dot_tn(v[bi][:, vs(h)], kd[bi][:, ks(h)])
    for bi in range(NB):
        og = og_ref[bi]
        outs = []
        for h in range(GLA_HEADS):
            gate = og[:, vs(h)]
            outs.append(_rms(os_[bi * GLA_HEADS + h], onorm) * (gate * _sigmoid(gate)))
        o_ref[bi] = jnp.concatenate(outs, axis=1).astype(o_ref.dtype)


GLA_BATCH_ROWS = 4


def _gla(qkvo, gd, gate_up_pad, gate_b, onorm, B, S):
    C = GLA_CHUNK
    KW = GLA_HEADS * GLA_DK
    VW = GLA_HEADS * GLA_DV
    x3 = qkvo.reshape(B, S, qkvo.shape[-1])
    gd3 = gd.reshape(B, S, LANES)
    nb = GLA_BATCH_ROWS if B % GLA_BATCH_ROWS == 0 else 1
    out = pl.pallas_call(
        _gla_kernel,
        grid=(B // nb, S // C),
        in_specs=[pl.BlockSpec((nb, C, KW), lambda b, c: (b, c, 0)),
                  pl.BlockSpec((nb, C, KW), lambda b, c: (b, c, 1)),
                  pl.BlockSpec((nb, C, VW), lambda b, c: (b, c, 1)),
                  pl.BlockSpec((nb, C, VW), lambda b, c: (b, c, 2)),
                  pl.BlockSpec((nb, C, LANES), lambda b, c: (b, c, 0)),
                  pl.BlockSpec((LANES, KW), lambda b, c: (0, 0)),
                  pl.BlockSpec((1, KW), lambda b, c: (0, 0)),
                  pl.BlockSpec((1, GLA_DV), lambda b, c: (0, 0))],
        out_specs=pl.BlockSpec((nb, C, VW), lambda b, c: (b, c, 0)),
        out_shape=jax.ShapeDtypeStruct((B, S, VW), BF16),
        scratch_shapes=[pltpu.VMEM((nb, GLA_HEADS, GLA_DV, GLA_DK), F32)],
        compiler_params=_cparams(2),
        name="gla",
    )(x3, x3, x3, x3, gd3, gate_up_pad, gate_b.reshape(1, KW), onorm.reshape(1, GLA_DV))
    return out.reshape(B * S, VW)


def _xattn_kernel(*refs, n_in):
    x_ref = refs[0]
    a_refs = refs[1:1 + n_in]
    w_refs = refs[1 + n_in:1 + 2 * n_in]
    g_ref, wq_ref, mk_ref, mv_ref, wo_ref, gm_ref, wr_ref, br_ref, o_ref, lg_ref = refs[1 + 2 * n_in:]
    tq = x_ref.shape[0]
    subs = [slice(r, r + XA_SUB_ROWS) for r in range(0, tq, XA_SUB_ROWS)]
    xs = [x_ref[sub, :] for sub in subs]
    for a_ref, w_ref in zip(a_refs, w_refs):
        xs = [x + jnp.dot(a_ref[sub, :], w_ref[...], preferred_element_type=F32) for x, sub in zip(xs, subs)]
    qs = [_dot(_rms(x, g_ref[...]), wq_ref[...]).astype(BF16) for x in xs]
    mk = mk_ref[...]
    mv = mv_ref[...]
    sls = [slice(hd * XA_HEAD_DIM, (hd + 1) * XA_HEAD_DIM) for hd in range(XA_HEADS)]
    scores = [[_dot_nt(q[:, sl], mk[:, sl]) for sl in sls] for q in qs]
    probs = []
    for sc in scores:
        ps = []
        for s in sc:
            s = s * (XA_HEAD_DIM ** -0.5)
            p = jnp.exp(s - jnp.max(s, axis=-1, keepdims=True))
            ps.append((p / jnp.sum(p, axis=-1, keepdims=True)).astype(BF16))
        probs.append(ps)
    os_ = [jnp.concatenate([_dot(p, mv[:, sl]) for p, sl in zip(ps, sls)], axis=1) for ps in probs]
    outs = [x + _dot(o, wo_ref[...]) for x, o in zip(xs, os_)]
    for out, sub in zip(outs, subs):
        o_ref[sub, :] = out
    for out, sub in zip(outs, subs):
        lg_ref[:, sub] = _dot_nt(wr_ref[...], _rms(out, gm_ref[...])) + br_ref[...]


XA_SUB_ROWS = 256


def _mix_proj_xattn(x, acts, weights, g, wq, mk, mv, wo, g_moe, wt_router, bt_router, B, S, tq=512):
    D = x.shape[-1]
    M = mk.shape[0] // B
    XW = mk.shape[-1]
    n_in = len(acts)
    seq3 = lambda a: a.reshape(B, S, a.shape[-1])
    row_spec = lambda a: pl.BlockSpec((None, tq, a.shape[-1]), lambda b, n: (b, n, 0))
    const = lambda a: pl.BlockSpec(a.shape, lambda b, n: (0,) * a.ndim)
    out, logits = pl.pallas_call(
        functools.partial(_xattn_kernel, n_in=n_in),
        grid=(B, S // tq),
        in_specs=[row_spec(x)] + [row_spec(a) for a in acts] + [const(w) for w in weights]
                 + [pl.BlockSpec((1, D), lambda b, n: (0, 0)),
                    pl.BlockSpec((D, XW), lambda b, n: (0, 0)),
                    pl.BlockSpec((None, M, XW), lambda b, n: (b, 0, 0)),
                    pl.BlockSpec((None, M, XW), lambda b, n: (b, 0, 0)),
                    pl.BlockSpec((XW, D), lambda b, n: (0, 0)),
                    pl.BlockSpec((1, D), lambda b, n: (0, 0)),
                    pl.BlockSpec((LANES, D), lambda b, n: (0, 0)),
                    pl.BlockSpec((LANES, XA_SUB_ROWS), lambda b, n: (0, 0))],
        out_specs=[pl.BlockSpec((None, tq, D), lambda b, n: (b, n, 0)),
                   pl.BlockSpec((None, LANES, tq), lambda b, n: (b * (S // tq) + n, 0, 0))],
        out_shape=[jax.ShapeDtypeStruct((B, S, D), F32),
                   jax.ShapeDtypeStruct((B * S // tq, LANES, tq), F32)],
        compiler_params=_cparams(2),
        name="xattn",
    )(seq3(x), *[seq3(a) for a in acts], *weights, g.reshape(1, D), wq,
      mk.reshape(B, M, XW), mv.reshape(B, M, XW), wo, g_moe.reshape(1, D), wt_router, bt_router)
    return out.reshape(B * S, D), logits


ROUTER_ROWS = 40


def _router_kernel(lg_ref, info_ref, slot_ref, cnt_ref, carry_ref):
    i = pl.program_id(0)

    @pl.when(i == 0)
    def _():
        carry_ref[...] = jnp.zeros_like(carry_ref)

    logits = lg_ref[:ROUTER_ROWS, :]
    tm = logits.shape[1]
    row = lax.broadcasted_iota(I32, logits.shape, 0)
    big = jnp.int32(LANES)
    neg = -jnp.inf
    gl = jnp.where(row < MOE_GROUPS, logits, neg)
    gmax = jnp.max(gl, axis=0, keepdims=True)
    g_top = jnp.min(jnp.where(gl == gmax, row, big), axis=0, keepdims=True)
    p_group = 1.0 / jnp.sum(jnp.exp(gl - gmax), axis=0, keepdims=True)
    lo = MOE_GROUPS + MOE_EXPERTS_PER_GROUP * g_top
    in_group = jnp.where(row >= lo, jnp.where(row < lo + MOE_EXPERTS_PER_GROUP, 1, 0), 0) > 0
    el = jnp.where(in_group, logits, neg)
    emax = jnp.max(el, axis=0, keepdims=True)
    ee = jnp.exp(el - emax)
    prob = ee / jnp.sum(ee, axis=0, keepdims=True)
    prob = jnp.where(in_group, prob, -1.0)
    p1 = jnp.max(prob, axis=0, keepdims=True)
    i1 = jnp.min(jnp.where(prob == p1, row, big), axis=0, keepdims=True)
    rest = jnp.where(row == i1, -1.0, prob)
    p2 = jnp.max(rest, axis=0, keepdims=True)
    i2 = jnp.min(jnp.where(rest == p2, row, big), axis=0, keepdims=True)
    tot = p1 + p2
    g1 = p_group * p1 / tot
    g2 = p_group * p2 / tot
    oh = jnp.concatenate([jnp.where(row == i1, 1.0, 0.0), jnp.where(row == i2, 1.0, 0.0)], axis=0)
    tr = lax.broadcasted_iota(I32, (tm, tm), 0)
    tc = lax.broadcasted_iota(I32, (tm, tm), 1)
    pre = _dot(oh, jnp.where(tr < tc, 1.0, 0.0))
    tots = _dot(oh, jnp.ones((tm, LANES), F32))
    reps = tm // LANES
    carry = carry_ref[...]
    base1 = jnp.concatenate([carry] * reps, axis=1)
    base2 = jnp.concatenate([carry + tots[:ROUTER_ROWS]] * reps, axis=1)
    r1 = jnp.sum(oh[:ROUTER_ROWS] * (base1 + pre[:ROUTER_ROWS]), axis=0, keepdims=True)
    r2 = jnp.sum(oh[ROUTER_ROWS:] * (base2 + pre[ROUTER_ROWS:]), axis=0, keepdims=True)
    carry = carry + tots[:ROUTER_ROWS] + tots[ROUTER_ROWS:]
    carry_ref[...] = carry
    cnt_ref[...] = carry
    e1 = (i1 - MOE_GROUPS).astype(F32)
    e2 = (i2 - MOE_GROUPS).astype(F32)
    slot_rows = [e1, e2, r1, r2, g1, g2]
    rows8 = lax.broadcasted_iota(I32, (SUBLANES, tm), 0)
    slot = jnp.zeros((SUBLANES, tm), F32)
    for j, val in enumerate(slot_rows):
        slot = jnp.where(rows8 == j, val, slot)
    slot_ref[...] = slot
    wide = jnp.concatenate([slot, jnp.zeros((LANES - SUBLANES, tm), F32)], axis=0)
    info_ref[...] = jnp.transpose(wide)


def _router(logits, tm=256):
    n_row_tiles, _, tq = logits.shape
    per = tq // tm
    T = n_row_tiles * tq
    NT = T // tm
    return pl.pallas_call(
        _router_kernel,
        grid=(NT,),
        in_specs=[pl.BlockSpec((None, LANES, tm), lambda i: (i // per, 0, i % per))],
        out_specs=[pl.BlockSpec((tm, LANES), lambda i: (i, 0)),
                   pl.BlockSpec((None, SUBLANES, tm), lambda i: (i, 0, 0)),
                   pl.BlockSpec((ROUTER_ROWS, LANES), lambda i: (0, 0))],
        out_shape=[jax.ShapeDtypeStruct((T, LANES), F32),
                   jax.ShapeDtypeStruct((NT, SUBLANES, tm), F32),
                   jax.ShapeDtypeStruct((ROUTER_ROWS, LANES), F32)],
        scratch_shapes=[pltpu.VMEM((ROUTER_ROWS, LANES), F32)],
        compiler_params=_cparams(1),
        name="router",
    )(logits)


def _row_bytes_wait(hbm, buf, sem):
    pltpu.make_async_copy(buf, hbm.at[pl.ds(0, buf.shape[0]), :], sem).wait()


def _to_row_tiles(ref, val):
    n = val.shape[0]
    for c in range(SUBLANES):
        ref[pl.ds(c, n, stride=SUBLANES), :] = val[:, c * LANES:(c + 1) * LANES]


def _from_row_tiles(ref):
    n = ref.shape[0] // SUBLANES
    return jnp.concatenate([ref[pl.ds(c, n, stride=SUBLANES), :] for c in range(SUBLANES)], axis=1)


def _moe_dispatch_kernel(pends_ref, cnt_ref, dest_ref, x_ref, g_ref, hs_hbm, hbuf, zbuf, sems, zsem, *, td):
    i = pl.program_id(0)
    nt = pl.num_programs(0)
    slot = lax.rem(i, 2)

    @pl.when(i == 0)
    def _():
        zbuf[...] = jnp.zeros_like(zbuf)
        for e in range(MOE_EXPERTS):
            @pl.when(cnt_ref[e] > 0)
            def _():
                start = pl.multiple_of((pends_ref[e] - MOE_BLOCK) * SUBLANES, MOE_BLOCK)
                pltpu.make_async_copy(zbuf, hs_hbm.at[pl.ds(start, MOE_BLOCK * SUBLANES), :], zsem).start()
        for e in range(MOE_EXPERTS):
            @pl.when(cnt_ref[e] > 0)
            def _():
                pltpu.make_async_copy(zbuf, hs_hbm.at[pl.ds(0, MOE_BLOCK * SUBLANES), :], zsem).wait()

        first_unused = pends_ref[MOE_EXPERTS - 1] // MOE_BLOCK
        n_blocks = hs_hbm.shape[0] // (MOE_BLOCK * SUBLANES)

        def zero_start(blk, carry):
            start = pl.multiple_of(blk * (MOE_BLOCK * SUBLANES), MOE_BLOCK)
            pltpu.make_async_copy(zbuf, hs_hbm.at[pl.ds(start, MOE_BLOCK * SUBLANES), :], zsem).start()
            return carry

        def zero_wait(blk, carry):
            pltpu.make_async_copy(zbuf, hs_hbm.at[pl.ds(0, MOE_BLOCK * SUBLANES), :], zsem).wait()
            return carry

        lax.fori_loop(first_unused, n_blocks, zero_start, 0)
        lax.fori_loop(first_unused, n_blocks, zero_wait, 0)

    hb = hbuf.at[slot]
    _to_row_tiles(hb, _rms(x_ref[...], g_ref[...]))
    for j in range(td):
        for c in range(2):
            row = pl.multiple_of(dest_ref[0, c * td + j] * SUBLANES, SUBLANES)
            pltpu.make_async_copy(hb.at[pl.ds(j * SUBLANES, SUBLANES), :],
                                  hs_hbm.at[pl.ds(row, SUBLANES), :],
                                  sems.at[slot]).start(priority=c)

    @pl.when(i > 0)
    def _():
        other = hbuf.at[1 - slot]
        _row_bytes_wait(hs_hbm, other, sems.at[1 - slot])
        _row_bytes_wait(hs_hbm, other, sems.at[1 - slot])

    @pl.when(i == nt - 1)
    def _():
        _row_bytes_wait(hs_hbm, hb, sems.at[slot])
        _row_bytes_wait(hs_hbm, hb, sems.at[slot])


def _moe_dispatch(x, g, pends, counts, dest3, P, td):
    T, D = x.shape
    assert D == ROW_TILE
    grid_spec = pltpu.PrefetchScalarGridSpec(
        num_scalar_prefetch=2,
        grid=(T // td,),
        in_specs=[pl.BlockSpec((None, 1, 2 * td), lambda i, pe, cn: (i, 0, 0), memory_space=pltpu.SMEM),
                  pl.BlockSpec((td, D), lambda i, pe, cn: (i, 0)),
                  pl.BlockSpec((1, D), lambda i, pe, cn: (0, 0))],
        out_specs=pl.BlockSpec(memory_space=pl.ANY),
        scratch_shapes=[pltpu.VMEM((2, td * SUBLANES, LANES), F32),
                        pltpu.VMEM((MOE_BLOCK * SUBLANES, LANES), F32),
                        pltpu.SemaphoreType.DMA((2,)),
                        pltpu.SemaphoreType.DMA(())],
    )
    return pl.pallas_call(
        functools.partial(_moe_dispatch_kernel, td=td),
        grid_spec=grid_spec,
        out_shape=jax.ShapeDtypeStruct((P * SUBLANES, LANES), F32),
        compiler_params=_cparams(1),
        name="moe_dispatch",
    )(pends, counts, dest3, x, g.reshape(1, D))


def _moe_expert_kernel(be_ref, nu_ref, hs_ref, w1_ref, w3_ref, w2_ref, o_ref, w1b, w3b, w2b):
    i = pl.program_id(0)
    used = i < nu_ref[0]
    changed = jnp.logical_or(i == 0, be_ref[i] != be_ref[jnp.maximum(i - 1, 0)])

    @pl.when(jnp.logical_and(used, changed))
    def _():
        w1b[...] = w1_ref[...].astype(BF16)
        w3b[...] = w3_ref[...].astype(BF16)
        w2b[...] = w2_ref[...].astype(BF16)

    @pl.when(used)
    def _():
        xe = _from_row_tiles(hs_ref).astype(BF16)
        ff = w1b.shape[1]
        halves = [slice(0, ff // 2), slice(ff // 2, ff)]
        ups = [(jnp.dot(xe, w1b[:, sl], preferred_element_type=F32),
                jnp.dot(xe, w3b[:, sl], preferred_element_type=F32)) for sl in halves]
        act = [(a * _sigmoid(a) * b).astype(BF16) for a, b in ups]
        y = sum(jnp.dot(a, w2b[sl, :], preferred_element_type=F32) for a, sl in zip(act, halves))
        _to_row_tiles(o_ref, y)

    @pl.when(jnp.logical_not(used))
    def _():
        o_ref[...] = jnp.zeros_like(o_ref)


def _moe_experts(hs, block_e, n_used, w1, w3, w2, layer):
    P = hs.shape[0] // SUBLANES
    D = ROW_TILE
    FF = w1.shape[-1]
    NB = P // MOE_BLOCK
    last = lambda i, nu: jnp.minimum(i, nu[0] - 1)
    grid_spec = pltpu.PrefetchScalarGridSpec(
        num_scalar_prefetch=2,
        grid=(NB,),
        in_specs=[pl.BlockSpec((MOE_BLOCK * SUBLANES, LANES), lambda i, be, nu: (last(i, nu), 0)),
                  pl.BlockSpec((None, None, D, FF), lambda i, be, nu: (layer, be[last(i, nu)], 0, 0)),
                  pl.BlockSpec((None, None, D, FF), lambda i, be, nu: (layer, be[last(i, nu)], 0, 0)),
                  pl.BlockSpec((None, None, FF, D), lambda i, be, nu: (layer, be[last(i, nu)], 0, 0))],
        out_specs=pl.BlockSpec((MOE_BLOCK * SUBLANES, LANES), lambda i, be, nu: (i, 0)),
        scratch_shapes=[pltpu.VMEM((D, FF), BF16),
                        pltpu.VMEM((D, FF), BF16),
                        pltpu.VMEM((FF, D), BF16)],
    )
    return pl.pallas_call(
        _moe_expert_kernel,
        grid_spec=grid_spec,
        out_shape=jax.ShapeDtypeStruct((P * SUBLANES, LANES), F32),
        compiler_params=_cparams(1),
        name="moe_experts",
    )(block_e, n_used, hs, w1, w3, w2)


def _gather_rows(src_hbm, idx_ref, dst_ref, sem, first, last):
    for r in range(first, last):
        row = pl.multiple_of(idx_ref[0, r] * SUBLANES, SUBLANES)
        pltpu.make_async_copy(src_hbm.at[pl.ds(row, SUBLANES), :],
                              dst_ref.at[pl.ds(r * SUBLANES, SUBLANES), :], sem).start(priority=r % 2)


def _moe_combine_kernel(pos_ref, posn_ref, x_ref, info_ref, yb_hbm, g_ref, *rest, tc, final_norm, splits):
    if splits:
        w_ref, o_ref = rest[0], rest[1]
        p_refs = rest[2:2 + len(splits)]
        ybuf, sems = rest[2 + len(splits):]
    else:
        o_ref, ybuf, sems = rest
    i = pl.program_id(0)
    nb = pl.num_programs(0)
    slot = lax.rem(i, 2)

    @pl.when(i == 0)
    def _():
        def issue(r, carry):
            src = pl.multiple_of(pos_ref[0, r] * SUBLANES, SUBLANES)
            dst = pl.multiple_of(r * SUBLANES, SUBLANES)
            pltpu.make_async_copy(yb_hbm.at[pl.ds(src, SUBLANES), :],
                                  ybuf.at[0, pl.ds(dst, SUBLANES), :], sems.at[0]).start()
            return carry
        lax.fori_loop(0, 2 * tc, issue, 0)

    def wait_tile(s_):
        pltpu.make_async_copy(yb_hbm.at[pl.ds(0, 2 * tc * SUBLANES), :], ybuf.at[s_], sems.at[s_]).wait()

    if not splits:
        @pl.when(i + 1 < nb)
        def _():
            _gather_rows(yb_hbm, posn_ref, ybuf.at[1 - slot], sems.at[1 - slot], 0, 2 * tc)

    wait_tile(slot)
    info = info_ref[...]
    yb = ybuf.at[slot]
    y0 = _from_row_tiles(yb.at[pl.ds(0, tc * SUBLANES), :])
    y1 = _from_row_tiles(yb.at[pl.ds(tc * SUBLANES, tc * SUBLANES), :])
    out = x_ref[...] + (y0 * info[:, 4:5] + y1 * info[:, 5:6])
    if final_norm:
        out = _rms(out, g_ref[...])
    o_ref[...] = out
    if splits:
        h = _rms(out, g_ref[...]).astype(BF16)
        chunks = []
        for p_ref, n in zip(p_refs, splits):
            col0 = sum(c[2] for c in chunks)
            for c0 in range(0, n, PROJ_CHUNK):
                chunks.append((p_ref, c0, min(PROJ_CHUNK, n - c0), col0))
        per = -(-2 * tc // len(chunks))
        off = 0
        for ci, (p_ref, c0, width, _) in enumerate(chunks):
            val = jnp.dot(h, w_ref[:, off:off + width], preferred_element_type=F32)
            _gather_rows(yb_hbm, posn_ref, ybuf.at[1 - slot], sems.at[1 - slot],
                         min(ci * per, 2 * tc), min((ci + 1) * per, 2 * tc))
            p_ref[:, c0:c0 + width] = val.astype(p_ref.dtype)
            off += width

        @pl.when(i == nb - 1)
        def _():
            wait_tile(1 - slot)


def _moe_combine(x, info, dest3, yb, g, final_norm, tc, next_proj=None):
    T, D = x.shape
    NT = T // tc
    in_specs = [pl.BlockSpec((None, 1, 2 * tc), lambda i: (i, 0, 0), memory_space=pltpu.SMEM),
                pl.BlockSpec((None, 1, 2 * tc), lambda i: (jnp.minimum(i + 1, NT - 1), 0, 0),
                             memory_space=pltpu.SMEM),
                pl.BlockSpec((tc, D), lambda i: (i, 0)),
                pl.BlockSpec((tc, LANES), lambda i: (i, 0)),
                pl.BlockSpec(memory_space=pl.ANY),
                pl.BlockSpec((1, D), lambda i: (0, 0))]
    out_specs = [pl.BlockSpec((tc, D), lambda i: (i, 0))]
    out_shape = [jax.ShapeDtypeStruct((T, D), F32)]
    args = [dest3, dest3, x, info, yb, g.reshape(1, D)]
    splits = ()
    if next_proj is not None:
        w, splits, out_dtypes = next_proj
        assert not final_norm and sum(splits) == w.shape[1]
        in_specs.append(pl.BlockSpec(w.shape, lambda i: (0, 0)))
        args.append(w)
        out_specs += [pl.BlockSpec((tc, n), lambda i: (i, 0)) for n in splits]
        out_shape += [jax.ShapeDtypeStruct((T, n), dt) for n, dt in zip(splits, out_dtypes)]
    outs = pl.pallas_call(
        functools.partial(_moe_combine_kernel, tc=tc, final_norm=final_norm, splits=tuple(splits)),
        grid=(NT,),
        in_specs=in_specs,
        out_specs=out_specs,
        out_shape=out_shape,
        scratch_shapes=[pltpu.VMEM((2, 2 * tc * SUBLANES, LANES), F32), pltpu.SemaphoreType.DMA((2,))],
        compiler_params=_cparams(1),
        name="moe_combine",
    )(*args)
    return outs[0], tuple(outs[1:])


MOE_TILE = 512
MOE_COMBINE_TILE = 256
MOE_LAST_COMBINE_TILE = 512
PROJ_CHUNK = 256
ROUTER_TILE = 256


def _tile_slots(dest, tile):
    n_tiles = dest.shape[0] * dest.shape[2] // tile
    return jnp.concatenate([dest[:, 0, :].reshape(n_tiles, 1, tile), dest[:, 1, :].reshape(n_tiles, 1, tile)], axis=2)


def _router_params(w_group, b_group, w_expert, b_expert, lanes_out):
    D = w_group.shape[0]
    n_log = MOE_GROUPS + MOE_EXPERTS
    wt = jnp.zeros((LANES, D), F32).at[:MOE_GROUPS].set(w_group.T).at[MOE_GROUPS:n_log].set(w_expert.T)
    bt = jnp.zeros((LANES,), F32).at[:MOE_GROUPS].set(b_group).at[MOE_GROUPS:n_log].set(b_expert)
    return wt.astype(BF16), jnp.broadcast_to(bt[:, None], (LANES, lanes_out))


def _moe_layer(x, g, logits, w1, w3, w2, layer, g_out, final_norm, next_proj):
    T, D = x.shape
    n_log = MOE_GROUPS + MOE_EXPERTS
    info, slot, cnt = _router(logits, ROUTER_TILE)
    P = 2 * T + MOE_EXPERTS * MOE_BLOCK
    NB = P // MOE_BLOCK
    counts = cnt[MOE_GROUPS:n_log, 0].astype(I32)
    padded = (counts + MOE_BLOCK - 1) // MOE_BLOCK * MOE_BLOCK
    pends = jnp.cumsum(padded).astype(I32)
    pstarts = pends - padded
    block_start = jnp.arange(NB, dtype=I32) * MOE_BLOCK
    block_e = jnp.minimum(jnp.sum((pends[None, :] <= block_start[:, None]).astype(I32), axis=1),
                          MOE_EXPERTS - 1).astype(I32)
    n_used = (pends[-1:] // MOE_BLOCK).astype(I32)
    eid = slot[:, 0:2, :].astype(I32)
    expert_ids = jnp.arange(MOE_EXPERTS, dtype=I32)
    seg_start = jnp.sum(jnp.where(eid[..., None] == expert_ids, pstarts, 0), axis=-1)
    dest = seg_start + slot[:, 2:4, :].astype(I32)
    hs = _moe_dispatch(x, g, pends, counts, _tile_slots(dest, MOE_TILE), P, MOE_TILE)
    yb = _moe_experts(hs, block_e, n_used, w1, w3, w2, layer)
    tc = MOE_LAST_COMBINE_TILE if next_proj is None else MOE_COMBINE_TILE
    return _moe_combine(x, info, _tile_slots(dest, tc), yb, g_out, final_norm, tc, next_proj)


def kernel(x, mem, norm_mix, norm_xattn, norm_moe, norm_final, ev_w_in, ev_sinks, ev_mu, ev_w0, ev_w2, ev_a0, ev_a2, ev_g2, ev_k_k, ev_k_a, ev_r_k, ev_lnx_w, ev_lnx_b, ev_w_out, od_w_in, od_gate_up, od_gate_b, od_onorm, od_w_out, mem_norm, mem_wk, mem_wv, xa_wq, xa_wo, moe_w_group, moe_b_group, moe_w_expert, moe_b_expert, moe_w1, moe_w3, moe_w2):
    B, S, D = x.shape
    M = mem.shape[1]
    T = B * S
    depth = norm_mix.shape[0]
    xf = x.reshape(T, D)

    XW = XA_HEADS * XA_HEAD_DIM
    w_kv = jnp.concatenate([mem_wk, mem_wv], axis=1).astype(BF16)
    mk, mv = _norm_matmul(mem.reshape(B * M, D), mem_norm, w_kv, (XW, XW), (BF16, BF16))

    KW = GLA_HEADS * GLA_DK
    VW = GLA_HEADS * GLA_DV
    swa_cols = SWA_Q_HEADS * HEAD_DIM + 2 * (SWA_Q_HEADS // SWA_GROUP) * HEAD_DIM

    def in_proj(layer):
        i = layer // 2
        if layer % 2 == 0:
            return ev_w_in[i].astype(BF16), (swa_cols, ev_w_in.shape[-1] - swa_cols), (F32, F32)
        R = od_gate_up.shape[1]
        w = od_w_in[i]
        w_re = jnp.concatenate([w[:, :2 * KW + VW], w[:, 2 * KW + VW + R:],
                                w[:, 2 * KW + VW:2 * KW + VW + R],
                                jnp.zeros((D, LANES - R), F32)], axis=1).astype(BF16)
        return w_re, (2 * KW + 2 * VW, LANES), (F32, F32)

    w_first, _, _ = in_proj(0)
    qkv0, rw0 = _in_proj_rwkv(xf, norm_mix[0], w_first, swa_cols, ev_mu[0], ev_w0[0], ev_w2[0], ev_a0[0], ev_a2[0],
                              ev_g2[0], ev_k_k[0], ev_k_a[0], S)
    proj = None
    for layer in range(depth):
        i = layer // 2
        if layer % 2 == 0:
            if layer == 0:
                qkv, rw = qkv0, [t.reshape(B, S, RWKV_WIDTH) for t in rw0]
            else:
                qkv, p_rw = proj
                rw = _rwkv_prep(p_rw, ev_mu[i], ev_w0[i], ev_w2[i], ev_a0[i], ev_a2[i],
                                ev_g2[i], ev_k_k[i], ev_k_a[i], B, S)
            o_a = _swa(qkv, ev_sinks[i], B, S)
            o_b = _rwkv_scan(*rw, ev_r_k[i].reshape(-1), ev_lnx_w[i], ev_lnx_b[i])
            w_out = ev_w_out[i].astype(BF16)
            qw = o_a.shape[-1]
            mix_acts, mix_ws = [o_a, o_b], [w_out[:qw], w_out[qw:]]
        else:
            qkvo, gd = proj
            R = od_gate_up.shape[1]
            gup = jnp.zeros((LANES, KW), F32).at[:R].set(od_gate_up[i]).astype(BF16)
            o = _gla(qkvo, gd, gup, od_gate_b[i], od_onorm[i], B, S)
            mix_acts, mix_ws = [o], [od_w_out[i].astype(BF16)]
        wt_router, bt_router = _router_params(moe_w_group[layer], moe_b_group[layer], moe_w_expert[layer],
                                              moe_b_expert[layer], XA_SUB_ROWS)
        xf, logits = _mix_proj_xattn(xf, mix_acts, mix_ws, norm_xattn[layer], xa_wq[layer].astype(BF16), mk, mv,
                                     xa_wo[layer].astype(BF16), norm_moe[layer], wt_router, bt_router, B, S)
        last = layer == depth - 1
        g_out = norm_final if last else norm_mix[layer + 1]
        xf, proj = _moe_layer(xf, norm_moe[layer], logits, moe_w1, moe_w3, moe_w2, layer,
                              g_out, last, None if last else in_proj(layer + 1))
    return xf.reshape(B, S, D)
```

```python
import functools

import jax
import jax.numpy as jnp
from jax import lax
from jax.experimental import pallas as pl
from jax.experimental.pallas import tpu as pltpu

F32 = jnp.float32
BF16 = jnp.bfloat16
I32 = jnp.int32

EPS = 1e-6
HEAD_DIM = 64
SWA_WINDOW = 128
SWA_Q_HEADS = 8
SWA_GROUP = 4
RWKV_HEADS = 8
RWKV_WIDTH = 512
RWKV_LN_EPS = 64e-5
RWKV_CHUNK = 64
GLA_HEADS = 4
GLA_DK = 128
GLA_DV = 256
GLA_CHUNK = 64
GLA_GATE_NORM = 16.0
XA_HEADS = 4
XA_HEAD_DIM = 128
MOE_GROUPS = 4
MOE_EXPERTS_PER_GROUP = 8
MOE_EXPERTS = 32
MOE_BLOCK = 512
LANES = 128
SUBLANES = 8
ROW_TILE = SUBLANES * LANES

VMEM_LIMIT_BYTES = 48 * 1024 * 1024


def _cparams(n_axes):
    return pltpu.CompilerParams(dimension_semantics=("arbitrary",) * n_axes,
                                vmem_limit_bytes=VMEM_LIMIT_BYTES)


def _dot(a, b):
    return jnp.dot(a.astype(BF16), b.astype(BF16), preferred_element_type=F32)


def _dot_nt(a, b):
    return lax.dot_general(a.astype(BF16), b.astype(BF16), (((1,), (1,)), ((), ())),
                           preferred_element_type=F32)


def _dot_tn(a, b):
    return lax.dot_general(a.astype(BF16), b.astype(BF16), (((0,), (0,)), ((), ())),
                           preferred_element_type=F32)


def _dot_f32(a, b):
    return jnp.dot(a, b, preferred_element_type=F32, precision=lax.Precision.HIGHEST)


def _rms(x, g):
    ms = jnp.mean(x * x, axis=-1, keepdims=True)
    return x * lax.rsqrt(ms + EPS) * g


def _sigmoid(x):
    return 1.0 / (1.0 + jnp.exp(-x))


def _softplus(x):
    return jnp.maximum(x, 0.0) + jnp.log(1.0 + jnp.exp(-jnp.abs(x)))


def _norm_matmul_kernel(x_ref, g_ref, w_ref, *o_refs, splits):
    h = _rms(x_ref[...], g_ref[...]).astype(BF16)
    off = 0
    for o_ref, n in zip(o_refs, splits):
        o_ref[...] = jnp.dot(h, w_ref[:, off:off + n], preferred_element_type=F32).astype(o_ref.dtype)
        off += n


def _norm_matmul(x, g, w, splits, out_dtypes, tm=256):
    T, D = x.shape
    N = w.shape[1]
    assert sum(splits) == N and T % tm == 0
    return pl.pallas_call(
        functools.partial(_norm_matmul_kernel, splits=tuple(splits)),
        grid=(T // tm,),
        in_specs=[pl.BlockSpec((tm, D), lambda i: (i, 0)),
                  pl.BlockSpec((1, D), lambda i: (0, 0)),
                  pl.BlockSpec((D, N), lambda i: (0, 0))],
        out_specs=[pl.BlockSpec((tm, n), lambda i: (i, 0)) for n in splits],
        out_shape=[jax.ShapeDtypeStruct((T, n), dt) for n, dt in zip(splits, out_dtypes)],
        compiler_params=_cparams(1),
        name="norm_matmul",
    )(x, g.reshape(1, D), w)


def _swa_kernel(sinks_ref, q_ref, kp_ref, kc_ref, vp_ref, vc_ref, o_ref):
    n = pl.program_id(1)
    W = SWA_WINDOW
    NB = q_ref.shape[0]
    qpos = lax.broadcasted_iota(I32, (W, 2 * W), 0) + W
    kpos = lax.broadcasted_iota(I32, (W, 2 * W), 1)
    rel = qpos - kpos
    in_window = jnp.where(rel >= 0, jnp.where(rel < W, 1, 0), 0)
    has_prev = jnp.where(n > 0, 1, 0)
    valid = (in_window * jnp.where(kpos >= W, 1, has_prev)) > 0
    n_groups = SWA_Q_HEADS // SWA_GROUP
    streams = [(bi, g) for bi in range(NB) for g in range(n_groups)]
    qb = [q_ref[bi].astype(BF16) for bi in range(NB)]
    kb = [jnp.concatenate([kp_ref[bi], kc_ref[bi]], axis=0).astype(BF16) for bi in range(NB)]
    vb = [jnp.concatenate([vp_ref[bi], vc_ref[bi]], axis=0).astype(BF16) for bi in range(NB)]
    gs = lambda g: slice(g * HEAD_DIM, (g + 1) * HEAD_DIM)
    scores = []
    for bi, g in streams:
        qg = jnp.concatenate([qb[bi][:, h * HEAD_DIM:(h + 1) * HEAD_DIM]
                              for h in range(g * SWA_GROUP, (g + 1) * SWA_GROUP)], axis=0)
        scores.append(_dot_nt(qg, kb[bi][:, gs(g)]))
    probs = []
    for i, (bi, g) in enumerate(streams):
        pieces = []
        for j in range(SWA_GROUP):
            s = jnp.where(valid, scores[i][j * W:(j + 1) * W] * (HEAD_DIM ** -0.5), -jnp.inf)
            sink = sinks_ref[g * SWA_GROUP + j]
            m = jnp.maximum(jnp.max(s, axis=-1, keepdims=True), sink)
            p = jnp.exp(s - m)
            den = jnp.sum(p, axis=-1, keepdims=True) + jnp.exp(sink - m)
            pieces.append((p / den).astype(BF16))
        probs.append(jnp.concatenate(pieces, axis=0))
    ogs = [_dot(probs[i], vb[bi][:, gs(g)]) for i, (bi, g) in enumerate(streams)]
    for bi in range(NB):
        outs = []
        for g in range(n_groups):
            og = ogs[bi * n_groups + g]
            outs += [og[j * W:(j + 1) * W] for j in range(SWA_GROUP)]
        o_ref[bi] = jnp.concatenate(outs, axis=1).astype(o_ref.dtype)


SWA_BATCH_ROWS = 2


def _swa(qkv, sinks, B, S):
    W = SWA_WINDOW
    qkv3 = qkv.reshape(B, S, qkv.shape[-1])
    qw = SWA_Q_HEADS * HEAD_DIM
    kw = qw // SWA_GROUP
    kcol = qw // kw
    nb = SWA_BATCH_ROWS if B % SWA_BATCH_ROWS == 0 else 1
    out = pl.pallas_call(
        _swa_kernel,
        grid=(B // nb, S // W),
        in_specs=[pl.BlockSpec(memory_space=pltpu.SMEM),
                  pl.BlockSpec((nb, W, qw), lambda b, n: (b, n, 0)),
                  pl.BlockSpec((nb, W, kw), lambda b, n: (b, jnp.maximum(n - 1, 0), kcol)),
                  pl.BlockSpec((nb, W, kw), lambda b, n: (b, n, kcol)),
                  pl.BlockSpec((nb, W, kw), lambda b, n: (b, jnp.maximum(n - 1, 0), kcol + 1)),
                  pl.BlockSpec((nb, W, kw), lambda b, n: (b, n, kcol + 1))],
        out_specs=pl.BlockSpec((nb, W, qw), lambda b, n: (b, n, 0)),
        out_shape=jax.ShapeDtypeStruct((B, S, qw), BF16),
        compiler_params=_cparams(2),
        name="swa",
    )(sinks, qkv3, qkv3, qkv3, qkv3, qkv3)
    return out.reshape(B * S, qw)


def _rwkv_prep_math(p, last, mu_ref, w0_ref, w2_ref, a0_ref, a2_ref, g2_ref, kk_ref, ka_ref, outs):
    r_out, lw_out, k_out, v_out, a_out, b_out, g_out = outs
    C = RWKV_WIDTH
    row = lax.broadcasted_iota(I32, p.shape, 0)
    p_prev = jnp.where(row == 0, last, pltpu.roll(p, 1, axis=0))
    p = p + (p_prev - p) * mu_ref[...]
    r = p[:, :C]
    k = p[:, C:2 * C]
    v = p[:, 2 * C:3 * C]
    xw = p[:, 3 * C:3 * C + 64]
    xa = p[:, 3 * C + 64:3 * C + 128]
    xg = p[:, 3 * C + 128:]
    w = -_softplus(-(w0_ref[...] + _dot(jnp.tanh(xw), w2_ref[...]))) - 0.5
    lw = -jnp.exp(w)
    a = _sigmoid(a0_ref[...] + _dot(xa, a2_ref[...]))
    g = _dot(_sigmoid(xg), g2_ref[...])
    kk = k * kk_ref[...]
    pieces = []
    for h in range(RWKV_HEADS):
        kh = kk[:, h * HEAD_DIM:(h + 1) * HEAD_DIM]
        nrm = jnp.sqrt(jnp.sum(kh * kh, axis=-1, keepdims=True))
        pieces.append(kh / jnp.maximum(nrm, 1e-12))
    kk = jnp.concatenate(pieces, axis=1)
    r_out[...] = r
    lw_out[...] = lw
    k_out[...] = k * (1.0 + (a - 1.0) * ka_ref[...])
    v_out[...] = v
    a_out[...] = -kk
    b_out[...] = kk * a
    g_out[...] = g


def _rwkv_prep_kernel(p_ref, pprev_ref, *refs):
    n = pl.program_id(1)
    last = jnp.where(n > 0, pprev_ref[7:8, :], 0.0)
    _rwkv_prep_math(p_ref[...], last, *refs[:8], refs[8:])


def _rwkv_params(mu, w0, w2, a0, a2, g2, k_k, k_a):
    row = lambda t: t.reshape(1, -1)
    return [row(mu), row(w0), w2.astype(BF16), row(a0), a2.astype(BF16), g2.astype(BF16), row(k_k), row(k_a)]


def _rwkv_prep(p, mu, w0, w2, a0, a2, g2, k_k, k_a, B, S, tt=256):
    C = RWKV_WIDTH
    PW = p.shape[-1]
    p3 = p.reshape(B, S, PW)
    full = lambda arr: pl.BlockSpec(arr.shape, lambda b, n: (0,) * arr.ndim)
    params = _rwkv_params(mu, w0, w2, a0, a2, g2, k_k, k_a)
    outs = pl.pallas_call(
        _rwkv_prep_kernel,
        grid=(B, S // tt),
        in_specs=[pl.BlockSpec((None, tt, PW), lambda b, n: (b, n, 0)),
                  pl.BlockSpec((None, 8, PW), lambda b, n: (b, jnp.maximum(n * (tt // 8) - 1, 0), 0))]
                 + [full(t) for t in params],
        out_specs=[pl.BlockSpec((None, tt, C), lambda b, n: (b, n, 0))] * 7,
        out_shape=[jax.ShapeDtypeStruct((B, S, C), F32)] * 7,
        compiler_params=_cparams(2),
        name="rwkv_prep",
    )(p3, p3, *params)
    return outs


def _in_proj_rwkv_kernel(x_ref, g_ref, w_ref, *refs, swa_cols, tiles_per_seq):
    params = refs[:8]
    qkv_out = refs[8]
    outs = refs[9:16]
    p_buf, last_buf = refs[16:]
    i = pl.program_id(0)

    @pl.when(i == 0)
    def _():
        p_buf[...] = jnp.zeros_like(p_buf)
        last_buf[...] = jnp.zeros_like(last_buf)

    j = i - 1
    p_prev_tile = p_buf[lax.rem(i + 1, 2)]
    tm = p_prev_tile.shape[0]
    last = jnp.where(lax.rem(j, tiles_per_seq) == 0, 0.0, last_buf[...])
    _rwkv_prep_math(p_prev_tile, last, *params, outs)
    last_buf[...] = p_prev_tile[tm - 1:tm, :]
    h = _rms(x_ref[...], g_ref[...]).astype(BF16)
    qkv_out[...] = jnp.dot(h, w_ref[:, :swa_cols], preferred_element_type=F32)
    p_buf[lax.rem(i, 2)] = jnp.dot(h, w_ref[:, swa_cols:], preferred_element_type=F32)


def _in_proj_rwkv(x, g, w, swa_cols, mu, w0, w2, a0, a2, g2, k_k, k_a, S, tm=256):
    T, D = x.shape
    N = w.shape[1]
    C = RWKV_WIDTH
    NT = T // tm
    params = _rwkv_params(mu, w0, w2, a0, a2, g2, k_k, k_a)
    const = lambda arr: pl.BlockSpec(arr.shape, lambda i: (0,) * arr.ndim)
    cur = lambda i: (jnp.minimum(i, NT - 1), 0)
    prev = lambda i: (jnp.maximum(i - 1, 0), 0)
    outs = pl.pallas_call(
        functools.partial(_in_proj_rwkv_kernel, swa_cols=swa_cols, tiles_per_seq=S // tm),
        grid=(NT + 1,),
        in_specs=[pl.BlockSpec((tm, D), cur), pl.BlockSpec((1, D), lambda i: (0, 0)), const(w)]
                 + [const(t) for t in params],
        out_specs=[pl.BlockSpec((tm, swa_cols), cur)] + [pl.BlockSpec((tm, C), prev)] * 7,
        out_shape=[jax.ShapeDtypeStruct((T, swa_cols), F32)] + [jax.ShapeDtypeStruct((T, C), F32)] * 7,
        scratch_shapes=[pltpu.VMEM((2, tm, N - swa_cols), F32), pltpu.VMEM((1, N - swa_cols), F32)],
        compiler_params=_cparams(1),
        name="in_proj_rwkv",
    )(x, g.reshape(1, D), w, *params)
    return outs[0], outs[1:]


def _pair_blockdiag(x):
    lane = lax.broadcasted_iota(I32, x.shape, 1)
    zero = jnp.zeros_like(x)
    return jnp.concatenate([jnp.where(lane < HEAD_DIM, x, zero), jnp.where(lane >= HEAD_DIM, x, zero)], axis=0)


def _rwkv_scan_kernel(r_ref, lw_ref, k_ref, v_ref, a_ref, b_ref, g_ref, rk_ref, lnw_ref, lnb_ref,
                      o_ref, s_ref):
    c = pl.program_id(1)

    @pl.when(c == 0)
    def _():
        s_ref[...] = jnp.zeros_like(s_ref)

    C = RWKV_CHUNK
    NB = r_ref.shape[0]
    NP = RWKV_HEADS // 2
    PW = 2 * HEAD_DIM
    row = lax.broadcasted_iota(I32, (C, C), 0)
    col = lax.broadcasted_iota(I32, (C, C), 1)
    tri = jnp.where(row >= col, 1.0, 0.0).astype(F32)
    rowp = lax.broadcasted_iota(I32, (C, PW), 0)
    colp = lax.broadcasted_iota(I32, (C, PW), 1)
    colp = jnp.where(colp >= HEAD_DIM, colp - HEAD_DIM, colp)
    lower_p = rowp >= colp
    strict_p = rowp > colp
    rows = lax.broadcasted_iota(I32, (PW, PW), 0)
    cols = lax.broadcasted_iota(I32, (PW, PW), 1)
    same_head = jnp.where(rows >= HEAD_DIM, 1, 0) == jnp.where(cols >= HEAD_DIM, 1, 0)
    first = lax.broadcasted_iota(I32, (C, PW), 1) < HEAD_DIM

    streams = [(bi, p) for bi in range(NB) for p in range(NP)]
    pre = []
    for bi in range(NB):
        lw = lw_ref[bi]
        cum = _dot_f32(tri, lw)
        cum_last = cum[C - 1:C, :]
        r = r_ref[bi]
        k = k_ref[bi]
        v = v_ref[bi]
        a = a_ref[bi]
        b = b_ref[bi]
        e_neg = jnp.exp(-cum)
        e_rem = jnp.exp(cum_last - cum)
        pre.append(dict(
            r_t=(r * jnp.exp(cum)).astype(BF16), a_t=(a * jnp.exp(cum - lw)).astype(BF16),
            b_t=(b * e_neg).astype(BF16), k_t=(k * e_neg).astype(BF16),
            b_d=(b * e_rem).astype(BF16), k_d=(k * e_rem).astype(BF16),
            v_b=v.astype(BF16), v=v, e_last=jnp.exp(cum_last), rkk=r * k * rk_ref[...], g=g_ref[bi]))

    def lanes(p):
        return slice(p * PW, (p + 1) * PW)

    ar = [jnp.concatenate([pre[bi]['a_t'][:, lanes(p)], pre[bi]['r_t'][:, lanes(p)]], axis=0) for bi, p in streams]
    s0 = [s_ref[bi, p] for bi, p in streams]
    big = [_dot_nt(ar[i], jnp.concatenate([_pair_blockdiag(pre[bi]['b_t'][:, lanes(p)]),
                                           _pair_blockdiag(pre[bi]['k_t'][:, lanes(p)]),
                                           s0[i].astype(BF16)], axis=0))
           for i, (bi, p) in enumerate(streams)]
    m_b = [t[:, :PW] for t in big]
    m_k = [t[:, PW:2 * PW] for t in big]
    ars = [t[:, 2 * PW:] for t in big]
    v_p = [pre[bi]['v_b'][:, lanes(p)] for bi, p in streams]
    v_bd = [_pair_blockdiag(vp) for vp in v_p]
    x = [ars[i][:C] + _dot(jnp.where(strict_p, m_k[i][:C], 0.0), v_bd[i]) for i in range(len(streams))]
    pw = [jnp.where(strict_p, m_b[i][:C], 0.0).astype(BF16) for i in range(len(streams))]
    n_stages = 6
    for stage in range(n_stages):
        if stage < n_stages - 1:
            prod = [_dot(pw[i], jnp.concatenate([_pair_blockdiag(x[i].astype(BF16)), _pair_blockdiag(pw[i])], axis=1))
                    for i in range(len(streams))]
            x = [x[i] + prod[i][:, :PW] for i in range(len(streams))]
            pw = [prod[i][:, PW:].astype(BF16) for i in range(len(streams))]
        else:
            x = [x[i] + _dot(pw[i], _pair_blockdiag(x[i].astype(BF16))) for i in range(len(streams))]
    u_b = [xi.astype(BF16) for xi in x]
    y = [ars[i][C:]
         + _dot(jnp.concatenate([jnp.where(lower_p, m_b[i][C:], 0.0), jnp.where(lower_p, m_k[i][C:], 0.0)], axis=1),
                jnp.concatenate([_pair_blockdiag(u_b[i]), v_bd[i]], axis=0))
         for i in range(len(streams))]
    for i, (bi, p) in enumerate(streams):
        upd = _dot_tn(jnp.concatenate([u_b[i], v_p[i]], axis=0),
                      jnp.concatenate([pre[bi]['b_d'][:, lanes(p)], pre[bi]['k_d'][:, lanes(p)]], axis=0))
        s_ref[bi, p] = s0[i] * pre[bi]['e_last'][:, lanes(p)] + jnp.where(same_head, upd, 0.0)

    lnw = lnw_ref[...]
    lnb = lnb_ref[...]

    def head_sum(t):
        s1 = jnp.sum(jnp.where(first, t, 0.0), axis=-1, keepdims=True)
        s2 = jnp.sum(jnp.where(first, 0.0, t), axis=-1, keepdims=True)
        return jnp.where(first, s1, s2)

    for bi in range(NB):
        outs = []
        for p in range(NP):
            yi = y[bi * NP + p]
            mean = head_sum(yi) * (1.0 / HEAD_DIM)
            yc = yi - mean
            var = head_sum(yc * yc) * (1.0 / HEAD_DIM)
            yn = yc * lax.rsqrt(var + RWKV_LN_EPS) * lnw[:, lanes(p)] + lnb[:, lanes(p)]
            bonus = head_sum(pre[bi]['rkk'][:, lanes(p)]) * pre[bi]['v'][:, lanes(p)]
            outs.append((yn + bonus) * pre[bi]['g'][:, lanes(p)])
        o_ref[bi] = jnp.concatenate(outs, axis=1).astype(o_ref.dtype)


RWKV_BATCH_ROWS = 4


def _rwkv_scan(r, lw, k, v, a, b, g, r_k, lnx_w, lnx_b):
    B, S, W = r.shape
    C = RWKV_CHUNK
    nb = RWKV_BATCH_ROWS if B % RWKV_BATCH_ROWS == 0 else 1
    seq = pl.BlockSpec((nb, C, W), lambda bb, c: (bb, c, 0))
    par = pl.BlockSpec((1, W), lambda bb, c: (0, 0))
    out = pl.pallas_call(
        _rwkv_scan_kernel,
        grid=(B // nb, S // C),
        in_specs=[seq] * 7 + [par] * 3,
        out_specs=seq,
        out_shape=jax.ShapeDtypeStruct((B, S, W), BF16),
        scratch_shapes=[pltpu.VMEM((nb, RWKV_HEADS // 2, 2 * HEAD_DIM, 2 * HEAD_DIM), F32)],
        compiler_params=_cparams(2),
        name="rwkv_scan",
    )(r, lw, k, v, a, b, g, r_k.reshape(1, W), lnx_w.reshape(1, W), lnx_b.reshape(1, W))
    return out.reshape(B * S, W)


def _gla_kernel(q_ref, k_ref, v_ref, og_ref, gd_ref, gup_ref, gb_ref, on_ref, o_ref, s_ref):
    c = pl.program_id(1)

    @pl.when(c == 0)
    def _():
        s_ref[...] = jnp.zeros_like(s_ref)

    C = GLA_CHUNK
    NB = q_ref.shape[0]
    row = lax.broadcasted_iota(I32, (C, C), 0)
    col = lax.broadcasted_iota(I32, (C, C), 1)
    lower = row >= col
    tri = jnp.where(lower, 1.0, 0.0).astype(F32)
    onorm = on_ref[...]
    zs = [_dot(gd_ref[bi], gup_ref[...]) + gb_ref[...] for bi in range(NB)]
    cums = [_dot_f32(tri, -_softplus(-z) / GLA_GATE_NORM) for z in zs]
    qe, ke, kd, e_last, v = [], [], [], [], []
    for bi in range(NB):
        cum = cums[bi]
        cum_last = cum[C - 1:C, :]
        k = k_ref[bi]
        qe.append((q_ref[bi] * (GLA_DK ** -0.5) * jnp.exp(cum)).astype(BF16))
        ke.append((k * jnp.exp(-cum)).astype(BF16))
        kd.append((k * jnp.exp(cum_last - cum)).astype(BF16))
        e_last.append(jnp.exp(cum_last))
        v.append(v_ref[bi].astype(BF16))
    streams = [(bi, h) for bi in range(NB) for h in range(GLA_HEADS)]
    ks = lambda h: slice(h * GLA_DK, (h + 1) * GLA_DK)
    vs = lambda h: slice(h * GLA_DV, (h + 1) * GLA_DV)
    sts = [s_ref[bi, h] for bi, h in streams]
    atts = [jnp.where(lower, _dot_nt(qe[bi][:, ks(h)], ke[bi][:, ks(h)]), 0.0) for bi, h in streams]
    inters = [_dot_nt(qe[bi][:, ks(h)], sts[i]) for i, (bi, h) in enumerate(streams)]
    os_ = [inters[i] + _dot(atts[i], v[bi][:, vs(h)]) for i, (bi, h) in enumerate(streams)]
    for i, (bi, h) in enumerate(streams):
        s_ref[bi, h] = sts[i] * e_last[bi][:, ks(h)] + _dot_tn(v[bi][:, vs(h)], kd[bi][:, ks(h)])
    for bi in range(NB):
        og = og_ref[bi]
        outs = []
        for h in range(GLA_HEADS):
            gate = og[:, vs(h)]
            outs.append(_rms(os_[bi * GLA_HEADS + h], onorm) * (gate * _sigmoid(gate)))
        o_ref[bi] = jnp.concatenate(outs, axis=1).astype(o_ref.dtype)


GLA_BATCH_ROWS = 4


def _gla(qkvo, gd, gate_up_pad, gate_b, onorm, B, S):
    C = GLA_CHUNK
    KW = GLA_HEADS * GLA_DK
    VW = GLA_HEADS * GLA_DV
    x3 = qkvo.reshape(B, S, qkvo.shape[-1])
    gd3 = gd.reshape(B, S, LANES)
    nb = GLA_BATCH_ROWS if B % GLA_BATCH_ROWS == 0 else 1
    out = pl.pallas_call(
        _gla_kernel,
        grid=(B // nb, S // C),
        in_specs=[pl.BlockSpec((nb, C, KW), lambda b, c: (b, c, 0)),
                  pl.BlockSpec((nb, C, KW), lambda b, c: (b, c, 1)),
                  pl.BlockSpec((nb, C, VW), lambda b, c: (b, c, 1)),
                  pl.BlockSpec((nb, C, VW), lambda b, c: (b, c, 2)),
                  pl.BlockSpec((nb, C, LANES), lambda b, c: (b, c, 0)),
                  pl.BlockSpec((LANES, KW), lambda b, c: (0, 0)),
                  pl.BlockSpec((1, KW), lambda b, c: (0, 0)),
                  pl.BlockSpec((1, GLA_DV), lambda b, c: (0, 0))],
        out_specs=pl.BlockSpec((nb, C, VW), lambda b, c: (b, c, 0)),
        out_shape=jax.ShapeDtypeStruct((B, S, VW), BF16),
        scratch_shapes=[pltpu.VMEM((nb, GLA_HEADS, GLA_DV, GLA_DK), F32)],
        compiler_params=_cparams(2),
        name="gla",
    )(x3, x3, x3, x3, gd3, gate_up_pad, gate_b.reshape(1, KW), onorm.reshape(1, GLA_DV))
    return out.reshape(B * S, VW)


def _xattn_kernel(*refs, n_in):
    x_ref = refs[0]
    a_refs = refs[1:1 + n_in]
    w_refs = refs[1 + n_in:1 + 2 * n_in]
    g_ref, wq_ref, mk_ref, mv_ref, wo_ref, gm_ref, wr_ref, br_ref, o_ref, lg_ref = refs[1 + 2 * n_in:]
    tq = x_ref.shape[0]
    subs = [slice(r, r + XA_SUB_ROWS) for r in range(0, tq, XA_SUB_ROWS)]
    xs = [x_ref[sub, :] for sub in subs]
    for a_ref, w_ref in zip(a_refs, w_refs):
        xs = [x + jnp.dot(a_ref[sub, :], w_ref[...], preferred_element_type=F32) for x, sub in zip(xs, subs)]
    qs = [_dot(_rms(x, g_ref[...]), wq_ref[...]).astype(BF16) for x in xs]
    mk = mk_ref[...]
    mv = mv_ref[...]
    sls = [slice(hd * XA_HEAD_DIM, (hd + 1) * XA_HEAD_DIM) for hd in range(XA_HEADS)]
    scores = [[_dot_nt(q[:, sl], mk[:, sl]) for sl in sls] for q in qs]
    probs = []
    for sc in scores:
        ps = []
        for s in sc:
            s = s * (XA_HEAD_DIM ** -0.5)
            p = jnp.exp(s - jnp.max(s, axis=-1, keepdims=True))
            ps.append((p / jnp.sum(p, axis=-1, keepdims=True)).astype(BF16))
        probs.append(ps)
    os_ = [jnp.concatenate([_dot(p, mv[:, sl]) for p, sl in zip(ps, sls)], axis=1) for ps in probs]
    outs = [x + _dot(o, wo_ref[...]) for x, o in zip(xs, os_)]
    for out, sub in zip(outs, subs):
        o_ref[sub, :] = out
    for out, sub in zip(outs, subs):
        lg_ref[:, sub] = _dot_nt(wr_ref[...], _rms(out, gm_ref[...])) + br_ref[...]


XA_SUB_ROWS = 256


def _mix_proj_xattn(x, acts, weights, g, wq, mk, mv, wo, g_moe, wt_router, bt_router, B, S, tq=1024):
    D = x.shape[-1]
    M = mk.shape[0] // B
    XW = mk.shape[-1]
    n_in = len(acts)
    seq3 = lambda a: a.reshape(B, S, a.shape[-1])
    row_spec = lambda a: pl.BlockSpec((None, tq, a.shape[-1]), lambda b, n: (b, n, 0))
    const = lambda a: pl.BlockSpec(a.shape, lambda b, n: (0,) * a.ndim)
    out, logits = pl.pallas_call(
        functools.partial(_xattn_kernel, n_in=n_in),
        grid=(B, S // tq),
        in_specs=[row_spec(x)] + [row_spec(a) for a in acts] + [const(w) for w in weights]
                 + [pl.BlockSpec((1, D), lambda b, n: (0, 0)),
                    pl.BlockSpec((D, XW), lambda b, n: (0, 0)),
                    pl.BlockSpec((None, M, XW), lambda b, n: (b, 0, 0)),
                    pl.BlockSpec((None, M, XW), lambda b, n: (b, 0, 0)),
                    pl.BlockSpec((XW, D), lambda b, n: (0, 0)),
                    pl.BlockSpec((1, D), lambda b, n: (0, 0)),
                    pl.BlockSpec((LANES, D), lambda b, n: (0, 0)),
                    pl.BlockSpec((LANES, XA_SUB_ROWS), lambda b, n: (0, 0))],
        out_specs=[pl.BlockSpec((None, tq, D), lambda b, n: (b, n, 0)),
                   pl.BlockSpec((None, LANES, tq), lambda b, n: (b * (S // tq) + n, 0, 0))],
        out_shape=[jax.ShapeDtypeStruct((B, S, D), F32),
                   jax.ShapeDtypeStruct((B * S // tq, LANES, tq), F32)],
        compiler_params=_cparams(2),
        name="xattn",
    )(seq3(x), *[seq3(a) for a in acts], *weights, g.reshape(1, D), wq,
      mk.reshape(B, M, XW), mv.reshape(B, M, XW), wo, g_moe.reshape(1, D), wt_router, bt_router)
    return out.reshape(B * S, D), logits


ROUTER_ROWS = 40


def _router_kernel(lg_ref, info_ref, slot_ref, cnt_ref, carry_ref):
    i = pl.program_id(0)

    @pl.when(i == 0)
    def _():
        carry_ref[...] = jnp.zeros_like(carry_ref)

    logits = lg_ref[:ROUTER_ROWS, :]
    tm = logits.shape[1]
    row = lax.broadcasted_iota(I32, logits.shape, 0)
    big = jnp.int32(LANES)
    neg = -jnp.inf
    gl = jnp.where(row < MOE_GROUPS, logits, neg)
    gmax = jnp.max(gl, axis=0, keepdims=True)
    g_top = jnp.min(jnp.where(gl == gmax, row, big), axis=0, keepdims=True)
    p_group = 1.0 / jnp.sum(jnp.exp(gl - gmax), axis=0, keepdims=True)
    lo = MOE_GROUPS + MOE_EXPERTS_PER_GROUP * g_top
    in_group = jnp.where(row >= lo, jnp.where(row < lo + MOE_EXPERTS_PER_GROUP, 1, 0), 0) > 0
    el = jnp.where(in_group, logits, neg)
    emax = jnp.max(el, axis=0, keepdims=True)
    ee = jnp.exp(el - emax)
    prob = ee / jnp.sum(ee, axis=0, keepdims=True)
    prob = jnp.where(in_group, prob, -1.0)
    p1 = jnp.max(prob, axis=0, keepdims=True)
    i1 = jnp.min(jnp.where(prob == p1, row, big), axis=0, keepdims=True)
    rest = jnp.where(row == i1, -1.0, prob)
    p2 = jnp.max(rest, axis=0, keepdims=True)
    i2 = jnp.min(jnp.where(rest == p2, row, big), axis=0, keepdims=True)
    tot = p1 + p2
    g1 = p_group * p1 / tot
    g2 = p_group * p2 / tot
    oh = jnp.concatenate([jnp.where(row == i1, 1.0, 0.0), jnp.where(row == i2, 1.0, 0.0)], axis=0)
    tr = lax.broadcasted_iota(I32, (tm, tm), 0)
    tc = lax.broadcasted_iota(I32, (tm, tm), 1)
    pre = _dot(oh, jnp.where(tr < tc, 1.0, 0.0))
    tots = _dot(oh, jnp.ones((tm, LANES), F32))
    reps = tm // LANES
    carry = carry_ref[...]
    base1 = jnp.concatenate([carry] * reps, axis=1)
    base2 = jnp.concatenate([carry + tots[:ROUTER_ROWS]] * reps, axis=1)
    r1 = jnp.sum(oh[:ROUTER_ROWS] * (base1 + pre[:ROUTER_ROWS]), axis=0, keepdims=True)
    r2 = jnp.sum(oh[ROUTER_ROWS:] * (base2 + pre[ROUTER_ROWS:]), axis=0, keepdims=True)
    carry = carry + tots[:ROUTER_ROWS] + tots[ROUTER_ROWS:]
    carry_ref[...] = carry
    cnt_ref[...] = carry
    e1 = (i1 - MOE_GROUPS).astype(F32)
    e2 = (i2 - MOE_GROUPS).astype(F32)
    slot_rows = [e1, e2, r1, r2, g1, g2]
    rows8 = lax.broadcasted_iota(I32, (SUBLANES, tm), 0)
    slot = jnp.zeros((SUBLANES, tm), F32)
    for j, val in enumerate(slot_rows):
        slot = jnp.where(rows8 == j, val, slot)
    slot_ref[...] = slot
    wide = jnp.concatenate([slot, jnp.zeros((LANES - SUBLANES, tm), F32)], axis=0)
    info_ref[...] = jnp.transpose(wide)


def _router(logits, tm=256):
    n_row_tiles, _, tq = logits.shape
    per = tq // tm
    T = n_row_tiles * tq
    NT = T // tm
    return pl.pallas_call(
        _router_kernel,
        grid=(NT,),
        in_specs=[pl.BlockSpec((None, LANES, tm), lambda i: (i // per, 0, i % per))],
        out_specs=[pl.BlockSpec((tm, LANES), lambda i: (i, 0)),
                   pl.BlockSpec((None, SUBLANES, tm), lambda i: (i, 0, 0)),
                   pl.BlockSpec((ROUTER_ROWS, LANES), lambda i: (0, 0))],
        out_shape=[jax.ShapeDtypeStruct((T, LANES), F32),
                   jax.ShapeDtypeStruct((NT, SUBLANES, tm), F32),
                   jax.ShapeDtypeStruct((ROUTER_ROWS, LANES), F32)],
        scratch_shapes=[pltpu.VMEM((ROUTER_ROWS, LANES), F32)],
        compiler_params=_cparams(1),
        name="router",
    )(logits)


def _row_bytes_wait(hbm, buf, sem):
    pltpu.make_async_copy(buf, hbm.at[pl.ds(0, buf.shape[0]), :], sem).wait()


def _to_row_tiles(ref, val):
    n = val.shape[0]
    for c in range(SUBLANES):
        ref[pl.ds(c, n, stride=SUBLANES), :] = val[:, c * LANES:(c + 1) * LANES]


def _from_row_tiles(ref):
    n = ref.shape[0] // SUBLANES
    return jnp.concatenate([ref[pl.ds(c, n, stride=SUBLANES), :] for c in range(SUBLANES)], axis=1)


def _moe_dispatch_kernel(pends_ref, cnt_ref, dest_ref, x_ref, g_ref, hs_hbm, hbuf, zbuf, sems, zsem, *, td):
    i = pl.program_id(0)
    nt = pl.num_programs(0)
    slot = lax.rem(i, 2)

    @pl.when(i == 0)
    def _():
        zbuf[...] = jnp.zeros_like(zbuf)
        for e in range(MOE_EXPERTS):
            @pl.when(cnt_ref[e] > 0)
            def _():
                start = pl.multiple_of((pends_ref[e] - MOE_BLOCK) * SUBLANES, MOE_BLOCK)
                pltpu.make_async_copy(zbuf, hs_hbm.at[pl.ds(start, MOE_BLOCK * SUBLANES), :], zsem).start()
        for e in range(MOE_EXPERTS):
            @pl.when(cnt_ref[e] > 0)
            def _():
                pltpu.make_async_copy(zbuf, hs_hbm.at[pl.ds(0, MOE_BLOCK * SUBLANES), :], zsem).wait()

        first_unused = pends_ref[MOE_EXPERTS - 1] // MOE_BLOCK
        n_blocks = hs_hbm.shape[0] // (MOE_BLOCK * SUBLANES)

        def zero_start(blk, carry):
            start = pl.multiple_of(blk * (MOE_BLOCK * SUBLANES), MOE_BLOCK)
            pltpu.make_async_copy(zbuf, hs_hbm.at[pl.ds(start, MOE_BLOCK * SUBLANES), :], zsem).start()
            return carry

        def zero_wait(blk, carry):
            pltpu.make_async_copy(zbuf, hs_hbm.at[pl.ds(0, MOE_BLOCK * SUBLANES), :], zsem).wait()
            return carry

        lax.fori_loop(first_unused, n_blocks, zero_start, 0)
        lax.fori_loop(first_unused, n_blocks, zero_wait, 0)

    hb = hbuf.at[slot]
    _to_row_tiles(hb, _rms(x_ref[...], g_ref[...]))
    for j in range(td):
        for c in range(2):
            row = pl.multiple_of(dest_ref[0, c * td + j] * SUBLANES, SUBLANES)
            pltpu.make_async_copy(hb.at[pl.ds(j * SUBLANES, SUBLANES), :],
                                  hs_hbm.at[pl.ds(row, SUBLANES), :],
                                  sems.at[slot]).start(priority=c)

    @pl.when(i > 0)
    def _():
        other = hbuf.at[1 - slot]
        _row_bytes_wait(hs_hbm, other, sems.at[1 - slot])
        _row_bytes_wait(hs_hbm, other, sems.at[1 - slot])

    @pl.when(i == nt - 1)
    def _():
        _row_bytes_wait(hs_hbm, hb, sems.at[slot])
        _row_bytes_wait(hs_hbm, hb, sems.at[slot])


def _moe_dispatch(x, g, pends, counts, dest3, P, td):
    T, D = x.shape
    assert D == ROW_TILE
    grid_spec = pltpu.PrefetchScalarGridSpec(
        num_scalar_prefetch=2,
        grid=(T // td,),
        in_specs=[pl.BlockSpec((None, 1, 2 * td), lambda i, pe, cn: (i, 0, 0), memory_space=pltpu.SMEM),
                  pl.BlockSpec((td, D), lambda i, pe, cn: (i, 0)),
                  pl.BlockSpec((1, D), lambda i, pe, cn: (0, 0))],
        out_specs=pl.BlockSpec(memory_space=pl.ANY),
        scratch_shapes=[pltpu.VMEM((2, td * SUBLANES, LANES), F32),
                        pltpu.VMEM((MOE_BLOCK * SUBLANES, LANES), F32),
                        pltpu.SemaphoreType.DMA((2,)),
                        pltpu.SemaphoreType.DMA(())],
    )
    return pl.pallas_call(
        functools.partial(_moe_dispatch_kernel, td=td),
        grid_spec=grid_spec,
        out_shape=jax.ShapeDtypeStruct((P * SUBLANES, LANES), F32),
        compiler_params=_cparams(1),
        name="moe_dispatch",
    )(pends, counts, dest3, x, g.reshape(1, D))


def _moe_expert_kernel(be_ref, nu_ref, hs_ref, w1_ref, w3_ref, w2_ref, o_ref, w1b, w3b, w2b):
    i = pl.program_id(0)
    used = i < nu_ref[0]
    changed = jnp.logical_or(i == 0, be_ref[i] != be_ref[jnp.maximum(i - 1, 0)])

    @pl.when(jnp.logical_and(used, changed))
    def _():
        w1b[...] = w1_ref[...].astype(BF16)
        w3b[...] = w3_ref[...].astype(BF16)
        w2b[...] = w2_ref[...].astype(BF16)

    @pl.when(used)
    def _():
        xe = _from_row_tiles(hs_ref).astype(BF16)
        ff = w1b.shape[1]
        halves = [slice(0, ff // 2), slice(ff // 2, ff)]
        ups = [(jnp.dot(xe, w1b[:, sl], preferred_element_type=F32),
                jnp.dot(xe, w3b[:, sl], preferred_element_type=F32)) for sl in halves]
        act = [(a * _sigmoid(a) * b).astype(BF16) for a, b in ups]
        y = sum(jnp.dot(a, w2b[sl, :], preferred_element_type=F32) for a, sl in zip(act, halves))
        _to_row_tiles(o_ref, y)

    @pl.when(jnp.logical_not(used))
    def _():
        o_ref[...] = jnp.zeros_like(o_ref)


def _moe_experts(hs, block_e, n_used, w1, w3, w2, layer):
    P = hs.shape[0] // SUBLANES
    D = ROW_TILE
    FF = w1.shape[-1]
    NB = P // MOE_BLOCK
    last = lambda i, nu: jnp.minimum(i, nu[0] - 1)
    grid_spec = pltpu.PrefetchScalarGridSpec(
        num_scalar_prefetch=2,
        grid=(NB,),
        in_specs=[pl.BlockSpec((MOE_BLOCK * SUBLANES, LANES), lambda i, be, nu: (last(i, nu), 0)),
                  pl.BlockSpec((None, None, D, FF), lambda i, be, nu: (layer, be[last(i, nu)], 0, 0)),
                  pl.BlockSpec((None, None, D, FF), lambda i, be, nu: (layer, be[last(i, nu)], 0, 0)),
                  pl.BlockSpec((None, None, FF, D), lambda i, be, nu: (layer, be[last(i, nu)], 0, 0))],
        out_specs=pl.BlockSpec((MOE_BLOCK * SUBLANES, LANES), lambda i, be, nu: (i, 0)),
        scratch_shapes=[pltpu.VMEM((D, FF), BF16),
                        pltpu.VMEM((D, FF), BF16),
                        pltpu.VMEM((FF, D), BF16)],
    )
    return pl.pallas_call(
        _moe_expert_kernel,
        grid_spec=grid_spec,
        out_shape=jax.ShapeDtypeStruct((P * SUBLANES, LANES), F32),
        compiler_params=_cparams(1),
        name="moe_experts",
    )(block_e, n_used, hs, w1, w3, w2)


def _gather_rows(src_hbm, idx_ref, dst_ref, sem, first, last):
    for r in range(first, last):
        row = pl.multiple_of(idx_ref[0, r] * SUBLANES, SUBLANES)
        pltpu.make_async_copy(src_hbm.at[pl.ds(row, SUBLANES), :],
                              dst_ref.at[pl.ds(r * SUBLANES, SUBLANES), :], sem).start(priority=r % 2)


def _moe_combine_kernel(pos_ref, posn_ref, x_ref, info_ref, yb_hbm, g_ref, *rest, tc, final_norm, splits):
    if splits:
        w_ref, o_ref = rest[0], rest[1]
        p_refs = rest[2:2 + len(splits)]
        ybuf, sems = rest[2 + len(splits):]
    else:
        o_ref, ybuf, sems = rest
    i = pl.program_id(0)
    nb = pl.num_programs(0)
    slot = lax.rem(i, 2)

    @pl.when(i == 0)
    def _():
        def issue(r, carry):
            src = pl.multiple_of(pos_ref[0, r] * SUBLANES, SUBLANES)
            dst = pl.multiple_of(r * SUBLANES, SUBLANES)
            pltpu.make_async_copy(yb_hbm.at[pl.ds(src, SUBLANES), :],
                                  ybuf.at[0, pl.ds(dst, SUBLANES), :], sems.at[0]).start()
            return carry
        lax.fori_loop(0, 2 * tc, issue, 0)

    def wait_tile(s_):
        pltpu.make_async_copy(yb_hbm.at[pl.ds(0, 2 * tc * SUBLANES), :], ybuf.at[s_], sems.at[s_]).wait()

    if not splits:
        @pl.when(i + 1 < nb)
        def _():
            _gather_rows(yb_hbm, posn_ref, ybuf.at[1 - slot], sems.at[1 - slot], 0, 2 * tc)

    wait_tile(slot)
    info = info_ref[...]
    yb = ybuf.at[slot]
    y0 = _from_row_tiles(yb.at[pl.ds(0, tc * SUBLANES), :])
    y1 = _from_row_tiles(yb.at[pl.ds(tc * SUBLANES, tc * SUBLANES), :])
    out = x_ref[...] + (y0 * info[:, 4:5] + y1 * info[:, 5:6])
    if final_norm:
        out = _rms(out, g_ref[...])
    o_ref[...] = out
    if splits:
        h = _rms(out, g_ref[...]).astype(BF16)
        chunks = []
        for p_ref, n in zip(p_refs, splits):
            col0 = sum(c[2] for c in chunks)
            for c0 in range(0, n, PROJ_CHUNK):
                chunks.append((p_ref, c0, min(PROJ_CHUNK, n - c0), col0))
        per = -(-2 * tc // len(chunks))
        off = 0
        for ci, (p_ref, c0, width, _) in enumerate(chunks):
            val = jnp.dot(h, w_ref[:, off:off + width], preferred_element_type=F32)
            _gather_rows(yb_hbm, posn_ref, ybuf.at[1 - slot], sems.at[1 - slot],
                         min(ci * per, 2 * tc), min((ci + 1) * per, 2 * tc))
            p_ref[:, c0:c0 + width] = val.astype(p_ref.dtype)
            off += width

        @pl.when(i == nb - 1)
        def _():
            wait_tile(1 - slot)


def _moe_combine(x, info, dest3, yb, g, final_norm, tc, next_proj=None):
    T, D = x.shape
    NT = T // tc
    in_specs = [pl.BlockSpec((None, 1, 2 * tc), lambda i: (i, 0, 0), memory_space=pltpu.SMEM),
                pl.BlockSpec((None, 1, 2 * tc), lambda i: (jnp.minimum(i + 1, NT - 1), 0, 0),
                             memory_space=pltpu.SMEM),
                pl.BlockSpec((tc, D), lambda i: (i, 0)),
                pl.BlockSpec((tc, LANES), lambda i: (i, 0)),
                pl.BlockSpec(memory_space=pl.ANY),
                pl.BlockSpec((1, D), lambda i: (0, 0))]
    out_specs = [pl.BlockSpec((tc, D), lambda i: (i, 0))]
    out_shape = [jax.ShapeDtypeStruct((T, D), F32)]
    args = [dest3, dest3, x, info, yb, g.reshape(1, D)]
    splits = ()
    if next_proj is not None:
        w, splits, out_dtypes = next_proj
        assert not final_norm and sum(splits) == w.shape[1]
        in_specs.append(pl.BlockSpec(w.shape, lambda i: (0, 0)))
        args.append(w)
        out_specs += [pl.BlockSpec((tc, n), lambda i: (i, 0)) for n in splits]
        out_shape += [jax.ShapeDtypeStruct((T, n), dt) for n, dt in zip(splits, out_dtypes)]
    outs = pl.pallas_call(
        functools.partial(_moe_combine_kernel, tc=tc, final_norm=final_norm, splits=tuple(splits)),
        grid=(NT,),
        in_specs=in_specs,
        out_specs=out_specs,
        out_shape=out_shape,
        scratch_shapes=[pltpu.VMEM((2, 2 * tc * SUBLANES, LANES), F32), pltpu.SemaphoreType.DMA((2,))],
        compiler_params=_cparams(1),
        name="moe_combine",
    )(*args)
    return outs[0], tuple(outs[1:])


MOE_TILE = 512
MOE_COMBINE_TILE = 256
MOE_LAST_COMBINE_TILE = 256
PROJ_CHUNK = 256
ROUTER_TILE = 256


def _tile_slots(dest, tile):
    n_tiles = dest.shape[0] * dest.shape[2] // tile
    return jnp.concatenate([dest[:, 0, :].reshape(n_tiles, 1, tile), dest[:, 1, :].reshape(n_tiles, 1, tile)], axis=2)


def _router_params(w_group, b_group, w_expert, b_expert, lanes_out):
    D = w_group.shape[0]
    n_log = MOE_GROUPS + MOE_EXPERTS
    wt = jnp.zeros((LANES, D), F32).at[:MOE_GROUPS].set(w_group.T).at[MOE_GROUPS:n_log].set(w_expert.T)
    bt = jnp.zeros((LANES,), F32).at[:MOE_GROUPS].set(b_group).at[MOE_GROUPS:n_log].set(b_expert)
    return wt.astype(BF16), jnp.broadcast_to(bt[:, None], (LANES, lanes_out))


def _moe_layer(x, g, logits, w1, w3, w2, layer, g_out, final_norm, next_proj):
    T, D = x.shape
    n_log = MOE_GROUPS + MOE_EXPERTS
    info, slot, cnt = _router(logits, ROUTER_TILE)
    P = 2 * T + MOE_EXPERTS * MOE_BLOCK
    NB = P // MOE_BLOCK
    counts = cnt[MOE_GROUPS:n_log, 0].astype(I32)
    padded = (counts + MOE_BLOCK - 1) // MOE_BLOCK * MOE_BLOCK
    pends = jnp.cumsum(padded).astype(I32)
    pstarts = pends - padded
    block_start = jnp.arange(NB, dtype=I32) * MOE_BLOCK
    block_e = jnp.minimum(jnp.sum((pends[None, :] <= block_start[:, None]).astype(I32), axis=1),
                          MOE_EXPERTS - 1).astype(I32)
    n_used = (pends[-1:] // MOE_BLOCK).astype(I32)
    eid = slot[:, 0:2, :].astype(I32)
    expert_ids = jnp.arange(MOE_EXPERTS, dtype=I32)
    seg_start = jnp.sum(jnp.where(eid[..., None] == expert_ids, pstarts, 0), axis=-1)
    dest = seg_start + slot[:, 2:4, :].astype(I32)
    hs = _moe_dispatch(x, g, pends, counts, _tile_slots(dest, MOE_TILE), P, MOE_TILE)
    yb = _moe_experts(hs, block_e, n_used, w1, w3, w2, layer)
    tc = MOE_LAST_COMBINE_TILE if next_proj is None else MOE_COMBINE_TILE
    return _moe_combine(x, info, _tile_slots(dest, tc), yb, g_out, final_norm, tc, next_proj)


def kernel(x, mem, norm_mix, norm_xattn, norm_moe, norm_final, ev_w_in, ev_sinks, ev_mu, ev_w0, ev_w2, ev_a0, ev_a2, ev_g2, ev_k_k, ev_k_a, ev_r_k, ev_lnx_w, ev_lnx_b, ev_w_out, od_w_in, od_gate_up, od_gate_b, od_onorm, od_w_out, mem_norm, mem_wk, mem_wv, xa_wq, xa_wo, moe_w_group, moe_b_group, moe_w_expert, moe_b_expert, moe_w1, moe_w3, moe_w2):
    B, S, D = x.shape
    M = mem.shape[1]
    T = B * S
    depth = norm_mix.shape[0]
    xf = x.reshape(T, D)

    XW = XA_HEADS * XA_HEAD_DIM
    w_kv = jnp.concatenate([mem_wk, mem_wv], axis=1).astype(BF16)
    mk, mv = _norm_matmul(mem.reshape(B * M, D), mem_norm, w_kv, (XW, XW), (BF16, BF16))

    KW = GLA_HEADS * GLA_DK
    VW = GLA_HEADS * GLA_DV
    swa_cols = SWA_Q_HEADS * HEAD_DIM + 2 * (SWA_Q_HEADS // SWA_GROUP) * HEAD_DIM

    def in_proj(layer):
        i = layer // 2
        if layer % 2 == 0:
            return ev_w_in[i].astype(BF16), (swa_cols, ev_w_in.shape[-1] - swa_cols), (F32, F32)
        R = od_gate_up.shape[1]
        w = od_w_in[i]
        w_re = jnp.concatenate([w[:, :2 * KW + VW], w[:, 2 * KW + VW + R:],
                                w[:, 2 * KW + VW:2 * KW + VW + R],
                                jnp.zeros((D, LANES - R), F32)], axis=1).astype(BF16)
        return w_re, (2 * KW + 2 * VW, LANES), (F32, F32)

    w_first, _, _ = in_proj(0)
    qkv0, rw0 = _in_proj_rwkv(xf, norm_mix[0], w_first, swa_cols, ev_mu[0], ev_w0[0], ev_w2[0], ev_a0[0], ev_a2[0],
                              ev_g2[0], ev_k_k[0], ev_k_a[0], S)
    proj = None
    for layer in range(depth):
        i = layer // 2
        if layer % 2 == 0:
            if layer == 0:
                qkv, rw = qkv0, [t.reshape(B, S, RWKV_WIDTH) for t in rw0]
            else:
                qkv, p_rw = proj
                rw = _rwkv_prep(p_rw, ev_mu[i], ev_w0[i], ev_w2[i], ev_a0[i], ev_a2[i],
                                ev_g2[i], ev_k_k[i], ev_k_a[i], B, S)
            o_a = _swa(qkv, ev_sinks[i], B, S)
            o_b = _rwkv_scan(*rw, ev_r_k[i].reshape(-1), ev_lnx_w[i], ev_lnx_b[i])
            w_out = ev_w_out[i].astype(BF16)
            qw = o_a.shape[-1]
            mix_acts, mix_ws = [o_a, o_b], [w_out[:qw], w_out[qw:]]
        else:
            qkvo, gd = proj
            R = od_gate_up.shape[1]
            gup = jnp.zeros((LANES, KW), F32).at[:R].set(od_gate_up[i]).astype(BF16)
            o = _gla(qkvo, gd, gup, od_gate_b[i], od_onorm[i], B, S)
            mix_acts, mix_ws = [o], [od_w_out[i].astype(BF16)]
        wt_router, bt_router = _router_params(moe_w_group[layer], moe_b_group[layer], moe_w_expert[layer],
                                              moe_b_expert[layer], XA_SUB_ROWS)
        xf, logits = _mix_proj_xattn(xf, mix_acts, mix_ws, norm_xattn[layer], xa_wq[layer].astype(BF16), mk, mv,
                                     xa_wo[layer].astype(BF16), norm_moe[layer], wt_router, bt_router, B, S)
        last = layer == depth - 1
        g_out = norm_final if last else norm_mix[layer + 1]
        xf, proj = _moe_layer(xf, norm_moe[layer], logits, moe_w1, moe_w3, moe_w2, layer,
                              g_out, last, None if last else in_proj(layer + 1))
    return xf.reshape(B, S, D)
```

```python
import functools

import jax
import jax.numpy as jnp
from jax import lax
from jax.experimental import pallas as pl
from jax.experimental.pallas import tpu as pltpu

F32 = jnp.float32
BF16 = jnp.bfloat16
I32 = jnp.int32

EPS = 1e-6
HEAD_DIM = 64
SWA_WINDOW = 128
SWA_Q_HEADS = 8
SWA_GROUP = 4
RWKV_HEADS = 8
RWKV_WIDTH = 512
RWKV_LN_EPS = 64e-5
RWKV_CHUNK = 64
GLA_HEADS = 4
GLA_DK = 128
GLA_DV = 256
GLA_CHUNK = 64
GLA_GATE_NORM = 16.0
XA_HEADS = 4
XA_HEAD_DIM = 128
MOE_GROUPS = 4
MOE_EXPERTS_PER_GROUP = 8
MOE_EXPERTS = 32
MOE_BLOCK = 512
LANES = 128
SUBLANES = 8
ROW_TILE = SUBLANES * LANES

VMEM_LIMIT_BYTES = 48 * 1024 * 1024


def _cparams(n_axes):
    return pltpu.CompilerParams(dimension_semantics=("arbitrary",) * n_axes,
                                vmem_limit_bytes=VMEM_LIMIT_BYTES)


def _dot(a, b):
    return jnp.dot(a.astype(BF16), b.astype(BF16), preferred_element_type=F32)


def _dot_nt(a, b):
    return lax.dot_general(a.astype(BF16), b.astype(BF16), (((1,), (1,)), ((), ())),
                           preferred_element_type=F32)


def _dot_tn(a, b):
    return lax.dot_general(a.astype(BF16), b.astype(BF16), (((0,), (0,)), ((), ())),
                           preferred_element_type=F32)


def _dot_f32(a, b):
    return jnp.dot(a, b, preferred_element_type=F32, precision=lax.Precision.HIGHEST)


def _rms(x, g):
    ms = jnp.mean(x * x, axis=-1, keepdims=True)
    return x * lax.rsqrt(ms + EPS) * g


def _sigmoid(x):
    return 1.0 / (1.0 + jnp.exp(-x))


def _softplus(x):
    return jnp.maximum(x, 0.0) + jnp.log(1.0 + jnp.exp(-jnp.abs(x)))


def _norm_matmul_kernel(x_ref, g_ref, w_ref, *o_refs, splits):
    h = _rms(x_ref[...], g_ref[...]).astype(BF16)
    off = 0
    for o_ref, n in zip(o_refs, splits):
        o_ref[...] = jnp.dot(h, w_ref[:, off:off + n], preferred_element_type=F32).astype(o_ref.dtype)
        off += n


def _norm_matmul(x, g, w, splits, out_dtypes, tm=256):
    T, D = x.shape
    N = w.shape[1]
    assert sum(splits) == N and T % tm == 0
    return pl.pallas_call(
        functools.partial(_norm_matmul_kernel, splits=tuple(splits)),
        grid=(T // tm,),
        in_specs=[pl.BlockSpec((tm, D), lambda i: (i, 0)),
                  pl.BlockSpec((1, D), lambda i: (0, 0)),
                  pl.BlockSpec((D, N), lambda i: (0, 0))],
        out_specs=[pl.BlockSpec((tm, n), lambda i: (i, 0)) for n in splits],
        out_shape=[jax.ShapeDtypeStruct((T, n), dt) for n, dt in zip(splits, out_dtypes)],
        compiler_params=_cparams(1),
        name="norm_matmul",
    )(x, g.reshape(1, D), w)


def _swa_kernel(sinks_ref, q_ref, kp_ref, kc_ref, vp_ref, vc_ref, o_ref):
    n = pl.program_id(1)
    W = SWA_WINDOW
    NB = q_ref.shape[0]
    qpos = lax.broadcasted_iota(I32, (W, 2 * W), 0) + W
    kpos = lax.broadcasted_iota(I32, (W, 2 * W), 1)
    rel = qpos - kpos
    in_window = jnp.where(rel >= 0, jnp.where(rel < W, 1, 0), 0)
    has_prev = jnp.where(n > 0, 1, 0)
    valid = (in_window * jnp.where(kpos >= W, 1, has_prev)) > 0
    n_groups = SWA_Q_HEADS // SWA_GROUP
    streams = [(bi, g) for bi in range(NB) for g in range(n_groups)]
    qb = [q_ref[bi].astype(BF16) for bi in range(NB)]
    kb = [jnp.concatenate([kp_ref[bi], kc_ref[bi]], axis=0).astype(BF16) for bi in range(NB)]
    vb = [jnp.concatenate([vp_ref[bi], vc_ref[bi]], axis=0).astype(BF16) for bi in range(NB)]
    gs = lambda g: slice(g * HEAD_DIM, (g + 1) * HEAD_DIM)
    scores = []
    for bi, g in streams:
        qg = jnp.concatenate([qb[bi][:, h * HEAD_DIM:(h + 1) * HEAD_DIM]
                              for h in range(g * SWA_GROUP, (g + 1) * SWA_GROUP)], axis=0)
        scores.append(_dot_nt(qg, kb[bi][:, gs(g)]))
    probs = []
    for i, (bi, g) in enumerate(streams):
        pieces = []
        for j in range(SWA_GROUP):
            s = jnp.where(valid, scores[i][j * W:(j + 1) * W] * (HEAD_DIM ** -0.5), -jnp.inf)
            sink = sinks_ref[g * SWA_GROUP + j]
            m = jnp.maximum(jnp.max(s, axis=-1, keepdims=True), sink)
            p = jnp.exp(s - m)
            den = jnp.sum(p, axis=-1, keepdims=True) + jnp.exp(sink - m)
            pieces.append((p / den).astype(BF16))
        probs.append(jnp.concatenate(pieces, axis=0))
    ogs = [_dot(probs[i], vb[bi][:, gs(g)]) for i, (bi, g) in enumerate(streams)]
    for bi in range(NB):
        outs = []
        for g in range(n_groups):
            og = ogs[bi * n_groups + g]
            outs += [og[j * W:(j + 1) * W] for j in range(SWA_GROUP)]
        o_ref[bi] = jnp.concatenate(outs, axis=1).astype(o_ref.dtype)


SWA_BATCH_ROWS = 2


def _swa(qkv, sinks, B, S):
    W = SWA_WINDOW
    assert S % W == 0
    qkv3 = qkv.reshape(B, S, qkv.shape[-1])
    qw = SWA_Q_HEADS * HEAD_DIM
    kw = qw // SWA_GROUP
    kcol = qw // kw
    nb = SWA_BATCH_ROWS if B % SWA_BATCH_ROWS == 0 else 1
    out = pl.pallas_call(
        _swa_kernel,
        grid=(B // nb, S // W),
        in_specs=[pl.BlockSpec(memory_space=pltpu.SMEM),
                  pl.BlockSpec((nb, W, qw), lambda b, n: (b, n, 0)),
                  pl.BlockSpec((nb, W, kw), lambda b, n: (b, jnp.maximum(n - 1, 0), kcol)),
                  pl.BlockSpec((nb, W, kw), lambda b, n: (b, n, kcol)),
                  pl.BlockSpec((nb, W, kw), lambda b, n: (b, jnp.maximum(n - 1, 0), kcol + 1)),
                  pl.BlockSpec((nb, W, kw), lambda b, n: (b, n, kcol + 1))],
        out_specs=pl.BlockSpec((nb, W, qw), lambda b, n: (b, n, 0)),
        out_shape=jax.ShapeDtypeStruct((B, S, qw), BF16),
        compiler_params=_cparams(2),
        name="swa",
    )(sinks, qkv3, qkv3, qkv3, qkv3, qkv3)
    return out.reshape(B * S, qw)


def _rwkv_prep_math(p, last, mu_ref, w0_ref, w2_ref, a0_ref, a2_ref, g2_ref, kk_ref, ka_ref, outs):
    r_out, lw_out, k_out, v_out, a_out, b_out, g_out = outs
    C = RWKV_WIDTH
    row = lax.broadcasted_iota(I32, p.shape, 0)
    p_prev = jnp.where(row == 0, last, pltpu.roll(p, 1, axis=0))
    p = p + (p_prev - p) * mu_ref[...]
    r = p[:, :C]
    k = p[:, C:2 * C]
    v = p[:, 2 * C:3 * C]
    xw = p[:, 3 * C:3 * C + 64]
    xa = p[:, 3 * C + 64:3 * C + 128]
    xg = p[:, 3 * C + 128:]
    w = -_softplus(-(w0_ref[...] + _dot(jnp.tanh(xw), w2_ref[...]))) - 0.5
    lw = -jnp.exp(w)
    a = _sigmoid(a0_ref[...] + _dot(xa, a2_ref[...]))
    g = _dot(_sigmoid(xg), g2_ref[...])
    kk = k * kk_ref[...]
    pieces = []
    for h in range(RWKV_HEADS):
        kh = kk[:, h * HEAD_DIM:(h + 1) * HEAD_DIM]
        nrm = jnp.sqrt(jnp.sum(kh * kh, axis=-1, keepdims=True))
        pieces.append(kh / jnp.maximum(nrm, 1e-12))
    kk = jnp.concatenate(pieces, axis=1)
    r_out[...] = r
    lw_out[...] = lw
    k_out[...] = k * (1.0 + (a - 1.0) * ka_ref[...])
    v_out[...] = v
    a_out[...] = -kk
    b_out[...] = kk * a
    g_out[...] = g


def _rwkv_prep_kernel(p_ref, pprev_ref, *refs):
    n = pl.program_id(1)
    last = jnp.where(n > 0, pprev_ref[SUBLANES - 1:SUBLANES, :], 0.0)
    _rwkv_prep_math(p_ref[...], last, *refs[:8], refs[8:])


def _rwkv_params(mu, w0, w2, a0, a2, g2, k_k, k_a):
    row = lambda t: t.reshape(1, -1)
    return [row(mu), row(w0), w2.astype(BF16), row(a0), a2.astype(BF16), g2.astype(BF16), row(k_k), row(k_a)]


def _rwkv_prep(p, mu, w0, w2, a0, a2, g2, k_k, k_a, B, S, tt=256):
    assert S % tt == 0
    C = RWKV_WIDTH
    PW = p.shape[-1]
    p3 = p.reshape(B, S, PW)
    full = lambda arr: pl.BlockSpec(arr.shape, lambda b, n: (0,) * arr.ndim)
    params = _rwkv_params(mu, w0, w2, a0, a2, g2, k_k, k_a)
    outs = pl.pallas_call(
        _rwkv_prep_kernel,
        grid=(B, S // tt),
        in_specs=[pl.BlockSpec((None, tt, PW), lambda b, n: (b, n, 0)),
                  pl.BlockSpec((None, SUBLANES, PW),
                               lambda b, n: (b, jnp.maximum(n * (tt // SUBLANES) - 1, 0), 0))]
                 + [full(t) for t in params],
        out_specs=[pl.BlockSpec((None, tt, C), lambda b, n: (b, n, 0))] * 7,
        out_shape=[jax.ShapeDtypeStruct((B, S, C), F32)] * 7,
        compiler_params=_cparams(2),
        name="rwkv_prep",
    )(p3, p3, *params)
    return outs


def _in_proj_rwkv_kernel(x_ref, g_ref, w_ref, *refs, swa_cols, tiles_per_seq):
    params = refs[:8]
    qkv_out = refs[8]
    outs = refs[9:16]
    p_buf, last_buf = refs[16:]
    i = pl.program_id(0)

    @pl.when(i == 0)
    def _():
        p_buf[...] = jnp.zeros_like(p_buf)
        last_buf[...] = jnp.zeros_like(last_buf)

    j = i - 1
    p_prev_tile = p_buf[lax.rem(i + 1, 2)]
    tm = p_prev_tile.shape[0]
    last = jnp.where(lax.rem(j, tiles_per_seq) == 0, 0.0, last_buf[...])
    _rwkv_prep_math(p_prev_tile, last, *params, outs)
    last_buf[...] = p_prev_tile[tm - 1:tm, :]
    h = _rms(x_ref[...], g_ref[...]).astype(BF16)
    qkv_out[...] = jnp.dot(h, w_ref[:, :swa_cols], preferred_element_type=F32)
    p_buf[lax.rem(i, 2)] = jnp.dot(h, w_ref[:, swa_cols:], preferred_element_type=F32)


def _in_proj_rwkv(x, g, w, swa_cols, mu, w0, w2, a0, a2, g2, k_k, k_a, S, tm=256):
    T, D = x.shape
    assert S % tm == 0 and T % S == 0
    N = w.shape[1]
    C = RWKV_WIDTH
    NT = T // tm
    params = _rwkv_params(mu, w0, w2, a0, a2, g2, k_k, k_a)
    const = lambda arr: pl.BlockSpec(arr.shape, lambda i: (0,) * arr.ndim)
    cur = lambda i: (jnp.minimum(i, NT - 1), 0)
    prev = lambda i: (jnp.maximum(i - 1, 0), 0)
    outs = pl.pallas_call(
        functools.partial(_in_proj_rwkv_kernel, swa_cols=swa_cols, tiles_per_seq=S // tm),
        grid=(NT + 1,),
        in_specs=[pl.BlockSpec((tm, D), cur), pl.BlockSpec((1, D), lambda i: (0, 0)), const(w)]
                 + [const(t) for t in params],
        out_specs=[pl.BlockSpec((tm, swa_cols), cur)] + [pl.BlockSpec((tm, C), prev)] * 7,
        out_shape=[jax.ShapeDtypeStruct((T, swa_cols), F32)] + [jax.ShapeDtypeStruct((T, C), F32)] * 7,
        scratch_shapes=[pltpu.VMEM((2, tm, N - swa_cols), F32), pltpu.VMEM((1, N - swa_cols), F32)],
        compiler_params=_cparams(1),
        name="in_proj_rwkv",
    )(x, g.reshape(1, D), w, *params)
    return outs[0], outs[1:]


def _pair_blockdiag(x):
    lane = lax.broadcasted_iota(I32, x.shape, 1)
    zero = jnp.zeros_like(x)
    return jnp.concatenate([jnp.where(lane < HEAD_DIM, x, zero), jnp.where(lane >= HEAD_DIM, x, zero)], axis=0)


def _rwkv_scan_kernel(r_ref, lw_ref, k_ref, v_ref, a_ref, b_ref, g_ref, rk_ref, lnw_ref, lnb_ref,
                      o_ref, s_ref):
    c = pl.program_id(1)

    @pl.when(c == 0)
    def _():
        s_ref[...] = jnp.zeros_like(s_ref)

    C = RWKV_CHUNK
    NB = r_ref.shape[0]
    NP = RWKV_HEADS // 2
    PW = 2 * HEAD_DIM
    row = lax.broadcasted_iota(I32, (C, C), 0)
    col = lax.broadcasted_iota(I32, (C, C), 1)
    tri = jnp.where(row >= col, 1.0, 0.0).astype(F32)
    rowp = lax.broadcasted_iota(I32, (C, PW), 0)
    colp = lax.broadcasted_iota(I32, (C, PW), 1)
    colp = jnp.where(colp >= HEAD_DIM, colp - HEAD_DIM, colp)
    lower_p = rowp >= colp
    strict_p = rowp > colp
    rows = lax.broadcasted_iota(I32, (PW, PW), 0)
    cols = lax.broadcasted_iota(I32, (PW, PW), 1)
    same_head = jnp.where(rows >= HEAD_DIM, 1, 0) == jnp.where(cols >= HEAD_DIM, 1, 0)
    first = lax.broadcasted_iota(I32, (C, PW), 1) < HEAD_DIM

    streams = [(bi, p) for bi in range(NB) for p in range(NP)]
    pre = []
    for bi in range(NB):
        lw = lw_ref[bi]
        cum = _dot_f32(tri, lw)
        cum_last = cum[C - 1:C, :]
        r = r_ref[bi]
        k = k_ref[bi]
        v = v_ref[bi]
        a = a_ref[bi]
        b = b_ref[bi]
        e_neg = jnp.exp(-cum)
        e_rem = jnp.exp(cum_last - cum)
        pre.append(dict(
            r_t=(r * jnp.exp(cum)).astype(BF16), a_t=(a * jnp.exp(cum - lw)).astype(BF16),
            b_t=(b * e_neg).astype(BF16), k_t=(k * e_neg).astype(BF16),
            b_d=(b * e_rem).astype(BF16), k_d=(k * e_rem).astype(BF16),
            v_b=v.astype(BF16), v=v, e_last=jnp.exp(cum_last), rkk=r * k * rk_ref[...], g=g_ref[bi]))

    def lanes(p):
        return slice(p * PW, (p + 1) * PW)

    ar = [jnp.concatenate([pre[bi]['a_t'][:, lanes(p)], pre[bi]['r_t'][:, lanes(p)]], axis=0) for bi, p in streams]
    s0 = [s_ref[bi, p] for bi, p in streams]
    big = [_dot_nt(ar[i], jnp.concatenate([_pair_blockdiag(pre[bi]['b_t'][:, lanes(p)]),
                                           _pair_blockdiag(pre[bi]['k_t'][:, lanes(p)]),
                                           s0[i].astype(BF16)], axis=0))
           for i, (bi, p) in enumerate(streams)]
    m_b = [t[:, :PW] for t in big]
    m_k = [t[:, PW:2 * PW] for t in big]
    ars = [t[:, 2 * PW:] for t in big]
    v_p = [pre[bi]['v_b'][:, lanes(p)] for bi, p in streams]
    v_bd = [_pair_blockdiag(vp) for vp in v_p]
    x = [ars[i][:C] + _dot(jnp.where(strict_p, m_k[i][:C], 0.0), v_bd[i]) for i in range(len(streams))]
    pw = [jnp.where(strict_p, m_b[i][:C], 0.0).astype(BF16) for i in range(len(streams))]
    n_stages = 6
    for stage in range(n_stages):
        if stage < n_stages - 1:
            prod = [_dot(pw[i], jnp.concatenate([_pair_blockdiag(x[i].astype(BF16)), _pair_blockdiag(pw[i])], axis=1))
                    for i in range(len(streams))]
            x = [x[i] + prod[i][:, :PW] for i in range(len(streams))]
            pw = [prod[i][:, PW:].astype(BF16) for i in range(len(streams))]
        else:
            x = [x[i] + _dot(pw[i], _pair_blockdiag(x[i].astype(BF16))) for i in range(len(streams))]
    u_b = [xi.astype(BF16) for xi in x]
    y = [ars[i][C:]
         + _dot(jnp.concatenate([jnp.where(lower_p, m_b[i][C:], 0.0), jnp.where(lower_p, m_k[i][C:], 0.0)], axis=1),
                jnp.concatenate([_pair_blockdiag(u_b[i]), v_bd[i]], axis=0))
         for i in range(len(streams))]
    for i, (bi, p) in enumerate(streams):
        upd = _dot_tn(jnp.concatenate([u_b[i], v_p[i]], axis=0),
                      jnp.concatenate([pre[bi]['b_d'][:, lanes(p)], pre[bi]['k_d'][:, lanes(p)]], axis=0))
        s_ref[bi, p] = s0[i] * pre[bi]['e_last'][:, lanes(p)] + jnp.where(same_head, upd, 0.0)

    lnw = lnw_ref[...]
    lnb = lnb_ref[...]

    def head_sum(t):
        s1 = jnp.sum(jnp.where(first, t, 0.0), axis=-1, keepdims=True)
        s2 = jnp.sum(jnp.where(first, 0.0, t), axis=-1, keepdims=True)
        return jnp.where(first, s1, s2)

    for bi in range(NB):
        outs = []
        for p in range(NP):
            yi = y[bi * NP + p]
            mean = head_sum(yi) * (1.0 / HEAD_DIM)
            yc = yi - mean
            var = head_sum(yc * yc) * (1.0 / HEAD_DIM)
            yn = yc * lax.rsqrt(var + RWKV_LN_EPS) * lnw[:, lanes(p)] + lnb[:, lanes(p)]
            bonus = head_sum(pre[bi]['rkk'][:, lanes(p)]) * pre[bi]['v'][:, lanes(p)]
            outs.append((yn + bonus) * pre[bi]['g'][:, lanes(p)])
        o_ref[bi] = jnp.concatenate(outs, axis=1).astype(o_ref.dtype)


RWKV_BATCH_ROWS = 4


def _rwkv_scan(r, lw, k, v, a, b, g, r_k, lnx_w, lnx_b):
    B, S, W = r.shape
    C = RWKV_CHUNK
    assert S % C == 0 and W == RWKV_HEADS * HEAD_DIM
    nb = RWKV_BATCH_ROWS if B % RWKV_BATCH_ROWS == 0 else 1
    seq = pl.BlockSpec((nb, C, W), lambda bb, c: (bb, c, 0))
    par = pl.BlockSpec((1, W), lambda bb, c: (0, 0))
    out = pl.pallas_call(
        _rwkv_scan_kernel,
        grid=(B // nb, S // C),
        in_specs=[seq] * 7 + [par] * 3,
        out_specs=seq,
        out_shape=jax.ShapeDtypeStruct((B, S, W), BF16),
        scratch_shapes=[pltpu.VMEM((nb, RWKV_HEADS // 2, 2 * HEAD_DIM, 2 * HEAD_DIM), F32)],
        compiler_params=_cparams(2),
        name="rwkv_scan",
    )(r, lw, k, v, a, b, g, r_k.reshape(1, W), lnx_w.reshape(1, W), lnx_b.reshape(1, W))
    return out.reshape(B * S, W)


def _gla_kernel(q_ref, k_ref, v_ref, og_ref, gd_ref, gup_ref, gb_ref, on_ref, o_ref, s_ref):
    c = pl.program_id(1)

    @pl.when(c == 0)
    def _():
        s_ref[...] = jnp.zeros_like(s_ref)

    C = GLA_CHUNK
    NB = q_ref.shape[0]
    row = lax.broadcasted_iota(I32, (C, C), 0)
    col = lax.broadcasted_iota(I32, (C, C), 1)
    lower = row >= col
    tri = jnp.where(lower, 1.0, 0.0).astype(F32)
    onorm = on_ref[...]
    zs = [_dot(gd_ref[bi], gup_ref[...]) + gb_ref[...] for bi in range(NB)]
    cums = [_dot_f32(tri, -_softplus(-z) / GLA_GATE_NORM) for z in zs]
    qe, ke, kd, e_last, v = [], [], [], [], []
    for bi in range(NB):
        cum = cums[bi]
        cum_last = cum[C - 1:C, :]
        k = k_ref[bi]
        qe.append((q_ref[bi] * (GLA_DK ** -0.5) * jnp.exp(cum)).astype(BF16))
        ke.append((k * jnp.exp(-cum)).astype(BF16))
        kd.append((k * jnp.exp(cum_last - cum)).astype(BF16))
        e_last.append(jnp.exp(cum_last))
        v.append(v_ref[bi].astype(BF16))
    streams = [(bi, h) for bi in range(NB) for h in range(GLA_HEADS)]
    ks = lambda h: slice(h * GLA_DK, (h + 1) * GLA_DK)
    vs = lambda h: slice(h * GLA_DV, (h + 1) * GLA_DV)
    sts = [s_ref[bi, h] for bi, h in streams]
    atts = [jnp.where(lower, _dot_nt(qe[bi][:, ks(h)], ke[bi][:, ks(h)]), 0.0) for bi, h in streams]
    inters = [_dot_nt(qe[bi][:, ks(h)], sts[i]) for i, (bi, h) in enumerate(streams)]
    os_ = [inters[i] + _dot(atts[i], v[bi][:, vs(h)]) for i, (bi, h) in enumerate(streams)]
    for i, (bi, h) in enumerate(streams):
        s_ref[bi, h] = sts[i] * e_last[bi][:, ks(h)] + _dot_tn(v[bi][:, vs(h)], kd[bi][:, ks(h)])
    for bi in range(NB):
        og = og_ref[bi]
        outs = []
        for h in range(GLA_HEADS):
            gate = og[:, vs(h)]
            outs.append(_rms(os_[bi * GLA_HEADS + h], onorm) * (gate * _sigmoid(gate)))
        o_ref[bi] = jnp.concatenate(outs, axis=1).astype(o_ref.dtype)


GLA_BATCH_ROWS = 8


def _gla(qkvo, gd, gate_up_pad, gate_b, onorm, B, S):
    C = GLA_CHUNK
    assert S % C == 0
    KW = GLA_HEADS * GLA_DK
    VW = GLA_HEADS * GLA_DV
    x3 = qkvo.reshape(B, S, qkvo.shape[-1])
    gd3 = gd.reshape(B, S, LANES)
    nb = GLA_BATCH_ROWS if B % GLA_BATCH_ROWS == 0 else 1
    out = pl.pallas_call(
        _gla_kernel,
        grid=(B // nb, S // C),
        in_specs=[pl.BlockSpec((nb, C, KW), lambda b, c: (b, c, 0)),
                  pl.BlockSpec((nb, C, KW), lambda b, c: (b, c, 1)),
                  pl.BlockSpec((nb, C, VW), lambda b, c: (b, c, 1)),
                  pl.BlockSpec((nb, C, VW), lambda b, c: (b, c, 2)),
                  pl.BlockSpec((nb, C, LANES), lambda b, c: (b, c, 0)),
                  pl.BlockSpec((LANES, KW), lambda b, c: (0, 0)),
                  pl.BlockSpec((1, KW), lambda b, c: (0, 0)),
                  pl.BlockSpec((1, GLA_DV), lambda b, c: (0, 0))],
        out_specs=pl.BlockSpec((nb, C, VW), lambda b, c: (b, c, 0)),
        out_shape=jax.ShapeDtypeStruct((B, S, VW), BF16),
        scratch_shapes=[pltpu.VMEM((nb, GLA_HEADS, GLA_DV, GLA_DK), F32)],
        compiler_params=_cparams(2),
        name="gla",
    )(x3, x3, x3, x3, gd3, gate_up_pad, gate_b.reshape(1, KW), onorm.reshape(1, GLA_DV))
    return out.reshape(B * S, VW)


def _xattn_kernel(*refs, n_in):
    x_ref = refs[0]
    a_refs = refs[1:1 + n_in]
    w_refs = refs[1 + n_in:1 + 2 * n_in]
    g_ref, wq_ref, mk_ref, mv_ref, wo_ref, gm_ref, wr_ref, br_ref, o_ref, lg_ref = refs[1 + 2 * n_in:]
    tq = x_ref.shape[0]
    subs = [slice(r, r + XA_SUB_ROWS) for r in range(0, tq, XA_SUB_ROWS)]
    xs = [x_ref[sub, :] for sub in subs]
    for a_ref, w_ref in zip(a_refs, w_refs):
        xs = [x + jnp.dot(a_ref[sub, :], w_ref[...], preferred_element_type=F32) for x, sub in zip(xs, subs)]
    qs = [_dot(_rms(x, g_ref[...]), wq_ref[...]).astype(BF16) for x in xs]
    mk = mk_ref[...]
    mv = mv_ref[...]
    sls = [slice(hd * XA_HEAD_DIM, (hd + 1) * XA_HEAD_DIM) for hd in range(XA_HEADS)]
    scores = [[_dot_nt(q[:, sl], mk[:, sl]) for sl in sls] for q in qs]
    probs = []
    for sc in scores:
        ps = []
        for s in sc:
            s = s * (XA_HEAD_DIM ** -0.5)
            p = jnp.exp(s - jnp.max(s, axis=-1, keepdims=True))
            ps.append((p / jnp.sum(p, axis=-1, keepdims=True)).astype(BF16))
        probs.append(ps)
    os_ = [jnp.concatenate([_dot(p, mv[:, sl]) for p, sl in zip(ps, sls)], axis=1) for ps in probs]
    outs = [x + _dot(o, wo_ref[...]) for x, o in zip(xs, os_)]
    for out, sub in zip(outs, subs):
        o_ref[sub, :] = out
    for out, sub in zip(outs, subs):
        lg_ref[:, sub] = _dot_nt(wr_ref[...], _rms(out, gm_ref[...])) + br_ref[...]


XA_SUB_ROWS = 256


def _mix_proj_xattn(x, acts, weights, g, wq, mk, mv, wo, g_moe, wt_router, bt_router, B, S, tq=1024):
    D = x.shape[-1]
    assert S % tq == 0 and tq % XA_SUB_ROWS == 0
    M = mk.shape[0] // B
    XW = mk.shape[-1]
    n_in = len(acts)
    seq3 = lambda a: a.reshape(B, S, a.shape[-1])
    row_spec = lambda a: pl.BlockSpec((None, tq, a.shape[-1]), lambda b, n: (b, n, 0))
    const = lambda a: pl.BlockSpec(a.shape, lambda b, n: (0,) * a.ndim)
    out, logits = pl.pallas_call(
        functools.partial(_xattn_kernel, n_in=n_in),
        grid=(B, S // tq),
        in_specs=[row_spec(x)] + [row_spec(a) for a in acts] + [const(w) for w in weights]
                 + [pl.BlockSpec((1, D), lambda b, n: (0, 0)),
                    pl.BlockSpec((D, XW), lambda b, n: (0, 0)),
                    pl.BlockSpec((None, M, XW), lambda b, n: (b, 0, 0)),
                    pl.BlockSpec((None, M, XW), lambda b, n: (b, 0, 0)),
                    pl.BlockSpec((XW, D), lambda b, n: (0, 0)),
                    pl.BlockSpec((1, D), lambda b, n: (0, 0)),
                    pl.BlockSpec((LANES, D), lambda b, n: (0, 0)),
                    pl.BlockSpec((LANES, XA_SUB_ROWS), lambda b, n: (0, 0))],
        out_specs=[pl.BlockSpec((None, tq, D), lambda b, n: (b, n, 0)),
                   pl.BlockSpec((None, LANES, tq), lambda b, n: (b * (S // tq) + n, 0, 0))],
        out_shape=[jax.ShapeDtypeStruct((B, S, D), F32),
                   jax.ShapeDtypeStruct((B * S // tq, LANES, tq), F32)],
        compiler_params=_cparams(2),
        name="xattn",
    )(seq3(x), *[seq3(a) for a in acts], *weights, g.reshape(1, D), wq,
      mk.reshape(B, M, XW), mv.reshape(B, M, XW), wo, g_moe.reshape(1, D), wt_router, bt_router)
    return out.reshape(B * S, D), logits


ROUTER_ROWS = 40


def _router_kernel(lg_ref, info_ref, slot_ref, cnt_ref, carry_ref):
    i = pl.program_id(0)

    @pl.when(i == 0)
    def _():
        carry_ref[...] = jnp.zeros_like(carry_ref)

    logits = lg_ref[:ROUTER_ROWS, :]
    tm = logits.shape[1]
    row = lax.broadcasted_iota(I32, logits.shape, 0)
    big = jnp.int32(LANES)
    neg = -jnp.inf
    gl = jnp.where(row < MOE_GROUPS, logits, neg)
    gmax = jnp.max(gl, axis=0, keepdims=True)
    g_top = jnp.min(jnp.where(gl == gmax, row, big), axis=0, keepdims=True)
    p_group = 1.0 / jnp.sum(jnp.exp(gl - gmax), axis=0, keepdims=True)
    lo = MOE_GROUPS + MOE_EXPERTS_PER_GROUP * g_top
    in_group = jnp.where(row >= lo, jnp.where(row < lo + MOE_EXPERTS_PER_GROUP, 1, 0), 0) > 0
    el = jnp.where(in_group, logits, neg)
    emax = jnp.max(el, axis=0, keepdims=True)
    ee = jnp.exp(el - emax)
    prob = ee / jnp.sum(ee, axis=0, keepdims=True)
    prob = jnp.where(in_group, prob, -1.0)
    p1 = jnp.max(prob, axis=0, keepdims=True)
    i1 = jnp.min(jnp.where(prob == p1, row, big), axis=0, keepdims=True)
    rest = jnp.where(row == i1, -1.0, prob)
    p2 = jnp.max(rest, axis=0, keepdims=True)
    i2 = jnp.min(jnp.where(rest == p2, row, big), axis=0, keepdims=True)
    tot = p1 + p2
    g1 = p_group * p1 / tot
    g2 = p_group * p2 / tot
    oh = jnp.concatenate([jnp.where(row == i1, 1.0, 0.0), jnp.where(row == i2, 1.0, 0.0)], axis=0)
    tr = lax.broadcasted_iota(I32, (tm, tm), 0)
    tc = lax.broadcasted_iota(I32, (tm, tm), 1)
    pre = _dot(oh, jnp.where(tr < tc, 1.0, 0.0))
    tots = _dot(oh, jnp.ones((tm, LANES), F32))
    reps = tm // LANES
    carry = carry_ref[...]
    base1 = jnp.concatenate([carry] * reps, axis=1)
    base2 = jnp.concatenate([carry + tots[:ROUTER_ROWS]] * reps, axis=1)
    r1 = jnp.sum(oh[:ROUTER_ROWS] * (base1 + pre[:ROUTER_ROWS]), axis=0, keepdims=True)
    r2 = jnp.sum(oh[ROUTER_ROWS:] * (base2 + pre[ROUTER_ROWS:]), axis=0, keepdims=True)
    carry = carry + tots[:ROUTER_ROWS] + tots[ROUTER_ROWS:]
    carry_ref[...] = carry
    cnt_ref[...] = carry
    e1 = (i1 - MOE_GROUPS).astype(F32)
    e2 = (i2 - MOE_GROUPS).astype(F32)
    slot_rows = [e1, e2, r1, r2, g1, g2]
    rows8 = lax.broadcasted_iota(I32, (SUBLANES, tm), 0)
    slot = jnp.zeros((SUBLANES, tm), F32)
    for j, val in enumerate(slot_rows):
        slot = jnp.where(rows8 == j, val, slot)
    slot_ref[...] = slot
    wide = jnp.concatenate([slot, jnp.zeros((LANES - SUBLANES, tm), F32)], axis=0)
    info_ref[...] = jnp.transpose(wide)


def _router(logits, tm=256):
    n_row_tiles, _, tq = logits.shape
    assert tq % tm == 0 and tm % LANES == 0
    per = tq // tm
    T = n_row_tiles * tq
    NT = T // tm
    return pl.pallas_call(
        _router_kernel,
        grid=(NT,),
        in_specs=[pl.BlockSpec((None, LANES, tm), lambda i: (i // per, 0, i % per))],
        out_specs=[pl.BlockSpec((tm, LANES), lambda i: (i, 0)),
                   pl.BlockSpec((None, SUBLANES, tm), lambda i: (i, 0, 0)),
                   pl.BlockSpec((ROUTER_ROWS, LANES), lambda i: (0, 0))],
        out_shape=[jax.ShapeDtypeStruct((T, LANES), F32),
                   jax.ShapeDtypeStruct((NT, SUBLANES, tm), F32),
                   jax.ShapeDtypeStruct((ROUTER_ROWS, LANES), F32)],
        scratch_shapes=[pltpu.VMEM((ROUTER_ROWS, LANES), F32)],
        compiler_params=_cparams(1),
        name="router",
    )(logits)


def _row_bytes_wait(hbm, buf, sem):
    pltpu.make_async_copy(buf, hbm.at[pl.ds(0, buf.shape[0]), :], sem).wait()


def _to_row_tiles(ref, val):
    n = val.shape[0]
    for c in range(SUBLANES):
        ref[pl.ds(c, n, stride=SUBLANES), :] = val[:, c * LANES:(c + 1) * LANES]


def _from_row_tiles(ref):
    n = ref.shape[0] // SUBLANES
    return jnp.concatenate([ref[pl.ds(c, n, stride=SUBLANES), :] for c in range(SUBLANES)], axis=1)


def _moe_dispatch_kernel(pends_ref, cnt_ref, dest_ref, x_ref, g_ref, hs_hbm, hbuf, zbuf, sems, zsem, *, td):
    i = pl.program_id(0)
    nt = pl.num_programs(0)
    slot = lax.rem(i, 2)

    @pl.when(i == 0)
    def _():
        zbuf[...] = jnp.zeros_like(zbuf)
        for e in range(MOE_EXPERTS):
            @pl.when(cnt_ref[e] > 0)
            def _():
                start = pl.multiple_of((pends_ref[e] - MOE_BLOCK) * SUBLANES, MOE_BLOCK)
                pltpu.make_async_copy(zbuf, hs_hbm.at[pl.ds(start, MOE_BLOCK * SUBLANES), :], zsem).start()
        for e in range(MOE_EXPERTS):
            @pl.when(cnt_ref[e] > 0)
            def _():
                pltpu.make_async_copy(zbuf, hs_hbm.at[pl.ds(0, MOE_BLOCK * SUBLANES), :], zsem).wait()

        first_unused = pends_ref[MOE_EXPERTS - 1] // MOE_BLOCK
        n_blocks = hs_hbm.shape[0] // (MOE_BLOCK * SUBLANES)

        def zero_start(blk, carry):
            start = pl.multiple_of(blk * (MOE_BLOCK * SUBLANES), MOE_BLOCK)
            pltpu.make_async_copy(zbuf, hs_hbm.at[pl.ds(start, MOE_BLOCK * SUBLANES), :], zsem).start()
            return carry

        def zero_wait(blk, carry):
            pltpu.make_async_copy(zbuf, hs_hbm.at[pl.ds(0, MOE_BLOCK * SUBLANES), :], zsem).wait()
            return carry

        lax.fori_loop(first_unused, n_blocks, zero_start, 0)
        lax.fori_loop(first_unused, n_blocks, zero_wait, 0)

    hb = hbuf.at[slot]
    _to_row_tiles(hb, _rms(x_ref[...], g_ref[...]))
    for j in range(td):
        for c in range(2):
            row = pl.multiple_of(dest_ref[0, c * td + j] * SUBLANES, SUBLANES)
            pltpu.make_async_copy(hb.at[pl.ds(j * SUBLANES, SUBLANES), :],
                                  hs_hbm.at[pl.ds(row, SUBLANES), :],
                                  sems.at[slot]).start(priority=c)

    @pl.when(i > 0)
    def _():
        other = hbuf.at[1 - slot]
        _row_bytes_wait(hs_hbm, other, sems.at[1 - slot])
        _row_bytes_wait(hs_hbm, other, sems.at[1 - slot])

    @pl.when(i == nt - 1)
    def _():
        _row_bytes_wait(hs_hbm, hb, sems.at[slot])
        _row_bytes_wait(hs_hbm, hb, sems.at[slot])


def _moe_dispatch(x, g, pends, counts, dest3, P, td):
    T, D = x.shape
    assert D == ROW_TILE and T % td == 0
    grid_spec = pltpu.PrefetchScalarGridSpec(
        num_scalar_prefetch=2,
        grid=(T // td,),
        in_specs=[pl.BlockSpec((None, 1, 2 * td), lambda i, pe, cn: (i, 0, 0), memory_space=pltpu.SMEM),
                  pl.BlockSpec((td, D), lambda i, pe, cn: (i, 0)),
                  pl.BlockSpec((1, D), lambda i, pe, cn: (0, 0))],
        out_specs=pl.BlockSpec(memory_space=pl.ANY),
        scratch_shapes=[pltpu.VMEM((2, td * SUBLANES, LANES), F32),
                        pltpu.VMEM((MOE_BLOCK * SUBLANES, LANES), F32),
                        pltpu.SemaphoreType.DMA((2,)),
                        pltpu.SemaphoreType.DMA(())],
    )
    return pl.pallas_call(
        functools.partial(_moe_dispatch_kernel, td=td),
        grid_spec=grid_spec,
        out_shape=jax.ShapeDtypeStruct((P * SUBLANES, LANES), F32),
        compiler_params=_cparams(1),
        name="moe_dispatch",
    )(pends, counts, dest3, x, g.reshape(1, D))


def _moe_expert_kernel(be_ref, nu_ref, hs_ref, w1_ref, w3_ref, w2_ref, o_ref, w1b, w3b, w2b):
    i = pl.program_id(0)
    used = i < nu_ref[0]
    changed = jnp.logical_or(i == 0, be_ref[i] != be_ref[jnp.maximum(i - 1, 0)])

    @pl.when(jnp.logical_and(used, changed))
    def _():
        w1b[...] = w1_ref[...].astype(BF16)
        w3b[...] = w3_ref[...].astype(BF16)
        w2b[...] = w2_ref[...].astype(BF16)

    @pl.when(used)
    def _():
        xe = _from_row_tiles(hs_ref).astype(BF16)
        ff = w1b.shape[1]
        halves = [slice(0, ff // 2), slice(ff // 2, ff)]
        ups = [(jnp.dot(xe, w1b[:, sl], preferred_element_type=F32),
                jnp.dot(xe, w3b[:, sl], preferred_element_type=F32)) for sl in halves]
        act = [(a * _sigmoid(a) * b).astype(BF16) for a, b in ups]
        y = sum(jnp.dot(a, w2b[sl, :], preferred_element_type=F32) for a, sl in zip(act, halves))
        _to_row_tiles(o_ref, y)

    @pl.when(jnp.logical_not(used))
    def _():
        o_ref[...] = jnp.zeros_like(o_ref)


def _moe_experts(hs, block_e, n_used, w1, w3, w2, layer):
    P = hs.shape[0] // SUBLANES
    D = ROW_TILE
    FF = w1.shape[-1]
    NB = P // MOE_BLOCK
    last = lambda i, nu: jnp.minimum(i, nu[0] - 1)
    grid_spec = pltpu.PrefetchScalarGridSpec(
        num_scalar_prefetch=2,
        grid=(NB,),
        in_specs=[pl.BlockSpec((MOE_BLOCK * SUBLANES, LANES), lambda i, be, nu: (last(i, nu), 0)),
                  pl.BlockSpec((None, None, D, FF), lambda i, be, nu: (layer, be[last(i, nu)], 0, 0)),
                  pl.BlockSpec((None, None, D, FF), lambda i, be, nu: (layer, be[last(i, nu)], 0, 0)),
                  pl.BlockSpec((None, None, FF, D), lambda i, be, nu: (layer, be[last(i, nu)], 0, 0))],
        out_specs=pl.BlockSpec((MOE_BLOCK * SUBLANES, LANES), lambda i, be, nu: (i, 0)),
        scratch_shapes=[pltpu.VMEM((D, FF), BF16),
                        pltpu.VMEM((D, FF), BF16),
                        pltpu.VMEM((FF, D), BF16)],
    )
    return pl.pallas_call(
        _moe_expert_kernel,
        grid_spec=grid_spec,
        out_shape=jax.ShapeDtypeStruct((P * SUBLANES, LANES), F32),
        compiler_params=_cparams(1),
        name="moe_experts",
    )(block_e, n_used, hs, w1, w3, w2)


def _gather_rows(src_hbm, idx_ref, dst_ref, sem, first, last):
    for r in range(first, last):
        row = pl.multiple_of(idx_ref[0, r] * SUBLANES, SUBLANES)
        pltpu.make_async_copy(src_hbm.at[pl.ds(row, SUBLANES), :],
                              dst_ref.at[pl.ds(r * SUBLANES, SUBLANES), :], sem).start(priority=r % 2)


def _moe_combine_kernel(pos_ref, posn_ref, x_ref, info_ref, yb_hbm, g_ref, *rest, tc, final_norm, splits):
    if splits:
        w_ref, o_ref = rest[0], rest[1]
        p_refs = rest[2:2 + len(splits)]
        ybuf, sems = rest[2 + len(splits):]
    else:
        o_ref, ybuf, sems = rest
    i = pl.program_id(0)
    nb = pl.num_programs(0)
    slot = lax.rem(i, 2)

    @pl.when(i == 0)
    def _():
        def issue(r, carry):
            src = pl.multiple_of(pos_ref[0, r] * SUBLANES, SUBLANES)
            dst = pl.multiple_of(r * SUBLANES, SUBLANES)
            pltpu.make_async_copy(yb_hbm.at[pl.ds(src, SUBLANES), :],
                                  ybuf.at[0, pl.ds(dst, SUBLANES), :], sems.at[0]).start()
            return carry
        lax.fori_loop(0, 2 * tc, issue, 0)

    def wait_tile(s_):
        pltpu.make_async_copy(yb_hbm.at[pl.ds(0, 2 * tc * SUBLANES), :], ybuf.at[s_], sems.at[s_]).wait()

    if not splits:
        @pl.when(i + 1 < nb)
        def _():
            _gather_rows(yb_hbm, posn_ref, ybuf.at[1 - slot], sems.at[1 - slot], 0, 2 * tc)

    wait_tile(slot)
    info = info_ref[...]
    yb = ybuf.at[slot]
    y0 = _from_row_tiles(yb.at[pl.ds(0, tc * SUBLANES), :])
    y1 = _from_row_tiles(yb.at[pl.ds(tc * SUBLANES, tc * SUBLANES), :])
    out = x_ref[...] + (y0 * info[:, 4:5] + y1 * info[:, 5:6])
    if final_norm:
        out = _rms(out, g_ref[...])
    o_ref[...] = out
    if splits:
        h = _rms(out, g_ref[...]).astype(BF16)
        chunks = []
        for p_ref, n in zip(p_refs, splits):
            for c0 in range(0, n, PROJ_CHUNK):
                chunks.append((p_ref, c0, min(PROJ_CHUNK, n - c0)))
        per = -(-2 * tc // len(chunks))
        off = 0
        for ci, (p_ref, c0, width) in enumerate(chunks):
            val = jnp.dot(h, w_ref[:, off:off + width], preferred_element_type=F32)
            _gather_rows(yb_hbm, posn_ref, ybuf.at[1 - slot], sems.at[1 - slot],
                         min(ci * per, 2 * tc), min((ci + 1) * per, 2 * tc))
            p_ref[:, c0:c0 + width] = val.astype(p_ref.dtype)
            off += width

        @pl.when(i == nb - 1)
        def _():
            wait_tile(1 - slot)


def _moe_combine(x, info, dest3, yb, g, final_norm, tc, next_proj=None):
    T, D = x.shape
    assert D == ROW_TILE and T % tc == 0
    NT = T // tc
    in_specs = [pl.BlockSpec((None, 1, 2 * tc), lambda i: (i, 0, 0), memory_space=pltpu.SMEM),
                pl.BlockSpec((None, 1, 2 * tc), lambda i: (jnp.minimum(i + 1, NT - 1), 0, 0),
                             memory_space=pltpu.SMEM),
                pl.BlockSpec((tc, D), lambda i: (i, 0)),
                pl.BlockSpec((tc, LANES), lambda i: (i, 0)),
                pl.BlockSpec(memory_space=pl.ANY),
                pl.BlockSpec((1, D), lambda i: (0, 0))]
    out_specs = [pl.BlockSpec((tc, D), lambda i: (i, 0))]
    out_shape = [jax.ShapeDtypeStruct((T, D), F32)]
    args = [dest3, dest3, x, info, yb, g.reshape(1, D)]
    splits = ()
    if next_proj is not None:
        w, splits, out_dtypes = next_proj
        assert not final_norm and sum(splits) == w.shape[1]
        in_specs.append(pl.BlockSpec(w.shape, lambda i: (0, 0)))
        args.append(w)
        out_specs += [pl.BlockSpec((tc, n), lambda i: (i, 0)) for n in splits]
        out_shape += [jax.ShapeDtypeStruct((T, n), dt) for n, dt in zip(splits, out_dtypes)]
    outs = pl.pallas_call(
        functools.partial(_moe_combine_kernel, tc=tc, final_norm=final_norm, splits=tuple(splits)),
        grid=(NT,),
        in_specs=in_specs,
        out_specs=out_specs,
        out_shape=out_shape,
        scratch_shapes=[pltpu.VMEM((2, 2 * tc * SUBLANES, LANES), F32), pltpu.SemaphoreType.DMA((2,))],
        compiler_params=_cparams(1),
        name="moe_combine",
    )(*args)
    return outs[0], tuple(outs[1:])


MOE_TILE = 512
MOE_COMBINE_TILE = 256
MOE_LAST_COMBINE_TILE = 256
PROJ_CHUNK = 256
ROUTER_TILE = 256


def _tile_slots(dest, tile):
    n_tiles = dest.shape[0] * dest.shape[2] // tile
    return jnp.concatenate([dest[:, 0, :].reshape(n_tiles, 1, tile), dest[:, 1, :].reshape(n_tiles, 1, tile)], axis=2)


def _router_params(w_group, b_group, w_expert, b_expert, lanes_out):
    D = w_group.shape[0]
    n_log = MOE_GROUPS + MOE_EXPERTS
    wt = jnp.zeros((LANES, D), F32).at[:MOE_GROUPS].set(w_group.T).at[MOE_GROUPS:n_log].set(w_expert.T)
    bt = jnp.zeros((LANES,), F32).at[:MOE_GROUPS].set(b_group).at[MOE_GROUPS:n_log].set(b_expert)
    return wt.astype(BF16), jnp.broadcast_to(bt[:, None], (LANES, lanes_out))


def _moe_layer(x, g, logits, w1, w3, w2, layer, g_out, final_norm, next_proj):
    T, D = x.shape
    n_log = MOE_GROUPS + MOE_EXPERTS
    info, slot, cnt = _router(logits, ROUTER_TILE)
    P = 2 * T + MOE_EXPERTS * MOE_BLOCK
    NB = P // MOE_BLOCK
    counts = cnt[MOE_GROUPS:n_log, 0].astype(I32)
    padded = (counts + MOE_BLOCK - 1) // MOE_BLOCK * MOE_BLOCK
    pends = jnp.cumsum(padded).astype(I32)
    pstarts = pends - padded
    block_start = jnp.arange(NB, dtype=I32) * MOE_BLOCK
    block_e = jnp.minimum(jnp.sum((pends[None, :] <= block_start[:, None]).astype(I32), axis=1),
                          MOE_EXPERTS - 1).astype(I32)
    n_used = (pends[-1:] // MOE_BLOCK).astype(I32)
    eid = slot[:, 0:2, :].astype(I32)
    expert_ids = jnp.arange(MOE_EXPERTS, dtype=I32)
    seg_start = jnp.sum(jnp.where(eid[..., None] == expert_ids, pstarts, 0), axis=-1)
    dest = seg_start + slot[:, 2:4, :].astype(I32)
    hs = _moe_dispatch(x, g, pends, counts, _tile_slots(dest, MOE_TILE), P, MOE_TILE)
    yb = _moe_experts(hs, block_e, n_used, w1, w3, w2, layer)
    tc = MOE_LAST_COMBINE_TILE if next_proj is None else MOE_COMBINE_TILE
    return _moe_combine(x, info, _tile_slots(dest, tc), yb, g_out, final_norm, tc, next_proj)


def kernel(x, mem, norm_mix, norm_xattn, norm_moe, norm_final, ev_w_in, ev_sinks, ev_mu, ev_w0, ev_w2, ev_a0, ev_a2, ev_g2, ev_k_k, ev_k_a, ev_r_k, ev_lnx_w, ev_lnx_b, ev_w_out, od_w_in, od_gate_up, od_gate_b, od_onorm, od_w_out, mem_norm, mem_wk, mem_wv, xa_wq, xa_wo, moe_w_group, moe_b_group, moe_w_expert, moe_b_expert, moe_w1, moe_w3, moe_w2):
    B, S, D = x.shape
    M = mem.shape[1]
    T = B * S
    depth = norm_mix.shape[0]
    xf = x.reshape(T, D)

    XW = XA_HEADS * XA_HEAD_DIM
    w_kv = jnp.concatenate([mem_wk, mem_wv], axis=1).astype(BF16)
    mk, mv = _norm_matmul(mem.reshape(B * M, D), mem_norm, w_kv, (XW, XW), (BF16, BF16))

    KW = GLA_HEADS * GLA_DK
    VW = GLA_HEADS * GLA_DV
    swa_cols = SWA_Q_HEADS * HEAD_DIM + 2 * (SWA_Q_HEADS // SWA_GROUP) * HEAD_DIM

    def in_proj(layer):
        i = layer // 2
        if layer % 2 == 0:
            return ev_w_in[i].astype(BF16), (swa_cols, ev_w_in.shape[-1] - swa_cols), (F32, F32)
        R = od_gate_up.shape[1]
        w = od_w_in[i]
        w_re = jnp.concatenate([w[:, :2 * KW + VW], w[:, 2 * KW + VW + R:],
                                w[:, 2 * KW + VW:2 * KW + VW + R],
                                jnp.zeros((D, LANES - R), F32)], axis=1).astype(BF16)
        return w_re, (2 * KW + 2 * VW, LANES), (F32, F32)

    w_first, _, _ = in_proj(0)
    qkv0, rw0 = _in_proj_rwkv(xf, norm_mix[0], w_first, swa_cols, ev_mu[0], ev_w0[0], ev_w2[0], ev_a0[0], ev_a2[0],
                              ev_g2[0], ev_k_k[0], ev_k_a[0], S)
    proj = None
    for layer in range(depth):
        i = layer // 2
        if layer % 2 == 0:
            if layer == 0:
                qkv, rw = qkv0, [t.reshape(B, S, RWKV_WIDTH) for t in rw0]
            else:
                qkv, p_rw = proj
                rw = _rwkv_prep(p_rw, ev_mu[i], ev_w0[i], ev_w2[i], ev_a0[i], ev_a2[i],
                                ev_g2[i], ev_k_k[i], ev_k_a[i], B, S)
            o_a = _swa(qkv, ev_sinks[i], B, S)
            o_b = _rwkv_scan(*rw, ev_r_k[i].reshape(-1), ev_lnx_w[i], ev_lnx_b[i])
            w_out = ev_w_out[i].astype(BF16)
            qw = o_a.shape[-1]
            mix_acts, mix_ws = [o_a, o_b], [w_out[:qw], w_out[qw:]]
        else:
            qkvo, gd = proj
            R = od_gate_up.shape[1]
            gup = jnp.zeros((LANES, KW), F32).at[:R].set(od_gate_up[i]).astype(BF16)
            o = _gla(qkvo, gd, gup, od_gate_b[i], od_onorm[i], B, S)
            mix_acts, mix_ws = [o], [od_w_out[i].astype(BF16)]
        wt_router, bt_router = _router_params(moe_w_group[layer], moe_b_group[layer], moe_w_expert[layer],
                                              moe_b_expert[layer], XA_SUB_ROWS)
        xf, logits = _mix_proj_xattn(xf, mix_acts, mix_ws, norm_xattn[layer], xa_wq[layer].astype(BF16), mk, mv,
                                     xa_wo[layer].astype(BF16), norm_moe[layer], wt_router, bt_router, B, S)
        last = layer == depth - 1
        g_out = norm_final if last else norm_mix[layer + 1]
        xf, proj = _moe_layer(xf, norm_moe[layer], logits, moe_w1, moe_w3, moe_w2, layer,
                              g_out, last, None if last else in_proj(layer + 1))
    return xf.reshape(B, S, D)
```

```python
import functools

import jax
import jax.numpy as jnp
from jax import lax
from jax.experimental import pallas as pl
from jax.experimental.pallas import tpu as pltpu

F32 = jnp.float32
BF16 = jnp.bfloat16
I32 = jnp.int32

EPS = 1e-6
HEAD_DIM = 64
SWA_WINDOW = 128
SWA_Q_HEADS = 8
SWA_GROUP = 4
RWKV_HEADS = 8
RWKV_WIDTH = 512
RWKV_LN_EPS = 64e-5
RWKV_CHUNK = 64
GLA_HEADS = 4
GLA_DK = 128
GLA_DV = 256
GLA_CHUNK = 64
GLA_GATE_NORM = 16.0
XA_HEADS = 4
XA_HEAD_DIM = 128
MOE_GROUPS = 4
MOE_EXPERTS_PER_GROUP = 8
MOE_EXPERTS = 32
MOE_BLOCK = 512
LANES = 128
SUBLANES = 8
ROW_TILE = SUBLANES * LANES

VMEM_LIMIT_BYTES = 48 * 1024 * 1024


def _cparams(n_axes):
    return pltpu.CompilerParams(dimension_semantics=("arbitrary",) * n_axes,
                                vmem_limit_bytes=VMEM_LIMIT_BYTES)


def _dot(a, b):
    return jnp.dot(a.astype(BF16), b.astype(BF16), preferred_element_type=F32)


def _dot_nt(a, b):
    return lax.dot_general(a.astype(BF16), b.astype(BF16), (((1,), (1,)), ((), ())),
                           preferred_element_type=F32)


def _dot_tn(a, b):
    return lax.dot_general(a.astype(BF16), b.astype(BF16), (((0,), (0,)), ((), ())),
                           preferred_element_type=F32)


def _tri_cumsum(tri, x):
    hi = x.astype(BF16)
    rest = x - hi.astype(F32)
    mid = rest.astype(BF16)
    lo = (rest - mid.astype(F32)).astype(BF16)
    w = x.shape[1]
    sums = jnp.dot(tri.astype(BF16), jnp.concatenate([hi, mid, lo], axis=1), preferred_element_type=F32)
    return sums[:, :w] + sums[:, w:2 * w] + sums[:, 2 * w:]


def _rms(x, g):
    ms = jnp.mean(x * x, axis=-1, keepdims=True)
    return x * lax.rsqrt(ms + EPS) * g


def _sigmoid(x):
    return 1.0 / (1.0 + jnp.exp(-x))


def _softplus(x):
    return jnp.maximum(x, 0.0) + jnp.log(1.0 + jnp.exp(-jnp.abs(x)))


def _norm_matmul_kernel(x_ref, g_ref, w_ref, *o_refs, splits):
    h = _rms(x_ref[...], g_ref[...]).astype(BF16)
    off = 0
    for o_ref, n in zip(o_refs, splits):
        o_ref[...] = jnp.dot(h, w_ref[:, off:off + n], preferred_element_type=F32).astype(o_ref.dtype)
        off += n


def _norm_matmul(x, g, w, splits, out_dtypes, tm=256):
    T, D = x.shape
    N = w.shape[1]
    assert sum(splits) == N and T % tm == 0
    return pl.pallas_call(
        functools.partial(_norm_matmul_kernel, splits=tuple(splits)),
        grid=(T // tm,),
        in_specs=[pl.BlockSpec((tm, D), lambda i: (i, 0)),
                  pl.BlockSpec((1, D), lambda i: (0, 0)),
                  pl.BlockSpec((D, N), lambda i: (0, 0))],
        out_specs=[pl.BlockSpec((tm, n), lambda i: (i, 0)) for n in splits],
        out_shape=[jax.ShapeDtypeStruct((T, n), dt) for n, dt in zip(splits, out_dtypes)],
        compiler_params=_cparams(1),
        name="norm_matmul",
    )(x, g.reshape(1, D), w)


def _swa_kernel(sinks_ref, q_ref, kp_ref, kc_ref, vp_ref, vc_ref, o_ref):
    n = pl.program_id(1)
    W = SWA_WINDOW
    NB = q_ref.shape[0]
    qpos = lax.broadcasted_iota(I32, (W, 2 * W), 0) + W
    kpos = lax.broadcasted_iota(I32, (W, 2 * W), 1)
    rel = qpos - kpos
    in_window = jnp.where(rel >= 0, jnp.where(rel < W, 1, 0), 0)
    has_prev = jnp.where(n > 0, 1, 0)
    valid = (in_window * jnp.where(kpos >= W, 1, has_prev)) > 0
    n_groups = SWA_Q_HEADS // SWA_GROUP
    streams = [(bi, g) for bi in range(NB) for g in range(n_groups)]
    qb = [q_ref[bi].astype(BF16) for bi in range(NB)]
    kb = [jnp.concatenate([kp_ref[bi], kc_ref[bi]], axis=0).astype(BF16) for bi in range(NB)]
    vb = [jnp.concatenate([vp_ref[bi], vc_ref[bi]], axis=0).astype(BF16) for bi in range(NB)]
    gs = lambda g: slice(g * HEAD_DIM, (g + 1) * HEAD_DIM)
    scores = []
    for bi, g in streams:
        qg = jnp.concatenate([qb[bi][:, h * HEAD_DIM:(h + 1) * HEAD_DIM]
                              for h in range(g * SWA_GROUP, (g + 1) * SWA_GROUP)], axis=0)
        scores.append(_dot_nt(qg, kb[bi][:, gs(g)]))
    probs = []
    for i, (bi, g) in enumerate(streams):
        pieces = []
        for j in range(SWA_GROUP):
            s = jnp.where(valid, scores[i][j * W:(j + 1) * W] * (HEAD_DIM ** -0.5), -jnp.inf)
            sink = sinks_ref[g * SWA_GROUP + j]
            m = jnp.maximum(jnp.max(s, axis=-1, keepdims=True), sink)
            p = jnp.exp(s - m)
            den = jnp.sum(p, axis=-1, keepdims=True) + jnp.exp(sink - m)
            pieces.append((p / den).astype(BF16))
        probs.append(jnp.concatenate(pieces, axis=0))
    ogs = [_dot(probs[i], vb[bi][:, gs(g)]) for i, (bi, g) in enumerate(streams)]
    for bi in range(NB):
        outs = []
        for g in range(n_groups):
            og = ogs[bi * n_groups + g]
            outs += [og[j * W:(j + 1) * W] for j in range(SWA_GROUP)]
        o_ref[bi] = jnp.concatenate(outs, axis=1).astype(o_ref.dtype)


SWA_BATCH_ROWS = 2


def _swa(qkv, sinks, B, S):
    W = SWA_WINDOW
    assert S % W == 0
    qkv3 = qkv.reshape(B, S, qkv.shape[-1])
    qw = SWA_Q_HEADS * HEAD_DIM
    kw = qw // SWA_GROUP
    kcol = qw // kw
    nb = SWA_BATCH_ROWS if B % SWA_BATCH_ROWS == 0 else 1
    out = pl.pallas_call(
        _swa_kernel,
        grid=(B // nb, S // W),
        in_specs=[pl.BlockSpec(memory_space=pltpu.SMEM),
                  pl.BlockSpec((nb, W, qw), lambda b, n: (b, n, 0)),
                  pl.BlockSpec((nb, W, kw), lambda b, n: (b, jnp.maximum(n - 1, 0), kcol)),
                  pl.BlockSpec((nb, W, kw), lambda b, n: (b, n, kcol)),
                  pl.BlockSpec((nb, W, kw), lambda b, n: (b, jnp.maximum(n - 1, 0), kcol + 1)),
                  pl.BlockSpec((nb, W, kw), lambda b, n: (b, n, kcol + 1))],
        out_specs=pl.BlockSpec((nb, W, qw), lambda b, n: (b, n, 0)),
        out_shape=jax.ShapeDtypeStruct((B, S, qw), BF16),
        compiler_params=_cparams(2),
        name="swa",
    )(sinks, qkv3, qkv3, qkv3, qkv3, qkv3)
    return out.reshape(B * S, qw)


def _rwkv_prep_math(p, last, mu_ref, w0_ref, w2_ref, a0_ref, a2_ref, g2_ref, kk_ref, ka_ref, outs):
    r_out, lw_out, k_out, v_out, a_out, b_out, g_out = outs
    C = RWKV_WIDTH
    row = lax.broadcasted_iota(I32, p.shape, 0)
    p_prev = jnp.where(row == 0, last, pltpu.roll(p, 1, axis=0))
    p = p + (p_prev - p) * mu_ref[...]
    r = p[:, :C]
    k = p[:, C:2 * C]
    v = p[:, 2 * C:3 * C]
    xw = p[:, 3 * C:3 * C + 64]
    xa = p[:, 3 * C + 64:3 * C + 128]
    xg = p[:, 3 * C + 128:]
    w = -_softplus(-(w0_ref[...] + _dot(jnp.tanh(xw), w2_ref[...]))) - 0.5
    lw = -jnp.exp(w)
    a = _sigmoid(a0_ref[...] + _dot(xa, a2_ref[...]))
    g = _dot(_sigmoid(xg), g2_ref[...])
    kk = k * kk_ref[...]
    pieces = []
    for h in range(RWKV_HEADS):
        kh = kk[:, h * HEAD_DIM:(h + 1) * HEAD_DIM]
        nrm = jnp.sqrt(jnp.sum(kh * kh, axis=-1, keepdims=True))
        pieces.append(kh / jnp.maximum(nrm, 1e-12))
    kk = jnp.concatenate(pieces, axis=1)
    r_out[...] = r
    lw_out[...] = lw
    k_out[...] = k * (1.0 + (a - 1.0) * ka_ref[...])
    v_out[...] = v
    a_out[...] = -kk
    b_out[...] = kk * a
    g_out[...] = g


def _rwkv_prep_kernel(p_ref, pprev_ref, *refs):
    n = pl.program_id(1)
    last = jnp.where(n > 0, pprev_ref[SUBLANES - 1:SUBLANES, :], 0.0)
    _rwkv_prep_math(p_ref[...], last, *refs[:8], refs[8:])


def _rwkv_params(mu, w0, w2, a0, a2, g2, k_k, k_a):
    row = lambda t: t.reshape(1, -1)
    return [row(mu), row(w0), w2.astype(BF16), row(a0), a2.astype(BF16), g2.astype(BF16), row(k_k), row(k_a)]


def _rwkv_prep(p, mu, w0, w2, a0, a2, g2, k_k, k_a, B, S, tt=256):
    assert S % tt == 0
    C = RWKV_WIDTH
    PW = p.shape[-1]
    p3 = p.reshape(B, S, PW)
    full = lambda arr: pl.BlockSpec(arr.shape, lambda b, n: (0,) * arr.ndim)
    params = _rwkv_params(mu, w0, w2, a0, a2, g2, k_k, k_a)
    outs = pl.pallas_call(
        _rwkv_prep_kernel,
        grid=(B, S // tt),
        in_specs=[pl.BlockSpec((None, tt, PW), lambda b, n: (b, n, 0)),
                  pl.BlockSpec((None, SUBLANES, PW),
                               lambda b, n: (b, jnp.maximum(n * (tt // SUBLANES) - 1, 0), 0))]
                 + [full(t) for t in params],
        out_specs=[pl.BlockSpec((None, tt, C), lambda b, n: (b, n, 0))] * 7,
        out_shape=[jax.ShapeDtypeStruct((B, S, C), F32)] * 7,
        compiler_params=_cparams(2),
        name="rwkv_prep",
    )(p3, p3, *params)
    return outs


def _in_proj_rwkv_kernel(x_ref, g_ref, w_ref, *refs, swa_cols, tiles_per_seq):
    params = refs[:8]
    qkv_out = refs[8]
    outs = refs[9:16]
    p_buf, last_buf = refs[16:]
    i = pl.program_id(0)

    @pl.when(i == 0)
    def _():
        p_buf[...] = jnp.zeros_like(p_buf)
        last_buf[...] = jnp.zeros_like(last_buf)

    j = i - 1
    p_prev_tile = p_buf[lax.rem(i + 1, 2)]
    tm = p_prev_tile.shape[0]
    last = jnp.where(lax.rem(j, tiles_per_seq) == 0, 0.0, last_buf[...])
    _rwkv_prep_math(p_prev_tile, last, *params, outs)
    last_buf[...] = p_prev_tile[tm - 1:tm, :]
    h = _rms(x_ref[...], g_ref[...]).astype(BF16)
    qkv_out[...] = jnp.dot(h, w_ref[:, :swa_cols], preferred_element_type=F32)
    p_buf[lax.rem(i, 2)] = jnp.dot(h, w_ref[:, swa_cols:], preferred_element_type=F32)


def _in_proj_rwkv(x, g, w, swa_cols, mu, w0, w2, a0, a2, g2, k_k, k_a, S, tm=256):
    T, D = x.shape
    assert S % tm == 0 and T % S == 0
    N = w.shape[1]
    C = RWKV_WIDTH
    NT = T // tm
    params = _rwkv_params(mu, w0, w2, a0, a2, g2, k_k, k_a)
    const = lambda arr: pl.BlockSpec(arr.shape, lambda i: (0,) * arr.ndim)
    cur = lambda i: (jnp.minimum(i, NT - 1), 0)
    prev = lambda i: (jnp.maximum(i - 1, 0), 0)
    outs = pl.pallas_call(
        functools.partial(_in_proj_rwkv_kernel, swa_cols=swa_cols, tiles_per_seq=S // tm),
        grid=(NT + 1,),
        in_specs=[pl.BlockSpec((tm, D), cur), pl.BlockSpec((1, D), lambda i: (0, 0)), const(w)]
                 + [const(t) for t in params],
        out_specs=[pl.BlockSpec((tm, swa_cols), cur)] + [pl.BlockSpec((tm, C), prev)] * 7,
        out_shape=[jax.ShapeDtypeStruct((T, swa_cols), F32)] + [jax.ShapeDtypeStruct((T, C), F32)] * 7,
        scratch_shapes=[pltpu.VMEM((2, tm, N - swa_cols), F32), pltpu.VMEM((1, N - swa_cols), F32)],
        compiler_params=_cparams(1),
        name="in_proj_rwkv",
    )(x, g.reshape(1, D), w, *params)
    return outs[0], outs[1:]


def _pair_blockdiag(x):
    lane = lax.broadcasted_iota(I32, x.shape, 1)
    zero = jnp.zeros_like(x)
    return jnp.concatenate([jnp.where(lane < HEAD_DIM, x, zero), jnp.where(lane >= HEAD_DIM, x, zero)], axis=0)


def _rwkv_scan_kernel(r_ref, lw_ref, k_ref, v_ref, a_ref, b_ref, g_ref, rk_ref, lnw_ref, lnb_ref,
                      o_ref, s_ref):
    c = pl.program_id(1)

    @pl.when(c == 0)
    def _():
        s_ref[...] = jnp.zeros_like(s_ref)

    C = RWKV_CHUNK
    NB = r_ref.shape[0]
    NP = RWKV_HEADS // 2
    PW = 2 * HEAD_DIM
    row = lax.broadcasted_iota(I32, (C, C), 0)
    col = lax.broadcasted_iota(I32, (C, C), 1)
    tri = jnp.where(row >= col, 1.0, 0.0).astype(F32)
    rowp = lax.broadcasted_iota(I32, (C, PW), 0)
    colp = lax.broadcasted_iota(I32, (C, PW), 1)
    colp = jnp.where(colp >= HEAD_DIM, colp - HEAD_DIM, colp)
    lower_p = rowp >= colp
    strict_p = rowp > colp
    rows = lax.broadcasted_iota(I32, (PW, PW), 0)
    cols = lax.broadcasted_iota(I32, (PW, PW), 1)
    same_head = jnp.where(rows >= HEAD_DIM, 1, 0) == jnp.where(cols >= HEAD_DIM, 1, 0)
    first = lax.broadcasted_iota(I32, (C, PW), 1) < HEAD_DIM

    streams = [(bi, p) for bi in range(NB) for p in range(NP)]
    pre = []
    for bi in range(NB):
        lw = lw_ref[bi]
        cum = _tri_cumsum(tri, lw)
        cum_last = cum[C - 1:C, :]
        r = r_ref[bi]
        k = k_ref[bi]
        v = v_ref[bi]
        a = a_ref[bi]
        b = b_ref[bi]
        e_neg = jnp.exp(-cum)
        e_rem = jnp.exp(cum_last - cum)
        pre.append(dict(
            r_t=(r * jnp.exp(cum)).astype(BF16), a_t=(a * jnp.exp(cum - lw)).astype(BF16),
            b_t=(b * e_neg).astype(BF16), k_t=(k * e_neg).astype(BF16),
            b_d=(b * e_rem).astype(BF16), k_d=(k * e_rem).astype(BF16),
            v_b=v.astype(BF16), v=v, e_last=jnp.exp(cum_last), rkk=r * k * rk_ref[...], g=g_ref[bi]))

    def lanes(p):
        return slice(p * PW, (p + 1) * PW)

    ar = [jnp.concatenate([pre[bi]['a_t'][:, lanes(p)], pre[bi]['r_t'][:, lanes(p)]], axis=0) for bi, p in streams]
    s0 = [s_ref[bi, p] for bi, p in streams]
    big = [_dot_nt(ar[i], jnp.concatenate([_pair_blockdiag(pre[bi]['b_t'][:, lanes(p)]),
                                           _pair_blockdiag(pre[bi]['k_t'][:, lanes(p)]),
                                           s0[i].astype(BF16)], axis=0))
           for i, (bi, p) in enumerate(streams)]
    m_b = [t[:, :PW] for t in big]
    m_k = [t[:, PW:2 * PW] for t in big]
    ars = [t[:, 2 * PW:] for t in big]
    v_p = [pre[bi]['v_b'][:, lanes(p)] for bi, p in streams]
    v_bd = [_pair_blockdiag(vp) for vp in v_p]
    x = [ars[i][:C] + _dot(jnp.where(strict_p, m_k[i][:C], 0.0), v_bd[i]) for i in range(len(streams))]
    pw = [jnp.where(strict_p, m_b[i][:C], 0.0).astype(BF16) for i in range(len(streams))]
    n_stages = 6
    for stage in range(n_stages):
        if stage < n_stages - 1:
            prod = [_dot(pw[i], jnp.concatenate([_pair_blockdiag(x[i].astype(BF16)), _pair_blockdiag(pw[i])], axis=1))
                    for i in range(len(streams))]
            x = [x[i] + prod[i][:, :PW] for i in range(len(streams))]
            pw = [prod[i][:, PW:].astype(BF16) for i in range(len(streams))]
        else:
            x = [x[i] + _dot(pw[i], _pair_blockdiag(x[i].astype(BF16))) for i in range(len(streams))]
    u_b = [xi.astype(BF16) for xi in x]
    y = [ars[i][C:]
         + _dot(jnp.concatenate([jnp.where(lower_p, m_b[i][C:], 0.0), jnp.where(lower_p, m_k[i][C:], 0.0)], axis=1),
                jnp.concatenate([_pair_blockdiag(u_b[i]), v_bd[i]], axis=0))
         for i in range(len(streams))]
    for i, (bi, p) in enumerate(streams):
        upd = _dot_tn(jnp.concatenate([u_b[i], v_p[i]], axis=0),
                      jnp.concatenate([pre[bi]['b_d'][:, lanes(p)], pre[bi]['k_d'][:, lanes(p)]], axis=0))
        s_ref[bi, p] = s0[i] * pre[bi]['e_last'][:, lanes(p)] + jnp.where(same_head, upd, 0.0)

    lnw = lnw_ref[...]
    lnb = lnb_ref[...]

    def head_sum(t):
        s1 = jnp.sum(jnp.where(first, t, 0.0), axis=-1, keepdims=True)
        s2 = jnp.sum(jnp.where(first, 0.0, t), axis=-1, keepdims=True)
        return jnp.where(first, s1, s2)

    for bi in range(NB):
        outs = []
        for p in range(NP):
            yi = y[bi * NP + p]
            mean = head_sum(yi) * (1.0 / HEAD_DIM)
            yc = yi - mean
            var = head_sum(yc * yc) * (1.0 / HEAD_DIM)
            yn = yc * lax.rsqrt(var + RWKV_LN_EPS) * lnw[:, lanes(p)] + lnb[:, lanes(p)]
            bonus = head_sum(pre[bi]['rkk'][:, lanes(p)]) * pre[bi]['v'][:, lanes(p)]
            outs.append((yn + bonus) * pre[bi]['g'][:, lanes(p)])
        o_ref[bi] = jnp.concatenate(outs, axis=1).astype(o_ref.dtype)


RWKV_BATCH_ROWS = 4


def _rwkv_scan(r, lw, k, v, a, b, g, r_k, lnx_w, lnx_b):
    B, S, W = r.shape
    C = RWKV_CHUNK
    assert S % C == 0 and W == RWKV_HEADS * HEAD_DIM
    nb = RWKV_BATCH_ROWS if B % RWKV_BATCH_ROWS == 0 else 1
    seq = pl.BlockSpec((nb, C, W), lambda bb, c: (bb, c, 0))
    par = pl.BlockSpec((1, W), lambda bb, c: (0, 0))
    out = pl.pallas_call(
        _rwkv_scan_kernel,
        grid=(B // nb, S // C),
        in_specs=[seq] * 7 + [par] * 3,
        out_specs=seq,
        out_shape=jax.ShapeDtypeStruct((B, S, W), BF16),
        scratch_shapes=[pltpu.VMEM((nb, RWKV_HEADS // 2, 2 * HEAD_DIM, 2 * HEAD_DIM), F32)],
        compiler_params=_cparams(2),
        name="rwkv_scan",
    )(r, lw, k, v, a, b, g, r_k.reshape(1, W), lnx_w.reshape(1, W), lnx_b.reshape(1, W))
    return out.reshape(B * S, W)


def _gla_kernel(q_ref, k_ref, v_ref, og_ref, gd_ref, gup_ref, gb_ref, on_ref, o_ref, s_ref):
    c = pl.program_id(1)

    @pl.when(c == 0)
    def _():
        s_ref[...] = jnp.zeros_like(s_ref)

    C = GLA_CHUNK
    NB = q_ref.shape[0]
    row = lax.broadcasted_iota(I32, (C, C), 0)
    col = lax.broadcasted_iota(I32, (C, C), 1)
    lower = row >= col
    tri = jnp.where(lower, 1.0, 0.0).astype(F32)
    onorm = on_ref[...]
    zs = [_dot(gd_ref[bi], gup_ref[...]) + gb_ref[...] for bi in range(NB)]
    cums = [_tri_cumsum(tri, -_softplus(-z) / GLA_GATE_NORM) for z in zs]
    qe, ke, kd, e_last, v = [], [], [], [], []
    for bi in range(NB):
        cum = cums[bi]
        cum_last = cum[C - 1:C, :]
        k = k_ref[bi]
        qe.append((q_ref[bi] * (GLA_DK ** -0.5) * jnp.exp(cum)).astype(BF16))
        ke.append((k * jnp.exp(-cum)).astype(BF16))
        kd.append((k * jnp.exp(cum_last - cum)).astype(BF16))
        e_last.append(jnp.exp(cum_last))
        v.append(v_ref[bi].astype(BF16))
    streams = [(bi, h) for bi in range(NB) for h in range(GLA_HEADS)]
    ks = lambda h: slice(h * GLA_DK, (h + 1) * GLA_DK)
    vs = lambda h: slice(h * GLA_DV, (h + 1) * GLA_DV)
    sts = [s_ref[bi, h] for bi, h in streams]
    atts = [jnp.where(lower, _dot_nt(qe[bi][:, ks(h)], ke[bi][:, ks(h)]), 0.0) for bi, h in streams]
    inters = [_dot_nt(qe[bi][:, ks(h)], sts[i]) for i, (bi, h) in enumerate(streams)]
    os_ = [inters[i] + _dot(atts[i], v[bi][:, vs(h)]) for i, (bi, h) in enumerate(streams)]
    for i, (bi, h) in enumerate(streams):
        s_ref[bi, h] = sts[i] * e_last[bi][:, ks(h)] + _dot_tn(v[bi][:, vs(h)], kd[bi][:, ks(h)])
    for bi in range(NB):
        og = og_ref[bi]
        outs = []
        for h in range(GLA_HEADS):
            gate = og[:, vs(h)]
            outs.append(_rms(os_[bi * GLA_HEADS + h], onorm) * (gate * _sigmoid(gate)))
        o_ref[bi] = jnp.concatenate(outs, axis=1).astype(o_ref.dtype)


GLA_BATCH_ROWS = 8


def _gla(qkvo, gd, gate_up_pad, gate_b, onorm, B, S):
    C = GLA_CHUNK
    assert S % C == 0
    KW = GLA_HEADS * GLA_DK
    VW = GLA_HEADS * GLA_DV
    x3 = qkvo.reshape(B, S, qkvo.shape[-1])
    gd3 = gd.reshape(B, S, LANES)
    nb = GLA_BATCH_ROWS if B % GLA_BATCH_ROWS == 0 else 1
    out = pl.pallas_call(
        _gla_kernel,
        grid=(B // nb, S // C),
        in_specs=[pl.BlockSpec((nb, C, KW), lambda b, c: (b, c, 0)),
                  pl.BlockSpec((nb, C, KW), lambda b, c: (b, c, 1)),
                  pl.BlockSpec((nb, C, VW), lambda b, c: (b, c, 1)),
                  pl.BlockSpec((nb, C, VW), lambda b, c: (b, c, 2)),
                  pl.BlockSpec((nb, C, LANES), lambda b, c: (b, c, 0)),
                  pl.BlockSpec((LANES, KW), lambda b, c: (0, 0)),
                  pl.BlockSpec((1, KW), lambda b, c: (0, 0)),
                  pl.BlockSpec((1, GLA_DV), lambda b, c: (0, 0))],
        out_specs=pl.BlockSpec((nb, C, VW), lambda b, c: (b, c, 0)),
        out_shape=jax.ShapeDtypeStruct((B, S, VW), BF16),
        scratch_shapes=[pltpu.VMEM((nb, GLA_HEADS, GLA_DV, GLA_DK), F32)],
        compiler_params=_cparams(2),
        name="gla",
    )(x3, x3, x3, x3, gd3, gate_up_pad, gate_b.reshape(1, KW), onorm.reshape(1, GLA_DV))
    return out.reshape(B * S, VW)


def _xattn_kernel(*refs, n_in):
    x_ref = refs[0]
    a_refs = refs[1:1 + n_in]
    w_refs = refs[1 + n_in:1 + 2 * n_in]
    g_ref, wq_ref, mk_ref, mv_ref, wo_ref, gm_ref, wr_ref, br_ref, o_ref, lg_ref = refs[1 + 2 * n_in:]
    tq = x_ref.shape[0]
    subs = [slice(r, r + XA_SUB_ROWS) for r in range(0, tq, XA_SUB_ROWS)]
    xs = [x_ref[sub, :] for sub in subs]
    for a_ref, w_ref in zip(a_refs, w_refs):
        xs = [x + jnp.dot(a_ref[sub, :], w_ref[...], preferred_element_type=F32) for x, sub in zip(xs, subs)]
    qs = [_dot(_rms(x, g_ref[...]), wq_ref[...]).astype(BF16) for x in xs]
    mk = mk_ref[...]
    mv = mv_ref[...]
    sls = [slice(hd * XA_HEAD_DIM, (hd + 1) * XA_HEAD_DIM) for hd in range(XA_HEADS)]
    scores = [[_dot_nt(q[:, sl], mk[:, sl]) for sl in sls] for q in qs]
    probs = []
    for sc in scores:
        ps = []
        for s in sc:
            s = s * (XA_HEAD_DIM ** -0.5)
            p = jnp.exp(s - jnp.max(s, axis=-1, keepdims=True))
            ps.append((p / jnp.sum(p, axis=-1, keepdims=True)).astype(BF16))
        probs.append(ps)
    os_ = [jnp.concatenate([_dot(p, mv[:, sl]) for p, sl in zip(ps, sls)], axis=1) for ps in probs]
    outs = [x + _dot(o, wo_ref[...]) for x, o in zip(xs, os_)]
    for out, sub in zip(outs, subs):
        o_ref[sub, :] = out
    for out, sub in zip(outs, subs):
        lg_ref[:, sub] = _dot_nt(wr_ref[...], _rms(out, gm_ref[...])) + br_ref[...]


XA_SUB_ROWS = 256


def _mix_proj_xattn(x, acts, weights, g, wq, mk, mv, wo, g_moe, wt_router, bt_router, B, S, tq=1024):
    D = x.shape[-1]
    assert S % tq == 0 and tq % XA_SUB_ROWS == 0
    M = mk.shape[0] // B
    XW = mk.shape[-1]
    n_in = len(acts)
    seq3 = lambda a: a.reshape(B, S, a.shape[-1])
    row_spec = lambda a: pl.BlockSpec((None, tq, a.shape[-1]), lambda b, n: (b, n, 0))
    const = lambda a: pl.BlockSpec(a.shape, lambda b, n: (0,) * a.ndim)
    out, logits = pl.pallas_call(
        functools.partial(_xattn_kernel, n_in=n_in),
        grid=(B, S // tq),
        in_specs=[row_spec(x)] + [row_spec(a) for a in acts] + [const(w) for w in weights]
                 + [pl.BlockSpec((1, D), lambda b, n: (0, 0)),
                    pl.BlockSpec((D, XW), lambda b, n: (0, 0)),
                    pl.BlockSpec((None, M, XW), lambda b, n: (b, 0, 0)),
                    pl.BlockSpec((None, M, XW), lambda b, n: (b, 0, 0)),
                    pl.BlockSpec((XW, D), lambda b, n: (0, 0)),
                    pl.BlockSpec((1, D), lambda b, n: (0, 0)),
                    pl.BlockSpec((LANES, D), lambda b, n: (0, 0)),
                    pl.BlockSpec((LANES, XA_SUB_ROWS), lambda b, n: (0, 0))],
        out_specs=[pl.BlockSpec((None, tq, D), lambda b, n: (b, n, 0)),
                   pl.BlockSpec((None, LANES, tq), lambda b, n: (b * (S // tq) + n, 0, 0))],
        out_shape=[jax.ShapeDtypeStruct((B, S, D), F32),
                   jax.ShapeDtypeStruct((B * S // tq, LANES, tq), F32)],
        compiler_params=_cparams(2),
        name="xattn",
    )(seq3(x), *[seq3(a) for a in acts], *weights, g.reshape(1, D), wq,
      mk.reshape(B, M, XW), mv.reshape(B, M, XW), wo, g_moe.reshape(1, D), wt_router, bt_router)
    return out.reshape(B * S, D), logits


ROUTER_ROWS = 40


def _router_kernel(lg_ref, info_ref, slot_ref, cnt_ref, carry_ref):
    i = pl.program_id(0)

    @pl.when(i == 0)
    def _():
        carry_ref[...] = jnp.zeros_like(carry_ref)

    logits = lg_ref[:ROUTER_ROWS, :]
    tm = logits.shape[1]
    row = lax.broadcasted_iota(I32, logits.shape, 0)
    big = jnp.int32(LANES)
    neg = -jnp.inf
    gl = jnp.where(row < MOE_GROUPS, logits, neg)
    gmax = jnp.max(gl, axis=0, keepdims=True)
    g_top = jnp.min(jnp.where(gl == gmax, row, big), axis=0, keepdims=True)
    p_group = 1.0 / jnp.sum(jnp.exp(gl - gmax), axis=0, keepdims=True)
    lo = MOE_GROUPS + MOE_EXPERTS_PER_GROUP * g_top
    in_group = jnp.where(row >= lo, jnp.where(row < lo + MOE_EXPERTS_PER_GROUP, 1, 0), 0) > 0
    el = jnp.where(in_group, logits, neg)
    emax = jnp.max(el, axis=0, keepdims=True)
    ee = jnp.exp(el - emax)
    prob = ee / jnp.sum(ee, axis=0, keepdims=True)
    prob = jnp.where(in_group, prob, -1.0)
    p1 = jnp.max(prob, axis=0, keepdims=True)
    i1 = jnp.min(jnp.where(prob == p1, row, big), axis=0, keepdims=True)
    rest = jnp.where(row == i1, -1.0, prob)
    p2 = jnp.max(rest, axis=0, keepdims=True)
    i2 = jnp.min(jnp.where(rest == p2, row, big), axis=0, keepdims=True)
    tot = p1 + p2
    g1 = p_group * p1 / tot
    g2 = p_group * p2 / tot
    oh = jnp.concatenate([jnp.where(row == i1, 1.0, 0.0), jnp.where(row == i2, 1.0, 0.0)], axis=0)
    tr = lax.broadcasted_iota(I32, (tm, tm), 0)
    tc = lax.broadcasted_iota(I32, (tm, tm), 1)
    pre = _dot(oh, jnp.where(tr < tc, 1.0, 0.0))
    tots = _dot(oh, jnp.ones((tm, LANES), F32))
    reps = tm // LANES
    carry = carry_ref[...]
    base1 = jnp.concatenate([carry] * reps, axis=1)
    base2 = jnp.concatenate([carry + tots[:ROUTER_ROWS]] * reps, axis=1)
    r1 = jnp.sum(oh[:ROUTER_ROWS] * (base1 + pre[:ROUTER_ROWS]), axis=0, keepdims=True)
    r2 = jnp.sum(oh[ROUTER_ROWS:] * (base2 + pre[ROUTER_ROWS:]), axis=0, keepdims=True)
    carry = carry + tots[:ROUTER_ROWS] + tots[ROUTER_ROWS:]
    carry_ref[...] = carry
    cnt_ref[...] = carry
    e1 = (i1 - MOE_GROUPS).astype(F32)
    e2 = (i2 - MOE_GROUPS).astype(F32)
    slot_rows = [e1, e2, r1, r2, g1, g2]
    rows8 = lax.broadcasted_iota(I32, (SUBLANES, tm), 0)
    slot = jnp.zeros((SUBLANES, tm), F32)
    for j, val in enumerate(slot_rows):
        slot = jnp.where(rows8 == j, val, slot)
    slot_ref[...] = slot
    wide = jnp.concatenate([slot, jnp.zeros((LANES - SUBLANES, tm), F32)], axis=0)
    info_ref[...] = jnp.transpose(wide)


def _router(logits, tm=256):
    n_row_tiles, _, tq = logits.shape
    assert tq % tm == 0 and tm % LANES == 0
    per = tq // tm
    T = n_row_tiles * tq
    NT = T // tm
    return pl.pallas_call(
        _router_kernel,
        grid=(NT,),
        in_specs=[pl.BlockSpec((None, LANES, tm), lambda i: (i // per, 0, i % per))],
        out_specs=[pl.BlockSpec((tm, LANES), lambda i: (i, 0)),
                   pl.BlockSpec((None, SUBLANES, tm), lambda i: (i, 0, 0)),
                   pl.BlockSpec((ROUTER_ROWS, LANES), lambda i: (0, 0))],
        out_shape=[jax.ShapeDtypeStruct((T, LANES), F32),
                   jax.ShapeDtypeStruct((NT, SUBLANES, tm), F32),
                   jax.ShapeDtypeStruct((ROUTER_ROWS, LANES), F32)],
        scratch_shapes=[pltpu.VMEM((ROUTER_ROWS, LANES), F32)],
        compiler_params=_cparams(1),
        name="router",
    )(logits)


def _row_bytes_wait(hbm, buf, sem):
    pltpu.make_async_copy(buf, hbm.at[pl.ds(0, buf.shape[0]), :], sem).wait()


def _to_row_tiles(ref, val):
    n = val.shape[0]
    for c in range(SUBLANES):
        ref[pl.ds(c, n, stride=SUBLANES), :] = val[:, c * LANES:(c + 1) * LANES]


def _from_row_tiles(ref):
    n = ref.shape[0] // SUBLANES
    return jnp.concatenate([ref[pl.ds(c, n, stride=SUBLANES), :] for c in range(SUBLANES)], axis=1)


def _moe_dispatch_kernel(pends_ref, cnt_ref, dest_ref, x_ref, g_ref, hs_hbm, hbuf, zbuf, sems, zsem, *, td):
    i = pl.program_id(0)
    nt = pl.num_programs(0)
    slot = lax.rem(i, 2)

    @pl.when(i == 0)
    def _():
        zbuf[...] = jnp.zeros_like(zbuf)
        for e in range(MOE_EXPERTS):
            @pl.when(cnt_ref[e] > 0)
            def _():
                start = pl.multiple_of((pends_ref[e] - MOE_BLOCK) * SUBLANES, MOE_BLOCK)
                pltpu.make_async_copy(zbuf, hs_hbm.at[pl.ds(start, MOE_BLOCK * SUBLANES), :], zsem).start()
        for e in range(MOE_EXPERTS):
            @pl.when(cnt_ref[e] > 0)
            def _():
                pltpu.make_async_copy(zbuf, hs_hbm.at[pl.ds(0, MOE_BLOCK * SUBLANES), :], zsem).wait()

        first_unused = pends_ref[MOE_EXPERTS - 1] // MOE_BLOCK
        n_blocks = hs_hbm.shape[0] // (MOE_BLOCK * SUBLANES)

        def zero_start(blk, carry):
            start = pl.multiple_of(blk * (MOE_BLOCK * SUBLANES), MOE_BLOCK)
            pltpu.make_async_copy(zbuf, hs_hbm.at[pl.ds(start, MOE_BLOCK * SUBLANES), :], zsem).start()
            return carry

        def zero_wait(blk, carry):
            pltpu.make_async_copy(zbuf, hs_hbm.at[pl.ds(0, MOE_BLOCK * SUBLANES), :], zsem).wait()
            return carry

        lax.fori_loop(first_unused, n_blocks, zero_start, 0)
        lax.fori_loop(first_unused, n_blocks, zero_wait, 0)

    hb = hbuf.at[slot]
    _to_row_tiles(hb, _rms(x_ref[...], g_ref[...]))
    for j in range(td):
        for c in range(2):
            row = pl.multiple_of(dest_ref[0, c * td + j] * SUBLANES, SUBLANES)
            pltpu.make_async_copy(hb.at[pl.ds(j * SUBLANES, SUBLANES), :],
                                  hs_hbm.at[pl.ds(row, SUBLANES), :],
                                  sems.at[slot]).start(priority=c)

    @pl.when(i > 0)
    def _():
        other = hbuf.at[1 - slot]
        _row_bytes_wait(hs_hbm, other, sems.at[1 - slot])
        _row_bytes_wait(hs_hbm, other, sems.at[1 - slot])

    @pl.when(i == nt - 1)
    def _():
        _row_bytes_wait(hs_hbm, hb, sems.at[slot])
        _row_bytes_wait(hs_hbm, hb, sems.at[slot])


def _moe_dispatch(x, g, pends, counts, dest3, P, td):
    T, D = x.shape
    assert D == ROW_TILE and T % td == 0
    grid_spec = pltpu.PrefetchScalarGridSpec(
        num_scalar_prefetch=2,
        grid=(T // td,),
        in_specs=[pl.BlockSpec((None, 1, 2 * td), lambda i, pe, cn: (i, 0, 0), memory_space=pltpu.SMEM),
                  pl.BlockSpec((td, D), lambda i, pe, cn: (i, 0)),
                  pl.BlockSpec((1, D), lambda i, pe, cn: (0, 0))],
        out_specs=pl.BlockSpec(memory_space=pl.ANY),
        scratch_shapes=[pltpu.VMEM((2, td * SUBLANES, LANES), F32),
                        pltpu.VMEM((MOE_BLOCK * SUBLANES, LANES), F32),
                        pltpu.SemaphoreType.DMA((2,)),
                        pltpu.SemaphoreType.DMA(())],
    )
    return pl.pallas_call(
        functools.partial(_moe_dispatch_kernel, td=td),
        grid_spec=grid_spec,
        out_shape=jax.ShapeDtypeStruct((P * SUBLANES, LANES), F32),
        compiler_params=_cparams(1),
        name="moe_dispatch",
    )(pends, counts, dest3, x, g.reshape(1, D))


def _moe_expert_kernel(be_ref, nu_ref, hs_ref, w1_ref, w3_ref, w2_ref, o_ref, w1b, w3b, w2b):
    i = pl.program_id(0)
    used = i < nu_ref[0]
    changed = jnp.logical_or(i == 0, be_ref[i] != be_ref[jnp.maximum(i - 1, 0)])

    @pl.when(jnp.logical_and(used, changed))
    def _():
        w1b[...] = w1_ref[...].astype(BF16)
        w3b[...] = w3_ref[...].astype(BF16)
        w2b[...] = w2_ref[...].astype(BF16)

    @pl.when(used)
    def _():
        xe = _from_row_tiles(hs_ref).astype(BF16)
        ff = w1b.shape[1]
        halves = [slice(0, ff // 2), slice(ff // 2, ff)]
        ups = [(jnp.dot(xe, w1b[:, sl], preferred_element_type=F32),
                jnp.dot(xe, w3b[:, sl], preferred_element_type=F32)) for sl in halves]
        act = [(a * _sigmoid(a) * b).astype(BF16) for a, b in ups]
        y = sum(jnp.dot(a, w2b[sl, :], preferred_element_type=F32) for a, sl in zip(act, halves))
        _to_row_tiles(o_ref, y)

    @pl.when(jnp.logical_not(used))
    def _():
        o_ref[...] = jnp.zeros_like(o_ref)


def _moe_experts(hs, block_e, n_used, w1, w3, w2, layer):
    P = hs.shape[0] // SUBLANES
    D = ROW_TILE
    FF = w1.shape[-1]
    NB = P // MOE_BLOCK
    last = lambda i, nu: jnp.minimum(i, nu[0] - 1)
    grid_spec = pltpu.PrefetchScalarGridSpec(
        num_scalar_prefetch=2,
        grid=(NB,),
        in_specs=[pl.BlockSpec((MOE_BLOCK * SUBLANES, LANES), lambda i, be, nu: (last(i, nu), 0)),
                  pl.BlockSpec((None, None, D, FF), lambda i, be, nu: (layer, be[last(i, nu)], 0, 0)),
                  pl.BlockSpec((None, None, D, FF), lambda i, be, nu: (layer, be[last(i, nu)], 0, 0)),
                  pl.BlockSpec((None, None, FF, D), lambda i, be, nu: (layer, be[last(i, nu)], 0, 0))],
        out_specs=pl.BlockSpec((MOE_BLOCK * SUBLANES, LANES), lambda i, be, nu: (i, 0)),
        scratch_shapes=[pltpu.VMEM((D, FF), BF16),
                        pltpu.VMEM((D, FF), BF16),
                        pltpu.VMEM((FF, D), BF16)],
    )
    return pl.pallas_call(
        _moe_expert_kernel,
        grid_spec=grid_spec,
        out_shape=jax.ShapeDtypeStruct((P * SUBLANES, LANES), F32),
        compiler_params=_cparams(1),
        name="moe_experts",
    )(block_e, n_used, hs, w1, w3, w2)


def _gather_rows(src_hbm, idx_ref, dst_ref, sem, first, last):
    for r in range(first, last):
        row = pl.multiple_of(idx_ref[0, r] * SUBLANES, SUBLANES)
        pltpu.make_async_copy(src_hbm.at[pl.ds(row, SUBLANES), :],
                              dst_ref.at[pl.ds(r * SUBLANES, SUBLANES), :], sem).start(priority=r % 2)


def _moe_combine_kernel(pos_ref, posn_ref, x_ref, info_ref, yb_hbm, g_ref, *rest, tc, final_norm, splits):
    if splits:
        w_ref, o_ref = rest[0], rest[1]
        p_refs = rest[2:2 + len(splits)]
        ybuf, sems = rest[2 + len(splits):]
    else:
        o_ref, ybuf, sems = rest
    i = pl.program_id(0)
    nb = pl.num_programs(0)
    slot = lax.rem(i, 2)

    @pl.when(i == 0)
    def _():
        def issue(r, carry):
            src = pl.multiple_of(pos_ref[0, r] * SUBLANES, SUBLANES)
            dst = pl.multiple_of(r * SUBLANES, SUBLANES)
            pltpu.make_async_copy(yb_hbm.at[pl.ds(src, SUBLANES), :],
                                  ybuf.at[0, pl.ds(dst, SUBLANES), :], sems.at[0]).start()
            return carry
        lax.fori_loop(0, 2 * tc, issue, 0)

    def wait_tile(s_):
        pltpu.make_async_copy(yb_hbm.at[pl.ds(0, 2 * tc * SUBLANES), :], ybuf.at[s_], sems.at[s_]).wait()

    if not splits:
        @pl.when(i + 1 < nb)
        def _():
            _gather_rows(yb_hbm, posn_ref, ybuf.at[1 - slot], sems.at[1 - slot], 0, 2 * tc)

    wait_tile(slot)
    info = info_ref[...]
    yb = ybuf.at[slot]
    y0 = _from_row_tiles(yb.at[pl.ds(0, tc * SUBLANES), :])
    y1 = _from_row_tiles(yb.at[pl.ds(tc * SUBLANES, tc * SUBLANES), :])
    out = x_ref[...] + (y0 * info[:, 4:5] + y1 * info[:, 5:6])
    if final_norm:
        out = _rms(out, g_ref[...])
    o_ref[...] = out
    if splits:
        h = _rms(out, g_ref[...]).astype(BF16)
        chunks = []
        for p_ref, n in zip(p_refs, splits):
            for c0 in range(0, n, PROJ_CHUNK):
                chunks.append((p_ref, c0, min(PROJ_CHUNK, n - c0)))
        per = -(-2 * tc // len(chunks))
        off = 0
        for ci, (p_ref, c0, width) in enumerate(chunks):
            val = jnp.dot(h, w_ref[:, off:off + width], preferred_element_type=F32)
            _gather_rows(yb_hbm, posn_ref, ybuf.at[1 - slot], sems.at[1 - slot],
                         min(ci * per, 2 * tc), min((ci + 1) * per, 2 * tc))
            p_ref[:, c0:c0 + width] = val.astype(p_ref.dtype)
            off += width

        @pl.when(i == nb - 1)
        def _():
            wait_tile(1 - slot)


def _moe_combine(x, info, dest3, yb, g, final_norm, tc, next_proj=None):
    T, D = x.shape
    assert D == ROW_TILE and T % tc == 0
    NT = T // tc
    in_specs = [pl.BlockSpec((None, 1, 2 * tc), lambda i: (i, 0, 0), memory_space=pltpu.SMEM),
                pl.BlockSpec((None, 1, 2 * tc), lambda i: (jnp.minimum(i + 1, NT - 1), 0, 0),
                             memory_space=pltpu.SMEM),
                pl.BlockSpec((tc, D), lambda i: (i, 0)),
                pl.BlockSpec((tc, LANES), lambda i: (i, 0)),
                pl.BlockSpec(memory_space=pl.ANY),
                pl.BlockSpec((1, D), lambda i: (0, 0))]
    out_specs = [pl.BlockSpec((tc, D), lambda i: (i, 0))]
    out_shape = [jax.ShapeDtypeStruct((T, D), F32)]
    args = [dest3, dest3, x, info, yb, g.reshape(1, D)]
    splits = ()
    if next_proj is not None:
        w, splits, out_dtypes = next_proj
        assert not final_norm and sum(splits) == w.shape[1]
        in_specs.append(pl.BlockSpec(w.shape, lambda i: (0, 0)))
        args.append(w)
        out_specs += [pl.BlockSpec((tc, n), lambda i: (i, 0)) for n in splits]
        out_shape += [jax.ShapeDtypeStruct((T, n), dt) for n, dt in zip(splits, out_dtypes)]
    outs = pl.pallas_call(
        functools.partial(_moe_combine_kernel, tc=tc, final_norm=final_norm, splits=tuple(splits)),
        grid=(NT,),
        in_specs=in_specs,
        out_specs=out_specs,
        out_shape=out_shape,
        scratch_shapes=[pltpu.VMEM((2, 2 * tc * SUBLANES, LANES), F32), pltpu.SemaphoreType.DMA((2,))],
        compiler_params=_cparams(1),
        name="moe_combine",
    )(*args)
    return outs[0], tuple(outs[1:])


MOE_TILE = 512
MOE_COMBINE_TILE = 256
MOE_LAST_COMBINE_TILE = 256
PROJ_CHUNK = 256
ROUTER_TILE = 256


def _tile_slots(dest, tile):
    n_tiles = dest.shape[0] * dest.shape[2] // tile
    return jnp.concatenate([dest[:, 0, :].reshape(n_tiles, 1, tile), dest[:, 1, :].reshape(n_tiles, 1, tile)], axis=2)


def _router_params(w_group, b_group, w_expert, b_expert, lanes_out):
    D = w_group.shape[0]
    n_log = MOE_GROUPS + MOE_EXPERTS
    wt = jnp.zeros((LANES, D), F32).at[:MOE_GROUPS].set(w_group.T).at[MOE_GROUPS:n_log].set(w_expert.T)
    bt = jnp.zeros((LANES,), F32).at[:MOE_GROUPS].set(b_group).at[MOE_GROUPS:n_log].set(b_expert)
    return wt.astype(BF16), jnp.broadcast_to(bt[:, None], (LANES, lanes_out))


def _moe_layer(x, g, logits, w1, w3, w2, layer, g_out, final_norm, next_proj):
    T, D = x.shape
    n_log = MOE_GROUPS + MOE_EXPERTS
    info, slot, cnt = _router(logits, ROUTER_TILE)
    P = 2 * T + MOE_EXPERTS * MOE_BLOCK
    NB = P // MOE_BLOCK
    counts = cnt[MOE_GROUPS:n_log, 0].astype(I32)
    padded = (counts + MOE_BLOCK - 1) // MOE_BLOCK * MOE_BLOCK
    pends = jnp.cumsum(padded).astype(I32)
    pstarts = pends - padded
    block_start = jnp.arange(NB, dtype=I32) * MOE_BLOCK
    block_e = jnp.minimum(jnp.sum((pends[None, :] <= block_start[:, None]).astype(I32), axis=1),
                          MOE_EXPERTS - 1).astype(I32)
    n_used = (pends[-1:] // MOE_BLOCK).astype(I32)
    eid = slot[:, 0:2, :].astype(I32)
    expert_ids = jnp.arange(MOE_EXPERTS, dtype=I32)
    seg_start = jnp.sum(jnp.where(eid[..., None] == expert_ids, pstarts, 0), axis=-1)
    dest = seg_start + slot[:, 2:4, :].astype(I32)
    hs = _moe_dispatch(x, g, pends, counts, _tile_slots(dest, MOE_TILE), P, MOE_TILE)
    yb = _moe_experts(hs, block_e, n_used, w1, w3, w2, layer)
    tc = MOE_LAST_COMBINE_TILE if next_proj is None else MOE_COMBINE_TILE
    return _moe_combine(x, info, _tile_slots(dest, tc), yb, g_out, final_norm, tc, next_proj)


def kernel(x, mem, norm_mix, norm_xattn, norm_moe, norm_final, ev_w_in, ev_sinks, ev_mu, ev_w0, ev_w2, ev_a0, ev_a2, ev_g2, ev_k_k, ev_k_a, ev_r_k, ev_lnx_w, ev_lnx_b, ev_w_out, od_w_in, od_gate_up, od_gate_b, od_onorm, od_w_out, mem_norm, mem_wk, mem_wv, xa_wq, xa_wo, moe_w_group, moe_b_group, moe_w_expert, moe_b_expert, moe_w1, moe_w3, moe_w2):
    B, S, D = x.shape
    M = mem.shape[1]
    T = B * S
    depth = norm_mix.shape[0]
    xf = x.reshape(T, D)

    XW = XA_HEADS * XA_HEAD_DIM
    w_kv = jnp.concatenate([mem_wk, mem_wv], axis=1).astype(BF16)
    mk, mv = _norm_matmul(mem.reshape(B * M, D), mem_norm, w_kv, (XW, XW), (BF16, BF16))

    KW = GLA_HEADS * GLA_DK
    VW = GLA_HEADS * GLA_DV
    swa_cols = SWA_Q_HEADS * HEAD_DIM + 2 * (SWA_Q_HEADS // SWA_GROUP) * HEAD_DIM

    def in_proj(layer):
        i = layer // 2
        if layer % 2 == 0:
            return ev_w_in[i].astype(BF16), (swa_cols, ev_w_in.shape[-1] - swa_cols), (F32, F32)
        R = od_gate_up.shape[1]
        w = od_w_in[i]
        w_re = jnp.concatenate([w[:, :2 * KW + VW], w[:, 2 * KW + VW + R:],
                                w[:, 2 * KW + VW:2 * KW + VW + R],
                                jnp.zeros((D, LANES - R), F32)], axis=1).astype(BF16)
        return w_re, (2 * KW + 2 * VW, LANES), (F32, F32)

    w_first, _, _ = in_proj(0)
    qkv0, rw0 = _in_proj_rwkv(xf, norm_mix[0], w_first, swa_cols, ev_mu[0], ev_w0[0], ev_w2[0], ev_a0[0], ev_a2[0],
                              ev_g2[0], ev_k_k[0], ev_k_a[0], S)
    proj = None
    for layer in range(depth):
        i = layer // 2
        if layer % 2 == 0:
            if layer == 0:
                qkv, rw = qkv0, [t.reshape(B, S, RWKV_WIDTH) for t in rw0]
            else:
                qkv, p_rw = proj
                rw = _rwkv_prep(p_rw, ev_mu[i], ev_w0[i], ev_w2[i], ev_a0[i], ev_a2[i],
                                ev_g2[i], ev_k_k[i], ev_k_a[i], B, S)
            o_a = _swa(qkv, ev_sinks[i], B, S)
            o_b = _rwkv_scan(*rw, ev_r_k[i].reshape(-1), ev_lnx_w[i], ev_lnx_b[i])
            w_out = ev_w_out[i].astype(BF16)
            qw = o_a.shape[-1]
            mix_acts, mix_ws = [o_a, o_b], [w_out[:qw], w_out[qw:]]
        else:
            qkvo, gd = proj
            R = od_gate_up.shape[1]
            gup = jnp.zeros((LANES, KW), F32).at[:R].set(od_gate_up[i]).astype(BF16)
            o = _gla(qkvo, gd, gup, od_gate_b[i], od_onorm[i], B, S)
            mix_acts, mix_ws = [o], [od_w_out[i].astype(BF16)]
        wt_router, bt_router = _router_params(moe_w_group[layer], moe_b_group[layer], moe_w_expert[layer],
                                              moe_b_expert[layer], XA_SUB_ROWS)
        xf, logits = _mix_proj_xattn(xf, mix_acts, mix_ws, norm_xattn[layer], xa_wq[layer].astype(BF16), mk, mv,
                                     xa_wo[layer].astype(BF16), norm_moe[layer], wt_router, bt_router, B, S)
        last = layer == depth - 1
        g_out = norm_final if last else norm_mix[layer + 1]
        xf, proj = _moe_layer(xf, norm_moe[layer], logits, moe_w1, moe_w3, moe_w2, layer,
                              g_out, last, None if last else in_proj(layer + 1))
    return xf.reshape(B, S, D)
```

```python
import functools

import jax
import jax.numpy as jnp
from jax import lax
from jax.experimental import pallas as pl
from jax.experimental.pallas import tpu as pltpu

F32 = jnp.float32
BF16 = jnp.bfloat16
I32 = jnp.int32

EPS = 1e-6
HEAD_DIM = 64
SWA_WINDOW = 128
SWA_Q_HEADS = 8
SWA_GROUP = 4
RWKV_HEADS = 8
RWKV_WIDTH = 512
RWKV_LN_EPS = 64e-5
RWKV_CHUNK = 64
GLA_HEADS = 4
GLA_DK = 128
GLA_DV = 256
GLA_CHUNK = 64
GLA_GATE_NORM = 16.0
XA_HEADS = 4
XA_HEAD_DIM = 128
MOE_GROUPS = 4
MOE_EXPERTS_PER_GROUP = 8
MOE_EXPERTS = 32
MOE_BLOCK = 512
LANES = 128
SUBLANES = 8
ROW_TILE = SUBLANES * LANES

VMEM_LIMIT_BYTES = 48 * 1024 * 1024


def _cparams(n_axes):
    return pltpu.CompilerParams(dimension_semantics=("arbitrary",) * n_axes,
                                vmem_limit_bytes=VMEM_LIMIT_BYTES)


def _dot(a, b):
    return jnp.dot(a.astype(BF16), b.astype(BF16), preferred_element_type=F32)


def _dot_nt(a, b):
    return lax.dot_general(a.astype(BF16), b.astype(BF16), (((1,), (1,)), ((), ())),
                           preferred_element_type=F32)


def _dot_tn(a, b):
    return lax.dot_general(a.astype(BF16), b.astype(BF16), (((0,), (0,)), ((), ())),
                           preferred_element_type=F32)


def _tri_cumsum(tri, x):
    hi = x.astype(BF16)
    rest = x - hi.astype(F32)
    mid = rest.astype(BF16)
    lo = (rest - mid.astype(F32)).astype(BF16)
    w = x.shape[1]
    sums = jnp.dot(tri.astype(BF16), jnp.concatenate([hi, mid, lo], axis=1), preferred_element_type=F32)
    return sums[:, :w] + sums[:, w:2 * w] + sums[:, 2 * w:]


def _rms(x, g):
    ms = jnp.mean(x * x, axis=-1, keepdims=True)
    return x * lax.rsqrt(ms + EPS) * g


def _sigmoid(x):
    return 1.0 / (1.0 + jnp.exp(-x))


def _softplus(x):
    return jnp.maximum(x, 0.0) + jnp.log(1.0 + jnp.exp(-jnp.abs(x)))


def _norm_matmul_kernel(x_ref, g_ref, w_ref, *o_refs, splits):
    h = _rms(x_ref[...], g_ref[...]).astype(BF16)
    off = 0
    for o_ref, n in zip(o_refs, splits):
        o_ref[...] = jnp.dot(h, w_ref[:, off:off + n], preferred_element_type=F32).astype(o_ref.dtype)
        off += n


def _norm_matmul(x, g, w, splits, out_dtypes, tm=256):
    T, D = x.shape
    N = w.shape[1]
    assert sum(splits) == N and T % tm == 0
    return pl.pallas_call(
        functools.partial(_norm_matmul_kernel, splits=tuple(splits)),
        grid=(T // tm,),
        in_specs=[pl.BlockSpec((tm, D), lambda i: (i, 0)),
                  pl.BlockSpec((1, D), lambda i: (0, 0)),
                  pl.BlockSpec((D, N), lambda i: (0, 0))],
        out_specs=[pl.BlockSpec((tm, n), lambda i: (i, 0)) for n in splits],
        out_shape=[jax.ShapeDtypeStruct((T, n), dt) for n, dt in zip(splits, out_dtypes)],
        compiler_params=_cparams(1),
        name="norm_matmul",
    )(x, g.reshape(1, D), w)


def _swa_kernel(sinks_ref, q_ref, kp_ref, kc_ref, vp_ref, vc_ref, o_ref):
    n = pl.program_id(1)
    W = SWA_WINDOW
    NB = q_ref.shape[0]
    qpos = lax.broadcasted_iota(I32, (W, 2 * W), 0) + W
    kpos = lax.broadcasted_iota(I32, (W, 2 * W), 1)
    rel = qpos - kpos
    in_window = jnp.where(rel >= 0, jnp.where(rel < W, 1, 0), 0)
    has_prev = jnp.where(n > 0, 1, 0)
    valid = (in_window * jnp.where(kpos >= W, 1, has_prev)) > 0
    n_groups = SWA_Q_HEADS // SWA_GROUP
    streams = [(bi, g) for bi in range(NB) for g in range(n_groups)]
    qb = [q_ref[bi].astype(BF16) for bi in range(NB)]
    kb = [jnp.concatenate([kp_ref[bi], kc_ref[bi]], axis=0).astype(BF16) for bi in range(NB)]
    vb = [jnp.concatenate([vp_ref[bi], vc_ref[bi]], axis=0).astype(BF16) for bi in range(NB)]
    gs = lambda g: slice(g * HEAD_DIM, (g + 1) * HEAD_DIM)
    scores = []
    for bi, g in streams:
        qg = jnp.concatenate([qb[bi][:, h * HEAD_DIM:(h + 1) * HEAD_DIM]
                              for h in range(g * SWA_GROUP, (g + 1) * SWA_GROUP)], axis=0)
        scores.append(_dot_nt(qg, kb[bi][:, gs(g)]))
    probs = []
    for i, (bi, g) in enumerate(streams):
        pieces = []
        for j in range(SWA_GROUP):
            s = jnp.where(valid, scores[i][j * W:(j + 1) * W] * (HEAD_DIM ** -0.5), -jnp.inf)
            sink = sinks_ref[g * SWA_GROUP + j]
            m = jnp.maximum(jnp.max(s, axis=-1, keepdims=True), sink)
            p = jnp.exp(s - m)
            den = jnp.sum(p, axis=-1, keepdims=True) + jnp.exp(sink - m)
            pieces.append((p / den).astype(BF16))
        probs.append(jnp.concatenate(pieces, axis=0))
    ogs = [_dot(probs[i], vb[bi][:, gs(g)]) for i, (bi, g) in enumerate(streams)]
    for bi in range(NB):
        outs = []
        for g in range(n_groups):
            og = ogs[bi * n_groups + g]
            outs += [og[j * W:(j + 1) * W] for j in range(SWA_GROUP)]
        o_ref[bi] = jnp.concatenate(outs, axis=1).astype(o_ref.dtype)


SWA_BATCH_ROWS = 2


def _swa(qkv, sinks, B, S):
    W = SWA_WINDOW
    assert S % W == 0
    qkv3 = qkv.reshape(B, S, qkv.shape[-1])
    qw = SWA_Q_HEADS * HEAD_DIM
    kw = qw // SWA_GROUP
    kcol = qw // kw
    nb = SWA_BATCH_ROWS if B % SWA_BATCH_ROWS == 0 else 1
    out = pl.pallas_call(
        _swa_kernel,
        grid=(B // nb, S // W),
        in_specs=[pl.BlockSpec(memory_space=pltpu.SMEM),
                  pl.BlockSpec((nb, W, qw), lambda b, n: (b, n, 0)),
                  pl.BlockSpec((nb, W, kw), lambda b, n: (b, jnp.maximum(n - 1, 0), kcol)),
                  pl.BlockSpec((nb, W, kw), lambda b, n: (b, n, kcol)),
                  pl.BlockSpec((nb, W, kw), lambda b, n: (b, jnp.maximum(n - 1, 0), kcol + 1)),
                  pl.BlockSpec((nb, W, kw), lambda b, n: (b, n, kcol + 1))],
        out_specs=pl.BlockSpec((nb, W, qw), lambda b, n: (b, n, 0)),
        out_shape=jax.ShapeDtypeStruct((B, S, qw), BF16),
        compiler_params=_cparams(2),
        name="swa",
    )(sinks, qkv3, qkv3, qkv3, qkv3, qkv3)
    return out.reshape(B * S, qw)


def _rwkv_prep_math(p, last, mu_ref, w0_ref, w2_ref, a0_ref, a2_ref, g2_ref, kk_ref, ka_ref, outs):
    r_out, lw_out, k_out, v_out, a_out, b_out, g_out = outs
    C = RWKV_WIDTH
    row = lax.broadcasted_iota(I32, p.shape, 0)
    p_prev = jnp.where(row == 0, last, pltpu.roll(p, 1, axis=0))
    p = p + (p_prev - p) * mu_ref[...]
    r = p[:, :C]
    k = p[:, C:2 * C]
    v = p[:, 2 * C:3 * C]
    xw = p[:, 3 * C:3 * C + 64]
    xa = p[:, 3 * C + 64:3 * C + 128]
    xg = p[:, 3 * C + 128:]
    w = -_softplus(-(w0_ref[...] + _dot(jnp.tanh(xw), w2_ref[...]))) - 0.5
    lw = -jnp.exp(w)
    a = _sigmoid(a0_ref[...] + _dot(xa, a2_ref[...]))
    g = _dot(_sigmoid(xg), g2_ref[...])
    kk = k * kk_ref[...]
    pieces = []
    for h in range(RWKV_HEADS):
        kh = kk[:, h * HEAD_DIM:(h + 1) * HEAD_DIM]
        nrm = jnp.sqrt(jnp.sum(kh * kh, axis=-1, keepdims=True))
        pieces.append(kh / jnp.maximum(nrm, 1e-12))
    kk = jnp.concatenate(pieces, axis=1)
    r_out[...] = r
    lw_out[...] = lw
    k_out[...] = k * (1.0 + (a - 1.0) * ka_ref[...])
    v_out[...] = v
    a_out[...] = -kk
    b_out[...] = kk * a
    g_out[...] = g


def _rwkv_prep_kernel(p_ref, pprev_ref, *refs):
    n = pl.program_id(1)
    last = jnp.where(n > 0, pprev_ref[SUBLANES - 1:SUBLANES, :], 0.0)
    _rwkv_prep_math(p_ref[...], last, *refs[:8], refs[8:])


def _rwkv_params(mu, w0, w2, a0, a2, g2, k_k, k_a):
    row = lambda t: t.reshape(1, -1)
    return [row(mu), row(w0), w2.astype(BF16), row(a0), a2.astype(BF16), g2.astype(BF16), row(k_k), row(k_a)]


def _rwkv_prep(p, mu, w0, w2, a0, a2, g2, k_k, k_a, B, S, tt=256):
    assert S % tt == 0
    C = RWKV_WIDTH
    PW = p.shape[-1]
    p3 = p.reshape(B, S, PW)
    full = lambda arr: pl.BlockSpec(arr.shape, lambda b, n: (0,) * arr.ndim)
    params = _rwkv_params(mu, w0, w2, a0, a2, g2, k_k, k_a)
    outs = pl.pallas_call(
        _rwkv_prep_kernel,
        grid=(B, S // tt),
        in_specs=[pl.BlockSpec((None, tt, PW), lambda b, n: (b, n, 0)),
                  pl.BlockSpec((None, SUBLANES, PW),
                               lambda b, n: (b, jnp.maximum(n * (tt // SUBLANES) - 1, 0), 0))]
                 + [full(t) for t in params],
        out_specs=[pl.BlockSpec((None, tt, C), lambda b, n: (b, n, 0))] * 7,
        out_shape=[jax.ShapeDtypeStruct((B, S, C), F32)] * 7,
        compiler_params=_cparams(2),
        name="rwkv_prep",
    )(p3, p3, *params)
    return outs


def _in_proj_rwkv_kernel(x_ref, g_ref, w_ref, *refs, swa_cols, tiles_per_seq):
    params = refs[:8]
    qkv_out = refs[8]
    outs = refs[9:16]
    p_buf, last_buf = refs[16:]
    i = pl.program_id(0)

    @pl.when(i == 0)
    def _():
        p_buf[...] = jnp.zeros_like(p_buf)
        last_buf[...] = jnp.zeros_like(last_buf)

    j = i - 1
    p_prev_tile = p_buf[lax.rem(i + 1, 2)]
    tm = p_prev_tile.shape[0]
    last = jnp.where(lax.rem(j, tiles_per_seq) == 0, 0.0, last_buf[...])
    _rwkv_prep_math(p_prev_tile, last, *params, outs)
    last_buf[...] = p_prev_tile[tm - 1:tm, :]
    h = _rms(x_ref[...], g_ref[...]).astype(BF16)
    qkv_out[...] = jnp.dot(h, w_ref[:, :swa_cols], preferred_element_type=F32)
    p_buf[lax.rem(i, 2)] = jnp.dot(h, w_ref[:, swa_cols:], preferred_element_type=F32)


def _in_proj_rwkv(x, g, w, swa_cols, mu, w0, w2, a0, a2, g2, k_k, k_a, S, tm=256):
    T, D = x.shape
    assert S % tm == 0 and T % S == 0
    N = w.shape[1]
    C = RWKV_WIDTH
    NT = T // tm
    params = _rwkv_params(mu, w0, w2, a0, a2, g2, k_k, k_a)
    const = lambda arr: pl.BlockSpec(arr.shape, lambda i: (0,) * arr.ndim)
    cur = lambda i: (jnp.minimum(i, NT - 1), 0)
    prev = lambda i: (jnp.maximum(i - 1, 0), 0)
    outs = pl.pallas_call(
        functools.partial(_in_proj_rwkv_kernel, swa_cols=swa_cols, tiles_per_seq=S // tm),
        grid=(NT + 1,),
        in_specs=[pl.BlockSpec((tm, D), cur), pl.BlockSpec((1, D), lambda i: (0, 0)), const(w)]
                 + [const(t) for t in params],
        out_specs=[pl.BlockSpec((tm, swa_cols), cur)] + [pl.BlockSpec((tm, C), prev)] * 7,
        out_shape=[jax.ShapeDtypeStruct((T, swa_cols), F32)] + [jax.ShapeDtypeStruct((T, C), F32)] * 7,
        scratch_shapes=[pltpu.VMEM((2, tm, N - swa_cols), F32), pltpu.VMEM((1, N - swa_cols), F32)],
        compiler_params=_cparams(1),
        name="in_proj_rwkv",
    )(x, g.reshape(1, D), w, *params)
    return outs[0], outs[1:]


def _pair_blockdiag(x):
    lane = lax.broadcasted_iota(I32, x.shape, 1)
    zero = jnp.zeros_like(x)
    return jnp.concatenate([jnp.where(lane < HEAD_DIM, x, zero), jnp.where(lane >= HEAD_DIM, x, zero)], axis=0)


def _rwkv_scan_kernel(r_ref, lw_ref, k_ref, v_ref, a_ref, b_ref, g_ref, rk_ref, lnw_ref, lnb_ref,
                      o_ref, s_ref):
    c = pl.program_id(1)

    @pl.when(c == 0)
    def _():
        s_ref[...] = jnp.zeros_like(s_ref)

    C = RWKV_CHUNK
    NB = r_ref.shape[0]
    NP = RWKV_HEADS // 2
    PW = 2 * HEAD_DIM
    row = lax.broadcasted_iota(I32, (C, C), 0)
    col = lax.broadcasted_iota(I32, (C, C), 1)
    tri = jnp.where(row >= col, 1.0, 0.0).astype(F32)
    rowp = lax.broadcasted_iota(I32, (C, PW), 0)
    colp = lax.broadcasted_iota(I32, (C, PW), 1)
    colp = jnp.where(colp >= HEAD_DIM, colp - HEAD_DIM, colp)
    lower_p = rowp >= colp
    strict_p = rowp > colp
    rows = lax.broadcasted_iota(I32, (PW, PW), 0)
    cols = lax.broadcasted_iota(I32, (PW, PW), 1)
    same_head = jnp.where(rows >= HEAD_DIM, 1, 0) == jnp.where(cols >= HEAD_DIM, 1, 0)
    first = lax.broadcasted_iota(I32, (C, PW), 1) < HEAD_DIM

    streams = [(bi, p) for bi in range(NB) for p in range(NP)]
    pre = []
    for bi in range(NB):
        lw = lw_ref[bi]
        cum = _tri_cumsum(tri, lw)
        cum_last = cum[C - 1:C, :]
        r = r_ref[bi]
        k = k_ref[bi]
        v = v_ref[bi]
        a = a_ref[bi]
        b = b_ref[bi]
        e_neg = jnp.exp(-cum)
        e_rem = jnp.exp(cum_last - cum)
        pre.append(dict(
            r_t=(r * jnp.exp(cum)).astype(BF16), a_t=(a * jnp.exp(cum - lw)).astype(BF16),
            b_t=(b * e_neg).astype(BF16), k_t=(k * e_neg).astype(BF16),
            b_d=(b * e_rem).astype(BF16), k_d=(k * e_rem).astype(BF16),
            v_b=v.astype(BF16), v=v, e_last=jnp.exp(cum_last), rkk=r * k * rk_ref[...], g=g_ref[bi]))

    def lanes(p):
        return slice(p * PW, (p + 1) * PW)

    ar = [jnp.concatenate([pre[bi]['a_t'][:, lanes(p)], pre[bi]['r_t'][:, lanes(p)]], axis=0) for bi, p in streams]
    s0 = [s_ref[bi, p] for bi, p in streams]
    big = [_dot_nt(ar[i], jnp.concatenate([_pair_blockdiag(pre[bi]['b_t'][:, lanes(p)]),
                                           _pair_blockdiag(pre[bi]['k_t'][:, lanes(p)]),
                                           s0[i].astype(BF16)], axis=0))
           for i, (bi, p) in enumerate(streams)]
    m_b = [t[:, :PW] for t in big]
    m_k = [t[:, PW:2 * PW] for t in big]
    ars = [t[:, 2 * PW:] for t in big]
    v_p = [pre[bi]['v_b'][:, lanes(p)] for bi, p in streams]
    v_bd = [_pair_blockdiag(vp) for vp in v_p]
    x = [ars[i][:C] + _dot(jnp.where(strict_p, m_k[i][:C], 0.0), v_bd[i]) for i in range(len(streams))]
    pw = [jnp.where(strict_p, m_b[i][:C], 0.0).astype(BF16) for i in range(len(streams))]
    n_stages = 6
    for stage in range(n_stages):
        if stage < n_stages - 1:
            prod = [_dot(pw[i], jnp.concatenate([_pair_blockdiag(x[i].astype(BF16)), _pair_blockdiag(pw[i])], axis=1))
                    for i in range(len(streams))]
            x = [x[i] + prod[i][:, :PW] for i in range(len(streams))]
            pw = [prod[i][:, PW:].astype(BF16) for i in range(len(streams))]
        else:
            x = [x[i] + _dot(pw[i], _pair_blockdiag(x[i].astype(BF16))) for i in range(len(streams))]
    u_b = [xi.astype(BF16) for xi in x]
    y = [ars[i][C:]
         + _dot(jnp.concatenate([jnp.where(lower_p, m_b[i][C:], 0.0), jnp.where(lower_p, m_k[i][C:], 0.0)], axis=1),
                jnp.concatenate([_pair_blockdiag(u_b[i]), v_bd[i]], axis=0))
         for i in range(len(streams))]
    for i, (bi, p) in enumerate(streams):
        upd = _dot_tn(jnp.concatenate([u_b[i], v_p[i]], axis=0),
                      jnp.concatenate([pre[bi]['b_d'][:, lanes(p)], pre[bi]['k_d'][:, lanes(p)]], axis=0))
        s_ref[bi, p] = s0[i] * pre[bi]['e_last'][:, lanes(p)] + jnp.where(same_head, upd, 0.0)

    lnw = lnw_ref[...]
    lnb = lnb_ref[...]

    def head_sum(t):
        s1 = jnp.sum(jnp.where(first, t, 0.0), axis=-1, keepdims=True)
        s2 = jnp.sum(jnp.where(first, 0.0, t), axis=-1, keepdims=True)
        return jnp.where(first, s1, s2)

    for bi in range(NB):
        outs = []
        for p in range(NP):
            yi = y[bi * NP + p]
            mean = head_sum(yi) * (1.0 / HEAD_DIM)
            yc = yi - mean
            var = head_sum(yc * yc) * (1.0 / HEAD_DIM)
            yn = yc * lax.rsqrt(var + RWKV_LN_EPS) * lnw[:, lanes(p)] + lnb[:, lanes(p)]
            bonus = head_sum(pre[bi]['rkk'][:, lanes(p)]) * pre[bi]['v'][:, lanes(p)]
            outs.append((yn + bonus) * pre[bi]['g'][:, lanes(p)])
        o_ref[bi] = jnp.concatenate(outs, axis=1).astype(o_ref.dtype)


RWKV_BATCH_ROWS = 8


def _rwkv_scan(r, lw, k, v, a, b, g, r_k, lnx_w, lnx_b):
    B, S, W = r.shape
    C = RWKV_CHUNK
    assert S % C == 0 and W == RWKV_HEADS * HEAD_DIM
    nb = RWKV_BATCH_ROWS if B % RWKV_BATCH_ROWS == 0 else 1
    seq = pl.BlockSpec((nb, C, W), lambda bb, c: (bb, c, 0))
    par = pl.BlockSpec((1, W), lambda bb, c: (0, 0))
    out = pl.pallas_call(
        _rwkv_scan_kernel,
        grid=(B // nb, S // C),
        in_specs=[seq] * 7 + [par] * 3,
        out_specs=seq,
        out_shape=jax.ShapeDtypeStruct((B, S, W), BF16),
        scratch_shapes=[pltpu.VMEM((nb, RWKV_HEADS // 2, 2 * HEAD_DIM, 2 * HEAD_DIM), F32)],
        compiler_params=_cparams(2),
        name="rwkv_scan",
    )(r, lw, k, v, a, b, g, r_k.reshape(1, W), lnx_w.reshape(1, W), lnx_b.reshape(1, W))
    return out.reshape(B * S, W)


def _gla_kernel(q_ref, k_ref, v_ref, og_ref, gd_ref, gup_ref, gb_ref, on_ref, o_ref, s_ref):
    c = pl.program_id(1)

    @pl.when(c == 0)
    def _():
        s_ref[...] = jnp.zeros_like(s_ref)

    C = GLA_CHUNK
    NB = q_ref.shape[0]
    row = lax.broadcasted_iota(I32, (C, C), 0)
    col = lax.broadcasted_iota(I32, (C, C), 1)
    lower = row >= col
    tri = jnp.where(lower, 1.0, 0.0).astype(F32)
    onorm = on_ref[...]
    zs = [_dot(gd_ref[bi], gup_ref[...]) + gb_ref[...] for bi in range(NB)]
    cums = [_tri_cumsum(tri, -_softplus(-z) / GLA_GATE_NORM) for z in zs]
    qe, ke, kd, e_last, v = [], [], [], [], []
    for bi in range(NB):
        cum = cums[bi]
        cum_last = cum[C - 1:C, :]
        k = k_ref[bi]
        qe.append((q_ref[bi] * (GLA_DK ** -0.5) * jnp.exp(cum)).astype(BF16))
        ke.append((k * jnp.exp(-cum)).astype(BF16))
        kd.append((k * jnp.exp(cum_last - cum)).astype(BF16))
        e_last.append(jnp.exp(cum_last))
        v.append(v_ref[bi].astype(BF16))
    streams = [(bi, h) for bi in range(NB) for h in range(GLA_HEADS)]
    ks = lambda h: slice(h * GLA_DK, (h + 1) * GLA_DK)
    vs = lambda h: slice(h * GLA_DV, (h + 1) * GLA_DV)
    sts = [s_ref[bi, h] for bi, h in streams]
    atts = [jnp.where(lower, _dot_nt(qe[bi][:, ks(h)], ke[bi][:, ks(h)]), 0.0) for bi, h in streams]
    inters = [_dot_nt(qe[bi][:, ks(h)], sts[i]) for i, (bi, h) in enumerate(streams)]
    os_ = [inters[i] + _dot(atts[i], v[bi][:, vs(h)]) for i, (bi, h) in enumerate(streams)]
    for i, (bi, h) in enumerate(streams):
        s_ref[bi, h] = sts[i] * e_last[bi][:, ks(h)] + _dot_tn(v[bi][:, vs(h)], kd[bi][:, ks(h)])
    for bi in range(NB):
        og = og_ref[bi]
        outs = []
        for h in range(GLA_HEADS):
            gate = og[:, vs(h)]
            outs.append(_rms(os_[bi * GLA_HEADS + h], onorm) * (gate * _sigmoid(gate)))
        o_ref[bi] = jnp.concatenate(outs, axis=1).astype(o_ref.dtype)


GLA_BATCH_ROWS = 8


def _gla(qkvo, gd, gate_up_pad, gate_b, onorm, B, S):
    C = GLA_CHUNK
    assert S % C == 0
    KW = GLA_HEADS * GLA_DK
    VW = GLA_HEADS * GLA_DV
    x3 = qkvo.reshape(B, S, qkvo.shape[-1])
    gd3 = gd.reshape(B, S, LANES)
    nb = GLA_BATCH_ROWS if B % GLA_BATCH_ROWS == 0 else 1
    out = pl.pallas_call(
        _gla_kernel,
        grid=(B // nb, S // C),
        in_specs=[pl.BlockSpec((nb, C, KW), lambda b, c: (b, c, 0)),
                  pl.BlockSpec((nb, C, KW), lambda b, c: (b, c, 1)),
                  pl.BlockSpec((nb, C, VW), lambda b, c: (b, c, 1)),
                  pl.BlockSpec((nb, C, VW), lambda b, c: (b, c, 2)),
                  pl.BlockSpec((nb, C, LANES), lambda b, c: (b, c, 0)),
                  pl.BlockSpec((LANES, KW), lambda b, c: (0, 0)),
                  pl.BlockSpec((1, KW), lambda b, c: (0, 0)),
                  pl.BlockSpec((1, GLA_DV), lambda b, c: (0, 0))],
        out_specs=pl.BlockSpec((nb, C, VW), lambda b, c: (b, c, 0)),
        out_shape=jax.ShapeDtypeStruct((B, S, VW), BF16),
        scratch_shapes=[pltpu.VMEM((nb, GLA_HEADS, GLA_DV, GLA_DK), F32)],
        compiler_params=_cparams(2),
        name="gla",
    )(x3, x3, x3, x3, gd3, gate_up_pad, gate_b.reshape(1, KW), onorm.reshape(1, GLA_DV))
    return out.reshape(B * S, VW)


def _xattn_kernel(*refs, n_in):
    x_ref = refs[0]
    a_refs = refs[1:1 + n_in]
    w_refs = refs[1 + n_in:1 + 2 * n_in]
    g_ref, wq_ref, mk_ref, mv_ref, wo_ref, gm_ref, wr_ref, br_ref, o_ref, lg_ref = refs[1 + 2 * n_in:]
    tq = x_ref.shape[0]
    subs = [slice(r, r + XA_SUB_ROWS) for r in range(0, tq, XA_SUB_ROWS)]
    xs = [x_ref[sub, :] for sub in subs]
    for a_ref, w_ref in zip(a_refs, w_refs):
        xs = [x + jnp.dot(a_ref[sub, :], w_ref[...], preferred_element_type=F32) for x, sub in zip(xs, subs)]
    qs = [_dot(_rms(x, g_ref[...]), wq_ref[...]).astype(BF16) for x in xs]
    mk = mk_ref[...]
    mv = mv_ref[...]
    sls = [slice(hd * XA_HEAD_DIM, (hd + 1) * XA_HEAD_DIM) for hd in range(XA_HEADS)]
    scores = [[_dot_nt(q[:, sl], mk[:, sl]) for sl in sls] for q in qs]
    probs = []
    for sc in scores:
        ps = []
        for s in sc:
            s = s * (XA_HEAD_DIM ** -0.5)
            p = jnp.exp(s - jnp.max(s, axis=-1, keepdims=True))
            ps.append((p / jnp.sum(p, axis=-1, keepdims=True)).astype(BF16))
        probs.append(ps)
    os_ = [jnp.concatenate([_dot(p, mv[:, sl]) for p, sl in zip(ps, sls)], axis=1) for ps in probs]
    outs = [x + _dot(o, wo_ref[...]) for x, o in zip(xs, os_)]
    for out, sub in zip(outs, subs):
        o_ref[sub, :] = out
    for out, sub in zip(outs, subs):
        lg_ref[:, sub] = _dot_nt(wr_ref[...], _rms(out, gm_ref[...])) + br_ref[...]


XA_SUB_ROWS = 256


def _mix_proj_xattn(x, acts, weights, g, wq, mk, mv, wo, g_moe, wt_router, bt_router, B, S, tq=1024):
    D = x.shape[-1]
    assert S % tq == 0 and tq % XA_SUB_ROWS == 0
    M = mk.shape[0] // B
    XW = mk.shape[-1]
    n_in = len(acts)
    seq3 = lambda a: a.reshape(B, S, a.shape[-1])
    row_spec = lambda a: pl.BlockSpec((None, tq, a.shape[-1]), lambda b, n: (b, n, 0))
    const = lambda a: pl.BlockSpec(a.shape, lambda b, n: (0,) * a.ndim)
    out, logits = pl.pallas_call(
        functools.partial(_xattn_kernel, n_in=n_in),
        grid=(B, S // tq),
        in_specs=[row_spec(x)] + [row_spec(a) for a in acts] + [const(w) for w in weights]
                 + [pl.BlockSpec((1, D), lambda b, n: (0, 0)),
                    pl.BlockSpec((D, XW), lambda b, n: (0, 0)),
                    pl.BlockSpec((None, M, XW), lambda b, n: (b, 0, 0)),
                    pl.BlockSpec((None, M, XW), lambda b, n: (b, 0, 0)),
                    pl.BlockSpec((XW, D), lambda b, n: (0, 0)),
                    pl.BlockSpec((1, D), lambda b, n: (0, 0)),
                    pl.BlockSpec((LANES, D), lambda b, n: (0, 0)),
                    pl.BlockSpec((LANES, XA_SUB_ROWS), lambda b, n: (0, 0))],
        out_specs=[pl.BlockSpec((None, tq, D), lambda b, n: (b, n, 0)),
                   pl.BlockSpec((None, LANES, tq), lambda b, n: (b * (S // tq) + n, 0, 0))],
        out_shape=[jax.ShapeDtypeStruct((B, S, D), F32),
                   jax.ShapeDtypeStruct((B * S // tq, LANES, tq), F32)],
        compiler_params=_cparams(2),
        name="xattn",
    )(seq3(x), *[seq3(a) for a in acts], *weights, g.reshape(1, D), wq,
      mk.reshape(B, M, XW), mv.reshape(B, M, XW), wo, g_moe.reshape(1, D), wt_router, bt_router)
    return out.reshape(B * S, D), logits


ROUTER_ROWS = 40


def _router_kernel(lg_ref, info_ref, slot_ref, cnt_ref, carry_ref):
    i = pl.program_id(0)

    @pl.when(i == 0)
    def _():
        carry_ref[...] = jnp.zeros_like(carry_ref)

    logits = lg_ref[:ROUTER_ROWS, :]
    tm = logits.shape[1]
    row = lax.broadcasted_iota(I32, logits.shape, 0)
    big = jnp.int32(LANES)
    neg = -jnp.inf
    gl = jnp.where(row < MOE_GROUPS, logits, neg)
    gmax = jnp.max(gl, axis=0, keepdims=True)
    g_top = jnp.min(jnp.where(gl == gmax, row, big), axis=0, keepdims=True)
    p_group = 1.0 / jnp.sum(jnp.exp(gl - gmax), axis=0, keepdims=True)
    lo = MOE_GROUPS + MOE_EXPERTS_PER_GROUP * g_top
    in_group = jnp.where(row >= lo, jnp.where(row < lo + MOE_EXPERTS_PER_GROUP, 1, 0), 0) > 0
    el = jnp.where(in_group, logits, neg)
    emax = jnp.max(el, axis=0, keepdims=True)
    ee = jnp.exp(el - emax)
    prob = ee / jnp.sum(ee, axis=0, keepdims=True)
    prob = jnp.where(in_group, prob, -1.0)
    p1 = jnp.max(prob, axis=0, keepdims=True)
    i1 = jnp.min(jnp.where(prob == p1, row, big), axis=0, keepdims=True)
    rest = jnp.where(row == i1, -1.0, prob)
    p2 = jnp.max(rest, axis=0, keepdims=True)
    i2 = jnp.min(jnp.where(rest == p2, row, big), axis=0, keepdims=True)
    tot = p1 + p2
    g1 = p_group * p1 / tot
    g2 = p_group * p2 / tot
    oh = jnp.concatenate([jnp.where(row == i1, 1.0, 0.0), jnp.where(row == i2, 1.0, 0.0)], axis=0)
    tr = lax.broadcasted_iota(I32, (tm, tm), 0)
    tc = lax.broadcasted_iota(I32, (tm, tm), 1)
    pre = _dot(oh, jnp.where(tr < tc, 1.0, 0.0))
    tots = _dot(oh, jnp.ones((tm, LANES), F32))
    reps = tm // LANES
    carry = carry_ref[...]
    base1 = jnp.concatenate([carry] * reps, axis=1)
    base2 = jnp.concatenate([carry + tots[:ROUTER_ROWS]] * reps, axis=1)
    r1 = jnp.sum(oh[:ROUTER_ROWS] * (base1 + pre[:ROUTER_ROWS]), axis=0, keepdims=True)
    r2 = jnp.sum(oh[ROUTER_ROWS:] * (base2 + pre[ROUTER_ROWS:]), axis=0, keepdims=True)
    carry = carry + tots[:ROUTER_ROWS] + tots[ROUTER_ROWS:]
    carry_ref[...] = carry
    cnt_ref[...] = carry
    e1 = (i1 - MOE_GROUPS).astype(F32)
    e2 = (i2 - MOE_GROUPS).astype(F32)
    slot_rows = [e1, e2, r1, r2, g1, g2]
    rows8 = lax.broadcasted_iota(I32, (SUBLANES, tm), 0)
    slot = jnp.zeros((SUBLANES, tm), F32)
    for j, val in enumerate(slot_rows):
        slot = jnp.where(rows8 == j, val, slot)
    slot_ref[...] = slot
    wide = jnp.concatenate([slot, jnp.zeros((LANES - SUBLANES, tm), F32)], axis=0)
    info_ref[...] = jnp.transpose(wide)


def _router(logits, tm=256):
    n_row_tiles, _, tq = logits.shape
    assert tq % tm == 0 and tm % LANES == 0
    per = tq // tm
    T = n_row_tiles * tq
    NT = T // tm
    return pl.pallas_call(
        _router_kernel,
        grid=(NT,),
        in_specs=[pl.BlockSpec((None, LANES, tm), lambda i: (i // per, 0, i % per))],
        out_specs=[pl.BlockSpec((tm, LANES), lambda i: (i, 0)),
                   pl.BlockSpec((None, SUBLANES, tm), lambda i: (i, 0, 0)),
                   pl.BlockSpec((ROUTER_ROWS, LANES), lambda i: (0, 0))],
        out_shape=[jax.ShapeDtypeStruct((T, LANES), F32),
                   jax.ShapeDtypeStruct((NT, SUBLANES, tm), F32),
                   jax.ShapeDtypeStruct((ROUTER_ROWS, LANES), F32)],
        scratch_shapes=[pltpu.VMEM((ROUTER_ROWS, LANES), F32)],
        compiler_params=_cparams(1),
        name="router",
    )(logits)


def _row_bytes_wait(hbm, buf, sem):
    pltpu.make_async_copy(buf, hbm.at[pl.ds(0, buf.shape[0]), :], sem).wait()


def _to_row_tiles(ref, val):
    n = val.shape[0]
    for c in range(SUBLANES):
        ref[pl.ds(c, n, stride=SUBLANES), :] = val[:, c * LANES:(c + 1) * LANES]


def _from_row_tiles(ref):
    n = ref.shape[0] // SUBLANES
    return jnp.concatenate([ref[pl.ds(c, n, stride=SUBLANES), :] for c in range(SUBLANES)], axis=1)


def _moe_dispatch_kernel(pends_ref, cnt_ref, dest_ref, x_ref, g_ref, hs_hbm, hbuf, zbuf, sems, zsem, *, td):
    i = pl.program_id(0)
    nt = pl.num_programs(0)
    slot = lax.rem(i, 2)

    @pl.when(i == 0)
    def _():
        zbuf[...] = jnp.zeros_like(zbuf)
        for e in range(MOE_EXPERTS):
            @pl.when(cnt_ref[e] > 0)
            def _():
                start = pl.multiple_of((pends_ref[e] - MOE_BLOCK) * SUBLANES, MOE_BLOCK)
                pltpu.make_async_copy(zbuf, hs_hbm.at[pl.ds(start, MOE_BLOCK * SUBLANES), :], zsem).start()
        for e in range(MOE_EXPERTS):
            @pl.when(cnt_ref[e] > 0)
            def _():
                pltpu.make_async_copy(zbuf, hs_hbm.at[pl.ds(0, MOE_BLOCK * SUBLANES), :], zsem).wait()

        first_unused = pends_ref[MOE_EXPERTS - 1] // MOE_BLOCK
        n_blocks = hs_hbm.shape[0] // (MOE_BLOCK * SUBLANES)

        def zero_start(blk, carry):
            start = pl.multiple_of(blk * (MOE_BLOCK * SUBLANES), MOE_BLOCK)
            pltpu.make_async_copy(zbuf, hs_hbm.at[pl.ds(start, MOE_BLOCK * SUBLANES), :], zsem).start()
            return carry

        def zero_wait(blk, carry):
            pltpu.make_async_copy(zbuf, hs_hbm.at[pl.ds(0, MOE_BLOCK * SUBLANES), :], zsem).wait()
            return carry

        lax.fori_loop(first_unused, n_blocks, zero_start, 0)
        lax.fori_loop(first_unused, n_blocks, zero_wait, 0)

    hb = hbuf.at[slot]
    _to_row_tiles(hb, _rms(x_ref[...], g_ref[...]))
    for j in range(td):
        for c in range(2):
            row = pl.multiple_of(dest_ref[0, c * td + j] * SUBLANES, SUBLANES)
            pltpu.make_async_copy(hb.at[pl.ds(j * SUBLANES, SUBLANES), :],
                                  hs_hbm.at[pl.ds(row, SUBLANES), :],
                                  sems.at[slot]).start(priority=c)

    @pl.when(i > 0)
    def _():
        other = hbuf.at[1 - slot]
        _row_bytes_wait(hs_hbm, other, sems.at[1 - slot])
        _row_bytes_wait(hs_hbm, other, sems.at[1 - slot])

    @pl.when(i == nt - 1)
    def _():
        _row_bytes_wait(hs_hbm, hb, sems.at[slot])
        _row_bytes_wait(hs_hbm, hb, sems.at[slot])


def _moe_dispatch(x, g, pends, counts, dest3, P, td):
    T, D = x.shape
    assert D == ROW_TILE and T % td == 0
    grid_spec = pltpu.PrefetchScalarGridSpec(
        num_scalar_prefetch=2,
        grid=(T // td,),
        in_specs=[pl.BlockSpec((None, 1, 2 * td), lambda i, pe, cn: (i, 0, 0), memory_space=pltpu.SMEM),
                  pl.BlockSpec((td, D), lambda i, pe, cn: (i, 0)),
                  pl.BlockSpec((1, D), lambda i, pe, cn: (0, 0))],
        out_specs=pl.BlockSpec(memory_space=pl.ANY),
        scratch_shapes=[pltpu.VMEM((2, td * SUBLANES, LANES), F32),
                        pltpu.VMEM((MOE_BLOCK * SUBLANES, LANES), F32),
                        pltpu.SemaphoreType.DMA((2,)),
                        pltpu.SemaphoreType.DMA(())],
    )
    return pl.pallas_call(
        functools.partial(_moe_dispatch_kernel, td=td),
        grid_spec=grid_spec,
        out_shape=jax.ShapeDtypeStruct((P * SUBLANES, LANES), F32),
        compiler_params=_cparams(1),
        name="moe_dispatch",
    )(pends, counts, dest3, x, g.reshape(1, D))


def _moe_expert_kernel(be_ref, nu_ref, hs_ref, w1_ref, w3_ref, w2_ref, o_ref, w1b, w3b, w2b):
    i = pl.program_id(0)
    used = i < nu_ref[0]
    changed = jnp.logical_or(i == 0, be_ref[i] != be_ref[jnp.maximum(i - 1, 0)])

    @pl.when(jnp.logical_and(used, changed))
    def _():
        w1b[...] = w1_ref[...].astype(BF16)
        w3b[...] = w3_ref[...].astype(BF16)
        w2b[...] = w2_ref[...].astype(BF16)

    @pl.when(used)
    def _():
        xe = _from_row_tiles(hs_ref).astype(BF16)
        ff = w1b.shape[1]
        halves = [slice(0, ff // 2), slice(ff // 2, ff)]
        ups = [(jnp.dot(xe, w1b[:, sl], preferred_element_type=F32),
                jnp.dot(xe, w3b[:, sl], preferred_element_type=F32)) for sl in halves]
        act = [(a * _sigmoid(a) * b).astype(BF16) for a, b in ups]
        y = sum(jnp.dot(a, w2b[sl, :], preferred_element_type=F32) for a, sl in zip(act, halves))
        _to_row_tiles(o_ref, y)

    @pl.when(jnp.logical_not(used))
    def _():
        o_ref[...] = jnp.zeros_like(o_ref)


def _moe_experts(hs, block_e, n_used, w1, w3, w2, layer):
    P = hs.shape[0] // SUBLANES
    D = ROW_TILE
    FF = w1.shape[-1]
    NB = P // MOE_BLOCK
    last = lambda i, nu: jnp.minimum(i, nu[0] - 1)
    grid_spec = pltpu.PrefetchScalarGridSpec(
        num_scalar_prefetch=2,
        grid=(NB,),
        in_specs=[pl.BlockSpec((MOE_BLOCK * SUBLANES, LANES), lambda i, be, nu: (last(i, nu), 0)),
                  pl.BlockSpec((None, None, D, FF), lambda i, be, nu: (layer, be[last(i, nu)], 0, 0)),
                  pl.BlockSpec((None, None, D, FF), lambda i, be, nu: (layer, be[last(i, nu)], 0, 0)),
                  pl.BlockSpec((None, None, FF, D), lambda i, be, nu: (layer, be[last(i, nu)], 0, 0))],
        out_specs=pl.BlockSpec((MOE_BLOCK * SUBLANES, LANES), lambda i, be, nu: (i, 0)),
        scratch_shapes=[pltpu.VMEM((D, FF), BF16),
                        pltpu.VMEM((D, FF), BF16),
                        pltpu.VMEM((FF, D), BF16)],
    )
    return pl.pallas_call(
        _moe_expert_kernel,
        grid_spec=grid_spec,
        out_shape=jax.ShapeDtypeStruct((P * SUBLANES, LANES), F32),
        compiler_params=_cparams(1),
        name="moe_experts",
    )(block_e, n_used, hs, w1, w3, w2)


def _gather_rows(src_hbm, idx_ref, dst_ref, sem, first, last):
    for r in range(first, last):
        row = pl.multiple_of(idx_ref[0, r] * SUBLANES, SUBLANES)
        pltpu.make_async_copy(src_hbm.at[pl.ds(row, SUBLANES), :],
                              dst_ref.at[pl.ds(r * SUBLANES, SUBLANES), :], sem).start(priority=r % 2)


def _moe_combine_kernel(pos_ref, posn_ref, x_ref, info_ref, yb_hbm, g_ref, *rest, tc, final_norm, splits):
    if splits:
        w_ref, o_ref = rest[0], rest[1]
        p_refs = rest[2:2 + len(splits)]
        ybuf, sems = rest[2 + len(splits):]
    else:
        o_ref, ybuf, sems = rest
    i = pl.program_id(0)
    nb = pl.num_programs(0)
    slot = lax.rem(i, 2)

    @pl.when(i == 0)
    def _():
        def issue(r, carry):
            src = pl.multiple_of(pos_ref[0, r] * SUBLANES, SUBLANES)
            dst = pl.multiple_of(r * SUBLANES, SUBLANES)
            pltpu.make_async_copy(yb_hbm.at[pl.ds(src, SUBLANES), :],
                                  ybuf.at[0, pl.ds(dst, SUBLANES), :], sems.at[0]).start()
            return carry
        lax.fori_loop(0, 2 * tc, issue, 0)

    def wait_tile(s_):
        pltpu.make_async_copy(yb_hbm.at[pl.ds(0, 2 * tc * SUBLANES), :], ybuf.at[s_], sems.at[s_]).wait()

    if not splits:
        @pl.when(i + 1 < nb)
        def _():
            _gather_rows(yb_hbm, posn_ref, ybuf.at[1 - slot], sems.at[1 - slot], 0, 2 * tc)

    wait_tile(slot)
    info = info_ref[...]
    yb = ybuf.at[slot]
    y0 = _from_row_tiles(yb.at[pl.ds(0, tc * SUBLANES), :])
    y1 = _from_row_tiles(yb.at[pl.ds(tc * SUBLANES, tc * SUBLANES), :])
    out = x_ref[...] + (y0 * info[:, 4:5] + y1 * info[:, 5:6])
    if final_norm:
        out = _rms(out, g_ref[...])
    o_ref[...] = out
    if splits:
        h = _rms(out, g_ref[...]).astype(BF16)
        chunks = []
        for p_ref, n in zip(p_refs, splits):
            for c0 in range(0, n, PROJ_CHUNK):
                chunks.append((p_ref, c0, min(PROJ_CHUNK, n - c0)))
        per = -(-2 * tc // max(len(chunks) // 2, 1))
        off = 0
        for ci, (p_ref, c0, width) in enumerate(chunks):
            val = jnp.dot(h, w_ref[:, off:off + width], preferred_element_type=F32)
            _gather_rows(yb_hbm, posn_ref, ybuf.at[1 - slot], sems.at[1 - slot],
                         min(ci * per, 2 * tc), min((ci + 1) * per, 2 * tc))
            p_ref[:, c0:c0 + width] = val.astype(p_ref.dtype)
            off += width

        @pl.when(i == nb - 1)
        def _():
            wait_tile(1 - slot)


def _moe_combine(x, info, dest3, yb, g, final_norm, tc, next_proj=None):
    T, D = x.shape
    assert D == ROW_TILE and T % tc == 0
    NT = T // tc
    in_specs = [pl.BlockSpec((None, 1, 2 * tc), lambda i: (i, 0, 0), memory_space=pltpu.SMEM),
                pl.BlockSpec((None, 1, 2 * tc), lambda i: (jnp.minimum(i + 1, NT - 1), 0, 0),
                             memory_space=pltpu.SMEM),
                pl.BlockSpec((tc, D), lambda i: (i, 0)),
                pl.BlockSpec((tc, LANES), lambda i: (i, 0)),
                pl.BlockSpec(memory_space=pl.ANY),
                pl.BlockSpec((1, D), lambda i: (0, 0))]
    out_specs = [pl.BlockSpec((tc, D), lambda i: (i, 0))]
    out_shape = [jax.ShapeDtypeStruct((T, D), F32)]
    args = [dest3, dest3, x, info, yb, g.reshape(1, D)]
    splits = ()
    if next_proj is not None:
        w, splits, out_dtypes = next_proj
        assert not final_norm and sum(splits) == w.shape[1]
        in_specs.append(pl.BlockSpec(w.shape, lambda i: (0, 0)))
        args.append(w)
        out_specs += [pl.BlockSpec((tc, n), lambda i: (i, 0)) for n in splits]
        out_shape += [jax.ShapeDtypeStruct((T, n), dt) for n, dt in zip(splits, out_dtypes)]
    outs = pl.pallas_call(
        functools.partial(_moe_combine_kernel, tc=tc, final_norm=final_norm, splits=tuple(splits)),
        grid=(NT,),
        in_specs=in_specs,
        out_specs=out_specs,
        out_shape=out_shape,
        scratch_shapes=[pltpu.VMEM((2, 2 * tc * SUBLANES, LANES), F32), pltpu.SemaphoreType.DMA((2,))],
        compiler_params=_cparams(1),
        name="moe_combine",
    )(*args)
    return outs[0], tuple(outs[1:])


MOE_TILE = 512
MOE_COMBINE_TILE = 256
MOE_LAST_COMBINE_TILE = 256
PROJ_CHUNK = 256
ROUTER_TILE = 256


def _tile_slots(dest, tile):
    n_tiles = dest.shape[0] * dest.shape[2] // tile
    return jnp.concatenate([dest[:, 0, :].reshape(n_tiles, 1, tile), dest[:, 1, :].reshape(n_tiles, 1, tile)], axis=2)


def _router_params(w_group, b_group, w_expert, b_expert, lanes_out):
    D = w_group.shape[0]
    n_log = MOE_GROUPS + MOE_EXPERTS
    wt = jnp.zeros((LANES, D), F32).at[:MOE_GROUPS].set(w_group.T).at[MOE_GROUPS:n_log].set(w_expert.T)
    bt = jnp.zeros((LANES,), F32).at[:MOE_GROUPS].set(b_group).at[MOE_GROUPS:n_log].set(b_expert)
    return wt.astype(BF16), jnp.broadcast_to(bt[:, None], (LANES, lanes_out))


def _moe_layer(x, g, logits, w1, w3, w2, layer, g_out, final_norm, next_proj):
    T, D = x.shape
    n_log = MOE_GROUPS + MOE_EXPERTS
    info, slot, cnt = _router(logits, ROUTER_TILE)
    P = 2 * T + MOE_EXPERTS * MOE_BLOCK
    NB = P // MOE_BLOCK
    counts = cnt[MOE_GROUPS:n_log, 0].astype(I32)
    padded = (counts + MOE_BLOCK - 1) // MOE_BLOCK * MOE_BLOCK
    pends = jnp.cumsum(padded).astype(I32)
    pstarts = pends - padded
    block_start = jnp.arange(NB, dtype=I32) * MOE_BLOCK
    block_e = jnp.minimum(jnp.sum((pends[None, :] <= block_start[:, None]).astype(I32), axis=1),
                          MOE_EXPERTS - 1).astype(I32)
    n_used = (pends[-1:] // MOE_BLOCK).astype(I32)
    eid = slot[:, 0:2, :].astype(I32)
    expert_ids = jnp.arange(MOE_EXPERTS, dtype=I32)
    seg_start = jnp.sum(jnp.where(eid[..., None] == expert_ids, pstarts, 0), axis=-1)
    dest = seg_start + slot[:, 2:4, :].astype(I32)
    hs = _moe_dispatch(x, g, pends, counts, _tile_slots(dest, MOE_TILE), P, MOE_TILE)
    yb = _moe_experts(hs, block_e, n_used, w1, w3, w2, layer)
    tc = MOE_LAST_COMBINE_TILE if next_proj is None else MOE_COMBINE_TILE
    return _moe_combine(x, info, _tile_slots(dest, tc), yb, g_out, final_norm, tc, next_proj)


def kernel(x, mem, norm_mix, norm_xattn, norm_moe, norm_final, ev_w_in, ev_sinks, ev_mu, ev_w0, ev_w2, ev_a0, ev_a2, ev_g2, ev_k_k, ev_k_a, ev_r_k, ev_lnx_w, ev_lnx_b, ev_w_out, od_w_in, od_gate_up, od_gate_b, od_onorm, od_w_out, mem_norm, mem_wk, mem_wv, xa_wq, xa_wo, moe_w_group, moe_b_group, moe_w_expert, moe_b_expert, moe_w1, moe_w3, moe_w2):
    B, S, D = x.shape
    M = mem.shape[1]
    T = B * S
    depth = norm_mix.shape[0]
    xf = x.reshape(T, D)

    XW = XA_HEADS * XA_HEAD_DIM
    w_kv = jnp.concatenate([mem_wk, mem_wv], axis=1).astype(BF16)
    mk, mv = _norm_matmul(mem.reshape(B * M, D), mem_norm, w_kv, (XW, XW), (BF16, BF16))

    KW = GLA_HEADS * GLA_DK
    VW = GLA_HEADS * GLA_DV
    swa_cols = SWA_Q_HEADS * HEAD_DIM + 2 * (SWA_Q_HEADS // SWA_GROUP) * HEAD_DIM

    def in_proj(layer):
        i = layer // 2
        if layer % 2 == 0:
            return ev_w_in[i].astype(BF16), (swa_cols, ev_w_in.shape[-1] - swa_cols), (F32, F32)
        R = od_gate_up.shape[1]
        w = od_w_in[i]
        w_re = jnp.concatenate([w[:, :2 * KW + VW], w[:, 2 * KW + VW + R:],
                                w[:, 2 * KW + VW:2 * KW + VW + R],
                                jnp.zeros((D, LANES - R), F32)], axis=1).astype(BF16)
        return w_re, (2 * KW + 2 * VW, LANES), (F32, F32)

    w_first, _, _ = in_proj(0)
    qkv0, rw0 = _in_proj_rwkv(xf, norm_mix[0], w_first, swa_cols, ev_mu[0], ev_w0[0], ev_w2[0], ev_a0[0], ev_a2[0],
                              ev_g2[0], ev_k_k[0], ev_k_a[0], S)
    proj = None
    for layer in range(depth):
        i = layer // 2
        if layer % 2 == 0:
            if layer == 0:
                qkv, rw = qkv0, [t.reshape(B, S, RWKV_WIDTH) for t in rw0]
            else:
                qkv, p_rw = proj
                rw = _rwkv_prep(p_rw, ev_mu[i], ev_w0[i], ev_w2[i], ev_a0[i], ev_a2[i],
                                ev_g2[i], ev_k_k[i], ev_k_a[i], B, S)
            o_a = _swa(qkv, ev_sinks[i], B, S)
            o_b = _rwkv_scan(*rw, ev_r_k[i].reshape(-1), ev_lnx_w[i], ev_lnx_b[i])
            w_out = ev_w_out[i].astype(BF16)
            qw = o_a.shape[-1]
            mix_acts, mix_ws = [o_a, o_b], [w_out[:qw], w_out[qw:]]
        else:
            qkvo, gd = proj
            R = od_gate_up.shape[1]
            gup = jnp.zeros((LANES, KW), F32).at[:R].set(od_gate_up[i]).astype(BF16)
            o = _gla(qkvo, gd, gup, od_gate_b[i], od_onorm[i], B, S)
            mix_acts, mix_ws = [o], [od_w_out[i].astype(BF16)]
        wt_router, bt_router = _router_params(moe_w_group[layer], moe_b_group[layer], moe_w_expert[layer],
                                              moe_b_expert[layer], XA_SUB_ROWS)
        xf, logits = _mix_proj_xattn(xf, mix_acts, mix_ws, norm_xattn[layer], xa_wq[layer].astype(BF16), mk, mv,
                                     xa_wo[layer].astype(BF16), norm_moe[layer], wt_router, bt_router, B, S)
        last = layer == depth - 1
        g_out = norm_final if last else norm_mix[layer + 1]
        xf, proj = _moe_layer(xf, norm_moe[layer], logits, moe_w1, moe_w3, moe_w2, layer,
                              g_out, last, None if last else in_proj(layer + 1))
    return xf.reshape(B, S, D)
```

```python
import functools

import jax
import jax.numpy as jnp
from jax import lax
from jax.experimental import pallas as pl
from jax.experimental.pallas import tpu as pltpu

F32 = jnp.float32
BF16 = jnp.bfloat16
I32 = jnp.int32

EPS = 1e-6
HEAD_DIM = 64
SWA_WINDOW = 128
SWA_Q_HEADS = 8
SWA_GROUP = 4
RWKV_HEADS = 8
RWKV_WIDTH = 512
RWKV_LN_EPS = 64e-5
RWKV_CHUNK = 64
GLA_HEADS = 4
GLA_DK = 128
GLA_DV = 256
GLA_CHUNK = 64
GLA_GATE_NORM = 16.0
XA_HEADS = 4
XA_HEAD_DIM = 128
MOE_GROUPS = 4
MOE_EXPERTS_PER_GROUP = 8
MOE_EXPERTS = 32
MOE_BLOCK = 512
LANES = 128
SUBLANES = 8
ROW_TILE = SUBLANES * LANES

VMEM_LIMIT_BYTES = 48 * 1024 * 1024


def _cparams(n_axes):
    return pltpu.CompilerParams(dimension_semantics=("arbitrary",) * n_axes,
                                vmem_limit_bytes=VMEM_LIMIT_BYTES)


def _dot(a, b):
    return jnp.dot(a.astype(BF16), b.astype(BF16), preferred_element_type=F32)


def _dot_nt(a, b):
    return lax.dot_general(a.astype(BF16), b.astype(BF16), (((1,), (1,)), ((), ())),
                           preferred_element_type=F32)


def _dot_tn(a, b):
    return lax.dot_general(a.astype(BF16), b.astype(BF16), (((0,), (0,)), ((), ())),
                           preferred_element_type=F32)


def _tri_cumsum(tri, x):
    hi = x.astype(BF16)
    rest = x - hi.astype(F32)
    mid = rest.astype(BF16)
    lo = (rest - mid.astype(F32)).astype(BF16)
    w = x.shape[1]
    sums = jnp.dot(tri.astype(BF16), jnp.concatenate([hi, mid, lo], axis=1), preferred_element_type=F32)
    return sums[:, :w] + sums[:, w:2 * w] + sums[:, 2 * w:]


def _rms(x, g):
    ms = jnp.mean(x * x, axis=-1, keepdims=True)
    return x * lax.rsqrt(ms + EPS) * g


def _sigmoid(x):
    return 1.0 / (1.0 + jnp.exp(-x))


def _softplus(x):
    return jnp.maximum(x, 0.0) + jnp.log(1.0 + jnp.exp(-jnp.abs(x)))


def _norm_matmul_kernel(x_ref, g_ref, w_ref, *o_refs, splits):
    h = _rms(x_ref[...], g_ref[...]).astype(BF16)
    off = 0
    for o_ref, n in zip(o_refs, splits):
        o_ref[...] = jnp.dot(h, w_ref[:, off:off + n], preferred_element_type=F32).astype(o_ref.dtype)
        off += n


def _norm_matmul(x, g, w, splits, out_dtypes, tm=256):
    T, D = x.shape
    N = w.shape[1]
    assert sum(splits) == N and T % tm == 0
    return pl.pallas_call(
        functools.partial(_norm_matmul_kernel, splits=tuple(splits)),
        grid=(T // tm,),
        in_specs=[pl.BlockSpec((tm, D), lambda i: (i, 0)),
                  pl.BlockSpec((1, D), lambda i: (0, 0)),
                  pl.BlockSpec((D, N), lambda i: (0, 0))],
        out_specs=[pl.BlockSpec((tm, n), lambda i: (i, 0)) for n in splits],
        out_shape=[jax.ShapeDtypeStruct((T, n), dt) for n, dt in zip(splits, out_dtypes)],
        compiler_params=_cparams(1),
        name="norm_matmul",
    )(x, g.reshape(1, D), w)


def _swa_kernel(sinks_ref, q_ref, kp_ref, kc_ref, vp_ref, vc_ref, o_ref):
    n = pl.program_id(1)
    W = SWA_WINDOW
    NB = q_ref.shape[0]
    qpos = lax.broadcasted_iota(I32, (W, 2 * W), 0) + W
    kpos = lax.broadcasted_iota(I32, (W, 2 * W), 1)
    rel = qpos - kpos
    in_window = jnp.where(rel >= 0, jnp.where(rel < W, 1, 0), 0)
    has_prev = jnp.where(n > 0, 1, 0)
    valid = (in_window * jnp.where(kpos >= W, 1, has_prev)) > 0
    n_groups = SWA_Q_HEADS // SWA_GROUP
    streams = [(bi, g) for bi in range(NB) for g in range(n_groups)]
    qb = [q_ref[bi].astype(BF16) for bi in range(NB)]
    kb = [jnp.concatenate([kp_ref[bi], kc_ref[bi]], axis=0).astype(BF16) for bi in range(NB)]
    vb = [jnp.concatenate([vp_ref[bi], vc_ref[bi]], axis=0).astype(BF16) for bi in range(NB)]
    gs = lambda g: slice(g * HEAD_DIM, (g + 1) * HEAD_DIM)
    scores = []
    for bi, g in streams:
        qg = jnp.concatenate([qb[bi][:, h * HEAD_DIM:(h + 1) * HEAD_DIM]
                              for h in range(g * SWA_GROUP, (g + 1) * SWA_GROUP)], axis=0)
        scores.append(_dot_nt(qg, kb[bi][:, gs(g)]))
    probs = []
    for i, (bi, g) in enumerate(streams):
        pieces = []
        for j in range(SWA_GROUP):
            s = jnp.where(valid, scores[i][j * W:(j + 1) * W] * (HEAD_DIM ** -0.5), -jnp.inf)
            sink = sinks_ref[g * SWA_GROUP + j]
            m = jnp.maximum(jnp.max(s, axis=-1, keepdims=True), sink)
            p = jnp.exp(s - m)
            den = jnp.sum(p, axis=-1, keepdims=True) + jnp.exp(sink - m)
            pieces.append((p / den).astype(BF16))
        probs.append(jnp.concatenate(pieces, axis=0))
    ogs = [_dot(probs[i], vb[bi][:, gs(g)]) for i, (bi, g) in enumerate(streams)]
    for bi in range(NB):
        outs = []
        for g in range(n_groups):
            og = ogs[bi * n_groups + g]
            outs += [og[j * W:(j + 1) * W] for j in range(SWA_GROUP)]
        o_ref[bi] = jnp.concatenate(outs, axis=1).astype(o_ref.dtype)


SWA_BATCH_ROWS = 2


def _swa(qkv, sinks, B, S):
    W = SWA_WINDOW
    assert S % W == 0
    qkv3 = qkv.reshape(B, S, qkv.shape[-1])
    qw = SWA_Q_HEADS * HEAD_DIM
    kw = qw // SWA_GROUP
    kcol = qw // kw
    nb = SWA_BATCH_ROWS if B % SWA_BATCH_ROWS == 0 else 1
    out = pl.pallas_call(
        _swa_kernel,
        grid=(B // nb, S // W),
        in_specs=[pl.BlockSpec(memory_space=pltpu.SMEM),
                  pl.BlockSpec((nb, W, qw), lambda b, n: (b, n, 0)),
                  pl.BlockSpec((nb, W, kw), lambda b, n: (b, jnp.maximum(n - 1, 0), kcol)),
                  pl.BlockSpec((nb, W, kw), lambda b, n: (b, n, kcol)),
                  pl.BlockSpec((nb, W, kw), lambda b, n: (b, jnp.maximum(n - 1, 0), kcol + 1)),
                  pl.BlockSpec((nb, W, kw), lambda b, n: (b, n, kcol + 1))],
        out_specs=pl.BlockSpec((nb, W, qw), lambda b, n: (b, n, 0)),
        out_shape=jax.ShapeDtypeStruct((B, S, qw), BF16),
        compiler_params=_cparams(2),
        name="swa",
    )(sinks, qkv3, qkv3, qkv3, qkv3, qkv3)
    return out.reshape(B * S, qw)


def _rwkv_prep_math(p, last, mu_ref, w0_ref, w2_ref, a0_ref, a2_ref, g2_ref, kk_ref, ka_ref, outs):
    r_out, lw_out, k_out, v_out, a_out, b_out, g_out = outs
    C = RWKV_WIDTH
    row = lax.broadcasted_iota(I32, p.shape, 0)
    p_prev = jnp.where(row == 0, last, pltpu.roll(p, 1, axis=0))
    p = p + (p_prev - p) * mu_ref[...]
    r = p[:, :C]
    k = p[:, C:2 * C]
    v = p[:, 2 * C:3 * C]
    xw = p[:, 3 * C:3 * C + 64]
    xa = p[:, 3 * C + 64:3 * C + 128]
    xg = p[:, 3 * C + 128:]
    w = -_softplus(-(w0_ref[...] + _dot(jnp.tanh(xw), w2_ref[...]))) - 0.5
    lw = -jnp.exp(w)
    a = _sigmoid(a0_ref[...] + _dot(xa, a2_ref[...]))
    g = _dot(_sigmoid(xg), g2_ref[...])
    kk = k * kk_ref[...]
    pieces = []
    for h in range(RWKV_HEADS):
        kh = kk[:, h * HEAD_DIM:(h + 1) * HEAD_DIM]
        nrm = jnp.sqrt(jnp.sum(kh * kh, axis=-1, keepdims=True))
        pieces.append(kh / jnp.maximum(nrm, 1e-12))
    kk = jnp.concatenate(pieces, axis=1)
    r_out[...] = r
    lw_out[...] = lw
    k_out[...] = k * (1.0 + (a - 1.0) * ka_ref[...])
    v_out[...] = v
    a_out[...] = -kk
    b_out[...] = kk * a
    g_out[...] = g


def _rwkv_prep_kernel(p_ref, pprev_ref, *refs):
    n = pl.program_id(1)
    last = jnp.where(n > 0, pprev_ref[SUBLANES - 1:SUBLANES, :], 0.0)
    _rwkv_prep_math(p_ref[...], last, *refs[:8], refs[8:])


def _rwkv_params(mu, w0, w2, a0, a2, g2, k_k, k_a):
    row = lambda t: t.reshape(1, -1)
    return [row(mu), row(w0), w2.astype(BF16), row(a0), a2.astype(BF16), g2.astype(BF16), row(k_k), row(k_a)]


def _rwkv_prep(p, mu, w0, w2, a0, a2, g2, k_k, k_a, B, S, tt=256):
    assert S % tt == 0
    C = RWKV_WIDTH
    PW = p.shape[-1]
    p3 = p.reshape(B, S, PW)
    full = lambda arr: pl.BlockSpec(arr.shape, lambda b, n: (0,) * arr.ndim)
    params = _rwkv_params(mu, w0, w2, a0, a2, g2, k_k, k_a)
    outs = pl.pallas_call(
        _rwkv_prep_kernel,
        grid=(B, S // tt),
        in_specs=[pl.BlockSpec((None, tt, PW), lambda b, n: (b, n, 0)),
                  pl.BlockSpec((None, SUBLANES, PW),
                               lambda b, n: (b, jnp.maximum(n * (tt // SUBLANES) - 1, 0), 0))]
                 + [full(t) for t in params],
        out_specs=[pl.BlockSpec((None, tt, C), lambda b, n: (b, n, 0))] * 7,
        out_shape=[jax.ShapeDtypeStruct((B, S, C), F32)] * 7,
        compiler_params=_cparams(2),
        name="rwkv_prep",
    )(p3, p3, *params)
    return outs


def _in_proj_rwkv_kernel(x_ref, g_ref, w_ref, *refs, swa_cols, tiles_per_seq):
    params = refs[:8]
    qkv_out = refs[8]
    outs = refs[9:16]
    p_buf, last_buf = refs[16:]
    i = pl.program_id(0)

    @pl.when(i == 0)
    def _():
        p_buf[...] = jnp.zeros_like(p_buf)
        last_buf[...] = jnp.zeros_like(last_buf)

    j = i - 1
    p_prev_tile = p_buf[lax.rem(i + 1, 2)]
    tm = p_prev_tile.shape[0]
    last = jnp.where(lax.rem(j, tiles_per_seq) == 0, 0.0, last_buf[...])
    _rwkv_prep_math(p_prev_tile, last, *params, outs)
    last_buf[...] = p_prev_tile[tm - 1:tm, :]
    h = _rms(x_ref[...], g_ref[...]).astype(BF16)
    qkv_out[...] = jnp.dot(h, w_ref[:, :swa_cols], preferred_element_type=F32).astype(qkv_out.dtype)
    p_buf[lax.rem(i, 2)] = jnp.dot(h, w_ref[:, swa_cols:], preferred_element_type=F32)


def _in_proj_rwkv(x, g, w, swa_cols, mu, w0, w2, a0, a2, g2, k_k, k_a, S, tm=256):
    T, D = x.shape
    assert S % tm == 0 and T % S == 0
    N = w.shape[1]
    C = RWKV_WIDTH
    NT = T // tm
    params = _rwkv_params(mu, w0, w2, a0, a2, g2, k_k, k_a)
    const = lambda arr: pl.BlockSpec(arr.shape, lambda i: (0,) * arr.ndim)
    cur = lambda i: (jnp.minimum(i, NT - 1), 0)
    prev = lambda i: (jnp.maximum(i - 1, 0), 0)
    outs = pl.pallas_call(
        functools.partial(_in_proj_rwkv_kernel, swa_cols=swa_cols, tiles_per_seq=S // tm),
        grid=(NT + 1,),
        in_specs=[pl.BlockSpec((tm, D), cur), pl.BlockSpec((1, D), lambda i: (0, 0)), const(w)]
                 + [const(t) for t in params],
        out_specs=[pl.BlockSpec((tm, swa_cols), cur)] + [pl.BlockSpec((tm, C), prev)] * 7,
        out_shape=[jax.ShapeDtypeStruct((T, swa_cols), BF16)] + [jax.ShapeDtypeStruct((T, C), F32)] * 7,
        scratch_shapes=[pltpu.VMEM((2, tm, N - swa_cols), F32), pltpu.VMEM((1, N - swa_cols), F32)],
        compiler_params=_cparams(1),
        name="in_proj_rwkv",
    )(x, g.reshape(1, D), w, *params)
    return outs[0], outs[1:]


def _pair_blockdiag(x):
    lane = lax.broadcasted_iota(I32, x.shape, 1)
    zero = jnp.zeros_like(x)
    return jnp.concatenate([jnp.where(lane < HEAD_DIM, x, zero), jnp.where(lane >= HEAD_DIM, x, zero)], axis=0)


def _rwkv_scan_kernel(r_ref, lw_ref, k_ref, v_ref, a_ref, b_ref, g_ref, rk_ref, lnw_ref, lnb_ref,
                      o_ref, s_ref):
    c = pl.program_id(1)

    @pl.when(c == 0)
    def _():
        s_ref[...] = jnp.zeros_like(s_ref)

    C = RWKV_CHUNK
    NB = r_ref.shape[0]
    NP = RWKV_HEADS // 2
    PW = 2 * HEAD_DIM
    row = lax.broadcasted_iota(I32, (C, C), 0)
    col = lax.broadcasted_iota(I32, (C, C), 1)
    tri = jnp.where(row >= col, 1.0, 0.0).astype(F32)
    rowp = lax.broadcasted_iota(I32, (C, PW), 0)
    colp = lax.broadcasted_iota(I32, (C, PW), 1)
    colp = jnp.where(colp >= HEAD_DIM, colp - HEAD_DIM, colp)
    lower_p = rowp >= colp
    strict_p = rowp > colp
    rows = lax.broadcasted_iota(I32, (PW, PW), 0)
    cols = lax.broadcasted_iota(I32, (PW, PW), 1)
    same_head = jnp.where(rows >= HEAD_DIM, 1, 0) == jnp.where(cols >= HEAD_DIM, 1, 0)
    first = lax.broadcasted_iota(I32, (C, PW), 1) < HEAD_DIM

    streams = [(bi, p) for bi in range(NB) for p in range(NP)]
    pre = []
    for bi in range(NB):
        lw = lw_ref[bi]
        cum = _tri_cumsum(tri, lw)
        cum_last = cum[C - 1:C, :]
        r = r_ref[bi]
        k = k_ref[bi]
        v = v_ref[bi]
        a = a_ref[bi]
        b = b_ref[bi]
        e_neg = jnp.exp(-cum)
        e_rem = jnp.exp(cum_last - cum)
        pre.append(dict(
            r_t=(r * jnp.exp(cum)).astype(BF16), a_t=(a * jnp.exp(cum - lw)).astype(BF16),
            b_t=(b * e_neg).astype(BF16), k_t=(k * e_neg).astype(BF16),
            b_d=(b * e_rem).astype(BF16), k_d=(k * e_rem).astype(BF16),
            v_b=v.astype(BF16), v=v, e_last=jnp.exp(cum_last), rkk=r * k * rk_ref[...], g=g_ref[bi]))

    def lanes(p):
        return slice(p * PW, (p + 1) * PW)

    ar = [jnp.concatenate([pre[bi]['a_t'][:, lanes(p)], pre[bi]['r_t'][:, lanes(p)]], axis=0) for bi, p in streams]
    s0 = [s_ref[bi, p] for bi, p in streams]
    big = [_dot_nt(ar[i], jnp.concatenate([_pair_blockdiag(pre[bi]['b_t'][:, lanes(p)]),
                                           _pair_blockdiag(pre[bi]['k_t'][:, lanes(p)]),
                                           s0[i].astype(BF16)], axis=0))
           for i, (bi, p) in enumerate(streams)]
    m_b = [t[:, :PW] for t in big]
    m_k = [t[:, PW:2 * PW] for t in big]
    ars = [t[:, 2 * PW:] for t in big]
    v_p = [pre[bi]['v_b'][:, lanes(p)] for bi, p in streams]
    v_bd = [_pair_blockdiag(vp) for vp in v_p]
    x = [ars[i][:C] + _dot(jnp.where(strict_p, m_k[i][:C], 0.0), v_bd[i]) for i in range(len(streams))]
    pw = [jnp.where(strict_p, m_b[i][:C], 0.0).astype(BF16) for i in range(len(streams))]
    n_stages = 6
    for stage in range(n_stages):
        if stage < n_stages - 1:
            prod = [_dot(pw[i], jnp.concatenate([_pair_blockdiag(x[i].astype(BF16)), _pair_blockdiag(pw[i])], axis=1))
                    for i in range(len(streams))]
            x = [x[i] + prod[i][:, :PW] for i in range(len(streams))]
            pw = [prod[i][:, PW:].astype(BF16) for i in range(len(streams))]
        else:
            x = [x[i] + _dot(pw[i], _pair_blockdiag(x[i].astype(BF16))) for i in range(len(streams))]
    u_b = [xi.astype(BF16) for xi in x]
    y = [ars[i][C:]
         + _dot(jnp.concatenate([jnp.where(lower_p, m_b[i][C:], 0.0), jnp.where(lower_p, m_k[i][C:], 0.0)], axis=1),
                jnp.concatenate([_pair_blockdiag(u_b[i]), v_bd[i]], axis=0))
         for i in range(len(streams))]
    for i, (bi, p) in enumerate(streams):
        upd = _dot_tn(jnp.concatenate([u_b[i], v_p[i]], axis=0),
                      jnp.concatenate([pre[bi]['b_d'][:, lanes(p)], pre[bi]['k_d'][:, lanes(p)]], axis=0))
        s_ref[bi, p] = s0[i] * pre[bi]['e_last'][:, lanes(p)] + jnp.where(same_head, upd, 0.0)

    lnw = lnw_ref[...]
    lnb = lnb_ref[...]

    def head_sum(t):
        s1 = jnp.sum(jnp.where(first, t, 0.0), axis=-1, keepdims=True)
        s2 = jnp.sum(jnp.where(first, 0.0, t), axis=-1, keepdims=True)
        return jnp.where(first, s1, s2)

    for bi in range(NB):
        outs = []
        for p in range(NP):
            yi = y[bi * NP + p]
            mean = head_sum(yi) * (1.0 / HEAD_DIM)
            yc = yi - mean
            var = head_sum(yc * yc) * (1.0 / HEAD_DIM)
            yn = yc * lax.rsqrt(var + RWKV_LN_EPS) * lnw[:, lanes(p)] + lnb[:, lanes(p)]
            bonus = head_sum(pre[bi]['rkk'][:, lanes(p)]) * pre[bi]['v'][:, lanes(p)]
            outs.append((yn + bonus) * pre[bi]['g'][:, lanes(p)])
        o_ref[bi] = jnp.concatenate(outs, axis=1).astype(o_ref.dtype)


RWKV_BATCH_ROWS = 8


def _rwkv_scan(r, lw, k, v, a, b, g, r_k, lnx_w, lnx_b):
    B, S, W = r.shape
    C = RWKV_CHUNK
    assert S % C == 0 and W == RWKV_HEADS * HEAD_DIM
    nb = RWKV_BATCH_ROWS if B % RWKV_BATCH_ROWS == 0 else 1
    seq = pl.BlockSpec((nb, C, W), lambda bb, c: (bb, c, 0))
    par = pl.BlockSpec((1, W), lambda bb, c: (0, 0))
    out = pl.pallas_call(
        _rwkv_scan_kernel,
        grid=(B // nb, S // C),
        in_specs=[seq] * 7 + [par] * 3,
        out_specs=seq,
        out_shape=jax.ShapeDtypeStruct((B, S, W), BF16),
        scratch_shapes=[pltpu.VMEM((nb, RWKV_HEADS // 2, 2 * HEAD_DIM, 2 * HEAD_DIM), F32)],
        compiler_params=_cparams(2),
        name="rwkv_scan",
    )(r, lw, k, v, a, b, g, r_k.reshape(1, W), lnx_w.reshape(1, W), lnx_b.reshape(1, W))
    return out.reshape(B * S, W)


def _gla_kernel(q_ref, k_ref, v_ref, og_ref, gd_ref, gup_ref, gb_ref, on_ref, o_ref, s_ref):
    c = pl.program_id(1)

    @pl.when(c == 0)
    def _():
        s_ref[...] = jnp.zeros_like(s_ref)

    C = GLA_CHUNK
    NB = q_ref.shape[0]
    row = lax.broadcasted_iota(I32, (C, C), 0)
    col = lax.broadcasted_iota(I32, (C, C), 1)
    lower = row >= col
    tri = jnp.where(lower, 1.0, 0.0).astype(F32)
    onorm = on_ref[...]
    zs = [_dot(gd_ref[bi], gup_ref[...]) + gb_ref[...] for bi in range(NB)]
    cums = [_tri_cumsum(tri, -_softplus(-z) / GLA_GATE_NORM) for z in zs]
    qe, ke, kd, e_last, v = [], [], [], [], []
    for bi in range(NB):
        cum = cums[bi]
        cum_last = cum[C - 1:C, :]
        k = k_ref[bi]
        qe.append((q_ref[bi] * (GLA_DK ** -0.5) * jnp.exp(cum)).astype(BF16))
        ke.append((k * jnp.exp(-cum)).astype(BF16))
        kd.append((k * jnp.exp(cum_last - cum)).astype(BF16))
        e_last.append(jnp.exp(cum_last))
        v.append(v_ref[bi].astype(BF16))
    streams = [(bi, h) for bi in range(NB) for h in range(GLA_HEADS)]
    ks = lambda h: slice(h * GLA_DK, (h + 1) * GLA_DK)
    vs = lambda h: slice(h * GLA_DV, (h + 1) * GLA_DV)
    sts = [s_ref[bi, h] for bi, h in streams]
    atts = [jnp.where(lower, _dot_nt(qe[bi][:, ks(h)], ke[bi][:, ks(h)]), 0.0) for bi, h in streams]
    inters = [_dot_nt(qe[bi][:, ks(h)], sts[i]) for i, (bi, h) in enumerate(streams)]
    os_ = [inters[i] + _dot(atts[i], v[bi][:, vs(h)]) for i, (bi, h) in enumerate(streams)]
    for i, (bi, h) in enumerate(streams):
        s_ref[bi, h] = sts[i] * e_last[bi][:, ks(h)] + _dot_tn(v[bi][:, vs(h)], kd[bi][:, ks(h)])
    for bi in range(NB):
        og = og_ref[bi]
        outs = []
        for h in range(GLA_HEADS):
            gate = og[:, vs(h)]
            outs.append(_rms(os_[bi * GLA_HEADS + h], onorm) * (gate * _sigmoid(gate)))
        o_ref[bi] = jnp.concatenate(outs, axis=1).astype(o_ref.dtype)


GLA_BATCH_ROWS = 8


def _gla(qk, v, og, gd, gate_up_pad, gate_b, onorm, B, S):
    C = GLA_CHUNK
    assert S % C == 0
    KW = GLA_HEADS * GLA_DK
    VW = GLA_HEADS * GLA_DV
    seq3 = lambda t: t.reshape(B, S, t.shape[-1])
    gd3 = gd.reshape(B, S, LANES)
    nb = GLA_BATCH_ROWS if B % GLA_BATCH_ROWS == 0 else 1
    out = pl.pallas_call(
        _gla_kernel,
        grid=(B // nb, S // C),
        in_specs=[pl.BlockSpec((nb, C, KW), lambda b, c: (b, c, 0)),
                  pl.BlockSpec((nb, C, KW), lambda b, c: (b, c, 1)),
                  pl.BlockSpec((nb, C, VW), lambda b, c: (b, c, 0)),
                  pl.BlockSpec((nb, C, VW), lambda b, c: (b, c, 0)),
                  pl.BlockSpec((nb, C, LANES), lambda b, c: (b, c, 0)),
                  pl.BlockSpec((LANES, KW), lambda b, c: (0, 0)),
                  pl.BlockSpec((1, KW), lambda b, c: (0, 0)),
                  pl.BlockSpec((1, GLA_DV), lambda b, c: (0, 0))],
        out_specs=pl.BlockSpec((nb, C, VW), lambda b, c: (b, c, 0)),
        out_shape=jax.ShapeDtypeStruct((B, S, VW), BF16),
        scratch_shapes=[pltpu.VMEM((nb, GLA_HEADS, GLA_DV, GLA_DK), F32)],
        compiler_params=_cparams(2),
        name="gla",
    )(seq3(qk), seq3(qk), seq3(v), seq3(og), gd3, gate_up_pad, gate_b.reshape(1, KW), onorm.reshape(1, GLA_DV))
    return out.reshape(B * S, VW)


def _xattn_kernel(*refs, n_in):
    x_ref = refs[0]
    a_refs = refs[1:1 + n_in]
    w_refs = refs[1 + n_in:1 + 2 * n_in]
    g_ref, wq_ref, mk_ref, mv_ref, wo_ref, gm_ref, wr_ref, br_ref, o_ref, lg_ref = refs[1 + 2 * n_in:]
    tq = x_ref.shape[0]
    subs = [slice(r, r + XA_SUB_ROWS) for r in range(0, tq, XA_SUB_ROWS)]
    xs = [x_ref[sub, :] for sub in subs]
    for a_ref, w_ref in zip(a_refs, w_refs):
        xs = [x + jnp.dot(a_ref[sub, :], w_ref[...], preferred_element_type=F32) for x, sub in zip(xs, subs)]
    qs = [_dot(_rms(x, g_ref[...]), wq_ref[...]).astype(BF16) for x in xs]
    mk = mk_ref[...]
    mv = mv_ref[...]
    sls = [slice(hd * XA_HEAD_DIM, (hd + 1) * XA_HEAD_DIM) for hd in range(XA_HEADS)]
    scores = [[_dot_nt(q[:, sl], mk[:, sl]) for sl in sls] for q in qs]
    probs = []
    for sc in scores:
        ps = []
        for s in sc:
            s = s * (XA_HEAD_DIM ** -0.5)
            p = jnp.exp(s - jnp.max(s, axis=-1, keepdims=True))
            ps.append((p / jnp.sum(p, axis=-1, keepdims=True)).astype(BF16))
        probs.append(ps)
    os_ = [jnp.concatenate([_dot(p, mv[:, sl]) for p, sl in zip(ps, sls)], axis=1) for ps in probs]
    outs = [x + _dot(o, wo_ref[...]) for x, o in zip(xs, os_)]
    for out, sub in zip(outs, subs):
        o_ref[sub, :] = out
    for out, sub in zip(outs, subs):
        lg_ref[:, sub] = _dot_nt(wr_ref[...], _rms(out, gm_ref[...])) + br_ref[...]


XA_SUB_ROWS = 256


def _mix_proj_xattn(x, acts, weights, g, wq, mk, mv, wo, g_moe, wt_router, bt_router, B, S, tq=1024):
    D = x.shape[-1]
    assert S % tq == 0 and tq % XA_SUB_ROWS == 0
    M = mk.shape[0] // B
    XW = mk.shape[-1]
    n_in = len(acts)
    seq3 = lambda a: a.reshape(B, S, a.shape[-1])
    row_spec = lambda a: pl.BlockSpec((None, tq, a.shape[-1]), lambda b, n: (b, n, 0))
    const = lambda a: pl.BlockSpec(a.shape, lambda b, n: (0,) * a.ndim)
    out, logits = pl.pallas_call(
        functools.partial(_xattn_kernel, n_in=n_in),
        grid=(B, S // tq),
        in_specs=[row_spec(x)] + [row_spec(a) for a in acts] + [const(w) for w in weights]
                 + [pl.BlockSpec((1, D), lambda b, n: (0, 0)),
                    pl.BlockSpec((D, XW), lambda b, n: (0, 0)),
                    pl.BlockSpec((None, M, XW), lambda b, n: (b, 0, 0)),
                    pl.BlockSpec((None, M, XW), lambda b, n: (b, 0, 0)),
                    pl.BlockSpec((XW, D), lambda b, n: (0, 0)),
                    pl.BlockSpec((1, D), lambda b, n: (0, 0)),
                    pl.BlockSpec((LANES, D), lambda b, n: (0, 0)),
                    pl.BlockSpec((LANES, XA_SUB_ROWS), lambda b, n: (0, 0))],
        out_specs=[pl.BlockSpec((None, tq, D), lambda b, n: (b, n, 0)),
                   pl.BlockSpec((None, LANES, tq), lambda b, n: (b * (S // tq) + n, 0, 0))],
        out_shape=[jax.ShapeDtypeStruct((B, S, D), F32),
                   jax.ShapeDtypeStruct((B * S // tq, LANES, tq), F32)],
        compiler_params=_cparams(2),
        name="xattn",
    )(seq3(x), *[seq3(a) for a in acts], *weights, g.reshape(1, D), wq,
      mk.reshape(B, M, XW), mv.reshape(B, M, XW), wo, g_moe.reshape(1, D), wt_router, bt_router)
    return out.reshape(B * S, D), logits


ROUTER_ROWS = 40


def _router_kernel(lg_ref, info_ref, slot_ref, cnt_ref, carry_ref):
    i = pl.program_id(0)

    @pl.when(i == 0)
    def _():
        carry_ref[...] = jnp.zeros_like(carry_ref)

    logits = lg_ref[:ROUTER_ROWS, :]
    tm = logits.shape[1]
    row = lax.broadcasted_iota(I32, logits.shape, 0)
    big = jnp.int32(LANES)
    neg = -jnp.inf
    gl = jnp.where(row < MOE_GROUPS, logits, neg)
    gmax = jnp.max(gl, axis=0, keepdims=True)
    g_top = jnp.min(jnp.where(gl == gmax, row, big), axis=0, keepdims=True)
    p_group = 1.0 / jnp.sum(jnp.exp(gl - gmax), axis=0, keepdims=True)
    lo = MOE_GROUPS + MOE_EXPERTS_PER_GROUP * g_top
    in_group = jnp.where(row >= lo, jnp.where(row < lo + MOE_EXPERTS_PER_GROUP, 1, 0), 0) > 0
    el = jnp.where(in_group, logits, neg)
    emax = jnp.max(el, axis=0, keepdims=True)
    ee = jnp.exp(el - emax)
    prob = ee / jnp.sum(ee, axis=0, keepdims=True)
    prob = jnp.where(in_group, prob, -1.0)
    p1 = jnp.max(prob, axis=0, keepdims=True)
    i1 = jnp.min(jnp.where(prob == p1, row, big), axis=0, keepdims=True)
    rest = jnp.where(row == i1, -1.0, prob)
    p2 = jnp.max(rest, axis=0, keepdims=True)
    i2 = jnp.min(jnp.where(rest == p2, row, big), axis=0, keepdims=True)
    tot = p1 + p2
    g1 = p_group * p1 / tot
    g2 = p_group * p2 / tot
    oh = jnp.concatenate([jnp.where(row == i1, 1.0, 0.0), jnp.where(row == i2, 1.0, 0.0)], axis=0)
    tr = lax.broadcasted_iota(I32, (tm, tm), 0)
    tc = lax.broadcasted_iota(I32, (tm, tm), 1)
    pre = _dot(oh, jnp.where(tr < tc, 1.0, 0.0))
    tots = _dot(oh, jnp.ones((tm, LANES), F32))
    reps = tm // LANES
    carry = carry_ref[...]
    base1 = jnp.concatenate([carry] * reps, axis=1)
    base2 = jnp.concatenate([carry + tots[:ROUTER_ROWS]] * reps, axis=1)
    r1 = jnp.sum(oh[:ROUTER_ROWS] * (base1 + pre[:ROUTER_ROWS]), axis=0, keepdims=True)
    r2 = jnp.sum(oh[ROUTER_ROWS:] * (base2 + pre[ROUTER_ROWS:]), axis=0, keepdims=True)
    carry = carry + tots[:ROUTER_ROWS] + tots[ROUTER_ROWS:]
    carry_ref[...] = carry
    cnt_ref[...] = carry
    e1 = (i1 - MOE_GROUPS).astype(F32)
    e2 = (i2 - MOE_GROUPS).astype(F32)
    slot_rows = [e1, e2, r1, r2, g1, g2]
    rows8 = lax.broadcasted_iota(I32, (SUBLANES, tm), 0)
    slot = jnp.zeros((SUBLANES, tm), F32)
    for j, val in enumerate(slot_rows):
        slot = jnp.where(rows8 == j, val, slot)
    slot_ref[...] = slot
    wide = jnp.concatenate([slot, jnp.zeros((LANES - SUBLANES, tm), F32)], axis=0)
    info_ref[...] = jnp.transpose(wide)


def _router(logits, tm=256):
    n_row_tiles, _, tq = logits.shape
    assert tq % tm == 0 and tm % LANES == 0
    per = tq // tm
    T = n_row_tiles * tq
    NT = T // tm
    return pl.pallas_call(
        _router_kernel,
        grid=(NT,),
        in_specs=[pl.BlockSpec((None, LANES, tm), lambda i: (i // per, 0, i % per))],
        out_specs=[pl.BlockSpec((tm, LANES), lambda i: (i, 0)),
                   pl.BlockSpec((None, SUBLANES, tm), lambda i: (i, 0, 0)),
                   pl.BlockSpec((ROUTER_ROWS, LANES), lambda i: (0, 0))],
        out_shape=[jax.ShapeDtypeStruct((T, LANES), F32),
                   jax.ShapeDtypeStruct((NT, SUBLANES, tm), F32),
                   jax.ShapeDtypeStruct((ROUTER_ROWS, LANES), F32)],
        scratch_shapes=[pltpu.VMEM((ROUTER_ROWS, LANES), F32)],
        compiler_params=_cparams(1),
        name="router",
    )(logits)


def _row_bytes_wait(hbm, buf, sem):
    pltpu.make_async_copy(buf, hbm.at[pl.ds(0, buf.shape[0]), :], sem).wait()


def _to_row_tiles(ref, val):
    n = val.shape[0]
    for c in range(SUBLANES):
        ref[pl.ds(c, n, stride=SUBLANES), :] = val[:, c * LANES:(c + 1) * LANES]


def _from_row_tiles(ref):
    n = ref.shape[0] // SUBLANES
    return jnp.concatenate([ref[pl.ds(c, n, stride=SUBLANES), :] for c in range(SUBLANES)], axis=1)


def _moe_dispatch_kernel(pends_ref, cnt_ref, dest_ref, x_ref, g_ref, hs_hbm, hbuf, zbuf, sems, zsem, *, td):
    i = pl.program_id(0)
    nt = pl.num_programs(0)
    slot = lax.rem(i, 2)

    @pl.when(i == 0)
    def _():
        zbuf[...] = jnp.zeros_like(zbuf)
        for e in range(MOE_EXPERTS):
            @pl.when(cnt_ref[e] > 0)
            def _():
                start = pl.multiple_of((pends_ref[e] - MOE_BLOCK) * SUBLANES, MOE_BLOCK)
                pltpu.make_async_copy(zbuf, hs_hbm.at[pl.ds(start, MOE_BLOCK * SUBLANES), :], zsem).start()
        for e in range(MOE_EXPERTS):
            @pl.when(cnt_ref[e] > 0)
            def _():
                pltpu.make_async_copy(zbuf, hs_hbm.at[pl.ds(0, MOE_BLOCK * SUBLANES), :], zsem).wait()

        first_unused = pends_ref[MOE_EXPERTS - 1] // MOE_BLOCK
        n_blocks = hs_hbm.shape[0] // (MOE_BLOCK * SUBLANES)

        def zero_start(blk, carry):
            start = pl.multiple_of(blk * (MOE_BLOCK * SUBLANES), MOE_BLOCK)
            pltpu.make_async_copy(zbuf, hs_hbm.at[pl.ds(start, MOE_BLOCK * SUBLANES), :], zsem).start()
            return carry

        def zero_wait(blk, carry):
            pltpu.make_async_copy(zbuf, hs_hbm.at[pl.ds(0, MOE_BLOCK * SUBLANES), :], zsem).wait()
            return carry

        lax.fori_loop(first_unused, n_blocks, zero_start, 0)
        lax.fori_loop(first_unused, n_blocks, zero_wait, 0)

    hb = hbuf.at[slot]
    _to_row_tiles(hb, _rms(x_ref[...], g_ref[...]))
    for j in range(td):
        for c in range(2):
            row = pl.multiple_of(dest_ref[0, c * td + j] * SUBLANES, SUBLANES)
            pltpu.make_async_copy(hb.at[pl.ds(j * SUBLANES, SUBLANES), :],
                                  hs_hbm.at[pl.ds(row, SUBLANES), :],
                                  sems.at[slot]).start(priority=c)

    @pl.when(i > 0)
    def _():
        other = hbuf.at[1 - slot]
        _row_bytes_wait(hs_hbm, other, sems.at[1 - slot])
        _row_bytes_wait(hs_hbm, other, sems.at[1 - slot])

    @pl.when(i == nt - 1)
    def _():
        _row_bytes_wait(hs_hbm, hb, sems.at[slot])
        _row_bytes_wait(hs_hbm, hb, sems.at[slot])


def _moe_dispatch(x, g, pends, counts, dest3, P, td):
    T, D = x.shape
    assert D == ROW_TILE and T % td == 0
    grid_spec = pltpu.PrefetchScalarGridSpec(
        num_scalar_prefetch=2,
        grid=(T // td,),
        in_specs=[pl.BlockSpec((None, 1, 2 * td), lambda i, pe, cn: (i, 0, 0), memory_space=pltpu.SMEM),
                  pl.BlockSpec((td, D), lambda i, pe, cn: (i, 0)),
                  pl.BlockSpec((1, D), lambda i, pe, cn: (0, 0))],
        out_specs=pl.BlockSpec(memory_space=pl.ANY),
        scratch_shapes=[pltpu.VMEM((2, td * SUBLANES, LANES), F32),
                        pltpu.VMEM((MOE_BLOCK * SUBLANES, LANES), F32),
                        pltpu.SemaphoreType.DMA((2,)),
                        pltpu.SemaphoreType.DMA(())],
    )
    return pl.pallas_call(
        functools.partial(_moe_dispatch_kernel, td=td),
        grid_spec=grid_spec,
        out_shape=jax.ShapeDtypeStruct((P * SUBLANES, LANES), F32),
        compiler_params=_cparams(1),
        name="moe_dispatch",
    )(pends, counts, dest3, x, g.reshape(1, D))


def _moe_expert_kernel(be_ref, nu_ref, hs_ref, w1_ref, w3_ref, w2_ref, o_ref, w1b, w3b, w2b):
    i = pl.program_id(0)
    used = i < nu_ref[0]
    changed = jnp.logical_or(i == 0, be_ref[i] != be_ref[jnp.maximum(i - 1, 0)])

    @pl.when(jnp.logical_and(used, changed))
    def _():
        w1b[...] = w1_ref[...].astype(BF16)
        w3b[...] = w3_ref[...].astype(BF16)
        w2b[...] = w2_ref[...].astype(BF16)

    @pl.when(used)
    def _():
        xe = _from_row_tiles(hs_ref).astype(BF16)
        ff = w1b.shape[1]
        halves = [slice(0, ff // 2), slice(ff // 2, ff)]
        ups = [(jnp.dot(xe, w1b[:, sl], preferred_element_type=F32),
                jnp.dot(xe, w3b[:, sl], preferred_element_type=F32)) for sl in halves]
        act = [(a * _sigmoid(a) * b).astype(BF16) for a, b in ups]
        y = sum(jnp.dot(a, w2b[sl, :], preferred_element_type=F32) for a, sl in zip(act, halves))
        _to_row_tiles(o_ref, y)

    @pl.when(jnp.logical_not(used))
    def _():
        o_ref[...] = jnp.zeros_like(o_ref)


def _moe_experts(hs, block_e, n_used, w1, w3, w2, layer):
    P = hs.shape[0] // SUBLANES
    D = ROW_TILE
    FF = w1.shape[-1]
    NB = P // MOE_BLOCK
    last = lambda i, nu: jnp.minimum(i, nu[0] - 1)
    grid_spec = pltpu.PrefetchScalarGridSpec(
        num_scalar_prefetch=2,
        grid=(NB,),
        in_specs=[pl.BlockSpec((MOE_BLOCK * SUBLANES, LANES), lambda i, be, nu: (last(i, nu), 0)),
                  pl.BlockSpec((None, None, D, FF), lambda i, be, nu: (layer, be[last(i, nu)], 0, 0)),
                  pl.BlockSpec((None, None, D, FF), lambda i, be, nu: (layer, be[last(i, nu)], 0, 0)),
                  pl.BlockSpec((None, None, FF, D), lambda i, be, nu: (layer, be[last(i, nu)], 0, 0))],
        out_specs=pl.BlockSpec((MOE_BLOCK * SUBLANES, LANES), lambda i, be, nu: (i, 0)),
        scratch_shapes=[pltpu.VMEM((D, FF), BF16),
                        pltpu.VMEM((D, FF), BF16),
                        pltpu.VMEM((FF, D), BF16)],
    )
    return pl.pallas_call(
        _moe_expert_kernel,
        grid_spec=grid_spec,
        out_shape=jax.ShapeDtypeStruct((P * SUBLANES, LANES), F32),
        compiler_params=_cparams(1),
        name="moe_experts",
    )(block_e, n_used, hs, w1, w3, w2)


def _gather_rows(src_hbm, idx_ref, dst_ref, sem, first, last):
    for r in range(first, last):
        row = pl.multiple_of(idx_ref[0, r] * SUBLANES, SUBLANES)
        pltpu.make_async_copy(src_hbm.at[pl.ds(row, SUBLANES), :],
                              dst_ref.at[pl.ds(r * SUBLANES, SUBLANES), :], sem).start(priority=r % 2)


def _moe_combine_kernel(pos_ref, posn_ref, x_ref, info_ref, yb_hbm, g_ref, *rest, tc, final_norm, splits):
    if splits:
        w_ref, o_ref = rest[0], rest[1]
        p_refs = rest[2:2 + len(splits)]
        ybuf, sems = rest[2 + len(splits):]
    else:
        o_ref, ybuf, sems = rest
    i = pl.program_id(0)
    nb = pl.num_programs(0)
    slot = lax.rem(i, 2)

    @pl.when(i == 0)
    def _():
        def issue(r, carry):
            src = pl.multiple_of(pos_ref[0, r] * SUBLANES, SUBLANES)
            dst = pl.multiple_of(r * SUBLANES, SUBLANES)
            pltpu.make_async_copy(yb_hbm.at[pl.ds(src, SUBLANES), :],
                                  ybuf.at[0, pl.ds(dst, SUBLANES), :], sems.at[0]).start()
            return carry
        lax.fori_loop(0, 2 * tc, issue, 0)

    def wait_tile(s_):
        pltpu.make_async_copy(yb_hbm.at[pl.ds(0, 2 * tc * SUBLANES), :], ybuf.at[s_], sems.at[s_]).wait()

    if not splits:
        @pl.when(i + 1 < nb)
        def _():
            _gather_rows(yb_hbm, posn_ref, ybuf.at[1 - slot], sems.at[1 - slot], 0, 2 * tc)

    wait_tile(slot)
    info = info_ref[...]
    yb = ybuf.at[slot]
    y0 = _from_row_tiles(yb.at[pl.ds(0, tc * SUBLANES), :])
    y1 = _from_row_tiles(yb.at[pl.ds(tc * SUBLANES, tc * SUBLANES), :])
    out = x_ref[...] + (y0 * info[:, 4:5] + y1 * info[:, 5:6])
    if final_norm:
        out = _rms(out, g_ref[...])
    o_ref[...] = out
    if splits:
        h = _rms(out, g_ref[...]).astype(BF16)
        chunks = []
        for p_ref, n in zip(p_refs, splits):
            for c0 in range(0, n, PROJ_CHUNK):
                chunks.append((p_ref, c0, min(PROJ_CHUNK, n - c0)))
        per = -(-2 * tc // max(len(chunks) // 2, 1))
        off = 0
        for ci, (p_ref, c0, width) in enumerate(chunks):
            val = jnp.dot(h, w_ref[:, off:off + width], preferred_element_type=F32)
            _gather_rows(yb_hbm, posn_ref, ybuf.at[1 - slot], sems.at[1 - slot],
                         min(ci * per, 2 * tc), min((ci + 1) * per, 2 * tc))
            p_ref[:, c0:c0 + width] = val.astype(p_ref.dtype)
            off += width

        @pl.when(i == nb - 1)
        def _():
            wait_tile(1 - slot)


def _moe_combine(x, info, dest3, yb, g, final_norm, tc, next_proj=None):
    T, D = x.shape
    assert D == ROW_TILE and T % tc == 0
    NT = T // tc
    in_specs = [pl.BlockSpec((None, 1, 2 * tc), lambda i: (i, 0, 0), memory_space=pltpu.SMEM),
                pl.BlockSpec((None, 1, 2 * tc), lambda i: (jnp.minimum(i + 1, NT - 1), 0, 0),
                             memory_space=pltpu.SMEM),
                pl.BlockSpec((tc, D), lambda i: (i, 0)),
                pl.BlockSpec((tc, LANES), lambda i: (i, 0)),
                pl.BlockSpec(memory_space=pl.ANY),
                pl.BlockSpec((1, D), lambda i: (0, 0))]
    out_specs = [pl.BlockSpec((tc, D), lambda i: (i, 0))]
    out_shape = [jax.ShapeDtypeStruct((T, D), F32)]
    args = [dest3, dest3, x, info, yb, g.reshape(1, D)]
    splits = ()
    if next_proj is not None:
        w, splits, out_dtypes = next_proj
        assert not final_norm and sum(splits) == w.shape[1]
        in_specs.append(pl.BlockSpec(w.shape, lambda i: (0, 0)))
        args.append(w)
        out_specs += [pl.BlockSpec((tc, n), lambda i: (i, 0)) for n in splits]
        out_shape += [jax.ShapeDtypeStruct((T, n), dt) for n, dt in zip(splits, out_dtypes)]
    outs = pl.pallas_call(
        functools.partial(_moe_combine_kernel, tc=tc, final_norm=final_norm, splits=tuple(splits)),
        grid=(NT,),
        in_specs=in_specs,
        out_specs=out_specs,
        out_shape=out_shape,
        scratch_shapes=[pltpu.VMEM((2, 2 * tc * SUBLANES, LANES), F32), pltpu.SemaphoreType.DMA((2,))],
        compiler_params=_cparams(1),
        name="moe_combine",
    )(*args)
    return outs[0], tuple(outs[1:])


MOE_TILE = 512
MOE_COMBINE_TILE = 256
MOE_LAST_COMBINE_TILE = 256
PROJ_CHUNK = 256
ROUTER_TILE = 256


def _tile_slots(dest, tile):
    n_tiles = dest.shape[0] * dest.shape[2] // tile
    return jnp.concatenate([dest[:, 0, :].reshape(n_tiles, 1, tile), dest[:, 1, :].reshape(n_tiles, 1, tile)], axis=2)


def _router_params(w_group, b_group, w_expert, b_expert, lanes_out):
    D = w_group.shape[0]
    n_log = MOE_GROUPS + MOE_EXPERTS
    wt = jnp.zeros((LANES, D), F32).at[:MOE_GROUPS].set(w_group.T).at[MOE_GROUPS:n_log].set(w_expert.T)
    bt = jnp.zeros((LANES,), F32).at[:MOE_GROUPS].set(b_group).at[MOE_GROUPS:n_log].set(b_expert)
    return wt.astype(BF16), jnp.broadcast_to(bt[:, None], (LANES, lanes_out))


def _moe_layer(x, g, logits, w1, w3, w2, layer, g_out, final_norm, next_proj):
    T, D = x.shape
    n_log = MOE_GROUPS + MOE_EXPERTS
    info, slot, cnt = _router(logits, ROUTER_TILE)
    P = 2 * T + MOE_EXPERTS * MOE_BLOCK
    NB = P // MOE_BLOCK
    counts = cnt[MOE_GROUPS:n_log, 0].astype(I32)
    padded = (counts + MOE_BLOCK - 1) // MOE_BLOCK * MOE_BLOCK
    pends = jnp.cumsum(padded).astype(I32)
    pstarts = pends - padded
    block_start = jnp.arange(NB, dtype=I32) * MOE_BLOCK
    block_e = jnp.minimum(jnp.sum((pends[None, :] <= block_start[:, None]).astype(I32), axis=1),
                          MOE_EXPERTS - 1).astype(I32)
    n_used = (pends[-1:] // MOE_BLOCK).astype(I32)
    eid = slot[:, 0:2, :].astype(I32)
    expert_ids = jnp.arange(MOE_EXPERTS, dtype=I32)
    seg_start = jnp.sum(jnp.where(eid[..., None] == expert_ids, pstarts, 0), axis=-1)
    dest = seg_start + slot[:, 2:4, :].astype(I32)
    hs = _moe_dispatch(x, g, pends, counts, _tile_slots(dest, MOE_TILE), P, MOE_TILE)
    yb = _moe_experts(hs, block_e, n_used, w1, w3, w2, layer)
    tc = MOE_LAST_COMBINE_TILE if next_proj is None else MOE_COMBINE_TILE
    return _moe_combine(x, info, _tile_slots(dest, tc), yb, g_out, final_norm, tc, next_proj)


def kernel(x, mem, norm_mix, norm_xattn, norm_moe, norm_final, ev_w_in, ev_sinks, ev_mu, ev_w0, ev_w2, ev_a0, ev_a2, ev_g2, ev_k_k, ev_k_a, ev_r_k, ev_lnx_w, ev_lnx_b, ev_w_out, od_w_in, od_gate_up, od_gate_b, od_onorm, od_w_out, mem_norm, mem_wk, mem_wv, xa_wq, xa_wo, moe_w_group, moe_b_group, moe_w_expert, moe_b_expert, moe_w1, moe_w3, moe_w2):
    B, S, D = x.shape
    M = mem.shape[1]
    T = B * S
    depth = norm_mix.shape[0]
    xf = x.reshape(T, D)

    XW = XA_HEADS * XA_HEAD_DIM
    w_kv = jnp.concatenate([mem_wk, mem_wv], axis=1).astype(BF16)
    mk, mv = _norm_matmul(mem.reshape(B * M, D), mem_norm, w_kv, (XW, XW), (BF16, BF16))

    KW = GLA_HEADS * GLA_DK
    VW = GLA_HEADS * GLA_DV
    swa_cols = SWA_Q_HEADS * HEAD_DIM + 2 * (SWA_Q_HEADS // SWA_GROUP) * HEAD_DIM

    def in_proj(layer):
        i = layer // 2
        if layer % 2 == 0:
            return ev_w_in[i].astype(BF16), (swa_cols, ev_w_in.shape[-1] - swa_cols), (BF16, F32)
        R = od_gate_up.shape[1]
        w = od_w_in[i]
        w_re = jnp.concatenate([w[:, :2 * KW + VW], w[:, 2 * KW + VW + R:],
                                w[:, 2 * KW + VW:2 * KW + VW + R],
                                jnp.zeros((D, LANES - R), F32)], axis=1).astype(BF16)
        return w_re, (2 * KW, VW, VW, LANES), (F32, BF16, F32, F32)

    w_first, _, _ = in_proj(0)
    qkv0, rw0 = _in_proj_rwkv(xf, norm_mix[0], w_first, swa_cols, ev_mu[0], ev_w0[0], ev_w2[0], ev_a0[0], ev_a2[0],
                              ev_g2[0], ev_k_k[0], ev_k_a[0], S)
    proj = None
    for layer in range(depth):
        i = layer // 2
        if layer % 2 == 0:
            if layer == 0:
                qkv, rw = qkv0, [t.reshape(B, S, RWKV_WIDTH) for t in rw0]
            else:
                qkv, p_rw = proj
                rw = _rwkv_prep(p_rw, ev_mu[i], ev_w0[i], ev_w2[i], ev_a0[i], ev_a2[i],
                                ev_g2[i], ev_k_k[i], ev_k_a[i], B, S)
            o_a = _swa(qkv, ev_sinks[i], B, S)
            o_b = _rwkv_scan(*rw, ev_r_k[i].reshape(-1), ev_lnx_w[i], ev_lnx_b[i])
            w_out = ev_w_out[i].astype(BF16)
            qw = o_a.shape[-1]
            mix_acts, mix_ws = [o_a, o_b], [w_out[:qw], w_out[qw:]]
        else:
            qk, v_gla, og, gd = proj
            R = od_gate_up.shape[1]
            gup = jnp.zeros((LANES, KW), F32).at[:R].set(od_gate_up[i]).astype(BF16)
            o = _gla(qk, v_gla, og, gd, gup, od_gate_b[i], od_onorm[i], B, S)
            mix_acts, mix_ws = [o], [od_w_out[i].astype(BF16)]
        wt_router, bt_router = _router_params(moe_w_group[layer], moe_b_group[layer], moe_w_expert[layer],
                                              moe_b_expert[layer], XA_SUB_ROWS)
        xf, logits = _mix_proj_xattn(xf, mix_acts, mix_ws, norm_xattn[layer], xa_wq[layer].astype(BF16), mk, mv,
                                     xa_wo[layer].astype(BF16), norm_moe[layer], wt_router, bt_router, B, S)
        last = layer == depth - 1
        g_out = norm_final if last else norm_mix[layer + 1]
        xf, proj = _moe_layer(xf, norm_moe[layer], logits, moe_w1, moe_w3, moe_w2, layer,
                              g_out, last, None if last else in_proj(layer + 1))
    return xf.reshape(B, S, D)
```

```python
import functools
import math

import jax
import jax.numpy as jnp
from jax import lax
from jax.experimental import pallas as pl
from jax.experimental.pallas import tpu as pltpu

F32 = jnp.float32
BF16 = jnp.bfloat16
I32 = jnp.int32

EPS = 1e-6
HEAD_DIM = 64
SWA_WINDOW = 128
SWA_Q_HEADS = 8
SWA_GROUP = 4
RWKV_HEADS = 8
RWKV_WIDTH = 512
RWKV_LN_EPS = 64e-5
RWKV_CHUNK = 64
RWKV_DECAY_SCALE = math.exp(-0.5)
GLA_HEADS = 4
GLA_DK = 128
GLA_DV = 256
GLA_CHUNK = 64
GLA_GATE_NORM = 16.0
XA_HEADS = 4
XA_HEAD_DIM = 128
MOE_GROUPS = 4
MOE_EXPERTS_PER_GROUP = 8
MOE_EXPERTS = 32
MOE_BLOCK = 512
LANES = 128
SUBLANES = 8
ROW_TILE = SUBLANES * LANES

VMEM_LIMIT_BYTES = 48 * 1024 * 1024


def _cparams(n_axes):
    return pltpu.CompilerParams(dimension_semantics=("arbitrary",) * n_axes,
                                vmem_limit_bytes=VMEM_LIMIT_BYTES)


def _dot(a, b):
    return jnp.dot(a.astype(BF16), b.astype(BF16), preferred_element_type=F32)


def _dot_nt(a, b):
    return lax.dot_general(a.astype(BF16), b.astype(BF16), (((1,), (1,)), ((), ())),
                           preferred_element_type=F32)


def _dot_tn(a, b):
    return lax.dot_general(a.astype(BF16), b.astype(BF16), (((0,), (0,)), ((), ())),
                           preferred_element_type=F32)


def _tri_cumsum(tri, x):
    hi = x.astype(BF16)
    rest = x - hi.astype(F32)
    mid = rest.astype(BF16)
    lo = (rest - mid.astype(F32)).astype(BF16)
    w = x.shape[1]
    sums = jnp.dot(tri.astype(BF16), jnp.concatenate([hi, mid, lo], axis=1), preferred_element_type=F32)
    return sums[:, :w] + sums[:, w:2 * w] + sums[:, 2 * w:]


def _rms(x, g):
    ms = jnp.mean(x * x, axis=-1, keepdims=True)
    return x * lax.rsqrt(ms + EPS) * g


def _sigmoid(x):
    return 1.0 / (1.0 + jnp.exp(-x))


def _softplus(x):
    return jnp.maximum(x, 0.0) + jnp.log(1.0 + jnp.exp(-jnp.abs(x)))


def _norm_matmul_kernel(x_ref, g_ref, w_ref, *o_refs, splits):
    h = _rms(x_ref[...], g_ref[...]).astype(BF16)
    off = 0
    for o_ref, n in zip(o_refs, splits):
        o_ref[...] = jnp.dot(h, w_ref[:, off:off + n], preferred_element_type=F32).astype(o_ref.dtype)
        off += n


def _norm_matmul(x, g, w, splits, out_dtypes, tm=256):
    T, D = x.shape
    N = w.shape[1]
    assert sum(splits) == N and T % tm == 0
    return pl.pallas_call(
        functools.partial(_norm_matmul_kernel, splits=tuple(splits)),
        grid=(T // tm,),
        in_specs=[pl.BlockSpec((tm, D), lambda i: (i, 0)),
                  pl.BlockSpec((1, D), lambda i: (0, 0)),
                  pl.BlockSpec((D, N), lambda i: (0, 0))],
        out_specs=[pl.BlockSpec((tm, n), lambda i: (i, 0)) for n in splits],
        out_shape=[jax.ShapeDtypeStruct((T, n), dt) for n, dt in zip(splits, out_dtypes)],
        compiler_params=_cparams(1),
        name="norm_matmul",
    )(x, g.reshape(1, D), w)


def _swa_kernel(sinks_ref, q_ref, kp_ref, kc_ref, vp_ref, vc_ref, o_ref):
    n = pl.program_id(1)
    W = SWA_WINDOW
    NB = q_ref.shape[0]
    qpos = lax.broadcasted_iota(I32, (W, 2 * W), 0) + W
    kpos = lax.broadcasted_iota(I32, (W, 2 * W), 1)
    rel = qpos - kpos
    in_window = jnp.where(rel >= 0, jnp.where(rel < W, 1, 0), 0)
    has_prev = jnp.where(n > 0, 1, 0)
    valid = (in_window * jnp.where(kpos >= W, 1, has_prev)) > 0
    n_groups = SWA_Q_HEADS // SWA_GROUP
    streams = [(bi, g) for bi in range(NB) for g in range(n_groups)]
    qb = [q_ref[bi].astype(BF16) for bi in range(NB)]
    kb = [jnp.concatenate([kp_ref[bi], kc_ref[bi]], axis=0).astype(BF16) for bi in range(NB)]
    vb = [jnp.concatenate([vp_ref[bi], vc_ref[bi]], axis=0).astype(BF16) for bi in range(NB)]
    gs = lambda g: slice(g * HEAD_DIM, (g + 1) * HEAD_DIM)
    scores = []
    for bi, g in streams:
        qg = jnp.concatenate([qb[bi][:, h * HEAD_DIM:(h + 1) * HEAD_DIM]
                              for h in range(g * SWA_GROUP, (g + 1) * SWA_GROUP)], axis=0)
        scores.append(_dot_nt(qg, kb[bi][:, gs(g)]))
    probs = []
    for i, (bi, g) in enumerate(streams):
        pieces = []
        for j in range(SWA_GROUP):
            s = jnp.where(valid, scores[i][j * W:(j + 1) * W] * (HEAD_DIM ** -0.5), -jnp.inf)
            sink = sinks_ref[g * SWA_GROUP + j]
            m = jnp.maximum(jnp.max(s, axis=-1, keepdims=True), sink)
            p = jnp.exp(s - m)
            den = jnp.sum(p, axis=-1, keepdims=True) + jnp.exp(sink - m)
            pieces.append((p / den).astype(BF16))
        probs.append(jnp.concatenate(pieces, axis=0))
    ogs = [_dot(probs[i], vb[bi][:, gs(g)]) for i, (bi, g) in enumerate(streams)]
    for bi in range(NB):
        outs = []
        for g in range(n_groups):
            og = ogs[bi * n_groups + g]
            outs += [og[j * W:(j + 1) * W] for j in range(SWA_GROUP)]
        o_ref[bi] = jnp.concatenate(outs, axis=1).astype(o_ref.dtype)


SWA_BATCH_ROWS = 2


def _swa(qkv, sinks, B, S):
    W = SWA_WINDOW
    assert S % W == 0
    qkv3 = qkv.reshape(B, S, qkv.shape[-1])
    qw = SWA_Q_HEADS * HEAD_DIM
    kw = qw // SWA_GROUP
    kcol = qw // kw
    nb = SWA_BATCH_ROWS if B % SWA_BATCH_ROWS == 0 else 1
    out = pl.pallas_call(
        _swa_kernel,
        grid=(B // nb, S // W),
        in_specs=[pl.BlockSpec(memory_space=pltpu.SMEM),
                  pl.BlockSpec((nb, W, qw), lambda b, n: (b, n, 0)),
                  pl.BlockSpec((nb, W, kw), lambda b, n: (b, jnp.maximum(n - 1, 0), kcol)),
                  pl.BlockSpec((nb, W, kw), lambda b, n: (b, n, kcol)),
                  pl.BlockSpec((nb, W, kw), lambda b, n: (b, jnp.maximum(n - 1, 0), kcol + 1)),
                  pl.BlockSpec((nb, W, kw), lambda b, n: (b, n, kcol + 1))],
        out_specs=pl.BlockSpec((nb, W, qw), lambda b, n: (b, n, 0)),
        out_shape=jax.ShapeDtypeStruct((B, S, qw), BF16),
        compiler_params=_cparams(2),
        name="swa",
    )(sinks, qkv3, qkv3, qkv3, qkv3, qkv3)
    return out.reshape(B * S, qw)


def _rwkv_prep_math(p, last, mu_ref, w0_ref, w2_ref, a0_ref, a2_ref, g2_ref, kk_ref, ka_ref, outs):
    r_out, lw_out, k_out, v_out, a_out, b_out, g_out = outs
    C = RWKV_WIDTH
    row = lax.broadcasted_iota(I32, p.shape, 0)
    p_prev = jnp.where(row == 0, last, pltpu.roll(p, 1, axis=0))
    p = p + (p_prev - p) * mu_ref[...]
    r = p[:, :C]
    k = p[:, C:2 * C]
    v = p[:, 2 * C:3 * C]
    xw = p[:, 3 * C:3 * C + 64]
    xa = p[:, 3 * C + 64:3 * C + 128]
    xg = p[:, 3 * C + 128:]
    z = w0_ref[...] + _dot(jnp.tanh(xw), w2_ref[...])
    lw = -RWKV_DECAY_SCALE * _sigmoid(z)
    a = _sigmoid(a0_ref[...] + _dot(xa, a2_ref[...]))
    g = _dot(_sigmoid(xg), g2_ref[...])
    kk = k * kk_ref[...]
    pieces = []
    for h in range(RWKV_HEADS):
        kh = kk[:, h * HEAD_DIM:(h + 1) * HEAD_DIM]
        nrm = jnp.sqrt(jnp.sum(kh * kh, axis=-1, keepdims=True))
        pieces.append(kh / jnp.maximum(nrm, 1e-12))
    kk = jnp.concatenate(pieces, axis=1)
    r_out[...] = r
    lw_out[...] = lw
    k_out[...] = k * (1.0 + (a - 1.0) * ka_ref[...])
    v_out[...] = v
    a_out[...] = -kk
    b_out[...] = kk * a
    g_out[...] = g


def _rwkv_prep_kernel(p_ref, pprev_ref, *refs):
    n = pl.program_id(1)
    last = jnp.where(n > 0, pprev_ref[SUBLANES - 1:SUBLANES, :], 0.0)
    _rwkv_prep_math(p_ref[...], last, *refs[:8], refs[8:])


def _rwkv_params(mu, w0, w2, a0, a2, g2, k_k, k_a):
    row = lambda t: t.reshape(1, -1)
    return [row(mu), row(w0), w2.astype(BF16), row(a0), a2.astype(BF16), g2.astype(BF16), row(k_k), row(k_a)]


def _rwkv_prep(p, mu, w0, w2, a0, a2, g2, k_k, k_a, B, S, tt=256):
    assert S % tt == 0
    C = RWKV_WIDTH
    PW = p.shape[-1]
    p3 = p.reshape(B, S, PW)
    full = lambda arr: pl.BlockSpec(arr.shape, lambda b, n: (0,) * arr.ndim)
    params = _rwkv_params(mu, w0, w2, a0, a2, g2, k_k, k_a)
    outs = pl.pallas_call(
        _rwkv_prep_kernel,
        grid=(B, S // tt),
        in_specs=[pl.BlockSpec((None, tt, PW), lambda b, n: (b, n, 0)),
                  pl.BlockSpec((None, SUBLANES, PW),
                               lambda b, n: (b, jnp.maximum(n * (tt // SUBLANES) - 1, 0), 0))]
                 + [full(t) for t in params],
        out_specs=[pl.BlockSpec((None, tt, C), lambda b, n: (b, n, 0))] * 7,
        out_shape=[jax.ShapeDtypeStruct((B, S, C), F32)] * 7,
        compiler_params=_cparams(2),
        name="rwkv_prep",
    )(p3, p3, *params)
    return outs


def _in_proj_rwkv_kernel(x_ref, g_ref, w_ref, *refs, swa_cols, tiles_per_seq):
    params = refs[:8]
    qkv_out = refs[8]
    outs = refs[9:16]
    p_buf, last_buf = refs[16:]
    i = pl.program_id(0)

    @pl.when(i == 0)
    def _():
        p_buf[...] = jnp.zeros_like(p_buf)
        last_buf[...] = jnp.zeros_like(last_buf)

    j = i - 1
    p_prev_tile = p_buf[lax.rem(i + 1, 2)]
    tm = p_prev_tile.shape[0]
    last = jnp.where(lax.rem(j, tiles_per_seq) == 0, 0.0, last_buf[...])
    _rwkv_prep_math(p_prev_tile, last, *params, outs)
    last_buf[...] = p_prev_tile[tm - 1:tm, :]
    h = _rms(x_ref[...], g_ref[...]).astype(BF16)
    qkv_out[...] = jnp.dot(h, w_ref[:, :swa_cols], preferred_element_type=F32).astype(qkv_out.dtype)
    p_buf[lax.rem(i, 2)] = jnp.dot(h, w_ref[:, swa_cols:], preferred_element_type=F32)


def _in_proj_rwkv(x, g, w, swa_cols, mu, w0, w2, a0, a2, g2, k_k, k_a, S, tm=256):
    T, D = x.shape
    assert S % tm == 0 and T % S == 0
    N = w.shape[1]
    C = RWKV_WIDTH
    NT = T // tm
    params = _rwkv_params(mu, w0, w2, a0, a2, g2, k_k, k_a)
    const = lambda arr: pl.BlockSpec(arr.shape, lambda i: (0,) * arr.ndim)
    cur = lambda i: (jnp.minimum(i, NT - 1), 0)
    prev = lambda i: (jnp.maximum(i - 1, 0), 0)
    outs = pl.pallas_call(
        functools.partial(_in_proj_rwkv_kernel, swa_cols=swa_cols, tiles_per_seq=S // tm),
        grid=(NT + 1,),
        in_specs=[pl.BlockSpec((tm, D), cur), pl.BlockSpec((1, D), lambda i: (0, 0)), const(w)]
                 + [const(t) for t in params],
        out_specs=[pl.BlockSpec((tm, swa_cols), cur)] + [pl.BlockSpec((tm, C), prev)] * 7,
        out_shape=[jax.ShapeDtypeStruct((T, swa_cols), BF16)] + [jax.ShapeDtypeStruct((T, C), F32)] * 7,
        scratch_shapes=[pltpu.VMEM((2, tm, N - swa_cols), F32), pltpu.VMEM((1, N - swa_cols), F32)],
        compiler_params=_cparams(1),
        name="in_proj_rwkv",
    )(x, g.reshape(1, D), w, *params)
    return outs[0], outs[1:]


def _pair_blockdiag(x):
    lane = lax.broadcasted_iota(I32, x.shape, 1)
    zero = jnp.zeros_like(x)
    return jnp.concatenate([jnp.where(lane < HEAD_DIM, x, zero), jnp.where(lane >= HEAD_DIM, x, zero)], axis=0)


def _rwkv_scan_kernel(r_ref, lw_ref, k_ref, v_ref, a_ref, b_ref, g_ref, rk_ref, lnw_ref, lnb_ref,
                      o_ref, s_ref):
    c = pl.program_id(1)

    @pl.when(c == 0)
    def _():
        s_ref[...] = jnp.zeros_like(s_ref)

    C = RWKV_CHUNK
    NB = r_ref.shape[0]
    NP = RWKV_HEADS // 2
    PW = 2 * HEAD_DIM
    row = lax.broadcasted_iota(I32, (C, C), 0)
    col = lax.broadcasted_iota(I32, (C, C), 1)
    tri = jnp.where(row >= col, 1.0, 0.0).astype(F32)
    rowp = lax.broadcasted_iota(I32, (C, PW), 0)
    colp = lax.broadcasted_iota(I32, (C, PW), 1)
    colp = jnp.where(colp >= HEAD_DIM, colp - HEAD_DIM, colp)
    lower_p = rowp >= colp
    strict_p = rowp > colp
    rows = lax.broadcasted_iota(I32, (PW, PW), 0)
    cols = lax.broadcasted_iota(I32, (PW, PW), 1)
    same_head = jnp.where(rows >= HEAD_DIM, 1, 0) == jnp.where(cols >= HEAD_DIM, 1, 0)
    first = lax.broadcasted_iota(I32, (C, PW), 1) < HEAD_DIM

    streams = [(bi, p) for bi in range(NB) for p in range(NP)]
    pre = []
    for bi in range(NB):
        lw = lw_ref[bi]
        cum = _tri_cumsum(tri, lw)
        cum_last = cum[C - 1:C, :]
        r = r_ref[bi]
        k = k_ref[bi]
        v = v_ref[bi]
        a = a_ref[bi]
        b = b_ref[bi]
        e_neg = jnp.exp(-cum)
        e_rem = jnp.exp(cum_last - cum)
        pre.append(dict(
            r_t=(r * jnp.exp(cum)).astype(BF16), a_t=(a * jnp.exp(cum - lw)).astype(BF16),
            b_t=(b * e_neg).astype(BF16), k_t=(k * e_neg).astype(BF16),
            b_d=(b * e_rem).astype(BF16), k_d=(k * e_rem).astype(BF16),
            v_b=v.astype(BF16), v=v, e_last=jnp.exp(cum_last), rkk=r * k * rk_ref[...], g=g_ref[bi]))

    def lanes(p):
        return slice(p * PW, (p + 1) * PW)

    ar = [jnp.concatenate([pre[bi]['a_t'][:, lanes(p)], pre[bi]['r_t'][:, lanes(p)]], axis=0) for bi, p in streams]
    s0 = [s_ref[bi, p] for bi, p in streams]
    big = [_dot_nt(ar[i], jnp.concatenate([_pair_blockdiag(pre[bi]['b_t'][:, lanes(p)]),
                                           _pair_blockdiag(pre[bi]['k_t'][:, lanes(p)]),
                                           s0[i].astype(BF16)], axis=0))
           for i, (bi, p) in enumerate(streams)]
    m_b = [t[:, :PW] for t in big]
    m_k = [t[:, PW:2 * PW] for t in big]
    ars = [t[:, 2 * PW:] for t in big]
    v_p = [pre[bi]['v_b'][:, lanes(p)] for bi, p in streams]
    v_bd = [_pair_blockdiag(vp) for vp in v_p]
    x = [ars[i][:C] + _dot(jnp.where(strict_p, m_k[i][:C], 0.0), v_bd[i]) for i in range(len(streams))]
    pw = [jnp.where(strict_p, m_b[i][:C], 0.0).astype(BF16) for i in range(len(streams))]
    n_stages = 6
    for stage in range(n_stages):
        if stage < n_stages - 1:
            prod = [_dot(pw[i], jnp.concatenate([_pair_blockdiag(x[i].astype(BF16)), _pair_blockdiag(pw[i])], axis=1))
                    for i in range(len(streams))]
            x = [x[i] + prod[i][:, :PW] for i in range(len(streams))]
            pw = [prod[i][:, PW:].astype(BF16) for i in range(len(streams))]
        else:
            x = [x[i] + _dot(pw[i], _pair_blockdiag(x[i].astype(BF16))) for i in range(len(streams))]
    u_b = [xi.astype(BF16) for xi in x]
    y = [ars[i][C:]
         + _dot(jnp.concatenate([jnp.where(lower_p, m_b[i][C:], 0.0), jnp.where(lower_p, m_k[i][C:], 0.0)], axis=1),
                jnp.concatenate([_pair_blockdiag(u_b[i]), v_bd[i]], axis=0))
         for i in range(len(streams))]
    for i, (bi, p) in enumerate(streams):
        upd = _dot_tn(jnp.concatenate([u_b[i], v_p[i]], axis=0),
                      jnp.concatenate([pre[bi]['b_d'][:, lanes(p)], pre[bi]['k_d'][:, lanes(p)]], axis=0))
        s_ref[bi, p] = s0[i] * pre[bi]['e_last'][:, lanes(p)] + jnp.where(same_head, upd, 0.0)

    lnw = lnw_ref[...]
    lnb = lnb_ref[...]

    def head_sum(t):
        s1 = jnp.sum(jnp.where(first, t, 0.0), axis=-1, keepdims=True)
        s2 = jnp.sum(jnp.where(first, 0.0, t), axis=-1, keepdims=True)
        return jnp.where(first, s1, s2)

    for bi in range(NB):
        outs = []
        for p in range(NP):
            yi = y[bi * NP + p]
            mean = head_sum(yi) * (1.0 / HEAD_DIM)
            yc = yi - mean
            var = head_sum(yc * yc) * (1.0 / HEAD_DIM)
            yn = yc * lax.rsqrt(var + RWKV_LN_EPS) * lnw[:, lanes(p)] + lnb[:, lanes(p)]
            bonus = head_sum(pre[bi]['rkk'][:, lanes(p)]) * pre[bi]['v'][:, lanes(p)]
            outs.append((yn + bonus) * pre[bi]['g'][:, lanes(p)])
        o_ref[bi] = jnp.concatenate(outs, axis=1).astype(o_ref.dtype)


RWKV_BATCH_ROWS = 8


def _rwkv_scan(r, lw, k, v, a, b, g, r_k, lnx_w, lnx_b):
    B, S, W = r.shape
    C = RWKV_CHUNK
    assert S % C == 0 and W == RWKV_HEADS * HEAD_DIM
    nb = RWKV_BATCH_ROWS if B % RWKV_BATCH_ROWS == 0 else 1
    seq = pl.BlockSpec((nb, C, W), lambda bb, c: (bb, c, 0))
    par = pl.BlockSpec((1, W), lambda bb, c: (0, 0))
    out = pl.pallas_call(
        _rwkv_scan_kernel,
        grid=(B // nb, S // C),
        in_specs=[seq] * 7 + [par] * 3,
        out_specs=seq,
        out_shape=jax.ShapeDtypeStruct((B, S, W), BF16),
        scratch_shapes=[pltpu.VMEM((nb, RWKV_HEADS // 2, 2 * HEAD_DIM, 2 * HEAD_DIM), F32)],
        compiler_params=_cparams(2),
        name="rwkv_scan",
    )(r, lw, k, v, a, b, g, r_k.reshape(1, W), lnx_w.reshape(1, W), lnx_b.reshape(1, W))
    return out.reshape(B * S, W)


def _gla_kernel(q_ref, k_ref, v_ref, og_ref, gd_ref, gup_ref, gb_ref, on_ref, o_ref, s_ref):
    c = pl.program_id(1)

    @pl.when(c == 0)
    def _():
        s_ref[...] = jnp.zeros_like(s_ref)

    C = GLA_CHUNK
    NB = q_ref.shape[0]
    row = lax.broadcasted_iota(I32, (C, C), 0)
    col = lax.broadcasted_iota(I32, (C, C), 1)
    lower = row >= col
    tri = jnp.where(lower, 1.0, 0.0).astype(F32)
    onorm = on_ref[...]
    zs = [_dot(gd_ref[bi], gup_ref[...]) + gb_ref[...] for bi in range(NB)]
    cums = [_tri_cumsum(tri, -_softplus(-z) / GLA_GATE_NORM) for z in zs]
    qe, ke, kd, e_last, v = [], [], [], [], []
    for bi in range(NB):
        cum = cums[bi]
        cum_last = cum[C - 1:C, :]
        k = k_ref[bi]
        qe.append((q_ref[bi] * (GLA_DK ** -0.5) * jnp.exp(cum)).astype(BF16))
        ke.append((k * jnp.exp(-cum)).astype(BF16))
        kd.append((k * jnp.exp(cum_last - cum)).astype(BF16))
        e_last.append(jnp.exp(cum_last))
        v.append(v_ref[bi].astype(BF16))
    streams = [(bi, h) for bi in range(NB) for h in range(GLA_HEADS)]
    ks = lambda h: slice(h * GLA_DK, (h + 1) * GLA_DK)
    vs = lambda h: slice(h * GLA_DV, (h + 1) * GLA_DV)
    sts = [s_ref[bi, h] for bi, h in streams]
    atts = [jnp.where(lower, _dot_nt(qe[bi][:, ks(h)], ke[bi][:, ks(h)]), 0.0) for bi, h in streams]
    inters = [_dot_nt(qe[bi][:, ks(h)], sts[i]) for i, (bi, h) in enumerate(streams)]
    os_ = [inters[i] + _dot(atts[i], v[bi][:, vs(h)]) for i, (bi, h) in enumerate(streams)]
    for i, (bi, h) in enumerate(streams):
        s_ref[bi, h] = sts[i] * e_last[bi][:, ks(h)] + _dot_tn(v[bi][:, vs(h)], kd[bi][:, ks(h)])
    for bi in range(NB):
        og = og_ref[bi]
        outs = []
        for h in range(GLA_HEADS):
            gate = og[:, vs(h)]
            outs.append(_rms(os_[bi * GLA_HEADS + h], onorm) * (gate * _sigmoid(gate)))
        o_ref[bi] = jnp.concatenate(outs, axis=1).astype(o_ref.dtype)


GLA_BATCH_ROWS = 8


def _gla(qk, v, og, gd, gate_up_pad, gate_b, onorm, B, S):
    C = GLA_CHUNK
    assert S % C == 0
    KW = GLA_HEADS * GLA_DK
    VW = GLA_HEADS * GLA_DV
    seq3 = lambda t: t.reshape(B, S, t.shape[-1])
    gd3 = gd.reshape(B, S, LANES)
    nb = GLA_BATCH_ROWS if B % GLA_BATCH_ROWS == 0 else 1
    out = pl.pallas_call(
        _gla_kernel,
        grid=(B // nb, S // C),
        in_specs=[pl.BlockSpec((nb, C, KW), lambda b, c: (b, c, 0)),
                  pl.BlockSpec((nb, C, KW), lambda b, c: (b, c, 1)),
                  pl.BlockSpec((nb, C, VW), lambda b, c: (b, c, 0)),
                  pl.BlockSpec((nb, C, VW), lambda b, c: (b, c, 0)),
                  pl.BlockSpec((nb, C, LANES), lambda b, c: (b, c, 0)),
                  pl.BlockSpec((LANES, KW), lambda b, c: (0, 0)),
                  pl.BlockSpec((1, KW), lambda b, c: (0, 0)),
                  pl.BlockSpec((1, GLA_DV), lambda b, c: (0, 0))],
        out_specs=pl.BlockSpec((nb, C, VW), lambda b, c: (b, c, 0)),
        out_shape=jax.ShapeDtypeStruct((B, S, VW), BF16),
        scratch_shapes=[pltpu.VMEM((nb, GLA_HEADS, GLA_DV, GLA_DK), F32)],
        compiler_params=_cparams(2),
        name="gla",
    )(seq3(qk), seq3(qk), seq3(v), seq3(og), gd3, gate_up_pad, gate_b.reshape(1, KW), onorm.reshape(1, GLA_DV))
    return out.reshape(B * S, VW)


def _xattn_kernel(*refs, n_in):
    x_ref = refs[0]
    a_refs = refs[1:1 + n_in]
    w_refs = refs[1 + n_in:1 + 2 * n_in]
    g_ref, wq_ref, mk_ref, mv_ref, wo_ref, gm_ref, wr_ref, br_ref, o_ref, lg_ref = refs[1 + 2 * n_in:]
    tq = x_ref.shape[0]
    subs = [slice(r, r + XA_SUB_ROWS) for r in range(0, tq, XA_SUB_ROWS)]
    xs = [x_ref[sub, :] for sub in subs]
    for a_ref, w_ref in zip(a_refs, w_refs):
        xs = [x + jnp.dot(a_ref[sub, :], w_ref[...], preferred_element_type=F32) for x, sub in zip(xs, subs)]
    qs = [_dot(_rms(x, g_ref[...]), wq_ref[...]).astype(BF16) for x in xs]
    mk = mk_ref[...]
    mv = mv_ref[...]
    sls = [slice(hd * XA_HEAD_DIM, (hd + 1) * XA_HEAD_DIM) for hd in range(XA_HEADS)]
    scores = [[_dot_nt(q[:, sl], mk[:, sl]) for sl in sls] for q in qs]
    probs = []
    for sc in scores:
        ps = []
        for s in sc:
            s = s * (XA_HEAD_DIM ** -0.5)
            p = jnp.exp(s - jnp.max(s, axis=-1, keepdims=True))
            ps.append((p / jnp.sum(p, axis=-1, keepdims=True)).astype(BF16))
        probs.append(ps)
    os_ = [jnp.concatenate([_dot(p, mv[:, sl]) for p, sl in zip(ps, sls)], axis=1) for ps in probs]
    outs = [x + _dot(o, wo_ref[...]) for x, o in zip(xs, os_)]
    for out, sub in zip(outs, subs):
        o_ref[sub, :] = out
    for out, sub in zip(outs, subs):
        lg_ref[:, sub] = _dot_nt(wr_ref[...], _rms(out, gm_ref[...])) + br_ref[...]


XA_SUB_ROWS = 256


def _mix_proj_xattn(x, acts, weights, g, wq, mk, mv, wo, g_moe, wt_router, bt_router, B, S, tq=1024):
    D = x.shape[-1]
    assert S % tq == 0 and tq % XA_SUB_ROWS == 0
    M = mk.shape[0] // B
    XW = mk.shape[-1]
    n_in = len(acts)
    seq3 = lambda a: a.reshape(B, S, a.shape[-1])
    row_spec = lambda a: pl.BlockSpec((None, tq, a.shape[-1]), lambda b, n: (b, n, 0))
    const = lambda a: pl.BlockSpec(a.shape, lambda b, n: (0,) * a.ndim)
    out, logits = pl.pallas_call(
        functools.partial(_xattn_kernel, n_in=n_in),
        grid=(B, S // tq),
        in_specs=[row_spec(x)] + [row_spec(a) for a in acts] + [const(w) for w in weights]
                 + [pl.BlockSpec((1, D), lambda b, n: (0, 0)),
                    pl.BlockSpec((D, XW), lambda b, n: (0, 0)),
                    pl.BlockSpec((None, M, XW), lambda b, n: (b, 0, 0)),
                    pl.BlockSpec((None, M, XW), lambda b, n: (b, 0, 0)),
                    pl.BlockSpec((XW, D), lambda b, n: (0, 0)),
                    pl.BlockSpec((1, D), lambda b, n: (0, 0)),
                    pl.BlockSpec((LANES, D), lambda b, n: (0, 0)),
                    pl.BlockSpec((LANES, XA_SUB_ROWS), lambda b, n: (0, 0))],
        out_specs=[pl.BlockSpec((None, tq, D), lambda b, n: (b, n, 0)),
                   pl.BlockSpec((None, LANES, tq), lambda b, n: (b * (S // tq) + n, 0, 0))],
        out_shape=[jax.ShapeDtypeStruct((B, S, D), F32),
                   jax.ShapeDtypeStruct((B * S // tq, LANES, tq), F32)],
        compiler_params=_cparams(2),
        name="xattn",
    )(seq3(x), *[seq3(a) for a in acts], *weights, g.reshape(1, D), wq,
      mk.reshape(B, M, XW), mv.reshape(B, M, XW), wo, g_moe.reshape(1, D), wt_router, bt_router)
    return out.reshape(B * S, D), logits


ROUTER_ROWS = 40


def _router_kernel(lg_ref, info_ref, slot_ref, cnt_ref, carry_ref):
    i = pl.program_id(0)

    @pl.when(i == 0)
    def _():
        carry_ref[...] = jnp.zeros_like(carry_ref)

    logits = lg_ref[:ROUTER_ROWS, :]
    tm = logits.shape[1]
    row = lax.broadcasted_iota(I32, logits.shape, 0)
    big = jnp.int32(LANES)
    neg = -jnp.inf
    gl = jnp.where(row < MOE_GROUPS, logits, neg)
    gmax = jnp.max(gl, axis=0, keepdims=True)
    g_top = jnp.min(jnp.where(gl == gmax, row, big), axis=0, keepdims=True)
    p_group = 1.0 / jnp.sum(jnp.exp(gl - gmax), axis=0, keepdims=True)
    lo = MOE_GROUPS + MOE_EXPERTS_PER_GROUP * g_top
    in_group = jnp.where(row >= lo, jnp.where(row < lo + MOE_EXPERTS_PER_GROUP, 1, 0), 0) > 0
    el = jnp.where(in_group, logits, neg)
    emax = jnp.max(el, axis=0, keepdims=True)
    ee = jnp.exp(el - emax)
    prob = ee / jnp.sum(ee, axis=0, keepdims=True)
    prob = jnp.where(in_group, prob, -1.0)
    p1 = jnp.max(prob, axis=0, keepdims=True)
    i1 = jnp.min(jnp.where(prob == p1, row, big), axis=0, keepdims=True)
    rest = jnp.where(row == i1, -1.0, prob)
    p2 = jnp.max(rest, axis=0, keepdims=True)
    i2 = jnp.min(jnp.where(rest == p2, row, big), axis=0, keepdims=True)
    tot = p1 + p2
    g1 = p_group * p1 / tot
    g2 = p_group * p2 / tot
    oh = jnp.concatenate([jnp.where(row == i1, 1.0, 0.0), jnp.where(row == i2, 1.0, 0.0)], axis=0)
    tr = lax.broadcasted_iota(I32, (tm, tm), 0)
    tc = lax.broadcasted_iota(I32, (tm, tm), 1)
    pre = _dot(oh, jnp.where(tr < tc, 1.0, 0.0))
    tots = _dot(oh, jnp.ones((tm, LANES), F32))
    reps = tm // LANES
    carry = carry_ref[...]
    base1 = jnp.concatenate([carry] * reps, axis=1)
    base2 = jnp.concatenate([carry + tots[:ROUTER_ROWS]] * reps, axis=1)
    r1 = jnp.sum(oh[:ROUTER_ROWS] * (base1 + pre[:ROUTER_ROWS]), axis=0, keepdims=True)
    r2 = jnp.sum(oh[ROUTER_ROWS:] * (base2 + pre[ROUTER_ROWS:]), axis=0, keepdims=True)
    carry = carry + tots[:ROUTER_ROWS] + tots[ROUTER_ROWS:]
    carry_ref[...] = carry
    cnt_ref[...] = carry
    e1 = (i1 - MOE_GROUPS).astype(F32)
    e2 = (i2 - MOE_GROUPS).astype(F32)
    slot_rows = [e1, e2, r1, r2, g1, g2]
    rows8 = lax.broadcasted_iota(I32, (SUBLANES, tm), 0)
    slot = jnp.zeros((SUBLANES, tm), F32)
    for j, val in enumerate(slot_rows):
        slot = jnp.where(rows8 == j, val, slot)
    slot_ref[...] = slot
    wide = jnp.concatenate([slot, jnp.zeros((LANES - SUBLANES, tm), F32)], axis=0)
    info_ref[...] = jnp.transpose(wide)


def _router(logits, tm=256):
    n_row_tiles, _, tq = logits.shape
    assert tq % tm == 0 and tm % LANES == 0
    per = tq // tm
    T = n_row_tiles * tq
    NT = T // tm
    return pl.pallas_call(
        _router_kernel,
        grid=(NT,),
        in_specs=[pl.BlockSpec((None, LANES, tm), lambda i: (i // per, 0, i % per))],
        out_specs=[pl.BlockSpec((tm, LANES), lambda i: (i, 0)),
                   pl.BlockSpec((None, SUBLANES, tm), lambda i: (i, 0, 0)),
                   pl.BlockSpec((ROUTER_ROWS, LANES), lambda i: (0, 0))],
        out_shape=[jax.ShapeDtypeStruct((T, LANES), F32),
                   jax.ShapeDtypeStruct((NT, SUBLANES, tm), F32),
                   jax.ShapeDtypeStruct((ROUTER_ROWS, LANES), F32)],
        scratch_shapes=[pltpu.VMEM((ROUTER_ROWS, LANES), F32)],
        compiler_params=_cparams(1),
        name="router",
    )(logits)


def _row_bytes_wait(hbm, buf, sem):
    pltpu.make_async_copy(buf, hbm.at[pl.ds(0, buf.shape[0]), :], sem).wait()


def _to_row_tiles(ref, val):
    n = val.shape[0]
    for c in range(SUBLANES):
        ref[pl.ds(c, n, stride=SUBLANES), :] = val[:, c * LANES:(c + 1) * LANES]


def _from_row_tiles(ref):
    n = ref.shape[0] // SUBLANES
    return jnp.concatenate([ref[pl.ds(c, n, stride=SUBLANES), :] for c in range(SUBLANES)], axis=1)


def _moe_dispatch_kernel(pends_ref, cnt_ref, dest_ref, x_ref, g_ref, hs_hbm, hbuf, zbuf, sems, zsem, *, td):
    i = pl.program_id(0)
    nt = pl.num_programs(0)
    slot = lax.rem(i, 2)

    @pl.when(i == 0)
    def _():
        zbuf[...] = jnp.zeros_like(zbuf)
        for e in range(MOE_EXPERTS):
            @pl.when(cnt_ref[e] > 0)
            def _():
                start = pl.multiple_of((pends_ref[e] - MOE_BLOCK) * SUBLANES, MOE_BLOCK)
                pltpu.make_async_copy(zbuf, hs_hbm.at[pl.ds(start, MOE_BLOCK * SUBLANES), :], zsem).start()
        for e in range(MOE_EXPERTS):
            @pl.when(cnt_ref[e] > 0)
            def _():
                pltpu.make_async_copy(zbuf, hs_hbm.at[pl.ds(0, MOE_BLOCK * SUBLANES), :], zsem).wait()

        first_unused = pends_ref[MOE_EXPERTS - 1] // MOE_BLOCK
        n_blocks = hs_hbm.shape[0] // (MOE_BLOCK * SUBLANES)

        def zero_start(blk, carry):
            start = pl.multiple_of(blk * (MOE_BLOCK * SUBLANES), MOE_BLOCK)
            pltpu.make_async_copy(zbuf, hs_hbm.at[pl.ds(start, MOE_BLOCK * SUBLANES), :], zsem).start()
            return carry

        def zero_wait(blk, carry):
            pltpu.make_async_copy(zbuf, hs_hbm.at[pl.ds(0, MOE_BLOCK * SUBLANES), :], zsem).wait()
            return carry

        lax.fori_loop(first_unused, n_blocks, zero_start, 0)
        lax.fori_loop(first_unused, n_blocks, zero_wait, 0)

    hb = hbuf.at[slot]
    _to_row_tiles(hb, _rms(x_ref[...], g_ref[...]))
    for j in range(td):
        for c in range(2):
            row = pl.multiple_of(dest_ref[0, c * td + j] * SUBLANES, SUBLANES)
            pltpu.make_async_copy(hb.at[pl.ds(j * SUBLANES, SUBLANES), :],
                                  hs_hbm.at[pl.ds(row, SUBLANES), :],
                                  sems.at[slot]).start(priority=c)

    @pl.when(i > 0)
    def _():
        other = hbuf.at[1 - slot]
        _row_bytes_wait(hs_hbm, other, sems.at[1 - slot])
        _row_bytes_wait(hs_hbm, other, sems.at[1 - slot])

    @pl.when(i == nt - 1)
    def _():
        _row_bytes_wait(hs_hbm, hb, sems.at[slot])
        _row_bytes_wait(hs_hbm, hb, sems.at[slot])


def _moe_dispatch(x, g, pends, counts, dest3, P, td):
    T, D = x.shape
    assert D == ROW_TILE and T % td == 0
    grid_spec = pltpu.PrefetchScalarGridSpec(
        num_scalar_prefetch=2,
        grid=(T // td,),
        in_specs=[pl.BlockSpec((None, 1, 2 * td), lambda i, pe, cn: (i, 0, 0), memory_space=pltpu.SMEM),
                  pl.BlockSpec((td, D), lambda i, pe, cn: (i, 0)),
                  pl.BlockSpec((1, D), lambda i, pe, cn: (0, 0))],
        out_specs=pl.BlockSpec(memory_space=pl.ANY),
        scratch_shapes=[pltpu.VMEM((2, td * SUBLANES, LANES), F32),
                        pltpu.VMEM((MOE_BLOCK * SUBLANES, LANES), F32),
                        pltpu.SemaphoreType.DMA((2,)),
                        pltpu.SemaphoreType.DMA(())],
    )
    return pl.pallas_call(
        functools.partial(_moe_dispatch_kernel, td=td),
        grid_spec=grid_spec,
        out_shape=jax.ShapeDtypeStruct((P * SUBLANES, LANES), F32),
        compiler_params=_cparams(1),
        name="moe_dispatch",
    )(pends, counts, dest3, x, g.reshape(1, D))


def _moe_expert_kernel(be_ref, nu_ref, hs_ref, w1_ref, w3_ref, w2_ref, o_ref, w1b, w3b, w2b):
    i = pl.program_id(0)
    used = i < nu_ref[0]
    changed = jnp.logical_or(i == 0, be_ref[i] != be_ref[jnp.maximum(i - 1, 0)])

    @pl.when(jnp.logical_and(used, changed))
    def _():
        w1b[...] = w1_ref[...].astype(BF16)
        w3b[...] = w3_ref[...].astype(BF16)
        w2b[...] = w2_ref[...].astype(BF16)

    @pl.when(used)
    def _():
        xe = _from_row_tiles(hs_ref).astype(BF16)
        ff = w1b.shape[1]
        halves = [slice(0, ff // 2), slice(ff // 2, ff)]
        ups = [(jnp.dot(xe, w1b[:, sl], preferred_element_type=F32),
                jnp.dot(xe, w3b[:, sl], preferred_element_type=F32)) for sl in halves]
        act = [(a * _sigmoid(a) * b).astype(BF16) for a, b in ups]
        y = sum(jnp.dot(a, w2b[sl, :], preferred_element_type=F32) for a, sl in zip(act, halves))
        _to_row_tiles(o_ref, y)

    @pl.when(jnp.logical_not(used))
    def _():
        o_ref[...] = jnp.zeros_like(o_ref)


def _moe_experts(hs, block_e, n_used, w1, w3, w2, layer):
    P = hs.shape[0] // SUBLANES
    D = ROW_TILE
    FF = w1.shape[-1]
    NB = P // MOE_BLOCK
    last = lambda i, nu: jnp.minimum(i, nu[0] - 1)
    grid_spec = pltpu.PrefetchScalarGridSpec(
        num_scalar_prefetch=2,
        grid=(NB,),
        in_specs=[pl.BlockSpec((MOE_BLOCK * SUBLANES, LANES), lambda i, be, nu: (last(i, nu), 0)),
                  pl.BlockSpec((None, None, D, FF), lambda i, be, nu: (layer, be[last(i, nu)], 0, 0)),
                  pl.BlockSpec((None, None, D, FF), lambda i, be, nu: (layer, be[last(i, nu)], 0, 0)),
                  pl.BlockSpec((None, None, FF, D), lambda i, be, nu: (layer, be[last(i, nu)], 0, 0))],
        out_specs=pl.BlockSpec((MOE_BLOCK * SUBLANES, LANES), lambda i, be, nu: (i, 0)),
        scratch_shapes=[pltpu.VMEM((D, FF), BF16),
                        pltpu.VMEM((D, FF), BF16),
                        pltpu.VMEM((FF, D), BF16)],
    )
    return pl.pallas_call(
        _moe_expert_kernel,
        grid_spec=grid_spec,
        out_shape=jax.ShapeDtypeStruct((P * SUBLANES, LANES), F32),
        compiler_params=_cparams(1),
        name="moe_experts",
    )(block_e, n_used, hs, w1, w3, w2)


def _gather_rows(src_hbm, idx_ref, dst_ref, sem, first, last):
    for r in range(first, last):
        row = pl.multiple_of(idx_ref[0, r] * SUBLANES, SUBLANES)
        pltpu.make_async_copy(src_hbm.at[pl.ds(row, SUBLANES), :],
                              dst_ref.at[pl.ds(r * SUBLANES, SUBLANES), :], sem).start(priority=r % 2)


def _moe_combine_kernel(pos_ref, posn_ref, x_ref, info_ref, yb_hbm, g_ref, *rest, tc, final_norm, splits):
    if splits:
        w_ref, o_ref = rest[0], rest[1]
        p_refs = rest[2:2 + len(splits)]
        ybuf, sems = rest[2 + len(splits):]
    else:
        o_ref, ybuf, sems = rest
    i = pl.program_id(0)
    nb = pl.num_programs(0)
    slot = lax.rem(i, 2)

    @pl.when(i == 0)
    def _():
        def issue(r, carry):
            src = pl.multiple_of(pos_ref[0, r] * SUBLANES, SUBLANES)
            dst = pl.multiple_of(r * SUBLANES, SUBLANES)
            pltpu.make_async_copy(yb_hbm.at[pl.ds(src, SUBLANES), :],
                                  ybuf.at[0, pl.ds(dst, SUBLANES), :], sems.at[0]).start()
            return carry
        lax.fori_loop(0, 2 * tc, issue, 0)

    def wait_tile(s_):
        pltpu.make_async_copy(yb_hbm.at[pl.ds(0, 2 * tc * SUBLANES), :], ybuf.at[s_], sems.at[s_]).wait()

    if not splits:
        @pl.when(i + 1 < nb)
        def _():
            _gather_rows(yb_hbm, posn_ref, ybuf.at[1 - slot], sems.at[1 - slot], 0, 2 * tc)

    wait_tile(slot)
    info = info_ref[...]
    yb = ybuf.at[slot]
    y0 = _from_row_tiles(yb.at[pl.ds(0, tc * SUBLANES), :])
    y1 = _from_row_tiles(yb.at[pl.ds(tc * SUBLANES, tc * SUBLANES), :])
    out = x_ref[...] + (y0 * info[:, 4:5] + y1 * info[:, 5:6])
    if final_norm:
        out = _rms(out, g_ref[...])
    o_ref[...] = out
    if splits:
        h = _rms(out, g_ref[...]).astype(BF16)
        chunks = []
        for p_ref, n in zip(p_refs, splits):
            for c0 in range(0, n, PROJ_CHUNK):
                chunks.append((p_ref, c0, min(PROJ_CHUNK, n - c0)))
        per = -(-2 * tc // max(len(chunks) // 2, 1))
        off = 0
        for ci, (p_ref, c0, width) in enumerate(chunks):
            val = jnp.dot(h, w_ref[:, off:off + width], preferred_element_type=F32)
            _gather_rows(yb_hbm, posn_ref, ybuf.at[1 - slot], sems.at[1 - slot],
                         min(ci * per, 2 * tc), min((ci + 1) * per, 2 * tc))
            p_ref[:, c0:c0 + width] = val.astype(p_ref.dtype)
            off += width

        @pl.when(i == nb - 1)
        def _():
            wait_tile(1 - slot)


def _moe_combine(x, info, dest3, yb, g, final_norm, tc, next_proj=None):
    T, D = x.shape
    assert D == ROW_TILE and T % tc == 0
    NT = T // tc
    in_specs = [pl.BlockSpec((None, 1, 2 * tc), lambda i: (i, 0, 0), memory_space=pltpu.SMEM),
                pl.BlockSpec((None, 1, 2 * tc), lambda i: (jnp.minimum(i + 1, NT - 1), 0, 0),
                             memory_space=pltpu.SMEM),
                pl.BlockSpec((tc, D), lambda i: (i, 0)),
                pl.BlockSpec((tc, LANES), lambda i: (i, 0)),
                pl.BlockSpec(memory_space=pl.ANY),
                pl.BlockSpec((1, D), lambda i: (0, 0))]
    out_specs = [pl.BlockSpec((tc, D), lambda i: (i, 0))]
    out_shape = [jax.ShapeDtypeStruct((T, D), F32)]
    args = [dest3, dest3, x, info, yb, g.reshape(1, D)]
    splits = ()
    if next_proj is not None:
        w, splits, out_dtypes = next_proj
        assert not final_norm and sum(splits) == w.shape[1]
        in_specs.append(pl.BlockSpec(w.shape, lambda i: (0, 0)))
        args.append(w)
        out_specs += [pl.BlockSpec((tc, n), lambda i: (i, 0)) for n in splits]
        out_shape += [jax.ShapeDtypeStruct((T, n), dt) for n, dt in zip(splits, out_dtypes)]
    outs = pl.pallas_call(
        functools.partial(_moe_combine_kernel, tc=tc, final_norm=final_norm, splits=tuple(splits)),
        grid=(NT,),
        in_specs=in_specs,
        out_specs=out_specs,
        out_shape=out_shape,
        scratch_shapes=[pltpu.VMEM((2, 2 * tc * SUBLANES, LANES), F32), pltpu.SemaphoreType.DMA((2,))],
        compiler_params=_cparams(1),
        name="moe_combine",
    )(*args)
    return outs[0], tuple(outs[1:])


MOE_TILE = 512
MOE_COMBINE_TILE = 256
MOE_LAST_COMBINE_TILE = 256
PROJ_CHUNK = 256
ROUTER_TILE = 256


def _tile_slots(dest, tile):
    n_tiles = dest.shape[0] * dest.shape[2] // tile
    return jnp.concatenate([dest[:, 0, :].reshape(n_tiles, 1, tile), dest[:, 1, :].reshape(n_tiles, 1, tile)], axis=2)


def _router_params(w_group, b_group, w_expert, b_expert, lanes_out):
    D = w_group.shape[0]
    n_log = MOE_GROUPS + MOE_EXPERTS
    wt = jnp.zeros((LANES, D), F32).at[:MOE_GROUPS].set(w_group.T).at[MOE_GROUPS:n_log].set(w_expert.T)
    bt = jnp.zeros((LANES,), F32).at[:MOE_GROUPS].set(b_group).at[MOE_GROUPS:n_log].set(b_expert)
    return wt.astype(BF16), jnp.broadcast_to(bt[:, None], (LANES, lanes_out))


def _moe_layer(x, g, logits, w1, w3, w2, layer, g_out, final_norm, next_proj):
    T, D = x.shape
    n_log = MOE_GROUPS + MOE_EXPERTS
    info, slot, cnt = _router(logits, ROUTER_TILE)
    P = 2 * T + MOE_EXPERTS * MOE_BLOCK
    NB = P // MOE_BLOCK
    counts = cnt[MOE_GROUPS:n_log, 0].astype(I32)
    padded = (counts + MOE_BLOCK - 1) // MOE_BLOCK * MOE_BLOCK
    pends = jnp.cumsum(padded).astype(I32)
    pstarts = pends - padded
    block_start = jnp.arange(NB, dtype=I32) * MOE_BLOCK
    block_e = jnp.minimum(jnp.sum((pends[None, :] <= block_start[:, None]).astype(I32), axis=1),
                          MOE_EXPERTS - 1).astype(I32)
    n_used = (pends[-1:] // MOE_BLOCK).astype(I32)
    eid = slot[:, 0:2, :].astype(I32)
    expert_ids = jnp.arange(MOE_EXPERTS, dtype=I32)
    seg_start = jnp.sum(jnp.where(eid[..., None] == expert_ids, pstarts, 0), axis=-1)
    dest = seg_start + slot[:, 2:4, :].astype(I32)
    hs = _moe_dispatch(x, g, pends, counts, _tile_slots(dest, MOE_TILE), P, MOE_TILE)
    yb = _moe_experts(hs, block_e, n_used, w1, w3, w2, layer)
    tc = MOE_LAST_COMBINE_TILE if next_proj is None else MOE_COMBINE_TILE
    return _moe_combine(x, info, _tile_slots(dest, tc), yb, g_out, final_norm, tc, next_proj)


def kernel(x, mem, norm_mix, norm_xattn, norm_moe, norm_final, ev_w_in, ev_sinks, ev_mu, ev_w0, ev_w2, ev_a0, ev_a2, ev_g2, ev_k_k, ev_k_a, ev_r_k, ev_lnx_w, ev_lnx_b, ev_w_out, od_w_in, od_gate_up, od_gate_b, od_onorm, od_w_out, mem_norm, mem_wk, mem_wv, xa_wq, xa_wo, moe_w_group, moe_b_group, moe_w_expert, moe_b_expert, moe_w1, moe_w3, moe_w2):
    B, S, D = x.shape
    M = mem.shape[1]
    T = B * S
    depth = norm_mix.shape[0]
    xf = x.reshape(T, D)

    XW = XA_HEADS * XA_HEAD_DIM
    w_kv = jnp.concatenate([mem_wk, mem_wv], axis=1).astype(BF16)
    mk, mv = _norm_matmul(mem.reshape(B * M, D), mem_norm, w_kv, (XW, XW), (BF16, BF16))

    KW = GLA_HEADS * GLA_DK
    VW = GLA_HEADS * GLA_DV
    swa_cols = SWA_Q_HEADS * HEAD_DIM + 2 * (SWA_Q_HEADS // SWA_GROUP) * HEAD_DIM

    def in_proj(layer):
        i = layer // 2
        if layer % 2 == 0:
            return ev_w_in[i].astype(BF16), (swa_cols, ev_w_in.shape[-1] - swa_cols), (BF16, F32)
        R = od_gate_up.shape[1]
        w = od_w_in[i]
        w_re = jnp.concatenate([w[:, :2 * KW + VW], w[:, 2 * KW + VW + R:],
                                w[:, 2 * KW + VW:2 * KW + VW + R],
                                jnp.zeros((D, LANES - R), F32)], axis=1).astype(BF16)
        return w_re, (2 * KW, VW, VW, LANES), (F32, BF16, F32, F32)

    w_first, _, _ = in_proj(0)
    qkv0, rw0 = _in_proj_rwkv(xf, norm_mix[0], w_first, swa_cols, ev_mu[0], ev_w0[0], ev_w2[0], ev_a0[0], ev_a2[0],
                              ev_g2[0], ev_k_k[0], ev_k_a[0], S)
    proj = None
    for layer in range(depth):
        i = layer // 2
        if layer % 2 == 0:
            if layer == 0:
                qkv, rw = qkv0, [t.reshape(B, S, RWKV_WIDTH) for t in rw0]
            else:
                qkv, p_rw = proj
                rw = _rwkv_prep(p_rw, ev_mu[i], ev_w0[i], ev_w2[i], ev_a0[i], ev_a2[i],
                                ev_g2[i], ev_k_k[i], ev_k_a[i], B, S)
            o_a = _swa(qkv, ev_sinks[i], B, S)
            o_b = _rwkv_scan(*rw, ev_r_k[i].reshape(-1), ev_lnx_w[i], ev_lnx_b[i])
            w_out = ev_w_out[i].astype(BF16)
            qw = o_a.shape[-1]
            mix_acts, mix_ws = [o_a, o_b], [w_out[:qw], w_out[qw:]]
        else:
            qk, v_gla, og, gd = proj
            R = od_gate_up.shape[1]
            gup = jnp.zeros((LANES, KW), F32).at[:R].set(od_gate_up[i]).astype(BF16)
            o = _gla(qk, v_gla, og, gd, gup, od_gate_b[i], od_onorm[i], B, S)
            mix_acts, mix_ws = [o], [od_w_out[i].astype(BF16)]
        wt_router, bt_router = _router_params(moe_w_group[layer], moe_b_group[layer], moe_w_expert[layer],
                                              moe_b_expert[layer], XA_SUB_ROWS)
        xf, logits = _mix_proj_xattn(xf, mix_acts, mix_ws, norm_xattn[layer], xa_wq[layer].astype(BF16), mk, mv,
                                     xa_wo[layer].astype(BF16), norm_moe[layer], wt_router, bt_router, B, S)
        last = layer == depth - 1
        g_out = norm_final if last else norm_mix[layer + 1]
        xf, proj = _moe_layer(xf, norm_moe[layer], logits, moe_w1, moe_w3, moe_w2, layer,
                              g_out, last, None if last else in_proj(layer + 1))
    return xf.reshape(B, S, D)
```

```python
import functools
import math

import jax
import jax.numpy as jnp
from jax import lax
from jax.experimental import pallas as pl
from jax.experimental.pallas import tpu as pltpu

F32 = jnp.float32
BF16 = jnp.bfloat16
I32 = jnp.int32

EPS = 1e-6
HEAD_DIM = 64
SWA_WINDOW = 128
SWA_Q_HEADS = 8
SWA_GROUP = 4
RWKV_HEADS = 8
RWKV_WIDTH = 512
RWKV_LN_EPS = 64e-5
RWKV_CHUNK = 64
RWKV_DECAY_SCALE = math.exp(-0.5)
GLA_HEADS = 4
GLA_DK = 128
GLA_DV = 256
GLA_CHUNK = 64
GLA_GATE_NORM = 16.0
XA_HEADS = 4
XA_HEAD_DIM = 128
MOE_GROUPS = 4
MOE_EXPERTS_PER_GROUP = 8
MOE_EXPERTS = 32
MOE_BLOCK = 512
LANES = 128
SUBLANES = 8
ROW_TILE = SUBLANES * LANES

VMEM_LIMIT_BYTES = 48 * 1024 * 1024


def _cparams(n_axes):
    return pltpu.CompilerParams(dimension_semantics=("arbitrary",) * n_axes,
                                vmem_limit_bytes=VMEM_LIMIT_BYTES)


def _dot(a, b):
    return jnp.dot(a.astype(BF16), b.astype(BF16), preferred_element_type=F32)


def _dot_nt(a, b):
    return lax.dot_general(a.astype(BF16), b.astype(BF16), (((1,), (1,)), ((), ())),
                           preferred_element_type=F32)


def _dot_tn(a, b):
    return lax.dot_general(a.astype(BF16), b.astype(BF16), (((0,), (0,)), ((), ())),
                           preferred_element_type=F32)


def _tri_cumsum(tri, x):
    hi = x.astype(BF16)
    rest = x - hi.astype(F32)
    mid = rest.astype(BF16)
    lo = (rest - mid.astype(F32)).astype(BF16)
    w = x.shape[1]
    sums = jnp.dot(tri.astype(BF16), jnp.concatenate([hi, mid, lo], axis=1), preferred_element_type=F32)
    return sums[:, :w] + sums[:, w:2 * w] + sums[:, 2 * w:]


def _rms(x, g):
    ms = jnp.mean(x * x, axis=-1, keepdims=True)
    return x * lax.rsqrt(ms + EPS) * g


def _sigmoid(x):
    return 1.0 / (1.0 + jnp.exp(-x))


def _softplus(x):
    return jnp.maximum(x, 0.0) + jnp.log(1.0 + jnp.exp(-jnp.abs(x)))


def _norm_matmul_kernel(x_ref, g_ref, w_ref, *o_refs, splits):
    h = _rms(x_ref[...], g_ref[...]).astype(BF16)
    off = 0
    for o_ref, n in zip(o_refs, splits):
        o_ref[...] = jnp.dot(h, w_ref[:, off:off + n], preferred_element_type=F32).astype(o_ref.dtype)
        off += n


def _norm_matmul(x, g, w, splits, out_dtypes, tm=256):
    T, D = x.shape
    N = w.shape[1]
    assert sum(splits) == N and T % tm == 0
    return pl.pallas_call(
        functools.partial(_norm_matmul_kernel, splits=tuple(splits)),
        grid=(T // tm,),
        in_specs=[pl.BlockSpec((tm, D), lambda i: (i, 0)),
                  pl.BlockSpec((1, D), lambda i: (0, 0)),
                  pl.BlockSpec((D, N), lambda i: (0, 0))],
        out_specs=[pl.BlockSpec((tm, n), lambda i: (i, 0)) for n in splits],
        out_shape=[jax.ShapeDtypeStruct((T, n), dt) for n, dt in zip(splits, out_dtypes)],
        compiler_params=_cparams(1),
        name="norm_matmul",
    )(x, g.reshape(1, D), w)


def _swa_kernel(sinks_ref, q_ref, kp_ref, kc_ref, vp_ref, vc_ref, o_ref):
    n = pl.program_id(1)
    W = SWA_WINDOW
    NB = q_ref.shape[0]
    qpos = lax.broadcasted_iota(I32, (W, 2 * W), 0) + W
    kpos = lax.broadcasted_iota(I32, (W, 2 * W), 1)
    rel = qpos - kpos
    in_window = jnp.where(rel >= 0, jnp.where(rel < W, 1, 0), 0)
    has_prev = jnp.where(n > 0, 1, 0)
    valid = (in_window * jnp.where(kpos >= W, 1, has_prev)) > 0
    n_groups = SWA_Q_HEADS // SWA_GROUP
    streams = [(bi, g) for bi in range(NB) for g in range(n_groups)]
    qb = [q_ref[bi].astype(BF16) for bi in range(NB)]
    kb = [jnp.concatenate([kp_ref[bi], kc_ref[bi]], axis=0).astype(BF16) for bi in range(NB)]
    vb = [jnp.concatenate([vp_ref[bi], vc_ref[bi]], axis=0).astype(BF16) for bi in range(NB)]
    gs = lambda g: slice(g * HEAD_DIM, (g + 1) * HEAD_DIM)
    scores = []
    for bi, g in streams:
        qg = jnp.concatenate([qb[bi][:, h * HEAD_DIM:(h + 1) * HEAD_DIM]
                              for h in range(g * SWA_GROUP, (g + 1) * SWA_GROUP)], axis=0)
        scores.append(_dot_nt(qg, kb[bi][:, gs(g)]))
    probs = []
    for i, (bi, g) in enumerate(streams):
        pieces = []
        for j in range(SWA_GROUP):
            s = jnp.where(valid, scores[i][j * W:(j + 1) * W] * (HEAD_DIM ** -0.5), -jnp.inf)
            sink = sinks_ref[g * SWA_GROUP + j]
            m = jnp.maximum(jnp.max(s, axis=-1, keepdims=True), sink)
            p = jnp.exp(s - m)
            den = jnp.sum(p, axis=-1, keepdims=True) + jnp.exp(sink - m)
            pieces.append((p / den).astype(BF16))
        probs.append(jnp.concatenate(pieces, axis=0))
    ogs = [_dot(probs[i], vb[bi][:, gs(g)]) for i, (bi, g) in enumerate(streams)]
    for bi in range(NB):
        outs = []
        for g in range(n_groups):
            og = ogs[bi * n_groups + g]
            outs += [og[j * W:(j + 1) * W] for j in range(SWA_GROUP)]
        o_ref[bi] = jnp.concatenate(outs, axis=1).astype(o_ref.dtype)


SWA_BATCH_ROWS = 2


def _swa(qkv, sinks, B, S):
    W = SWA_WINDOW
    assert S % W == 0
    qkv3 = qkv.reshape(B, S, qkv.shape[-1])
    qw = SWA_Q_HEADS * HEAD_DIM
    kw = qw // SWA_GROUP
    kcol = qw // kw
    nb = SWA_BATCH_ROWS if B % SWA_BATCH_ROWS == 0 else 1
    out = pl.pallas_call(
        _swa_kernel,
        grid=(B // nb, S // W),
        in_specs=[pl.BlockSpec(memory_space=pltpu.SMEM),
                  pl.BlockSpec((nb, W, qw), lambda b, n: (b, n, 0)),
                  pl.BlockSpec((nb, W, kw), lambda b, n: (b, jnp.maximum(n - 1, 0), kcol)),
                  pl.BlockSpec((nb, W, kw), lambda b, n: (b, n, kcol)),
                  pl.BlockSpec((nb, W, kw), lambda b, n: (b, jnp.maximum(n - 1, 0), kcol + 1)),
                  pl.BlockSpec((nb, W, kw), lambda b, n: (b, n, kcol + 1))],
        out_specs=pl.BlockSpec((nb, W, qw), lambda b, n: (b, n, 0)),
        out_shape=jax.ShapeDtypeStruct((B, S, qw), BF16),
        compiler_params=_cparams(2),
        name="swa",
    )(sinks, qkv3, qkv3, qkv3, qkv3, qkv3)
    return out.reshape(B * S, qw)


def _rwkv_prep_math(p, last, mu_ref, w0_ref, w2_ref, a0_ref, a2_ref, g2_ref, kk_ref, ka_ref, outs):
    r_out, lw_out, k_out, v_out, a_out, b_out, g_out = outs
    C = RWKV_WIDTH
    row = lax.broadcasted_iota(I32, p.shape, 0)
    p_prev = jnp.where(row == 0, last, pltpu.roll(p, 1, axis=0))
    p = p + (p_prev - p) * mu_ref[...]
    r = p[:, :C]
    k = p[:, C:2 * C]
    v = p[:, 2 * C:3 * C]
    xw = p[:, 3 * C:3 * C + 64]
    xa = p[:, 3 * C + 64:3 * C + 128]
    xg = p[:, 3 * C + 128:]
    z = w0_ref[...] + _dot(jnp.tanh(xw), w2_ref[...])
    lw = -RWKV_DECAY_SCALE * _sigmoid(z)
    a = _sigmoid(a0_ref[...] + _dot(xa, a2_ref[...]))
    g = _dot(_sigmoid(xg), g2_ref[...])
    kk = k * kk_ref[...]
    pieces = []
    for h in range(RWKV_HEADS):
        kh = kk[:, h * HEAD_DIM:(h + 1) * HEAD_DIM]
        ss = jnp.sum(kh * kh, axis=-1, keepdims=True)
        pieces.append(kh * lax.rsqrt(jnp.maximum(ss, 1e-24)))
    kk = jnp.concatenate(pieces, axis=1)
    r_out[...] = r
    lw_out[...] = lw
    k_out[...] = k * (1.0 + (a - 1.0) * ka_ref[...])
    v_out[...] = v
    a_out[...] = -kk
    b_out[...] = kk * a
    g_out[...] = g


def _rwkv_prep_kernel(p_ref, pprev_ref, *refs):
    n = pl.program_id(1)
    last = jnp.where(n > 0, pprev_ref[SUBLANES - 1:SUBLANES, :], 0.0)
    _rwkv_prep_math(p_ref[...], last, *refs[:8], refs[8:])


def _rwkv_params(mu, w0, w2, a0, a2, g2, k_k, k_a):
    row = lambda t: t.reshape(1, -1)
    return [row(mu), row(w0), w2.astype(BF16), row(a0), a2.astype(BF16), g2.astype(BF16), row(k_k), row(k_a)]


def _rwkv_prep(p, mu, w0, w2, a0, a2, g2, k_k, k_a, B, S, tt=256):
    assert S % tt == 0
    C = RWKV_WIDTH
    PW = p.shape[-1]
    p3 = p.reshape(B, S, PW)
    full = lambda arr: pl.BlockSpec(arr.shape, lambda b, n: (0,) * arr.ndim)
    params = _rwkv_params(mu, w0, w2, a0, a2, g2, k_k, k_a)
    outs = pl.pallas_call(
        _rwkv_prep_kernel,
        grid=(B, S // tt),
        in_specs=[pl.BlockSpec((None, tt, PW), lambda b, n: (b, n, 0)),
                  pl.BlockSpec((None, SUBLANES, PW),
                               lambda b, n: (b, jnp.maximum(n * (tt // SUBLANES) - 1, 0), 0))]
                 + [full(t) for t in params],
        out_specs=[pl.BlockSpec((None, tt, C), lambda b, n: (b, n, 0))] * 7,
        out_shape=[jax.ShapeDtypeStruct((B, S, C), F32)] * 7,
        compiler_params=_cparams(2),
        name="rwkv_prep",
    )(p3, p3, *params)
    return outs


def _in_proj_rwkv_kernel(x_ref, g_ref, w_ref, *refs, swa_cols, tiles_per_seq):
    params = refs[:8]
    qkv_out = refs[8]
    outs = refs[9:16]
    p_buf, last_buf = refs[16:]
    i = pl.program_id(0)

    @pl.when(i == 0)
    def _():
        p_buf[...] = jnp.zeros_like(p_buf)
        last_buf[...] = jnp.zeros_like(last_buf)

    j = i - 1
    p_prev_tile = p_buf[lax.rem(i + 1, 2)]
    tm = p_prev_tile.shape[0]
    last = jnp.where(lax.rem(j, tiles_per_seq) == 0, 0.0, last_buf[...])
    _rwkv_prep_math(p_prev_tile, last, *params, outs)
    last_buf[...] = p_prev_tile[tm - 1:tm, :]
    h = _rms(x_ref[...], g_ref[...]).astype(BF16)
    qkv_out[...] = jnp.dot(h, w_ref[:, :swa_cols], preferred_element_type=F32).astype(qkv_out.dtype)
    p_buf[lax.rem(i, 2)] = jnp.dot(h, w_ref[:, swa_cols:], preferred_element_type=F32)


def _in_proj_rwkv(x, g, w, swa_cols, mu, w0, w2, a0, a2, g2, k_k, k_a, S, tm=256):
    T, D = x.shape
    assert S % tm == 0 and T % S == 0
    N = w.shape[1]
    C = RWKV_WIDTH
    NT = T // tm
    params = _rwkv_params(mu, w0, w2, a0, a2, g2, k_k, k_a)
    const = lambda arr: pl.BlockSpec(arr.shape, lambda i: (0,) * arr.ndim)
    cur = lambda i: (jnp.minimum(i, NT - 1), 0)
    prev = lambda i: (jnp.maximum(i - 1, 0), 0)
    outs = pl.pallas_call(
        functools.partial(_in_proj_rwkv_kernel, swa_cols=swa_cols, tiles_per_seq=S // tm),
        grid=(NT + 1,),
        in_specs=[pl.BlockSpec((tm, D), cur), pl.BlockSpec((1, D), lambda i: (0, 0)), const(w)]
                 + [const(t) for t in params],
        out_specs=[pl.BlockSpec((tm, swa_cols), cur)] + [pl.BlockSpec((tm, C), prev)] * 7,
        out_shape=[jax.ShapeDtypeStruct((T, swa_cols), BF16)] + [jax.ShapeDtypeStruct((T, C), F32)] * 7,
        scratch_shapes=[pltpu.VMEM((2, tm, N - swa_cols), F32), pltpu.VMEM((1, N - swa_cols), F32)],
        compiler_params=_cparams(1),
        name="in_proj_rwkv",
    )(x, g.reshape(1, D), w, *params)
    return outs[0], outs[1:]


def _pair_blockdiag(x):
    lane = lax.broadcasted_iota(I32, x.shape, 1)
    zero = jnp.zeros_like(x)
    return jnp.concatenate([jnp.where(lane < HEAD_DIM, x, zero), jnp.where(lane >= HEAD_DIM, x, zero)], axis=0)


def _rwkv_scan_kernel(r_ref, lw_ref, k_ref, v_ref, a_ref, b_ref, g_ref, rk_ref, lnw_ref, lnb_ref,
                      o_ref, s_ref):
    c = pl.program_id(1)

    @pl.when(c == 0)
    def _():
        s_ref[...] = jnp.zeros_like(s_ref)

    C = RWKV_CHUNK
    NB = r_ref.shape[0]
    NP = RWKV_HEADS // 2
    PW = 2 * HEAD_DIM
    row = lax.broadcasted_iota(I32, (C, C), 0)
    col = lax.broadcasted_iota(I32, (C, C), 1)
    tri = jnp.where(row >= col, 1.0, 0.0).astype(F32)
    rowp = lax.broadcasted_iota(I32, (C, PW), 0)
    colp = lax.broadcasted_iota(I32, (C, PW), 1)
    colp = jnp.where(colp >= HEAD_DIM, colp - HEAD_DIM, colp)
    lower_p = rowp >= colp
    strict_p = rowp > colp
    rows = lax.broadcasted_iota(I32, (PW, PW), 0)
    cols = lax.broadcasted_iota(I32, (PW, PW), 1)
    same_head = jnp.where(rows >= HEAD_DIM, 1, 0) == jnp.where(cols >= HEAD_DIM, 1, 0)
    first = lax.broadcasted_iota(I32, (C, PW), 1) < HEAD_DIM

    streams = [(bi, p) for bi in range(NB) for p in range(NP)]
    pre = []
    for bi in range(NB):
        lw = lw_ref[bi]
        cum = _tri_cumsum(tri, lw)
        cum_last = cum[C - 1:C, :]
        r = r_ref[bi]
        k = k_ref[bi]
        v = v_ref[bi]
        a = a_ref[bi]
        b = b_ref[bi]
        e_neg = jnp.exp(-cum)
        e_rem = jnp.exp(cum_last - cum)
        pre.append(dict(
            r_t=(r * jnp.exp(cum)).astype(BF16), a_t=(a * jnp.exp(cum - lw)).astype(BF16),
            b_t=(b * e_neg).astype(BF16), k_t=(k * e_neg).astype(BF16),
            b_d=(b * e_rem).astype(BF16), k_d=(k * e_rem).astype(BF16),
            v_b=v.astype(BF16), v=v, e_last=jnp.exp(cum_last), rkk=r * k * rk_ref[...], g=g_ref[bi]))

    def lanes(p):
        return slice(p * PW, (p + 1) * PW)

    ar = [jnp.concatenate([pre[bi]['a_t'][:, lanes(p)], pre[bi]['r_t'][:, lanes(p)]], axis=0) for bi, p in streams]
    s0 = [s_ref[bi, p] for bi, p in streams]
    big = [_dot_nt(ar[i], jnp.concatenate([_pair_blockdiag(pre[bi]['b_t'][:, lanes(p)]),
                                           _pair_blockdiag(pre[bi]['k_t'][:, lanes(p)]),
                                           s0[i].astype(BF16)], axis=0))
           for i, (bi, p) in enumerate(streams)]
    m_b = [t[:, :PW] for t in big]
    m_k = [t[:, PW:2 * PW] for t in big]
    ars = [t[:, 2 * PW:] for t in big]
    v_p = [pre[bi]['v_b'][:, lanes(p)] for bi, p in streams]
    v_bd = [_pair_blockdiag(vp) for vp in v_p]
    x = [ars[i][:C] + _dot(jnp.where(strict_p, m_k[i][:C], 0.0), v_bd[i]) for i in range(len(streams))]
    pw = [jnp.where(strict_p, m_b[i][:C], 0.0).astype(BF16) for i in range(len(streams))]
    n_stages = 6
    for stage in range(n_stages):
        if stage < n_stages - 1:
            prod = [_dot(pw[i], jnp.concatenate([_pair_blockdiag(x[i].astype(BF16)), _pair_blockdiag(pw[i])], axis=1))
                    for i in range(len(streams))]
            x = [x[i] + prod[i][:, :PW] for i in range(len(streams))]
            pw = [prod[i][:, PW:].astype(BF16) for i in range(len(streams))]
        else:
            x = [x[i] + _dot(pw[i], _pair_blockdiag(x[i].astype(BF16))) for i in range(len(streams))]
    u_b = [xi.astype(BF16) for xi in x]
    y = [ars[i][C:]
         + _dot(jnp.concatenate([jnp.where(lower_p, m_b[i][C:], 0.0), jnp.where(lower_p, m_k[i][C:], 0.0)], axis=1),
                jnp.concatenate([_pair_blockdiag(u_b[i]), v_bd[i]], axis=0))
         for i in range(len(streams))]
    for i, (bi, p) in enumerate(streams):
        upd = _dot_tn(jnp.concatenate([u_b[i], v_p[i]], axis=0),
                      jnp.concatenate([pre[bi]['b_d'][:, lanes(p)], pre[bi]['k_d'][:, lanes(p)]], axis=0))
        s_ref[bi, p] = s0[i] * pre[bi]['e_last'][:, lanes(p)] + jnp.where(same_head, upd, 0.0)

    lnw = lnw_ref[...]
    lnb = lnb_ref[...]

    def head_sum(t):
        s1 = jnp.sum(jnp.where(first, t, 0.0), axis=-1, keepdims=True)
        s2 = jnp.sum(jnp.where(first, 0.0, t), axis=-1, keepdims=True)
        return jnp.where(first, s1, s2)

    for bi in range(NB):
        outs = []
        for p in range(NP):
            yi = y[bi * NP + p]
            mean = head_sum(yi) * (1.0 / HEAD_DIM)
            yc = yi - mean
            var = head_sum(yc * yc) * (1.0 / HEAD_DIM)
            yn = yc * lax.rsqrt(var + RWKV_LN_EPS) * lnw[:, lanes(p)] + lnb[:, lanes(p)]
            bonus = head_sum(pre[bi]['rkk'][:, lanes(p)]) * pre[bi]['v'][:, lanes(p)]
            outs.append((yn + bonus) * pre[bi]['g'][:, lanes(p)])
        o_ref[bi] = jnp.concatenate(outs, axis=1).astype(o_ref.dtype)


RWKV_BATCH_ROWS = 8


def _rwkv_scan(r, lw, k, v, a, b, g, r_k, lnx_w, lnx_b):
    B, S, W = r.shape
    C = RWKV_CHUNK
    assert S % C == 0 and W == RWKV_HEADS * HEAD_DIM
    nb = RWKV_BATCH_ROWS if B % RWKV_BATCH_ROWS == 0 else 1
    seq = pl.BlockSpec((nb, C, W), lambda bb, c: (bb, c, 0))
    par = pl.BlockSpec((1, W), lambda bb, c: (0, 0))
    out = pl.pallas_call(
        _rwkv_scan_kernel,
        grid=(B // nb, S // C),
        in_specs=[seq] * 7 + [par] * 3,
        out_specs=seq,
        out_shape=jax.ShapeDtypeStruct((B, S, W), BF16),
        scratch_shapes=[pltpu.VMEM((nb, RWKV_HEADS // 2, 2 * HEAD_DIM, 2 * HEAD_DIM), F32)],
        compiler_params=_cparams(2),
        name="rwkv_scan",
    )(r, lw, k, v, a, b, g, r_k.reshape(1, W), lnx_w.reshape(1, W), lnx_b.reshape(1, W))
    return out.reshape(B * S, W)


def _gla_kernel(q_ref, k_ref, v_ref, og_ref, gd_ref, gup_ref, gb_ref, on_ref, o_ref, s_ref):
    c = pl.program_id(1)

    @pl.when(c == 0)
    def _():
        s_ref[...] = jnp.zeros_like(s_ref)

    C = GLA_CHUNK
    NB = q_ref.shape[0]
    row = lax.broadcasted_iota(I32, (C, C), 0)
    col = lax.broadcasted_iota(I32, (C, C), 1)
    lower = row >= col
    tri = jnp.where(lower, 1.0, 0.0).astype(F32)
    onorm = on_ref[...]
    zs = [_dot(gd_ref[bi], gup_ref[...]) + gb_ref[...] for bi in range(NB)]
    cums = [_tri_cumsum(tri, -_softplus(-z) / GLA_GATE_NORM) for z in zs]
    qe, ke, kd, e_last, v = [], [], [], [], []
    for bi in range(NB):
        cum = cums[bi]
        cum_last = cum[C - 1:C, :]
        k = k_ref[bi]
        qe.append((q_ref[bi] * (GLA_DK ** -0.5) * jnp.exp(cum)).astype(BF16))
        ke.append((k * jnp.exp(-cum)).astype(BF16))
        kd.append((k * jnp.exp(cum_last - cum)).astype(BF16))
        e_last.append(jnp.exp(cum_last))
        v.append(v_ref[bi].astype(BF16))
    streams = [(bi, h) for bi in range(NB) for h in range(GLA_HEADS)]
    ks = lambda h: slice(h * GLA_DK, (h + 1) * GLA_DK)
    vs = lambda h: slice(h * GLA_DV, (h + 1) * GLA_DV)
    sts = [s_ref[bi, h] for bi, h in streams]
    atts = [jnp.where(lower, _dot_nt(qe[bi][:, ks(h)], ke[bi][:, ks(h)]), 0.0) for bi, h in streams]
    inters = [_dot_nt(qe[bi][:, ks(h)], sts[i]) for i, (bi, h) in enumerate(streams)]
    os_ = [inters[i] + _dot(atts[i], v[bi][:, vs(h)]) for i, (bi, h) in enumerate(streams)]
    for i, (bi, h) in enumerate(streams):
        s_ref[bi, h] = sts[i] * e_last[bi][:, ks(h)] + _dot_tn(v[bi][:, vs(h)], kd[bi][:, ks(h)])
    for bi in range(NB):
        og = og_ref[bi]
        outs = []
        for h in range(GLA_HEADS):
            gate = og[:, vs(h)]
            outs.append(_rms(os_[bi * GLA_HEADS + h], onorm) * (gate * _sigmoid(gate)))
        o_ref[bi] = jnp.concatenate(outs, axis=1).astype(o_ref.dtype)


GLA_BATCH_ROWS = 8


def _gla(qk, v, og, gd, gate_up_pad, gate_b, onorm, B, S):
    C = GLA_CHUNK
    assert S % C == 0
    KW = GLA_HEADS * GLA_DK
    VW = GLA_HEADS * GLA_DV
    seq3 = lambda t: t.reshape(B, S, t.shape[-1])
    gd3 = gd.reshape(B, S, LANES)
    nb = GLA_BATCH_ROWS if B % GLA_BATCH_ROWS == 0 else 1
    out = pl.pallas_call(
        _gla_kernel,
        grid=(B // nb, S // C),
        in_specs=[pl.BlockSpec((nb, C, KW), lambda b, c: (b, c, 0)),
                  pl.BlockSpec((nb, C, KW), lambda b, c: (b, c, 1)),
                  pl.BlockSpec((nb, C, VW), lambda b, c: (b, c, 0)),
                  pl.BlockSpec((nb, C, VW), lambda b, c: (b, c, 0)),
                  pl.BlockSpec((nb, C, LANES), lambda b, c: (b, c, 0)),
                  pl.BlockSpec((LANES, KW), lambda b, c: (0, 0)),
                  pl.BlockSpec((1, KW), lambda b, c: (0, 0)),
                  pl.BlockSpec((1, GLA_DV), lambda b, c: (0, 0))],
        out_specs=pl.BlockSpec((nb, C, VW), lambda b, c: (b, c, 0)),
        out_shape=jax.ShapeDtypeStruct((B, S, VW), BF16),
        scratch_shapes=[pltpu.VMEM((nb, GLA_HEADS, GLA_DV, GLA_DK), F32)],
        compiler_params=_cparams(2),
        name="gla",
    )(seq3(qk), seq3(qk), seq3(v), seq3(og), gd3, gate_up_pad, gate_b.reshape(1, KW), onorm.reshape(1, GLA_DV))
    return out.reshape(B * S, VW)


def _xattn_kernel(*refs, n_in):
    x_ref = refs[0]
    a_refs = refs[1:1 + n_in]
    w_refs = refs[1 + n_in:1 + 2 * n_in]
    g_ref, wq_ref, mk_ref, mv_ref, wo_ref, gm_ref, wr_ref, br_ref, o_ref, lg_ref = refs[1 + 2 * n_in:]
    tq = x_ref.shape[0]
    subs = [slice(r, r + XA_SUB_ROWS) for r in range(0, tq, XA_SUB_ROWS)]
    xs = [x_ref[sub, :] for sub in subs]
    for a_ref, w_ref in zip(a_refs, w_refs):
        xs = [x + jnp.dot(a_ref[sub, :], w_ref[...], preferred_element_type=F32) for x, sub in zip(xs, subs)]
    qs = [_dot(_rms(x, g_ref[...]), wq_ref[...]).astype(BF16) for x in xs]
    mk = mk_ref[...]
    mv = mv_ref[...]
    sls = [slice(hd * XA_HEAD_DIM, (hd + 1) * XA_HEAD_DIM) for hd in range(XA_HEADS)]
    scores = [[_dot_nt(q[:, sl], mk[:, sl]) for sl in sls] for q in qs]
    probs = []
    for sc in scores:
        ps = []
        for s in sc:
            s = s * (XA_HEAD_DIM ** -0.5)
            p = jnp.exp(s - jnp.max(s, axis=-1, keepdims=True))
            ps.append((p / jnp.sum(p, axis=-1, keepdims=True)).astype(BF16))
        probs.append(ps)
    os_ = [jnp.concatenate([_dot(p, mv[:, sl]) for p, sl in zip(ps, sls)], axis=1) for ps in probs]
    outs = [x + _dot(o, wo_ref[...]) for x, o in zip(xs, os_)]
    for out, sub in zip(outs, subs):
        o_ref[sub, :] = out
    for out, sub in zip(outs, subs):
        lg_ref[:, sub] = _dot_nt(wr_ref[...], _rms(out, gm_ref[...])) + br_ref[...]


XA_SUB_ROWS = 256


def _mix_proj_xattn(x, acts, weights, g, wq, mk, mv, wo, g_moe, wt_router, bt_router, B, S, tq=1024):
    D = x.shape[-1]
    assert S % tq == 0 and tq % XA_SUB_ROWS == 0
    M = mk.shape[0] // B
    XW = mk.shape[-1]
    n_in = len(acts)
    seq3 = lambda a: a.reshape(B, S, a.shape[-1])
    row_spec = lambda a: pl.BlockSpec((None, tq, a.shape[-1]), lambda b, n: (b, n, 0))
    const = lambda a: pl.BlockSpec(a.shape, lambda b, n: (0,) * a.ndim)
    out, logits = pl.pallas_call(
        functools.partial(_xattn_kernel, n_in=n_in),
        grid=(B, S // tq),
        in_specs=[row_spec(x)] + [row_spec(a) for a in acts] + [const(w) for w in weights]
                 + [pl.BlockSpec((1, D), lambda b, n: (0, 0)),
                    pl.BlockSpec((D, XW), lambda b, n: (0, 0)),
                    pl.BlockSpec((None, M, XW), lambda b, n: (b, 0, 0)),
                    pl.BlockSpec((None, M, XW), lambda b, n: (b, 0, 0)),
                    pl.BlockSpec((XW, D), lambda b, n: (0, 0)),
                    pl.BlockSpec((1, D), lambda b, n: (0, 0)),
                    pl.BlockSpec((LANES, D), lambda b, n: (0, 0)),
                    pl.BlockSpec((LANES, XA_SUB_ROWS), lambda b, n: (0, 0))],
        out_specs=[pl.BlockSpec((None, tq, D), lambda b, n: (b, n, 0)),
                   pl.BlockSpec((None, LANES, tq), lambda b, n: (b * (S // tq) + n, 0, 0))],
        out_shape=[jax.ShapeDtypeStruct((B, S, D), F32),
                   jax.ShapeDtypeStruct((B * S // tq, LANES, tq), F32)],
        compiler_params=_cparams(2),
        name="xattn",
    )(seq3(x), *[seq3(a) for a in acts], *weights, g.reshape(1, D), wq,
      mk.reshape(B, M, XW), mv.reshape(B, M, XW), wo, g_moe.reshape(1, D), wt_router, bt_router)
    return out.reshape(B * S, D), logits


ROUTER_ROWS = 40


def _router_kernel(lg_ref, info_ref, slot_ref, cnt_ref, carry_ref):
    i = pl.program_id(0)

    @pl.when(i == 0)
    def _():
        carry_ref[...] = jnp.zeros_like(carry_ref)

    logits = lg_ref[:ROUTER_ROWS, :]
    tm = logits.shape[1]
    row = lax.broadcasted_iota(I32, logits.shape, 0)
    big = jnp.int32(LANES)
    neg = -jnp.inf
    gl = jnp.where(row < MOE_GROUPS, logits, neg)
    gmax = jnp.max(gl, axis=0, keepdims=True)
    g_top = jnp.min(jnp.where(gl == gmax, row, big), axis=0, keepdims=True)
    p_group = 1.0 / jnp.sum(jnp.exp(gl - gmax), axis=0, keepdims=True)
    lo = MOE_GROUPS + MOE_EXPERTS_PER_GROUP * g_top
    in_group = jnp.where(row >= lo, jnp.where(row < lo + MOE_EXPERTS_PER_GROUP, 1, 0), 0) > 0
    el = jnp.where(in_group, logits, neg)
    emax = jnp.max(el, axis=0, keepdims=True)
    ee = jnp.exp(el - emax)
    prob = ee / jnp.sum(ee, axis=0, keepdims=True)
    prob = jnp.where(in_group, prob, -1.0)
    p1 = jnp.max(prob, axis=0, keepdims=True)
    i1 = jnp.min(jnp.where(prob == p1, row, big), axis=0, keepdims=True)
    rest = jnp.where(row == i1, -1.0, prob)
    p2 = jnp.max(rest, axis=0, keepdims=True)
    i2 = jnp.min(jnp.where(rest == p2, row, big), axis=0, keepdims=True)
    tot = p1 + p2
    g1 = p_group * p1 / tot
    g2 = p_group * p2 / tot
    oh = jnp.concatenate([jnp.where(row == i1, 1.0, 0.0), jnp.where(row == i2, 1.0, 0.0)], axis=0)
    tr = lax.broadcasted_iota(I32, (tm, tm), 0)
    tc = lax.broadcasted_iota(I32, (tm, tm), 1)
    pre = _dot(oh, jnp.where(tr < tc, 1.0, 0.0))
    tots = _dot(oh, jnp.ones((tm, LANES), F32))
    reps = tm // LANES
    carry = carry_ref[...]
    base1 = jnp.concatenate([carry] * reps, axis=1)
    base2 = jnp.concatenate([carry + tots[:ROUTER_ROWS]] * reps, axis=1)
    r1 = jnp.sum(oh[:ROUTER_ROWS] * (base1 + pre[:ROUTER_ROWS]), axis=0, keepdims=True)
    r2 = jnp.sum(oh[ROUTER_ROWS:] * (base2 + pre[ROUTER_ROWS:]), axis=0, keepdims=True)
    carry = carry + tots[:ROUTER_ROWS] + tots[ROUTER_ROWS:]
    carry_ref[...] = carry
    cnt_ref[...] = carry
    e1 = (i1 - MOE_GROUPS).astype(F32)
    e2 = (i2 - MOE_GROUPS).astype(F32)
    slot_rows = [e1, e2, r1, r2, g1, g2]
    rows8 = lax.broadcasted_iota(I32, (SUBLANES, tm), 0)
    slot = jnp.zeros((SUBLANES, tm), F32)
    for j, val in enumerate(slot_rows):
        slot = jnp.where(rows8 == j, val, slot)
    slot_ref[...] = slot
    wide = jnp.concatenate([slot, jnp.zeros((LANES - SUBLANES, tm), F32)], axis=0)
    info_ref[...] = jnp.transpose(wide)


def _router(logits, tm=256):
    n_row_tiles, _, tq = logits.shape
    assert tq % tm == 0 and tm % LANES == 0
    per = tq // tm
    T = n_row_tiles * tq
    NT = T // tm
    return pl.pallas_call(
        _router_kernel,
        grid=(NT,),
        in_specs=[pl.BlockSpec((None, LANES, tm), lambda i: (i // per, 0, i % per))],
        out_specs=[pl.BlockSpec((tm, LANES), lambda i: (i, 0)),
                   pl.BlockSpec((None, SUBLANES, tm), lambda i: (i, 0, 0)),
                   pl.BlockSpec((ROUTER_ROWS, LANES), lambda i: (0, 0))],
        out_shape=[jax.ShapeDtypeStruct((T, LANES), F32),
                   jax.ShapeDtypeStruct((NT, SUBLANES, tm), F32),
                   jax.ShapeDtypeStruct((ROUTER_ROWS, LANES), F32)],
        scratch_shapes=[pltpu.VMEM((ROUTER_ROWS, LANES), F32)],
        compiler_params=_cparams(1),
        name="router",
    )(logits)


def _row_bytes_wait(hbm, buf, sem):
    pltpu.make_async_copy(buf, hbm.at[pl.ds(0, buf.shape[0]), :], sem).wait()


def _to_row_tiles(ref, val):
    n = val.shape[0]
    for c in range(SUBLANES):
        ref[pl.ds(c, n, stride=SUBLANES), :] = val[:, c * LANES:(c + 1) * LANES]


def _from_row_tiles(ref):
    n = ref.shape[0] // SUBLANES
    return jnp.concatenate([ref[pl.ds(c, n, stride=SUBLANES), :] for c in range(SUBLANES)], axis=1)


def _moe_dispatch_kernel(pends_ref, cnt_ref, dest_ref, x_ref, g_ref, hs_hbm, hbuf, zbuf, sems, zsem, *, td):
    i = pl.program_id(0)
    nt = pl.num_programs(0)
    slot = lax.rem(i, 2)

    @pl.when(i == 0)
    def _():
        zbuf[...] = jnp.zeros_like(zbuf)
        for e in range(MOE_EXPERTS):
            @pl.when(cnt_ref[e] > 0)
            def _():
                start = pl.multiple_of((pends_ref[e] - MOE_BLOCK) * SUBLANES, MOE_BLOCK)
                pltpu.make_async_copy(zbuf, hs_hbm.at[pl.ds(start, MOE_BLOCK * SUBLANES), :], zsem).start()
        for e in range(MOE_EXPERTS):
            @pl.when(cnt_ref[e] > 0)
            def _():
                pltpu.make_async_copy(zbuf, hs_hbm.at[pl.ds(0, MOE_BLOCK * SUBLANES), :], zsem).wait()

        first_unused = pends_ref[MOE_EXPERTS - 1] // MOE_BLOCK
        n_blocks = hs_hbm.shape[0] // (MOE_BLOCK * SUBLANES)

        def zero_start(blk, carry):
            start = pl.multiple_of(blk * (MOE_BLOCK * SUBLANES), MOE_BLOCK)
            pltpu.make_async_copy(zbuf, hs_hbm.at[pl.ds(start, MOE_BLOCK * SUBLANES), :], zsem).start()
            return carry

        def zero_wait(blk, carry):
            pltpu.make_async_copy(zbuf, hs_hbm.at[pl.ds(0, MOE_BLOCK * SUBLANES), :], zsem).wait()
            return carry

        lax.fori_loop(first_unused, n_blocks, zero_start, 0)
        lax.fori_loop(first_unused, n_blocks, zero_wait, 0)

    hb = hbuf.at[slot]
    _to_row_tiles(hb, _rms(x_ref[...], g_ref[...]))
    for j in range(td):
        for c in range(2):
            row = pl.multiple_of(dest_ref[0, c * td + j] * SUBLANES, SUBLANES)
            pltpu.make_async_copy(hb.at[pl.ds(j * SUBLANES, SUBLANES), :],
                                  hs_hbm.at[pl.ds(row, SUBLANES), :],
                                  sems.at[slot]).start(priority=c)

    @pl.when(i > 0)
    def _():
        other = hbuf.at[1 - slot]
        _row_bytes_wait(hs_hbm, other, sems.at[1 - slot])
        _row_bytes_wait(hs_hbm, other, sems.at[1 - slot])

    @pl.when(i == nt - 1)
    def _():
        _row_bytes_wait(hs_hbm, hb, sems.at[slot])
        _row_bytes_wait(hs_hbm, hb, sems.at[slot])


def _moe_dispatch(x, g, pends, counts, dest3, P, td):
    T, D = x.shape
    assert D == ROW_TILE and T % td == 0
    grid_spec = pltpu.PrefetchScalarGridSpec(
        num_scalar_prefetch=2,
        grid=(T // td,),
        in_specs=[pl.BlockSpec((None, 1, 2 * td), lambda i, pe, cn: (i, 0, 0), memory_space=pltpu.SMEM),
                  pl.BlockSpec((td, D), lambda i, pe, cn: (i, 0)),
                  pl.BlockSpec((1, D), lambda i, pe, cn: (0, 0))],
        out_specs=pl.BlockSpec(memory_space=pl.ANY),
        scratch_shapes=[pltpu.VMEM((2, td * SUBLANES, LANES), F32),
                        pltpu.VMEM((MOE_BLOCK * SUBLANES, LANES), F32),
                        pltpu.SemaphoreType.DMA((2,)),
                        pltpu.SemaphoreType.DMA(())],
    )
    return pl.pallas_call(
        functools.partial(_moe_dispatch_kernel, td=td),
        grid_spec=grid_spec,
        out_shape=jax.ShapeDtypeStruct((P * SUBLANES, LANES), F32),
        compiler_params=_cparams(1),
        name="moe_dispatch",
    )(pends, counts, dest3, x, g.reshape(1, D))


def _moe_expert_kernel(be_ref, nu_ref, hs_ref, w1_ref, w3_ref, w2_ref, o_ref, w1b, w3b, w2b):
    i = pl.program_id(0)
    used = i < nu_ref[0]
    changed = jnp.logical_or(i == 0, be_ref[i] != be_ref[jnp.maximum(i - 1, 0)])

    @pl.when(jnp.logical_and(used, changed))
    def _():
        w1b[...] = w1_ref[...].astype(BF16)
        w3b[...] = w3_ref[...].astype(BF16)
        w2b[...] = w2_ref[...].astype(BF16)

    @pl.when(used)
    def _():
        xe = _from_row_tiles(hs_ref).astype(BF16)
        ff = w1b.shape[1]
        halves = [slice(0, ff // 2), slice(ff // 2, ff)]
        ups = [(jnp.dot(xe, w1b[:, sl], preferred_element_type=F32),
                jnp.dot(xe, w3b[:, sl], preferred_element_type=F32)) for sl in halves]
        act = [(a * _sigmoid(a) * b).astype(BF16) for a, b in ups]
        y = sum(jnp.dot(a, w2b[sl, :], preferred_element_type=F32) for a, sl in zip(act, halves))
        _to_row_tiles(o_ref, y)

    @pl.when(jnp.logical_not(used))
    def _():
        o_ref[...] = jnp.zeros_like(o_ref)


def _moe_experts(hs, block_e, n_used, w1, w3, w2, layer):
    P = hs.shape[0] // SUBLANES
    D = ROW_TILE
    FF = w1.shape[-1]
    NB = P // MOE_BLOCK
    last = lambda i, nu: jnp.minimum(i, nu[0] - 1)
    grid_spec = pltpu.PrefetchScalarGridSpec(
        num_scalar_prefetch=2,
        grid=(NB,),
        in_specs=[pl.BlockSpec((MOE_BLOCK * SUBLANES, LANES), lambda i, be, nu: (last(i, nu), 0)),
                  pl.BlockSpec((None, None, D, FF), lambda i, be, nu: (layer, be[last(i, nu)], 0, 0)),
                  pl.BlockSpec((None, None, D, FF), lambda i, be, nu: (layer, be[last(i, nu)], 0, 0)),
                  pl.BlockSpec((None, None, FF, D), lambda i, be, nu: (layer, be[last(i, nu)], 0, 0))],
        out_specs=pl.BlockSpec((MOE_BLOCK * SUBLANES, LANES), lambda i, be, nu: (i, 0)),
        scratch_shapes=[pltpu.VMEM((D, FF), BF16),
                        pltpu.VMEM((D, FF), BF16),
                        pltpu.VMEM((FF, D), BF16)],
    )
    return pl.pallas_call(
        _moe_expert_kernel,
        grid_spec=grid_spec,
        out_shape=jax.ShapeDtypeStruct((P * SUBLANES, LANES), F32),
        compiler_params=_cparams(1),
        name="moe_experts",
    )(block_e, n_used, hs, w1, w3, w2)


def _gather_rows(src_hbm, idx_ref, dst_ref, sem, first, last):
    for r in range(first, last):
        row = pl.multiple_of(idx_ref[0, r] * SUBLANES, SUBLANES)
        pltpu.make_async_copy(src_hbm.at[pl.ds(row, SUBLANES), :],
                              dst_ref.at[pl.ds(r * SUBLANES, SUBLANES), :], sem).start(priority=r % 2)


def _moe_combine_kernel(pos_ref, posn_ref, x_ref, info_ref, yb_hbm, g_ref, *rest, tc, final_norm, splits):
    if splits:
        w_ref, o_ref = rest[0], rest[1]
        p_refs = rest[2:2 + len(splits)]
        ybuf, sems = rest[2 + len(splits):]
    else:
        o_ref, ybuf, sems = rest
    i = pl.program_id(0)
    nb = pl.num_programs(0)
    slot = lax.rem(i, 2)

    @pl.when(i == 0)
    def _():
        def issue(r, carry):
            src = pl.multiple_of(pos_ref[0, r] * SUBLANES, SUBLANES)
            dst = pl.multiple_of(r * SUBLANES, SUBLANES)
            pltpu.make_async_copy(yb_hbm.at[pl.ds(src, SUBLANES), :],
                                  ybuf.at[0, pl.ds(dst, SUBLANES), :], sems.at[0]).start()
            return carry
        lax.fori_loop(0, 2 * tc, issue, 0)

    def wait_tile(s_):
        pltpu.make_async_copy(yb_hbm.at[pl.ds(0, 2 * tc * SUBLANES), :], ybuf.at[s_], sems.at[s_]).wait()

    if not splits:
        @pl.when(i + 1 < nb)
        def _():
            _gather_rows(yb_hbm, posn_ref, ybuf.at[1 - slot], sems.at[1 - slot], 0, 2 * tc)

    wait_tile(slot)
    info = info_ref[...]
    yb = ybuf.at[slot]
    y0 = _from_row_tiles(yb.at[pl.ds(0, tc * SUBLANES), :])
    y1 = _from_row_tiles(yb.at[pl.ds(tc * SUBLANES, tc * SUBLANES), :])
    out = x_ref[...] + (y0 * info[:, 4:5] + y1 * info[:, 5:6])
    if final_norm:
        out = _rms(out, g_ref[...])
    o_ref[...] = out
    if splits:
        h = _rms(out, g_ref[...]).astype(BF16)
        chunks = []
        for p_ref, n in zip(p_refs, splits):
            for c0 in range(0, n, PROJ_CHUNK):
                chunks.append((p_ref, c0, min(PROJ_CHUNK, n - c0)))
        per = -(-2 * tc // max(len(chunks) // 2, 1))
        off = 0
        for ci, (p_ref, c0, width) in enumerate(chunks):
            val = jnp.dot(h, w_ref[:, off:off + width], preferred_element_type=F32)
            _gather_rows(yb_hbm, posn_ref, ybuf.at[1 - slot], sems.at[1 - slot],
                         min(ci * per, 2 * tc), min((ci + 1) * per, 2 * tc))
            p_ref[:, c0:c0 + width] = val.astype(p_ref.dtype)
            off += width

        @pl.when(i == nb - 1)
        def _():
            wait_tile(1 - slot)


def _moe_combine(x, info, dest3, yb, g, final_norm, tc, next_proj=None):
    T, D = x.shape
    assert D == ROW_TILE and T % tc == 0
    NT = T // tc
    in_specs = [pl.BlockSpec((None, 1, 2 * tc), lambda i: (i, 0, 0), memory_space=pltpu.SMEM),
                pl.BlockSpec((None, 1, 2 * tc), lambda i: (jnp.minimum(i + 1, NT - 1), 0, 0),
                             memory_space=pltpu.SMEM),
                pl.BlockSpec((tc, D), lambda i: (i, 0)),
                pl.BlockSpec((tc, LANES), lambda i: (i, 0)),
                pl.BlockSpec(memory_space=pl.ANY),
                pl.BlockSpec((1, D), lambda i: (0, 0))]
    out_specs = [pl.BlockSpec((tc, D), lambda i: (i, 0))]
    out_shape = [jax.ShapeDtypeStruct((T, D), F32)]
    args = [dest3, dest3, x, info, yb, g.reshape(1, D)]
    splits = ()
    if next_proj is not None:
        w, splits, out_dtypes = next_proj
        assert not final_norm and sum(splits) == w.shape[1]
        in_specs.append(pl.BlockSpec(w.shape, lambda i: (0, 0)))
        args.append(w)
        out_specs += [pl.BlockSpec((tc, n), lambda i: (i, 0)) for n in splits]
        out_shape += [jax.ShapeDtypeStruct((T, n), dt) for n, dt in zip(splits, out_dtypes)]
    outs = pl.pallas_call(
        functools.partial(_moe_combine_kernel, tc=tc, final_norm=final_norm, splits=tuple(splits)),
        grid=(NT,),
        in_specs=in_specs,
        out_specs=out_specs,
        out_shape=out_shape,
        scratch_shapes=[pltpu.VMEM((2, 2 * tc * SUBLANES, LANES), F32), pltpu.SemaphoreType.DMA((2,))],
        compiler_params=_cparams(1),
        name="moe_combine",
    )(*args)
    return outs[0], tuple(outs[1:])


MOE_TILE = 512
MOE_COMBINE_TILE = 256
MOE_LAST_COMBINE_TILE = 256
PROJ_CHUNK = 256
ROUTER_TILE = 256


def _tile_slots(dest, tile):
    n_tiles = dest.shape[0] * dest.shape[2] // tile
    return jnp.concatenate([dest[:, 0, :].reshape(n_tiles, 1, tile), dest[:, 1, :].reshape(n_tiles, 1, tile)], axis=2)


def _router_params(w_group, b_group, w_expert, b_expert, lanes_out):
    D = w_group.shape[0]
    n_log = MOE_GROUPS + MOE_EXPERTS
    wt = jnp.zeros((LANES, D), F32).at[:MOE_GROUPS].set(w_group.T).at[MOE_GROUPS:n_log].set(w_expert.T)
    bt = jnp.zeros((LANES,), F32).at[:MOE_GROUPS].set(b_group).at[MOE_GROUPS:n_log].set(b_expert)
    return wt.astype(BF16), jnp.broadcast_to(bt[:, None], (LANES, lanes_out))


def _moe_layer(x, g, logits, w1, w3, w2, layer, g_out, final_norm, next_proj):
    T, D = x.shape
    n_log = MOE_GROUPS + MOE_EXPERTS
    info, slot, cnt = _router(logits, ROUTER_TILE)
    P = 2 * T + MOE_EXPERTS * MOE_BLOCK
    NB = P // MOE_BLOCK
    counts = cnt[MOE_GROUPS:n_log, 0].astype(I32)
    padded = (counts + MOE_BLOCK - 1) // MOE_BLOCK * MOE_BLOCK
    pends = jnp.cumsum(padded).astype(I32)
    pstarts = pends - padded
    block_start = jnp.arange(NB, dtype=I32) * MOE_BLOCK
    block_e = jnp.minimum(jnp.sum((pends[None, :] <= block_start[:, None]).astype(I32), axis=1),
                          MOE_EXPERTS - 1).astype(I32)
    n_used = (pends[-1:] // MOE_BLOCK).astype(I32)
    eid = slot[:, 0:2, :].astype(I32)
    expert_ids = jnp.arange(MOE_EXPERTS, dtype=I32)
    seg_start = jnp.sum(jnp.where(eid[..., None] == expert_ids, pstarts, 0), axis=-1)
    dest = seg_start + slot[:, 2:4, :].astype(I32)
    hs = _moe_dispatch(x, g, pends, counts, _tile_slots(dest, MOE_TILE), P, MOE_TILE)
    yb = _moe_experts(hs, block_e, n_used, w1, w3, w2, layer)
    tc = MOE_LAST_COMBINE_TILE if next_proj is None else MOE_COMBINE_TILE
    return _moe_combine(x, info, _tile_slots(dest, tc), yb, g_out, final_norm, tc, next_proj)


def kernel(x, mem, norm_mix, norm_xattn, norm_moe, norm_final, ev_w_in, ev_sinks, ev_mu, ev_w0, ev_w2, ev_a0, ev_a2, ev_g2, ev_k_k, ev_k_a, ev_r_k, ev_lnx_w, ev_lnx_b, ev_w_out, od_w_in, od_gate_up, od_gate_b, od_onorm, od_w_out, mem_norm, mem_wk, mem_wv, xa_wq, xa_wo, moe_w_group, moe_b_group, moe_w_expert, moe_b_expert, moe_w1, moe_w3, moe_w2):
    B, S, D = x.shape
    M = mem.shape[1]
    T = B * S
    depth = norm_mix.shape[0]
    xf = x.reshape(T, D)

    XW = XA_HEADS * XA_HEAD_DIM
    w_kv = jnp.concatenate([mem_wk, mem_wv], axis=1).astype(BF16)
    mk, mv = _norm_matmul(mem.reshape(B * M, D), mem_norm, w_kv, (XW, XW), (BF16, BF16))

    KW = GLA_HEADS * GLA_DK
    VW = GLA_HEADS * GLA_DV
    swa_cols = SWA_Q_HEADS * HEAD_DIM + 2 * (SWA_Q_HEADS // SWA_GROUP) * HEAD_DIM

    def in_proj(layer):
        i = layer // 2
        if layer % 2 == 0:
            return ev_w_in[i].astype(BF16), (swa_cols, ev_w_in.shape[-1] - swa_cols), (BF16, F32)
        R = od_gate_up.shape[1]
        w = od_w_in[i]
        w_re = jnp.concatenate([w[:, :2 * KW + VW], w[:, 2 * KW + VW + R:],
                                w[:, 2 * KW + VW:2 * KW + VW + R],
                                jnp.zeros((D, LANES - R), F32)], axis=1).astype(BF16)
        return w_re, (2 * KW, VW, VW, LANES), (F32, BF16, F32, F32)

    w_first, _, _ = in_proj(0)
    qkv0, rw0 = _in_proj_rwkv(xf, norm_mix[0], w_first, swa_cols, ev_mu[0], ev_w0[0], ev_w2[0], ev_a0[0], ev_a2[0],
                              ev_g2[0], ev_k_k[0], ev_k_a[0], S)
    proj = None
    for layer in range(depth):
        i = layer // 2
        if layer % 2 == 0:
            if layer == 0:
                qkv, rw = qkv0, [t.reshape(B, S, RWKV_WIDTH) for t in rw0]
            else:
                qkv, p_rw = proj
                rw = _rwkv_prep(p_rw, ev_mu[i], ev_w0[i], ev_w2[i], ev_a0[i], ev_a2[i],
                                ev_g2[i], ev_k_k[i], ev_k_a[i], B, S)
            o_a = _swa(qkv, ev_sinks[i], B, S)
            o_b = _rwkv_scan(*rw, ev_r_k[i].reshape(-1), ev_lnx_w[i], ev_lnx_b[i])
            w_out = ev_w_out[i].astype(BF16)
            qw = o_a.shape[-1]
            mix_acts, mix_ws = [o_a, o_b], [w_out[:qw], w_out[qw:]]
        else:
            qk, v_gla, og, gd = proj
            R = od_gate_up.shape[1]
            gup = jnp.zeros((LANES, KW), F32).at[:R].set(od_gate_up[i]).astype(BF16)
            o = _gla(qk, v_gla, og, gd, gup, od_gate_b[i], od_onorm[i], B, S)
            mix_acts, mix_ws = [o], [od_w_out[i].astype(BF16)]
        wt_router, bt_router = _router_params(moe_w_group[layer], moe_b_group[layer], moe_w_expert[layer],
                                              moe_b_expert[layer], XA_SUB_ROWS)
        xf, logits = _mix_proj_xattn(xf, mix_acts, mix_ws, norm_xattn[layer], xa_wq[layer].astype(BF16), mk, mv,
                                     xa_wo[layer].astype(BF16), norm_moe[layer], wt_router, bt_router, B, S)
        last = layer == depth - 1
        g_out = norm_final if last else norm_mix[layer + 1]
        xf, proj = _moe_layer(xf, norm_moe[layer], logits, moe_w1, moe_w3, moe_w2, layer,
                              g_out, last, None if last else in_proj(layer + 1))
    return xf.reshape(B, S, D)
```

```python
import functools
import math

import jax
import jax.numpy as jnp
from jax import lax
from jax.experimental import pallas as pl
from jax.experimental.pallas import tpu as pltpu

F32 = jnp.float32
BF16 = jnp.bfloat16
I32 = jnp.int32

EPS = 1e-6
HEAD_DIM = 64
SWA_WINDOW = 128
SWA_Q_HEADS = 8
SWA_GROUP = 4
RWKV_HEADS = 8
RWKV_WIDTH = 512
RWKV_LN_EPS = 64e-5
RWKV_CHUNK = 64
RWKV_DECAY_SCALE = math.exp(-0.5)
GLA_HEADS = 4
GLA_DK = 128
GLA_DV = 256
GLA_CHUNK = 64
GLA_GATE_NORM = 16.0
XA_HEADS = 4
XA_HEAD_DIM = 128
MOE_GROUPS = 4
MOE_EXPERTS_PER_GROUP = 8
MOE_EXPERTS = 32
MOE_BLOCK = 512
LANES = 128
SUBLANES = 8
ROW_TILE = SUBLANES * LANES

VMEM_LIMIT_BYTES = 48 * 1024 * 1024


def _cparams(n_axes):
    return pltpu.CompilerParams(dimension_semantics=("arbitrary",) * n_axes,
                                vmem_limit_bytes=VMEM_LIMIT_BYTES)


def _dot(a, b):
    return jnp.dot(a.astype(BF16), b.astype(BF16), preferred_element_type=F32)


def _dot_nt(a, b):
    return lax.dot_general(a.astype(BF16), b.astype(BF16), (((1,), (1,)), ((), ())),
                           preferred_element_type=F32)


def _dot_tn(a, b):
    return lax.dot_general(a.astype(BF16), b.astype(BF16), (((0,), (0,)), ((), ())),
                           preferred_element_type=F32)


def _tri_cumsum(tri, x):
    hi = x.astype(BF16)
    rest = x - hi.astype(F32)
    mid = rest.astype(BF16)
    lo = (rest - mid.astype(F32)).astype(BF16)
    w = x.shape[1]
    sums = jnp.dot(tri.astype(BF16), jnp.concatenate([hi, mid, lo], axis=1), preferred_element_type=F32)
    return sums[:, :w] + sums[:, w:2 * w] + sums[:, 2 * w:]


def _rms(x, g):
    ms = jnp.mean(x * x, axis=-1, keepdims=True)
    return x * lax.rsqrt(ms + EPS) * g


def _sigmoid(x):
    return 1.0 / (1.0 + jnp.exp(-x))


def _softplus(x):
    return jnp.maximum(x, 0.0) + jnp.log(1.0 + jnp.exp(-jnp.abs(x)))


def _norm_matmul_kernel(x_ref, g_ref, w_ref, *o_refs, splits):
    h = _rms(x_ref[...], g_ref[...]).astype(BF16)
    off = 0
    for o_ref, n in zip(o_refs, splits):
        o_ref[...] = jnp.dot(h, w_ref[:, off:off + n], preferred_element_type=F32).astype(o_ref.dtype)
        off += n


def _norm_matmul(x, g, w, splits, out_dtypes, tm=256):
    T, D = x.shape
    N = w.shape[1]
    assert sum(splits) == N and T % tm == 0
    return pl.pallas_call(
        functools.partial(_norm_matmul_kernel, splits=tuple(splits)),
        grid=(T // tm,),
        in_specs=[pl.BlockSpec((tm, D), lambda i: (i, 0)),
                  pl.BlockSpec((1, D), lambda i: (0, 0)),
                  pl.BlockSpec((D, N), lambda i: (0, 0))],
        out_specs=[pl.BlockSpec((tm, n), lambda i: (i, 0)) for n in splits],
        out_shape=[jax.ShapeDtypeStruct((T, n), dt) for n, dt in zip(splits, out_dtypes)],
        compiler_params=_cparams(1),
        name="norm_matmul",
    )(x, g.reshape(1, D), w)


def _swa_kernel(sinks_ref, q_ref, kp_ref, kc_ref, vp_ref, vc_ref, o_ref):
    n = pl.program_id(1)
    W = SWA_WINDOW
    NB = q_ref.shape[0]
    qpos = lax.broadcasted_iota(I32, (W, 2 * W), 0) + W
    kpos = lax.broadcasted_iota(I32, (W, 2 * W), 1)
    rel = qpos - kpos
    in_window = jnp.where(rel >= 0, jnp.where(rel < W, 1, 0), 0)
    has_prev = jnp.where(n > 0, 1, 0)
    valid = (in_window * jnp.where(kpos >= W, 1, has_prev)) > 0
    n_groups = SWA_Q_HEADS // SWA_GROUP
    streams = [(bi, g) for bi in range(NB) for g in range(n_groups)]
    qb = [q_ref[bi].astype(BF16) for bi in range(NB)]
    kb = [jnp.concatenate([kp_ref[bi], kc_ref[bi]], axis=0).astype(BF16) for bi in range(NB)]
    vb = [jnp.concatenate([vp_ref[bi], vc_ref[bi]], axis=0).astype(BF16) for bi in range(NB)]
    gs = lambda g: slice(g * HEAD_DIM, (g + 1) * HEAD_DIM)
    scores = []
    for bi, g in streams:
        qg = jnp.concatenate([qb[bi][:, h * HEAD_DIM:(h + 1) * HEAD_DIM]
                              for h in range(g * SWA_GROUP, (g + 1) * SWA_GROUP)], axis=0)
        scores.append(_dot_nt(qg, kb[bi][:, gs(g)]))
    probs = []
    for i, (bi, g) in enumerate(streams):
        pieces = []
        for j in range(SWA_GROUP):
            s = jnp.where(valid, scores[i][j * W:(j + 1) * W] * (HEAD_DIM ** -0.5), -jnp.inf)
            sink = sinks_ref[g * SWA_GROUP + j]
            m = jnp.maximum(jnp.max(s, axis=-1, keepdims=True), sink)
            p = jnp.exp(s - m)
            den = jnp.sum(p, axis=-1, keepdims=True) + jnp.exp(sink - m)
            pieces.append((p / den).astype(BF16))
        probs.append(jnp.concatenate(pieces, axis=0))
    ogs = [_dot(probs[i], vb[bi][:, gs(g)]) for i, (bi, g) in enumerate(streams)]
    for bi in range(NB):
        outs = []
        for g in range(n_groups):
            og = ogs[bi * n_groups + g]
            outs += [og[j * W:(j + 1) * W] for j in range(SWA_GROUP)]
        o_ref[bi] = jnp.concatenate(outs, axis=1).astype(o_ref.dtype)


SWA_BATCH_ROWS = 2


def _swa(qkv, sinks, B, S):
    W = SWA_WINDOW
    assert S % W == 0
    qkv3 = qkv.reshape(B, S, qkv.shape[-1])
    qw = SWA_Q_HEADS * HEAD_DIM
    kw = qw // SWA_GROUP
    kcol = qw // kw
    nb = SWA_BATCH_ROWS if B % SWA_BATCH_ROWS == 0 else 1
    out = pl.pallas_call(
        _swa_kernel,
        grid=(B // nb, S // W),
        in_specs=[pl.BlockSpec(memory_space=pltpu.SMEM),
                  pl.BlockSpec((nb, W, qw), lambda b, n: (b, n, 0)),
                  pl.BlockSpec((nb, W, kw), lambda b, n: (b, jnp.maximum(n - 1, 0), kcol)),
                  pl.BlockSpec((nb, W, kw), lambda b, n: (b, n, kcol)),
                  pl.BlockSpec((nb, W, kw), lambda b, n: (b, jnp.maximum(n - 1, 0), kcol + 1)),
                  pl.BlockSpec((nb, W, kw), lambda b, n: (b, n, kcol + 1))],
        out_specs=pl.BlockSpec((nb, W, qw), lambda b, n: (b, n, 0)),
        out_shape=jax.ShapeDtypeStruct((B, S, qw), BF16),
        compiler_params=_cparams(2),
        name="swa",
    )(sinks, qkv3, qkv3, qkv3, qkv3, qkv3)
    return out.reshape(B * S, qw)


def _rwkv_prep_math(p, last, mu_ref, w0_ref, w2_ref, a0_ref, a2_ref, g2_ref, kk_ref, ka_ref, outs):
    r_out, lw_out, k_out, v_out, a_out, b_out, g_out = outs
    C = RWKV_WIDTH
    row = lax.broadcasted_iota(I32, p.shape, 0)
    p_prev = jnp.where(row == 0, last, pltpu.roll(p, 1, axis=0))
    p = p + (p_prev - p) * mu_ref[...]
    r = p[:, :C]
    k = p[:, C:2 * C]
    v = p[:, 2 * C:3 * C]
    xw = p[:, 3 * C:3 * C + 64]
    xa = p[:, 3 * C + 64:3 * C + 128]
    xg = p[:, 3 * C + 128:]
    z = w0_ref[...] + _dot(jnp.tanh(xw), w2_ref[...])
    lw = -RWKV_DECAY_SCALE * _sigmoid(z)
    a = _sigmoid(a0_ref[...] + _dot(xa, a2_ref[...]))
    g = _dot(_sigmoid(xg), g2_ref[...])
    kk = k * kk_ref[...]
    pieces = []
    for h in range(RWKV_HEADS):
        kh = kk[:, h * HEAD_DIM:(h + 1) * HEAD_DIM]
        ss = jnp.sum(kh * kh, axis=-1, keepdims=True)
        pieces.append(kh * lax.rsqrt(jnp.maximum(ss, 1e-24)))
    kk = jnp.concatenate(pieces, axis=1)
    r_out[...] = r
    lw_out[...] = lw
    k_out[...] = k * (1.0 + (a - 1.0) * ka_ref[...])
    v_out[...] = v
    a_out[...] = -kk
    b_out[...] = kk * a
    g_out[...] = g


def _rwkv_prep_kernel(p_ref, pprev_ref, *refs):
    n = pl.program_id(1)
    last = jnp.where(n > 0, pprev_ref[SUBLANES - 1:SUBLANES, :], 0.0)
    _rwkv_prep_math(p_ref[...], last, *refs[:8], refs[8:])


def _rwkv_params(mu, w0, w2, a0, a2, g2, k_k, k_a):
    row = lambda t: t.reshape(1, -1)
    return [row(mu), row(w0), w2.astype(BF16), row(a0), a2.astype(BF16), g2.astype(BF16), row(k_k), row(k_a)]


def _rwkv_prep(p, mu, w0, w2, a0, a2, g2, k_k, k_a, B, S, tt=256):
    assert S % tt == 0
    C = RWKV_WIDTH
    PW = p.shape[-1]
    p3 = p.reshape(B, S, PW)
    full = lambda arr: pl.BlockSpec(arr.shape, lambda b, n: (0,) * arr.ndim)
    params = _rwkv_params(mu, w0, w2, a0, a2, g2, k_k, k_a)
    outs = pl.pallas_call(
        _rwkv_prep_kernel,
        grid=(B, S // tt),
        in_specs=[pl.BlockSpec((None, tt, PW), lambda b, n: (b, n, 0)),
                  pl.BlockSpec((None, SUBLANES, PW),
                               lambda b, n: (b, jnp.maximum(n * (tt // SUBLANES) - 1, 0), 0))]
                 + [full(t) for t in params],
        out_specs=[pl.BlockSpec((None, tt, C), lambda b, n: (b, n, 0))] * 7,
        out_shape=[jax.ShapeDtypeStruct((B, S, C), F32)] * 7,
        compiler_params=_cparams(2),
        name="rwkv_prep",
    )(p3, p3, *params)
    return outs


def _in_proj_rwkv_kernel(x_ref, g_ref, w_ref, *refs, swa_cols, tiles_per_seq):
    params = refs[:8]
    qkv_out = refs[8]
    outs = refs[9:16]
    p_buf, last_buf = refs[16:]
    i = pl.program_id(0)

    @pl.when(i == 0)
    def _():
        p_buf[...] = jnp.zeros_like(p_buf)
        last_buf[...] = jnp.zeros_like(last_buf)

    j = i - 1
    p_prev_tile = p_buf[lax.rem(i + 1, 2)]
    tm = p_prev_tile.shape[0]
    last = jnp.where(lax.rem(j, tiles_per_seq) == 0, 0.0, last_buf[...])
    _rwkv_prep_math(p_prev_tile, last, *params, outs)
    last_buf[...] = p_prev_tile[tm - 1:tm, :]
    h = _rms(x_ref[...], g_ref[...]).astype(BF16)
    qkv_out[...] = jnp.dot(h, w_ref[:, :swa_cols], preferred_element_type=F32).astype(qkv_out.dtype)
    p_buf[lax.rem(i, 2)] = jnp.dot(h, w_ref[:, swa_cols:], preferred_element_type=F32)


def _in_proj_rwkv(x, g, w, swa_cols, mu, w0, w2, a0, a2, g2, k_k, k_a, S, tm=256):
    T, D = x.shape
    assert S % tm == 0 and T % S == 0
    N = w.shape[1]
    C = RWKV_WIDTH
    NT = T // tm
    params = _rwkv_params(mu, w0, w2, a0, a2, g2, k_k, k_a)
    const = lambda arr: pl.BlockSpec(arr.shape, lambda i: (0,) * arr.ndim)
    cur = lambda i: (jnp.minimum(i, NT - 1), 0)
    prev = lambda i: (jnp.maximum(i - 1, 0), 0)
    outs = pl.pallas_call(
        functools.partial(_in_proj_rwkv_kernel, swa_cols=swa_cols, tiles_per_seq=S // tm),
        grid=(NT + 1,),
        in_specs=[pl.BlockSpec((tm, D), cur), pl.BlockSpec((1, D), lambda i: (0, 0)), const(w)]
                 + [const(t) for t in params],
        out_specs=[pl.BlockSpec((tm, swa_cols), cur)] + [pl.BlockSpec((tm, C), prev)] * 7,
        out_shape=[jax.ShapeDtypeStruct((T, swa_cols), BF16)] + [jax.ShapeDtypeStruct((T, C), F32)] * 7,
        scratch_shapes=[pltpu.VMEM((2, tm, N - swa_cols), F32), pltpu.VMEM((1, N - swa_cols), F32)],
        compiler_params=_cparams(1),
        name="in_proj_rwkv",
    )(x, g.reshape(1, D), w, *params)
    return outs[0], outs[1:]


def _pair_blockdiag(x):
    lane = lax.broadcasted_iota(I32, x.shape, 1)
    zero = jnp.zeros_like(x)
    return jnp.concatenate([jnp.where(lane < HEAD_DIM, x, zero), jnp.where(lane >= HEAD_DIM, x, zero)], axis=0)


def _rwkv_scan_kernel(r_ref, lw_ref, k_ref, v_ref, a_ref, b_ref, g_ref, rk_ref, lnw_ref, lnb_ref,
                      o_ref, s_ref):
    c = pl.program_id(1)

    @pl.when(c == 0)
    def _():
        s_ref[...] = jnp.zeros_like(s_ref)

    C = RWKV_CHUNK
    NB = r_ref.shape[0]
    NP = RWKV_HEADS // 2
    PW = 2 * HEAD_DIM
    row = lax.broadcasted_iota(I32, (C, C), 0)
    col = lax.broadcasted_iota(I32, (C, C), 1)
    tri = jnp.where(row >= col, 1.0, 0.0).astype(F32)
    rowp = lax.broadcasted_iota(I32, (C, PW), 0)
    colp = lax.broadcasted_iota(I32, (C, PW), 1)
    colp = jnp.where(colp >= HEAD_DIM, colp - HEAD_DIM, colp)
    lower_p = rowp >= colp
    strict_p = rowp > colp
    rows = lax.broadcasted_iota(I32, (PW, PW), 0)
    cols = lax.broadcasted_iota(I32, (PW, PW), 1)
    same_head = jnp.where(rows >= HEAD_DIM, 1, 0) == jnp.where(cols >= HEAD_DIM, 1, 0)
    first = lax.broadcasted_iota(I32, (C, PW), 1) < HEAD_DIM

    streams = [(bi, p) for bi in range(NB) for p in range(NP)]
    pre = []
    for bi in range(NB):
        lw = lw_ref[bi]
        cum = _tri_cumsum(tri, lw)
        cum_last = cum[C - 1:C, :]
        r = r_ref[bi]
        k = k_ref[bi]
        v = v_ref[bi]
        a = a_ref[bi]
        b = b_ref[bi]
        e_neg = jnp.exp(-cum)
        e_rem = jnp.exp(cum_last - cum)
        pre.append(dict(
            r_t=(r * jnp.exp(cum)).astype(BF16), a_t=(a * jnp.exp(cum - lw)).astype(BF16),
            b_t=(b * e_neg).astype(BF16), k_t=(k * e_neg).astype(BF16),
            b_d=(b * e_rem).astype(BF16), k_d=(k * e_rem).astype(BF16),
            v_b=v.astype(BF16), v=v, e_last=jnp.exp(cum_last), rkk=r * k * rk_ref[...], g=g_ref[bi]))

    def lanes(p):
        return slice(p * PW, (p + 1) * PW)

    ar = [jnp.concatenate([pre[bi]['a_t'][:, lanes(p)], pre[bi]['r_t'][:, lanes(p)]], axis=0) for bi, p in streams]
    s0 = [s_ref[bi, p] for bi, p in streams]
    big = [_dot_nt(ar[i], jnp.concatenate([_pair_blockdiag(pre[bi]['b_t'][:, lanes(p)]),
                                           _pair_blockdiag(pre[bi]['k_t'][:, lanes(p)]),
                                           s0[i].astype(BF16)], axis=0))
           for i, (bi, p) in enumerate(streams)]
    m_b = [t[:, :PW] for t in big]
    m_k = [t[:, PW:2 * PW] for t in big]
    ars = [t[:, 2 * PW:] for t in big]
    v_p = [pre[bi]['v_b'][:, lanes(p)] for bi, p in streams]
    v_bd = [_pair_blockdiag(vp) for vp in v_p]
    x = [ars[i][:C] + _dot(jnp.where(strict_p, m_k[i][:C], 0.0), v_bd[i]) for i in range(len(streams))]
    pw = [jnp.where(strict_p, m_b[i][:C], 0.0).astype(BF16) for i in range(len(streams))]
    n_stages = 6
    for stage in range(n_stages):
        if stage < n_stages - 1:
            prod = [_dot(pw[i], jnp.concatenate([_pair_blockdiag(x[i].astype(BF16)), _pair_blockdiag(pw[i])], axis=1))
                    for i in range(len(streams))]
            x = [x[i] + prod[i][:, :PW] for i in range(len(streams))]
            pw = [prod[i][:, PW:].astype(BF16) for i in range(len(streams))]
        else:
            x = [x[i] + _dot(pw[i], _pair_blockdiag(x[i].astype(BF16))) for i in range(len(streams))]
    u_b = [xi.astype(BF16) for xi in x]
    y = [ars[i][C:]
         + _dot(jnp.concatenate([jnp.where(lower_p, m_b[i][C:], 0.0), jnp.where(lower_p, m_k[i][C:], 0.0)], axis=1),
                jnp.concatenate([_pair_blockdiag(u_b[i]), v_bd[i]], axis=0))
         for i in range(len(streams))]
    for i, (bi, p) in enumerate(streams):
        upd = _dot_tn(jnp.concatenate([u_b[i], v_p[i]], axis=0),
                      jnp.concatenate([pre[bi]['b_d'][:, lanes(p)], pre[bi]['k_d'][:, lanes(p)]], axis=0))
        s_ref[bi, p] = s0[i] * pre[bi]['e_last'][:, lanes(p)] + jnp.where(same_head, upd, 0.0)

    lnw = lnw_ref[...]
    lnb = lnb_ref[...]

    def head_sum(t):
        s1 = jnp.sum(jnp.where(first, t, 0.0), axis=-1, keepdims=True)
        s2 = jnp.sum(jnp.where(first, 0.0, t), axis=-1, keepdims=True)
        return jnp.where(first, s1, s2)

    for bi in range(NB):
        outs = []
        for p in range(NP):
            yi = y[bi * NP + p]
            mean = head_sum(yi) * (1.0 / HEAD_DIM)
            yc = yi - mean
            var = head_sum(yc * yc) * (1.0 / HEAD_DIM)
            yn = yc * lax.rsqrt(var + RWKV_LN_EPS) * lnw[:, lanes(p)] + lnb[:, lanes(p)]
            bonus = head_sum(pre[bi]['rkk'][:, lanes(p)]) * pre[bi]['v'][:, lanes(p)]
            outs.append((yn + bonus) * pre[bi]['g'][:, lanes(p)])
        o_ref[bi] = jnp.concatenate(outs, axis=1).astype(o_ref.dtype)


RWKV_BATCH_ROWS = 8


def _rwkv_scan(r, lw, k, v, a, b, g, r_k, lnx_w, lnx_b):
    B, S, W = r.shape
    C = RWKV_CHUNK
    assert S % C == 0 and W == RWKV_HEADS * HEAD_DIM
    nb = RWKV_BATCH_ROWS if B % RWKV_BATCH_ROWS == 0 else 1
    seq = pl.BlockSpec((nb, C, W), lambda bb, c: (bb, c, 0))
    par = pl.BlockSpec((1, W), lambda bb, c: (0, 0))
    out = pl.pallas_call(
        _rwkv_scan_kernel,
        grid=(B // nb, S // C),
        in_specs=[seq] * 7 + [par] * 3,
        out_specs=seq,
        out_shape=jax.ShapeDtypeStruct((B, S, W), BF16),
        scratch_shapes=[pltpu.VMEM((nb, RWKV_HEADS // 2, 2 * HEAD_DIM, 2 * HEAD_DIM), F32)],
        compiler_params=_cparams(2),
        name="rwkv_scan",
    )(r, lw, k, v, a, b, g, r_k.reshape(1, W), lnx_w.reshape(1, W), lnx_b.reshape(1, W))
    return out.reshape(B * S, W)


def _gla_kernel(q_ref, k_ref, v_ref, og_ref, gd_ref, gup_ref, gb_ref, on_ref, o_ref, s_ref):
    c = pl.program_id(1)

    @pl.when(c == 0)
    def _():
        s_ref[...] = jnp.zeros_like(s_ref)

    C = GLA_CHUNK
    NB = q_ref.shape[0]
    row = lax.broadcasted_iota(I32, (C, C), 0)
    col = lax.broadcasted_iota(I32, (C, C), 1)
    lower = row >= col
    tri = jnp.where(lower, 1.0, 0.0).astype(F32)
    onorm = on_ref[...]
    zs = [_dot(gd_ref[bi], gup_ref[...]) + gb_ref[...] for bi in range(NB)]
    cums = [_tri_cumsum(tri, -_softplus(-z) / GLA_GATE_NORM) for z in zs]
    qe, ke, kd, e_last, v = [], [], [], [], []
    for bi in range(NB):
        cum = cums[bi]
        cum_last = cum[C - 1:C, :]
        k = k_ref[bi]
        qe.append((q_ref[bi] * (GLA_DK ** -0.5) * jnp.exp(cum)).astype(BF16))
        ke.append((k * jnp.exp(-cum)).astype(BF16))
        kd.append((k * jnp.exp(cum_last - cum)).astype(BF16))
        e_last.append(jnp.exp(cum_last))
        v.append(v_ref[bi].astype(BF16))
    streams = [(bi, h) for bi in range(NB) for h in range(GLA_HEADS)]
    ks = lambda h: slice(h * GLA_DK, (h + 1) * GLA_DK)
    vs = lambda h: slice(h * GLA_DV, (h + 1) * GLA_DV)
    sts = [s_ref[bi, h] for bi, h in streams]
    atts = [jnp.where(lower, _dot_nt(qe[bi][:, ks(h)], ke[bi][:, ks(h)]), 0.0) for bi, h in streams]
    inters = [_dot_nt(qe[bi][:, ks(h)], sts[i]) for i, (bi, h) in enumerate(streams)]
    os_ = [inters[i] + _dot(atts[i], v[bi][:, vs(h)]) for i, (bi, h) in enumerate(streams)]
    for i, (bi, h) in enumerate(streams):
        s_ref[bi, h] = sts[i] * e_last[bi][:, ks(h)] + _dot_tn(v[bi][:, vs(h)], kd[bi][:, ks(h)])
    for bi in range(NB):
        og = og_ref[bi]
        outs = []
        for h in range(GLA_HEADS):
            gate = og[:, vs(h)]
            outs.append(_rms(os_[bi * GLA_HEADS + h], onorm) * (gate * _sigmoid(gate)))
        o_ref[bi] = jnp.concatenate(outs, axis=1).astype(o_ref.dtype)


GLA_BATCH_ROWS = 8


def _gla(qk, v, og, gd, gate_up_pad, gate_b, onorm, B, S):
    C = GLA_CHUNK
    assert S % C == 0
    KW = GLA_HEADS * GLA_DK
    VW = GLA_HEADS * GLA_DV
    seq3 = lambda t: t.reshape(B, S, t.shape[-1])
    gd3 = gd.reshape(B, S, LANES)
    nb = GLA_BATCH_ROWS if B % GLA_BATCH_ROWS == 0 else 1
    out = pl.pallas_call(
        _gla_kernel,
        grid=(B // nb, S // C),
        in_specs=[pl.BlockSpec((nb, C, KW), lambda b, c: (b, c, 0)),
                  pl.BlockSpec((nb, C, KW), lambda b, c: (b, c, 1)),
                  pl.BlockSpec((nb, C, VW), lambda b, c: (b, c, 0)),
                  pl.BlockSpec((nb, C, VW), lambda b, c: (b, c, 0)),
                  pl.BlockSpec((nb, C, LANES), lambda b, c: (b, c, 0)),
                  pl.BlockSpec((LANES, KW), lambda b, c: (0, 0)),
                  pl.BlockSpec((1, KW), lambda b, c: (0, 0)),
                  pl.BlockSpec((1, GLA_DV), lambda b, c: (0, 0))],
        out_specs=pl.BlockSpec((nb, C, VW), lambda b, c: (b, c, 0)),
        out_shape=jax.ShapeDtypeStruct((B, S, VW), BF16),
        scratch_shapes=[pltpu.VMEM((nb, GLA_HEADS, GLA_DV, GLA_DK), F32)],
        compiler_params=_cparams(2),
        name="gla",
    )(seq3(qk), seq3(qk), seq3(v), seq3(og), gd3, gate_up_pad, gate_b.reshape(1, KW), onorm.reshape(1, GLA_DV))
    return out.reshape(B * S, VW)


def _xattn_kernel(*refs, n_in):
    x_ref = refs[0]
    a_refs = refs[1:1 + n_in]
    w_refs = refs[1 + n_in:1 + 2 * n_in]
    g_ref, wq_ref, mk_ref, mv_ref, wo_ref, gm_ref, wr_ref, br_ref, o_ref, lg_ref = refs[1 + 2 * n_in:]
    tq = x_ref.shape[0]
    subs = [slice(r, r + XA_SUB_ROWS) for r in range(0, tq, XA_SUB_ROWS)]
    xs = [x_ref[sub, :] for sub in subs]
    for a_ref, w_ref in zip(a_refs, w_refs):
        xs = [x + jnp.dot(a_ref[sub, :], w_ref[...], preferred_element_type=F32) for x, sub in zip(xs, subs)]
    qs = [_dot(_rms(x, g_ref[...]), wq_ref[...]).astype(BF16) for x in xs]
    mk = mk_ref[...]
    mv = mv_ref[...]
    sls = [slice(hd * XA_HEAD_DIM, (hd + 1) * XA_HEAD_DIM) for hd in range(XA_HEADS)]
    scores = [[_dot_nt(q[:, sl], mk[:, sl]) for sl in sls] for q in qs]
    probs = []
    for sc in scores:
        ps = []
        for s in sc:
            s = s * (XA_HEAD_DIM ** -0.5)
            p = jnp.exp(s - jnp.max(s, axis=-1, keepdims=True))
            ps.append((p / jnp.sum(p, axis=-1, keepdims=True)).astype(BF16))
        probs.append(ps)
    os_ = [jnp.concatenate([_dot(p, mv[:, sl]) for p, sl in zip(ps, sls)], axis=1) for ps in probs]
    outs = [x + _dot(o, wo_ref[...]) for x, o in zip(xs, os_)]
    for out, sub in zip(outs, subs):
        o_ref[sub, :] = out
    for out, sub in zip(outs, subs):
        lg_ref[:, sub] = _dot_nt(wr_ref[...], _rms(out, gm_ref[...])) + br_ref[...]


XA_SUB_ROWS = 256


def _mix_proj_xattn(x, acts, weights, g, wq, mk, mv, wo, g_moe, wt_router, bt_router, B, S, tq=1024):
    D = x.shape[-1]
    assert S % tq == 0 and tq % XA_SUB_ROWS == 0
    M = mk.shape[0] // B
    XW = mk.shape[-1]
    n_in = len(acts)
    seq3 = lambda a: a.reshape(B, S, a.shape[-1])
    row_spec = lambda a: pl.BlockSpec((None, tq, a.shape[-1]), lambda b, n: (b, n, 0))
    const = lambda a: pl.BlockSpec(a.shape, lambda b, n: (0,) * a.ndim)
    out, logits = pl.pallas_call(
        functools.partial(_xattn_kernel, n_in=n_in),
        grid=(B, S // tq),
        in_specs=[row_spec(x)] + [row_spec(a) for a in acts] + [const(w) for w in weights]
                 + [pl.BlockSpec((1, D), lambda b, n: (0, 0)),
                    pl.BlockSpec((D, XW), lambda b, n: (0, 0)),
                    pl.BlockSpec((None, M, XW), lambda b, n: (b, 0, 0)),
                    pl.BlockSpec((None, M, XW), lambda b, n: (b, 0, 0)),
                    pl.BlockSpec((XW, D), lambda b, n: (0, 0)),
                    pl.BlockSpec((1, D), lambda b, n: (0, 0)),
                    pl.BlockSpec((LANES, D), lambda b, n: (0, 0)),
                    pl.BlockSpec((LANES, XA_SUB_ROWS), lambda b, n: (0, 0))],
        out_specs=[pl.BlockSpec((None, tq, D), lambda b, n: (b, n, 0)),
                   pl.BlockSpec((None, LANES, tq), lambda b, n: (b * (S // tq) + n, 0, 0))],
        out_shape=[jax.ShapeDtypeStruct((B, S, D), F32),
                   jax.ShapeDtypeStruct((B * S // tq, LANES, tq), F32)],
        compiler_params=_cparams(2),
        name="xattn",
    )(seq3(x), *[seq3(a) for a in acts], *weights, g.reshape(1, D), wq,
      mk.reshape(B, M, XW), mv.reshape(B, M, XW), wo, g_moe.reshape(1, D), wt_router, bt_router)
    return out.reshape(B * S, D), logits


ROUTER_ROWS = 40


def _router_kernel(lg_ref, info_ref, slot_ref, cnt_ref, carry_ref):
    i = pl.program_id(0)

    @pl.when(i == 0)
    def _():
        carry_ref[...] = jnp.zeros_like(carry_ref)

    logits = lg_ref[:ROUTER_ROWS, :]
    tm = logits.shape[1]
    row = lax.broadcasted_iota(I32, logits.shape, 0)
    big = jnp.int32(LANES)
    neg = -jnp.inf
    gl = jnp.where(row < MOE_GROUPS, logits, neg)
    gmax = jnp.max(gl, axis=0, keepdims=True)
    g_top = jnp.min(jnp.where(gl == gmax, row, big), axis=0, keepdims=True)
    p_group = 1.0 / jnp.sum(jnp.exp(gl - gmax), axis=0, keepdims=True)
    lo = MOE_GROUPS + MOE_EXPERTS_PER_GROUP * g_top
    in_group = jnp.where(row >= lo, jnp.where(row < lo + MOE_EXPERTS_PER_GROUP, 1, 0), 0) > 0
    el = jnp.where(in_group, logits, neg)
    emax = jnp.max(el, axis=0, keepdims=True)
    ee = jnp.exp(el - emax)
    prob = ee / jnp.sum(ee, axis=0, keepdims=True)
    prob = jnp.where(in_group, prob, -1.0)
    p1 = jnp.max(prob, axis=0, keepdims=True)
    i1 = jnp.min(jnp.where(prob == p1, row, big), axis=0, keepdims=True)
    rest = jnp.where(row == i1, -1.0, prob)
    p2 = jnp.max(rest, axis=0, keepdims=True)
    i2 = jnp.min(jnp.where(rest == p2, row, big), axis=0, keepdims=True)
    tot = p1 + p2
    g1 = p_group * p1 / tot
    g2 = p_group * p2 / tot
    oh = jnp.concatenate([jnp.where(row == i1, 1.0, 0.0), jnp.where(row == i2, 1.0, 0.0)], axis=0)
    tr = lax.broadcasted_iota(I32, (tm, tm), 0)
    tc = lax.broadcasted_iota(I32, (tm, tm), 1)
    pre = _dot(oh, jnp.where(tr < tc, 1.0, 0.0))
    tots = _dot(oh, jnp.ones((tm, LANES), F32))
    reps = tm // LANES
    carry = carry_ref[...]
    base1 = jnp.concatenate([carry] * reps, axis=1)
    base2 = jnp.concatenate([carry + tots[:ROUTER_ROWS]] * reps, axis=1)
    r1 = jnp.sum(oh[:ROUTER_ROWS] * (base1 + pre[:ROUTER_ROWS]), axis=0, keepdims=True)
    r2 = jnp.sum(oh[ROUTER_ROWS:] * (base2 + pre[ROUTER_ROWS:]), axis=0, keepdims=True)
    carry = carry + tots[:ROUTER_ROWS] + tots[ROUTER_ROWS:]
    carry_ref[...] = carry
    cnt_ref[...] = carry
    e1 = (i1 - MOE_GROUPS).astype(F32)
    e2 = (i2 - MOE_GROUPS).astype(F32)
    slot_rows = [e1, e2, r1, r2, g1, g2]
    rows8 = lax.broadcasted_iota(I32, (SUBLANES, tm), 0)
    slot = jnp.zeros((SUBLANES, tm), F32)
    for j, val in enumerate(slot_rows):
        slot = jnp.where(rows8 == j, val, slot)
    slot_ref[...] = slot
    wide = jnp.concatenate([slot, jnp.zeros((LANES - SUBLANES, tm), F32)], axis=0)
    info_ref[...] = jnp.transpose(wide)


def _router(logits, tm=256):
    n_row_tiles, _, tq = logits.shape
    assert tq % tm == 0 and tm % LANES == 0
    per = tq // tm
    T = n_row_tiles * tq
    NT = T // tm
    return pl.pallas_call(
        _router_kernel,
        grid=(NT,),
        in_specs=[pl.BlockSpec((None, LANES, tm), lambda i: (i // per, 0, i % per))],
        out_specs=[pl.BlockSpec((tm, LANES), lambda i: (i, 0)),
                   pl.BlockSpec((None, SUBLANES, tm), lambda i: (i, 0, 0)),
                   pl.BlockSpec((ROUTER_ROWS, LANES), lambda i: (0, 0))],
        out_shape=[jax.ShapeDtypeStruct((T, LANES), F32),
                   jax.ShapeDtypeStruct((NT, SUBLANES, tm), F32),
                   jax.ShapeDtypeStruct((ROUTER_ROWS, LANES), F32)],
        scratch_shapes=[pltpu.VMEM((ROUTER_ROWS, LANES), F32)],
        compiler_params=_cparams(1),
        name="router",
    )(logits)


def _row_bytes_wait(hbm, buf, sem):
    pltpu.make_async_copy(buf, hbm.at[pl.ds(0, buf.shape[0]), :], sem).wait()


def _to_row_tiles(ref, val):
    n = val.shape[0]
    for c in range(SUBLANES):
        ref[pl.ds(c, n, stride=SUBLANES), :] = val[:, c * LANES:(c + 1) * LANES]


def _from_row_tiles(ref):
    n = ref.shape[0] // SUBLANES
    return jnp.concatenate([ref[pl.ds(c, n, stride=SUBLANES), :] for c in range(SUBLANES)], axis=1)


def _moe_dispatch_kernel(pends_ref, cnt_ref, dest_ref, x_ref, g_ref, hs_hbm, hbuf, zbuf, sems, zsem, *, td):
    i = pl.program_id(0)
    nt = pl.num_programs(0)
    slot = lax.rem(i, 2)

    @pl.when(i == 0)
    def _():
        zbuf[...] = jnp.zeros_like(zbuf)
        for e in range(MOE_EXPERTS):
            @pl.when(cnt_ref[e] > 0)
            def _():
                start = pl.multiple_of((pends_ref[e] - MOE_BLOCK) * SUBLANES, MOE_BLOCK)
                pltpu.make_async_copy(zbuf, hs_hbm.at[pl.ds(start, MOE_BLOCK * SUBLANES), :], zsem).start()
        for e in range(MOE_EXPERTS):
            @pl.when(cnt_ref[e] > 0)
            def _():
                pltpu.make_async_copy(zbuf, hs_hbm.at[pl.ds(0, MOE_BLOCK * SUBLANES), :], zsem).wait()

        first_unused = pends_ref[MOE_EXPERTS - 1] // MOE_BLOCK
        n_blocks = hs_hbm.shape[0] // (MOE_BLOCK * SUBLANES)

        def zero_start(blk, carry):
            start = pl.multiple_of(blk * (MOE_BLOCK * SUBLANES), MOE_BLOCK)
            pltpu.make_async_copy(zbuf, hs_hbm.at[pl.ds(start, MOE_BLOCK * SUBLANES), :], zsem).start()
            return carry

        def zero_wait(blk, carry):
            pltpu.make_async_copy(zbuf, hs_hbm.at[pl.ds(0, MOE_BLOCK * SUBLANES), :], zsem).wait()
            return carry

        lax.fori_loop(first_unused, n_blocks, zero_start, 0)
        lax.fori_loop(first_unused, n_blocks, zero_wait, 0)

    hb = hbuf.at[slot]
    _to_row_tiles(hb, _rms(x_ref[...], g_ref[...]))
    for j in range(td):
        for c in range(2):
            row = pl.multiple_of(dest_ref[0, c * td + j] * SUBLANES, SUBLANES)
            pltpu.make_async_copy(hb.at[pl.ds(j * SUBLANES, SUBLANES), :],
                                  hs_hbm.at[pl.ds(row, SUBLANES), :],
                                  sems.at[slot]).start(priority=c)

    @pl.when(i > 0)
    def _():
        other = hbuf.at[1 - slot]
        _row_bytes_wait(hs_hbm, other, sems.at[1 - slot])
        _row_bytes_wait(hs_hbm, other, sems.at[1 - slot])

    @pl.when(i == nt - 1)
    def _():
        _row_bytes_wait(hs_hbm, hb, sems.at[slot])
        _row_bytes_wait(hs_hbm, hb, sems.at[slot])


def _moe_dispatch(x, g, pends, counts, dest3, P, td):
    T, D = x.shape
    assert D == ROW_TILE and T % td == 0
    grid_spec = pltpu.PrefetchScalarGridSpec(
        num_scalar_prefetch=2,
        grid=(T // td,),
        in_specs=[pl.BlockSpec((None, 1, 2 * td), lambda i, pe, cn: (i, 0, 0), memory_space=pltpu.SMEM),
                  pl.BlockSpec((td, D), lambda i, pe, cn: (i, 0)),
                  pl.BlockSpec((1, D), lambda i, pe, cn: (0, 0))],
        out_specs=pl.BlockSpec(memory_space=pl.ANY),
        scratch_shapes=[pltpu.VMEM((2, td * SUBLANES, LANES), F32),
                        pltpu.VMEM((MOE_BLOCK * SUBLANES, LANES), F32),
                        pltpu.SemaphoreType.DMA((2,)),
                        pltpu.SemaphoreType.DMA(())],
    )
    return pl.pallas_call(
        functools.partial(_moe_dispatch_kernel, td=td),
        grid_spec=grid_spec,
        out_shape=jax.ShapeDtypeStruct((P * SUBLANES, LANES), F32),
        compiler_params=_cparams(1),
        name="moe_dispatch",
    )(pends, counts, dest3, x, g.reshape(1, D))


def _moe_expert_kernel(be_ref, nu_ref, hs_ref, w1_ref, w3_ref, w2_ref, o_ref, w1b, w3b, w2b):
    i = pl.program_id(0)
    used = i < nu_ref[0]
    changed = jnp.logical_or(i == 0, be_ref[i] != be_ref[jnp.maximum(i - 1, 0)])

    @pl.when(jnp.logical_and(used, changed))
    def _():
        w1b[...] = w1_ref[...].astype(BF16)
        w3b[...] = w3_ref[...].astype(BF16)
        w2b[...] = w2_ref[...].astype(BF16)

    @pl.when(used)
    def _():
        xe = _from_row_tiles(hs_ref).astype(BF16)
        ff = w1b.shape[1]
        halves = [slice(0, ff // 2), slice(ff // 2, ff)]
        ups = [(jnp.dot(xe, w1b[:, sl], preferred_element_type=F32),
                jnp.dot(xe, w3b[:, sl], preferred_element_type=F32)) for sl in halves]
        act = [(a * _sigmoid(a) * b).astype(BF16) for a, b in ups]
        y = sum(jnp.dot(a, w2b[sl, :], preferred_element_type=F32) for a, sl in zip(act, halves))
        _to_row_tiles(o_ref, y)

    @pl.when(jnp.logical_not(used))
    def _():
        o_ref[...] = jnp.zeros_like(o_ref)


def _moe_experts(hs, block_e, n_used, w1, w3, w2, layer):
    P = hs.shape[0] // SUBLANES
    D = ROW_TILE
    FF = w1.shape[-1]
    NB = P // MOE_BLOCK
    last = lambda i, nu: jnp.minimum(i, nu[0] - 1)
    grid_spec = pltpu.PrefetchScalarGridSpec(
        num_scalar_prefetch=2,
        grid=(NB,),
        in_specs=[pl.BlockSpec((MOE_BLOCK * SUBLANES, LANES), lambda i, be, nu: (last(i, nu), 0)),
                  pl.BlockSpec((None, None, D, FF), lambda i, be, nu: (layer, be[last(i, nu)], 0, 0)),
                  pl.BlockSpec((None, None, D, FF), lambda i, be, nu: (layer, be[last(i, nu)], 0, 0)),
                  pl.BlockSpec((None, None, FF, D), lambda i, be, nu: (layer, be[last(i, nu)], 0, 0))],
        out_specs=pl.BlockSpec((MOE_BLOCK * SUBLANES, LANES), lambda i, be, nu: (i, 0)),
        scratch_shapes=[pltpu.VMEM((D, FF), BF16),
                        pltpu.VMEM((D, FF), BF16),
                        pltpu.VMEM((FF, D), BF16)],
    )
    return pl.pallas_call(
        _moe_expert_kernel,
        grid_spec=grid_spec,
        out_shape=jax.ShapeDtypeStruct((P * SUBLANES, LANES), F32),
        compiler_params=_cparams(1),
        name="moe_experts",
    )(block_e, n_used, hs, w1, w3, w2)


def _gather_rows(src_hbm, idx_ref, dst_ref, sem, first, last):
    for r in range(first, last):
        row = pl.multiple_of(idx_ref[0, r] * SUBLANES, SUBLANES)
        pltpu.make_async_copy(src_hbm.at[pl.ds(row, SUBLANES), :],
                              dst_ref.at[pl.ds(r * SUBLANES, SUBLANES), :], sem).start(priority=r % 2)


def _moe_combine_kernel(pos_ref, posn_ref, x_ref, info_ref, yb_hbm, g_ref, *rest, tc, final_norm, splits):
    if splits:
        w_ref, o_ref = rest[0], rest[1]
        p_refs = rest[2:2 + len(splits)]
        ybuf, sems = rest[2 + len(splits):]
    else:
        o_ref, ybuf, sems = rest
    i = pl.program_id(0)
    nb = pl.num_programs(0)
    slot = lax.rem(i, 2)

    @pl.when(i == 0)
    def _():
        def issue(r, carry):
            src = pl.multiple_of(pos_ref[0, r] * SUBLANES, SUBLANES)
            dst = pl.multiple_of(r * SUBLANES, SUBLANES)
            pltpu.make_async_copy(yb_hbm.at[pl.ds(src, SUBLANES), :],
                                  ybuf.at[0, pl.ds(dst, SUBLANES), :], sems.at[0]).start()
            return carry
        lax.fori_loop(0, 2 * tc, issue, 0)

    def wait_tile(s_):
        pltpu.make_async_copy(yb_hbm.at[pl.ds(0, 2 * tc * SUBLANES), :], ybuf.at[s_], sems.at[s_]).wait()

    if not splits:
        @pl.when(i + 1 < nb)
        def _():
            _gather_rows(yb_hbm, posn_ref, ybuf.at[1 - slot], sems.at[1 - slot], 0, 2 * tc)

    wait_tile(slot)
    info = info_ref[...]
    yb = ybuf.at[slot]
    y0 = _from_row_tiles(yb.at[pl.ds(0, tc * SUBLANES), :])
    y1 = _from_row_tiles(yb.at[pl.ds(tc * SUBLANES, tc * SUBLANES), :])
    out = x_ref[...] + (y0 * info[:, 4:5] + y1 * info[:, 5:6])
    if final_norm:
        out = _rms(out, g_ref[...])
    o_ref[...] = out
    if splits:
        h = _rms(out, g_ref[...]).astype(BF16)
        chunks = []
        for p_ref, n in zip(p_refs, splits):
            for c0 in range(0, n, PROJ_CHUNK):
                chunks.append((p_ref, c0, min(PROJ_CHUNK, n - c0)))
        per = -(-2 * tc // max(len(chunks) // 2, 1))
        off = 0
        for ci, (p_ref, c0, width) in enumerate(chunks):
            val = jnp.dot(h, w_ref[:, off:off + width], preferred_element_type=F32)
            _gather_rows(yb_hbm, posn_ref, ybuf.at[1 - slot], sems.at[1 - slot],
                         min(ci * per, 2 * tc), min((ci + 1) * per, 2 * tc))
            p_ref[:, c0:c0 + width] = val.astype(p_ref.dtype)
            off += width

        @pl.when(i == nb - 1)
        def _():
            wait_tile(1 - slot)


def _moe_combine(x, info, dest3, yb, g, final_norm, tc, next_proj=None):
    T, D = x.shape
    assert D == ROW_TILE and T % tc == 0
    NT = T // tc
    in_specs = [pl.BlockSpec((None, 1, 2 * tc), lambda i: (i, 0, 0), memory_space=pltpu.SMEM),
                pl.BlockSpec((None, 1, 2 * tc), lambda i: (jnp.minimum(i + 1, NT - 1), 0, 0),
                             memory_space=pltpu.SMEM),
                pl.BlockSpec((tc, D), lambda i: (i, 0)),
                pl.BlockSpec((tc, LANES), lambda i: (i, 0)),
                pl.BlockSpec(memory_space=pl.ANY),
                pl.BlockSpec((1, D), lambda i: (0, 0))]
    out_specs = [pl.BlockSpec((tc, D), lambda i: (i, 0))]
    out_shape = [jax.ShapeDtypeStruct((T, D), F32)]
    args = [dest3, dest3, x, info, yb, g.reshape(1, D)]
    splits = ()
    if next_proj is not None:
        w, splits, out_dtypes = next_proj
        assert not final_norm and sum(splits) == w.shape[1]
        in_specs.append(pl.BlockSpec(w.shape, lambda i: (0, 0)))
        args.append(w)
        out_specs += [pl.BlockSpec((tc, n), lambda i: (i, 0)) for n in splits]
        out_shape += [jax.ShapeDtypeStruct((T, n), dt) for n, dt in zip(splits, out_dtypes)]
    outs = pl.pallas_call(
        functools.partial(_moe_combine_kernel, tc=tc, final_norm=final_norm, splits=tuple(splits)),
        grid=(NT,),
        in_specs=in_specs,
        out_specs=out_specs,
        out_shape=out_shape,
        scratch_shapes=[pltpu.VMEM((2, 2 * tc * SUBLANES, LANES), F32), pltpu.SemaphoreType.DMA((2,))],
        compiler_params=_cparams(1),
        name="moe_combine",
    )(*args)
    return outs[0], tuple(outs[1:])


MOE_TILE = 512
MOE_COMBINE_TILE = 256
MOE_LAST_COMBINE_TILE = 256
PROJ_CHUNK = 256
ROUTER_TILE = 512


def _tile_slots(dest, tile):
    n_tiles = dest.shape[0] * dest.shape[2] // tile
    return jnp.concatenate([dest[:, 0, :].reshape(n_tiles, 1, tile), dest[:, 1, :].reshape(n_tiles, 1, tile)], axis=2)


def _router_params(w_group, b_group, w_expert, b_expert, lanes_out):
    D = w_group.shape[0]
    n_log = MOE_GROUPS + MOE_EXPERTS
    wt = jnp.zeros((LANES, D), F32).at[:MOE_GROUPS].set(w_group.T).at[MOE_GROUPS:n_log].set(w_expert.T)
    bt = jnp.zeros((LANES,), F32).at[:MOE_GROUPS].set(b_group).at[MOE_GROUPS:n_log].set(b_expert)
    return wt.astype(BF16), jnp.broadcast_to(bt[:, None], (LANES, lanes_out))


def _moe_layer(x, g, logits, w1, w3, w2, layer, g_out, final_norm, next_proj):
    T, D = x.shape
    n_log = MOE_GROUPS + MOE_EXPERTS
    info, slot, cnt = _router(logits, ROUTER_TILE)
    P = 2 * T + MOE_EXPERTS * MOE_BLOCK
    NB = P // MOE_BLOCK
    counts = cnt[MOE_GROUPS:n_log, 0].astype(I32)
    padded = (counts + MOE_BLOCK - 1) // MOE_BLOCK * MOE_BLOCK
    pends = jnp.cumsum(padded).astype(I32)
    pstarts = pends - padded
    block_start = jnp.arange(NB, dtype=I32) * MOE_BLOCK
    block_e = jnp.minimum(jnp.sum((pends[None, :] <= block_start[:, None]).astype(I32), axis=1),
                          MOE_EXPERTS - 1).astype(I32)
    n_used = (pends[-1:] // MOE_BLOCK).astype(I32)
    eid = slot[:, 0:2, :].astype(I32)
    expert_ids = jnp.arange(MOE_EXPERTS, dtype=I32)
    seg_start = jnp.sum(jnp.where(eid[..., None] == expert_ids, pstarts, 0), axis=-1)
    dest = seg_start + slot[:, 2:4, :].astype(I32)
    hs = _moe_dispatch(x, g, pends, counts, _tile_slots(dest, MOE_TILE), P, MOE_TILE)
    yb = _moe_experts(hs, block_e, n_used, w1, w3, w2, layer)
    tc = MOE_LAST_COMBINE_TILE if next_proj is None else MOE_COMBINE_TILE
    return _moe_combine(x, info, _tile_slots(dest, tc), yb, g_out, final_norm, tc, next_proj)


def kernel(x, mem, norm_mix, norm_xattn, norm_moe, norm_final, ev_w_in, ev_sinks, ev_mu, ev_w0, ev_w2, ev_a0, ev_a2, ev_g2, ev_k_k, ev_k_a, ev_r_k, ev_lnx_w, ev_lnx_b, ev_w_out, od_w_in, od_gate_up, od_gate_b, od_onorm, od_w_out, mem_norm, mem_wk, mem_wv, xa_wq, xa_wo, moe_w_group, moe_b_group, moe_w_expert, moe_b_expert, moe_w1, moe_w3, moe_w2):
    B, S, D = x.shape
    M = mem.shape[1]
    T = B * S
    depth = norm_mix.shape[0]
    xf = x.reshape(T, D)

    XW = XA_HEADS * XA_HEAD_DIM
    w_kv = jnp.concatenate([mem_wk, mem_wv], axis=1).astype(BF16)
    mk, mv = _norm_matmul(mem.reshape(B * M, D), mem_norm, w_kv, (XW, XW), (BF16, BF16))

    KW = GLA_HEADS * GLA_DK
    VW = GLA_HEADS * GLA_DV
    swa_cols = SWA_Q_HEADS * HEAD_DIM + 2 * (SWA_Q_HEADS // SWA_GROUP) * HEAD_DIM

    def in_proj(layer):
        i = layer // 2
        if layer % 2 == 0:
            return ev_w_in[i].astype(BF16), (swa_cols, ev_w_in.shape[-1] - swa_cols), (BF16, F32)
        R = od_gate_up.shape[1]
        w = od_w_in[i]
        w_re = jnp.concatenate([w[:, :2 * KW + VW], w[:, 2 * KW + VW + R:],
                                w[:, 2 * KW + VW:2 * KW + VW + R],
                                jnp.zeros((D, LANES - R), F32)], axis=1).astype(BF16)
        return w_re, (2 * KW, VW, VW, LANES), (F32, BF16, F32, F32)

    w_first, _, _ = in_proj(0)
    qkv0, rw0 = _in_proj_rwkv(xf, norm_mix[0], w_first, swa_cols, ev_mu[0], ev_w0[0], ev_w2[0], ev_a0[0], ev_a2[0],
                              ev_g2[0], ev_k_k[0], ev_k_a[0], S)
    proj = None
    for layer in range(depth):
        i = layer // 2
        if layer % 2 == 0:
            if layer == 0:
                qkv, rw = qkv0, [t.reshape(B, S, RWKV_WIDTH) for t in rw0]
            else:
                qkv, p_rw = proj
                rw = _rwkv_prep(p_rw, ev_mu[i], ev_w0[i], ev_w2[i], ev_a0[i], ev_a2[i],
                                ev_g2[i], ev_k_k[i], ev_k_a[i], B, S)
            o_a = _swa(qkv, ev_sinks[i], B, S)
            o_b = _rwkv_scan(*rw, ev_r_k[i].reshape(-1), ev_lnx_w[i], ev_lnx_b[i])
            w_out = ev_w_out[i].astype(BF16)
            qw = o_a.shape[-1]
            mix_acts, mix_ws = [o_a, o_b], [w_out[:qw], w_out[qw:]]
        else:
            qk, v_gla, og, gd = proj
            R = od_gate_up.shape[1]
            gup = jnp.zeros((LANES, KW), F32).at[:R].set(od_gate_up[i]).astype(BF16)
            o = _gla(qk, v_gla, og, gd, gup, od_gate_b[i], od_onorm[i], B, S)
            mix_acts, mix_ws = [o], [od_w_out[i].astype(BF16)]
        wt_router, bt_router = _router_params(moe_w_group[layer], moe_b_group[layer], moe_w_expert[layer],
                                              moe_b_expert[layer], XA_SUB_ROWS)
        xf, logits = _mix_proj_xattn(xf, mix_acts, mix_ws, norm_xattn[layer], xa_wq[layer].astype(BF16), mk, mv,
                                     xa_wo[layer].astype(BF16), norm_moe[layer], wt_router, bt_router, B, S)
        last = layer == depth - 1
        g_out = norm_final if last else norm_mix[layer + 1]
        xf, proj = _moe_layer(xf, norm_moe[layer], logits, moe_w1, moe_w3, moe_w2, layer,
                              g_out, last, None if last else in_proj(layer + 1))
    return xf.reshape(B, S, D)
```
